```python
import jax, jax.numpy as jnp
from jax import lax
import numpy as np

D_MODEL = 1024
BATCH = 8
SEQ = 2048
DEPTH = 2
DEC_BATCH = 128
DEC_SEQ = 1
PAST_LEN = 16384
PAGE_SIZE = 128

N_MIXERS = 2
N_GMLP_LAYERS = (DEPTH + 1) // 2
N_GLA_LAYERS = DEPTH // 2
GM_CHUNK = 128
GM_HALF = 2 * D_MODEL
GM_GROUPS = 8
GM_GROUP_DIM = GM_HALF // GM_GROUPS
GLA_HEADS = 4
GLA_DK_TOT = D_MODEL // 2
GLA_DV_TOT = D_MODEL
GLA_DK = GLA_DK_TOT // GLA_HEADS
GLA_DV = GLA_DV_TOT // GLA_HEADS
GLA_GATE_RANK = 16
GLA_GATE_NORMALIZER = 16.0
GLA_CHUNK = 64
N_EXPERTS = 64
TOP_K = 8
N_EXPERT_GROUPS = 8
TOPK_GROUPS = 4
EXPERT_DIM = D_MODEL // 4
SHARED_DIM = D_MODEL // 4
ROUTED_SCALE = 2.5
MOE_BLOCK = 512
NORM_EPS = 1e-6
LN_EPS = 1e-5

kernel_name = 'hybrid_gmlp_gla_moe_decoder_step'

F32 = jnp.float32


def rms_norm(x, g):
    xf = x.astype(F32)
    y = xf * lax.rsqrt(jnp.mean(xf * xf, axis=-1, keepdims=True) + NORM_EPS)
    return (y * g.astype(F32)).astype(x.dtype)


def layer_norm(x, g, b):
    xf = x.astype(F32)
    mu = jnp.mean(xf, axis=-1, keepdims=True)
    xc = xf - mu
    var = jnp.mean(xc * xc, axis=-1, keepdims=True)
    return (xc * lax.rsqrt(var + LN_EPS) * g.astype(F32) + b.astype(F32)).astype(x.dtype)


def modulate(xn, shift, scale):
    return xn * (1 + scale[:, None]) + shift[:, None]


def gmlp_mixer(h, w_in, b_in, ln_g, ln_b, w_s, b_s, w_out):
    B, T, _ = h.shape
    z = jax.nn.gelu(h @ w_in + b_in, approximate=False)
    u, v = jnp.split(z, 2, axis=-1)
    v = layer_norm(v, ln_g, ln_b)
    n_chunks = -(-T // GM_CHUNK)
    pad = n_chunks * GM_CHUNK - T
    vp = jnp.pad(v, ((0, 0), (0, pad), (0, 0))).reshape(B, n_chunks, GM_CHUNK, GM_GROUPS, GM_GROUP_DIM)
    w_causal = jnp.tril(w_s)
    mixed = jnp.einsum('gts,bnsgd->bntgd', w_causal, vp) + b_s.T[None, None, :, :, None]
    mixed = mixed.reshape(B, n_chunks * GM_CHUNK, GM_HALF)[:, :T]
    return (u * mixed) @ w_out, v


def gla_chunked(q, k, v, log_a, state0):
    B, H, T, DK = q.shape
    DV = v.shape[-1]
    C = GLA_CHUNK if T % GLA_CHUNK == 0 else T
    N = T // C
    dt = v.dtype
    qc = q.astype(F32).reshape(B, H, N, C, DK)
    kc = k.astype(F32).reshape(B, H, N, C, DK)
    vc = v.astype(F32).reshape(B, H, N, C, DV)
    b = jnp.cumsum(log_a.astype(F32).reshape(B, H, N, C, DK), axis=3)
    b_last = b[:, :, :, -1:, :]
    q_dec = qc * jnp.exp(b)
    k_inv = kc * jnp.exp(-b)
    k_end = kc * jnp.exp(b_last - b)
    causal = jnp.tril(jnp.ones((C, C), dtype=bool))
    att = jnp.where(causal, jnp.einsum('bhntk,bhnsk->bhnts', q_dec, k_inv), 0.0)
    o_intra = jnp.einsum('bhnts,bhnsv->bhntv', att, vc)
    d_state = jnp.einsum('bhnsk,bhnsv->bhnkv', k_end, vc)
    decay_chunk = jnp.exp(b_last[:, :, :, 0, :])

    def step(S, inp):
        dec, dS = inp
        return dec[..., None] * S + dS, S

    S_final, S_enter = lax.scan(step, state0.astype(F32),
                                (jnp.moveaxis(decay_chunk, 2, 0), jnp.moveaxis(d_state, 2, 0)))
    S_enter = jnp.moveaxis(S_enter, 0, 2)
    o_inter = jnp.einsum('bhntk,bhnkv->bhntv', q_dec, S_enter)
    o = (o_intra + o_inter).reshape(B, H, T, DV)
    return o.astype(dt), S_final.astype(state0.dtype)


def gla_mixer(h, state0, w_in, w_gk, b_gk, norm_g, w_out):
    B, T, _ = h.shape
    proj = h @ w_in
    cuts = [GLA_DK_TOT, 2 * GLA_DK_TOT, 2 * GLA_DK_TOT + GLA_DV_TOT, 2 * GLA_DK_TOT + 2 * GLA_DV_TOT]
    q, k, v, g_out, lr = jnp.split(proj, cuts, axis=-1)
    log_a = jax.nn.log_sigmoid((lr @ w_gk + b_gk).astype(F32)) / GLA_GATE_NORMALIZER

    def heads(t, d):
        return t.reshape(B, T, GLA_HEADS, d).transpose(0, 2, 1, 3)

    o, state = gla_chunked(heads(q, GLA_DK) * (GLA_DK ** -0.5), heads(k, GLA_DK),
                           heads(v, GLA_DV), heads(log_a, GLA_DK), state0)
    o = o.transpose(0, 2, 1, 3)
    o = rms_norm(o, norm_g) * jax.nn.silu(g_out.reshape(B, T, GLA_HEADS, GLA_DV))
    return o.reshape(B, T, GLA_DV_TOT) @ w_out, state


def route(x, router_w, router_bias):
    scores = jax.nn.sigmoid(x.astype(F32) @ router_w.astype(F32))
    sel = scores + router_bias.astype(F32)
    grp = sel.reshape(-1, N_EXPERT_GROUPS, N_EXPERTS // N_EXPERT_GROUPS)
    grp_score = jnp.sum(lax.top_k(grp, 2)[0], axis=-1)
    _, top_g = lax.top_k(grp_score, TOPK_GROUPS)
    g_mask = jnp.sum(jax.nn.one_hot(top_g, N_EXPERT_GROUPS, dtype=F32), axis=1) > 0
    e_mask = jnp.repeat(g_mask, N_EXPERTS // N_EXPERT_GROUPS, axis=-1)
    _, top_e = lax.top_k(jnp.where(e_mask, sel, -jnp.inf), TOP_K)
    w = jnp.take_along_axis(scores, top_e, axis=-1)
    w = w / jnp.sum(w, axis=-1, keepdims=True) * ROUTED_SCALE
    return jnp.sum(jax.nn.one_hot(top_e, N_EXPERTS, dtype=F32) * w[..., None], axis=1)


def moe_ffn(h, router_w, router_bias, w_gate, w_up, w_down, sw_gate, sw_up, sw_down):
    B, T, D = h.shape
    n = B * T
    blk = min(MOE_BLOCK, n)
    nb = -(-n // blk)
    xp = jnp.pad(h.reshape(n, D), ((0, nb * blk - n), (0, 0))).reshape(nb, blk, D)

    def block(xb):
        gates = route(xb, router_w, router_bias).astype(xb.dtype)
        hg = jnp.einsum('td,edf->tef', xb, w_gate)
        hu = jnp.einsum('td,edf->tef', xb, w_up)
        routed = jnp.einsum('tef,efd->td', jax.nn.silu(hg) * hu * gates[..., None], w_down)
        shared = (jax.nn.silu(xb @ sw_gate) * (xb @ sw_up)) @ sw_down
        return routed + shared

    out = lax.map(block, xp).reshape(nb * blk, D)[:n]
    return out.reshape(B, T, D)


def trunk(x, c, gla_state, norm_g, ada_w, ada_b, gm_w_in, gm_b_in, gm_ln_g, gm_ln_b, gm_w_s, gm_b_s,
          gm_w_out, gla_w_in, gla_w_gk, gla_b_gk, gla_norm_g, gla_w_out, router_w, router_bias,
          exp_w_gate, exp_w_up, exp_w_down, sh_w_gate, sh_w_up, sh_w_down):
    new_gla, new_v = [], []
    for i in range(DEPTH):
        mod = jax.nn.silu(c) @ ada_w[i] + ada_b[i]
        sh1, sc1, g1, sh2, sc2, g2 = jnp.split(mod, 6, axis=-1)
        hmix = modulate(rms_norm(x, norm_g[i, 0]), sh1, sc1)
        j = i // N_MIXERS
        if i % N_MIXERS == 0:
            y, v_rows = gmlp_mixer(hmix, gm_w_in[j], gm_b_in[j], gm_ln_g[j], gm_ln_b[j],
                                   gm_w_s[j], gm_b_s[j], gm_w_out[j])
            new_v.append(v_rows)
        else:
            y, s_new = gla_mixer(hmix, gla_state[:, j], gla_w_in[j], gla_w_gk[j], gla_b_gk[j],
                                 gla_norm_g[j], gla_w_out[j])
            new_gla.append(s_new)
        x = x + g1[:, None] * rms_norm(y, norm_g[i, 1])
        hffn = modulate(rms_norm(x, norm_g[i, 2]), sh2, sc2)
        y = moe_ffn(hffn, router_w[i], router_bias[i], exp_w_gate[i], exp_w_up[i], exp_w_down[i],
                    sh_w_gate[i], sh_w_up[i], sh_w_down[i])
        x = x + g2[:, None] * rms_norm(y, norm_g[i, 3])
    return x, jnp.stack(new_gla, axis=1), new_v


def setup_inputs(seed: int = 0) -> dict:
    key = jax.random.key(seed)
    keys = list(jax.random.split(key, 32))

    def nrm(shape, scale):
        return jax.random.normal(keys.pop(), shape, F32) * scale

    D = D_MODEL
    NA, NB = N_GMLP_LAYERS, N_GLA_LAYERS
    gla_in_cols = 2 * GLA_DK_TOT + 2 * GLA_DV_TOT + GLA_GATE_RANK
    return {
        'x_prompt': nrm((BATCH, SEQ, D), 1.0),
        'x_sample': nrm((DEC_BATCH, DEC_SEQ, D), 1.0),
        'state_gla': nrm((DEC_BATCH, NB, GLA_HEADS, GLA_DK, GLA_DV), 0.5),
        'c_prompt': nrm((BATCH, D), 1.0),
        'c_sample': nrm((DEC_BATCH, D), 1.0),
        'norm_g': 1.0 + nrm((DEPTH, 4, D), 0.02),
        'ada_w': nrm((DEPTH, D, 6 * D), D ** -0.5 * 0.3),
        'ada_b': nrm((DEPTH, 6 * D), 0.02),
        'gm_w_in': nrm((NA, D, 2 * GM_HALF), D ** -0.5),
        'gm_b_in': nrm((NA, 2 * GM_HALF), 0.02),
        'gm_ln_g': 1.0 + nrm((NA, GM_HALF), 0.02),
        'gm_ln_b': nrm((NA, GM_HALF), 0.02),
        'gm_w_s': nrm((NA, GM_GROUPS, GM_CHUNK, GM_CHUNK), GM_CHUNK ** -0.5 * 0.5),
        'gm_b_s': 1.0 + nrm((NA, GM_GROUPS, GM_CHUNK), 0.02),
        'gm_w_out': nrm((NA, GM_HALF, D), GM_HALF ** -0.5),
        'gla_w_in': nrm((NB, D, gla_in_cols), D ** -0.5),
        'gla_w_gk': nrm((NB, GLA_GATE_RANK, GLA_DK_TOT), GLA_GATE_RANK ** -0.5),
        'gla_b_gk': nrm((NB, GLA_DK_TOT), 0.02),
        'gla_norm_g': 1.0 + nrm((NB, GLA_DV), 0.02),
        'gla_w_out': nrm((NB, GLA_DV_TOT, D), GLA_DV_TOT ** -0.5),
        'router_w': nrm((DEPTH, D, N_EXPERTS), D ** -0.5),
        'router_bias': nrm((DEPTH, N_EXPERTS), 0.01),
        'exp_w_gate': nrm((DEPTH, N_EXPERTS, D, EXPERT_DIM), D ** -0.5),
        'exp_w_up': nrm((DEPTH, N_EXPERTS, D, EXPERT_DIM), D ** -0.5),
        'exp_w_down': nrm((DEPTH, N_EXPERTS, EXPERT_DIM, D), EXPERT_DIM ** -0.5),
        'sh_w_gate': nrm((DEPTH, D, SHARED_DIM), D ** -0.5),
        'sh_w_up': nrm((DEPTH, D, SHARED_DIM), D ** -0.5),
        'sh_w_down': nrm((DEPTH, SHARED_DIM, D), SHARED_DIM ** -0.5),
    }


def reference(x_prompt, x_sample, state_gla, c_prompt, c_sample, norm_g, ada_w, ada_b, gm_w_in, gm_b_in,
              gm_ln_g, gm_ln_b, gm_w_s, gm_b_s, gm_w_out, gla_w_in, gla_w_gk, gla_b_gk, gla_norm_g,
              gla_w_out, router_w, router_bias, exp_w_gate, exp_w_up, exp_w_down, sh_w_gate, sh_w_up,
              sh_w_down):
    weights = (norm_g, ada_w, ada_b, gm_w_in, gm_b_in, gm_ln_g, gm_ln_b, gm_w_s, gm_b_s, gm_w_out,
               gla_w_in, gla_w_gk, gla_b_gk, gla_norm_g, gla_w_out, router_w, router_bias,
               exp_w_gate, exp_w_up, exp_w_down, sh_w_gate, sh_w_up, sh_w_down)
    state0 = jnp.zeros((x_prompt.shape[0],) + state_gla.shape[1:], x_prompt.dtype)
    y_prompt, state_gla_prompt, _ = trunk(x_prompt, c_prompt, state0, *weights)
    y_sample, state_gla_sample, v_rows = trunk(x_sample, c_sample, state_gla, *weights)
    gmlp_v_sample = jnp.stack(v_rows, axis=2)
    return (y_prompt, y_sample, state_gla_prompt, state_gla_sample, gmlp_v_sample)
```

```python
import functools
import math

import jax
import jax.numpy as jnp
from jax import lax
from jax.experimental import pallas as pl
from jax.experimental.pallas import tpu as pltpu

F32 = jnp.float32
BF16 = jnp.bfloat16

D = 1024
GM_CHUNK = 128
GM_HALF = 2 * D
GM_GROUPS = 8
GM_GROUP_DIM = GM_HALF // GM_GROUPS
GLA_HEADS = 4
GLA_DK = 128
GLA_DV = 256
GLA_DK_TOT = GLA_HEADS * GLA_DK
GLA_DV_TOT = GLA_HEADS * GLA_DV
GLA_GATE_RANK = 16
GLA_GATE_NORMALIZER = 16.0
GLA_CHUNK = 64
N_EXPERTS = 64
TOP_K = 8
N_EXPERT_GROUPS = 8
GROUP_SIZE = N_EXPERTS // N_EXPERT_GROUPS
TOPK_GROUPS = 4
EXPERT_DIM = D // 4
ROUTED_SCALE = 2.5
NORM_EPS = 1e-6
LN_EPS = 1e-5

LANES = 128
VMEM_LIMIT = 56 * 1024 * 1024

MIX_TILE = 256
MOE_TILE = 1024
ROUTER_TILE = 1024


def _cparams(*sem):
    return pltpu.CompilerParams(dimension_semantics=sem, vmem_limit_bytes=VMEM_LIMIT)


def _rms(x, g):
    return x * lax.rsqrt(jnp.mean(x * x, axis=-1, keepdims=True) + NORM_EPS) * g


def _silu(x):
    return x * (1.0 / (1.0 + jnp.exp(-x)))


def _gelu(x):
    return 0.5 * x * (1.0 + lax.erf(x * (1.0 / math.sqrt(2.0))))


def _bdot(a, b):
    return jnp.dot(a.astype(BF16), b.astype(BF16), preferred_element_type=F32)


def _dot_nt(a, b, precision=None):
    return lax.dot_general(a, b, (((1,), (1,)), ((), ())), preferred_element_type=F32,
                           precision=precision)


def _mod_slices(mod_ref):
    return [mod_ref[:, i * D:(i + 1) * D] for i in range(6)]


def _ffn_prep(x1, ng, sh2, sc2, rw_ref, h_ref, lg_ref):
    hffn = _rms(x1, ng[2:3]) * (1.0 + sc2) + sh2
    h_ref[...] = hffn.astype(BF16)
    lg_ref[...] = _dot_nt(rw_ref[...], hffn, precision=lax.Precision.HIGHEST)


def _ada_body(c_ref, w_ref, b_ref, o_ref):
    c = c_ref[...]
    o_ref[...] = _bdot(_silu(c), w_ref[...]) + b_ref[...]


def _ada(c, ada_w, ada_b):
    n = c.shape[0]
    depth = ada_w.shape[0]
    tn = 1536
    return pl.pallas_call(
        _ada_body,
        grid=(depth, 6 * D // tn),
        in_specs=[pl.BlockSpec((n, D), lambda l, j: (0, 0)),
                  pl.BlockSpec((None, D, tn), lambda l, j: (l, 0, j)),
                  pl.BlockSpec((None, 1, tn), lambda l, j: (l, 0, j))],
        out_specs=pl.BlockSpec((None, n, tn), lambda l, j: (l, 0, j)),
        out_shape=jax.ShapeDtypeStruct((depth, n, 6 * D), F32),
        compiler_params=_cparams("parallel", "parallel"),
        name="ada_mod",
    )(c, ada_w, ada_b.reshape(depth, 1, 6 * D))


def _mod_spec(per_row, tt, tiles_per_batch):
    if per_row:
        return pl.BlockSpec((tt, 6 * D), lambda i: (i, 0))
    return pl.BlockSpec((None, 1, 6 * D), lambda i: (i // tiles_per_batch, 0, 0))


def _const_spec(shape):
    zeros = (0,) * len(shape)
    return pl.BlockSpec(shape, lambda *_: zeros)


def _gmlp_body(x_ref, mod_ref, ng_ref, win_ref, bin_ref, lng_ref, lnb_ref, ws_ref, bs_ref, wout_ref,
               rw_ref, x1_ref, h_ref, lg_ref, *rest, n_chunks, emit_v):
    if emit_v:
        v_ref, um_ref = rest
    else:
        (um_ref,) = rest
    sh1, sc1, g1, sh2, sc2, _ = _mod_slices(mod_ref)
    ng = ng_ref[...]
    x = x_ref[...]
    hb = (_rms(x, ng[0:1]) * (1.0 + sc1) + sh1).astype(BF16)
    u = _gelu(jnp.dot(hb, win_ref[:, :GM_HALF], preferred_element_type=F32) + bin_ref[:, :GM_HALF])
    v = _gelu(jnp.dot(hb, win_ref[:, GM_HALF:], preferred_element_type=F32) + bin_ref[:, GM_HALF:])
    mu = jnp.mean(v, axis=-1, keepdims=True)
    vc = v - mu
    var = jnp.mean(vc * vc, axis=-1, keepdims=True)
    v = vc * lax.rsqrt(var + LN_EPS) * lng_ref[...] + lnb_ref[...]
    if emit_v:
        v_ref[...] = v
    vb = v.astype(BF16)
    for c in range(n_chunks):
        rows = slice(c * GM_CHUNK, (c + 1) * GM_CHUNK)
        for g in range(GM_GROUPS):
            cols = slice(g * GM_GROUP_DIM, (g + 1) * GM_GROUP_DIM)
            mixed = jnp.dot(ws_ref[g], vb[rows, cols], preferred_element_type=F32) + bs_ref[:, g:g + 1]
            um_ref[rows, cols] = (u[rows, cols] * mixed).astype(BF16)
    y = jnp.dot(um_ref[...], wout_ref[...], preferred_element_type=F32)
    x1 = x + g1 * _rms(y, ng[1:2])
    x1_ref[...] = x1
    _ffn_prep(x1, ng, sh2, sc2, rw_ref, h_ref, lg_ref)


def _gmlp_mixer(x2d, mod, per_row, tt, tiles_per_batch, ng, win, b_in, ln_g, ln_b, ws, bs, wout, rw_t,
                emit_v):
    n = x2d.shape[0]
    out_shape = [jax.ShapeDtypeStruct((n, D), F32), jax.ShapeDtypeStruct((n, D), BF16),
                 jax.ShapeDtypeStruct((N_EXPERTS, n), F32)]
    out_specs = [pl.BlockSpec((tt, D), lambda i: (i, 0)), pl.BlockSpec((tt, D), lambda i: (i, 0)),
                 pl.BlockSpec((N_EXPERTS, tt), lambda i: (0, i))]
    if emit_v:
        out_shape.append(jax.ShapeDtypeStruct((n, GM_HALF), F32))
        out_specs.append(pl.BlockSpec((tt, GM_HALF), lambda i: (i, 0)))
    return pl.pallas_call(
        functools.partial(_gmlp_body, n_chunks=tt // GM_CHUNK, emit_v=emit_v),
        grid=(n // tt,),
        in_specs=[pl.BlockSpec((tt, D), lambda i: (i, 0)),
                  _mod_spec(per_row, tt, tiles_per_batch),
                  _const_spec(ng.shape), _const_spec(win.shape), _const_spec(b_in.shape),
                  _const_spec(ln_g.shape), _const_spec(ln_b.shape), _const_spec(ws.shape),
                  _const_spec(bs.shape), _const_spec(wout.shape), _const_spec(rw_t.shape)],
        out_specs=out_specs,
        out_shape=out_shape,
        scratch_shapes=[pltpu.VMEM((tt, GM_HALF), BF16)],
        compiler_params=_cparams("parallel"),
        name="gmlp_mixer_rows" if per_row else "gmlp_mixer",
    )(x2d, mod, ng, win, b_in, ln_g, ln_b, ws, bs, wout, rw_t)


def _resid_body(x_ref, y_ref, mod_ref, ng_ref, o_ref):
    g2 = mod_ref[:, 5 * D:6 * D]
    o_ref[...] = x_ref[...] + g2 * _rms(y_ref[...], ng_ref[3:4, :])


def _resid(x2d, y2d, y_block0, mod, per_row, tt, tiles_per_batch, ng):
    n = x2d.shape[0]
    return pl.pallas_call(
        _resid_body,
        grid=(n // tt,),
        in_specs=[pl.BlockSpec((tt, D), lambda i: (i, 0)),
                  pl.BlockSpec((tt, D), lambda i: (i + y_block0, 0)),
                  _mod_spec(per_row, tt, tiles_per_batch),
                  _const_spec(ng.shape)],
        out_specs=pl.BlockSpec((tt, D), lambda i: (i, 0)),
        out_shape=jax.ShapeDtypeStruct((n, D), F32),
        compiler_params=_cparams("parallel"),
        name="resid_rows" if per_row else "resid",
    )(x2d, y2d, mod, ng)


def _router_body(lg_ref, bias_ref, g_ref):
    lg = lg_ref[...]
    tn = lg.shape[1]
    scores = 1.0 / (1.0 + jnp.exp(-lg))
    sel = scores + bias_ref[...]
    neg = -jnp.inf
    sub8 = lax.broadcasted_iota(jnp.int32, (GROUP_SIZE, tn), 0)
    gsub = lax.broadcasted_iota(jnp.int32, (N_EXPERT_GROUPS, tn), 0)
    gs = jnp.zeros((N_EXPERT_GROUPS, tn), F32)
    for g in range(N_EXPERT_GROUPS):
        blk = sel[g * GROUP_SIZE:(g + 1) * GROUP_SIZE, :]
        m1 = jnp.max(blk, axis=0, keepdims=True)
        i1 = jnp.min(jnp.where(blk == m1, sub8, GROUP_SIZE), axis=0, keepdims=True)
        m2 = jnp.max(jnp.where(sub8 == i1, neg, blk), axis=0, keepdims=True)
        gs = jnp.where(gsub == g, m1 + m2, gs)
    gmask = jnp.zeros((N_EXPERT_GROUPS, tn), jnp.bool_)
    for _ in range(TOPK_GROUPS):
        m = jnp.max(gs, axis=0, keepdims=True)
        i = jnp.min(jnp.where(gs == m, gsub, N_EXPERT_GROUPS), axis=0, keepdims=True)
        hit = gsub == i
        gmask = jnp.logical_or(gmask, hit)
        gs = jnp.where(hit, neg, gs)
    gmaskf = gmask.astype(F32)
    blocks = []
    for g in range(N_EXPERT_GROUPS):
        keep = jnp.broadcast_to(gmaskf[g:g + 1, :], (GROUP_SIZE, tn)) > 0.5
        blocks.append(jnp.where(keep, sel[g * GROUP_SIZE:(g + 1) * GROUP_SIZE, :], neg))
    msel = jnp.concatenate(blocks, axis=0)
    esub = lax.broadcasted_iota(jnp.int32, (N_EXPERTS, tn), 0)
    chosen = jnp.zeros((N_EXPERTS, tn), jnp.bool_)
    for _ in range(TOP_K):
        m = jnp.max(msel, axis=0, keepdims=True)
        i = jnp.min(jnp.where(msel == m, esub, N_EXPERTS), axis=0, keepdims=True)
        hit = esub == i
        chosen = jnp.logical_or(chosen, hit)
        msel = jnp.where(hit, neg, msel)
    w = jnp.where(chosen, scores, 0.0)
    g_ref[...] = w / jnp.sum(w, axis=0, keepdims=True) * ROUTED_SCALE


def _router(lg_t, bias):
    n = lg_t.shape[1]
    tn = ROUTER_TILE
    return pl.pallas_call(
        _router_body,
        grid=(n // tn,),
        in_specs=[pl.BlockSpec((N_EXPERTS, tn), lambda i: (0, i)), _const_spec((N_EXPERTS, 1))],
        out_specs=pl.BlockSpec((N_EXPERTS, tn), lambda i: (0, i)),
        out_shape=jax.ShapeDtypeStruct((N_EXPERTS, n), F32),
        compiler_params=_cparams("parallel"),
        name="router",
    )(lg_t, bias.reshape(N_EXPERTS, 1))


def _moe_body(x_ref, g_ref, wg_ref, wu_ref, wd_ref, swg_ref, swu_ref, swd_ref, o_ref):
    e = pl.program_id(1)
    x = x_ref[...]

    @pl.when(e == 0)
    def _():
        hs = _silu(jnp.dot(x, swg_ref[...], preferred_element_type=F32)) * jnp.dot(
            x, swu_ref[...], preferred_element_type=F32)
        o_ref[...] = jnp.dot(hs.astype(BF16), swd_ref[...], preferred_element_type=F32)

    hg = jnp.dot(x, wg_ref[...], preferred_element_type=F32)
    hu = jnp.dot(x, wu_ref[...], preferred_element_type=F32)
    gates = g_ref[...]
    lane = lax.broadcasted_iota(jnp.int32, gates.shape, 1)
    gcol = jnp.sum(jnp.where(lane == e, gates, 0.0), axis=1, keepdims=True)
    a = _silu(hg) * hu * gcol
    o_ref[...] += jnp.dot(a.astype(BF16), wd_ref[...], preferred_element_type=F32)


def _moe_dense(h, gates, wg, wu, wd, swg, swu, swd):
    n = h.shape[0]
    tm = MOE_TILE
    return pl.pallas_call(
        _moe_body,
        grid=(n // tm, N_EXPERTS),
        in_specs=[pl.BlockSpec((tm, D), lambda i, e: (i, 0)),
                  pl.BlockSpec((tm, N_EXPERTS), lambda i, e: (i, 0)),
                  pl.BlockSpec((None, D, EXPERT_DIM), lambda i, e: (e, 0, 0)),
                  pl.BlockSpec((None, D, EXPERT_DIM), lambda i, e: (e, 0, 0)),
                  pl.BlockSpec((None, EXPERT_DIM, D), lambda i, e: (e, 0, 0)),
                  _const_spec(swg.shape), _const_spec(swu.shape), _const_spec(swd.shape)],
        out_specs=pl.BlockSpec((tm, D), lambda i, e: (i, 0)),
        out_shape=jax.ShapeDtypeStruct((n, D), F32),
        compiler_params=_cparams("parallel", "arbitrary"),
        name="moe_dense",
    )(h, gates, wg, wu, wd, swg, swu, swd)


def _log_sigmoid(z):
    return jnp.minimum(z, 0.0) - jnp.log(1.0 + jnp.exp(-jnp.abs(z)))


def _gla_gate(hb, wlr_ref, wgk_ref, bgk_ref):
    lr = jnp.dot(hb, wlr_ref[...], preferred_element_type=F32)
    z = _bdot(lr, wgk_ref[...]) + bgk_ref[...]
    return _log_sigmoid(z) * (1.0 / GLA_GATE_NORMALIZER)


def _split3(a):
    hi = a.astype(BF16)
    r1 = a - hi.astype(F32)
    mid = r1.astype(BF16)
    lo = (r1 - mid.astype(F32)).astype(BF16)
    return hi, mid, lo


def _gla_out(o_ref_val, go, gng):
    parts = []
    for hd in range(GLA_HEADS):
        cols = slice(hd * GLA_DV, (hd + 1) * GLA_DV)
        parts.append((_rms(o_ref_val[:, cols], gng) * _silu(go[:, cols])).astype(BF16))
    return jnp.concatenate(parts, axis=1)


def _gla_body(x_ref, mod_ref, ng_ref, wqkvg_ref, wlr_ref, wgk_ref, bgk_ref, tril_ref, gng_ref, wout_ref,
              rw_ref, x1_ref, h_ref, lg_ref, st_ref, st_scr, o_scr, *, tt):
    j = pl.program_id(1)

    @pl.when(j == 0)
    def _():
        st_scr[...] = jnp.zeros_like(st_scr)

    sh1, sc1, g1, sh2, sc2, _ = _mod_slices(mod_ref)
    ng = ng_ref[...]
    x = x_ref[...]
    hb = (_rms(x, ng[0:1]) * (1.0 + sc1) + sh1).astype(BF16)
    proj = jnp.dot(hb, wqkvg_ref[...], preferred_element_type=F32)
    q = proj[:, :GLA_DK_TOT] * (GLA_DK ** -0.5)
    k = proj[:, GLA_DK_TOT:2 * GLA_DK_TOT]
    v = proj[:, 2 * GLA_DK_TOT:2 * GLA_DK_TOT + GLA_DV_TOT].astype(BF16)
    go = proj[:, 2 * GLA_DK_TOT + GLA_DV_TOT:]
    log_a = _gla_gate(hb, wlr_ref, wgk_ref, bgk_ref)
    tril = tril_ref[...]
    b = sum(jnp.dot(tril, part, preferred_element_type=F32) for part in _split3(log_a))
    row = lax.broadcasted_iota(jnp.int32, (GLA_CHUNK, GLA_CHUNK), 0)
    col = lax.broadcasted_iota(jnp.int32, (GLA_CHUNK, GLA_CHUNK), 1)
    causal = row >= col
    for c in range(tt // GLA_CHUNK):
        rows = slice(c * GLA_CHUNK, (c + 1) * GLA_CHUNK)
        last = (c + 1) * GLA_CHUNK - 1
        for hd in range(GLA_HEADS):
            kc = slice(hd * GLA_DK, (hd + 1) * GLA_DK)
            vc = slice(hd * GLA_DV, (hd + 1) * GLA_DV)
            bb = b[rows, kc]
            b_last = b[last:last + 1, kc]
            q_dec = (q[rows, kc] * jnp.exp(bb)).astype(BF16)
            k_inv = (k[rows, kc] * jnp.exp(-bb)).astype(BF16)
            k_end = (k[rows, kc] * jnp.exp(b_last - bb)).astype(BF16)
            att = jnp.where(causal, _dot_nt(q_dec, k_inv), 0.0).astype(BF16)
            st = st_scr[hd]
            o = jnp.dot(att, v[rows, vc], preferred_element_type=F32) + _dot_nt(q_dec, st.astype(BF16))
            o_scr[rows, vc] = o
            d_st = lax.dot_general(v[rows, vc], k_end, (((0,), (0,)), ((), ())),
                                   preferred_element_type=F32)
            st_scr[hd] = st * jnp.exp(b_last) + d_st

    @pl.when(j == pl.num_programs(1) - 1)
    def _():
        for hd in range(GLA_HEADS):
            st_ref[hd] = st_scr[hd].T

    y = jnp.dot(_gla_out(o_scr[...], go, gng_ref[...]), wout_ref[...], preferred_element_type=F32)
    x1 = x + g1 * _rms(y, ng[1:2])
    x1_ref[...] = x1
    _ffn_prep(x1, ng, sh2, sc2, rw_ref, h_ref, lg_ref)


def _gla_mixer(x2d, mod3, batch, seq, ng, wqkvg, wlr, wgk, bgk, gng, wout, rw_t):
    tt = MIX_TILE
    tpb = seq // tt
    n = x2d.shape[0]
    idx = jnp.arange(tt)
    tril = ((idx[:, None] >= idx[None, :]) &
            (idx[:, None] // GLA_CHUNK == idx[None, :] // GLA_CHUNK)).astype(BF16)
    row_map = lambda b, j: (b * tpb + j, 0)
    consts = (ng, wqkvg, wlr, wgk, bgk, tril, gng, wout, rw_t)
    return pl.pallas_call(
        functools.partial(_gla_body, tt=tt),
        grid=(batch, tpb),
        in_specs=[pl.BlockSpec((tt, D), row_map),
                  pl.BlockSpec((None, 1, 6 * D), lambda b, j: (b, 0, 0))] +
                 [_const_spec(a.shape) for a in consts],
        out_specs=[pl.BlockSpec((tt, D), row_map), pl.BlockSpec((tt, D), row_map),
                   pl.BlockSpec((N_EXPERTS, tt), lambda b, j: (0, b * tpb + j)),
                   pl.BlockSpec((None, GLA_HEADS, GLA_DK, GLA_DV), lambda b, j: (b, 0, 0, 0))],
        out_shape=[jax.ShapeDtypeStruct((n, D), F32), jax.ShapeDtypeStruct((n, D), BF16),
                   jax.ShapeDtypeStruct((N_EXPERTS, n), F32),
                   jax.ShapeDtypeStruct((batch, GLA_HEADS, GLA_DK, GLA_DV), F32)],
        scratch_shapes=[pltpu.VMEM((GLA_HEADS, GLA_DV, GLA_DK), F32),
                        pltpu.VMEM((tt, GLA_DV_TOT), F32)],
        compiler_params=_cparams("parallel", "arbitrary"),
        name="gla_mixer",
    )(x2d, mod3, *consts)


def _gla1_proj_body(x_ref, mod_ref, ng_ref, wqkvg_ref, wlr_ref, wgk_ref, bgk_ref,
                    q_ref, k_ref, v_ref, go_ref, dec_ref):
    sh1, sc1, _, _, _, _ = _mod_slices(mod_ref)
    ng = ng_ref[...]
    hb = (_rms(x_ref[...], ng[0:1]) * (1.0 + sc1) + sh1).astype(BF16)
    proj = jnp.dot(hb, wqkvg_ref[...], preferred_element_type=F32)
    q_ref[...] = proj[:, :GLA_DK_TOT] * (GLA_DK ** -0.5)
    k_ref[...] = proj[:, GLA_DK_TOT:2 * GLA_DK_TOT]
    v_ref[...] = proj[:, 2 * GLA_DK_TOT:2 * GLA_DK_TOT + GLA_DV_TOT]
    go_ref[...] = proj[:, 2 * GLA_DK_TOT + GLA_DV_TOT:]
    dec_ref[...] = jnp.exp(_gla_gate(hb, wlr_ref, wgk_ref, bgk_ref))


GLA1_TOK = 8


def _gla1_state_body(st_ref, qc_ref, kc_ref, dc_ref, v_ref, nst_ref, o_ref):
    v = v_ref[...]
    for i in range(GLA1_TOK):
        for hd in range(GLA_HEADS):
            vrow = v[i:i + 1, hd * GLA_DV:(hd + 1) * GLA_DV]
            s_new = dc_ref[hd][:, i:i + 1] * st_ref[i, hd] + kc_ref[hd][:, i:i + 1] * vrow
            nst_ref[i, hd] = s_new
            o_ref[i:i + 1, hd * GLA_DV:(hd + 1) * GLA_DV] = jnp.sum(
                qc_ref[hd][:, i:i + 1] * s_new, axis=0, keepdims=True)


def _gla1_out_body(x_ref, o_ref, go_ref, mod_ref, ng_ref, gng_ref, wout_ref, rw_ref, x1_ref, h_ref, lg_ref):
    _, _, g1, sh2, sc2, _ = _mod_slices(mod_ref)
    ng = ng_ref[...]
    y = jnp.dot(_gla_out(o_ref[...], go_ref[...], gng_ref[...]), wout_ref[...], preferred_element_type=F32)
    x1 = x_ref[...] + g1 * _rms(y, ng[1:2])
    x1_ref[...] = x1
    _ffn_prep(x1, ng, sh2, sc2, rw_ref, h_ref, lg_ref)


def _gla_mixer_one(x2d, mod2, state, ng, wqkvg, wlr, wgk, bgk, gng, wout, rw_t):
    n = x2d.shape[0]
    consts = (ng, wqkvg, wlr, wgk, bgk)
    q, k, v, go, dec = pl.pallas_call(
        _gla1_proj_body,
        in_specs=[_const_spec(a.shape) for a in (x2d, mod2) + consts],
        out_specs=[_const_spec((n, GLA_DK_TOT)), _const_spec((n, GLA_DK_TOT)), _const_spec((n, GLA_DV_TOT)),
                   _const_spec((n, GLA_DV_TOT)), _const_spec((n, GLA_DK_TOT))],
        out_shape=[jax.ShapeDtypeStruct((n, GLA_DK_TOT), F32), jax.ShapeDtypeStruct((n, GLA_DK_TOT), F32),
                   jax.ShapeDtypeStruct((n, GLA_DV_TOT), F32), jax.ShapeDtypeStruct((n, GLA_DV_TOT), F32),
                   jax.ShapeDtypeStruct((n, GLA_DK_TOT), F32)],
        grid=(1,),
        compiler_params=_cparams("arbitrary"),
        name="gla1_proj",
    )(x2d, mod2, *consts)

    def cols(a):
        return a.reshape(n // GLA1_TOK, GLA1_TOK, GLA_HEADS, GLA_DK).transpose(0, 2, 3, 1)

    col_spec = pl.BlockSpec((None, GLA_HEADS, GLA_DK, GLA1_TOK), lambda i: (i, 0, 0, 0))
    st_spec = pl.BlockSpec((GLA1_TOK, GLA_HEADS, GLA_DK, GLA_DV), lambda i: (i, 0, 0, 0))
    new_state, o = pl.pallas_call(
        _gla1_state_body,
        grid=(n // GLA1_TOK,),
        in_specs=[st_spec, col_spec, col_spec, col_spec, pl.BlockSpec((GLA1_TOK, GLA_DV_TOT), lambda i: (i, 0))],
        out_specs=[st_spec, pl.BlockSpec((GLA1_TOK, GLA_DV_TOT), lambda i: (i, 0))],
        out_shape=[jax.ShapeDtypeStruct(state.shape, F32), jax.ShapeDtypeStruct((n, GLA_DV_TOT), F32)],
        compiler_params=_cparams("parallel"),
        name="gla1_state",
    )(state, cols(q), cols(k), cols(dec), v)

    consts = (mod2, ng, gng, wout, rw_t)
    x1, h, lg = pl.pallas_call(
        _gla1_out_body,
        grid=(1,),
        in_specs=[_const_spec(a.shape) for a in (x2d, o, go) + consts],
        out_specs=[_const_spec((n, D)), _const_spec((n, D)), _const_spec((N_EXPERTS, n))],
        out_shape=[jax.ShapeDtypeStruct((n, D), F32), jax.ShapeDtypeStruct((n, D), BF16),
                   jax.ShapeDtypeStruct((N_EXPERTS, n), F32)],
        compiler_params=_cparams("arbitrary"),
        name="gla1_out",
    )(x2d, o, go, *consts)
    return x1, h, lg, new_state


def _moe_layer(h_p, h_s, lg_p, lg_s, router_bias, wg, wu, wd, swg, swu, swd):
    n = h_p.shape[0] + h_s.shape[0]
    n_pad = -(-n // MOE_TILE) * MOE_TILE
    h = jnp.pad(jnp.concatenate([h_p, h_s], axis=0), ((0, n_pad - n), (0, 0)))
    lg = jnp.pad(jnp.concatenate([lg_p, lg_s], axis=1), ((0, 0), (0, n_pad - n)))
    gates = _router(lg, router_bias).T
    return _moe_dense(h, gates, wg.astype(BF16), wu.astype(BF16), wd.astype(BF16),
                      swg.astype(BF16), swu.astype(BF16), swd.astype(BF16))


def kernel(x_prompt, x_sample, state_gla, c_prompt, c_sample, norm_g, ada_w, ada_b, gm_w_in, gm_b_in,
           gm_ln_g, gm_ln_b, gm_w_s, gm_b_s, gm_w_out, gla_w_in, gla_w_gk, gla_b_gk, gla_norm_g,
           gla_w_out, router_w, router_bias, exp_w_gate, exp_w_up, exp_w_down, sh_w_gate, sh_w_up,
           sh_w_down):
    batch, seq, _ = x_prompt.shape
    n_s = x_sample.shape[0]
    n_p = batch * seq
    tpb = seq // MIX_TILE
    xp = x_prompt.reshape(n_p, D)
    xs = x_sample.reshape(n_s, D)

    mod = _ada(jnp.concatenate([c_prompt, c_sample], axis=0), ada_w, ada_b)
    mod_p = [mod[i, :batch].reshape(batch, 1, 6 * D) for i in range(2)]
    mod_s = [mod[i, batch:] for i in range(2)]
    rw_t = [router_w[i].T for i in range(2)]

    ws_causal = jnp.tril(gm_w_s[0]).astype(BF16)
    bs_cols = gm_b_s[0].T
    eye = jnp.eye(GM_CHUNK, dtype=F32)
    ws_first = (gm_w_s[0][:, 0, 0][:, None, None] * eye).astype(BF16)
    bs_first = jnp.broadcast_to(gm_b_s[0][:, 0][None, :], (GM_CHUNK, GM_GROUPS))
    gm_args = (norm_g[0], gm_w_in[0].astype(BF16), gm_b_in[0].reshape(1, -1), gm_ln_g[0].reshape(1, -1),
               gm_ln_b[0].reshape(1, -1))
    wout0 = gm_w_out[0].astype(BF16)
    x1p, hp, lgp = _gmlp_mixer(xp, mod_p[0], False, MIX_TILE, tpb, *gm_args, ws_causal, bs_cols, wout0,
                               rw_t[0], emit_v=False)
    x1s, hs, lgs, v_rows = _gmlp_mixer(xs, mod_s[0], True, n_s, 1, *gm_args, ws_first, bs_first, wout0,
                                       rw_t[0], emit_v=True)
    y0 = _moe_layer(hp, hs, lgp, lgs, router_bias[0], exp_w_gate[0], exp_w_up[0], exp_w_down[0],
                    sh_w_gate[0], sh_w_up[0], sh_w_down[0])
    x2p = _resid(x1p, y0, 0, mod_p[0], False, MIX_TILE, tpb, norm_g[0])
    x2s = _resid(x1s, y0, n_p // n_s, mod_s[0], True, n_s, 1, norm_g[0])

    n_qkvg = 2 * GLA_DK_TOT + 2 * GLA_DV_TOT
    wqkvg = gla_w_in[0][:, :n_qkvg].astype(BF16)
    wlr = jnp.pad(gla_w_in[0][:, n_qkvg:], ((0, 0), (0, LANES - GLA_GATE_RANK))).astype(BF16)
    wgk = jnp.pad(gla_w_gk[0], ((0, LANES - GLA_GATE_RANK), (0, 0))).astype(BF16)
    gla_args = (norm_g[1], wqkvg, wlr, wgk, gla_b_gk[0].reshape(1, -1), gla_norm_g[0].reshape(1, -1),
                gla_w_out[0].astype(BF16), rw_t[1])
    x3p, hp, lgp, st_p = _gla_mixer(x2p, mod_p[1], batch, seq, *gla_args)
    x3s, hs, lgs, st_s = _gla_mixer_one(x2s, mod_s[1], state_gla[:, 0], *gla_args)
    y1 = _moe_layer(hp, hs, lgp, lgs, router_bias[1], exp_w_gate[1], exp_w_up[1], exp_w_down[1],
                    sh_w_gate[1], sh_w_up[1], sh_w_down[1])
    x4p = _resid(x3p, y1, 0, mod_p[1], False, MIX_TILE, tpb, norm_g[1])
    x4s = _resid(x3s, y1, n_p // n_s, mod_s[1], True, n_s, 1, norm_g[1])

    return (x4p.reshape(batch, seq, D), x4s.reshape(n_s, 1, D), st_p[:, None], st_s[:, None],
            v_rows.reshape(n_s, 1, 1, GM_HALF))
```

```python
import functools
import math

import jax
import jax.numpy as jnp
from jax import lax
from jax.experimental import pallas as pl
from jax.experimental.pallas import tpu as pltpu
from jax.experimental.pallas import tpu_sc as plsc

F32 = jnp.float32
BF16 = jnp.bfloat16

D = 1024
DP = D // 2
GM_CHUNK = 128
GM_HALF = 2 * D
GM_GROUPS = 8
GM_GROUP_DIM = GM_HALF // GM_GROUPS
GLA_HEADS = 4
GLA_DK = 128
GLA_DV = 256
GLA_DK_TOT = GLA_HEADS * GLA_DK
GLA_DV_TOT = GLA_HEADS * GLA_DV
GLA_GATE_RANK = 16
GLA_GATE_NORMALIZER = 16.0
GLA_CHUNK = 64
N_EXPERTS = 64
TOP_K = 8
N_EXPERT_GROUPS = 8
GROUP_SIZE = N_EXPERTS // N_EXPERT_GROUPS
TOPK_GROUPS = 4
EXPERT_DIM = D // 4
ROUTED_SCALE = 2.5
NORM_EPS = 1e-6
LN_EPS = 1e-5

LANES = 128
VMEM_LIMIT = 56 * 1024 * 1024

MIX_TILE = 256
ROUTER_TILE = 512
EXPERT_TILE = 512
SC_WORKERS = 32
DISPATCH_W = 32
GATHER_W = 64
TOKEN_PAD = SC_WORKERS * DISPATCH_W


def _cparams(*sem):
    return pltpu.CompilerParams(dimension_semantics=sem, vmem_limit_bytes=VMEM_LIMIT)


def _rms(x, g):
    return x * lax.rsqrt(jnp.mean(x * x, axis=-1, keepdims=True) + NORM_EPS) * g


def _silu(x):
    return x * (1.0 / (1.0 + jnp.exp(-x)))


def _gelu(x):
    return 0.5 * x * (1.0 + lax.erf(x * (1.0 / math.sqrt(2.0))))


def _bdot(a, b):
    return jnp.dot(a.astype(BF16), b.astype(BF16), preferred_element_type=F32)


def _dot_nt(a, b, precision=None):
    return lax.dot_general(a, b, (((1,), (1,)), ((), ())), preferred_element_type=F32,
                           precision=precision)


def _mod_slices(mod_ref):
    return [mod_ref[:, i * D:(i + 1) * D] for i in range(6)]


HI_HALF = -65536


def _pack_rows(x):
    lo = lax.bitcast_convert_type(x[:, :DP].astype(BF16).astype(F32), jnp.int32)
    hi = lax.bitcast_convert_type(x[:, DP:].astype(BF16).astype(F32), jnp.int32)
    return lax.shift_right_logical(lo, 16) | (hi & HI_HALF)


def _unpack_rows(p):
    lo = lax.bitcast_convert_type(lax.shift_left(p, 16), F32)
    hi = lax.bitcast_convert_type(p & HI_HALF, F32)
    return lo, hi


def _ffn_prep(x1, ng, sh2, sc2, rw_ref, h_ref, lg_ref):
    hffn = _rms(x1, ng[2:3]) * (1.0 + sc2) + sh2
    h_ref[...] = _pack_rows(hffn)
    lg_ref[...] = _dot_nt(rw_ref[...], hffn, precision=lax.Precision.HIGHEST)


def _ada_body(c_ref, w_ref, b_ref, o_ref):
    c = c_ref[...]
    o_ref[...] = _bdot(_silu(c), w_ref[...]) + b_ref[...]


def _ada(c, ada_w, ada_b):
    n = c.shape[0]
    depth = ada_w.shape[0]
    tn = 1536
    return pl.pallas_call(
        _ada_body,
        grid=(depth, 6 * D // tn),
        in_specs=[pl.BlockSpec((n, D), lambda l, j: (0, 0)),
                  pl.BlockSpec((None, D, tn), lambda l, j: (l, 0, j)),
                  pl.BlockSpec((None, 1, tn), lambda l, j: (l, 0, j))],
        out_specs=pl.BlockSpec((None, n, tn), lambda l, j: (l, 0, j)),
        out_shape=jax.ShapeDtypeStruct((depth, n, 6 * D), F32),
        compiler_params=_cparams("parallel", "parallel"),
        name="ada_mod",
    )(c, ada_w, ada_b.reshape(depth, 1, 6 * D))


def _mod_spec(per_row, tt, tiles_per_batch):
    if per_row:
        return pl.BlockSpec((tt, 6 * D), lambda i: (i, 0))
    return pl.BlockSpec((None, 1, 6 * D), lambda i: (i // tiles_per_batch, 0, 0))


def _const_spec(shape):
    zeros = (0,) * len(shape)
    return pl.BlockSpec(shape, lambda *_: zeros)


def _gmlp_body(x_ref, mod_ref, ng_ref, win_ref, bin_ref, lng_ref, lnb_ref, ws_ref, bs_ref, wout_ref,
               rw_ref, x1_ref, h_ref, lg_ref, *rest, n_chunks, emit_v):
    if emit_v:
        v_ref, um_ref = rest
    else:
        (um_ref,) = rest
    sh1, sc1, g1, sh2, sc2, _ = _mod_slices(mod_ref)
    ng = ng_ref[...]
    x = x_ref[...]
    hb = (_rms(x, ng[0:1]) * (1.0 + sc1) + sh1).astype(BF16)
    u = _gelu(jnp.dot(hb, win_ref[:, :GM_HALF], preferred_element_type=F32) + bin_ref[:, :GM_HALF])
    v = _gelu(jnp.dot(hb, win_ref[:, GM_HALF:], preferred_element_type=F32) + bin_ref[:, GM_HALF:])
    mu = jnp.mean(v, axis=-1, keepdims=True)
    vc = v - mu
    var = jnp.mean(vc * vc, axis=-1, keepdims=True)
    v = vc * lax.rsqrt(var + LN_EPS) * lng_ref[...] + lnb_ref[...]
    if emit_v:
        v_ref[...] = v
    vb = v.astype(BF16)
    for c in range(n_chunks):
        rows = slice(c * GM_CHUNK, (c + 1) * GM_CHUNK)
        for g in range(GM_GROUPS):
            cols = slice(g * GM_GROUP_DIM, (g + 1) * GM_GROUP_DIM)
            mixed = jnp.dot(ws_ref[g], vb[rows, cols], preferred_element_type=F32) + bs_ref[:, g:g + 1]
            um_ref[rows, cols] = (u[rows, cols] * mixed).astype(BF16)
    y = jnp.dot(um_ref[...], wout_ref[...], preferred_element_type=F32)
    x1 = x + g1 * _rms(y, ng[1:2])
    x1_ref[...] = x1
    _ffn_prep(x1, ng, sh2, sc2, rw_ref, h_ref, lg_ref)


def _gmlp_mixer(x2d, mod, per_row, tt, tiles_per_batch, ng, win, b_in, ln_g, ln_b, ws, bs, wout, rw_t,
                emit_v):
    n = x2d.shape[0]
    out_shape = [jax.ShapeDtypeStruct((n, D), F32), jax.ShapeDtypeStruct((n, DP), jnp.int32),
                 jax.ShapeDtypeStruct((N_EXPERTS, n), F32)]
    out_specs = [pl.BlockSpec((tt, D), lambda i: (i, 0)), pl.BlockSpec((tt, DP), lambda i: (i, 0)),
                 pl.BlockSpec((N_EXPERTS, tt), lambda i: (0, i))]
    if emit_v:
        out_shape.append(jax.ShapeDtypeStruct((n, GM_HALF), F32))
        out_specs.append(pl.BlockSpec((tt, GM_HALF), lambda i: (i, 0)))
    return pl.pallas_call(
        functools.partial(_gmlp_body, n_chunks=tt // GM_CHUNK, emit_v=emit_v),
        grid=(n // tt,),
        in_specs=[pl.BlockSpec((tt, D), lambda i: (i, 0)),
                  _mod_spec(per_row, tt, tiles_per_batch),
                  _const_spec(ng.shape), _const_spec(win.shape), _const_spec(b_in.shape),
                  _const_spec(ln_g.shape), _const_spec(ln_b.shape), _const_spec(ws.shape),
                  _const_spec(bs.shape), _const_spec(wout.shape), _const_spec(rw_t.shape)],
        out_specs=out_specs,
        out_shape=out_shape,
        scratch_shapes=[pltpu.VMEM((tt, GM_HALF), BF16)],
        compiler_params=_cparams("parallel"),
        name="gmlp_mixer_rows" if per_row else "gmlp_mixer",
    )(x2d, mod, ng, win, b_in, ln_g, ln_b, ws, bs, wout, rw_t)


def _combine_body(x_ref, *refs):
    y_refs = refs[:TOP_K]
    w_ref, h_ref, mod_ref, ng_ref, swg_ref, swu_ref, swd_ref, o_ref = refs[TOP_K:]
    h_lo, h_hi = _unpack_rows(h_ref[...])
    h_lo = h_lo.astype(BF16)
    h_hi = h_hi.astype(BF16)

    def hdot(w_ref_):
        return (jnp.dot(h_lo, w_ref_[:DP, :], preferred_element_type=F32) +
                jnp.dot(h_hi, w_ref_[DP:, :], preferred_element_type=F32))

    hs = (_silu(hdot(swg_ref)) * hdot(swu_ref)).astype(BF16)
    y = jnp.dot(hs, swd_ref[...], preferred_element_type=F32)
    w = w_ref[...]
    acc_lo = jnp.zeros((x_ref.shape[0], DP), F32)
    acc_hi = jnp.zeros((x_ref.shape[0], DP), F32)
    for k in range(TOP_K):
        lo, hi = _unpack_rows(y_refs[k][...])
        acc_lo += w[:, k:k + 1] * lo
        acc_hi += w[:, k:k + 1] * hi
    y = y + jnp.concatenate([acc_lo, acc_hi], axis=1)
    g2 = mod_ref[:, 5 * D:6 * D]
    o_ref[...] = x_ref[...] + g2 * _rms(y, ng_ref[3:4, :])


def _combine(x2d, y8, wts_t, hp, blk0, mod, per_row, tt, tiles_per_batch, ng, swg, swu, swd):
    n = x2d.shape[0]
    y_specs = [pl.BlockSpec((None, tt, DP), functools.partial(lambda i, k: (k, i + blk0, 0), k=k))
               for k in range(TOP_K)]
    return pl.pallas_call(
        _combine_body,
        grid=(n // tt,),
        in_specs=[pl.BlockSpec((tt, D), lambda i: (i, 0))] + y_specs +
                 [pl.BlockSpec((tt, TOP_K), lambda i: (i + blk0, 0)),
                  pl.BlockSpec((tt, DP), lambda i: (i + blk0, 0)),
                  _mod_spec(per_row, tt, tiles_per_batch),
                  _const_spec(ng.shape), _const_spec(swg.shape), _const_spec(swu.shape),
                  _const_spec(swd.shape)],
        out_specs=pl.BlockSpec((tt, D), lambda i: (i, 0)),
        out_shape=jax.ShapeDtypeStruct((n, D), F32),
        compiler_params=_cparams("parallel"),
        name="combine_rows" if per_row else "combine",
    )(x2d, *([y8] * TOP_K), wts_t, hp, mod, ng, swg, swu, swd)


def _router_body(lg_ref, bias_ref, tri_ref, eid_ref, rank_ref, wts_ref, cnt_ref, carry_ref, *, n_real):
    step = pl.program_id(0)

    @pl.when(step == 0)
    def _():
        carry_ref[...] = jnp.zeros_like(carry_ref)

    lg = lg_ref[...]
    tn = lg.shape[1]
    real = (step * tn + lax.broadcasted_iota(jnp.int32, (1, tn), 1)) < n_real
    lg = jnp.where(real, lg, 0.0)
    scores = 1.0 / (1.0 + jnp.exp(-lg))
    sel = scores + bias_ref[...]
    neg = -jnp.inf
    sub8 = lax.broadcasted_iota(jnp.int32, (GROUP_SIZE, tn), 0)
    gsub = lax.broadcasted_iota(jnp.int32, (N_EXPERT_GROUPS, tn), 0)
    gs = jnp.zeros((N_EXPERT_GROUPS, tn), F32)
    for g in range(N_EXPERT_GROUPS):
        blk = sel[g * GROUP_SIZE:(g + 1) * GROUP_SIZE, :]
        m1 = jnp.max(blk, axis=0, keepdims=True)
        i1 = jnp.min(jnp.where(blk == m1, sub8, GROUP_SIZE), axis=0, keepdims=True)
        m2 = jnp.max(jnp.where(sub8 == i1, neg, blk), axis=0, keepdims=True)
        gs = jnp.where(gsub == g, m1 + m2, gs)
    gmask = jnp.zeros((N_EXPERT_GROUPS, tn), jnp.bool_)
    for _ in range(TOPK_GROUPS):
        m = jnp.max(gs, axis=0, keepdims=True)
        i = jnp.min(jnp.where(gs == m, gsub, N_EXPERT_GROUPS), axis=0, keepdims=True)
        hit = gsub == i
        gmask = jnp.logical_or(gmask, hit)
        gs = jnp.where(hit, neg, gs)
    gmaskf = gmask.astype(F32)
    blocks = []
    for g in range(N_EXPERT_GROUPS):
        keep = jnp.broadcast_to(gmaskf[g:g + 1, :], (GROUP_SIZE, tn)) > 0.5
        blocks.append(jnp.where(keep, sel[g * GROUP_SIZE:(g + 1) * GROUP_SIZE, :], neg))
    msel = jnp.concatenate(blocks, axis=0)
    esub = lax.broadcasted_iota(jnp.int32, (N_EXPERTS, tn), 0)
    chosen = jnp.zeros((N_EXPERTS, tn), jnp.bool_)
    picks = []
    for _ in range(TOP_K):
        m = jnp.max(msel, axis=0, keepdims=True)
        i = jnp.min(jnp.where(msel == m, esub, N_EXPERTS), axis=0, keepdims=True)
        hit = esub == i
        picks.append(i)
        chosen = jnp.logical_or(chosen, hit)
        msel = jnp.where(hit, neg, msel)
    w = jnp.where(chosen, scores, 0.0)
    w = w / jnp.sum(w, axis=0, keepdims=True) * ROUTED_SCALE
    counted = jnp.where(jnp.logical_and(chosen, real), 1.0, 0.0)
    incl = jnp.dot(counted.astype(BF16), tri_ref[...], preferred_element_type=F32)
    rank_full = carry_ref[:, 0:1] + incl - 1.0
    ksub = lax.broadcasted_iota(jnp.int32, (TOP_K, tn), 0)
    eid = jnp.zeros((TOP_K, tn), jnp.int32)
    rank = jnp.zeros((TOP_K, tn), F32)
    wts = jnp.zeros((TOP_K, tn), F32)
    for k in range(TOP_K):
        hit = esub == picks[k]
        eid = jnp.where(ksub == k, picks[k], eid)
        rank = jnp.where(ksub == k, jnp.sum(jnp.where(hit, rank_full, 0.0), axis=0, keepdims=True), rank)
        wts = jnp.where(ksub == k, jnp.sum(jnp.where(hit, w, 0.0), axis=0, keepdims=True), wts)
    eid_ref[...] = eid
    rank_ref[...] = rank.astype(jnp.int32)
    wts_ref[...] = wts
    carry = carry_ref[...] + incl[:, tn - 1:tn]
    carry_ref[...] = carry
    cnt_ref[...] = carry.astype(jnp.int32)


def _router(lg_t, bias, n_real):
    n = lg_t.shape[1]
    tn = ROUTER_TILE
    idx = jnp.arange(tn)
    tri = (idx[:, None] <= idx[None, :]).astype(BF16)
    kspec = pl.BlockSpec((TOP_K, tn), lambda i: (0, i))
    return pl.pallas_call(
        functools.partial(_router_body, n_real=n_real),
        grid=(n // tn,),
        in_specs=[pl.BlockSpec((N_EXPERTS, tn), lambda i: (0, i)), _const_spec((N_EXPERTS, 1)),
                  _const_spec((tn, tn))],
        out_specs=[kspec, kspec, kspec, _const_spec((N_EXPERTS, LANES))],
        out_shape=[jax.ShapeDtypeStruct((TOP_K, n), jnp.int32), jax.ShapeDtypeStruct((TOP_K, n), jnp.int32),
                   jax.ShapeDtypeStruct((TOP_K, n), F32), jax.ShapeDtypeStruct((N_EXPERTS, LANES), jnp.int32)],
        scratch_shapes=[pltpu.VMEM((N_EXPERTS, LANES), F32)],
        compiler_params=_cparams("arbitrary"),
        name="router",
    )(lg_t, bias.reshape(N_EXPERTS, 1), tri)


def _dest_body(off_ref, eid_ref, rank_ref, dest_ref, *, n_real, trash_row):
    eid = eid_ref[...]
    base = jnp.zeros(eid.shape, jnp.int32)
    for e in range(N_EXPERTS):
        base = jnp.where(eid == e, off_ref[e], base)
    real = lax.broadcasted_iota(jnp.int32, eid.shape, 1) < n_real
    dest_ref[...] = jnp.where(real, base + rank_ref[...], trash_row)


def _dest(off, eid, rank, n_real, trash_row):
    spec = pl.BlockSpec(eid.shape, lambda i, off_ref: (0, 0))
    return pl.pallas_call(
        functools.partial(_dest_body, n_real=n_real, trash_row=trash_row),
        grid_spec=pltpu.PrefetchScalarGridSpec(num_scalar_prefetch=1, grid=(1,), in_specs=[spec, spec],
                                               out_specs=spec),
        out_shape=jax.ShapeDtypeStruct(eid.shape, jnp.int32),
        compiler_params=_cparams("arbitrary"),
        name="dest_rows",
    )(off, eid, rank)


def _sc_mesh():
    return plsc.VectorSubcoreMesh(core_axis_name="core", subcore_axis_name="subcore")


def _sc_dispatch(hp, dest_w, p_alloc):
    n = hp.shape[0]
    w = dest_w.shape[2]

    @functools.partial(pl.kernel, out_type=jax.ShapeDtypeStruct((p_alloc, DP), jnp.int32), mesh=_sc_mesh(),
                       name="sc_dispatch")
    def run(hp_hbm, dest_hbm, xs_hbm):
        def body(x_vmem, i_vmem):
            for k in range(TOP_K):
                pltpu.sync_copy(x_vmem, xs_hbm.at[i_vmem.at[k]])

        pltpu.emit_pipeline(
            body,
            grid=(n // w,),
            in_specs=[pl.BlockSpec((w, DP), lambda i: (i, 0)),
                      pl.BlockSpec((None, TOP_K, w), lambda i: (i, 0, 0))],
            out_specs=[],
            core_axis_name=("core", "subcore"),
            dimension_semantics=(pltpu.PARALLEL,),
        )(hp_hbm, dest_hbm)

    return run(hp, dest_w)


def _sc_gather(ys, dest_g):
    g, _, w = dest_g.shape

    @functools.partial(pl.kernel, out_type=jax.ShapeDtypeStruct((g * w, DP), jnp.int32), mesh=_sc_mesh(),
                       name="sc_gather")
    def run(ys_hbm, dest_hbm, o_hbm):
        def body(i_vmem, o_vmem):
            pltpu.sync_copy(ys_hbm.at[i_vmem.at[0]], o_vmem)

        pltpu.emit_pipeline(
            body,
            grid=(g,),
            in_specs=[pl.BlockSpec((None, 1, w), lambda i: (i, 0, 0))],
            out_specs=[pl.BlockSpec((w, DP), lambda i: (i, 0))],
            core_axis_name=("core", "subcore"),
            dimension_semantics=(pltpu.PARALLEL,),
        )(dest_hbm, o_hbm)

    return run(ys, dest_g)


def _expert_body(te_ref, nused_ref, x_ref, wg_ref, wu_ref, wd_ref, o_ref, wg_s, wu_s, wd_s):
    i = pl.program_id(0)

    @pl.when(i < nused_ref[0])
    def _():
        fresh = jnp.logical_or(i == 0, te_ref[i] != te_ref[jnp.maximum(i - 1, 0)])

        @pl.when(fresh)
        def _():
            wg_s[...] = wg_ref[...].astype(BF16)
            wu_s[...] = wu_ref[...].astype(BF16)
            wd_s[...] = wd_ref[...].astype(BF16)

        lo, hi = _unpack_rows(x_ref[...])
        lo = lo.astype(BF16)
        hi = hi.astype(BF16)

        def xdot(w_s):
            return (jnp.dot(lo, w_s[:DP, :], preferred_element_type=F32) +
                    jnp.dot(hi, w_s[DP:, :], preferred_element_type=F32))

        a = (_silu(xdot(wg_s)) * xdot(wu_s)).astype(BF16)
        o_ref[...] = _pack_rows(jnp.dot(a, wd_s[...], preferred_element_type=F32))


def _experts(xs, tile_expert, n_used, wg, wu, wd):
    n_tiles = xs.shape[0] // EXPERT_TILE

    def row_map(i, te, nu):
        return (jnp.minimum(i, nu[0] - 1), 0)

    def w_map(i, te, nu):
        return (te[jnp.minimum(i, nu[0] - 1)], 0, 0)

    return pl.pallas_call(
        _expert_body,
        grid_spec=pltpu.PrefetchScalarGridSpec(
            num_scalar_prefetch=2, grid=(n_tiles,),
            in_specs=[pl.BlockSpec((EXPERT_TILE, DP), row_map),
                      pl.BlockSpec((None, D, EXPERT_DIM), w_map),
                      pl.BlockSpec((None, D, EXPERT_DIM), w_map),
                      pl.BlockSpec((None, EXPERT_DIM, D), w_map)],
            out_specs=pl.BlockSpec((EXPERT_TILE, DP), row_map),
            scratch_shapes=[pltpu.VMEM((D, EXPERT_DIM), BF16), pltpu.VMEM((D, EXPERT_DIM), BF16),
                            pltpu.VMEM((EXPERT_DIM, D), BF16)]),
        out_shape=jax.ShapeDtypeStruct(xs.shape, jnp.int32),
        compiler_params=_cparams("arbitrary"),
        name="experts",
    )(tile_expert, n_used, xs, wg, wu, wd)


def _log_sigmoid(z):
    return jnp.minimum(z, 0.0) - jnp.log(1.0 + jnp.exp(-jnp.abs(z)))


def _gla_gate(hb, wlr_ref, wgk_ref, bgk_ref):
    lr = jnp.dot(hb, wlr_ref[...], preferred_element_type=F32)
    z = _bdot(lr, wgk_ref[...]) + bgk_ref[...]
    return _log_sigmoid(z) * (1.0 / GLA_GATE_NORMALIZER)


def _split3(a):
    hi = a.astype(BF16)
    r1 = a - hi.astype(F32)
    mid = r1.astype(BF16)
    lo = (r1 - mid.astype(F32)).astype(BF16)
    return hi, mid, lo


def _gla_out(o_ref_val, go, gng):
    parts = []
    for hd in range(GLA_HEADS):
        cols = slice(hd * GLA_DV, (hd + 1) * GLA_DV)
        parts.append((_rms(o_ref_val[:, cols], gng) * _silu(go[:, cols])).astype(BF16))
    return jnp.concatenate(parts, axis=1)


def _gla_body(x_ref, mod_ref, ng_ref, wqkvg_ref, wlr_ref, wgk_ref, bgk_ref, tril_ref, gng_ref, wout_ref,
              rw_ref, x1_ref, h_ref, lg_ref, st_ref, st_scr, o_scr, *, tt):
    j = pl.program_id(1)

    @pl.when(j == 0)
    def _():
        st_scr[...] = jnp.zeros_like(st_scr)

    sh1, sc1, g1, sh2, sc2, _ = _mod_slices(mod_ref)
    ng = ng_ref[...]
    x = x_ref[...]
    hb = (_rms(x, ng[0:1]) * (1.0 + sc1) + sh1).astype(BF16)
    proj = jnp.dot(hb, wqkvg_ref[...], preferred_element_type=F32)
    q = proj[:, :GLA_DK_TOT] * (GLA_DK ** -0.5)
    k = proj[:, GLA_DK_TOT:2 * GLA_DK_TOT]
    v = proj[:, 2 * GLA_DK_TOT:2 * GLA_DK_TOT + GLA_DV_TOT].astype(BF16)
    go = proj[:, 2 * GLA_DK_TOT + GLA_DV_TOT:]
    log_a = _gla_gate(hb, wlr_ref, wgk_ref, bgk_ref)
    tril = tril_ref[...]
    b = sum(jnp.dot(tril, part, preferred_element_type=F32) for part in _split3(log_a))
    row = lax.broadcasted_iota(jnp.int32, (GLA_CHUNK, GLA_CHUNK), 0)
    col = lax.broadcasted_iota(jnp.int32, (GLA_CHUNK, GLA_CHUNK), 1)
    causal = row >= col
    for c in range(tt // GLA_CHUNK):
        rows = slice(c * GLA_CHUNK, (c + 1) * GLA_CHUNK)
        last = (c + 1) * GLA_CHUNK - 1
        for hd in range(GLA_HEADS):
            kc = slice(hd * GLA_DK, (hd + 1) * GLA_DK)
            vc = slice(hd * GLA_DV, (hd + 1) * GLA_DV)
            bb = b[rows, kc]
            b_last = b[last:last + 1, kc]
            q_dec = (q[rows, kc] * jnp.exp(bb)).astype(BF16)
            k_inv = (k[rows, kc] * jnp.exp(-bb)).astype(BF16)
            k_end = (k[rows, kc] * jnp.exp(b_last - bb)).astype(BF16)
            att = jnp.where(causal, _dot_nt(q_dec, k_inv), 0.0).astype(BF16)
            st = st_scr[hd]
            o = jnp.dot(att, v[rows, vc], preferred_element_type=F32) + _dot_nt(q_dec, st.astype(BF16))
            o_scr[rows, vc] = o
            d_st = lax.dot_general(v[rows, vc], k_end, (((0,), (0,)), ((), ())),
                                   preferred_element_type=F32)
            st_scr[hd] = st * jnp.exp(b_last) + d_st

    @pl.when(j == pl.num_programs(1) - 1)
    def _():
        for hd in range(GLA_HEADS):
            st_ref[hd] = st_scr[hd].T

    y = jnp.dot(_gla_out(o_scr[...], go, gng_ref[...]), wout_ref[...], preferred_element_type=F32)
    x1 = x + g1 * _rms(y, ng[1:2])
    x1_ref[...] = x1
    _ffn_prep(x1, ng, sh2, sc2, rw_ref, h_ref, lg_ref)


def _gla_mixer(x2d, mod3, batch, seq, ng, wqkvg, wlr, wgk, bgk, gng, wout, rw_t):
    tt = MIX_TILE
    tpb = seq // tt
    n = x2d.shape[0]
    idx = jnp.arange(tt)
    tril = ((idx[:, None] >= idx[None, :]) &
            (idx[:, None] // GLA_CHUNK == idx[None, :] // GLA_CHUNK)).astype(BF16)
    row_map = lambda b, j: (b * tpb + j, 0)
    consts = (ng, wqkvg, wlr, wgk, bgk, tril, gng, wout, rw_t)
    return pl.pallas_call(
        functools.partial(_gla_body, tt=tt),
        grid=(batch, tpb),
        in_specs=[pl.BlockSpec((tt, D), row_map),
                  pl.BlockSpec((None, 1, 6 * D), lambda b, j: (b, 0, 0))] +
                 [_const_spec(a.shape) for a in consts],
        out_specs=[pl.BlockSpec((tt, D), row_map), pl.BlockSpec((tt, DP), row_map),
                   pl.BlockSpec((N_EXPERTS, tt), lambda b, j: (0, b * tpb + j)),
                   pl.BlockSpec((None, GLA_HEADS, GLA_DK, GLA_DV), lambda b, j: (b, 0, 0, 0))],
        out_shape=[jax.ShapeDtypeStruct((n, D), F32), jax.ShapeDtypeStruct((n, DP), jnp.int32),
                   jax.ShapeDtypeStruct((N_EXPERTS, n), F32),
                   jax.ShapeDtypeStruct((batch, GLA_HEADS, GLA_DK, GLA_DV), F32)],
        scratch_shapes=[pltpu.VMEM((GLA_HEADS, GLA_DV, GLA_DK), F32),
                        pltpu.VMEM((tt, GLA_DV_TOT), F32)],
        compiler_params=_cparams("parallel", "arbitrary"),
        name="gla_mixer",
    )(x2d, mod3, *consts)


def _gla1_proj_body(x_ref, mod_ref, ng_ref, wqkvg_ref, wlr_ref, wgk_ref, bgk_ref,
                    q_ref, k_ref, v_ref, go_ref, dec_ref):
    sh1, sc1, _, _, _, _ = _mod_slices(mod_ref)
    ng = ng_ref[...]
    hb = (_rms(x_ref[...], ng[0:1]) * (1.0 + sc1) + sh1).astype(BF16)
    proj = jnp.dot(hb, wqkvg_ref[...], preferred_element_type=F32)
    q_ref[...] = proj[:, :GLA_DK_TOT] * (GLA_DK ** -0.5)
    k_ref[...] = proj[:, GLA_DK_TOT:2 * GLA_DK_TOT]
    v_ref[...] = proj[:, 2 * GLA_DK_TOT:2 * GLA_DK_TOT + GLA_DV_TOT]
    go_ref[...] = proj[:, 2 * GLA_DK_TOT + GLA_DV_TOT:]
    dec_ref[...] = jnp.exp(_gla_gate(hb, wlr_ref, wgk_ref, bgk_ref))


GLA1_TOK = 8


def _gla1_state_body(st_ref, qc_ref, kc_ref, dc_ref, v_ref, nst_ref, o_ref):
    v = v_ref[...]
    for i in range(GLA1_TOK):
        for hd in range(GLA_HEADS):
            vrow = v[i:i + 1, hd * GLA_DV:(hd + 1) * GLA_DV]
            s_new = dc_ref[hd][:, i:i + 1] * st_ref[i, hd] + kc_ref[hd][:, i:i + 1] * vrow
            nst_ref[i, hd] = s_new
            o_ref[i:i + 1, hd * GLA_DV:(hd + 1) * GLA_DV] = jnp.sum(
                qc_ref[hd][:, i:i + 1] * s_new, axis=0, keepdims=True)


def _gla1_out_body(x_ref, o_ref, go_ref, mod_ref, ng_ref, gng_ref, wout_ref, rw_ref, x1_ref, h_ref, lg_ref):
    _, _, g1, sh2, sc2, _ = _mod_slices(mod_ref)
    ng = ng_ref[...]
    y = jnp.dot(_gla_out(o_ref[...], go_ref[...], gng_ref[...]), wout_ref[...], preferred_element_type=F32)
    x1 = x_ref[...] + g1 * _rms(y, ng[1:2])
    x1_ref[...] = x1
    _ffn_prep(x1, ng, sh2, sc2, rw_ref, h_ref, lg_ref)


def _gla_mixer_one(x2d, mod2, state, ng, wqkvg, wlr, wgk, bgk, gng, wout, rw_t):
    n = x2d.shape[0]
    consts = (ng, wqkvg, wlr, wgk, bgk)
    q, k, v, go, dec = pl.pallas_call(
        _gla1_proj_body,
        in_specs=[_const_spec(a.shape) for a in (x2d, mod2) + consts],
        out_specs=[_const_spec((n, GLA_DK_TOT)), _const_spec((n, GLA_DK_TOT)), _const_spec((n, GLA_DV_TOT)),
                   _const_spec((n, GLA_DV_TOT)), _const_spec((n, GLA_DK_TOT))],
        out_shape=[jax.ShapeDtypeStruct((n, GLA_DK_TOT), F32), jax.ShapeDtypeStruct((n, GLA_DK_TOT), F32),
                   jax.ShapeDtypeStruct((n, GLA_DV_TOT), F32), jax.ShapeDtypeStruct((n, GLA_DV_TOT), F32),
                   jax.ShapeDtypeStruct((n, GLA_DK_TOT), F32)],
        grid=(1,),
        compiler_params=_cparams("arbitrary"),
        name="gla1_proj",
    )(x2d, mod2, *consts)

    def cols(a):
        return a.reshape(n // GLA1_TOK, GLA1_TOK, GLA_HEADS, GLA_DK).transpose(0, 2, 3, 1)

    col_spec = pl.BlockSpec((None, GLA_HEADS, GLA_DK, GLA1_TOK), lambda i: (i, 0, 0, 0))
    st_spec = pl.BlockSpec((GLA1_TOK, GLA_HEADS, GLA_DK, GLA_DV), lambda i: (i, 0, 0, 0))
    new_state, o = pl.pallas_call(
        _gla1_state_body,
        grid=(n // GLA1_TOK,),
        in_specs=[st_spec, col_spec, col_spec, col_spec, pl.BlockSpec((GLA1_TOK, GLA_DV_TOT), lambda i: (i, 0))],
        out_specs=[st_spec, pl.BlockSpec((GLA1_TOK, GLA_DV_TOT), lambda i: (i, 0))],
        out_shape=[jax.ShapeDtypeStruct(state.shape, F32), jax.ShapeDtypeStruct((n, GLA_DV_TOT), F32)],
        compiler_params=_cparams("parallel"),
        name="gla1_state",
    )(state, cols(q), cols(k), cols(dec), v)

    consts = (mod2, ng, gng, wout, rw_t)
    x1, h, lg = pl.pallas_call(
        _gla1_out_body,
        grid=(1,),
        in_specs=[_const_spec(a.shape) for a in (x2d, o, go) + consts],
        out_specs=[_const_spec((n, D)), _const_spec((n, DP)), _const_spec((N_EXPERTS, n))],
        out_shape=[jax.ShapeDtypeStruct((n, D), F32), jax.ShapeDtypeStruct((n, DP), jnp.int32),
                   jax.ShapeDtypeStruct((N_EXPERTS, n), F32)],
        compiler_params=_cparams("arbitrary"),
        name="gla1_out",
    )(x2d, o, go, *consts)
    return x1, h, lg, new_state


def _moe_routed(h_p, h_s, lg_p, lg_s, router_bias, wg, wu, wd):
    n = h_p.shape[0] + h_s.shape[0]
    n_pad = -(-n // TOKEN_PAD) * TOKEN_PAD
    h = jnp.pad(jnp.concatenate([h_p, h_s], axis=0), ((0, n_pad - n), (0, 0)))
    lg = jnp.pad(jnp.concatenate([lg_p, lg_s], axis=1), ((0, 0), (0, n_pad - n)))
    eid, rank, wts, counts = _router(lg, router_bias, n)
    tiles_per = (counts[:, 0] + EXPERT_TILE - 1) // EXPERT_TILE
    tile_end = jnp.cumsum(tiles_per)
    off = ((tile_end - tiles_per) * EXPERT_TILE).astype(jnp.int32)
    n_tiles = (TOP_K * n_pad) // EXPERT_TILE + N_EXPERTS
    p_alloc = n_tiles * EXPERT_TILE
    tile_expert = jnp.minimum(jnp.searchsorted(tile_end, jnp.arange(n_tiles), side="right"),
                              N_EXPERTS - 1).astype(jnp.int32)
    n_used = tile_end[-1:].astype(jnp.int32)
    dest = _dest(off, eid, rank, n, p_alloc - 1)
    dest_w = dest.reshape(TOP_K, n_pad // DISPATCH_W, DISPATCH_W).transpose(1, 0, 2)
    xs = _sc_dispatch(h, dest_w, p_alloc)
    ys = _experts(xs, tile_expert, n_used, wg, wu, wd)
    y8 = _sc_gather(ys, dest.reshape(TOP_K * n_pad // GATHER_W, 1, GATHER_W))
    return y8.reshape(TOP_K, n_pad, DP), wts.T, h


def kernel(x_prompt, x_sample, state_gla, c_prompt, c_sample, norm_g, ada_w, ada_b, gm_w_in, gm_b_in,
           gm_ln_g, gm_ln_b, gm_w_s, gm_b_s, gm_w_out, gla_w_in, gla_w_gk, gla_b_gk, gla_norm_g,
           gla_w_out, router_w, router_bias, exp_w_gate, exp_w_up, exp_w_down, sh_w_gate, sh_w_up,
           sh_w_down):
    batch, seq, _ = x_prompt.shape
    n_s = x_sample.shape[0]
    n_p = batch * seq
    tpb = seq // MIX_TILE
    xp = x_prompt.reshape(n_p, D)
    xs = x_sample.reshape(n_s, D)

    mod = _ada(jnp.concatenate([c_prompt, c_sample], axis=0), ada_w, ada_b)
    mod_p = [mod[i, :batch].reshape(batch, 1, 6 * D) for i in range(2)]
    mod_s = [mod[i, batch:] for i in range(2)]
    rw_t = [router_w[i].T for i in range(2)]

    ws_causal = jnp.tril(gm_w_s[0]).astype(BF16)
    bs_cols = gm_b_s[0].T
    eye = jnp.eye(GM_CHUNK, dtype=F32)
    ws_first = (gm_w_s[0][:, 0, 0][:, None, None] * eye).astype(BF16)
    bs_first = jnp.broadcast_to(gm_b_s[0][:, 0][None, :], (GM_CHUNK, GM_GROUPS))
    gm_args = (norm_g[0], gm_w_in[0].astype(BF16), gm_b_in[0].reshape(1, -1), gm_ln_g[0].reshape(1, -1),
               gm_ln_b[0].reshape(1, -1))
    wout0 = gm_w_out[0].astype(BF16)
    x1p, hp, lgp = _gmlp_mixer(xp, mod_p[0], False, MIX_TILE, tpb, *gm_args, ws_causal, bs_cols, wout0,
                               rw_t[0], emit_v=False)
    x1s, hs, lgs, v_rows = _gmlp_mixer(xs, mod_s[0], True, n_s, 1, *gm_args, ws_first, bs_first, wout0,
                                       rw_t[0], emit_v=True)
    shared = [(sh_w_gate[i].astype(BF16), sh_w_up[i].astype(BF16), sh_w_down[i].astype(BF16))
              for i in range(2)]
    moe0 = _moe_routed(hp, hs, lgp, lgs, router_bias[0], exp_w_gate[0], exp_w_up[0], exp_w_down[0])
    x2p = _combine(x1p, *moe0, 0, mod_p[0], False, MIX_TILE, tpb, norm_g[0], *shared[0])
    x2s = _combine(x1s, *moe0, n_p // n_s, mod_s[0], True, n_s, 1, norm_g[0], *shared[0])

    n_qkvg = 2 * GLA_DK_TOT + 2 * GLA_DV_TOT
    wqkvg = gla_w_in[0][:, :n_qkvg].astype(BF16)
    wlr = jnp.pad(gla_w_in[0][:, n_qkvg:], ((0, 0), (0, LANES - GLA_GATE_RANK))).astype(BF16)
    wgk = jnp.pad(gla_w_gk[0], ((0, LANES - GLA_GATE_RANK), (0, 0))).astype(BF16)
    gla_args = (norm_g[1], wqkvg, wlr, wgk, gla_b_gk[0].reshape(1, -1), gla_norm_g[0].reshape(1, -1),
                gla_w_out[0].astype(BF16), rw_t[1])
    x3p, hp, lgp, st_p = _gla_mixer(x2p, mod_p[1], batch, seq, *gla_args)
    x3s, hs, lgs, st_s = _gla_mixer_one(x2s, mod_s[1], state_gla[:, 0], *gla_args)
    moe1 = _moe_routed(hp, hs, lgp, lgs, router_bias[1], exp_w_gate[1], exp_w_up[1], exp_w_down[1])
    x4p = _combine(x3p, *moe1, 0, mod_p[1], False, MIX_TILE, tpb, norm_g[1], *shared[1])
    x4s = _combine(x3s, *moe1, n_p // n_s, mod_s[1], True, n_s, 1, norm_g[1], *shared[1])

    return (x4p.reshape(batch, seq, D), x4s.reshape(n_s, 1, D), st_p[:, None], st_s[:, None],
            v_rows.reshape(n_s, 1, 1, GM_HALF))
```

```python
import functools
import math

import jax
import jax.numpy as jnp
from jax import lax
from jax.experimental import pallas as pl
from jax.experimental.pallas import tpu as pltpu
from jax.experimental.pallas import tpu_sc as plsc

F32 = jnp.float32
BF16 = jnp.bfloat16

D = 1024
DP = D // 2
GM_CHUNK = 128
GM_HALF = 2 * D
GM_GROUPS = 8
GM_GROUP_DIM = GM_HALF // GM_GROUPS
GLA_HEADS = 4
GLA_DK = 128
GLA_DV = 256
GLA_DK_TOT = GLA_HEADS * GLA_DK
GLA_DV_TOT = GLA_HEADS * GLA_DV
GLA_GATE_RANK = 16
GLA_GATE_NORMALIZER = 16.0
GLA_CHUNK = 64
N_EXPERTS = 64
TOP_K = 8
N_EXPERT_GROUPS = 8
GROUP_SIZE = N_EXPERTS // N_EXPERT_GROUPS
TOPK_GROUPS = 4
EXPERT_DIM = D // 4
ROUTED_SCALE = 2.5
NORM_EPS = 1e-6
LN_EPS = 1e-5

LANES = 128
VMEM_LIMIT = 56 * 1024 * 1024

MIX_TILE = 256
ROUTER_TILE = 512
EXPERT_TILE = 512
SC_WORKERS = 32
DISPATCH_W = 32
GATHER_W = 64
TOKEN_PAD = SC_WORKERS * DISPATCH_W


def _cparams(*sem):
    return pltpu.CompilerParams(dimension_semantics=sem, vmem_limit_bytes=VMEM_LIMIT)


def _rms(x, g):
    return x * lax.rsqrt(jnp.mean(x * x, axis=-1, keepdims=True) + NORM_EPS) * g


def _silu(x):
    return x * (1.0 / (1.0 + jnp.exp(-x)))


def _gelu(x):
    return 0.5 * x * (1.0 + lax.erf(x * (1.0 / math.sqrt(2.0))))


def _bdot(a, b):
    return jnp.dot(a.astype(BF16), b.astype(BF16), preferred_element_type=F32)


def _dot_nt(a, b, precision=None):
    return lax.dot_general(a, b, (((1,), (1,)), ((), ())), preferred_element_type=F32,
                           precision=precision)


def _mod_slices(mod_ref):
    return [mod_ref[:, i * D:(i + 1) * D] for i in range(6)]


HI_HALF = -65536


def _pack_rows(x):
    lo = lax.bitcast_convert_type(x[:, :DP].astype(BF16).astype(F32), jnp.int32)
    hi = lax.bitcast_convert_type(x[:, DP:].astype(BF16).astype(F32), jnp.int32)
    return lax.shift_right_logical(lo, 16) | (hi & HI_HALF)


def _unpack_rows(p):
    lo = lax.bitcast_convert_type(lax.shift_left(p, 16), F32)
    hi = lax.bitcast_convert_type(p & HI_HALF, F32)
    return lo, hi


def _ffn_prep(x1, ng, sh2, sc2, rw_ref, h_ref, lg_ref):
    hffn = _rms(x1, ng[2:3]) * (1.0 + sc2) + sh2
    h_ref[...] = _pack_rows(hffn)
    lg_ref[...] = _dot_nt(rw_ref[...], hffn, precision=lax.Precision.HIGHEST)


def _ada_body(c_ref, w_ref, b_ref, o_ref):
    c = c_ref[...]
    o_ref[...] = _bdot(_silu(c), w_ref[...]) + b_ref[...]


def _ada(c, ada_w, ada_b):
    n = c.shape[0]
    depth = ada_w.shape[0]
    tn = 1536
    return pl.pallas_call(
        _ada_body,
        grid=(depth, 6 * D // tn),
        in_specs=[pl.BlockSpec((n, D), lambda l, j: (0, 0)),
                  pl.BlockSpec((None, D, tn), lambda l, j: (l, 0, j)),
                  pl.BlockSpec((None, 1, tn), lambda l, j: (l, 0, j))],
        out_specs=pl.BlockSpec((None, n, tn), lambda l, j: (l, 0, j)),
        out_shape=jax.ShapeDtypeStruct((depth, n, 6 * D), F32),
        compiler_params=_cparams("parallel", "parallel"),
        name="ada_mod",
    )(c, ada_w, ada_b.reshape(depth, 1, 6 * D))


def _mod_spec(per_row, tt, tiles_per_batch):
    if per_row:
        return pl.BlockSpec((tt, 6 * D), lambda i: (i, 0))
    return pl.BlockSpec((None, 1, 6 * D), lambda i: (i // tiles_per_batch, 0, 0))


def _const_spec(shape):
    zeros = (0,) * len(shape)
    return pl.BlockSpec(shape, lambda *_: zeros)


def _gmlp_body(x_ref, mod_ref, ng_ref, win_ref, bin_ref, lng_ref, lnb_ref, ws_ref, bs_ref, wout_ref,
               rw_ref, x1_ref, h_ref, lg_ref, *rest, n_chunks, emit_v):
    if emit_v:
        v_ref, um_ref = rest
    else:
        (um_ref,) = rest
    sh1, sc1, g1, sh2, sc2, _ = _mod_slices(mod_ref)
    ng = ng_ref[...]
    x = x_ref[...]
    hb = (_rms(x, ng[0:1]) * (1.0 + sc1) + sh1).astype(BF16)
    u = _gelu(jnp.dot(hb, win_ref[:, :GM_HALF], preferred_element_type=F32) + bin_ref[:, :GM_HALF])
    v = _gelu(jnp.dot(hb, win_ref[:, GM_HALF:], preferred_element_type=F32) + bin_ref[:, GM_HALF:])
    mu = jnp.mean(v, axis=-1, keepdims=True)
    vc = v - mu
    var = jnp.mean(vc * vc, axis=-1, keepdims=True)
    v = vc * lax.rsqrt(var + LN_EPS) * lng_ref[...] + lnb_ref[...]
    if emit_v:
        v_ref[...] = v
    vb = v.astype(BF16)
    for c in range(n_chunks):
        rows = slice(c * GM_CHUNK, (c + 1) * GM_CHUNK)
        for g in range(GM_GROUPS):
            cols = slice(g * GM_GROUP_DIM, (g + 1) * GM_GROUP_DIM)
            mixed = jnp.dot(ws_ref[g], vb[rows, cols], preferred_element_type=F32) + bs_ref[:, g:g + 1]
            um_ref[rows, cols] = (u[rows, cols] * mixed).astype(BF16)
    y = jnp.dot(um_ref[...], wout_ref[...], preferred_element_type=F32)
    x1 = x + g1 * _rms(y, ng[1:2])
    x1_ref[...] = x1
    _ffn_prep(x1, ng, sh2, sc2, rw_ref, h_ref, lg_ref)


def _gmlp_mixer(x2d, mod, per_row, tt, tiles_per_batch, ng, win, b_in, ln_g, ln_b, ws, bs, wout, rw_t,
                emit_v):
    n = x2d.shape[0]
    out_shape = [jax.ShapeDtypeStruct((n, D), F32), jax.ShapeDtypeStruct((n, DP), jnp.int32),
                 jax.ShapeDtypeStruct((N_EXPERTS, n), F32)]
    out_specs = [pl.BlockSpec((tt, D), lambda i: (i, 0)), pl.BlockSpec((tt, DP), lambda i: (i, 0)),
                 pl.BlockSpec((N_EXPERTS, tt), lambda i: (0, i))]
    if emit_v:
        out_shape.append(jax.ShapeDtypeStruct((n, GM_HALF), F32))
        out_specs.append(pl.BlockSpec((tt, GM_HALF), lambda i: (i, 0)))
    return pl.pallas_call(
        functools.partial(_gmlp_body, n_chunks=tt // GM_CHUNK, emit_v=emit_v),
        grid=(n // tt,),
        in_specs=[pl.BlockSpec((tt, D), lambda i: (i, 0)),
                  _mod_spec(per_row, tt, tiles_per_batch),
                  _const_spec(ng.shape), _const_spec(win.shape), _const_spec(b_in.shape),
                  _const_spec(ln_g.shape), _const_spec(ln_b.shape), _const_spec(ws.shape),
                  _const_spec(bs.shape), _const_spec(wout.shape), _const_spec(rw_t.shape)],
        out_specs=out_specs,
        out_shape=out_shape,
        scratch_shapes=[pltpu.VMEM((tt, GM_HALF), BF16)],
        compiler_params=_cparams("parallel"),
        name="gmlp_mixer_rows" if per_row else "gmlp_mixer",
    )(x2d, mod, ng, win, b_in, ln_g, ln_b, ws, bs, wout, rw_t)


def _combine_body(x_ref, *refs):
    y_refs = refs[:TOP_K]
    w_ref, h_ref, mod_ref, ng_ref, swg_ref, swu_ref, swd_ref, o_ref = refs[TOP_K:]
    h_lo, h_hi = _unpack_rows(h_ref[...])
    h_lo = h_lo.astype(BF16)
    h_hi = h_hi.astype(BF16)

    def hdot(w_ref_):
        return (jnp.dot(h_lo, w_ref_[:DP, :], preferred_element_type=F32) +
                jnp.dot(h_hi, w_ref_[DP:, :], preferred_element_type=F32))

    hs = (_silu(hdot(swg_ref)) * hdot(swu_ref)).astype(BF16)
    y = jnp.dot(hs, swd_ref[...], preferred_element_type=F32)
    w = w_ref[...]
    acc_lo = jnp.zeros((x_ref.shape[0], DP), F32)
    acc_hi = jnp.zeros((x_ref.shape[0], DP), F32)
    for k in range(TOP_K):
        lo, hi = _unpack_rows(y_refs[k][...])
        acc_lo += w[:, k:k + 1] * lo
        acc_hi += w[:, k:k + 1] * hi
    y = y + jnp.concatenate([acc_lo, acc_hi], axis=1)
    g2 = mod_ref[:, 5 * D:6 * D]
    o_ref[...] = x_ref[...] + g2 * _rms(y, ng_ref[3:4, :])


def _combine(x2d, y8, wts_t, hp, blk0, mod, per_row, tt, tiles_per_batch, ng, swg, swu, swd):
    n = x2d.shape[0]
    y_specs = [pl.BlockSpec((None, tt, DP), functools.partial(lambda i, k: (k, i + blk0, 0), k=k))
               for k in range(TOP_K)]
    return pl.pallas_call(
        _combine_body,
        grid=(n // tt,),
        in_specs=[pl.BlockSpec((tt, D), lambda i: (i, 0))] + y_specs +
                 [pl.BlockSpec((tt, TOP_K), lambda i: (i + blk0, 0)),
                  pl.BlockSpec((tt, DP), lambda i: (i + blk0, 0)),
                  _mod_spec(per_row, tt, tiles_per_batch),
                  _const_spec(ng.shape), _const_spec(swg.shape), _const_spec(swu.shape),
                  _const_spec(swd.shape)],
        out_specs=pl.BlockSpec((tt, D), lambda i: (i, 0)),
        out_shape=jax.ShapeDtypeStruct((n, D), F32),
        compiler_params=_cparams("parallel"),
        name="combine_rows" if per_row else "combine",
    )(x2d, *([y8] * TOP_K), wts_t, hp, mod, ng, swg, swu, swd)


def _router_body(lg_ref, bias_ref, tri_ref, eid_ref, rank_ref, wts_ref, cnt_ref, carry_ref, *, n_real):
    step = pl.program_id(0)

    @pl.when(step == 0)
    def _():
        carry_ref[...] = jnp.zeros_like(carry_ref)

    lg = lg_ref[...]
    tn = lg.shape[1]
    real = (step * tn + lax.broadcasted_iota(jnp.int32, (1, tn), 1)) < n_real
    lg = jnp.where(real, lg, 0.0)
    scores = 1.0 / (1.0 + jnp.exp(-lg))
    sel = scores + bias_ref[...]
    neg = -jnp.inf
    sub8 = lax.broadcasted_iota(jnp.int32, (GROUP_SIZE, tn), 0)
    gsub = lax.broadcasted_iota(jnp.int32, (N_EXPERT_GROUPS, tn), 0)
    gs = jnp.zeros((N_EXPERT_GROUPS, tn), F32)
    for g in range(N_EXPERT_GROUPS):
        blk = sel[g * GROUP_SIZE:(g + 1) * GROUP_SIZE, :]
        m1 = jnp.max(blk, axis=0, keepdims=True)
        i1 = jnp.min(jnp.where(blk == m1, sub8, GROUP_SIZE), axis=0, keepdims=True)
        m2 = jnp.max(jnp.where(sub8 == i1, neg, blk), axis=0, keepdims=True)
        gs = jnp.where(gsub == g, m1 + m2, gs)
    gmask = jnp.zeros((N_EXPERT_GROUPS, tn), jnp.bool_)
    for _ in range(TOPK_GROUPS):
        m = jnp.max(gs, axis=0, keepdims=True)
        i = jnp.min(jnp.where(gs == m, gsub, N_EXPERT_GROUPS), axis=0, keepdims=True)
        hit = gsub == i
        gmask = jnp.logical_or(gmask, hit)
        gs = jnp.where(hit, neg, gs)
    gmaskf = gmask.astype(F32)
    blocks = []
    for g in range(N_EXPERT_GROUPS):
        keep = jnp.broadcast_to(gmaskf[g:g + 1, :], (GROUP_SIZE, tn)) > 0.5
        blocks.append(jnp.where(keep, sel[g * GROUP_SIZE:(g + 1) * GROUP_SIZE, :], neg))
    msel = jnp.concatenate(blocks, axis=0)
    esub = lax.broadcasted_iota(jnp.int32, (N_EXPERTS, tn), 0)
    chosen = jnp.zeros((N_EXPERTS, tn), jnp.bool_)
    picks = []
    for _ in range(TOP_K):
        m = jnp.max(msel, axis=0, keepdims=True)
        i = jnp.min(jnp.where(msel == m, esub, N_EXPERTS), axis=0, keepdims=True)
        hit = esub == i
        picks.append(i)
        chosen = jnp.logical_or(chosen, hit)
        msel = jnp.where(hit, neg, msel)
    w = jnp.where(chosen, scores, 0.0)
    w = w / jnp.sum(w, axis=0, keepdims=True) * ROUTED_SCALE
    counted = jnp.where(jnp.logical_and(chosen, real), 1.0, 0.0)
    incl = jnp.dot(counted.astype(BF16), tri_ref[...], preferred_element_type=F32)
    rank_full = carry_ref[:, 0:1] + incl - 1.0
    ksub = lax.broadcasted_iota(jnp.int32, (TOP_K, tn), 0)
    eid = jnp.zeros((TOP_K, tn), jnp.int32)
    rank = jnp.zeros((TOP_K, tn), F32)
    wts = jnp.zeros((TOP_K, tn), F32)
    for k in range(TOP_K):
        hit = esub == picks[k]
        eid = jnp.where(ksub == k, picks[k], eid)
        rank = jnp.where(ksub == k, jnp.sum(jnp.where(hit, rank_full, 0.0), axis=0, keepdims=True), rank)
        wts = jnp.where(ksub == k, jnp.sum(jnp.where(hit, w, 0.0), axis=0, keepdims=True), wts)
    eid_ref[...] = eid
    rank_ref[...] = rank.astype(jnp.int32)
    wts_ref[...] = wts
    carry = carry_ref[...] + incl[:, tn - 1:tn]
    carry_ref[...] = carry
    cnt_ref[...] = carry.astype(jnp.int32)


def _router(lg_t, bias, n_real):
    n = lg_t.shape[1]
    tn = ROUTER_TILE
    idx = jnp.arange(tn)
    tri = (idx[:, None] <= idx[None, :]).astype(BF16)
    kspec = pl.BlockSpec((TOP_K, tn), lambda i: (0, i))
    return pl.pallas_call(
        functools.partial(_router_body, n_real=n_real),
        grid=(n // tn,),
        in_specs=[pl.BlockSpec((N_EXPERTS, tn), lambda i: (0, i)), _const_spec((N_EXPERTS, 1)),
                  _const_spec((tn, tn))],
        out_specs=[kspec, kspec, kspec, _const_spec((N_EXPERTS, LANES))],
        out_shape=[jax.ShapeDtypeStruct((TOP_K, n), jnp.int32), jax.ShapeDtypeStruct((TOP_K, n), jnp.int32),
                   jax.ShapeDtypeStruct((TOP_K, n), F32), jax.ShapeDtypeStruct((N_EXPERTS, LANES), jnp.int32)],
        scratch_shapes=[pltpu.VMEM((N_EXPERTS, LANES), F32)],
        compiler_params=_cparams("arbitrary"),
        name="router",
    )(lg_t, bias.reshape(N_EXPERTS, 1), tri)


def _dest_body(off_ref, eid_ref, rank_ref, dest_ref, *, n_real, last_row):
    eid = eid_ref[...]
    base = jnp.zeros(eid.shape, jnp.int32)
    for e in range(N_EXPERTS):
        base = jnp.where(eid == e, off_ref[e], base)
    tok = lax.broadcasted_iota(jnp.int32, eid.shape, 1)
    slot = lax.broadcasted_iota(jnp.int32, eid.shape, 0)
    unused = last_row - ((tok - n_real) * TOP_K + slot)
    dest_ref[...] = jnp.where(tok < n_real, base + rank_ref[...], unused)


def _dest(off, eid, rank, n_real, last_row):
    spec = pl.BlockSpec(eid.shape, lambda i, off_ref: (0, 0))
    return pl.pallas_call(
        functools.partial(_dest_body, n_real=n_real, last_row=last_row),
        grid_spec=pltpu.PrefetchScalarGridSpec(num_scalar_prefetch=1, grid=(1,), in_specs=[spec, spec],
                                               out_specs=spec),
        out_shape=jax.ShapeDtypeStruct(eid.shape, jnp.int32),
        compiler_params=_cparams("arbitrary"),
        name="dest_rows",
    )(off, eid, rank)


def _sc_mesh():
    return plsc.VectorSubcoreMesh(core_axis_name="core", subcore_axis_name="subcore")


def _sc_dispatch(hp, dest_w, p_alloc):
    n = hp.shape[0]
    w = dest_w.shape[2]

    @functools.partial(pl.kernel, out_type=jax.ShapeDtypeStruct((p_alloc, DP), jnp.int32), mesh=_sc_mesh(),
                       name="sc_dispatch")
    def run(hp_hbm, dest_hbm, xs_hbm):
        def body(x_vmem, i_vmem):
            for k in range(TOP_K):
                pltpu.sync_copy(x_vmem, xs_hbm.at[i_vmem.at[k]])

        pltpu.emit_pipeline(
            body,
            grid=(n // w,),
            in_specs=[pl.BlockSpec((w, DP), lambda i: (i, 0)),
                      pl.BlockSpec((None, TOP_K, w), lambda i: (i, 0, 0))],
            out_specs=[],
            core_axis_name=("core", "subcore"),
            dimension_semantics=(pltpu.PARALLEL,),
        )(hp_hbm, dest_hbm)

    return run(hp, dest_w)


def _sc_gather(ys, dest_g):
    g, _, w = dest_g.shape

    @functools.partial(pl.kernel, out_type=jax.ShapeDtypeStruct((g * w, DP), jnp.int32), mesh=_sc_mesh(),
                       name="sc_gather")
    def run(ys_hbm, dest_hbm, o_hbm):
        def body(i_vmem, o_vmem):
            pltpu.sync_copy(ys_hbm.at[i_vmem.at[0]], o_vmem)

        pltpu.emit_pipeline(
            body,
            grid=(g,),
            in_specs=[pl.BlockSpec((None, 1, w), lambda i: (i, 0, 0))],
            out_specs=[pl.BlockSpec((w, DP), lambda i: (i, 0))],
            core_axis_name=("core", "subcore"),
            dimension_semantics=(pltpu.PARALLEL,),
        )(dest_hbm, o_hbm)

    return run(ys, dest_g)


def _expert_body(te_ref, nused_ref, x_ref, wg_ref, wu_ref, wd_ref, o_ref, wg_s, wu_s, wd_s):
    i = pl.program_id(0)

    @pl.when(i < nused_ref[0])
    def _():
        fresh = jnp.logical_or(i == 0, te_ref[i] != te_ref[jnp.maximum(i - 1, 0)])

        @pl.when(fresh)
        def _():
            wg_s[...] = wg_ref[...].astype(BF16)
            wu_s[...] = wu_ref[...].astype(BF16)
            wd_s[...] = wd_ref[...].astype(BF16)

        lo, hi = _unpack_rows(x_ref[...])
        lo = lo.astype(BF16)
        hi = hi.astype(BF16)

        def xdot(w_s):
            return (jnp.dot(lo, w_s[:DP, :], preferred_element_type=F32) +
                    jnp.dot(hi, w_s[DP:, :], preferred_element_type=F32))

        a = (_silu(xdot(wg_s)) * xdot(wu_s)).astype(BF16)
        o_ref[...] = _pack_rows(jnp.dot(a, wd_s[...], preferred_element_type=F32))


def _experts(xs, tile_expert, n_used, layer, wg, wu, wd):
    n_tiles = xs.shape[0] // EXPERT_TILE

    def row_map(i, te, nu):
        return (jnp.minimum(i, nu[0] - 1), 0)

    def w_map(i, te, nu):
        return (layer, te[jnp.minimum(i, nu[0] - 1)], 0, 0)

    return pl.pallas_call(
        _expert_body,
        grid_spec=pltpu.PrefetchScalarGridSpec(
            num_scalar_prefetch=2, grid=(n_tiles,),
            in_specs=[pl.BlockSpec((EXPERT_TILE, DP), row_map),
                      pl.BlockSpec((None, None, D, EXPERT_DIM), w_map),
                      pl.BlockSpec((None, None, D, EXPERT_DIM), w_map),
                      pl.BlockSpec((None, None, EXPERT_DIM, D), w_map)],
            out_specs=pl.BlockSpec((EXPERT_TILE, DP), row_map),
            scratch_shapes=[pltpu.VMEM((D, EXPERT_DIM), BF16), pltpu.VMEM((D, EXPERT_DIM), BF16),
                            pltpu.VMEM((EXPERT_DIM, D), BF16)]),
        out_shape=jax.ShapeDtypeStruct(xs.shape, jnp.int32),
        compiler_params=_cparams("arbitrary"),
        name="experts",
    )(tile_expert, n_used, xs, wg, wu, wd)


def _log_sigmoid(z):
    return jnp.minimum(z, 0.0) - jnp.log(1.0 + jnp.exp(-jnp.abs(z)))


def _gla_gate(hb, wlr_ref, wgk_ref, bgk_ref):
    lr = jnp.dot(hb, wlr_ref[...], preferred_element_type=F32)
    z = _bdot(lr, wgk_ref[...]) + bgk_ref[...]
    return _log_sigmoid(z) * (1.0 / GLA_GATE_NORMALIZER)


def _split3(a):
    hi = a.astype(BF16)
    r1 = a - hi.astype(F32)
    mid = r1.astype(BF16)
    lo = (r1 - mid.astype(F32)).astype(BF16)
    return hi, mid, lo


def _gla_out(o_ref_val, go, gng):
    parts = []
    for hd in range(GLA_HEADS):
        cols = slice(hd * GLA_DV, (hd + 1) * GLA_DV)
        parts.append((_rms(o_ref_val[:, cols], gng) * _silu(go[:, cols])).astype(BF16))
    return jnp.concatenate(parts, axis=1)


def _gla_body(x_ref, mod_ref, ng_ref, wqkvg_ref, wlr_ref, wgk_ref, bgk_ref, tril_ref, gng_ref, wout_ref,
              rw_ref, x1_ref, h_ref, lg_ref, st_ref, st_scr, o_scr, *, tt):
    j = pl.program_id(1)

    @pl.when(j == 0)
    def _():
        st_scr[...] = jnp.zeros_like(st_scr)

    sh1, sc1, g1, sh2, sc2, _ = _mod_slices(mod_ref)
    ng = ng_ref[...]
    x = x_ref[...]
    hb = (_rms(x, ng[0:1]) * (1.0 + sc1) + sh1).astype(BF16)
    proj = jnp.dot(hb, wqkvg_ref[...], preferred_element_type=F32)
    q = proj[:, :GLA_DK_TOT] * (GLA_DK ** -0.5)
    k = proj[:, GLA_DK_TOT:2 * GLA_DK_TOT]
    v = proj[:, 2 * GLA_DK_TOT:2 * GLA_DK_TOT + GLA_DV_TOT].astype(BF16)
    go = proj[:, 2 * GLA_DK_TOT + GLA_DV_TOT:]
    log_a = _gla_gate(hb, wlr_ref, wgk_ref, bgk_ref)
    tril = tril_ref[...]
    b = sum(jnp.dot(tril, part, preferred_element_type=F32) for part in _split3(log_a))
    row = lax.broadcasted_iota(jnp.int32, (GLA_CHUNK, GLA_CHUNK), 0)
    col = lax.broadcasted_iota(jnp.int32, (GLA_CHUNK, GLA_CHUNK), 1)
    causal = row >= col
    for c in range(tt // GLA_CHUNK):
        rows = slice(c * GLA_CHUNK, (c + 1) * GLA_CHUNK)
        last = (c + 1) * GLA_CHUNK - 1
        for hd in range(GLA_HEADS):
            kc = slice(hd * GLA_DK, (hd + 1) * GLA_DK)
            vc = slice(hd * GLA_DV, (hd + 1) * GLA_DV)
            bb = b[rows, kc]
            b_last = b[last:last + 1, kc]
            q_dec = (q[rows, kc] * jnp.exp(bb)).astype(BF16)
            k_inv = (k[rows, kc] * jnp.exp(-bb)).astype(BF16)
            k_end = (k[rows, kc] * jnp.exp(b_last - bb)).astype(BF16)
            att = jnp.where(causal, _dot_nt(q_dec, k_inv), 0.0).astype(BF16)
            st = st_scr[hd]
            o = jnp.dot(att, v[rows, vc], preferred_element_type=F32) + _dot_nt(q_dec, st.astype(BF16))
            o_scr[rows, vc] = o
            d_st = lax.dot_general(v[rows, vc], k_end, (((0,), (0,)), ((), ())),
                                   preferred_element_type=F32)
            st_scr[hd] = st * jnp.exp(b_last) + d_st

    @pl.when(j == pl.num_programs(1) - 1)
    def _():
        for hd in range(GLA_HEADS):
            st_ref[hd] = st_scr[hd].T

    y = jnp.dot(_gla_out(o_scr[...], go, gng_ref[...]), wout_ref[...], preferred_element_type=F32)
    x1 = x + g1 * _rms(y, ng[1:2])
    x1_ref[...] = x1
    _ffn_prep(x1, ng, sh2, sc2, rw_ref, h_ref, lg_ref)


def _gla_mixer(x2d, mod3, batch, seq, ng, wqkvg, wlr, wgk, bgk, gng, wout, rw_t):
    tt = MIX_TILE
    tpb = seq // tt
    n = x2d.shape[0]
    idx = jnp.arange(tt)
    tril = ((idx[:, None] >= idx[None, :]) &
            (idx[:, None] // GLA_CHUNK == idx[None, :] // GLA_CHUNK)).astype(BF16)
    row_map = lambda b, j: (b * tpb + j, 0)
    consts = (ng, wqkvg, wlr, wgk, bgk, tril, gng, wout, rw_t)
    return pl.pallas_call(
        functools.partial(_gla_body, tt=tt),
        grid=(batch, tpb),
        in_specs=[pl.BlockSpec((tt, D), row_map),
                  pl.BlockSpec((None, 1, 6 * D), lambda b, j: (b, 0, 0))] +
                 [_const_spec(a.shape) for a in consts],
        out_specs=[pl.BlockSpec((tt, D), row_map), pl.BlockSpec((tt, DP), row_map),
                   pl.BlockSpec((N_EXPERTS, tt), lambda b, j: (0, b * tpb + j)),
                   pl.BlockSpec((None, GLA_HEADS, GLA_DK, GLA_DV), lambda b, j: (b, 0, 0, 0))],
        out_shape=[jax.ShapeDtypeStruct((n, D), F32), jax.ShapeDtypeStruct((n, DP), jnp.int32),
                   jax.ShapeDtypeStruct((N_EXPERTS, n), F32),
                   jax.ShapeDtypeStruct((batch, GLA_HEADS, GLA_DK, GLA_DV), F32)],
        scratch_shapes=[pltpu.VMEM((GLA_HEADS, GLA_DV, GLA_DK), F32),
                        pltpu.VMEM((tt, GLA_DV_TOT), F32)],
        compiler_params=_cparams("parallel", "arbitrary"),
        name="gla_mixer",
    )(x2d, mod3, *consts)


def _gla1_proj_body(x_ref, mod_ref, ng_ref, wqkvg_ref, wlr_ref, wgk_ref, bgk_ref,
                    q_ref, k_ref, v_ref, go_ref, dec_ref):
    sh1, sc1, _, _, _, _ = _mod_slices(mod_ref)
    ng = ng_ref[...]
    hb = (_rms(x_ref[...], ng[0:1]) * (1.0 + sc1) + sh1).astype(BF16)
    proj = jnp.dot(hb, wqkvg_ref[...], preferred_element_type=F32)
    q_ref[...] = proj[:, :GLA_DK_TOT] * (GLA_DK ** -0.5)
    k_ref[...] = proj[:, GLA_DK_TOT:2 * GLA_DK_TOT]
    v_ref[...] = proj[:, 2 * GLA_DK_TOT:2 * GLA_DK_TOT + GLA_DV_TOT]
    go_ref[...] = proj[:, 2 * GLA_DK_TOT + GLA_DV_TOT:]
    dec_ref[...] = jnp.exp(_gla_gate(hb, wlr_ref, wgk_ref, bgk_ref))


GLA1_TOK = 8


def _gla1_state_body(st_ref, qc_ref, kc_ref, dc_ref, v_ref, nst_ref, o_ref):
    v = v_ref[...]
    for i in range(GLA1_TOK):
        for hd in range(GLA_HEADS):
            vrow = v[i:i + 1, hd * GLA_DV:(hd + 1) * GLA_DV]
            s_new = dc_ref[hd][:, i:i + 1] * st_ref[i, hd] + kc_ref[hd][:, i:i + 1] * vrow
            nst_ref[i, hd] = s_new
            o_ref[i:i + 1, hd * GLA_DV:(hd + 1) * GLA_DV] = jnp.sum(
                qc_ref[hd][:, i:i + 1] * s_new, axis=0, keepdims=True)


def _gla1_out_body(x_ref, o_ref, go_ref, mod_ref, ng_ref, gng_ref, wout_ref, rw_ref, x1_ref, h_ref, lg_ref):
    _, _, g1, sh2, sc2, _ = _mod_slices(mod_ref)
    ng = ng_ref[...]
    y = jnp.dot(_gla_out(o_ref[...], go_ref[...], gng_ref[...]), wout_ref[...], preferred_element_type=F32)
    x1 = x_ref[...] + g1 * _rms(y, ng[1:2])
    x1_ref[...] = x1
    _ffn_prep(x1, ng, sh2, sc2, rw_ref, h_ref, lg_ref)


def _gla_mixer_one(x2d, mod2, state, ng, wqkvg, wlr, wgk, bgk, gng, wout, rw_t):
    n = x2d.shape[0]
    consts = (ng, wqkvg, wlr, wgk, bgk)
    q, k, v, go, dec = pl.pallas_call(
        _gla1_proj_body,
        in_specs=[_const_spec(a.shape) for a in (x2d, mod2) + consts],
        out_specs=[_const_spec((n, GLA_DK_TOT)), _const_spec((n, GLA_DK_TOT)), _const_spec((n, GLA_DV_TOT)),
                   _const_spec((n, GLA_DV_TOT)), _const_spec((n, GLA_DK_TOT))],
        out_shape=[jax.ShapeDtypeStruct((n, GLA_DK_TOT), F32), jax.ShapeDtypeStruct((n, GLA_DK_TOT), F32),
                   jax.ShapeDtypeStruct((n, GLA_DV_TOT), F32), jax.ShapeDtypeStruct((n, GLA_DV_TOT), F32),
                   jax.ShapeDtypeStruct((n, GLA_DK_TOT), F32)],
        grid=(1,),
        compiler_params=_cparams("arbitrary"),
        name="gla1_proj",
    )(x2d, mod2, *consts)

    def cols(a):
        return a.reshape(n // GLA1_TOK, GLA1_TOK, GLA_HEADS, GLA_DK).transpose(0, 2, 3, 1)

    col_spec = pl.BlockSpec((None, GLA_HEADS, GLA_DK, GLA1_TOK), lambda i: (i, 0, 0, 0))
    st_spec = pl.BlockSpec((GLA1_TOK, GLA_HEADS, GLA_DK, GLA_DV), lambda i: (i, 0, 0, 0))
    new_state, o = pl.pallas_call(
        _gla1_state_body,
        grid=(n // GLA1_TOK,),
        in_specs=[st_spec, col_spec, col_spec, col_spec, pl.BlockSpec((GLA1_TOK, GLA_DV_TOT), lambda i: (i, 0))],
        out_specs=[st_spec, pl.BlockSpec((GLA1_TOK, GLA_DV_TOT), lambda i: (i, 0))],
        out_shape=[jax.ShapeDtypeStruct(state.shape, F32), jax.ShapeDtypeStruct((n, GLA_DV_TOT), F32)],
        compiler_params=_cparams("parallel"),
        name="gla1_state",
    )(state, cols(q), cols(k), cols(dec), v)

    consts = (mod2, ng, gng, wout, rw_t)
    x1, h, lg = pl.pallas_call(
        _gla1_out_body,
        grid=(1,),
        in_specs=[_const_spec(a.shape) for a in (x2d, o, go) + consts],
        out_specs=[_const_spec((n, D)), _const_spec((n, DP)), _const_spec((N_EXPERTS, n))],
        out_shape=[jax.ShapeDtypeStruct((n, D), F32), jax.ShapeDtypeStruct((n, DP), jnp.int32),
                   jax.ShapeDtypeStruct((N_EXPERTS, n), F32)],
        compiler_params=_cparams("arbitrary"),
        name="gla1_out",
    )(x2d, o, go, *consts)
    return x1, h, lg, new_state


def _moe_routed(h_p, h_s, lg_p, lg_s, router_bias, layer, wg, wu, wd):
    n = h_p.shape[0] + h_s.shape[0]
    n_pad = -(-n // TOKEN_PAD) * TOKEN_PAD
    h = jnp.pad(jnp.concatenate([h_p, h_s], axis=0), ((0, n_pad - n), (0, 0)))
    lg = jnp.pad(jnp.concatenate([lg_p, lg_s], axis=1), ((0, 0), (0, n_pad - n)))
    eid, rank, wts, counts = _router(lg, router_bias, n)
    tiles_per = (counts[:, 0] + EXPERT_TILE - 1) // EXPERT_TILE
    tile_end = jnp.cumsum(tiles_per)
    off = ((tile_end - tiles_per) * EXPERT_TILE).astype(jnp.int32)
    n_tiles = (TOP_K * n_pad) // EXPERT_TILE + N_EXPERTS
    p_alloc = n_tiles * EXPERT_TILE
    tile_expert = jnp.minimum(jnp.sum(jnp.arange(n_tiles)[:, None] >= tile_end[None, :], axis=1),
                              N_EXPERTS - 1).astype(jnp.int32)
    n_used = tile_end[-1:].astype(jnp.int32)
    dest = _dest(off, eid, rank, n, p_alloc - 1)
    dest_w = dest.reshape(TOP_K, n_pad // DISPATCH_W, DISPATCH_W).transpose(1, 0, 2)
    xs = _sc_dispatch(h, dest_w, p_alloc)
    ys = _experts(xs, tile_expert, n_used, layer, wg, wu, wd)
    y8 = _sc_gather(ys, dest.reshape(TOP_K * n_pad // GATHER_W, 1, GATHER_W))
    return y8.reshape(TOP_K, n_pad, DP), wts.T, h


def kernel(x_prompt, x_sample, state_gla, c_prompt, c_sample, norm_g, ada_w, ada_b, gm_w_in, gm_b_in,
           gm_ln_g, gm_ln_b, gm_w_s, gm_b_s, gm_w_out, gla_w_in, gla_w_gk, gla_b_gk, gla_norm_g,
           gla_w_out, router_w, router_bias, exp_w_gate, exp_w_up, exp_w_down, sh_w_gate, sh_w_up,
           sh_w_down):
    batch, seq, _ = x_prompt.shape
    n_s = x_sample.shape[0]
    n_p = batch * seq
    tpb = seq // MIX_TILE
    xp = x_prompt.reshape(n_p, D)
    xs = x_sample.reshape(n_s, D)

    mod = _ada(jnp.concatenate([c_prompt, c_sample], axis=0), ada_w, ada_b)
    mod_p = [mod[i, :batch].reshape(batch, 1, 6 * D) for i in range(2)]
    mod_s = [mod[i, batch:] for i in range(2)]
    rw_t = [router_w[i].T for i in range(2)]

    ws_causal = jnp.tril(gm_w_s[0]).astype(BF16)
    bs_cols = gm_b_s[0].T
    eye = jnp.eye(GM_CHUNK, dtype=F32)
    ws_first = (gm_w_s[0][:, 0, 0][:, None, None] * eye).astype(BF16)
    bs_first = jnp.broadcast_to(gm_b_s[0][:, 0][None, :], (GM_CHUNK, GM_GROUPS))
    gm_args = (norm_g[0], gm_w_in[0].astype(BF16), gm_b_in[0].reshape(1, -1), gm_ln_g[0].reshape(1, -1),
               gm_ln_b[0].reshape(1, -1))
    wout0 = gm_w_out[0].astype(BF16)
    x1p, hp, lgp = _gmlp_mixer(xp, mod_p[0], False, MIX_TILE, tpb, *gm_args, ws_causal, bs_cols, wout0,
                               rw_t[0], emit_v=False)
    x1s, hs, lgs, v_rows = _gmlp_mixer(xs, mod_s[0], True, n_s, 1, *gm_args, ws_first, bs_first, wout0,
                                       rw_t[0], emit_v=True)
    shared = [(sh_w_gate[i].astype(BF16), sh_w_up[i].astype(BF16), sh_w_down[i].astype(BF16))
              for i in range(2)]
    moe0 = _moe_routed(hp, hs, lgp, lgs, router_bias[0], 0, exp_w_gate, exp_w_up, exp_w_down)
    x2p = _combine(x1p, *moe0, 0, mod_p[0], False, MIX_TILE, tpb, norm_g[0], *shared[0])
    x2s = _combine(x1s, *moe0, n_p // n_s, mod_s[0], True, n_s, 1, norm_g[0], *shared[0])

    n_qkvg = 2 * GLA_DK_TOT + 2 * GLA_DV_TOT
    wqkvg = gla_w_in[0][:, :n_qkvg].astype(BF16)
    wlr = jnp.pad(gla_w_in[0][:, n_qkvg:], ((0, 0), (0, LANES - GLA_GATE_RANK))).astype(BF16)
    wgk = jnp.pad(gla_w_gk[0], ((0, LANES - GLA_GATE_RANK), (0, 0))).astype(BF16)
    gla_args = (norm_g[1], wqkvg, wlr, wgk, gla_b_gk[0].reshape(1, -1), gla_norm_g[0].reshape(1, -1),
                gla_w_out[0].astype(BF16), rw_t[1])
    x3p, hp, lgp, st_p = _gla_mixer(x2p, mod_p[1], batch, seq, *gla_args)
    x3s, hs, lgs, st_s = _gla_mixer_one(x2s, mod_s[1], state_gla[:, 0], *gla_args)
    moe1 = _moe_routed(hp, hs, lgp, lgs, router_bias[1], 1, exp_w_gate, exp_w_up, exp_w_down)
    x4p = _combine(x3p, *moe1, 0, mod_p[1], False, MIX_TILE, tpb, norm_g[1], *shared[1])
    x4s = _combine(x3s, *moe1, n_p // n_s, mod_s[1], True, n_s, 1, norm_g[1], *shared[1])

    return (x4p.reshape(batch, seq, D), x4s.reshape(n_s, 1, D), st_p[:, None], st_s[:, None],
            v_rows.reshape(n_s, 1, 1, GM_HALF))
```

```python
import functools
import math

import jax
import jax.numpy as jnp
from jax import lax
from jax.experimental import pallas as pl
from jax.experimental.pallas import tpu as pltpu
from jax.experimental.pallas import tpu_sc as plsc

F32 = jnp.float32
BF16 = jnp.bfloat16

D = 1024
DP = D // 2
GM_CHUNK = 128
GM_HALF = 2 * D
GM_GROUPS = 8
GM_GROUP_DIM = GM_HALF // GM_GROUPS
GLA_HEADS = 4
GLA_DK = 128
GLA_DV = 256
GLA_DK_TOT = GLA_HEADS * GLA_DK
GLA_DV_TOT = GLA_HEADS * GLA_DV
GLA_GATE_RANK = 16
GLA_GATE_NORMALIZER = 16.0
GLA_CHUNK = 64
N_EXPERTS = 64
TOP_K = 8
N_EXPERT_GROUPS = 8
GROUP_SIZE = N_EXPERTS // N_EXPERT_GROUPS
TOPK_GROUPS = 4
EXPERT_DIM = D // 4
ROUTED_SCALE = 2.5
NORM_EPS = 1e-6
LN_EPS = 1e-5

LANES = 128
VMEM_LIMIT = 56 * 1024 * 1024

MIX_TILE = 256
ROUTER_TILE = 512
EXPERT_TILE = 512
SC_WORKERS = 32
DISPATCH_W = 32
GATHER_W = 64
TOKEN_PAD = SC_WORKERS * DISPATCH_W


def _cparams(*sem):
    return pltpu.CompilerParams(dimension_semantics=sem, vmem_limit_bytes=VMEM_LIMIT)


def _rms(x, g):
    return x * lax.rsqrt(jnp.mean(x * x, axis=-1, keepdims=True) + NORM_EPS) * g


def _silu(x):
    return x * (1.0 / (1.0 + jnp.exp(-x)))


def _gelu(x):
    return 0.5 * x * (1.0 + lax.erf(x * (1.0 / math.sqrt(2.0))))


def _bdot(a, b):
    return jnp.dot(a.astype(BF16), b.astype(BF16), preferred_element_type=F32)


def _dot_nt(a, b, precision=None):
    return lax.dot_general(a, b, (((1,), (1,)), ((), ())), preferred_element_type=F32,
                           precision=precision)


def _mod_slices(mod_ref):
    return [mod_ref[:, i * D:(i + 1) * D] for i in range(6)]


HI_HALF = -65536


def _pack_rows(x):
    lo = lax.bitcast_convert_type(x[:, :DP].astype(BF16).astype(F32), jnp.int32)
    hi = lax.bitcast_convert_type(x[:, DP:].astype(BF16).astype(F32), jnp.int32)
    return lax.shift_right_logical(lo, 16) | (hi & HI_HALF)


def _unpack_rows(p):
    lo = lax.bitcast_convert_type(lax.shift_left(p, 16), F32)
    hi = lax.bitcast_convert_type(p & HI_HALF, F32)
    return lo, hi


def _ffn_prep(x1, ng, sh2, sc2, rw_ref, h_ref, lg_ref):
    hffn = _rms(x1, ng[2:3]) * (1.0 + sc2) + sh2
    h_ref[...] = _pack_rows(hffn)
    lg_ref[...] = _dot_nt(rw_ref[...], hffn, precision=lax.Precision.HIGHEST)


def _ada_body(c_ref, w_ref, b_ref, o_ref):
    c = c_ref[...]
    o_ref[...] = _bdot(_silu(c), w_ref[...]) + b_ref[...]


def _ada(c, ada_w, ada_b):
    n = c.shape[0]
    depth = ada_w.shape[0]
    tn = 1536
    return pl.pallas_call(
        _ada_body,
        grid=(depth, 6 * D // tn),
        in_specs=[pl.BlockSpec((n, D), lambda l, j: (0, 0)),
                  pl.BlockSpec((None, D, tn), lambda l, j: (l, 0, j)),
                  pl.BlockSpec((None, 1, tn), lambda l, j: (l, 0, j))],
        out_specs=pl.BlockSpec((None, n, tn), lambda l, j: (l, 0, j)),
        out_shape=jax.ShapeDtypeStruct((depth, n, 6 * D), F32),
        compiler_params=_cparams("parallel", "parallel"),
        name="ada_mod",
    )(c, ada_w, ada_b.reshape(depth, 1, 6 * D))


def _mod_spec(per_row, tt, tiles_per_batch):
    if per_row:
        return pl.BlockSpec((tt, 6 * D), lambda i: (i, 0))
    return pl.BlockSpec((None, 1, 6 * D), lambda i: (i // tiles_per_batch, 0, 0))


def _const_spec(shape):
    zeros = (0,) * len(shape)
    return pl.BlockSpec(shape, lambda *_: zeros)


def _gmlp_body(x_ref, mod_ref, ng_ref, win_ref, bin_ref, lng_ref, lnb_ref, ws_ref, bs_ref, wout_ref,
               rw_ref, x1_ref, h_ref, lg_ref, *rest, n_chunks, emit_v):
    if emit_v:
        v_ref, um_ref = rest
    else:
        (um_ref,) = rest
    sh1, sc1, g1, sh2, sc2, _ = _mod_slices(mod_ref)
    ng = ng_ref[...]
    x = x_ref[...]
    hb = (_rms(x, ng[0:1]) * (1.0 + sc1) + sh1).astype(BF16)
    u = _gelu(jnp.dot(hb, win_ref[:, :GM_HALF], preferred_element_type=F32) + bin_ref[:, :GM_HALF])
    v = _gelu(jnp.dot(hb, win_ref[:, GM_HALF:], preferred_element_type=F32) + bin_ref[:, GM_HALF:])
    mu = jnp.mean(v, axis=-1, keepdims=True)
    vc = v - mu
    var = jnp.mean(vc * vc, axis=-1, keepdims=True)
    v = vc * lax.rsqrt(var + LN_EPS) * lng_ref[...] + lnb_ref[...]
    if emit_v:
        v_ref[...] = v
    vb = v.astype(BF16)
    for c in range(n_chunks):
        rows = slice(c * GM_CHUNK, (c + 1) * GM_CHUNK)
        for g in range(GM_GROUPS):
            cols = slice(g * GM_GROUP_DIM, (g + 1) * GM_GROUP_DIM)
            mixed = jnp.dot(ws_ref[g], vb[rows, cols], preferred_element_type=F32) + bs_ref[:, g:g + 1]
            um_ref[rows, cols] = (u[rows, cols] * mixed).astype(BF16)
    y = jnp.dot(um_ref[...], wout_ref[...], preferred_element_type=F32)
    x1 = x + g1 * _rms(y, ng[1:2])
    x1_ref[...] = x1
    _ffn_prep(x1, ng, sh2, sc2, rw_ref, h_ref, lg_ref)


def _gmlp_mixer(x2d, blk0, n, mod, per_row, tt, tiles_per_batch, ng, win, b_in, ln_g, ln_b, ws, bs, wout,
                rw_t, emit_v):
    out_shape = [jax.ShapeDtypeStruct((n, D), F32), jax.ShapeDtypeStruct((n, DP), jnp.int32),
                 jax.ShapeDtypeStruct((N_EXPERTS, n), F32)]
    out_specs = [pl.BlockSpec((tt, D), lambda i: (i, 0)), pl.BlockSpec((tt, DP), lambda i: (i, 0)),
                 pl.BlockSpec((N_EXPERTS, tt), lambda i: (0, i))]
    if emit_v:
        out_shape.append(jax.ShapeDtypeStruct((n, GM_HALF), F32))
        out_specs.append(pl.BlockSpec((tt, GM_HALF), lambda i: (i, 0)))
    return pl.pallas_call(
        functools.partial(_gmlp_body, n_chunks=tt // GM_CHUNK, emit_v=emit_v),
        grid=(n // tt,),
        in_specs=[pl.BlockSpec((tt, D), lambda i: (i + blk0, 0)),
                  _mod_spec(per_row, tt, tiles_per_batch),
                  _const_spec(ng.shape), _const_spec(win.shape), _const_spec(b_in.shape),
                  _const_spec(ln_g.shape), _const_spec(ln_b.shape), _const_spec(ws.shape),
                  _const_spec(bs.shape), _const_spec(wout.shape), _const_spec(rw_t.shape)],
        out_specs=out_specs,
        out_shape=out_shape,
        scratch_shapes=[pltpu.VMEM((tt, GM_HALF), BF16)],
        compiler_params=_cparams("parallel"),
        name="gmlp_mixer_rows" if per_row else "gmlp_mixer",
    )(x2d, mod, ng, win, b_in, ln_g, ln_b, ws, bs, wout, rw_t)


def _combine_body(x_ref, *refs):
    y_refs = refs[:TOP_K]
    w_ref, h_ref, mod_ref, ng_ref, swg_ref, swu_ref, swd_ref = refs[TOP_K:TOP_K + 7]
    o_ref = refs[-1]
    h_lo, h_hi = _unpack_rows(h_ref[...])
    h_lo = h_lo.astype(BF16)
    h_hi = h_hi.astype(BF16)

    def hdot(w_ref_):
        return (jnp.dot(h_lo, w_ref_[:DP, :], preferred_element_type=F32) +
                jnp.dot(h_hi, w_ref_[DP:, :], preferred_element_type=F32))

    hs = (_silu(hdot(swg_ref)) * hdot(swu_ref)).astype(BF16)
    y = jnp.dot(hs, swd_ref[...], preferred_element_type=F32)
    w = w_ref[...]
    acc_lo = jnp.zeros((x_ref.shape[0], DP), F32)
    acc_hi = jnp.zeros((x_ref.shape[0], DP), F32)
    for k in range(TOP_K):
        lo, hi = _unpack_rows(y_refs[k][...])
        acc_lo += w[:, k:k + 1] * lo
        acc_hi += w[:, k:k + 1] * hi
    y = y + jnp.concatenate([acc_lo, acc_hi], axis=1)
    g2 = mod_ref[:, 5 * D:6 * D]
    o_ref[...] = x_ref[...] + g2 * _rms(y, ng_ref[3:4, :])


def _combine(x2d, y8, wts_t, hp, blk0, mod, per_row, tt, tiles_per_batch, ng, swg, swu, swd,
             out_rows=None, out_blk0=0, out_buf=None):
    n = x2d.shape[0]
    y_specs = [pl.BlockSpec((None, tt, DP), functools.partial(lambda i, k: (k, i + blk0, 0), k=k))
               for k in range(TOP_K)]
    in_specs = ([pl.BlockSpec((tt, D), lambda i: (i, 0))] + y_specs +
                [pl.BlockSpec((tt, TOP_K), lambda i: (i + blk0, 0)),
                 pl.BlockSpec((tt, DP), lambda i: (i + blk0, 0)),
                 _mod_spec(per_row, tt, tiles_per_batch),
                 _const_spec(ng.shape), _const_spec(swg.shape), _const_spec(swu.shape),
                 _const_spec(swd.shape)])
    args = [x2d] + [y8] * TOP_K + [wts_t, hp, mod, ng, swg, swu, swd]
    aliases = {}
    if out_buf is not None:
        in_specs.append(pl.BlockSpec(memory_space=pl.ANY))
        aliases = {len(args): 0}
        args.append(out_buf)
    return pl.pallas_call(
        _combine_body,
        grid=(n // tt,),
        in_specs=in_specs,
        out_specs=pl.BlockSpec((tt, D), lambda i: (i + out_blk0, 0)),
        out_shape=jax.ShapeDtypeStruct((out_rows or n, D), F32),
        input_output_aliases=aliases,
        compiler_params=_cparams("parallel"),
        name="combine_rows" if per_row else "combine",
    )(*args)


def _router_body(lg_ref, bias_ref, tri_ref, eid_ref, rank_ref, wts_ref, cnt_ref, carry_ref, *, n_real):
    step = pl.program_id(0)

    @pl.when(step == 0)
    def _():
        carry_ref[...] = jnp.zeros_like(carry_ref)

    lg = lg_ref[...]
    tn = lg.shape[1]
    real = (step * tn + lax.broadcasted_iota(jnp.int32, (1, tn), 1)) < n_real
    lg = jnp.where(real, lg, 0.0)
    scores = 1.0 / (1.0 + jnp.exp(-lg))
    sel = scores + bias_ref[...]
    neg = -jnp.inf
    sub8 = lax.broadcasted_iota(jnp.int32, (GROUP_SIZE, tn), 0)
    gsub = lax.broadcasted_iota(jnp.int32, (N_EXPERT_GROUPS, tn), 0)
    gs = jnp.zeros((N_EXPERT_GROUPS, tn), F32)
    for g in range(N_EXPERT_GROUPS):
        blk = sel[g * GROUP_SIZE:(g + 1) * GROUP_SIZE, :]
        m1 = jnp.max(blk, axis=0, keepdims=True)
        i1 = jnp.min(jnp.where(blk == m1, sub8, GROUP_SIZE), axis=0, keepdims=True)
        m2 = jnp.max(jnp.where(sub8 == i1, neg, blk), axis=0, keepdims=True)
        gs = jnp.where(gsub == g, m1 + m2, gs)
    gmask = jnp.zeros((N_EXPERT_GROUPS, tn), jnp.bool_)
    for _ in range(TOPK_GROUPS):
        m = jnp.max(gs, axis=0, keepdims=True)
        i = jnp.min(jnp.where(gs == m, gsub, N_EXPERT_GROUPS), axis=0, keepdims=True)
        hit = gsub == i
        gmask = jnp.logical_or(gmask, hit)
        gs = jnp.where(hit, neg, gs)
    gmaskf = gmask.astype(F32)
    blocks = []
    for g in range(N_EXPERT_GROUPS):
        keep = jnp.broadcast_to(gmaskf[g:g + 1, :], (GROUP_SIZE, tn)) > 0.5
        blocks.append(jnp.where(keep, sel[g * GROUP_SIZE:(g + 1) * GROUP_SIZE, :], neg))
    msel = jnp.concatenate(blocks, axis=0)
    esub = lax.broadcasted_iota(jnp.int32, (N_EXPERTS, tn), 0)
    chosen = jnp.zeros((N_EXPERTS, tn), jnp.bool_)
    picks = []
    for _ in range(TOP_K):
        m = jnp.max(msel, axis=0, keepdims=True)
        i = jnp.min(jnp.where(msel == m, esub, N_EXPERTS), axis=0, keepdims=True)
        hit = esub == i
        picks.append(i)
        chosen = jnp.logical_or(chosen, hit)
        msel = jnp.where(hit, neg, msel)
    w = jnp.where(chosen, scores, 0.0)
    w = w / jnp.sum(w, axis=0, keepdims=True) * ROUTED_SCALE
    counted = jnp.where(jnp.logical_and(chosen, real), 1.0, 0.0)
    incl = jnp.dot(counted.astype(BF16), tri_ref[...], preferred_element_type=F32)
    rank_full = carry_ref[:, 0:1] + incl - 1.0
    ksub = lax.broadcasted_iota(jnp.int32, (TOP_K, tn), 0)
    eid = jnp.zeros((TOP_K, tn), jnp.int32)
    rank = jnp.zeros((TOP_K, tn), F32)
    wts = jnp.zeros((TOP_K, tn), F32)
    for k in range(TOP_K):
        hit = esub == picks[k]
        eid = jnp.where(ksub == k, picks[k], eid)
        rank = jnp.where(ksub == k, jnp.sum(jnp.where(hit, rank_full, 0.0), axis=0, keepdims=True), rank)
        wts = jnp.where(ksub == k, jnp.sum(jnp.where(hit, w, 0.0), axis=0, keepdims=True), wts)
    eid_ref[...] = eid
    rank_ref[...] = rank.astype(jnp.int32)
    wts_ref[...] = wts
    carry = carry_ref[...] + incl[:, tn - 1:tn]
    carry_ref[...] = carry
    cnt_ref[...] = carry.astype(jnp.int32)


def _router(lg_t, bias, n_real):
    n = lg_t.shape[1]
    tn = ROUTER_TILE
    idx = jnp.arange(tn)
    tri = (idx[:, None] <= idx[None, :]).astype(BF16)
    kspec = pl.BlockSpec((TOP_K, tn), lambda i: (0, i))
    return pl.pallas_call(
        functools.partial(_router_body, n_real=n_real),
        grid=(n // tn,),
        in_specs=[pl.BlockSpec((N_EXPERTS, tn), lambda i: (0, i)), _const_spec((N_EXPERTS, 1)),
                  _const_spec((tn, tn))],
        out_specs=[kspec, kspec, kspec, _const_spec((N_EXPERTS, LANES))],
        out_shape=[jax.ShapeDtypeStruct((TOP_K, n), jnp.int32), jax.ShapeDtypeStruct((TOP_K, n), jnp.int32),
                   jax.ShapeDtypeStruct((TOP_K, n), F32), jax.ShapeDtypeStruct((N_EXPERTS, LANES), jnp.int32)],
        scratch_shapes=[pltpu.VMEM((N_EXPERTS, LANES), F32)],
        compiler_params=_cparams("arbitrary"),
        name="router",
    )(lg_t, bias.reshape(N_EXPERTS, 1), tri)


def _dest_body(off_ref, eid_ref, rank_ref, dest_ref, *, n_real, last_row):
    eid = eid_ref[...]
    base = jnp.zeros(eid.shape, jnp.int32)
    for e in range(N_EXPERTS):
        base = jnp.where(eid == e, off_ref[e], base)
    tok = lax.broadcasted_iota(jnp.int32, eid.shape, 1)
    slot = lax.broadcasted_iota(jnp.int32, eid.shape, 0)
    unused = last_row - ((tok - n_real) * TOP_K + slot)
    dest_ref[...] = jnp.where(tok < n_real, base + rank_ref[...], unused)


def _dest(off, eid, rank, n_real, last_row):
    spec = pl.BlockSpec(eid.shape, lambda i, off_ref: (0, 0))
    return pl.pallas_call(
        functools.partial(_dest_body, n_real=n_real, last_row=last_row),
        grid_spec=pltpu.PrefetchScalarGridSpec(num_scalar_prefetch=1, grid=(1,), in_specs=[spec, spec],
                                               out_specs=spec),
        out_shape=jax.ShapeDtypeStruct(eid.shape, jnp.int32),
        compiler_params=_cparams("arbitrary"),
        name="dest_rows",
    )(off, eid, rank)


def _sc_mesh():
    return plsc.VectorSubcoreMesh(core_axis_name="core", subcore_axis_name="subcore")


def _sc_dispatch(hp, dest_w, p_alloc):
    n = hp.shape[0]
    w = dest_w.shape[2]

    @functools.partial(pl.kernel, out_type=jax.ShapeDtypeStruct((p_alloc, DP), jnp.int32), mesh=_sc_mesh(),
                       name="sc_dispatch")
    def run(hp_hbm, dest_hbm, xs_hbm):
        def body(x_vmem, i_vmem):
            for k in range(TOP_K):
                pltpu.sync_copy(x_vmem, xs_hbm.at[i_vmem.at[k]])

        pltpu.emit_pipeline(
            body,
            grid=(n // w,),
            in_specs=[pl.BlockSpec((w, DP), lambda i: (i, 0)),
                      pl.BlockSpec((None, TOP_K, w), lambda i: (i, 0, 0))],
            out_specs=[],
            core_axis_name=("core", "subcore"),
            dimension_semantics=(pltpu.PARALLEL,),
        )(hp_hbm, dest_hbm)

    return run(hp, dest_w)


def _sc_gather(ys, dest_g):
    g, _, w = dest_g.shape

    @functools.partial(pl.kernel, out_type=jax.ShapeDtypeStruct((g * w, DP), jnp.int32), mesh=_sc_mesh(),
                       name="sc_gather")
    def run(ys_hbm, dest_hbm, o_hbm):
        def body(i_vmem, o_vmem):
            pltpu.sync_copy(ys_hbm.at[i_vmem.at[0]], o_vmem)

        pltpu.emit_pipeline(
            body,
            grid=(g,),
            in_specs=[pl.BlockSpec((None, 1, w), lambda i: (i, 0, 0))],
            out_specs=[pl.BlockSpec((w, DP), lambda i: (i, 0))],
            core_axis_name=("core", "subcore"),
            dimension_semantics=(pltpu.PARALLEL,),
        )(dest_hbm, o_hbm)

    return run(ys, dest_g)


def _expert_body(te_ref, nused_ref, x_ref, wg_ref, wu_ref, wd_ref, o_ref, wg_s, wu_s, wd_s):
    i = pl.program_id(0)

    @pl.when(i < nused_ref[0])
    def _():
        fresh = jnp.logical_or(i == 0, te_ref[i] != te_ref[jnp.maximum(i - 1, 0)])

        @pl.when(fresh)
        def _():
            wg_s[...] = wg_ref[...].astype(BF16)
            wu_s[...] = wu_ref[...].astype(BF16)
            wd_s[...] = wd_ref[...].astype(BF16)

        lo, hi = _unpack_rows(x_ref[...])
        lo = lo.astype(BF16)
        hi = hi.astype(BF16)

        def xdot(w_s):
            return (jnp.dot(lo, w_s[:DP, :], preferred_element_type=F32) +
                    jnp.dot(hi, w_s[DP:, :], preferred_element_type=F32))

        a = (_silu(xdot(wg_s)) * xdot(wu_s)).astype(BF16)
        o_ref[...] = _pack_rows(jnp.dot(a, wd_s[...], preferred_element_type=F32))


def _experts(xs, tile_expert, n_used, layer, wg, wu, wd):
    n_tiles = xs.shape[0] // EXPERT_TILE

    def row_map(i, te, nu):
        return (jnp.minimum(i, nu[0] - 1), 0)

    def w_map(i, te, nu):
        return (layer, te[jnp.minimum(i, nu[0] - 1)], 0, 0)

    return pl.pallas_call(
        _expert_body,
        grid_spec=pltpu.PrefetchScalarGridSpec(
            num_scalar_prefetch=2, grid=(n_tiles,),
            in_specs=[pl.BlockSpec((EXPERT_TILE, DP), row_map),
                      pl.BlockSpec((None, None, D, EXPERT_DIM), w_map),
                      pl.BlockSpec((None, None, D, EXPERT_DIM), w_map),
                      pl.BlockSpec((None, None, EXPERT_DIM, D), w_map)],
            out_specs=pl.BlockSpec((EXPERT_TILE, DP), row_map),
            scratch_shapes=[pltpu.VMEM((D, EXPERT_DIM), BF16), pltpu.VMEM((D, EXPERT_DIM), BF16),
                            pltpu.VMEM((EXPERT_DIM, D), BF16)]),
        out_shape=jax.ShapeDtypeStruct(xs.shape, jnp.int32),
        compiler_params=_cparams("arbitrary"),
        name="experts",
    )(tile_expert, n_used, xs, wg, wu, wd)


def _log_sigmoid(z):
    return jnp.minimum(z, 0.0) - jnp.log(1.0 + jnp.exp(-jnp.abs(z)))


def _gla_gate(hb, wlr_ref, wgk_ref, bgk_ref):
    lr = jnp.dot(hb, wlr_ref[...], preferred_element_type=F32)
    z = _bdot(lr, wgk_ref[...]) + bgk_ref[...]
    return _log_sigmoid(z) * (1.0 / GLA_GATE_NORMALIZER)


def _split3(a):
    hi = a.astype(BF16)
    r1 = a - hi.astype(F32)
    mid = r1.astype(BF16)
    lo = (r1 - mid.astype(F32)).astype(BF16)
    return hi, mid, lo


def _gla_out(o_ref_val, go, gng):
    parts = []
    for hd in range(GLA_HEADS):
        cols = slice(hd * GLA_DV, (hd + 1) * GLA_DV)
        parts.append((_rms(o_ref_val[:, cols], gng) * _silu(go[:, cols])).astype(BF16))
    return jnp.concatenate(parts, axis=1)


def _gla_body(x_ref, mod_ref, ng_ref, wqkvg_ref, wlr_ref, wgk_ref, bgk_ref, tril_ref, gng_ref, wout_ref,
              rw_ref, x1_ref, h_ref, lg_ref, st_ref, st_scr, o_scr, *, tt):
    j = pl.program_id(1)

    @pl.when(j == 0)
    def _():
        st_scr[...] = jnp.zeros_like(st_scr)

    sh1, sc1, g1, sh2, sc2, _ = _mod_slices(mod_ref)
    ng = ng_ref[...]
    x = x_ref[...]
    hb = (_rms(x, ng[0:1]) * (1.0 + sc1) + sh1).astype(BF16)
    proj = jnp.dot(hb, wqkvg_ref[...], preferred_element_type=F32)
    q = proj[:, :GLA_DK_TOT] * (GLA_DK ** -0.5)
    k = proj[:, GLA_DK_TOT:2 * GLA_DK_TOT]
    v = proj[:, 2 * GLA_DK_TOT:2 * GLA_DK_TOT + GLA_DV_TOT].astype(BF16)
    go = proj[:, 2 * GLA_DK_TOT + GLA_DV_TOT:]
    log_a = _gla_gate(hb, wlr_ref, wgk_ref, bgk_ref)
    tril = tril_ref[...]
    b = sum(jnp.dot(tril, part, preferred_element_type=F32) for part in _split3(log_a))
    row = lax.broadcasted_iota(jnp.int32, (GLA_CHUNK, GLA_CHUNK), 0)
    col = lax.broadcasted_iota(jnp.int32, (GLA_CHUNK, GLA_CHUNK), 1)
    causal = row >= col
    for c in range(tt // GLA_CHUNK):
        rows = slice(c * GLA_CHUNK, (c + 1) * GLA_CHUNK)
        last = (c + 1) * GLA_CHUNK - 1
        for hd in range(GLA_HEADS):
            kc = slice(hd * GLA_DK, (hd + 1) * GLA_DK)
            vc = slice(hd * GLA_DV, (hd + 1) * GLA_DV)
            bb = b[rows, kc]
            b_last = b[last:last + 1, kc]
            q_dec = (q[rows, kc] * jnp.exp(bb)).astype(BF16)
            k_inv = (k[rows, kc] * jnp.exp(-bb)).astype(BF16)
            k_end = (k[rows, kc] * jnp.exp(b_last - bb)).astype(BF16)
            att = jnp.where(causal, _dot_nt(q_dec, k_inv), 0.0).astype(BF16)
            st = st_scr[hd]
            o = jnp.dot(att, v[rows, vc], preferred_element_type=F32) + _dot_nt(q_dec, st.astype(BF16))
            o_scr[rows, vc] = o
            d_st = lax.dot_general(v[rows, vc], k_end, (((0,), (0,)), ((), ())),
                                   preferred_element_type=F32)
            st_scr[hd] = st * jnp.exp(b_last) + d_st

    @pl.when(j == pl.num_programs(1) - 1)
    def _():
        for hd in range(GLA_HEADS):
            st_ref[hd] = st_scr[hd].T

    y = jnp.dot(_gla_out(o_scr[...], go, gng_ref[...]), wout_ref[...], preferred_element_type=F32)
    x1 = x + g1 * _rms(y, ng[1:2])
    x1_ref[...] = x1
    _ffn_prep(x1, ng, sh2, sc2, rw_ref, h_ref, lg_ref)


def _gla_mixer(x2d, mod3, batch, seq, ng, wqkvg, wlr, wgk, bgk, gng, wout, rw_t):
    tt = MIX_TILE
    tpb = seq // tt
    n = x2d.shape[0]
    idx = jnp.arange(tt)
    tril = ((idx[:, None] >= idx[None, :]) &
            (idx[:, None] // GLA_CHUNK == idx[None, :] // GLA_CHUNK)).astype(BF16)
    row_map = lambda b, j: (b * tpb + j, 0)
    consts = (ng, wqkvg, wlr, wgk, bgk, tril, gng, wout, rw_t)
    return pl.pallas_call(
        functools.partial(_gla_body, tt=tt),
        grid=(batch, tpb),
        in_specs=[pl.BlockSpec((tt, D), row_map),
                  pl.BlockSpec((None, 1, 6 * D), lambda b, j: (b, 0, 0))] +
                 [_const_spec(a.shape) for a in consts],
        out_specs=[pl.BlockSpec((tt, D), row_map), pl.BlockSpec((tt, DP), row_map),
                   pl.BlockSpec((N_EXPERTS, tt), lambda b, j: (0, b * tpb + j)),
                   pl.BlockSpec((None, GLA_HEADS, GLA_DK, GLA_DV), lambda b, j: (b, 0, 0, 0))],
        out_shape=[jax.ShapeDtypeStruct((n, D), F32), jax.ShapeDtypeStruct((n, DP), jnp.int32),
                   jax.ShapeDtypeStruct((N_EXPERTS, n), F32),
                   jax.ShapeDtypeStruct((batch, GLA_HEADS, GLA_DK, GLA_DV), F32)],
        scratch_shapes=[pltpu.VMEM((GLA_HEADS, GLA_DV, GLA_DK), F32),
                        pltpu.VMEM((tt, GLA_DV_TOT), F32)],
        compiler_params=_cparams("parallel", "arbitrary"),
        name="gla_mixer",
    )(x2d, mod3, *consts)


def _gla1_proj_body(x_ref, mod_ref, ng_ref, wqkvg_ref, wlr_ref, wgk_ref, bgk_ref,
                    q_ref, k_ref, v_ref, go_ref, dec_ref):
    sh1, sc1, _, _, _, _ = _mod_slices(mod_ref)
    ng = ng_ref[...]
    hb = (_rms(x_ref[...], ng[0:1]) * (1.0 + sc1) + sh1).astype(BF16)
    proj = jnp.dot(hb, wqkvg_ref[...], preferred_element_type=F32)
    q_ref[...] = proj[:, :GLA_DK_TOT] * (GLA_DK ** -0.5)
    k_ref[...] = proj[:, GLA_DK_TOT:2 * GLA_DK_TOT]
    v_ref[...] = proj[:, 2 * GLA_DK_TOT:2 * GLA_DK_TOT + GLA_DV_TOT]
    go_ref[...] = proj[:, 2 * GLA_DK_TOT + GLA_DV_TOT:]
    dec_ref[...] = jnp.exp(_gla_gate(hb, wlr_ref, wgk_ref, bgk_ref))


GLA1_TOK = 8


def _gla1_state_body(st_ref, qc_ref, kc_ref, dc_ref, v_ref, nst_ref, o_ref):
    v = v_ref[...]
    for i in range(GLA1_TOK):
        for hd in range(GLA_HEADS):
            vrow = v[i:i + 1, hd * GLA_DV:(hd + 1) * GLA_DV]
            s_new = dc_ref[hd][:, i:i + 1] * st_ref[i, hd] + kc_ref[hd][:, i:i + 1] * vrow
            nst_ref[i, hd] = s_new
            o_ref[i:i + 1, hd * GLA_DV:(hd + 1) * GLA_DV] = jnp.sum(
                qc_ref[hd][:, i:i + 1] * s_new, axis=0, keepdims=True)


def _gla1_out_body(x_ref, o_ref, go_ref, mod_ref, ng_ref, gng_ref, wout_ref, rw_ref, x1_ref, h_ref, lg_ref):
    _, _, g1, sh2, sc2, _ = _mod_slices(mod_ref)
    ng = ng_ref[...]
    y = jnp.dot(_gla_out(o_ref[...], go_ref[...], gng_ref[...]), wout_ref[...], preferred_element_type=F32)
    x1 = x_ref[...] + g1 * _rms(y, ng[1:2])
    x1_ref[...] = x1
    _ffn_prep(x1, ng, sh2, sc2, rw_ref, h_ref, lg_ref)


def _gla_mixer_one(x2d, mod2, state, ng, wqkvg, wlr, wgk, bgk, gng, wout, rw_t):
    n = x2d.shape[0]
    consts = (ng, wqkvg, wlr, wgk, bgk)
    q, k, v, go, dec = pl.pallas_call(
        _gla1_proj_body,
        in_specs=[_const_spec(a.shape) for a in (x2d, mod2) + consts],
        out_specs=[_const_spec((n, GLA_DK_TOT)), _const_spec((n, GLA_DK_TOT)), _const_spec((n, GLA_DV_TOT)),
                   _const_spec((n, GLA_DV_TOT)), _const_spec((n, GLA_DK_TOT))],
        out_shape=[jax.ShapeDtypeStruct((n, GLA_DK_TOT), F32), jax.ShapeDtypeStruct((n, GLA_DK_TOT), F32),
                   jax.ShapeDtypeStruct((n, GLA_DV_TOT), F32), jax.ShapeDtypeStruct((n, GLA_DV_TOT), F32),
                   jax.ShapeDtypeStruct((n, GLA_DK_TOT), F32)],
        grid=(1,),
        compiler_params=_cparams("arbitrary"),
        name="gla1_proj",
    )(x2d, mod2, *consts)

    def cols(a):
        return a.reshape(n // GLA1_TOK, GLA1_TOK, GLA_HEADS, GLA_DK).transpose(0, 2, 3, 1)

    col_spec = pl.BlockSpec((None, GLA_HEADS, GLA_DK, GLA1_TOK), lambda i: (i, 0, 0, 0))
    st_spec = pl.BlockSpec((GLA1_TOK, GLA_HEADS, GLA_DK, GLA_DV), lambda i: (i, 0, 0, 0))
    new_state, o = pl.pallas_call(
        _gla1_state_body,
        grid=(n // GLA1_TOK,),
        in_specs=[st_spec, col_spec, col_spec, col_spec, pl.BlockSpec((GLA1_TOK, GLA_DV_TOT), lambda i: (i, 0))],
        out_specs=[st_spec, pl.BlockSpec((GLA1_TOK, GLA_DV_TOT), lambda i: (i, 0))],
        out_shape=[jax.ShapeDtypeStruct(state.shape, F32), jax.ShapeDtypeStruct((n, GLA_DV_TOT), F32)],
        compiler_params=_cparams("parallel"),
        name="gla1_state",
    )(state, cols(q), cols(k), cols(dec), v)

    consts = (mod2, ng, gng, wout, rw_t)
    x1, h, lg = pl.pallas_call(
        _gla1_out_body,
        grid=(1,),
        in_specs=[_const_spec(a.shape) for a in (x2d, o, go) + consts],
        out_specs=[_const_spec((n, D)), _const_spec((n, DP)), _const_spec((N_EXPERTS, n))],
        out_shape=[jax.ShapeDtypeStruct((n, D), F32), jax.ShapeDtypeStruct((n, DP), jnp.int32),
                   jax.ShapeDtypeStruct((N_EXPERTS, n), F32)],
        compiler_params=_cparams("arbitrary"),
        name="gla1_out",
    )(x2d, o, go, *consts)
    return x1, h, lg, new_state


def _moe_routed(h_p, h_s, lg_p, lg_s, router_bias, layer, wg, wu, wd):
    h, lg = h_p, lg_p
    if h_s is not None:
        h = jnp.concatenate([h_p, h_s], axis=0)
        lg = jnp.concatenate([lg_p, lg_s], axis=1)
    n = h.shape[0]
    n_pad = -(-n // TOKEN_PAD) * TOKEN_PAD
    if n_pad != n:
        h = jnp.pad(h, ((0, n_pad - n), (0, 0)))
        lg = jnp.pad(lg, ((0, 0), (0, n_pad - n)))
    eid, rank, wts, counts = _router(lg, router_bias, n)
    tiles_per = (counts[:, 0] + EXPERT_TILE - 1) // EXPERT_TILE
    tile_end = jnp.cumsum(tiles_per)
    off = ((tile_end - tiles_per) * EXPERT_TILE).astype(jnp.int32)
    n_tiles = (TOP_K * n_pad) // EXPERT_TILE + N_EXPERTS
    p_alloc = n_tiles * EXPERT_TILE
    tile_expert = jnp.minimum(jnp.sum(jnp.arange(n_tiles)[:, None] >= tile_end[None, :], axis=1),
                              N_EXPERTS - 1).astype(jnp.int32)
    n_used = tile_end[-1:].astype(jnp.int32)
    dest = _dest(off, eid, rank, n, p_alloc - 1)
    dest_w = dest.reshape(TOP_K, n_pad // DISPATCH_W, DISPATCH_W).transpose(1, 0, 2)
    xs = _sc_dispatch(h, dest_w, p_alloc)
    ys = _experts(xs, tile_expert, n_used, layer, wg, wu, wd)
    y8 = _sc_gather(ys, dest.reshape(TOP_K * n_pad // GATHER_W, 1, GATHER_W))
    return y8.reshape(TOP_K, n_pad, DP), wts.T, h


def kernel(x_prompt, x_sample, state_gla, c_prompt, c_sample, norm_g, ada_w, ada_b, gm_w_in, gm_b_in,
           gm_ln_g, gm_ln_b, gm_w_s, gm_b_s, gm_w_out, gla_w_in, gla_w_gk, gla_b_gk, gla_norm_g,
           gla_w_out, router_w, router_bias, exp_w_gate, exp_w_up, exp_w_down, sh_w_gate, sh_w_up,
           sh_w_down):
    batch, seq, _ = x_prompt.shape
    n_s = x_sample.shape[0]
    n_p = batch * seq
    tpb = seq // MIX_TILE
    xp = x_prompt.reshape(n_p, D)
    xs = x_sample.reshape(n_s, D)

    mod = _ada(jnp.concatenate([c_prompt, c_sample], axis=0), ada_w, ada_b)
    mod_p = [mod[i, :batch].reshape(batch, 1, 6 * D) for i in range(2)]
    mod_s = [mod[i, batch:] for i in range(2)]
    rw_t = [router_w[i].T for i in range(2)]

    ws_causal = jnp.tril(gm_w_s[0]).astype(BF16)
    bs_cols = gm_b_s[0].T
    eye = jnp.eye(GM_CHUNK, dtype=F32)
    ws_first = (gm_w_s[0][:, 0, 0][:, None, None] * eye).astype(BF16)
    bs_first = jnp.broadcast_to(gm_b_s[0][:, 0][None, :], (GM_CHUNK, GM_GROUPS))
    gm_args = (norm_g[0], gm_w_in[0].astype(BF16), gm_b_in[0].reshape(1, -1), gm_ln_g[0].reshape(1, -1),
               gm_ln_b[0].reshape(1, -1))
    wout0 = gm_w_out[0].astype(BF16)
    shared = [(sh_w_gate[i].astype(BF16), sh_w_up[i].astype(BF16), sh_w_down[i].astype(BF16))
              for i in range(2)]
    n_qkvg = 2 * GLA_DK_TOT + 2 * GLA_DV_TOT
    wqkvg = gla_w_in[0][:, :n_qkvg].astype(BF16)
    wlr = jnp.pad(gla_w_in[0][:, n_qkvg:], ((0, 0), (0, LANES - GLA_GATE_RANK))).astype(BF16)
    wgk = jnp.pad(gla_w_gk[0], ((0, LANES - GLA_GATE_RANK), (0, 0))).astype(BF16)
    gla_args = (norm_g[1], wqkvg, wlr, wgk, gla_b_gk[0].reshape(1, -1), gla_norm_g[0].reshape(1, -1),
                gla_w_out[0].astype(BF16), rw_t[1])
    experts = (exp_w_gate, exp_w_up, exp_w_down)

    half = batch // 2
    streams = [(0, half, False), (half, batch - half, True)]
    st = [dict() for _ in streams]

    for s, (b0, nb, with_new) in zip(st, streams):
        s["mod_p"] = [mod_p[i][b0:b0 + nb] for i in range(2)]
        s["n"] = nb * seq
        s["x1p"], s["hp"], s["lgp"] = _gmlp_mixer(xp, b0 * tpb, s["n"], s["mod_p"][0], False, MIX_TILE, tpb,
                                                  *gm_args, ws_causal, bs_cols, wout0, rw_t[0], emit_v=False)
        s["hs"] = s["lgs"] = None
        if with_new:
            s["x1s"], s["hs"], s["lgs"], v_rows = _gmlp_mixer(xs, 0, n_s, mod_s[0], True, n_s, 1, *gm_args,
                                                              ws_first, bs_first, wout0, rw_t[0], emit_v=True)
    for s, (b0, nb, with_new) in zip(st, streams):
        moe = _moe_routed(s["hp"], s["hs"], s["lgp"], s["lgs"], router_bias[0], 0, *experts)
        s["x2p"] = _combine(s["x1p"], *moe, 0, s["mod_p"][0], False, MIX_TILE, tpb, norm_g[0], *shared[0])
        if with_new:
            s["x2s"] = _combine(s["x1s"], *moe, s["n"] // n_s, mod_s[0], True, n_s, 1, norm_g[0], *shared[0])
    for s, (b0, nb, with_new) in zip(st, streams):
        s["x3p"], s["hp"], s["lgp"], s["st_p"] = _gla_mixer(s["x2p"], s["mod_p"][1], nb, seq, *gla_args)
        if with_new:
            s["x3s"], s["hs"], s["lgs"], st_s = _gla_mixer_one(s["x2s"], mod_s[1], state_gla[:, 0], *gla_args)
    y_prompt = None
    for s, (b0, nb, with_new) in zip(st, streams):
        moe = _moe_routed(s["hp"], s["hs"], s["lgp"], s["lgs"], router_bias[1], 1, *experts)
        y_prompt = _combine(s["x3p"], *moe, 0, s["mod_p"][1], False, MIX_TILE, tpb, norm_g[1], *shared[1],
                            out_rows=n_p, out_blk0=b0 * tpb, out_buf=y_prompt)
        if with_new:
            y_new = _combine(s["x3s"], *moe, s["n"] // n_s, mod_s[1], True, n_s, 1, norm_g[1], *shared[1])
    st_p = jnp.concatenate([s["st_p"] for s in st], axis=0)

    return (y_prompt.reshape(batch, seq, D), y_new.reshape(n_s, 1, D), st_p[:, None], st_s[:, None],
            v_rows.reshape(n_s, 1, 1, GM_HALF))
```

```python
import functools
import math

import jax
import jax.numpy as jnp
from jax import lax
from jax.experimental import pallas as pl
from jax.experimental.pallas import tpu as pltpu
from jax.experimental.pallas import tpu_sc as plsc

F32 = jnp.float32
BF16 = jnp.bfloat16

D = 1024
DP = D // 2
GM_CHUNK = 128
GM_HALF = 2 * D
GM_GROUPS = 8
GM_GROUP_DIM = GM_HALF // GM_GROUPS
GLA_HEADS = 4
GLA_DK = 128
GLA_DV = 256
GLA_DK_TOT = GLA_HEADS * GLA_DK
GLA_DV_TOT = GLA_HEADS * GLA_DV
GLA_GATE_RANK = 16
GLA_GATE_NORMALIZER = 16.0
GLA_CHUNK = 64
N_EXPERTS = 64
TOP_K = 8
N_EXPERT_GROUPS = 8
GROUP_SIZE = N_EXPERTS // N_EXPERT_GROUPS
TOPK_GROUPS = 4
EXPERT_DIM = D // 4
ROUTED_SCALE = 2.5
NORM_EPS = 1e-6
LN_EPS = 1e-5

LANES = 128
VMEM_LIMIT = 56 * 1024 * 1024

MIX_TILE = 256
ROUTER_TILE = 512
EXPERT_TILE = 256
SC_WORKERS = 32
DISPATCH_W = 32
GATHER_W = 64
TOKEN_PAD = SC_WORKERS * DISPATCH_W


def _cparams(*sem):
    return pltpu.CompilerParams(dimension_semantics=sem, vmem_limit_bytes=VMEM_LIMIT)


def _rms(x, g):
    return x * lax.rsqrt(jnp.mean(x * x, axis=-1, keepdims=True) + NORM_EPS) * g


def _silu(x):
    return x * (1.0 / (1.0 + jnp.exp(-x)))


def _gelu(x):
    return 0.5 * x * (1.0 + lax.erf(x * (1.0 / math.sqrt(2.0))))


def _bdot(a, b):
    return jnp.dot(a.astype(BF16), b.astype(BF16), preferred_element_type=F32)


def _dot_nt(a, b, precision=None):
    return lax.dot_general(a, b, (((1,), (1,)), ((), ())), preferred_element_type=F32,
                           precision=precision)


def _mod_slices(mod_ref):
    return [mod_ref[:, i * D:(i + 1) * D] for i in range(6)]


HI_HALF = -65536


def _pack_rows(x):
    lo = lax.bitcast_convert_type(x[:, :DP].astype(BF16).astype(F32), jnp.int32)
    hi = lax.bitcast_convert_type(x[:, DP:].astype(BF16).astype(F32), jnp.int32)
    return lax.shift_right_logical(lo, 16) | (hi & HI_HALF)


def _unpack_rows(p):
    lo = lax.bitcast_convert_type(lax.shift_left(p, 16), F32)
    hi = lax.bitcast_convert_type(p & HI_HALF, F32)
    return lo, hi


def _ffn_prep(x1, ng, sh2, sc2, rw_ref, h_ref, lg_ref):
    hffn = _rms(x1, ng[2:3]) * (1.0 + sc2) + sh2
    h_ref[...] = _pack_rows(hffn)
    lg_ref[...] = _dot_nt(rw_ref[...], hffn, precision=lax.Precision.HIGHEST)


def _ada_body(c_ref, w_ref, b_ref, o_ref):
    c = c_ref[...]
    o_ref[...] = _bdot(_silu(c), w_ref[...]) + b_ref[...]


def _ada(c, ada_w, ada_b):
    n = c.shape[0]
    depth = ada_w.shape[0]
    tn = 1536
    return pl.pallas_call(
        _ada_body,
        grid=(depth, 6 * D // tn),
        in_specs=[pl.BlockSpec((n, D), lambda l, j: (0, 0)),
                  pl.BlockSpec((None, D, tn), lambda l, j: (l, 0, j)),
                  pl.BlockSpec((None, 1, tn), lambda l, j: (l, 0, j))],
        out_specs=pl.BlockSpec((None, n, tn), lambda l, j: (l, 0, j)),
        out_shape=jax.ShapeDtypeStruct((depth, n, 6 * D), F32),
        compiler_params=_cparams("parallel", "parallel"),
        name="ada_mod",
    )(c, ada_w, ada_b.reshape(depth, 1, 6 * D))


def _mod_spec(per_row, tt, tiles_per_batch):
    if per_row:
        return pl.BlockSpec((tt, 6 * D), lambda i: (i, 0))
    return pl.BlockSpec((None, 1, 6 * D), lambda i: (i // tiles_per_batch, 0, 0))


def _const_spec(shape):
    zeros = (0,) * len(shape)
    return pl.BlockSpec(shape, lambda *_: zeros)


def _gmlp_body(x_ref, mod_ref, ng_ref, win_ref, bin_ref, lng_ref, lnb_ref, ws_ref, bs_ref, wout_ref,
               rw_ref, x1_ref, h_ref, lg_ref, *rest, n_chunks, emit_v):
    if emit_v:
        v_ref, um_ref = rest
    else:
        (um_ref,) = rest
    sh1, sc1, g1, sh2, sc2, _ = _mod_slices(mod_ref)
    ng = ng_ref[...]
    x = x_ref[...]
    hb = (_rms(x, ng[0:1]) * (1.0 + sc1) + sh1).astype(BF16)
    u = _gelu(jnp.dot(hb, win_ref[:, :GM_HALF], preferred_element_type=F32) + bin_ref[:, :GM_HALF])
    v = _gelu(jnp.dot(hb, win_ref[:, GM_HALF:], preferred_element_type=F32) + bin_ref[:, GM_HALF:])
    mu = jnp.mean(v, axis=-1, keepdims=True)
    vc = v - mu
    var = jnp.mean(vc * vc, axis=-1, keepdims=True)
    v = vc * lax.rsqrt(var + LN_EPS) * lng_ref[...] + lnb_ref[...]
    if emit_v:
        v_ref[...] = v
    vb = v.astype(BF16)
    for c in range(n_chunks):
        rows = slice(c * GM_CHUNK, (c + 1) * GM_CHUNK)
        for g in range(GM_GROUPS):
            cols = slice(g * GM_GROUP_DIM, (g + 1) * GM_GROUP_DIM)
            mixed = jnp.dot(ws_ref[g], vb[rows, cols], preferred_element_type=F32) + bs_ref[:, g:g + 1]
            um_ref[rows, cols] = (u[rows, cols] * mixed).astype(BF16)
    y = jnp.dot(um_ref[...], wout_ref[...], preferred_element_type=F32)
    x1 = x + g1 * _rms(y, ng[1:2])
    x1_ref[...] = x1
    _ffn_prep(x1, ng, sh2, sc2, rw_ref, h_ref, lg_ref)


def _gmlp_mixer(x2d, blk0, n, mod, per_row, tt, tiles_per_batch, ng, win, b_in, ln_g, ln_b, ws, bs, wout,
                rw_t, emit_v):
    out_shape = [jax.ShapeDtypeStruct((n, D), F32), jax.ShapeDtypeStruct((n, DP), jnp.int32),
                 jax.ShapeDtypeStruct((N_EXPERTS, n), F32)]
    out_specs = [pl.BlockSpec((tt, D), lambda i: (i, 0)), pl.BlockSpec((tt, DP), lambda i: (i, 0)),
                 pl.BlockSpec((N_EXPERTS, tt), lambda i: (0, i))]
    if emit_v:
        out_shape.append(jax.ShapeDtypeStruct((n, GM_HALF), F32))
        out_specs.append(pl.BlockSpec((tt, GM_HALF), lambda i: (i, 0)))
    return pl.pallas_call(
        functools.partial(_gmlp_body, n_chunks=tt // GM_CHUNK, emit_v=emit_v),
        grid=(n // tt,),
        in_specs=[pl.BlockSpec((tt, D), lambda i: (i + blk0, 0)),
                  _mod_spec(per_row, tt, tiles_per_batch),
                  _const_spec(ng.shape), _const_spec(win.shape), _const_spec(b_in.shape),
                  _const_spec(ln_g.shape), _const_spec(ln_b.shape), _const_spec(ws.shape),
                  _const_spec(bs.shape), _const_spec(wout.shape), _const_spec(rw_t.shape)],
        out_specs=out_specs,
        out_shape=out_shape,
        scratch_shapes=[pltpu.VMEM((tt, GM_HALF), BF16)],
        compiler_params=_cparams("parallel"),
        name="gmlp_mixer_rows" if per_row else "gmlp_mixer",
    )(x2d, mod, ng, win, b_in, ln_g, ln_b, ws, bs, wout, rw_t)


def _combine_body(x_ref, *refs):
    y_refs = refs[:TOP_K]
    w_ref, h_ref, mod_ref, ng_ref, swg_ref, swu_ref, swd_ref = refs[TOP_K:TOP_K + 7]
    o_ref = refs[-1]
    h_lo, h_hi = _unpack_rows(h_ref[...])
    h_lo = h_lo.astype(BF16)
    h_hi = h_hi.astype(BF16)

    def hdot(w_ref_):
        return (jnp.dot(h_lo, w_ref_[:DP, :], preferred_element_type=F32) +
                jnp.dot(h_hi, w_ref_[DP:, :], preferred_element_type=F32))

    hs = (_silu(hdot(swg_ref)) * hdot(swu_ref)).astype(BF16)
    y = jnp.dot(hs, swd_ref[...], preferred_element_type=F32)
    w = w_ref[...]
    acc_lo = jnp.zeros((x_ref.shape[0], DP), F32)
    acc_hi = jnp.zeros((x_ref.shape[0], DP), F32)
    for k in range(TOP_K):
        lo, hi = _unpack_rows(y_refs[k][...])
        acc_lo += w[:, k:k + 1] * lo
        acc_hi += w[:, k:k + 1] * hi
    y = y + jnp.concatenate([acc_lo, acc_hi], axis=1)
    g2 = mod_ref[:, 5 * D:6 * D]
    o_ref[...] = x_ref[...] + g2 * _rms(y, ng_ref[3:4, :])


def _combine(x2d, y8, wts_t, hp, blk0, mod, per_row, tt, tiles_per_batch, ng, swg, swu, swd,
             out_rows=None, out_blk0=0, out_buf=None):
    n = x2d.shape[0]
    y_specs = [pl.BlockSpec((None, tt, DP), functools.partial(lambda i, k: (k, i + blk0, 0), k=k))
               for k in range(TOP_K)]
    in_specs = ([pl.BlockSpec((tt, D), lambda i: (i, 0))] + y_specs +
                [pl.BlockSpec((tt, TOP_K), lambda i: (i + blk0, 0)),
                 pl.BlockSpec((tt, DP), lambda i: (i + blk0, 0)),
                 _mod_spec(per_row, tt, tiles_per_batch),
                 _const_spec(ng.shape), _const_spec(swg.shape), _const_spec(swu.shape),
                 _const_spec(swd.shape)])
    args = [x2d] + [y8] * TOP_K + [wts_t, hp, mod, ng, swg, swu, swd]
    aliases = {}
    if out_buf is not None:
        in_specs.append(pl.BlockSpec(memory_space=pl.ANY))
        aliases = {len(args): 0}
        args.append(out_buf)
    return pl.pallas_call(
        _combine_body,
        grid=(n // tt,),
        in_specs=in_specs,
        out_specs=pl.BlockSpec((tt, D), lambda i: (i + out_blk0, 0)),
        out_shape=jax.ShapeDtypeStruct((out_rows or n, D), F32),
        input_output_aliases=aliases,
        compiler_params=_cparams("parallel"),
        name="combine_rows" if per_row else "combine",
    )(*args)


def _router_body(lg_ref, bias_ref, tri_ref, eid_ref, rank_ref, wts_ref, cnt_ref, carry_ref, *, n_real):
    step = pl.program_id(0)

    @pl.when(step == 0)
    def _():
        carry_ref[...] = jnp.zeros_like(carry_ref)

    lg = lg_ref[...]
    tn = lg.shape[1]
    real = (step * tn + lax.broadcasted_iota(jnp.int32, (1, tn), 1)) < n_real
    lg = jnp.where(real, lg, 0.0)
    scores = 1.0 / (1.0 + jnp.exp(-lg))
    sel = scores + bias_ref[...]
    neg = -jnp.inf
    sub8 = lax.broadcasted_iota(jnp.int32, (GROUP_SIZE, tn), 0)
    gsub = lax.broadcasted_iota(jnp.int32, (N_EXPERT_GROUPS, tn), 0)
    gs = jnp.zeros((N_EXPERT_GROUPS, tn), F32)
    for g in range(N_EXPERT_GROUPS):
        blk = sel[g * GROUP_SIZE:(g + 1) * GROUP_SIZE, :]
        m1 = jnp.max(blk, axis=0, keepdims=True)
        i1 = jnp.min(jnp.where(blk == m1, sub8, GROUP_SIZE), axis=0, keepdims=True)
        m2 = jnp.max(jnp.where(sub8 == i1, neg, blk), axis=0, keepdims=True)
        gs = jnp.where(gsub == g, m1 + m2, gs)
    gmask = jnp.zeros((N_EXPERT_GROUPS, tn), jnp.bool_)
    for _ in range(TOPK_GROUPS):
        m = jnp.max(gs, axis=0, keepdims=True)
        i = jnp.min(jnp.where(gs == m, gsub, N_EXPERT_GROUPS), axis=0, keepdims=True)
        hit = gsub == i
        gmask = jnp.logical_or(gmask, hit)
        gs = jnp.where(hit, neg, gs)
    gmaskf = gmask.astype(F32)
    blocks = []
    for g in range(N_EXPERT_GROUPS):
        keep = jnp.broadcast_to(gmaskf[g:g + 1, :], (GROUP_SIZE, tn)) > 0.5
        blocks.append(jnp.where(keep, sel[g * GROUP_SIZE:(g + 1) * GROUP_SIZE, :], neg))
    msel = jnp.concatenate(blocks, axis=0)
    esub = lax.broadcasted_iota(jnp.int32, (N_EXPERTS, tn), 0)
    chosen = jnp.zeros((N_EXPERTS, tn), jnp.bool_)
    picks = []
    for _ in range(TOP_K):
        m = jnp.max(msel, axis=0, keepdims=True)
        i = jnp.min(jnp.where(msel == m, esub, N_EXPERTS), axis=0, keepdims=True)
        hit = esub == i
        picks.append(i)
        chosen = jnp.logical_or(chosen, hit)
        msel = jnp.where(hit, neg, msel)
    w = jnp.where(chosen, scores, 0.0)
    w = w / jnp.sum(w, axis=0, keepdims=True) * ROUTED_SCALE
    counted = jnp.where(jnp.logical_and(chosen, real), 1.0, 0.0)
    incl = jnp.dot(counted.astype(BF16), tri_ref[...], preferred_element_type=F32)
    rank_full = carry_ref[:, 0:1] + incl - 1.0
    ksub = lax.broadcasted_iota(jnp.int32, (TOP_K, tn), 0)
    eid = jnp.zeros((TOP_K, tn), jnp.int32)
    rank = jnp.zeros((TOP_K, tn), F32)
    wts = jnp.zeros((TOP_K, tn), F32)
    for k in range(TOP_K):
        hit = esub == picks[k]
        eid = jnp.where(ksub == k, picks[k], eid)
        rank = jnp.where(ksub == k, jnp.sum(jnp.where(hit, rank_full, 0.0), axis=0, keepdims=True), rank)
        wts = jnp.where(ksub == k, jnp.sum(jnp.where(hit, w, 0.0), axis=0, keepdims=True), wts)
    eid_ref[...] = eid
    rank_ref[...] = rank.astype(jnp.int32)
    wts_ref[...] = wts
    carry = carry_ref[...] + incl[:, tn - 1:tn]
    carry_ref[...] = carry
    cnt_ref[...] = carry.astype(jnp.int32)


def _router(lg_t, bias, n_real):
    n = lg_t.shape[1]
    tn = ROUTER_TILE
    idx = jnp.arange(tn)
    tri = (idx[:, None] <= idx[None, :]).astype(BF16)
    kspec = pl.BlockSpec((TOP_K, tn), lambda i: (0, i))
    return pl.pallas_call(
        functools.partial(_router_body, n_real=n_real),
        grid=(n // tn,),
        in_specs=[pl.BlockSpec((N_EXPERTS, tn), lambda i: (0, i)), _const_spec((N_EXPERTS, 1)),
                  _const_spec((tn, tn))],
        out_specs=[kspec, kspec, kspec, _const_spec((N_EXPERTS, LANES))],
        out_shape=[jax.ShapeDtypeStruct((TOP_K, n), jnp.int32), jax.ShapeDtypeStruct((TOP_K, n), jnp.int32),
                   jax.ShapeDtypeStruct((TOP_K, n), F32), jax.ShapeDtypeStruct((N_EXPERTS, LANES), jnp.int32)],
        scratch_shapes=[pltpu.VMEM((N_EXPERTS, LANES), F32)],
        compiler_params=_cparams("arbitrary"),
        name="router",
    )(lg_t, bias.reshape(N_EXPERTS, 1), tri)


def _dest_body(off_ref, eid_ref, rank_ref, dest_ref, *, n_real, last_row):
    eid = eid_ref[...]
    base = jnp.zeros(eid.shape, jnp.int32)
    for e in range(N_EXPERTS):
        base = jnp.where(eid == e, off_ref[e], base)
    tok = lax.broadcasted_iota(jnp.int32, eid.shape, 1)
    slot = lax.broadcasted_iota(jnp.int32, eid.shape, 0)
    unused = last_row - ((tok - n_real) * TOP_K + slot)
    dest_ref[...] = jnp.where(tok < n_real, base + rank_ref[...], unused)


def _dest(off, eid, rank, n_real, last_row):
    spec = pl.BlockSpec(eid.shape, lambda i, off_ref: (0, 0))
    return pl.pallas_call(
        functools.partial(_dest_body, n_real=n_real, last_row=last_row),
        grid_spec=pltpu.PrefetchScalarGridSpec(num_scalar_prefetch=1, grid=(1,), in_specs=[spec, spec],
                                               out_specs=spec),
        out_shape=jax.ShapeDtypeStruct(eid.shape, jnp.int32),
        compiler_params=_cparams("arbitrary"),
        name="dest_rows",
    )(off, eid, rank)


def _sc_mesh():
    return plsc.VectorSubcoreMesh(core_axis_name="core", subcore_axis_name="subcore")


def _sc_dispatch(hp, dest_w, p_alloc):
    n = hp.shape[0]
    w = dest_w.shape[2]

    @functools.partial(pl.kernel, out_type=jax.ShapeDtypeStruct((p_alloc, DP), jnp.int32), mesh=_sc_mesh(),
                       name="sc_dispatch")
    def run(hp_hbm, dest_hbm, xs_hbm):
        def body(x_vmem, i_vmem):
            for k in range(TOP_K):
                pltpu.sync_copy(x_vmem, xs_hbm.at[i_vmem.at[k]])

        pltpu.emit_pipeline(
            body,
            grid=(n // w,),
            in_specs=[pl.BlockSpec((w, DP), lambda i: (i, 0)),
                      pl.BlockSpec((None, TOP_K, w), lambda i: (i, 0, 0))],
            out_specs=[],
            core_axis_name=("core", "subcore"),
            dimension_semantics=(pltpu.PARALLEL,),
        )(hp_hbm, dest_hbm)

    return run(hp, dest_w)


def _sc_gather(ys, dest_g):
    g, _, w = dest_g.shape

    @functools.partial(pl.kernel, out_type=jax.ShapeDtypeStruct((g * w, DP), jnp.int32), mesh=_sc_mesh(),
                       name="sc_gather")
    def run(ys_hbm, dest_hbm, o_hbm):
        def body(i_vmem, o_vmem):
            pltpu.sync_copy(ys_hbm.at[i_vmem.at[0]], o_vmem)

        pltpu.emit_pipeline(
            body,
            grid=(g,),
            in_specs=[pl.BlockSpec((None, 1, w), lambda i: (i, 0, 0))],
            out_specs=[pl.BlockSpec((w, DP), lambda i: (i, 0))],
            core_axis_name=("core", "subcore"),
            dimension_semantics=(pltpu.PARALLEL,),
        )(dest_hbm, o_hbm)

    return run(ys, dest_g)


def _expert_body(first_ref, cnt_ref, xs_hbm, wg_ref, wu_ref, wd_ref, ys_hbm, wg_s, wu_s, wd_s):
    e = pl.program_id(0)
    wg_s[...] = wg_ref[...].astype(BF16)
    wu_s[...] = wu_ref[...].astype(BF16)
    wd_s[...] = wd_ref[...].astype(BF16)

    def tile(x_ref, o_ref):
        lo, hi = _unpack_rows(x_ref[...])
        lo = lo.astype(BF16)
        hi = hi.astype(BF16)

        def xdot(w_s):
            return (jnp.dot(lo, w_s[:DP, :], preferred_element_type=F32) +
                    jnp.dot(hi, w_s[DP:, :], preferred_element_type=F32))

        a = (_silu(xdot(wg_s)) * xdot(wu_s)).astype(BF16)
        o_ref[...] = _pack_rows(jnp.dot(a, wd_s[...], preferred_element_type=F32))

    first = first_ref[e]
    spec = pl.BlockSpec((EXPERT_TILE, DP), lambda j: (first + j, 0))

    @pl.when(cnt_ref[e] > 0)
    def _():
        pltpu.emit_pipeline(tile, grid=(cnt_ref[e],), in_specs=[spec], out_specs=[spec])(xs_hbm, ys_hbm)


def _experts(xs, tile_first, tile_count, layer, wg, wu, wd):
    def w_map(e, first, cnt):
        return (layer, e, 0, 0)

    return pl.pallas_call(
        _expert_body,
        grid_spec=pltpu.PrefetchScalarGridSpec(
            num_scalar_prefetch=2, grid=(N_EXPERTS,),
            in_specs=[pl.BlockSpec(memory_space=pl.ANY),
                      pl.BlockSpec((None, None, D, EXPERT_DIM), w_map),
                      pl.BlockSpec((None, None, D, EXPERT_DIM), w_map),
                      pl.BlockSpec((None, None, EXPERT_DIM, D), w_map)],
            out_specs=pl.BlockSpec(memory_space=pl.ANY),
            scratch_shapes=[pltpu.VMEM((D, EXPERT_DIM), BF16), pltpu.VMEM((D, EXPERT_DIM), BF16),
                            pltpu.VMEM((EXPERT_DIM, D), BF16)]),
        out_shape=jax.ShapeDtypeStruct(xs.shape, jnp.int32),
        compiler_params=_cparams("arbitrary"),
        name="experts",
    )(tile_first, tile_count, xs, wg, wu, wd)


def _log_sigmoid(z):
    return jnp.minimum(z, 0.0) - jnp.log(1.0 + jnp.exp(-jnp.abs(z)))


def _gla_gate(hb, wlr_ref, wgk_ref, bgk_ref):
    lr = jnp.dot(hb, wlr_ref[...], preferred_element_type=F32)
    z = _bdot(lr, wgk_ref[...]) + bgk_ref[...]
    return _log_sigmoid(z) * (1.0 / GLA_GATE_NORMALIZER)


def _split3(a):
    hi = a.astype(BF16)
    r1 = a - hi.astype(F32)
    mid = r1.astype(BF16)
    lo = (r1 - mid.astype(F32)).astype(BF16)
    return hi, mid, lo


def _gla_out(o_ref_val, go, gng):
    parts = []
    for hd in range(GLA_HEADS):
        cols = slice(hd * GLA_DV, (hd + 1) * GLA_DV)
        parts.append((_rms(o_ref_val[:, cols], gng) * _silu(go[:, cols])).astype(BF16))
    return jnp.concatenate(parts, axis=1)


def _gla_body(x_ref, mod_ref, ng_ref, wqkvg_ref, wlr_ref, wgk_ref, bgk_ref, tril_ref, gng_ref, wout_ref,
              rw_ref, x1_ref, h_ref, lg_ref, st_ref, st_scr, o_scr, *, tt):
    j = pl.program_id(1)

    @pl.when(j == 0)
    def _():
        st_scr[...] = jnp.zeros_like(st_scr)

    sh1, sc1, g1, sh2, sc2, _ = _mod_slices(mod_ref)
    ng = ng_ref[...]
    x = x_ref[...]
    hb = (_rms(x, ng[0:1]) * (1.0 + sc1) + sh1).astype(BF16)
    proj = jnp.dot(hb, wqkvg_ref[...], preferred_element_type=F32)
    q = proj[:, :GLA_DK_TOT] * (GLA_DK ** -0.5)
    k = proj[:, GLA_DK_TOT:2 * GLA_DK_TOT]
    v = proj[:, 2 * GLA_DK_TOT:2 * GLA_DK_TOT + GLA_DV_TOT].astype(BF16)
    go = proj[:, 2 * GLA_DK_TOT + GLA_DV_TOT:]
    log_a = _gla_gate(hb, wlr_ref, wgk_ref, bgk_ref)
    tril = tril_ref[...]
    b = sum(jnp.dot(tril, part, preferred_element_type=F32) for part in _split3(log_a))
    row = lax.broadcasted_iota(jnp.int32, (GLA_CHUNK, GLA_CHUNK), 0)
    col = lax.broadcasted_iota(jnp.int32, (GLA_CHUNK, GLA_CHUNK), 1)
    causal = row >= col
    for c in range(tt // GLA_CHUNK):
        rows = slice(c * GLA_CHUNK, (c + 1) * GLA_CHUNK)
        last = (c + 1) * GLA_CHUNK - 1
        for hd in range(GLA_HEADS):
            kc = slice(hd * GLA_DK, (hd + 1) * GLA_DK)
            vc = slice(hd * GLA_DV, (hd + 1) * GLA_DV)
            bb = b[rows, kc]
            b_last = b[last:last + 1, kc]
            q_dec = (q[rows, kc] * jnp.exp(bb)).astype(BF16)
            k_inv = (k[rows, kc] * jnp.exp(-bb)).astype(BF16)
            k_end = (k[rows, kc] * jnp.exp(b_last - bb)).astype(BF16)
            att = jnp.where(causal, _dot_nt(q_dec, k_inv), 0.0).astype(BF16)
            st = st_scr[hd]
            o = jnp.dot(att, v[rows, vc], preferred_element_type=F32) + _dot_nt(q_dec, st.astype(BF16))
            o_scr[rows, vc] = o
            d_st = lax.dot_general(v[rows, vc], k_end, (((0,), (0,)), ((), ())),
                                   preferred_element_type=F32)
            st_scr[hd] = st * jnp.exp(b_last) + d_st

    @pl.when(j == pl.num_programs(1) - 1)
    def _():
        for hd in range(GLA_HEADS):
            st_ref[hd] = st_scr[hd].T

    y = jnp.dot(_gla_out(o_scr[...], go, gng_ref[...]), wout_ref[...], preferred_element_type=F32)
    x1 = x + g1 * _rms(y, ng[1:2])
    x1_ref[...] = x1
    _ffn_prep(x1, ng, sh2, sc2, rw_ref, h_ref, lg_ref)


def _gla_mixer(x2d, mod3, batch, seq, ng, wqkvg, wlr, wgk, bgk, gng, wout, rw_t):
    tt = MIX_TILE
    tpb = seq // tt
    n = x2d.shape[0]
    idx = jnp.arange(tt)
    tril = ((idx[:, None] >= idx[None, :]) &
            (idx[:, None] // GLA_CHUNK == idx[None, :] // GLA_CHUNK)).astype(BF16)
    row_map = lambda b, j: (b * tpb + j, 0)
    consts = (ng, wqkvg, wlr, wgk, bgk, tril, gng, wout, rw_t)
    return pl.pallas_call(
        functools.partial(_gla_body, tt=tt),
        grid=(batch, tpb),
        in_specs=[pl.BlockSpec((tt, D), row_map),
                  pl.BlockSpec((None, 1, 6 * D), lambda b, j: (b, 0, 0))] +
                 [_const_spec(a.shape) for a in consts],
        out_specs=[pl.BlockSpec((tt, D), row_map), pl.BlockSpec((tt, DP), row_map),
                   pl.BlockSpec((N_EXPERTS, tt), lambda b, j: (0, b * tpb + j)),
                   pl.BlockSpec((None, GLA_HEADS, GLA_DK, GLA_DV), lambda b, j: (b, 0, 0, 0))],
        out_shape=[jax.ShapeDtypeStruct((n, D), F32), jax.ShapeDtypeStruct((n, DP), jnp.int32),
                   jax.ShapeDtypeStruct((N_EXPERTS, n), F32),
                   jax.ShapeDtypeStruct((batch, GLA_HEADS, GLA_DK, GLA_DV), F32)],
        scratch_shapes=[pltpu.VMEM((GLA_HEADS, GLA_DV, GLA_DK), F32),
                        pltpu.VMEM((tt, GLA_DV_TOT), F32)],
        compiler_params=_cparams("parallel", "arbitrary"),
        name="gla_mixer",
    )(x2d, mod3, *consts)


def _gla1_proj_body(x_ref, mod_ref, ng_ref, wqkvg_ref, wlr_ref, wgk_ref, bgk_ref,
                    q_ref, k_ref, v_ref, go_ref, dec_ref):
    sh1, sc1, _, _, _, _ = _mod_slices(mod_ref)
    ng = ng_ref[...]
    hb = (_rms(x_ref[...], ng[0:1]) * (1.0 + sc1) + sh1).astype(BF16)
    proj = jnp.dot(hb, wqkvg_ref[...], preferred_element_type=F32)
    q_ref[...] = proj[:, :GLA_DK_TOT] * (GLA_DK ** -0.5)
    k_ref[...] = proj[:, GLA_DK_TOT:2 * GLA_DK_TOT]
    v_ref[...] = proj[:, 2 * GLA_DK_TOT:2 * GLA_DK_TOT + GLA_DV_TOT]
    go_ref[...] = proj[:, 2 * GLA_DK_TOT + GLA_DV_TOT:]
    dec_ref[...] = jnp.exp(_gla_gate(hb, wlr_ref, wgk_ref, bgk_ref))


GLA1_TOK = 8


def _gla1_state_body(st_ref, qc_ref, kc_ref, dc_ref, v_ref, nst_ref, o_ref):
    v = v_ref[...]
    for i in range(GLA1_TOK):
        for hd in range(GLA_HEADS):
            vrow = v[i:i + 1, hd * GLA_DV:(hd + 1) * GLA_DV]
            s_new = dc_ref[hd][:, i:i + 1] * st_ref[i, hd] + kc_ref[hd][:, i:i + 1] * vrow
            nst_ref[i, hd] = s_new
            o_ref[i:i + 1, hd * GLA_DV:(hd + 1) * GLA_DV] = jnp.sum(
                qc_ref[hd][:, i:i + 1] * s_new, axis=0, keepdims=True)


def _gla1_out_body(x_ref, o_ref, go_ref, mod_ref, ng_ref, gng_ref, wout_ref, rw_ref, x1_ref, h_ref, lg_ref):
    _, _, g1, sh2, sc2, _ = _mod_slices(mod_ref)
    ng = ng_ref[...]
    y = jnp.dot(_gla_out(o_ref[...], go_ref[...], gng_ref[...]), wout_ref[...], preferred_element_type=F32)
    x1 = x_ref[...] + g1 * _rms(y, ng[1:2])
    x1_ref[...] = x1
    _ffn_prep(x1, ng, sh2, sc2, rw_ref, h_ref, lg_ref)


def _gla_mixer_one(x2d, mod2, state, ng, wqkvg, wlr, wgk, bgk, gng, wout, rw_t):
    n = x2d.shape[0]
    consts = (ng, wqkvg, wlr, wgk, bgk)
    q, k, v, go, dec = pl.pallas_call(
        _gla1_proj_body,
        in_specs=[_const_spec(a.shape) for a in (x2d, mod2) + consts],
        out_specs=[_const_spec((n, GLA_DK_TOT)), _const_spec((n, GLA_DK_TOT)), _const_spec((n, GLA_DV_TOT)),
                   _const_spec((n, GLA_DV_TOT)), _const_spec((n, GLA_DK_TOT))],
        out_shape=[jax.ShapeDtypeStruct((n, GLA_DK_TOT), F32), jax.ShapeDtypeStruct((n, GLA_DK_TOT), F32),
                   jax.ShapeDtypeStruct((n, GLA_DV_TOT), F32), jax.ShapeDtypeStruct((n, GLA_DV_TOT), F32),
                   jax.ShapeDtypeStruct((n, GLA_DK_TOT), F32)],
        grid=(1,),
        compiler_params=_cparams("arbitrary"),
        name="gla1_proj",
    )(x2d, mod2, *consts)

    def cols(a):
        return a.reshape(n // GLA1_TOK, GLA1_TOK, GLA_HEADS, GLA_DK).transpose(0, 2, 3, 1)

    col_spec = pl.BlockSpec((None, GLA_HEADS, GLA_DK, GLA1_TOK), lambda i: (i, 0, 0, 0))
    st_spec = pl.BlockSpec((GLA1_TOK, GLA_HEADS, GLA_DK, GLA_DV), lambda i: (i, 0, 0, 0))
    new_state, o = pl.pallas_call(
        _gla1_state_body,
        grid=(n // GLA1_TOK,),
        in_specs=[st_spec, col_spec, col_spec, col_spec, pl.BlockSpec((GLA1_TOK, GLA_DV_TOT), lambda i: (i, 0))],
        out_specs=[st_spec, pl.BlockSpec((GLA1_TOK, GLA_DV_TOT), lambda i: (i, 0))],
        out_shape=[jax.ShapeDtypeStruct(state.shape, F32), jax.ShapeDtypeStruct((n, GLA_DV_TOT), F32)],
        compiler_params=_cparams("parallel"),
        name="gla1_state",
    )(state, cols(q), cols(k), cols(dec), v)

    consts = (mod2, ng, gng, wout, rw_t)
    x1, h, lg = pl.pallas_call(
        _gla1_out_body,
        grid=(1,),
        in_specs=[_const_spec(a.shape) for a in (x2d, o, go) + consts],
        out_specs=[_const_spec((n, D)), _const_spec((n, DP)), _const_spec((N_EXPERTS, n))],
        out_shape=[jax.ShapeDtypeStruct((n, D), F32), jax.ShapeDtypeStruct((n, DP), jnp.int32),
                   jax.ShapeDtypeStruct((N_EXPERTS, n), F32)],
        compiler_params=_cparams("arbitrary"),
        name="gla1_out",
    )(x2d, o, go, *consts)
    return x1, h, lg, new_state


def _moe_routed(h_p, h_s, lg_p, lg_s, router_bias, layer, wg, wu, wd):
    h, lg = h_p, lg_p
    if h_s is not None:
        h = jnp.concatenate([h_p, h_s], axis=0)
        lg = jnp.concatenate([lg_p, lg_s], axis=1)
    n = h.shape[0]
    n_pad = -(-n // TOKEN_PAD) * TOKEN_PAD
    if n_pad != n:
        h = jnp.pad(h, ((0, n_pad - n), (0, 0)))
        lg = jnp.pad(lg, ((0, 0), (0, n_pad - n)))
    eid, rank, wts, counts = _router(lg, router_bias, n)
    tile_count = ((counts[:, 0] + EXPERT_TILE - 1) // EXPERT_TILE).astype(jnp.int32)
    tile_first = (jnp.cumsum(tile_count) - tile_count).astype(jnp.int32)
    off = tile_first * EXPERT_TILE
    p_alloc = TOP_K * n_pad + N_EXPERTS * EXPERT_TILE
    dest = _dest(off, eid, rank, n, p_alloc - 1)
    dest_w = dest.reshape(TOP_K, n_pad // DISPATCH_W, DISPATCH_W).transpose(1, 0, 2)
    xs = _sc_dispatch(h, dest_w, p_alloc)
    ys = _experts(xs, tile_first, tile_count, layer, wg, wu, wd)
    y8 = _sc_gather(ys, dest.reshape(TOP_K * n_pad // GATHER_W, 1, GATHER_W))
    return y8.reshape(TOP_K, n_pad, DP), wts.T, h


def kernel(x_prompt, x_sample, state_gla, c_prompt, c_sample, norm_g, ada_w, ada_b, gm_w_in, gm_b_in,
           gm_ln_g, gm_ln_b, gm_w_s, gm_b_s, gm_w_out, gla_w_in, gla_w_gk, gla_b_gk, gla_norm_g,
           gla_w_out, router_w, router_bias, exp_w_gate, exp_w_up, exp_w_down, sh_w_gate, sh_w_up,
           sh_w_down):
    batch, seq, _ = x_prompt.shape
    n_s = x_sample.shape[0]
    n_p = batch * seq
    tpb = seq // MIX_TILE
    xp = x_prompt.reshape(n_p, D)
    xs = x_sample.reshape(n_s, D)

    mod = _ada(jnp.concatenate([c_prompt, c_sample], axis=0), ada_w, ada_b)
    mod_p = [mod[i, :batch].reshape(batch, 1, 6 * D) for i in range(2)]
    mod_s = [mod[i, batch:] for i in range(2)]
    rw_t = [router_w[i].T for i in range(2)]

    ws_causal = jnp.tril(gm_w_s[0]).astype(BF16)
    bs_cols = gm_b_s[0].T
    eye = jnp.eye(GM_CHUNK, dtype=F32)
    ws_first = (gm_w_s[0][:, 0, 0][:, None, None] * eye).astype(BF16)
    bs_first = jnp.broadcast_to(gm_b_s[0][:, 0][None, :], (GM_CHUNK, GM_GROUPS))
    gm_args = (norm_g[0], gm_w_in[0].astype(BF16), gm_b_in[0].reshape(1, -1), gm_ln_g[0].reshape(1, -1),
               gm_ln_b[0].reshape(1, -1))
    wout0 = gm_w_out[0].astype(BF16)
    shared = [(sh_w_gate[i].astype(BF16), sh_w_up[i].astype(BF16), sh_w_down[i].astype(BF16))
              for i in range(2)]
    n_qkvg = 2 * GLA_DK_TOT + 2 * GLA_DV_TOT
    wqkvg = gla_w_in[0][:, :n_qkvg].astype(BF16)
    wlr = jnp.pad(gla_w_in[0][:, n_qkvg:], ((0, 0), (0, LANES - GLA_GATE_RANK))).astype(BF16)
    wgk = jnp.pad(gla_w_gk[0], ((0, LANES - GLA_GATE_RANK), (0, 0))).astype(BF16)
    gla_args = (norm_g[1], wqkvg, wlr, wgk, gla_b_gk[0].reshape(1, -1), gla_norm_g[0].reshape(1, -1),
                gla_w_out[0].astype(BF16), rw_t[1])
    experts = (exp_w_gate, exp_w_up, exp_w_down)

    half = batch // 2
    streams = [(0, half, False), (half, batch - half, True)]
    st = [dict() for _ in streams]

    for s, (b0, nb, with_new) in zip(st, streams):
        s["mod_p"] = [mod_p[i][b0:b0 + nb] for i in range(2)]
        s["n"] = nb * seq
        s["x1p"], s["hp"], s["lgp"] = _gmlp_mixer(xp, b0 * tpb, s["n"], s["mod_p"][0], False, MIX_TILE, tpb,
                                                  *gm_args, ws_causal, bs_cols, wout0, rw_t[0], emit_v=False)
        s["hs"] = s["lgs"] = None
        if with_new:
            s["x1s"], s["hs"], s["lgs"], v_rows = _gmlp_mixer(xs, 0, n_s, mod_s[0], True, n_s, 1, *gm_args,
                                                              ws_first, bs_first, wout0, rw_t[0], emit_v=True)
    for s, (b0, nb, with_new) in zip(st, streams):
        moe = _moe_routed(s["hp"], s["hs"], s["lgp"], s["lgs"], router_bias[0], 0, *experts)
        s["x2p"] = _combine(s["x1p"], *moe, 0, s["mod_p"][0], False, MIX_TILE, tpb, norm_g[0], *shared[0])
        if with_new:
            s["x2s"] = _combine(s["x1s"], *moe, s["n"] // n_s, mod_s[0], True, n_s, 1, norm_g[0], *shared[0])
    for s, (b0, nb, with_new) in zip(st, streams):
        s["x3p"], s["hp"], s["lgp"], s["st_p"] = _gla_mixer(s["x2p"], s["mod_p"][1], nb, seq, *gla_args)
        if with_new:
            s["x3s"], s["hs"], s["lgs"], st_s = _gla_mixer_one(s["x2s"], mod_s[1], state_gla[:, 0], *gla_args)
    y_prompt = None
    for s, (b0, nb, with_new) in zip(st, streams):
        moe = _moe_routed(s["hp"], s["hs"], s["lgp"], s["lgs"], router_bias[1], 1, *experts)
        y_prompt = _combine(s["x3p"], *moe, 0, s["mod_p"][1], False, MIX_TILE, tpb, norm_g[1], *shared[1],
                            out_rows=n_p, out_blk0=b0 * tpb, out_buf=y_prompt)
        if with_new:
            y_new = _combine(s["x3s"], *moe, s["n"] // n_s, mod_s[1], True, n_s, 1, norm_g[1], *shared[1])
    st_p = jnp.concatenate([s["st_p"] for s in st], axis=0)

    return (y_prompt.reshape(batch, seq, D), y_new.reshape(n_s, 1, D), st_p[:, None], st_s[:, None],
            v_rows.reshape(n_s, 1, 1, GM_HALF))
```

```python
import functools
import math

import jax
import jax.numpy as jnp
from jax import lax
from jax.experimental import pallas as pl
from jax.experimental.pallas import tpu as pltpu
from jax.experimental.pallas import tpu_sc as plsc

F32 = jnp.float32
BF16 = jnp.bfloat16

D = 1024
DP = D // 2
GM_CHUNK = 128
GM_HALF = 2 * D
GM_GROUPS = 8
GM_GROUP_DIM = GM_HALF // GM_GROUPS
GLA_HEADS = 4
GLA_DK = 128
GLA_DV = 256
GLA_DK_TOT = GLA_HEADS * GLA_DK
GLA_DV_TOT = GLA_HEADS * GLA_DV
GLA_GATE_RANK = 16
GLA_GATE_NORMALIZER = 16.0
GLA_CHUNK = 64
N_EXPERTS = 64
TOP_K = 8
N_EXPERT_GROUPS = 8
GROUP_SIZE = N_EXPERTS // N_EXPERT_GROUPS
TOPK_GROUPS = 4
EXPERT_DIM = D // 4
ROUTED_SCALE = 2.5
NORM_EPS = 1e-6
LN_EPS = 1e-5

LANES = 128
VMEM_LIMIT = 56 * 1024 * 1024

MIX_TILE = 256
ROUTER_TILE = 512
EXPERT_TILE = 256
EXPERT_X_SLOTS = 4
EXPERT_Y_SLOTS = 2
SC_WORKERS = 32
DISPATCH_W = 32
GATHER_W = 64
TOKEN_PAD = SC_WORKERS * DISPATCH_W


def _cparams(*sem):
    return pltpu.CompilerParams(dimension_semantics=sem, vmem_limit_bytes=VMEM_LIMIT)


def _rms(x, g):
    return x * lax.rsqrt(jnp.mean(x * x, axis=-1, keepdims=True) + NORM_EPS) * g


def _silu(x):
    return x * (1.0 / (1.0 + jnp.exp(-x)))


def _gelu(x):
    return 0.5 * x * (1.0 + lax.erf(x * (1.0 / math.sqrt(2.0))))


def _bdot(a, b):
    return jnp.dot(a.astype(BF16), b.astype(BF16), preferred_element_type=F32)


def _dot_nt(a, b, precision=None):
    return lax.dot_general(a, b, (((1,), (1,)), ((), ())), preferred_element_type=F32,
                           precision=precision)


def _mod_slices(mod_ref):
    return [mod_ref[:, i * D:(i + 1) * D] for i in range(6)]


HI_HALF = -65536


def _pack_rows(x):
    lo = lax.bitcast_convert_type(x[:, :DP].astype(BF16).astype(F32), jnp.int32)
    hi = lax.bitcast_convert_type(x[:, DP:].astype(BF16).astype(F32), jnp.int32)
    return lax.shift_right_logical(lo, 16) | (hi & HI_HALF)


def _unpack_rows(p):
    lo = lax.bitcast_convert_type(lax.shift_left(p, 16), F32)
    hi = lax.bitcast_convert_type(p & HI_HALF, F32)
    return lo, hi


def _ffn_prep(x1, ng, sh2, sc2, rw_ref, h_ref, lg_ref):
    hffn = _rms(x1, ng[2:3]) * (1.0 + sc2) + sh2
    h_ref[...] = _pack_rows(hffn)
    lg_ref[...] = _dot_nt(rw_ref[...], hffn, precision=lax.Precision.HIGHEST)


def _ada_body(c_ref, w_ref, b_ref, o_ref):
    c = c_ref[...]
    o_ref[...] = _bdot(_silu(c), w_ref[...]) + b_ref[...]


def _ada(c, ada_w, ada_b):
    n = c.shape[0]
    depth = ada_w.shape[0]
    tn = 1536
    return pl.pallas_call(
        _ada_body,
        grid=(depth, 6 * D // tn),
        in_specs=[pl.BlockSpec((n, D), lambda l, j: (0, 0)),
                  pl.BlockSpec((None, D, tn), lambda l, j: (l, 0, j)),
                  pl.BlockSpec((None, 1, tn), lambda l, j: (l, 0, j))],
        out_specs=pl.BlockSpec((None, n, tn), lambda l, j: (l, 0, j)),
        out_shape=jax.ShapeDtypeStruct((depth, n, 6 * D), F32),
        compiler_params=_cparams("parallel", "parallel"),
        name="ada_mod",
    )(c, ada_w, ada_b.reshape(depth, 1, 6 * D))


def _mod_spec(per_row, tt, tiles_per_batch):
    if per_row:
        return pl.BlockSpec((tt, 6 * D), lambda i: (i, 0))
    return pl.BlockSpec((None, 1, 6 * D), lambda i: (i // tiles_per_batch, 0, 0))


def _const_spec(shape):
    zeros = (0,) * len(shape)
    return pl.BlockSpec(shape, lambda *_: zeros)


def _gmlp_body(x_ref, mod_ref, ng_ref, win_ref, bin_ref, lng_ref, lnb_ref, ws_ref, bs_ref, wout_ref,
               rw_ref, x1_ref, h_ref, lg_ref, *rest, n_chunks, emit_v):
    if emit_v:
        v_ref, um_ref = rest
    else:
        (um_ref,) = rest
    sh1, sc1, g1, sh2, sc2, _ = _mod_slices(mod_ref)
    ng = ng_ref[...]
    x = x_ref[...]
    hb = (_rms(x, ng[0:1]) * (1.0 + sc1) + sh1).astype(BF16)
    u = _gelu(jnp.dot(hb, win_ref[:, :GM_HALF], preferred_element_type=F32) + bin_ref[:, :GM_HALF])
    v = _gelu(jnp.dot(hb, win_ref[:, GM_HALF:], preferred_element_type=F32) + bin_ref[:, GM_HALF:])
    mu = jnp.mean(v, axis=-1, keepdims=True)
    vc = v - mu
    var = jnp.mean(vc * vc, axis=-1, keepdims=True)
    v = vc * lax.rsqrt(var + LN_EPS) * lng_ref[...] + lnb_ref[...]
    if emit_v:
        v_ref[...] = v
    vb = v.astype(BF16)
    for c in range(n_chunks):
        rows = slice(c * GM_CHUNK, (c + 1) * GM_CHUNK)
        for g in range(GM_GROUPS):
            cols = slice(g * GM_GROUP_DIM, (g + 1) * GM_GROUP_DIM)
            mixed = jnp.dot(ws_ref[g], vb[rows, cols], preferred_element_type=F32) + bs_ref[:, g:g + 1]
            um_ref[rows, cols] = (u[rows, cols] * mixed).astype(BF16)
    y = jnp.dot(um_ref[...], wout_ref[...], preferred_element_type=F32)
    x1 = x + g1 * _rms(y, ng[1:2])
    x1_ref[...] = x1
    _ffn_prep(x1, ng, sh2, sc2, rw_ref, h_ref, lg_ref)


def _gmlp_mixer(x2d, blk0, n, mod, per_row, tt, tiles_per_batch, ng, win, b_in, ln_g, ln_b, ws, bs, wout,
                rw_t, emit_v):
    out_shape = [jax.ShapeDtypeStruct((n, D), F32), jax.ShapeDtypeStruct((n, DP), jnp.int32),
                 jax.ShapeDtypeStruct((N_EXPERTS, n), F32)]
    out_specs = [pl.BlockSpec((tt, D), lambda i: (i, 0)), pl.BlockSpec((tt, DP), lambda i: (i, 0)),
                 pl.BlockSpec((N_EXPERTS, tt), lambda i: (0, i))]
    if emit_v:
        out_shape.append(jax.ShapeDtypeStruct((n, GM_HALF), F32))
        out_specs.append(pl.BlockSpec((tt, GM_HALF), lambda i: (i, 0)))
    return pl.pallas_call(
        functools.partial(_gmlp_body, n_chunks=tt // GM_CHUNK, emit_v=emit_v),
        grid=(n // tt,),
        in_specs=[pl.BlockSpec((tt, D), lambda i: (i + blk0, 0)),
                  _mod_spec(per_row, tt, tiles_per_batch),
                  _const_spec(ng.shape), _const_spec(win.shape), _const_spec(b_in.shape),
                  _const_spec(ln_g.shape), _const_spec(ln_b.shape), _const_spec(ws.shape),
                  _const_spec(bs.shape), _const_spec(wout.shape), _const_spec(rw_t.shape)],
        out_specs=out_specs,
        out_shape=out_shape,
        scratch_shapes=[pltpu.VMEM((tt, GM_HALF), BF16)],
        compiler_params=_cparams("parallel"),
        name="gmlp_mixer_rows" if per_row else "gmlp_mixer",
    )(x2d, mod, ng, win, b_in, ln_g, ln_b, ws, bs, wout, rw_t)


def _combine_body(x_ref, *refs):
    y_refs = refs[:TOP_K]
    w_ref, h_ref, mod_ref, ng_ref, swg_ref, swu_ref, swd_ref = refs[TOP_K:TOP_K + 7]
    o_ref = refs[-1]
    h_lo, h_hi = _unpack_rows(h_ref[...])
    h_lo = h_lo.astype(BF16)
    h_hi = h_hi.astype(BF16)

    def hdot(w_ref_):
        return (jnp.dot(h_lo, w_ref_[:DP, :], preferred_element_type=F32) +
                jnp.dot(h_hi, w_ref_[DP:, :], preferred_element_type=F32))

    hs = (_silu(hdot(swg_ref)) * hdot(swu_ref)).astype(BF16)
    y = jnp.dot(hs, swd_ref[...], preferred_element_type=F32)
    w = w_ref[...]
    acc_lo = jnp.zeros((x_ref.shape[0], DP), F32)
    acc_hi = jnp.zeros((x_ref.shape[0], DP), F32)
    for k in range(TOP_K):
        lo, hi = _unpack_rows(y_refs[k][...])
        acc_lo += w[:, k:k + 1] * lo
        acc_hi += w[:, k:k + 1] * hi
    y = y + jnp.concatenate([acc_lo, acc_hi], axis=1)
    g2 = mod_ref[:, 5 * D:6 * D]
    o_ref[...] = x_ref[...] + g2 * _rms(y, ng_ref[3:4, :])


def _combine(x2d, y8, wts_t, hp, blk0, mod, per_row, tt, tiles_per_batch, ng, swg, swu, swd,
             out_rows=None, out_blk0=0, out_buf=None):
    n = x2d.shape[0]
    y_specs = [pl.BlockSpec((None, tt, DP), functools.partial(lambda i, k: (k, i + blk0, 0), k=k))
               for k in range(TOP_K)]
    in_specs = ([pl.BlockSpec((tt, D), lambda i: (i, 0))] + y_specs +
                [pl.BlockSpec((tt, TOP_K), lambda i: (i + blk0, 0)),
                 pl.BlockSpec((tt, DP), lambda i: (i + blk0, 0)),
                 _mod_spec(per_row, tt, tiles_per_batch),
                 _const_spec(ng.shape), _const_spec(swg.shape), _const_spec(swu.shape),
                 _const_spec(swd.shape)])
    args = [x2d] + [y8] * TOP_K + [wts_t, hp, mod, ng, swg, swu, swd]
    aliases = {}
    if out_buf is not None:
        in_specs.append(pl.BlockSpec(memory_space=pl.ANY))
        aliases = {len(args): 0}
        args.append(out_buf)
    return pl.pallas_call(
        _combine_body,
        grid=(n // tt,),
        in_specs=in_specs,
        out_specs=pl.BlockSpec((tt, D), lambda i: (i + out_blk0, 0)),
        out_shape=jax.ShapeDtypeStruct((out_rows or n, D), F32),
        input_output_aliases=aliases,
        compiler_params=_cparams("parallel"),
        name="combine_rows" if per_row else "combine",
    )(*args)


def _router_body(lg_ref, bias_ref, tri_ref, eid_ref, rank_ref, wts_ref, cnt_ref, carry_ref, *, n_real):
    step = pl.program_id(0)

    @pl.when(step == 0)
    def _():
        carry_ref[...] = jnp.zeros_like(carry_ref)

    lg = lg_ref[...]
    tn = lg.shape[1]
    real = (step * tn + lax.broadcasted_iota(jnp.int32, (1, tn), 1)) < n_real
    lg = jnp.where(real, lg, 0.0)
    scores = 1.0 / (1.0 + jnp.exp(-lg))
    sel = scores + bias_ref[...]
    neg = -jnp.inf
    sub8 = lax.broadcasted_iota(jnp.int32, (GROUP_SIZE, tn), 0)
    gsub = lax.broadcasted_iota(jnp.int32, (N_EXPERT_GROUPS, tn), 0)
    gs = jnp.zeros((N_EXPERT_GROUPS, tn), F32)
    for g in range(N_EXPERT_GROUPS):
        blk = sel[g * GROUP_SIZE:(g + 1) * GROUP_SIZE, :]
        m1 = jnp.max(blk, axis=0, keepdims=True)
        i1 = jnp.min(jnp.where(blk == m1, sub8, GROUP_SIZE), axis=0, keepdims=True)
        m2 = jnp.max(jnp.where(sub8 == i1, neg, blk), axis=0, keepdims=True)
        gs = jnp.where(gsub == g, m1 + m2, gs)
    gmask = jnp.zeros((N_EXPERT_GROUPS, tn), jnp.bool_)
    for _ in range(TOPK_GROUPS):
        m = jnp.max(gs, axis=0, keepdims=True)
        i = jnp.min(jnp.where(gs == m, gsub, N_EXPERT_GROUPS), axis=0, keepdims=True)
        hit = gsub == i
        gmask = jnp.logical_or(gmask, hit)
        gs = jnp.where(hit, neg, gs)
    gmaskf = gmask.astype(F32)
    blocks = []
    for g in range(N_EXPERT_GROUPS):
        keep = jnp.broadcast_to(gmaskf[g:g + 1, :], (GROUP_SIZE, tn)) > 0.5
        blocks.append(jnp.where(keep, sel[g * GROUP_SIZE:(g + 1) * GROUP_SIZE, :], neg))
    msel = jnp.concatenate(blocks, axis=0)
    esub = lax.broadcasted_iota(jnp.int32, (N_EXPERTS, tn), 0)
    chosen = jnp.zeros((N_EXPERTS, tn), jnp.bool_)
    picks = []
    for _ in range(TOP_K):
        m = jnp.max(msel, axis=0, keepdims=True)
        i = jnp.min(jnp.where(msel == m, esub, N_EXPERTS), axis=0, keepdims=True)
        hit = esub == i
        picks.append(i)
        chosen = jnp.logical_or(chosen, hit)
        msel = jnp.where(hit, neg, msel)
    w = jnp.where(chosen, scores, 0.0)
    w = w / jnp.sum(w, axis=0, keepdims=True) * ROUTED_SCALE
    counted = jnp.where(jnp.logical_and(chosen, real), 1.0, 0.0)
    incl = jnp.dot(counted.astype(BF16), tri_ref[...], preferred_element_type=F32)
    rank_full = carry_ref[:, 0:1] + incl - 1.0
    ksub = lax.broadcasted_iota(jnp.int32, (TOP_K, tn), 0)
    eid = jnp.zeros((TOP_K, tn), jnp.int32)
    rank = jnp.zeros((TOP_K, tn), F32)
    wts = jnp.zeros((TOP_K, tn), F32)
    for k in range(TOP_K):
        hit = esub == picks[k]
        eid = jnp.where(ksub == k, picks[k], eid)
        rank = jnp.where(ksub == k, jnp.sum(jnp.where(hit, rank_full, 0.0), axis=0, keepdims=True), rank)
        wts = jnp.where(ksub == k, jnp.sum(jnp.where(hit, w, 0.0), axis=0, keepdims=True), wts)
    eid_ref[...] = eid
    rank_ref[...] = rank.astype(jnp.int32)
    wts_ref[...] = wts
    carry = carry_ref[...] + incl[:, tn - 1:tn]
    carry_ref[...] = carry
    cnt_ref[...] = carry.astype(jnp.int32)


def _router(lg_t, bias, n_real):
    n = lg_t.shape[1]
    tn = ROUTER_TILE
    idx = jnp.arange(tn)
    tri = (idx[:, None] <= idx[None, :]).astype(BF16)
    kspec = pl.BlockSpec((TOP_K, tn), lambda i: (0, i))
    return pl.pallas_call(
        functools.partial(_router_body, n_real=n_real),
        grid=(n // tn,),
        in_specs=[pl.BlockSpec((N_EXPERTS, tn), lambda i: (0, i)), _const_spec((N_EXPERTS, 1)),
                  _const_spec((tn, tn))],
        out_specs=[kspec, kspec, kspec, _const_spec((N_EXPERTS, LANES))],
        out_shape=[jax.ShapeDtypeStruct((TOP_K, n), jnp.int32), jax.ShapeDtypeStruct((TOP_K, n), jnp.int32),
                   jax.ShapeDtypeStruct((TOP_K, n), F32), jax.ShapeDtypeStruct((N_EXPERTS, LANES), jnp.int32)],
        scratch_shapes=[pltpu.VMEM((N_EXPERTS, LANES), F32)],
        compiler_params=_cparams("arbitrary"),
        name="router",
    )(lg_t, bias.reshape(N_EXPERTS, 1), tri)


def _dest_body(off_ref, eid_ref, rank_ref, dest_ref, *, n_real, last_row):
    eid = eid_ref[...]
    base = jnp.zeros(eid.shape, jnp.int32)
    for e in range(N_EXPERTS):
        base = jnp.where(eid == e, off_ref[e], base)
    tok = lax.broadcasted_iota(jnp.int32, eid.shape, 1)
    slot = lax.broadcasted_iota(jnp.int32, eid.shape, 0)
    unused = last_row - ((tok - n_real) * TOP_K + slot)
    dest_ref[...] = jnp.where(tok < n_real, base + rank_ref[...], unused)


def _dest(off, eid, rank, n_real, last_row):
    spec = pl.BlockSpec(eid.shape, lambda i, off_ref: (0, 0))
    return pl.pallas_call(
        functools.partial(_dest_body, n_real=n_real, last_row=last_row),
        grid_spec=pltpu.PrefetchScalarGridSpec(num_scalar_prefetch=1, grid=(1,), in_specs=[spec, spec],
                                               out_specs=spec),
        out_shape=jax.ShapeDtypeStruct(eid.shape, jnp.int32),
        compiler_params=_cparams("arbitrary"),
        name="dest_rows",
    )(off, eid, rank)


def _sc_mesh():
    return plsc.VectorSubcoreMesh(core_axis_name="core", subcore_axis_name="subcore")


def _sc_dispatch(hp, dest_w, p_alloc):
    n = hp.shape[0]
    w = dest_w.shape[2]

    @functools.partial(pl.kernel, out_type=jax.ShapeDtypeStruct((p_alloc, DP), jnp.int32), mesh=_sc_mesh(),
                       name="sc_dispatch")
    def run(hp_hbm, dest_hbm, xs_hbm):
        def body(x_vmem, i_vmem):
            for k in range(TOP_K):
                pltpu.sync_copy(x_vmem, xs_hbm.at[i_vmem.at[k]])

        pltpu.emit_pipeline(
            body,
            grid=(n // w,),
            in_specs=[pl.BlockSpec((w, DP), lambda i: (i, 0)),
                      pl.BlockSpec((None, TOP_K, w), lambda i: (i, 0, 0))],
            out_specs=[],
            core_axis_name=("core", "subcore"),
            dimension_semantics=(pltpu.PARALLEL,),
        )(hp_hbm, dest_hbm)

    return run(hp, dest_w)


def _sc_gather(ys, dest_g):
    g, _, w = dest_g.shape

    @functools.partial(pl.kernel, out_type=jax.ShapeDtypeStruct((g * w, DP), jnp.int32), mesh=_sc_mesh(),
                       name="sc_gather")
    def run(ys_hbm, dest_hbm, o_hbm):
        def body(i_vmem, o_vmem):
            pltpu.sync_copy(ys_hbm.at[i_vmem.at[0]], o_vmem)

        pltpu.emit_pipeline(
            body,
            grid=(g,),
            in_specs=[pl.BlockSpec((None, 1, w), lambda i: (i, 0, 0))],
            out_specs=[pl.BlockSpec((w, DP), lambda i: (i, 0))],
            core_axis_name=("core", "subcore"),
            dimension_semantics=(pltpu.PARALLEL,),
        )(dest_hbm, o_hbm)

    return run(ys, dest_g)


def _expert_body(first_ref, cnt_ref, nused_ref, xs_hbm, wg_ref, wu_ref, wd_ref, ys_hbm,
                 wg_s, wu_s, wd_s, xbuf, ybuf, xsem, ysem):
    e = pl.program_id(0)
    n_used = nused_ref[0]

    def load(g):
        rows = pl.ds(pl.multiple_of(g * EXPERT_TILE, EXPERT_TILE), EXPERT_TILE)
        slot = g % EXPERT_X_SLOTS
        return pltpu.make_async_copy(xs_hbm.at[rows], xbuf.at[slot], xsem.at[slot])

    def store(g):
        rows = pl.ds(pl.multiple_of(g * EXPERT_TILE, EXPERT_TILE), EXPERT_TILE)
        slot = g % EXPERT_Y_SLOTS
        return pltpu.make_async_copy(ybuf.at[slot], ys_hbm.at[rows], ysem.at[slot])

    @pl.when(e == 0)
    def _():
        for g in range(EXPERT_X_SLOTS - 1):
            @pl.when(g < n_used)
            def _():
                load(g).start()

    wg_s[...] = wg_ref[...].astype(BF16)
    wu_s[...] = wu_ref[...].astype(BF16)
    wd_s[...] = wd_ref[...].astype(BF16)
    first = first_ref[e]

    def tile(j, carry):
        g = first + j
        ahead = g + (EXPERT_X_SLOTS - 1)

        @pl.when(ahead < n_used)
        def _():
            load(ahead).start()

        load(g).wait()

        @pl.when(g >= EXPERT_Y_SLOTS)
        def _():
            store(g - EXPERT_Y_SLOTS).wait()

        lo, hi = _unpack_rows(xbuf[g % EXPERT_X_SLOTS])
        lo = lo.astype(BF16)
        hi = hi.astype(BF16)

        def xdot(w_s):
            return (jnp.dot(lo, w_s[:DP, :], preferred_element_type=F32) +
                    jnp.dot(hi, w_s[DP:, :], preferred_element_type=F32))

        a = (_silu(xdot(wg_s)) * xdot(wu_s)).astype(BF16)
        ybuf[g % EXPERT_Y_SLOTS] = _pack_rows(jnp.dot(a, wd_s[...], preferred_element_type=F32))
        store(g).start()
        return carry

    lax.fori_loop(0, cnt_ref[e], tile, 0)

    @pl.when(e == N_EXPERTS - 1)
    def _():
        for k in range(EXPERT_Y_SLOTS):
            g = n_used - 1 - k

            @pl.when(g >= 0)
            def _():
                store(g).wait()


def _experts(xs, tile_first, tile_count, n_used, layer, wg, wu, wd):
    def w_map(e, first, cnt, nu):
        return (layer, e, 0, 0)

    return pl.pallas_call(
        _expert_body,
        grid_spec=pltpu.PrefetchScalarGridSpec(
            num_scalar_prefetch=3, grid=(N_EXPERTS,),
            in_specs=[pl.BlockSpec(memory_space=pl.ANY),
                      pl.BlockSpec((None, None, D, EXPERT_DIM), w_map),
                      pl.BlockSpec((None, None, D, EXPERT_DIM), w_map),
                      pl.BlockSpec((None, None, EXPERT_DIM, D), w_map)],
            out_specs=pl.BlockSpec(memory_space=pl.ANY),
            scratch_shapes=[pltpu.VMEM((D, EXPERT_DIM), BF16), pltpu.VMEM((D, EXPERT_DIM), BF16),
                            pltpu.VMEM((EXPERT_DIM, D), BF16),
                            pltpu.VMEM((EXPERT_X_SLOTS, EXPERT_TILE, DP), jnp.int32),
                            pltpu.VMEM((EXPERT_Y_SLOTS, EXPERT_TILE, DP), jnp.int32),
                            pltpu.SemaphoreType.DMA((EXPERT_X_SLOTS,)),
                            pltpu.SemaphoreType.DMA((EXPERT_Y_SLOTS,))]),
        out_shape=jax.ShapeDtypeStruct(xs.shape, jnp.int32),
        compiler_params=_cparams("arbitrary"),
        name="experts",
    )(tile_first, tile_count, n_used, xs, wg, wu, wd)


def _log_sigmoid(z):
    return jnp.minimum(z, 0.0) - jnp.log(1.0 + jnp.exp(-jnp.abs(z)))


def _gla_gate(hb, wlr_ref, wgk_ref, bgk_ref):
    lr = jnp.dot(hb, wlr_ref[...], preferred_element_type=F32)
    z = _bdot(lr, wgk_ref[...]) + bgk_ref[...]
    return _log_sigmoid(z) * (1.0 / GLA_GATE_NORMALIZER)


def _split3(a):
    hi = a.astype(BF16)
    r1 = a - hi.astype(F32)
    mid = r1.astype(BF16)
    lo = (r1 - mid.astype(F32)).astype(BF16)
    return hi, mid, lo


def _gla_out(o_ref_val, go, gng):
    parts = []
    for hd in range(GLA_HEADS):
        cols = slice(hd * GLA_DV, (hd + 1) * GLA_DV)
        parts.append((_rms(o_ref_val[:, cols], gng) * _silu(go[:, cols])).astype(BF16))
    return jnp.concatenate(parts, axis=1)


def _gla_body(x_ref, mod_ref, ng_ref, wqkvg_ref, wlr_ref, wgk_ref, bgk_ref, tril_ref, gng_ref, wout_ref,
              rw_ref, x1_ref, h_ref, lg_ref, st_ref, st_scr, o_scr, *, tt):
    j = pl.program_id(1)

    @pl.when(j == 0)
    def _():
        st_scr[...] = jnp.zeros_like(st_scr)

    sh1, sc1, g1, sh2, sc2, _ = _mod_slices(mod_ref)
    ng = ng_ref[...]
    x = x_ref[...]
    hb = (_rms(x, ng[0:1]) * (1.0 + sc1) + sh1).astype(BF16)
    proj = jnp.dot(hb, wqkvg_ref[...], preferred_element_type=F32)
    q = proj[:, :GLA_DK_TOT] * (GLA_DK ** -0.5)
    k = proj[:, GLA_DK_TOT:2 * GLA_DK_TOT]
    v = proj[:, 2 * GLA_DK_TOT:2 * GLA_DK_TOT + GLA_DV_TOT].astype(BF16)
    go = proj[:, 2 * GLA_DK_TOT + GLA_DV_TOT:]
    log_a = _gla_gate(hb, wlr_ref, wgk_ref, bgk_ref)
    tril = tril_ref[...]
    b = sum(jnp.dot(tril, part, preferred_element_type=F32) for part in _split3(log_a))
    row = lax.broadcasted_iota(jnp.int32, (GLA_CHUNK, GLA_CHUNK), 0)
    col = lax.broadcasted_iota(jnp.int32, (GLA_CHUNK, GLA_CHUNK), 1)
    causal = row >= col
    for c in range(tt // GLA_CHUNK):
        rows = slice(c * GLA_CHUNK, (c + 1) * GLA_CHUNK)
        last = (c + 1) * GLA_CHUNK - 1
        for hd in range(GLA_HEADS):
            kc = slice(hd * GLA_DK, (hd + 1) * GLA_DK)
            vc = slice(hd * GLA_DV, (hd + 1) * GLA_DV)
            bb = b[rows, kc]
            b_last = b[last:last + 1, kc]
            q_dec = (q[rows, kc] * jnp.exp(bb)).astype(BF16)
            k_inv = (k[rows, kc] * jnp.exp(-bb)).astype(BF16)
            k_end = (k[rows, kc] * jnp.exp(b_last - bb)).astype(BF16)
            att = jnp.where(causal, _dot_nt(q_dec, k_inv), 0.0).astype(BF16)
            st = st_scr[hd]
            o = jnp.dot(att, v[rows, vc], preferred_element_type=F32) + _dot_nt(q_dec, st.astype(BF16))
            o_scr[rows, vc] = o
            d_st = lax.dot_general(v[rows, vc], k_end, (((0,), (0,)), ((), ())),
                                   preferred_element_type=F32)
            st_scr[hd] = st * jnp.exp(b_last) + d_st

    @pl.when(j == pl.num_programs(1) - 1)
    def _():
        for hd in range(GLA_HEADS):
            st_ref[hd] = st_scr[hd].T

    y = jnp.dot(_gla_out(o_scr[...], go, gng_ref[...]), wout_ref[...], preferred_element_type=F32)
    x1 = x + g1 * _rms(y, ng[1:2])
    x1_ref[...] = x1
    _ffn_prep(x1, ng, sh2, sc2, rw_ref, h_ref, lg_ref)


def _gla_mixer(x2d, mod3, batch, seq, ng, wqkvg, wlr, wgk, bgk, gng, wout, rw_t):
    tt = MIX_TILE
    tpb = seq // tt
    n = x2d.shape[0]
    idx = jnp.arange(tt)
    tril = ((idx[:, None] >= idx[None, :]) &
            (idx[:, None] // GLA_CHUNK == idx[None, :] // GLA_CHUNK)).astype(BF16)
    row_map = lambda b, j: (b * tpb + j, 0)
    consts = (ng, wqkvg, wlr, wgk, bgk, tril, gng, wout, rw_t)
    return pl.pallas_call(
        functools.partial(_gla_body, tt=tt),
        grid=(batch, tpb),
        in_specs=[pl.BlockSpec((tt, D), row_map),
                  pl.BlockSpec((None, 1, 6 * D), lambda b, j: (b, 0, 0))] +
                 [_const_spec(a.shape) for a in consts],
        out_specs=[pl.BlockSpec((tt, D), row_map), pl.BlockSpec((tt, DP), row_map),
                   pl.BlockSpec((N_EXPERTS, tt), lambda b, j: (0, b * tpb + j)),
                   pl.BlockSpec((None, GLA_HEADS, GLA_DK, GLA_DV), lambda b, j: (b, 0, 0, 0))],
        out_shape=[jax.ShapeDtypeStruct((n, D), F32), jax.ShapeDtypeStruct((n, DP), jnp.int32),
                   jax.ShapeDtypeStruct((N_EXPERTS, n), F32),
                   jax.ShapeDtypeStruct((batch, GLA_HEADS, GLA_DK, GLA_DV), F32)],
        scratch_shapes=[pltpu.VMEM((GLA_HEADS, GLA_DV, GLA_DK), F32),
                        pltpu.VMEM((tt, GLA_DV_TOT), F32)],
        compiler_params=_cparams("parallel", "arbitrary"),
        name="gla_mixer",
    )(x2d, mod3, *consts)


def _gla1_proj_body(x_ref, mod_ref, ng_ref, wqkvg_ref, wlr_ref, wgk_ref, bgk_ref,
                    q_ref, k_ref, v_ref, go_ref, dec_ref):
    sh1, sc1, _, _, _, _ = _mod_slices(mod_ref)
    ng = ng_ref[...]
    hb = (_rms(x_ref[...], ng[0:1]) * (1.0 + sc1) + sh1).astype(BF16)
    proj = jnp.dot(hb, wqkvg_ref[...], preferred_element_type=F32)
    q_ref[...] = proj[:, :GLA_DK_TOT] * (GLA_DK ** -0.5)
    k_ref[...] = proj[:, GLA_DK_TOT:2 * GLA_DK_TOT]
    v_ref[...] = proj[:, 2 * GLA_DK_TOT:2 * GLA_DK_TOT + GLA_DV_TOT]
    go_ref[...] = proj[:, 2 * GLA_DK_TOT + GLA_DV_TOT:]
    dec_ref[...] = jnp.exp(_gla_gate(hb, wlr_ref, wgk_ref, bgk_ref))


GLA1_TOK = 8


def _gla1_state_body(st_ref, qc_ref, kc_ref, dc_ref, v_ref, nst_ref, o_ref):
    v = v_ref[...]
    for i in range(GLA1_TOK):
        for hd in range(GLA_HEADS):
            vrow = v[i:i + 1, hd * GLA_DV:(hd + 1) * GLA_DV]
            s_new = dc_ref[hd][:, i:i + 1] * st_ref[i, hd] + kc_ref[hd][:, i:i + 1] * vrow
            nst_ref[i, hd] = s_new
            o_ref[i:i + 1, hd * GLA_DV:(hd + 1) * GLA_DV] = jnp.sum(
                qc_ref[hd][:, i:i + 1] * s_new, axis=0, keepdims=True)


def _gla1_out_body(x_ref, o_ref, go_ref, mod_ref, ng_ref, gng_ref, wout_ref, rw_ref, x1_ref, h_ref, lg_ref):
    _, _, g1, sh2, sc2, _ = _mod_slices(mod_ref)
    ng = ng_ref[...]
    y = jnp.dot(_gla_out(o_ref[...], go_ref[...], gng_ref[...]), wout_ref[...], preferred_element_type=F32)
    x1 = x_ref[...] + g1 * _rms(y, ng[1:2])
    x1_ref[...] = x1
    _ffn_prep(x1, ng, sh2, sc2, rw_ref, h_ref, lg_ref)


def _gla_mixer_one(x2d, mod2, state, ng, wqkvg, wlr, wgk, bgk, gng, wout, rw_t):
    n = x2d.shape[0]
    consts = (ng, wqkvg, wlr, wgk, bgk)
    q, k, v, go, dec = pl.pallas_call(
        _gla1_proj_body,
        in_specs=[_const_spec(a.shape) for a in (x2d, mod2) + consts],
        out_specs=[_const_spec((n, GLA_DK_TOT)), _const_spec((n, GLA_DK_TOT)), _const_spec((n, GLA_DV_TOT)),
                   _const_spec((n, GLA_DV_TOT)), _const_spec((n, GLA_DK_TOT))],
        out_shape=[jax.ShapeDtypeStruct((n, GLA_DK_TOT), F32), jax.ShapeDtypeStruct((n, GLA_DK_TOT), F32),
                   jax.ShapeDtypeStruct((n, GLA_DV_TOT), F32), jax.ShapeDtypeStruct((n, GLA_DV_TOT), F32),
                   jax.ShapeDtypeStruct((n, GLA_DK_TOT), F32)],
        grid=(1,),
        compiler_params=_cparams("arbitrary"),
        name="gla1_proj",
    )(x2d, mod2, *consts)

    def cols(a):
        return a.reshape(n // GLA1_TOK, GLA1_TOK, GLA_HEADS, GLA_DK).transpose(0, 2, 3, 1)

    col_spec = pl.BlockSpec((None, GLA_HEADS, GLA_DK, GLA1_TOK), lambda i: (i, 0, 0, 0))
    st_spec = pl.BlockSpec((GLA1_TOK, GLA_HEADS, GLA_DK, GLA_DV), lambda i: (i, 0, 0, 0))
    new_state, o = pl.pallas_call(
        _gla1_state_body,
        grid=(n // GLA1_TOK,),
        in_specs=[st_spec, col_spec, col_spec, col_spec, pl.BlockSpec((GLA1_TOK, GLA_DV_TOT), lambda i: (i, 0))],
        out_specs=[st_spec, pl.BlockSpec((GLA1_TOK, GLA_DV_TOT), lambda i: (i, 0))],
        out_shape=[jax.ShapeDtypeStruct(state.shape, F32), jax.ShapeDtypeStruct((n, GLA_DV_TOT), F32)],
        compiler_params=_cparams("parallel"),
        name="gla1_state",
    )(state, cols(q), cols(k), cols(dec), v)

    consts = (mod2, ng, gng, wout, rw_t)
    x1, h, lg = pl.pallas_call(
        _gla1_out_body,
        grid=(1,),
        in_specs=[_const_spec(a.shape) for a in (x2d, o, go) + consts],
        out_specs=[_const_spec((n, D)), _const_spec((n, DP)), _const_spec((N_EXPERTS, n))],
        out_shape=[jax.ShapeDtypeStruct((n, D), F32), jax.ShapeDtypeStruct((n, DP), jnp.int32),
                   jax.ShapeDtypeStruct((N_EXPERTS, n), F32)],
        compiler_params=_cparams("arbitrary"),
        name="gla1_out",
    )(x2d, o, go, *consts)
    return x1, h, lg, new_state


def _moe_routed(h_p, h_s, lg_p, lg_s, router_bias, layer, wg, wu, wd):
    h, lg = h_p, lg_p
    if h_s is not None:
        h = jnp.concatenate([h_p, h_s], axis=0)
        lg = jnp.concatenate([lg_p, lg_s], axis=1)
    n = h.shape[0]
    n_pad = -(-n // TOKEN_PAD) * TOKEN_PAD
    if n_pad != n:
        h = jnp.pad(h, ((0, n_pad - n), (0, 0)))
        lg = jnp.pad(lg, ((0, 0), (0, n_pad - n)))
    eid, rank, wts, counts = _router(lg, router_bias, n)
    tile_count = ((counts[:, 0] + EXPERT_TILE - 1) // EXPERT_TILE).astype(jnp.int32)
    tile_end = jnp.cumsum(tile_count).astype(jnp.int32)
    tile_first = tile_end - tile_count
    off = tile_first * EXPERT_TILE
    p_alloc = TOP_K * n_pad + N_EXPERTS * EXPERT_TILE
    dest = _dest(off, eid, rank, n, p_alloc - 1)
    dest_w = dest.reshape(TOP_K, n_pad // DISPATCH_W, DISPATCH_W).transpose(1, 0, 2)
    xs = _sc_dispatch(h, dest_w, p_alloc)
    ys = _experts(xs, tile_first, tile_count, tile_end[-1:], layer, wg, wu, wd)
    y8 = _sc_gather(ys, dest.reshape(TOP_K * n_pad // GATHER_W, 1, GATHER_W))
    return y8.reshape(TOP_K, n_pad, DP), wts.T, h


def kernel(x_prompt, x_sample, state_gla, c_prompt, c_sample, norm_g, ada_w, ada_b, gm_w_in, gm_b_in,
           gm_ln_g, gm_ln_b, gm_w_s, gm_b_s, gm_w_out, gla_w_in, gla_w_gk, gla_b_gk, gla_norm_g,
           gla_w_out, router_w, router_bias, exp_w_gate, exp_w_up, exp_w_down, sh_w_gate, sh_w_up,
           sh_w_down):
    batch, seq, _ = x_prompt.shape
    n_s = x_sample.shape[0]
    n_p = batch * seq
    tpb = seq // MIX_TILE
    xp = x_prompt.reshape(n_p, D)
    xs = x_sample.reshape(n_s, D)

    mod = _ada(jnp.concatenate([c_prompt, c_sample], axis=0), ada_w, ada_b)
    mod_p = [mod[i, :batch].reshape(batch, 1, 6 * D) for i in range(2)]
    mod_s = [mod[i, batch:] for i in range(2)]
    rw_t = [router_w[i].T for i in range(2)]

    ws_causal = jnp.tril(gm_w_s[0]).astype(BF16)
    bs_cols = gm_b_s[0].T
    eye = jnp.eye(GM_CHUNK, dtype=F32)
    ws_first = (gm_w_s[0][:, 0, 0][:, None, None] * eye).astype(BF16)
    bs_first = jnp.broadcast_to(gm_b_s[0][:, 0][None, :], (GM_CHUNK, GM_GROUPS))
    gm_args = (norm_g[0], gm_w_in[0].astype(BF16), gm_b_in[0].reshape(1, -1), gm_ln_g[0].reshape(1, -1),
               gm_ln_b[0].reshape(1, -1))
    wout0 = gm_w_out[0].astype(BF16)
    shared = [(sh_w_gate[i].astype(BF16), sh_w_up[i].astype(BF16), sh_w_down[i].astype(BF16))
              for i in range(2)]
    n_qkvg = 2 * GLA_DK_TOT + 2 * GLA_DV_TOT
    wqkvg = gla_w_in[0][:, :n_qkvg].astype(BF16)
    wlr = jnp.pad(gla_w_in[0][:, n_qkvg:], ((0, 0), (0, LANES - GLA_GATE_RANK))).astype(BF16)
    wgk = jnp.pad(gla_w_gk[0], ((0, LANES - GLA_GATE_RANK), (0, 0))).astype(BF16)
    gla_args = (norm_g[1], wqkvg, wlr, wgk, gla_b_gk[0].reshape(1, -1), gla_norm_g[0].reshape(1, -1),
                gla_w_out[0].astype(BF16), rw_t[1])
    experts = (exp_w_gate, exp_w_up, exp_w_down)

    half = batch // 2
    streams = [(0, half, False), (half, batch - half, True)]
    st = [dict() for _ in streams]

    for s, (b0, nb, with_new) in zip(st, streams):
        s["mod_p"] = [mod_p[i][b0:b0 + nb] for i in range(2)]
        s["n"] = nb * seq
        s["x1p"], s["hp"], s["lgp"] = _gmlp_mixer(xp, b0 * tpb, s["n"], s["mod_p"][0], False, MIX_TILE, tpb,
                                                  *gm_args, ws_causal, bs_cols, wout0, rw_t[0], emit_v=False)
        s["hs"] = s["lgs"] = None
        if with_new:
            s["x1s"], s["hs"], s["lgs"], v_rows = _gmlp_mixer(xs, 0, n_s, mod_s[0], True, n_s, 1, *gm_args,
                                                              ws_first, bs_first, wout0, rw_t[0], emit_v=True)
    for s, (b0, nb, with_new) in zip(st, streams):
        moe = _moe_routed(s["hp"], s["hs"], s["lgp"], s["lgs"], router_bias[0], 0, *experts)
        s["x2p"] = _combine(s["x1p"], *moe, 0, s["mod_p"][0], False, MIX_TILE, tpb, norm_g[0], *shared[0])
        if with_new:
            s["x2s"] = _combine(s["x1s"], *moe, s["n"] // n_s, mod_s[0], True, n_s, 1, norm_g[0], *shared[0])
    for s, (b0, nb, with_new) in zip(st, streams):
        s["x3p"], s["hp"], s["lgp"], s["st_p"] = _gla_mixer(s["x2p"], s["mod_p"][1], nb, seq, *gla_args)
        if with_new:
            s["x3s"], s["hs"], s["lgs"], st_s = _gla_mixer_one(s["x2s"], mod_s[1], state_gla[:, 0], *gla_args)
    y_prompt = None
    for s, (b0, nb, with_new) in zip(st, streams):
        moe = _moe_routed(s["hp"], s["hs"], s["lgp"], s["lgs"], router_bias[1], 1, *experts)
        y_prompt = _combine(s["x3p"], *moe, 0, s["mod_p"][1], False, MIX_TILE, tpb, norm_g[1], *shared[1],
                            out_rows=n_p, out_blk0=b0 * tpb, out_buf=y_prompt)
        if with_new:
            y_new = _combine(s["x3s"], *moe, s["n"] // n_s, mod_s[1], True, n_s, 1, norm_g[1], *shared[1])
    st_p = jnp.concatenate([s["st_p"] for s in st], axis=0)

    return (y_prompt.reshape(batch, seq, D), y_new.reshape(n_s, 1, D), st_p[:, None], st_s[:, None],
            v_rows.reshape(n_s, 1, 1, GM_HALF))
```

```python
import functools
import math

import jax
import jax.numpy as jnp
from jax import lax
from jax.experimental import pallas as pl
from jax.experimental.pallas import tpu as pltpu
from jax.experimental.pallas import tpu_sc as plsc

F32 = jnp.float32
BF16 = jnp.bfloat16

D = 1024
DP = D // 2
GM_CHUNK = 128
GM_HALF = 2 * D
GM_GROUPS = 8
GM_GROUP_DIM = GM_HALF // GM_GROUPS
GLA_HEADS = 4
GLA_DK = 128
GLA_DV = 256
GLA_DK_TOT = GLA_HEADS * GLA_DK
GLA_DV_TOT = GLA_HEADS * GLA_DV
GLA_GATE_RANK = 16
GLA_GATE_NORMALIZER = 16.0
GLA_CHUNK = 64
N_EXPERTS = 64
TOP_K = 8
N_EXPERT_GROUPS = 8
GROUP_SIZE = N_EXPERTS // N_EXPERT_GROUPS
TOPK_GROUPS = 4
EXPERT_DIM = D // 4
ROUTED_SCALE = 2.5
NORM_EPS = 1e-6
LN_EPS = 1e-5

LANES = 128
VMEM_LIMIT = 56 * 1024 * 1024

MIX_TILE = 256
ROUTER_TILE = 512
EXPERT_TILE = 256
EXPERT_X_SLOTS = 6
EXPERT_AHEAD = EXPERT_X_SLOTS - 2
EXPERT_Y_SLOTS = 4
SC_WORKERS = 32
DISPATCH_W = 32
GATHER_W = 64
TOKEN_PAD = SC_WORKERS * DISPATCH_W


def _cparams(*sem):
    return pltpu.CompilerParams(dimension_semantics=sem, vmem_limit_bytes=VMEM_LIMIT)


def _rms(x, g):
    return x * lax.rsqrt(jnp.mean(x * x, axis=-1, keepdims=True) + NORM_EPS) * g


def _silu(x):
    return x * (1.0 / (1.0 + jnp.exp(-x)))


def _gelu(x):
    return 0.5 * x * (1.0 + lax.erf(x * (1.0 / math.sqrt(2.0))))


def _bdot(a, b):
    return jnp.dot(a.astype(BF16), b.astype(BF16), preferred_element_type=F32)


def _dot_nt(a, b, precision=None):
    return lax.dot_general(a, b, (((1,), (1,)), ((), ())), preferred_element_type=F32,
                           precision=precision)


def _mod_slices(mod_ref):
    return [mod_ref[:, i * D:(i + 1) * D] for i in range(6)]


HI_HALF = -65536


def _pack_rows(x):
    lo = lax.bitcast_convert_type(x[:, :DP].astype(BF16).astype(F32), jnp.int32)
    hi = lax.bitcast_convert_type(x[:, DP:].astype(BF16).astype(F32), jnp.int32)
    return lax.shift_right_logical(lo, 16) | (hi & HI_HALF)


def _unpack_rows(p):
    lo = lax.bitcast_convert_type(lax.shift_left(p, 16), F32)
    hi = lax.bitcast_convert_type(p & HI_HALF, F32)
    return lo, hi


def _ffn_prep(x1, ng, sh2, sc2, rw_ref, h_ref, lg_ref):
    hffn = _rms(x1, ng[2:3]) * (1.0 + sc2) + sh2
    h_ref[...] = _pack_rows(hffn)
    lg_ref[...] = _dot_nt(rw_ref[...], hffn, precision=lax.Precision.HIGHEST)


def _ada_body(c_ref, w_ref, b_ref, o_ref):
    c = c_ref[...]
    o_ref[...] = _bdot(_silu(c), w_ref[...]) + b_ref[...]


def _ada(c, ada_w, ada_b):
    n = c.shape[0]
    depth = ada_w.shape[0]
    tn = 1536
    return pl.pallas_call(
        _ada_body,
        grid=(depth, 6 * D // tn),
        in_specs=[pl.BlockSpec((n, D), lambda l, j: (0, 0)),
                  pl.BlockSpec((None, D, tn), lambda l, j: (l, 0, j)),
                  pl.BlockSpec((None, 1, tn), lambda l, j: (l, 0, j))],
        out_specs=pl.BlockSpec((None, n, tn), lambda l, j: (l, 0, j)),
        out_shape=jax.ShapeDtypeStruct((depth, n, 6 * D), F32),
        compiler_params=_cparams("parallel", "parallel"),
        name="ada_mod",
    )(c, ada_w, ada_b.reshape(depth, 1, 6 * D))


def _mod_spec(per_row, tt, tiles_per_batch):
    if per_row:
        return pl.BlockSpec((tt, 6 * D), lambda i: (i, 0))
    return pl.BlockSpec((None, 1, 6 * D), lambda i: (i // tiles_per_batch, 0, 0))


def _const_spec(shape):
    zeros = (0,) * len(shape)
    return pl.BlockSpec(shape, lambda *_: zeros)


def _gmlp_body(x_ref, mod_ref, ng_ref, win_ref, bin_ref, lng_ref, lnb_ref, ws_ref, bs_ref, wout_ref,
               rw_ref, x1_ref, h_ref, lg_ref, *rest, n_chunks, emit_v):
    if emit_v:
        v_ref, um_ref = rest
    else:
        (um_ref,) = rest
    sh1, sc1, g1, sh2, sc2, _ = _mod_slices(mod_ref)
    ng = ng_ref[...]
    x = x_ref[...]
    hb = (_rms(x, ng[0:1]) * (1.0 + sc1) + sh1).astype(BF16)
    u = _gelu(jnp.dot(hb, win_ref[:, :GM_HALF], preferred_element_type=F32) + bin_ref[:, :GM_HALF])
    v = _gelu(jnp.dot(hb, win_ref[:, GM_HALF:], preferred_element_type=F32) + bin_ref[:, GM_HALF:])
    mu = jnp.mean(v, axis=-1, keepdims=True)
    vc = v - mu
    var = jnp.mean(vc * vc, axis=-1, keepdims=True)
    v = vc * lax.rsqrt(var + LN_EPS) * lng_ref[...] + lnb_ref[...]
    if emit_v:
        v_ref[...] = v
    vb = v.astype(BF16)
    for c in range(n_chunks):
        rows = slice(c * GM_CHUNK, (c + 1) * GM_CHUNK)
        for g in range(GM_GROUPS):
            cols = slice(g * GM_GROUP_DIM, (g + 1) * GM_GROUP_DIM)
            mixed = jnp.dot(ws_ref[g], vb[rows, cols], preferred_element_type=F32) + bs_ref[:, g:g + 1]
            um_ref[rows, cols] = (u[rows, cols] * mixed).astype(BF16)
    y = jnp.dot(um_ref[...], wout_ref[...], preferred_element_type=F32)
    x1 = x + g1 * _rms(y, ng[1:2])
    x1_ref[...] = x1
    _ffn_prep(x1, ng, sh2, sc2, rw_ref, h_ref, lg_ref)


def _gmlp_mixer(x2d, blk0, n, mod, per_row, tt, tiles_per_batch, ng, win, b_in, ln_g, ln_b, ws, bs, wout,
                rw_t, emit_v):
    out_shape = [jax.ShapeDtypeStruct((n, D), F32), jax.ShapeDtypeStruct((n, DP), jnp.int32),
                 jax.ShapeDtypeStruct((N_EXPERTS, n), F32)]
    out_specs = [pl.BlockSpec((tt, D), lambda i: (i, 0)), pl.BlockSpec((tt, DP), lambda i: (i, 0)),
                 pl.BlockSpec((N_EXPERTS, tt), lambda i: (0, i))]
    if emit_v:
        out_shape.append(jax.ShapeDtypeStruct((n, GM_HALF), F32))
        out_specs.append(pl.BlockSpec((tt, GM_HALF), lambda i: (i, 0)))
    return pl.pallas_call(
        functools.partial(_gmlp_body, n_chunks=tt // GM_CHUNK, emit_v=emit_v),
        grid=(n // tt,),
        in_specs=[pl.BlockSpec((tt, D), lambda i: (i + blk0, 0)),
                  _mod_spec(per_row, tt, tiles_per_batch),
                  _const_spec(ng.shape), _const_spec(win.shape), _const_spec(b_in.shape),
                  _const_spec(ln_g.shape), _const_spec(ln_b.shape), _const_spec(ws.shape),
                  _const_spec(bs.shape), _const_spec(wout.shape), _const_spec(rw_t.shape)],
        out_specs=out_specs,
        out_shape=out_shape,
        scratch_shapes=[pltpu.VMEM((tt, GM_HALF), BF16)],
        compiler_params=_cparams("parallel"),
        name="gmlp_mixer_rows" if per_row else "gmlp_mixer",
    )(x2d, mod, ng, win, b_in, ln_g, ln_b, ws, bs, wout, rw_t)


def _combine_body(x_ref, *refs):
    y_refs = refs[:TOP_K]
    w_ref, h_ref, mod_ref, ng_ref, swg_ref, swu_ref, swd_ref = refs[TOP_K:TOP_K + 7]
    o_ref = refs[-1]
    h_lo, h_hi = _unpack_rows(h_ref[...])
    h_lo = h_lo.astype(BF16)
    h_hi = h_hi.astype(BF16)

    def hdot(w_ref_):
        return (jnp.dot(h_lo, w_ref_[:DP, :], preferred_element_type=F32) +
                jnp.dot(h_hi, w_ref_[DP:, :], preferred_element_type=F32))

    hs = (_silu(hdot(swg_ref)) * hdot(swu_ref)).astype(BF16)
    y = jnp.dot(hs, swd_ref[...], preferred_element_type=F32)
    w = w_ref[...]
    acc_lo = jnp.zeros((x_ref.shape[0], DP), F32)
    acc_hi = jnp.zeros((x_ref.shape[0], DP), F32)
    for k in range(TOP_K):
        lo, hi = _unpack_rows(y_refs[k][...])
        acc_lo += w[:, k:k + 1] * lo
        acc_hi += w[:, k:k + 1] * hi
    y = y + jnp.concatenate([acc_lo, acc_hi], axis=1)
    g2 = mod_ref[:, 5 * D:6 * D]
    o_ref[...] = x_ref[...] + g2 * _rms(y, ng_ref[3:4, :])


def _combine(x2d, y8, wts_t, hp, blk0, mod, per_row, tt, tiles_per_batch, ng, swg, swu, swd,
             out_rows=None, out_blk0=0, out_buf=None):
    n = x2d.shape[0]
    y_specs = [pl.BlockSpec((None, tt, DP), functools.partial(lambda i, k: (k, i + blk0, 0), k=k))
               for k in range(TOP_K)]
    in_specs = ([pl.BlockSpec((tt, D), lambda i: (i, 0))] + y_specs +
                [pl.BlockSpec((tt, TOP_K), lambda i: (i + blk0, 0)),
                 pl.BlockSpec((tt, DP), lambda i: (i + blk0, 0)),
                 _mod_spec(per_row, tt, tiles_per_batch),
                 _const_spec(ng.shape), _const_spec(swg.shape), _const_spec(swu.shape),
                 _const_spec(swd.shape)])
    args = [x2d] + [y8] * TOP_K + [wts_t, hp, mod, ng, swg, swu, swd]
    aliases = {}
    if out_buf is not None:
        in_specs.append(pl.BlockSpec(memory_space=pl.ANY))
        aliases = {len(args): 0}
        args.append(out_buf)
    return pl.pallas_call(
        _combine_body,
        grid=(n // tt,),
        in_specs=in_specs,
        out_specs=pl.BlockSpec((tt, D), lambda i: (i + out_blk0, 0)),
        out_shape=jax.ShapeDtypeStruct((out_rows or n, D), F32),
        input_output_aliases=aliases,
        compiler_params=_cparams("parallel"),
        name="combine_rows" if per_row else "combine",
    )(*args)


def _router_body(lg_ref, bias_ref, tri_ref, eid_ref, rank_ref, wts_ref, cnt_ref, carry_ref, *, n_real):
    step = pl.program_id(0)

    @pl.when(step == 0)
    def _():
        carry_ref[...] = jnp.zeros_like(carry_ref)

    lg = lg_ref[...]
    tn = lg.shape[1]
    real = (step * tn + lax.broadcasted_iota(jnp.int32, (1, tn), 1)) < n_real
    lg = jnp.where(real, lg, 0.0)
    scores = 1.0 / (1.0 + jnp.exp(-lg))
    sel = scores + bias_ref[...]
    neg = -jnp.inf
    sub8 = lax.broadcasted_iota(jnp.int32, (GROUP_SIZE, tn), 0)
    gsub = lax.broadcasted_iota(jnp.int32, (N_EXPERT_GROUPS, tn), 0)
    gs = jnp.zeros((N_EXPERT_GROUPS, tn), F32)
    for g in range(N_EXPERT_GROUPS):
        blk = sel[g * GROUP_SIZE:(g + 1) * GROUP_SIZE, :]
        m1 = jnp.max(blk, axis=0, keepdims=True)
        i1 = jnp.min(jnp.where(blk == m1, sub8, GROUP_SIZE), axis=0, keepdims=True)
        m2 = jnp.max(jnp.where(sub8 == i1, neg, blk), axis=0, keepdims=True)
        gs = jnp.where(gsub == g, m1 + m2, gs)
    gmask = jnp.zeros((N_EXPERT_GROUPS, tn), jnp.bool_)
    for _ in range(TOPK_GROUPS):
        m = jnp.max(gs, axis=0, keepdims=True)
        i = jnp.min(jnp.where(gs == m, gsub, N_EXPERT_GROUPS), axis=0, keepdims=True)
        hit = gsub == i
        gmask = jnp.logical_or(gmask, hit)
        gs = jnp.where(hit, neg, gs)
    gmaskf = gmask.astype(F32)
    blocks = []
    for g in range(N_EXPERT_GROUPS):
        keep = jnp.broadcast_to(gmaskf[g:g + 1, :], (GROUP_SIZE, tn)) > 0.5
        blocks.append(jnp.where(keep, sel[g * GROUP_SIZE:(g + 1) * GROUP_SIZE, :], neg))
    msel = jnp.concatenate(blocks, axis=0)
    esub = lax.broadcasted_iota(jnp.int32, (N_EXPERTS, tn), 0)
    chosen = jnp.zeros((N_EXPERTS, tn), jnp.bool_)
    picks = []
    for _ in range(TOP_K):
        m = jnp.max(msel, axis=0, keepdims=True)
        i = jnp.min(jnp.where(msel == m, esub, N_EXPERTS), axis=0, keepdims=True)
        hit = esub == i
        picks.append(i)
        chosen = jnp.logical_or(chosen, hit)
        msel = jnp.where(hit, neg, msel)
    w = jnp.where(chosen, scores, 0.0)
    w = w / jnp.sum(w, axis=0, keepdims=True) * ROUTED_SCALE
    counted = jnp.where(jnp.logical_and(chosen, real), 1.0, 0.0)
    incl = jnp.dot(counted.astype(BF16), tri_ref[...], preferred_element_type=F32)
    rank_full = carry_ref[:, 0:1] + incl - 1.0
    ksub = lax.broadcasted_iota(jnp.int32, (TOP_K, tn), 0)
    eid = jnp.zeros((TOP_K, tn), jnp.int32)
    rank = jnp.zeros((TOP_K, tn), F32)
    wts = jnp.zeros((TOP_K, tn), F32)
    for k in range(TOP_K):
        hit = esub == picks[k]
        eid = jnp.where(ksub == k, picks[k], eid)
        rank = jnp.where(ksub == k, jnp.sum(jnp.where(hit, rank_full, 0.0), axis=0, keepdims=True), rank)
        wts = jnp.where(ksub == k, jnp.sum(jnp.where(hit, w, 0.0), axis=0, keepdims=True), wts)
    eid_ref[...] = eid
    rank_ref[...] = rank.astype(jnp.int32)
    wts_ref[...] = wts
    carry = carry_ref[...] + incl[:, tn - 1:tn]
    carry_ref[...] = carry
    cnt_ref[...] = carry.astype(jnp.int32)


def _router(lg_t, bias, n_real):
    n = lg_t.shape[1]
    tn = ROUTER_TILE
    idx = jnp.arange(tn)
    tri = (idx[:, None] <= idx[None, :]).astype(BF16)
    kspec = pl.BlockSpec((TOP_K, tn), lambda i: (0, i))
    return pl.pallas_call(
        functools.partial(_router_body, n_real=n_real),
        grid=(n // tn,),
        in_specs=[pl.BlockSpec((N_EXPERTS, tn), lambda i: (0, i)), _const_spec((N_EXPERTS, 1)),
                  _const_spec((tn, tn))],
        out_specs=[kspec, kspec, kspec, _const_spec((N_EXPERTS, LANES))],
        out_shape=[jax.ShapeDtypeStruct((TOP_K, n), jnp.int32), jax.ShapeDtypeStruct((TOP_K, n), jnp.int32),
                   jax.ShapeDtypeStruct((TOP_K, n), F32), jax.ShapeDtypeStruct((N_EXPERTS, LANES), jnp.int32)],
        scratch_shapes=[pltpu.VMEM((N_EXPERTS, LANES), F32)],
        compiler_params=_cparams("arbitrary"),
        name="router",
    )(lg_t, bias.reshape(N_EXPERTS, 1), tri)


def _dest_body(off_ref, eid_ref, rank_ref, dest_ref, *, n_real, last_row):
    eid = eid_ref[...]
    base = jnp.zeros(eid.shape, jnp.int32)
    for e in range(N_EXPERTS):
        base = jnp.where(eid == e, off_ref[e], base)
    tok = lax.broadcasted_iota(jnp.int32, eid.shape, 1)
    slot = lax.broadcasted_iota(jnp.int32, eid.shape, 0)
    unused = last_row - ((tok - n_real) * TOP_K + slot)
    dest_ref[...] = jnp.where(tok < n_real, base + rank_ref[...], unused)


def _dest(off, eid, rank, n_real, last_row):
    spec = pl.BlockSpec(eid.shape, lambda i, off_ref: (0, 0))
    return pl.pallas_call(
        functools.partial(_dest_body, n_real=n_real, last_row=last_row),
        grid_spec=pltpu.PrefetchScalarGridSpec(num_scalar_prefetch=1, grid=(1,), in_specs=[spec, spec],
                                               out_specs=spec),
        out_shape=jax.ShapeDtypeStruct(eid.shape, jnp.int32),
        compiler_params=_cparams("arbitrary"),
        name="dest_rows",
    )(off, eid, rank)


def _sc_mesh():
    return plsc.VectorSubcoreMesh(core_axis_name="core", subcore_axis_name="subcore")


def _sc_dispatch(hp, dest_w, p_alloc):
    n = hp.shape[0]
    w = dest_w.shape[2]

    @functools.partial(pl.kernel, out_type=jax.ShapeDtypeStruct((p_alloc, DP), jnp.int32), mesh=_sc_mesh(),
                       name="sc_dispatch")
    def run(hp_hbm, dest_hbm, xs_hbm):
        def body(x_vmem, i_vmem):
            for k in range(TOP_K):
                pltpu.sync_copy(x_vmem, xs_hbm.at[i_vmem.at[k]])

        pltpu.emit_pipeline(
            body,
            grid=(n // w,),
            in_specs=[pl.BlockSpec((w, DP), lambda i: (i, 0)),
                      pl.BlockSpec((None, TOP_K, w), lambda i: (i, 0, 0))],
            out_specs=[],
            core_axis_name=("core", "subcore"),
            dimension_semantics=(pltpu.PARALLEL,),
        )(hp_hbm, dest_hbm)

    return run(hp, dest_w)


def _sc_gather(ys, dest_g):
    g, _, w = dest_g.shape

    @functools.partial(pl.kernel, out_type=jax.ShapeDtypeStruct((g * w, DP), jnp.int32), mesh=_sc_mesh(),
                       name="sc_gather")
    def run(ys_hbm, dest_hbm, o_hbm):
        def body(i_vmem, o_vmem):
            pltpu.sync_copy(ys_hbm.at[i_vmem.at[0]], o_vmem)

        pltpu.emit_pipeline(
            body,
            grid=(g,),
            in_specs=[pl.BlockSpec((None, 1, w), lambda i: (i, 0, 0))],
            out_specs=[pl.BlockSpec((w, DP), lambda i: (i, 0))],
            core_axis_name=("core", "subcore"),
            dimension_semantics=(pltpu.PARALLEL,),
        )(dest_hbm, o_hbm)

    return run(ys, dest_g)


def _expert_body(first_ref, cnt_ref, nused_ref, xs_hbm, wg_ref, wu_ref, wd_ref, ys_hbm,
                 wg_s, wu_s, wd_s, xbuf, ybuf, xsem, ysem):
    e = pl.program_id(0)
    n_used = nused_ref[0]

    def load(g):
        rows = pl.ds(pl.multiple_of(g * EXPERT_TILE, EXPERT_TILE), EXPERT_TILE)
        slot = g % EXPERT_X_SLOTS
        return pltpu.make_async_copy(xs_hbm.at[rows], xbuf.at[slot], xsem.at[slot])

    def store(g):
        rows = pl.ds(pl.multiple_of(g * EXPERT_TILE, EXPERT_TILE), EXPERT_TILE)
        slot = g % EXPERT_Y_SLOTS
        return pltpu.make_async_copy(ybuf.at[slot], ys_hbm.at[rows], ysem.at[slot])

    @pl.when(e == 0)
    def _():
        for g in range(EXPERT_AHEAD):
            @pl.when(g < n_used)
            def _():
                load(g).start()

    wg_s[...] = wg_ref[...].astype(BF16)
    wu_s[...] = wu_ref[...].astype(BF16)
    wd_s[...] = wd_ref[...].astype(BF16)
    first = first_ref[e]
    cnt = cnt_ref[e]

    def acquire(g):
        ahead = g + EXPERT_AHEAD

        @pl.when(ahead < n_used)
        def _():
            load(ahead).start()

        load(g).wait()

        @pl.when(g >= EXPERT_Y_SLOTS)
        def _():
            store(g - EXPERT_Y_SLOTS).wait()

    def compute(g):
        lo, hi = _unpack_rows(xbuf[g % EXPERT_X_SLOTS])
        lo = lo.astype(BF16)
        hi = hi.astype(BF16)

        def xdot(w_s):
            return (jnp.dot(lo, w_s[:DP, :], preferred_element_type=F32) +
                    jnp.dot(hi, w_s[DP:, :], preferred_element_type=F32))

        a = (_silu(xdot(wg_s)) * xdot(wu_s)).astype(BF16)
        ybuf[g % EXPERT_Y_SLOTS] = _pack_rows(jnp.dot(a, wd_s[...], preferred_element_type=F32))

    def pair(j, carry):
        g = first + 2 * j
        acquire(g)
        acquire(g + 1)
        compute(g)
        compute(g + 1)
        store(g).start()
        store(g + 1).start()
        return carry

    lax.fori_loop(0, cnt // 2, pair, 0)

    @pl.when(cnt % 2 == 1)
    def _():
        g = first + cnt - 1
        acquire(g)
        compute(g)
        store(g).start()

    @pl.when(e == N_EXPERTS - 1)
    def _():
        for k in range(EXPERT_Y_SLOTS):
            g = n_used - 1 - k

            @pl.when(g >= 0)
            def _():
                store(g).wait()


def _experts(xs, tile_first, tile_count, n_used, layer, wg, wu, wd):
    def w_map(e, first, cnt, nu):
        return (layer, e, 0, 0)

    return pl.pallas_call(
        _expert_body,
        grid_spec=pltpu.PrefetchScalarGridSpec(
            num_scalar_prefetch=3, grid=(N_EXPERTS,),
            in_specs=[pl.BlockSpec(memory_space=pl.ANY),
                      pl.BlockSpec((None, None, D, EXPERT_DIM), w_map),
                      pl.BlockSpec((None, None, D, EXPERT_DIM), w_map),
                      pl.BlockSpec((None, None, EXPERT_DIM, D), w_map)],
            out_specs=pl.BlockSpec(memory_space=pl.ANY),
            scratch_shapes=[pltpu.VMEM((D, EXPERT_DIM), BF16), pltpu.VMEM((D, EXPERT_DIM), BF16),
                            pltpu.VMEM((EXPERT_DIM, D), BF16),
                            pltpu.VMEM((EXPERT_X_SLOTS, EXPERT_TILE, DP), jnp.int32),
                            pltpu.VMEM((EXPERT_Y_SLOTS, EXPERT_TILE, DP), jnp.int32),
                            pltpu.SemaphoreType.DMA((EXPERT_X_SLOTS,)),
                            pltpu.SemaphoreType.DMA((EXPERT_Y_SLOTS,))]),
        out_shape=jax.ShapeDtypeStruct(xs.shape, jnp.int32),
        compiler_params=_cparams("arbitrary"),
        name="experts",
    )(tile_first, tile_count, n_used, xs, wg, wu, wd)


def _log_sigmoid(z):
    return jnp.minimum(z, 0.0) - jnp.log(1.0 + jnp.exp(-jnp.abs(z)))


def _gla_gate(hb, wlr_ref, wgk_ref, bgk_ref):
    lr = jnp.dot(hb, wlr_ref[...], preferred_element_type=F32)
    z = _bdot(lr, wgk_ref[...]) + bgk_ref[...]
    return _log_sigmoid(z) * (1.0 / GLA_GATE_NORMALIZER)


def _split3(a):
    hi = a.astype(BF16)
    r1 = a - hi.astype(F32)
    mid = r1.astype(BF16)
    lo = (r1 - mid.astype(F32)).astype(BF16)
    return hi, mid, lo


def _gla_out(o_ref_val, go, gng):
    parts = []
    for hd in range(GLA_HEADS):
        cols = slice(hd * GLA_DV, (hd + 1) * GLA_DV)
        parts.append((_rms(o_ref_val[:, cols], gng) * _silu(go[:, cols])).astype(BF16))
    return jnp.concatenate(parts, axis=1)


def _gla_body(x_ref, mod_ref, ng_ref, wqkvg_ref, wlr_ref, wgk_ref, bgk_ref, tril_ref, gng_ref, wout_ref,
              rw_ref, x1_ref, h_ref, lg_ref, st_ref, st_scr, o_scr, *, tt):
    j = pl.program_id(1)

    @pl.when(j == 0)
    def _():
        st_scr[...] = jnp.zeros_like(st_scr)

    sh1, sc1, g1, sh2, sc2, _ = _mod_slices(mod_ref)
    ng = ng_ref[...]
    x = x_ref[...]
    hb = (_rms(x, ng[0:1]) * (1.0 + sc1) + sh1).astype(BF16)
    proj = jnp.dot(hb, wqkvg_ref[...], preferred_element_type=F32)
    q = proj[:, :GLA_DK_TOT] * (GLA_DK ** -0.5)
    k = proj[:, GLA_DK_TOT:2 * GLA_DK_TOT]
    v = proj[:, 2 * GLA_DK_TOT:2 * GLA_DK_TOT + GLA_DV_TOT].astype(BF16)
    go = proj[:, 2 * GLA_DK_TOT + GLA_DV_TOT:]
    log_a = _gla_gate(hb, wlr_ref, wgk_ref, bgk_ref)
    tril = tril_ref[...]
    b = sum(jnp.dot(tril, part, preferred_element_type=F32) for part in _split3(log_a))
    row = lax.broadcasted_iota(jnp.int32, (GLA_CHUNK, GLA_CHUNK), 0)
    col = lax.broadcasted_iota(jnp.int32, (GLA_CHUNK, GLA_CHUNK), 1)
    causal = row >= col
    for c in range(tt // GLA_CHUNK):
        rows = slice(c * GLA_CHUNK, (c + 1) * GLA_CHUNK)
        last = (c + 1) * GLA_CHUNK - 1
        for hd in range(GLA_HEADS):
            kc = slice(hd * GLA_DK, (hd + 1) * GLA_DK)
            vc = slice(hd * GLA_DV, (hd + 1) * GLA_DV)
            bb = b[rows, kc]
            b_last = b[last:last + 1, kc]
            q_dec = (q[rows, kc] * jnp.exp(bb)).astype(BF16)
            k_inv = (k[rows, kc] * jnp.exp(-bb)).astype(BF16)
            k_end = (k[rows, kc] * jnp.exp(b_last - bb)).astype(BF16)
            att = jnp.where(causal, _dot_nt(q_dec, k_inv), 0.0).astype(BF16)
            st = st_scr[hd]
            o = jnp.dot(att, v[rows, vc], preferred_element_type=F32) + _dot_nt(q_dec, st.astype(BF16))
            o_scr[rows, vc] = o
            d_st = lax.dot_general(v[rows, vc], k_end, (((0,), (0,)), ((), ())),
                                   preferred_element_type=F32)
            st_scr[hd] = st * jnp.exp(b_last) + d_st

    @pl.when(j == pl.num_programs(1) - 1)
    def _():
        for hd in range(GLA_HEADS):
            st_ref[hd] = st_scr[hd].T

    y = jnp.dot(_gla_out(o_scr[...], go, gng_ref[...]), wout_ref[...], preferred_element_type=F32)
    x1 = x + g1 * _rms(y, ng[1:2])
    x1_ref[...] = x1
    _ffn_prep(x1, ng, sh2, sc2, rw_ref, h_ref, lg_ref)


def _gla_mixer(x2d, mod3, batch, seq, ng, wqkvg, wlr, wgk, bgk, gng, wout, rw_t):
    tt = MIX_TILE
    tpb = seq // tt
    n = x2d.shape[0]
    idx = jnp.arange(tt)
    tril = ((idx[:, None] >= idx[None, :]) &
            (idx[:, None] // GLA_CHUNK == idx[None, :] // GLA_CHUNK)).astype(BF16)
    row_map = lambda b, j: (b * tpb + j, 0)
    consts = (ng, wqkvg, wlr, wgk, bgk, tril, gng, wout, rw_t)
    return pl.pallas_call(
        functools.partial(_gla_body, tt=tt),
        grid=(batch, tpb),
        in_specs=[pl.BlockSpec((tt, D), row_map),
                  pl.BlockSpec((None, 1, 6 * D), lambda b, j: (b, 0, 0))] +
                 [_const_spec(a.shape) for a in consts],
        out_specs=[pl.BlockSpec((tt, D), row_map), pl.BlockSpec((tt, DP), row_map),
                   pl.BlockSpec((N_EXPERTS, tt), lambda b, j: (0, b * tpb + j)),
                   pl.BlockSpec((None, GLA_HEADS, GLA_DK, GLA_DV), lambda b, j: (b, 0, 0, 0))],
        out_shape=[jax.ShapeDtypeStruct((n, D), F32), jax.ShapeDtypeStruct((n, DP), jnp.int32),
                   jax.ShapeDtypeStruct((N_EXPERTS, n), F32),
                   jax.ShapeDtypeStruct((batch, GLA_HEADS, GLA_DK, GLA_DV), F32)],
        scratch_shapes=[pltpu.VMEM((GLA_HEADS, GLA_DV, GLA_DK), F32),
                        pltpu.VMEM((tt, GLA_DV_TOT), F32)],
        compiler_params=_cparams("parallel", "arbitrary"),
        name="gla_mixer",
    )(x2d, mod3, *consts)


def _gla1_proj_body(x_ref, mod_ref, ng_ref, wqkvg_ref, wlr_ref, wgk_ref, bgk_ref,
                    q_ref, k_ref, v_ref, go_ref, dec_ref):
    sh1, sc1, _, _, _, _ = _mod_slices(mod_ref)
    ng = ng_ref[...]
    hb = (_rms(x_ref[...], ng[0:1]) * (1.0 + sc1) + sh1).astype(BF16)
    proj = jnp.dot(hb, wqkvg_ref[...], preferred_element_type=F32)
    q_ref[...] = proj[:, :GLA_DK_TOT] * (GLA_DK ** -0.5)
    k_ref[...] = proj[:, GLA_DK_TOT:2 * GLA_DK_TOT]
    v_ref[...] = proj[:, 2 * GLA_DK_TOT:2 * GLA_DK_TOT + GLA_DV_TOT]
    go_ref[...] = proj[:, 2 * GLA_DK_TOT + GLA_DV_TOT:]
    dec_ref[...] = jnp.exp(_gla_gate(hb, wlr_ref, wgk_ref, bgk_ref))


GLA1_TOK = 8


def _gla1_state_body(st_ref, qc_ref, kc_ref, dc_ref, v_ref, nst_ref, o_ref):
    v = v_ref[...]
    for i in range(GLA1_TOK):
        for hd in range(GLA_HEADS):
            vrow = v[i:i + 1, hd * GLA_DV:(hd + 1) * GLA_DV]
            s_new = dc_ref[hd][:, i:i + 1] * st_ref[i, hd] + kc_ref[hd][:, i:i + 1] * vrow
            nst_ref[i, hd] = s_new
            o_ref[i:i + 1, hd * GLA_DV:(hd + 1) * GLA_DV] = jnp.sum(
                qc_ref[hd][:, i:i + 1] * s_new, axis=0, keepdims=True)


def _gla1_out_body(x_ref, o_ref, go_ref, mod_ref, ng_ref, gng_ref, wout_ref, rw_ref, x1_ref, h_ref, lg_ref):
    _, _, g1, sh2, sc2, _ = _mod_slices(mod_ref)
    ng = ng_ref[...]
    y = jnp.dot(_gla_out(o_ref[...], go_ref[...], gng_ref[...]), wout_ref[...], preferred_element_type=F32)
    x1 = x_ref[...] + g1 * _rms(y, ng[1:2])
    x1_ref[...] = x1
    _ffn_prep(x1, ng, sh2, sc2, rw_ref, h_ref, lg_ref)


def _gla_mixer_one(x2d, mod2, state, ng, wqkvg, wlr, wgk, bgk, gng, wout, rw_t):
    n = x2d.shape[0]
    consts = (ng, wqkvg, wlr, wgk, bgk)
    q, k, v, go, dec = pl.pallas_call(
        _gla1_proj_body,
        in_specs=[_const_spec(a.shape) for a in (x2d, mod2) + consts],
        out_specs=[_const_spec((n, GLA_DK_TOT)), _const_spec((n, GLA_DK_TOT)), _const_spec((n, GLA_DV_TOT)),
                   _const_spec((n, GLA_DV_TOT)), _const_spec((n, GLA_DK_TOT))],
        out_shape=[jax.ShapeDtypeStruct((n, GLA_DK_TOT), F32), jax.ShapeDtypeStruct((n, GLA_DK_TOT), F32),
                   jax.ShapeDtypeStruct((n, GLA_DV_TOT), F32), jax.ShapeDtypeStruct((n, GLA_DV_TOT), F32),
                   jax.ShapeDtypeStruct((n, GLA_DK_TOT), F32)],
        grid=(1,),
        compiler_params=_cparams("arbitrary"),
        name="gla1_proj",
    )(x2d, mod2, *consts)

    def cols(a):
        return a.reshape(n // GLA1_TOK, GLA1_TOK, GLA_HEADS, GLA_DK).transpose(0, 2, 3, 1)

    col_spec = pl.BlockSpec((None, GLA_HEADS, GLA_DK, GLA1_TOK), lambda i: (i, 0, 0, 0))
    st_spec = pl.BlockSpec((GLA1_TOK, GLA_HEADS, GLA_DK, GLA_DV), lambda i: (i, 0, 0, 0))
    new_state, o = pl.pallas_call(
        _gla1_state_body,
        grid=(n // GLA1_TOK,),
        in_specs=[st_spec, col_spec, col_spec, col_spec, pl.BlockSpec((GLA1_TOK, GLA_DV_TOT), lambda i: (i, 0))],
        out_specs=[st_spec, pl.BlockSpec((GLA1_TOK, GLA_DV_TOT), lambda i: (i, 0))],
        out_shape=[jax.ShapeDtypeStruct(state.shape, F32), jax.ShapeDtypeStruct((n, GLA_DV_TOT), F32)],
        compiler_params=_cparams("parallel"),
        name="gla1_state",
    )(state, cols(q), cols(k), cols(dec), v)

    consts = (mod2, ng, gng, wout, rw_t)
    x1, h, lg = pl.pallas_call(
        _gla1_out_body,
        grid=(1,),
        in_specs=[_const_spec(a.shape) for a in (x2d, o, go) + consts],
        out_specs=[_const_spec((n, D)), _const_spec((n, DP)), _const_spec((N_EXPERTS, n))],
        out_shape=[jax.ShapeDtypeStruct((n, D), F32), jax.ShapeDtypeStruct((n, DP), jnp.int32),
                   jax.ShapeDtypeStruct((N_EXPERTS, n), F32)],
        compiler_params=_cparams("arbitrary"),
        name="gla1_out",
    )(x2d, o, go, *consts)
    return x1, h, lg, new_state


def _moe_routed(h_p, h_s, lg_p, lg_s, router_bias, layer, wg, wu, wd):
    h, lg = h_p, lg_p
    if h_s is not None:
        h = jnp.concatenate([h_p, h_s], axis=0)
        lg = jnp.concatenate([lg_p, lg_s], axis=1)
    n = h.shape[0]
    n_pad = -(-n // TOKEN_PAD) * TOKEN_PAD
    if n_pad != n:
        h = jnp.pad(h, ((0, n_pad - n), (0, 0)))
        lg = jnp.pad(lg, ((0, 0), (0, n_pad - n)))
    eid, rank, wts, counts = _router(lg, router_bias, n)
    tile_count = ((counts[:, 0] + EXPERT_TILE - 1) // EXPERT_TILE).astype(jnp.int32)
    tile_end = jnp.cumsum(tile_count).astype(jnp.int32)
    tile_first = tile_end - tile_count
    off = tile_first * EXPERT_TILE
    p_alloc = TOP_K * n_pad + N_EXPERTS * EXPERT_TILE
    dest = _dest(off, eid, rank, n, p_alloc - 1)
    dest_w = dest.reshape(TOP_K, n_pad // DISPATCH_W, DISPATCH_W).transpose(1, 0, 2)
    xs = _sc_dispatch(h, dest_w, p_alloc)
    ys = _experts(xs, tile_first, tile_count, tile_end[-1:], layer, wg, wu, wd)
    y8 = _sc_gather(ys, dest.reshape(TOP_K * n_pad // GATHER_W, 1, GATHER_W))
    return y8.reshape(TOP_K, n_pad, DP), wts.T, h


def kernel(x_prompt, x_sample, state_gla, c_prompt, c_sample, norm_g, ada_w, ada_b, gm_w_in, gm_b_in,
           gm_ln_g, gm_ln_b, gm_w_s, gm_b_s, gm_w_out, gla_w_in, gla_w_gk, gla_b_gk, gla_norm_g,
           gla_w_out, router_w, router_bias, exp_w_gate, exp_w_up, exp_w_down, sh_w_gate, sh_w_up,
           sh_w_down):
    batch, seq, _ = x_prompt.shape
    n_s = x_sample.shape[0]
    n_p = batch * seq
    tpb = seq // MIX_TILE
    xp = x_prompt.reshape(n_p, D)
    xs = x_sample.reshape(n_s, D)

    mod = _ada(jnp.concatenate([c_prompt, c_sample], axis=0), ada_w, ada_b)
    mod_p = [mod[i, :batch].reshape(batch, 1, 6 * D) for i in range(2)]
    mod_s = [mod[i, batch:] for i in range(2)]
    rw_t = [router_w[i].T for i in range(2)]

    ws_causal = jnp.tril(gm_w_s[0]).astype(BF16)
    bs_cols = gm_b_s[0].T
    eye = jnp.eye(GM_CHUNK, dtype=F32)
    ws_first = (gm_w_s[0][:, 0, 0][:, None, None] * eye).astype(BF16)
    bs_first = jnp.broadcast_to(gm_b_s[0][:, 0][None, :], (GM_CHUNK, GM_GROUPS))
    gm_args = (norm_g[0], gm_w_in[0].astype(BF16), gm_b_in[0].reshape(1, -1), gm_ln_g[0].reshape(1, -1),
               gm_ln_b[0].reshape(1, -1))
    wout0 = gm_w_out[0].astype(BF16)
    shared = [(sh_w_gate[i].astype(BF16), sh_w_up[i].astype(BF16), sh_w_down[i].astype(BF16))
              for i in range(2)]
    n_qkvg = 2 * GLA_DK_TOT + 2 * GLA_DV_TOT
    wqkvg = gla_w_in[0][:, :n_qkvg].astype(BF16)
    wlr = jnp.pad(gla_w_in[0][:, n_qkvg:], ((0, 0), (0, LANES - GLA_GATE_RANK))).astype(BF16)
    wgk = jnp.pad(gla_w_gk[0], ((0, LANES - GLA_GATE_RANK), (0, 0))).astype(BF16)
    gla_args = (norm_g[1], wqkvg, wlr, wgk, gla_b_gk[0].reshape(1, -1), gla_norm_g[0].reshape(1, -1),
                gla_w_out[0].astype(BF16), rw_t[1])
    experts = (exp_w_gate, exp_w_up, exp_w_down)

    half = batch // 2
    streams = [(0, half, False), (half, batch - half, True)]
    st = [dict() for _ in streams]

    for s, (b0, nb, with_new) in zip(st, streams):
        s["mod_p"] = [mod_p[i][b0:b0 + nb] for i in range(2)]
        s["n"] = nb * seq
        s["x1p"], s["hp"], s["lgp"] = _gmlp_mixer(xp, b0 * tpb, s["n"], s["mod_p"][0], False, MIX_TILE, tpb,
                                                  *gm_args, ws_causal, bs_cols, wout0, rw_t[0], emit_v=False)
        s["hs"] = s["lgs"] = None
        if with_new:
            s["x1s"], s["hs"], s["lgs"], v_rows = _gmlp_mixer(xs, 0, n_s, mod_s[0], True, n_s, 1, *gm_args,
                                                              ws_first, bs_first, wout0, rw_t[0], emit_v=True)
    for s, (b0, nb, with_new) in zip(st, streams):
        moe = _moe_routed(s["hp"], s["hs"], s["lgp"], s["lgs"], router_bias[0], 0, *experts)
        s["x2p"] = _combine(s["x1p"], *moe, 0, s["mod_p"][0], False, MIX_TILE, tpb, norm_g[0], *shared[0])
        if with_new:
            s["x2s"] = _combine(s["x1s"], *moe, s["n"] // n_s, mod_s[0], True, n_s, 1, norm_g[0], *shared[0])
    for s, (b0, nb, with_new) in zip(st, streams):
        s["x3p"], s["hp"], s["lgp"], s["st_p"] = _gla_mixer(s["x2p"], s["mod_p"][1], nb, seq, *gla_args)
        if with_new:
            s["x3s"], s["hs"], s["lgs"], st_s = _gla_mixer_one(s["x2s"], mod_s[1], state_gla[:, 0], *gla_args)
    y_prompt = None
    for s, (b0, nb, with_new) in zip(st, streams):
        moe = _moe_routed(s["hp"], s["hs"], s["lgp"], s["lgs"], router_bias[1], 1, *experts)
        y_prompt = _combine(s["x3p"], *moe, 0, s["mod_p"][1], False, MIX_TILE, tpb, norm_g[1], *shared[1],
                            out_rows=n_p, out_blk0=b0 * tpb, out_buf=y_prompt)
        if with_new:
            y_new = _combine(s["x3s"], *moe, s["n"] // n_s, mod_s[1], True, n_s, 1, norm_g[1], *shared[1])
    st_p = jnp.concatenate([s["st_p"] for s in st], axis=0)

    return (y_prompt.reshape(batch, seq, D), y_new.reshape(n_s, 1, D), st_p[:, None], st_s[:, None],
            v_rows.reshape(n_s, 1, 1, GM_HALF))
```

```python
import functools
import math

import jax
import jax.numpy as jnp
from jax import lax
from jax.experimental import pallas as pl
from jax.experimental.pallas import tpu as pltpu
from jax.experimental.pallas import tpu_sc as plsc

F32 = jnp.float32
BF16 = jnp.bfloat16

D = 1024
DP = D // 2
GM_CHUNK = 128
GM_HALF = 2 * D
GM_GROUPS = 8
GM_GROUP_DIM = GM_HALF // GM_GROUPS
GLA_HEADS = 4
GLA_DK = 128
GLA_DV = 256
GLA_DK_TOT = GLA_HEADS * GLA_DK
GLA_DV_TOT = GLA_HEADS * GLA_DV
GLA_GATE_RANK = 16
GLA_GATE_NORMALIZER = 16.0
GLA_CHUNK = 64
N_EXPERTS = 64
TOP_K = 8
N_EXPERT_GROUPS = 8
GROUP_SIZE = N_EXPERTS // N_EXPERT_GROUPS
TOPK_GROUPS = 4
EXPERT_DIM = D // 4
ROUTED_SCALE = 2.5
NORM_EPS = 1e-6
LN_EPS = 1e-5

LANES = 128
VMEM_LIMIT = 56 * 1024 * 1024

MIX_TILE = 256
GM_COL_BLOCK = 512
ROUTER_TILE = 512
EXPERT_TILE = 272
EXPERT_X_SLOTS = 6
EXPERT_AHEAD = EXPERT_X_SLOTS - 2
EXPERT_Y_SLOTS = 4
SC_WORKERS = 32
DISPATCH_W = 32
GATHER_W = 64
TOKEN_PAD = SC_WORKERS * DISPATCH_W


def _cparams(*sem):
    return pltpu.CompilerParams(dimension_semantics=sem, vmem_limit_bytes=VMEM_LIMIT)


def _rms(x, g):
    return x * lax.rsqrt(jnp.mean(x * x, axis=-1, keepdims=True) + NORM_EPS) * g


def _silu(x):
    return x * (1.0 / (1.0 + jnp.exp(-x)))


def _gelu(x):
    return 0.5 * x * (1.0 + lax.erf(x * (1.0 / math.sqrt(2.0))))


def _bdot(a, b):
    return jnp.dot(a.astype(BF16), b.astype(BF16), preferred_element_type=F32)


def _dot_nt(a, b, precision=None):
    return lax.dot_general(a, b, (((1,), (1,)), ((), ())), preferred_element_type=F32,
                           precision=precision)


def _mod_slices(mod_ref):
    return [mod_ref[:, i * D:(i + 1) * D] for i in range(6)]


HI_HALF = -65536


def _pack_rows(x):
    lo = lax.bitcast_convert_type(x[:, :DP].astype(BF16).astype(F32), jnp.int32)
    hi = lax.bitcast_convert_type(x[:, DP:].astype(BF16).astype(F32), jnp.int32)
    return lax.shift_right_logical(lo, 16) | (hi & HI_HALF)


def _unpack_rows(p):
    lo = lax.bitcast_convert_type(lax.shift_left(p, 16), F32)
    hi = lax.bitcast_convert_type(p & HI_HALF, F32)
    return lo, hi


def _ffn_prep(x1, ng, sh2, sc2, rw_ref, h_ref, lg_ref):
    hffn = _rms(x1, ng[2:3]) * (1.0 + sc2) + sh2
    h_ref[...] = _pack_rows(hffn)
    lg3 = _dot_nt(rw_ref[...], hffn.astype(BF16))
    lg_ref[...] = lg3[:N_EXPERTS] + lg3[N_EXPERTS:2 * N_EXPERTS] + lg3[2 * N_EXPERTS:]


def _ada_body(c_ref, w_ref, b_ref, o_ref):
    c = c_ref[...]
    o_ref[...] = _bdot(_silu(c), w_ref[...]) + b_ref[...]


def _ada(c, ada_w, ada_b):
    n = c.shape[0]
    depth = ada_w.shape[0]
    tn = 1536
    return pl.pallas_call(
        _ada_body,
        grid=(depth, 6 * D // tn),
        in_specs=[pl.BlockSpec((n, D), lambda l, j: (0, 0)),
                  pl.BlockSpec((None, D, tn), lambda l, j: (l, 0, j)),
                  pl.BlockSpec((None, 1, tn), lambda l, j: (l, 0, j))],
        out_specs=pl.BlockSpec((None, n, tn), lambda l, j: (l, 0, j)),
        out_shape=jax.ShapeDtypeStruct((depth, n, 6 * D), F32),
        compiler_params=_cparams("parallel", "parallel"),
        name="ada_mod",
    )(c, ada_w, ada_b.reshape(depth, 1, 6 * D))


def _mod_spec(per_row, tt, tiles_per_batch):
    if per_row:
        return pl.BlockSpec((tt, 6 * D), lambda i: (i, 0))
    return pl.BlockSpec((None, 1, 6 * D), lambda i: (i // tiles_per_batch, 0, 0))


def _const_spec(shape):
    zeros = (0,) * len(shape)
    return pl.BlockSpec(shape, lambda *_: zeros)


def _gmlp_body(x_ref, mod_ref, ng_ref, win_ref, bin_ref, lng_ref, lnb_ref, ws_ref, bs_ref, wout_ref,
               rw_ref, x1_ref, h_ref, lg_ref, *rest, n_chunks, emit_v):
    if emit_v:
        v_ref, um_ref, z_ref = rest
    else:
        um_ref, z_ref = rest
    sh1, sc1, g1, sh2, sc2, _ = _mod_slices(mod_ref)
    ng = ng_ref[...]
    x = x_ref[...]
    hb = (_rms(x, ng[0:1]) * (1.0 + sc1) + sh1).astype(BF16)
    for cb in range(2 * GM_HALF // GM_COL_BLOCK):
        cols = slice(cb * GM_COL_BLOCK, (cb + 1) * GM_COL_BLOCK)
        z_ref[:, cols] = _gelu(jnp.dot(hb, win_ref[:, cols], preferred_element_type=F32) + bin_ref[:, cols])
    u = z_ref[:, :GM_HALF]
    v = z_ref[:, GM_HALF:]
    mu = jnp.mean(v, axis=-1, keepdims=True)
    vc = v - mu
    var = jnp.mean(vc * vc, axis=-1, keepdims=True)
    v = vc * lax.rsqrt(var + LN_EPS) * lng_ref[...] + lnb_ref[...]
    if emit_v:
        v_ref[...] = v
    vb = v.astype(BF16)
    for c in range(n_chunks):
        rows = slice(c * GM_CHUNK, (c + 1) * GM_CHUNK)
        for g in range(GM_GROUPS):
            cols = slice(g * GM_GROUP_DIM, (g + 1) * GM_GROUP_DIM)
            mixed = jnp.dot(ws_ref[g], vb[rows, cols], preferred_element_type=F32) + bs_ref[:, g:g + 1]
            um_ref[rows, cols] = (u[rows, cols] * mixed).astype(BF16)
    y = jnp.dot(um_ref[...], wout_ref[...], preferred_element_type=F32)
    x1 = x + g1 * _rms(y, ng[1:2])
    x1_ref[...] = x1
    _ffn_prep(x1, ng, sh2, sc2, rw_ref, h_ref, lg_ref)


def _gmlp_mixer(x2d, blk0, n, mod, per_row, tt, tiles_per_batch, ng, win, b_in, ln_g, ln_b, ws, bs, wout,
                rw_t, emit_v):
    out_shape = [jax.ShapeDtypeStruct((n, D), F32), jax.ShapeDtypeStruct((n, DP), jnp.int32),
                 jax.ShapeDtypeStruct((N_EXPERTS, n), F32)]
    out_specs = [pl.BlockSpec((tt, D), lambda i: (i, 0)), pl.BlockSpec((tt, DP), lambda i: (i, 0)),
                 pl.BlockSpec((N_EXPERTS, tt), lambda i: (0, i))]
    if emit_v:
        out_shape.append(jax.ShapeDtypeStruct((n, GM_HALF), F32))
        out_specs.append(pl.BlockSpec((tt, GM_HALF), lambda i: (i, 0)))
    return pl.pallas_call(
        functools.partial(_gmlp_body, n_chunks=tt // GM_CHUNK, emit_v=emit_v),
        grid=(n // tt,),
        in_specs=[pl.BlockSpec((tt, D), lambda i: (i + blk0, 0)),
                  _mod_spec(per_row, tt, tiles_per_batch),
                  _const_spec(ng.shape), _const_spec(win.shape), _const_spec(b_in.shape),
                  _const_spec(ln_g.shape), _const_spec(ln_b.shape), _const_spec(ws.shape),
                  _const_spec(bs.shape), _const_spec(wout.shape), _const_spec(rw_t.shape)],
        out_specs=out_specs,
        out_shape=out_shape,
        scratch_shapes=[pltpu.VMEM((tt, GM_HALF), BF16), pltpu.VMEM((tt, 2 * GM_HALF), F32)],
        compiler_params=_cparams("parallel"),
        name="gmlp_mixer_rows" if per_row else "gmlp_mixer",
    )(x2d, mod, ng, win, b_in, ln_g, ln_b, ws, bs, wout, rw_t)


def _combine_body(x_ref, *refs):
    y_refs = refs[:TOP_K]
    w_ref, h_ref, mod_ref, ng_ref, swg_ref, swu_ref, swd_ref = refs[TOP_K:TOP_K + 7]
    o_ref = refs[-1]
    h_lo, h_hi = _unpack_rows(h_ref[...])
    h_lo = h_lo.astype(BF16)
    h_hi = h_hi.astype(BF16)

    def hdot(w_ref_):
        return (jnp.dot(h_lo, w_ref_[:DP, :], preferred_element_type=F32) +
                jnp.dot(h_hi, w_ref_[DP:, :], preferred_element_type=F32))

    hs = (_silu(hdot(swg_ref)) * hdot(swu_ref)).astype(BF16)
    y = jnp.dot(hs, swd_ref[...], preferred_element_type=F32)
    w = w_ref[...]
    acc_lo = jnp.zeros((x_ref.shape[0], DP), F32)
    acc_hi = jnp.zeros((x_ref.shape[0], DP), F32)
    for k in range(TOP_K):
        lo, hi = _unpack_rows(y_refs[k][...])
        acc_lo += w[:, k:k + 1] * lo
        acc_hi += w[:, k:k + 1] * hi
    y = y + jnp.concatenate([acc_lo, acc_hi], axis=1)
    g2 = mod_ref[:, 5 * D:6 * D]
    o_ref[...] = x_ref[...] + g2 * _rms(y, ng_ref[3:4, :])


def _combine(x2d, y8, wts_t, hp, blk0, mod, per_row, tt, tiles_per_batch, ng, swg, swu, swd,
             out_rows=None, out_blk0=0, out_buf=None):
    n = x2d.shape[0]
    y_specs = [pl.BlockSpec((None, tt, DP), functools.partial(lambda i, k: (k, i + blk0, 0), k=k))
               for k in range(TOP_K)]
    in_specs = ([pl.BlockSpec((tt, D), lambda i: (i, 0))] + y_specs +
                [pl.BlockSpec((tt, TOP_K), lambda i: (i + blk0, 0)),
                 pl.BlockSpec((tt, DP), lambda i: (i + blk0, 0)),
                 _mod_spec(per_row, tt, tiles_per_batch),
                 _const_spec(ng.shape), _const_spec(swg.shape), _const_spec(swu.shape),
                 _const_spec(swd.shape)])
    args = [x2d] + [y8] * TOP_K + [wts_t, hp, mod, ng, swg, swu, swd]
    aliases = {}
    if out_buf is not None:
        in_specs.append(pl.BlockSpec(memory_space=pl.ANY))
        aliases = {len(args): 0}
        args.append(out_buf)
    return pl.pallas_call(
        _combine_body,
        grid=(n // tt,),
        in_specs=in_specs,
        out_specs=pl.BlockSpec((tt, D), lambda i: (i + out_blk0, 0)),
        out_shape=jax.ShapeDtypeStruct((out_rows or n, D), F32),
        input_output_aliases=aliases,
        compiler_params=_cparams("parallel"),
        name="combine_rows" if per_row else "combine",
    )(*args)


def _router_body(lg_ref, bias_ref, tri_ref, eid_ref, rank_ref, wts_ref, cnt_ref, carry_ref, *, n_real):
    step = pl.program_id(0)

    @pl.when(step == 0)
    def _():
        carry_ref[...] = jnp.zeros_like(carry_ref)

    lg = lg_ref[...]
    tn = lg.shape[1]
    real = (step * tn + lax.broadcasted_iota(jnp.int32, (1, tn), 1)) < n_real
    lg = jnp.where(real, lg, 0.0)
    scores = 1.0 / (1.0 + jnp.exp(-lg))
    sel = scores + bias_ref[...]
    neg = -jnp.inf
    sub8 = lax.broadcasted_iota(jnp.int32, (GROUP_SIZE, tn), 0)
    gsub = lax.broadcasted_iota(jnp.int32, (N_EXPERT_GROUPS, tn), 0)
    gs = jnp.zeros((N_EXPERT_GROUPS, tn), F32)
    for g in range(N_EXPERT_GROUPS):
        blk = sel[g * GROUP_SIZE:(g + 1) * GROUP_SIZE, :]
        m1 = jnp.max(blk, axis=0, keepdims=True)
        i1 = jnp.min(jnp.where(blk == m1, sub8, GROUP_SIZE), axis=0, keepdims=True)
        m2 = jnp.max(jnp.where(sub8 == i1, neg, blk), axis=0, keepdims=True)
        gs = jnp.where(gsub == g, m1 + m2, gs)
    gmask = jnp.zeros((N_EXPERT_GROUPS, tn), jnp.bool_)
    for _ in range(TOPK_GROUPS):
        m = jnp.max(gs, axis=0, keepdims=True)
        i = jnp.min(jnp.where(gs == m, gsub, N_EXPERT_GROUPS), axis=0, keepdims=True)
        hit = gsub == i
        gmask = jnp.logical_or(gmask, hit)
        gs = jnp.where(hit, neg, gs)
    gmaskf = gmask.astype(F32)
    blocks = []
    for g in range(N_EXPERT_GROUPS):
        keep = jnp.broadcast_to(gmaskf[g:g + 1, :], (GROUP_SIZE, tn)) > 0.5
        blocks.append(jnp.where(keep, sel[g * GROUP_SIZE:(g + 1) * GROUP_SIZE, :], neg))
    msel = jnp.concatenate(blocks, axis=0)
    esub = lax.broadcasted_iota(jnp.int32, (N_EXPERTS, tn), 0)
    chosen = jnp.zeros((N_EXPERTS, tn), jnp.bool_)
    picks = []
    for _ in range(TOP_K):
        m = jnp.max(msel, axis=0, keepdims=True)
        i = jnp.min(jnp.where(msel == m, esub, N_EXPERTS), axis=0, keepdims=True)
        hit = esub == i
        picks.append(i)
        chosen = jnp.logical_or(chosen, hit)
        msel = jnp.where(hit, neg, msel)
    w = jnp.where(chosen, scores, 0.0)
    w = w / jnp.sum(w, axis=0, keepdims=True) * ROUTED_SCALE
    counted = jnp.where(jnp.logical_and(chosen, real), 1.0, 0.0)
    incl = jnp.dot(counted.astype(BF16), tri_ref[...], preferred_element_type=F32)
    rank_full = carry_ref[:, 0:1] + incl - 1.0
    ksub = lax.broadcasted_iota(jnp.int32, (TOP_K, tn), 0)
    eid = jnp.zeros((TOP_K, tn), jnp.int32)
    rank = jnp.zeros((TOP_K, tn), F32)
    wts = jnp.zeros((TOP_K, tn), F32)
    for k in range(TOP_K):
        hit = esub == picks[k]
        eid = jnp.where(ksub == k, picks[k], eid)
        rank = jnp.where(ksub == k, jnp.sum(jnp.where(hit, rank_full, 0.0), axis=0, keepdims=True), rank)
        wts = jnp.where(ksub == k, jnp.sum(jnp.where(hit, w, 0.0), axis=0, keepdims=True), wts)
    eid_ref[...] = eid
    rank_ref[...] = rank.astype(jnp.int32)
    wts_ref[...] = wts
    carry = carry_ref[...] + incl[:, tn - 1:tn]
    carry_ref[...] = carry
    cnt_ref[...] = carry.astype(jnp.int32)


def _router(lg_t, bias, n_real):
    n = lg_t.shape[1]
    tn = ROUTER_TILE
    idx = jnp.arange(tn)
    tri = (idx[:, None] <= idx[None, :]).astype(BF16)
    kspec = pl.BlockSpec((TOP_K, tn), lambda i: (0, i))
    return pl.pallas_call(
        functools.partial(_router_body, n_real=n_real),
        grid=(n // tn,),
        in_specs=[pl.BlockSpec((N_EXPERTS, tn), lambda i: (0, i)), _const_spec((N_EXPERTS, 1)),
                  _const_spec((tn, tn))],
        out_specs=[kspec, kspec, kspec, _const_spec((N_EXPERTS, LANES))],
        out_shape=[jax.ShapeDtypeStruct((TOP_K, n), jnp.int32), jax.ShapeDtypeStruct((TOP_K, n), jnp.int32),
                   jax.ShapeDtypeStruct((TOP_K, n), F32), jax.ShapeDtypeStruct((N_EXPERTS, LANES), jnp.int32)],
        scratch_shapes=[pltpu.VMEM((N_EXPERTS, LANES), F32)],
        compiler_params=_cparams("arbitrary"),
        name="router",
    )(lg_t, bias.reshape(N_EXPERTS, 1), tri)


def _dest_body(off_ref, eid_ref, rank_ref, dest_ref, *, n_real, last_row):
    eid = eid_ref[...]
    base = jnp.zeros(eid.shape, jnp.int32)
    for e in range(N_EXPERTS):
        base = jnp.where(eid == e, off_ref[e], base)
    tok = lax.broadcasted_iota(jnp.int32, eid.shape, 1)
    slot = lax.broadcasted_iota(jnp.int32, eid.shape, 0)
    unused = last_row - ((tok - n_real) * TOP_K + slot)
    dest_ref[...] = jnp.where(tok < n_real, base + rank_ref[...], unused)


def _dest(off, eid, rank, n_real, last_row):
    spec = pl.BlockSpec(eid.shape, lambda i, off_ref: (0, 0))
    return pl.pallas_call(
        functools.partial(_dest_body, n_real=n_real, last_row=last_row),
        grid_spec=pltpu.PrefetchScalarGridSpec(num_scalar_prefetch=1, grid=(1,), in_specs=[spec, spec],
                                               out_specs=spec),
        out_shape=jax.ShapeDtypeStruct(eid.shape, jnp.int32),
        compiler_params=_cparams("arbitrary"),
        name="dest_rows",
    )(off, eid, rank)


def _sc_mesh():
    return plsc.VectorSubcoreMesh(core_axis_name="core", subcore_axis_name="subcore")


def _sc_dispatch(hp, dest_w, p_alloc):
    n = hp.shape[0]
    w = dest_w.shape[2]

    @functools.partial(pl.kernel, out_type=jax.ShapeDtypeStruct((p_alloc, DP), jnp.int32), mesh=_sc_mesh(),
                       name="sc_dispatch")
    def run(hp_hbm, dest_hbm, xs_hbm):
        def body(x_vmem, i_vmem):
            for k in range(TOP_K):
                pltpu.sync_copy(x_vmem, xs_hbm.at[i_vmem.at[k]])

        pltpu.emit_pipeline(
            body,
            grid=(n // w,),
            in_specs=[pl.BlockSpec((w, DP), lambda i: (i, 0)),
                      pl.BlockSpec((None, TOP_K, w), lambda i: (i, 0, 0))],
            out_specs=[],
            core_axis_name=("core", "subcore"),
            dimension_semantics=(pltpu.PARALLEL,),
        )(hp_hbm, dest_hbm)

    return run(hp, dest_w)


def _sc_gather(ys, dest_g):
    g, _, w = dest_g.shape

    @functools.partial(pl.kernel, out_type=jax.ShapeDtypeStruct((g * w, DP), jnp.int32), mesh=_sc_mesh(),
                       name="sc_gather")
    def run(ys_hbm, dest_hbm, o_hbm):
        def body(i_vmem, o_vmem):
            pltpu.sync_copy(ys_hbm.at[i_vmem.at[0]], o_vmem)

        pltpu.emit_pipeline(
            body,
            grid=(g,),
            in_specs=[pl.BlockSpec((None, 1, w), lambda i: (i, 0, 0))],
            out_specs=[pl.BlockSpec((w, DP), lambda i: (i, 0))],
            core_axis_name=("core", "subcore"),
            dimension_semantics=(pltpu.PARALLEL,),
        )(dest_hbm, o_hbm)

    return run(ys, dest_g)


def _expert_body(first_ref, cnt_ref, nused_ref, xs_hbm, wg_ref, wu_ref, wd_ref, ys_hbm,
                 wg_s, wu_s, wd_s, xbuf, ybuf, xsem, ysem):
    e = pl.program_id(0)
    n_used = nused_ref[0]

    def load(g):
        rows = pl.ds(pl.multiple_of(g * EXPERT_TILE, EXPERT_TILE), EXPERT_TILE)
        slot = g % EXPERT_X_SLOTS
        return pltpu.make_async_copy(xs_hbm.at[rows], xbuf.at[slot], xsem.at[slot])

    def store(g):
        rows = pl.ds(pl.multiple_of(g * EXPERT_TILE, EXPERT_TILE), EXPERT_TILE)
        slot = g % EXPERT_Y_SLOTS
        return pltpu.make_async_copy(ybuf.at[slot], ys_hbm.at[rows], ysem.at[slot])

    @pl.when(e == 0)
    def _():
        for g in range(EXPERT_AHEAD):
            @pl.when(g < n_used)
            def _():
                load(g).start()

    wg_s[...] = wg_ref[...].astype(BF16)
    wu_s[...] = wu_ref[...].astype(BF16)
    wd_s[...] = wd_ref[...].astype(BF16)
    first = first_ref[e]
    cnt = cnt_ref[e]

    def acquire(g):
        ahead = g + EXPERT_AHEAD

        @pl.when(ahead < n_used)
        def _():
            load(ahead).start()

        load(g).wait()

        @pl.when(g >= EXPERT_Y_SLOTS)
        def _():
            store(g - EXPERT_Y_SLOTS).wait()

    def compute(g):
        lo, hi = _unpack_rows(xbuf[g % EXPERT_X_SLOTS])
        lo = lo.astype(BF16)
        hi = hi.astype(BF16)

        def xdot(w_s):
            return (jnp.dot(lo, w_s[:DP, :], preferred_element_type=F32) +
                    jnp.dot(hi, w_s[DP:, :], preferred_element_type=F32))

        a = (_silu(xdot(wg_s)) * xdot(wu_s)).astype(BF16)
        ybuf[g % EXPERT_Y_SLOTS] = _pack_rows(jnp.dot(a, wd_s[...], preferred_element_type=F32))

    def pair(j, carry):
        g = first + 2 * j
        acquire(g)
        acquire(g + 1)
        compute(g)
        compute(g + 1)
        store(g).start()
        store(g + 1).start()
        return carry

    lax.fori_loop(0, cnt // 2, pair, 0)

    @pl.when(cnt % 2 == 1)
    def _():
        g = first + cnt - 1
        acquire(g)
        compute(g)
        store(g).start()

    @pl.when(e == N_EXPERTS - 1)
    def _():
        for k in range(EXPERT_Y_SLOTS):
            g = n_used - 1 - k

            @pl.when(g >= 0)
            def _():
                store(g).wait()


def _experts(xs, tile_first, tile_count, n_used, layer, wg, wu, wd):
    def w_map(e, first, cnt, nu):
        return (layer, e, 0, 0)

    return pl.pallas_call(
        _expert_body,
        grid_spec=pltpu.PrefetchScalarGridSpec(
            num_scalar_prefetch=3, grid=(N_EXPERTS,),
            in_specs=[pl.BlockSpec(memory_space=pl.ANY),
                      pl.BlockSpec((None, None, D, EXPERT_DIM), w_map),
                      pl.BlockSpec((None, None, D, EXPERT_DIM), w_map),
                      pl.BlockSpec((None, None, EXPERT_DIM, D), w_map)],
            out_specs=pl.BlockSpec(memory_space=pl.ANY),
            scratch_shapes=[pltpu.VMEM((D, EXPERT_DIM), BF16), pltpu.VMEM((D, EXPERT_DIM), BF16),
                            pltpu.VMEM((EXPERT_DIM, D), BF16),
                            pltpu.VMEM((EXPERT_X_SLOTS, EXPERT_TILE, DP), jnp.int32),
                            pltpu.VMEM((EXPERT_Y_SLOTS, EXPERT_TILE, DP), jnp.int32),
                            pltpu.SemaphoreType.DMA((EXPERT_X_SLOTS,)),
                            pltpu.SemaphoreType.DMA((EXPERT_Y_SLOTS,))]),
        out_shape=jax.ShapeDtypeStruct(xs.shape, jnp.int32),
        compiler_params=_cparams("arbitrary"),
        name="experts",
    )(tile_first, tile_count, n_used, xs, wg, wu, wd)


def _log_sigmoid(z):
    return jnp.minimum(z, 0.0) - jnp.log(1.0 + jnp.exp(-jnp.abs(z)))


def _gla_gate(hb, wlr_ref, wgk_ref, bgk_ref):
    lr = jnp.dot(hb, wlr_ref[...], preferred_element_type=F32)
    z = _bdot(lr, wgk_ref[...]) + bgk_ref[...]
    return _log_sigmoid(z) * (1.0 / GLA_GATE_NORMALIZER)


def _split3(a):
    hi = a.astype(BF16)
    r1 = a - hi.astype(F32)
    mid = r1.astype(BF16)
    lo = (r1 - mid.astype(F32)).astype(BF16)
    return hi, mid, lo


def _gla_out(o_ref_val, go, gng):
    parts = []
    for hd in range(GLA_HEADS):
        cols = slice(hd * GLA_DV, (hd + 1) * GLA_DV)
        parts.append((_rms(o_ref_val[:, cols], gng) * _silu(go[:, cols])).astype(BF16))
    return jnp.concatenate(parts, axis=1)


def _gla_body(x_ref, mod_ref, ng_ref, wqkvg_ref, wlr_ref, wgk_ref, bgk_ref, tril_ref, gng_ref, wout_ref,
              rw_ref, x1_ref, h_ref, lg_ref, st_ref, st_scr, o_scr, qd_scr, dst_scr, *, tt):
    j = pl.program_id(1)

    @pl.when(j == 0)
    def _():
        st_scr[...] = jnp.zeros_like(st_scr)

    sh1, sc1, g1, sh2, sc2, _ = _mod_slices(mod_ref)
    ng = ng_ref[...]
    x = x_ref[...]
    hb = (_rms(x, ng[0:1]) * (1.0 + sc1) + sh1).astype(BF16)
    proj = jnp.dot(hb, wqkvg_ref[...], preferred_element_type=F32)
    q = proj[:, :GLA_DK_TOT] * (GLA_DK ** -0.5)
    k = proj[:, GLA_DK_TOT:2 * GLA_DK_TOT]
    v = proj[:, 2 * GLA_DK_TOT:2 * GLA_DK_TOT + GLA_DV_TOT].astype(BF16)
    go = proj[:, 2 * GLA_DK_TOT + GLA_DV_TOT:]
    log_a = _gla_gate(hb, wlr_ref, wgk_ref, bgk_ref)
    tril = tril_ref[...]
    b = sum(jnp.dot(tril, part, preferred_element_type=F32) for part in _split3(log_a))
    row = lax.broadcasted_iota(jnp.int32, (GLA_CHUNK, GLA_CHUNK), 0)
    col = lax.broadcasted_iota(jnp.int32, (GLA_CHUNK, GLA_CHUNK), 1)
    causal = row >= col
    n_chunks = tt // GLA_CHUNK
    for c in range(n_chunks):
        rows = slice(c * GLA_CHUNK, (c + 1) * GLA_CHUNK)
        last = (c + 1) * GLA_CHUNK - 1
        for hd in range(GLA_HEADS):
            kc = slice(hd * GLA_DK, (hd + 1) * GLA_DK)
            vc = slice(hd * GLA_DV, (hd + 1) * GLA_DV)
            bb = b[rows, kc]
            b_last = b[last:last + 1, kc]
            q_dec = (q[rows, kc] * jnp.exp(bb)).astype(BF16)
            k_inv = (k[rows, kc] * jnp.exp(-bb)).astype(BF16)
            k_end = (k[rows, kc] * jnp.exp(b_last - bb)).astype(BF16)
            att = jnp.where(causal, _dot_nt(q_dec, k_inv), 0.0).astype(BF16)
            qd_scr[rows, kc] = q_dec
            o_scr[rows, vc] = jnp.dot(att, v[rows, vc], preferred_element_type=F32)
            dst_scr[c * GLA_HEADS + hd] = lax.dot_general(
                v[rows, vc], k_end, (((0,), (0,)), ((), ())), preferred_element_type=F32)
    states = [st_scr[hd] for hd in range(GLA_HEADS)]
    for c in range(n_chunks):
        rows = slice(c * GLA_CHUNK, (c + 1) * GLA_CHUNK)
        last = (c + 1) * GLA_CHUNK - 1
        for hd in range(GLA_HEADS):
            kc = slice(hd * GLA_DK, (hd + 1) * GLA_DK)
            vc = slice(hd * GLA_DV, (hd + 1) * GLA_DV)
            o_scr[rows, vc] += _dot_nt(qd_scr[rows, kc], states[hd].astype(BF16))
            states[hd] = states[hd] * jnp.exp(b[last:last + 1, kc]) + dst_scr[c * GLA_HEADS + hd]
    for hd in range(GLA_HEADS):
        st_scr[hd] = states[hd]

    @pl.when(j == pl.num_programs(1) - 1)
    def _():
        for hd in range(GLA_HEADS):
            st_ref[hd] = st_scr[hd].T

    y = jnp.dot(_gla_out(o_scr[...], go, gng_ref[...]), wout_ref[...], preferred_element_type=F32)
    x1 = x + g1 * _rms(y, ng[1:2])
    x1_ref[...] = x1
    _ffn_prep(x1, ng, sh2, sc2, rw_ref, h_ref, lg_ref)


def _gla_mixer(x2d, mod3, batch, seq, ng, wqkvg, wlr, wgk, bgk, gng, wout, rw_t):
    tt = MIX_TILE
    tpb = seq // tt
    n = x2d.shape[0]
    idx = jnp.arange(tt)
    tril = ((idx[:, None] >= idx[None, :]) &
            (idx[:, None] // GLA_CHUNK == idx[None, :] // GLA_CHUNK)).astype(BF16)
    row_map = lambda b, j: (b * tpb + j, 0)
    consts = (ng, wqkvg, wlr, wgk, bgk, tril, gng, wout, rw_t)
    return pl.pallas_call(
        functools.partial(_gla_body, tt=tt),
        grid=(batch, tpb),
        in_specs=[pl.BlockSpec((tt, D), row_map),
                  pl.BlockSpec((None, 1, 6 * D), lambda b, j: (b, 0, 0))] +
                 [_const_spec(a.shape) for a in consts],
        out_specs=[pl.BlockSpec((tt, D), row_map), pl.BlockSpec((tt, DP), row_map),
                   pl.BlockSpec((N_EXPERTS, tt), lambda b, j: (0, b * tpb + j)),
                   pl.BlockSpec((None, GLA_HEADS, GLA_DK, GLA_DV), lambda b, j: (b, 0, 0, 0))],
        out_shape=[jax.ShapeDtypeStruct((n, D), F32), jax.ShapeDtypeStruct((n, DP), jnp.int32),
                   jax.ShapeDtypeStruct((N_EXPERTS, n), F32),
                   jax.ShapeDtypeStruct((batch, GLA_HEADS, GLA_DK, GLA_DV), F32)],
        scratch_shapes=[pltpu.VMEM((GLA_HEADS, GLA_DV, GLA_DK), F32),
                        pltpu.VMEM((tt, GLA_DV_TOT), F32),
                        pltpu.VMEM((tt, GLA_DK_TOT), BF16),
                        pltpu.VMEM((tt // GLA_CHUNK * GLA_HEADS, GLA_DV, GLA_DK), F32)],
        compiler_params=_cparams("parallel", "arbitrary"),
        name="gla_mixer",
    )(x2d, mod3, *consts)


def _gla1_proj_body(x_ref, mod_ref, ng_ref, wqkvg_ref, wlr_ref, wgk_ref, bgk_ref,
                    q_ref, k_ref, v_ref, go_ref, dec_ref):
    sh1, sc1, _, _, _, _ = _mod_slices(mod_ref)
    ng = ng_ref[...]
    hb = (_rms(x_ref[...], ng[0:1]) * (1.0 + sc1) + sh1).astype(BF16)
    proj = jnp.dot(hb, wqkvg_ref[...], preferred_element_type=F32)
    q_ref[...] = proj[:, :GLA_DK_TOT] * (GLA_DK ** -0.5)
    k_ref[...] = proj[:, GLA_DK_TOT:2 * GLA_DK_TOT]
    v_ref[...] = proj[:, 2 * GLA_DK_TOT:2 * GLA_DK_TOT + GLA_DV_TOT]
    go_ref[...] = proj[:, 2 * GLA_DK_TOT + GLA_DV_TOT:]
    dec_ref[...] = jnp.exp(_gla_gate(hb, wlr_ref, wgk_ref, bgk_ref))


GLA1_TOK = 8


def _gla1_state_body(st_ref, qc_ref, kc_ref, dc_ref, v_ref, nst_ref, o_ref):
    v = v_ref[...]
    for i in range(GLA1_TOK):
        for hd in range(GLA_HEADS):
            vrow = v[i:i + 1, hd * GLA_DV:(hd + 1) * GLA_DV]
            s_new = dc_ref[hd][:, i:i + 1] * st_ref[i, hd] + kc_ref[hd][:, i:i + 1] * vrow
            nst_ref[i, hd] = s_new
            o_ref[i:i + 1, hd * GLA_DV:(hd + 1) * GLA_DV] = jnp.sum(
                qc_ref[hd][:, i:i + 1] * s_new, axis=0, keepdims=True)


def _gla1_out_body(x_ref, o_ref, go_ref, mod_ref, ng_ref, gng_ref, wout_ref, rw_ref, x1_ref, h_ref, lg_ref):
    _, _, g1, sh2, sc2, _ = _mod_slices(mod_ref)
    ng = ng_ref[...]
    y = jnp.dot(_gla_out(o_ref[...], go_ref[...], gng_ref[...]), wout_ref[...], preferred_element_type=F32)
    x1 = x_ref[...] + g1 * _rms(y, ng[1:2])
    x1_ref[...] = x1
    _ffn_prep(x1, ng, sh2, sc2, rw_ref, h_ref, lg_ref)


def _gla_mixer_one(x2d, mod2, state, ng, wqkvg, wlr, wgk, bgk, gng, wout, rw_t):
    n = x2d.shape[0]
    consts = (ng, wqkvg, wlr, wgk, bgk)
    q, k, v, go, dec = pl.pallas_call(
        _gla1_proj_body,
        in_specs=[_const_spec(a.shape) for a in (x2d, mod2) + consts],
        out_specs=[_const_spec((n, GLA_DK_TOT)), _const_spec((n, GLA_DK_TOT)), _const_spec((n, GLA_DV_TOT)),
                   _const_spec((n, GLA_DV_TOT)), _const_spec((n, GLA_DK_TOT))],
        out_shape=[jax.ShapeDtypeStruct((n, GLA_DK_TOT), F32), jax.ShapeDtypeStruct((n, GLA_DK_TOT), F32),
                   jax.ShapeDtypeStruct((n, GLA_DV_TOT), F32), jax.ShapeDtypeStruct((n, GLA_DV_TOT), F32),
                   jax.ShapeDtypeStruct((n, GLA_DK_TOT), F32)],
        grid=(1,),
        compiler_params=_cparams("arbitrary"),
        name="gla1_proj",
    )(x2d, mod2, *consts)

    def cols(a):
        return a.reshape(n // GLA1_TOK, GLA1_TOK, GLA_HEADS, GLA_DK).transpose(0, 2, 3, 1)

    col_spec = pl.BlockSpec((None, GLA_HEADS, GLA_DK, GLA1_TOK), lambda i: (i, 0, 0, 0))
    st_spec = pl.BlockSpec((GLA1_TOK, GLA_HEADS, GLA_DK, GLA_DV), lambda i: (i, 0, 0, 0))
    new_state, o = pl.pallas_call(
        _gla1_state_body,
        grid=(n // GLA1_TOK,),
        in_specs=[st_spec, col_spec, col_spec, col_spec, pl.BlockSpec((GLA1_TOK, GLA_DV_TOT), lambda i: (i, 0))],
        out_specs=[st_spec, pl.BlockSpec((GLA1_TOK, GLA_DV_TOT), lambda i: (i, 0))],
        out_shape=[jax.ShapeDtypeStruct(state.shape, F32), jax.ShapeDtypeStruct((n, GLA_DV_TOT), F32)],
        compiler_params=_cparams("parallel"),
        name="gla1_state",
    )(state, cols(q), cols(k), cols(dec), v)

    consts = (mod2, ng, gng, wout, rw_t)
    x1, h, lg = pl.pallas_call(
        _gla1_out_body,
        grid=(1,),
        in_specs=[_const_spec(a.shape) for a in (x2d, o, go) + consts],
        out_specs=[_const_spec((n, D)), _const_spec((n, DP)), _const_spec((N_EXPERTS, n))],
        out_shape=[jax.ShapeDtypeStruct((n, D), F32), jax.ShapeDtypeStruct((n, DP), jnp.int32),
                   jax.ShapeDtypeStruct((N_EXPERTS, n), F32)],
        compiler_params=_cparams("arbitrary"),
        name="gla1_out",
    )(x2d, o, go, *consts)
    return x1, h, lg, new_state


def _moe_routed(h_p, h_s, lg_p, lg_s, router_bias, layer, wg, wu, wd):
    h, lg = h_p, lg_p
    if h_s is not None:
        h = jnp.concatenate([h_p, h_s], axis=0)
        lg = jnp.concatenate([lg_p, lg_s], axis=1)
    n = h.shape[0]
    n_pad = -(-n // TOKEN_PAD) * TOKEN_PAD
    if n_pad != n:
        h = jnp.pad(h, ((0, n_pad - n), (0, 0)))
        lg = jnp.pad(lg, ((0, 0), (0, n_pad - n)))
    eid, rank, wts, counts = _router(lg, router_bias, n)
    tile_count = ((counts[:, 0] + EXPERT_TILE - 1) // EXPERT_TILE).astype(jnp.int32)
    tile_end = jnp.cumsum(tile_count).astype(jnp.int32)
    tile_first = tile_end - tile_count
    off = tile_first * EXPERT_TILE
    p_alloc = TOP_K * n_pad + N_EXPERTS * EXPERT_TILE
    dest = _dest(off, eid, rank, n, p_alloc - 1)
    dest_w = dest.reshape(TOP_K, n_pad // DISPATCH_W, DISPATCH_W).transpose(1, 0, 2)
    xs = _sc_dispatch(h, dest_w, p_alloc)
    ys = _experts(xs, tile_first, tile_count, tile_end[-1:], layer, wg, wu, wd)
    y8 = _sc_gather(ys, dest.reshape(TOP_K * n_pad // GATHER_W, 1, GATHER_W))
    return y8.reshape(TOP_K, n_pad, DP), wts.T, h


def kernel(x_prompt, x_sample, state_gla, c_prompt, c_sample, norm_g, ada_w, ada_b, gm_w_in, gm_b_in,
           gm_ln_g, gm_ln_b, gm_w_s, gm_b_s, gm_w_out, gla_w_in, gla_w_gk, gla_b_gk, gla_norm_g,
           gla_w_out, router_w, router_bias, exp_w_gate, exp_w_up, exp_w_down, sh_w_gate, sh_w_up,
           sh_w_down):
    batch, seq, _ = x_prompt.shape
    n_s = x_sample.shape[0]
    n_p = batch * seq
    tpb = seq // MIX_TILE
    xp = x_prompt.reshape(n_p, D)
    xs = x_sample.reshape(n_s, D)

    mod = _ada(jnp.concatenate([c_prompt, c_sample], axis=0), ada_w, ada_b)
    mod_p = [mod[i, :batch].reshape(batch, 1, 6 * D) for i in range(2)]
    mod_s = [mod[i, batch:] for i in range(2)]
    rw_t = [jnp.concatenate(_split3(router_w[i].T), axis=0) for i in range(2)]

    ws_causal = jnp.tril(gm_w_s[0]).astype(BF16)
    bs_cols = gm_b_s[0].T
    eye = jnp.eye(GM_CHUNK, dtype=F32)
    ws_first = (gm_w_s[0][:, 0, 0][:, None, None] * eye).astype(BF16)
    bs_first = jnp.broadcast_to(gm_b_s[0][:, 0][None, :], (GM_CHUNK, GM_GROUPS))
    gm_args = (norm_g[0], gm_w_in[0].astype(BF16), gm_b_in[0].reshape(1, -1), gm_ln_g[0].reshape(1, -1),
               gm_ln_b[0].reshape(1, -1))
    wout0 = gm_w_out[0].astype(BF16)
    shared = [(sh_w_gate[i].astype(BF16), sh_w_up[i].astype(BF16), sh_w_down[i].astype(BF16))
              for i in range(2)]
    n_qkvg = 2 * GLA_DK_TOT + 2 * GLA_DV_TOT
    wqkvg = gla_w_in[0][:, :n_qkvg].astype(BF16)
    wlr = jnp.pad(gla_w_in[0][:, n_qkvg:], ((0, 0), (0, LANES - GLA_GATE_RANK))).astype(BF16)
    wgk = jnp.pad(gla_w_gk[0], ((0, LANES - GLA_GATE_RANK), (0, 0))).astype(BF16)
    gla_args = (norm_g[1], wqkvg, wlr, wgk, gla_b_gk[0].reshape(1, -1), gla_norm_g[0].reshape(1, -1),
                gla_w_out[0].astype(BF16), rw_t[1])
    experts = (exp_w_gate, exp_w_up, exp_w_down)

    half = batch // 2
    streams = [(0, half, False), (half, batch - half, True)]
    st = [dict() for _ in streams]

    for s, (b0, nb, with_new) in zip(st, streams):
        s["mod_p"] = [mod_p[i][b0:b0 + nb] for i in range(2)]
        s["n"] = nb * seq
        s["x1p"], s["hp"], s["lgp"] = _gmlp_mixer(xp, b0 * tpb, s["n"], s["mod_p"][0], False, MIX_TILE, tpb,
                                                  *gm_args, ws_causal, bs_cols, wout0, rw_t[0], emit_v=False)
        s["hs"] = s["lgs"] = None
        if with_new:
            s["x1s"], s["hs"], s["lgs"], v_rows = _gmlp_mixer(xs, 0, n_s, mod_s[0], True, n_s, 1, *gm_args,
                                                              ws_first, bs_first, wout0, rw_t[0], emit_v=True)
    for s, (b0, nb, with_new) in zip(st, streams):
        moe = _moe_routed(s["hp"], s["hs"], s["lgp"], s["lgs"], router_bias[0], 0, *experts)
        s["x2p"] = _combine(s["x1p"], *moe, 0, s["mod_p"][0], False, MIX_TILE, tpb, norm_g[0], *shared[0])
        if with_new:
            s["x2s"] = _combine(s["x1s"], *moe, s["n"] // n_s, mod_s[0], True, n_s, 1, norm_g[0], *shared[0])
    for s, (b0, nb, with_new) in zip(st, streams):
        s["x3p"], s["hp"], s["lgp"], s["st_p"] = _gla_mixer(s["x2p"], s["mod_p"][1], nb, seq, *gla_args)
        if with_new:
            s["x3s"], s["hs"], s["lgs"], st_s = _gla_mixer_one(s["x2s"], mod_s[1], state_gla[:, 0], *gla_args)
    y_prompt = None
    for s, (b0, nb, with_new) in zip(st, streams):
        moe = _moe_routed(s["hp"], s["hs"], s["lgp"], s["lgs"], router_bias[1], 1, *experts)
        y_prompt = _combine(s["x3p"], *moe, 0, s["mod_p"][1], False, MIX_TILE, tpb, norm_g[1], *shared[1],
                            out_rows=n_p, out_blk0=b0 * tpb, out_buf=y_prompt)
        if with_new:
            y_new = _combine(s["x3s"], *moe, s["n"] // n_s, mod_s[1], True, n_s, 1, norm_g[1], *shared[1])
    st_p = jnp.concatenate([s["st_p"] for s in st], axis=0)

    return (y_prompt.reshape(batch, seq, D), y_new.reshape(n_s, 1, D), st_p[:, None], st_s[:, None],
            v_rows.reshape(n_s, 1, 1, GM_HALF))
```

```python
import functools
import math

import jax
import jax.numpy as jnp
from jax import lax
from jax.experimental import pallas as pl
from jax.experimental.pallas import tpu as pltpu
from jax.experimental.pallas import tpu_sc as plsc

F32 = jnp.float32
BF16 = jnp.bfloat16

D = 1024
DP = D // 2
GM_CHUNK = 128
GM_HALF = 2 * D
GM_GROUPS = 8
GM_GROUP_DIM = GM_HALF // GM_GROUPS
GLA_HEADS = 4
GLA_DK = 128
GLA_DV = 256
GLA_DK_TOT = GLA_HEADS * GLA_DK
GLA_DV_TOT = GLA_HEADS * GLA_DV
GLA_GATE_RANK = 16
GLA_GATE_NORMALIZER = 16.0
GLA_CHUNK = 64
N_EXPERTS = 64
TOP_K = 8
N_EXPERT_GROUPS = 8
GROUP_SIZE = N_EXPERTS // N_EXPERT_GROUPS
TOPK_GROUPS = 4
EXPERT_DIM = D // 4
ROUTED_SCALE = 2.5
NORM_EPS = 1e-6
LN_EPS = 1e-5

LANES = 128
VMEM_LIMIT = 56 * 1024 * 1024

MIX_TILE = 256
GM_COL_BLOCK = 512
ROUTER_TILE = 512
EXPERT_TILE = 272
EXPERT_X_SLOTS = 6
EXPERT_AHEAD = EXPERT_X_SLOTS - 2
EXPERT_Y_SLOTS = 4
EXPERT_W_PARTS = 4
SC_WORKERS = 32
DISPATCH_W = 32
GATHER_W = 64
TOKEN_PAD = SC_WORKERS * DISPATCH_W


def _cparams(*sem):
    return pltpu.CompilerParams(dimension_semantics=sem, vmem_limit_bytes=VMEM_LIMIT)


def _rms(x, g):
    return x * lax.rsqrt(jnp.mean(x * x, axis=-1, keepdims=True) + NORM_EPS) * g


def _silu(x):
    return x * (1.0 / (1.0 + jnp.exp(-x)))


def _gelu(x):
    return 0.5 * x * (1.0 + lax.erf(x * (1.0 / math.sqrt(2.0))))


def _bdot(a, b):
    return jnp.dot(a.astype(BF16), b.astype(BF16), preferred_element_type=F32)


def _dot_nt(a, b, precision=None):
    return lax.dot_general(a, b, (((1,), (1,)), ((), ())), preferred_element_type=F32,
                           precision=precision)


def _mod_slices(mod_ref):
    return [mod_ref[:, i * D:(i + 1) * D] for i in range(6)]


HI_HALF = -65536


def _pack_rows(x):
    lo = lax.bitcast_convert_type(x[:, :DP].astype(BF16).astype(F32), jnp.int32)
    hi = lax.bitcast_convert_type(x[:, DP:].astype(BF16).astype(F32), jnp.int32)
    return lax.shift_right_logical(lo, 16) | (hi & HI_HALF)


def _unpack_rows(p):
    lo = lax.bitcast_convert_type(lax.shift_left(p, 16), F32)
    hi = lax.bitcast_convert_type(p & HI_HALF, F32)
    return lo, hi


def _ffn_prep(x1, ng, sh2, sc2, rw_ref, h_ref, lg_ref):
    hffn = _rms(x1, ng[2:3]) * (1.0 + sc2) + sh2
    h_ref[...] = _pack_rows(hffn)
    lg3 = _dot_nt(rw_ref[...], hffn.astype(BF16))
    lg_ref[...] = lg3[:N_EXPERTS] + lg3[N_EXPERTS:2 * N_EXPERTS] + lg3[2 * N_EXPERTS:]


def _ada_body(c_ref, w_ref, b_ref, o_ref):
    c = c_ref[...]
    o_ref[...] = _bdot(_silu(c), w_ref[...]) + b_ref[...]


def _ada(c, ada_w, ada_b):
    n = c.shape[0]
    depth = ada_w.shape[0]
    tn = 1536
    return pl.pallas_call(
        _ada_body,
        grid=(depth, 6 * D // tn),
        in_specs=[pl.BlockSpec((n, D), lambda l, j: (0, 0)),
                  pl.BlockSpec((None, D, tn), lambda l, j: (l, 0, j)),
                  pl.BlockSpec((None, 1, tn), lambda l, j: (l, 0, j))],
        out_specs=pl.BlockSpec((None, n, tn), lambda l, j: (l, 0, j)),
        out_shape=jax.ShapeDtypeStruct((depth, n, 6 * D), F32),
        compiler_params=_cparams("parallel", "parallel"),
        name="ada_mod",
    )(c, ada_w, ada_b.reshape(depth, 1, 6 * D))


def _mod_spec(per_row, tt, tiles_per_batch):
    if per_row:
        return pl.BlockSpec((tt, 6 * D), lambda i: (i, 0))
    return pl.BlockSpec((None, 1, 6 * D), lambda i: (i // tiles_per_batch, 0, 0))


def _const_spec(shape):
    zeros = (0,) * len(shape)
    return pl.BlockSpec(shape, lambda *_: zeros)


def _gmlp_body(x_ref, mod_ref, ng_ref, win_ref, bin_ref, lng_ref, lnb_ref, ws_ref, bs_ref, wout_ref,
               rw_ref, x1_ref, h_ref, lg_ref, *rest, n_chunks, emit_v):
    if emit_v:
        v_ref, um_ref, z_ref = rest
    else:
        um_ref, z_ref = rest
    sh1, sc1, g1, sh2, sc2, _ = _mod_slices(mod_ref)
    ng = ng_ref[...]
    x = x_ref[...]
    hb = (_rms(x, ng[0:1]) * (1.0 + sc1) + sh1).astype(BF16)
    for cb in range(2 * GM_HALF // GM_COL_BLOCK):
        cols = slice(cb * GM_COL_BLOCK, (cb + 1) * GM_COL_BLOCK)
        z_ref[:, cols] = _gelu(jnp.dot(hb, win_ref[:, cols], preferred_element_type=F32) + bin_ref[:, cols])
    u = z_ref[:, :GM_HALF]
    v = z_ref[:, GM_HALF:]
    mu = jnp.mean(v, axis=-1, keepdims=True)
    vc = v - mu
    var = jnp.mean(vc * vc, axis=-1, keepdims=True)
    v = vc * lax.rsqrt(var + LN_EPS) * lng_ref[...] + lnb_ref[...]
    if emit_v:
        v_ref[...] = v
    vb = v.astype(BF16)
    for c in range(n_chunks):
        rows = slice(c * GM_CHUNK, (c + 1) * GM_CHUNK)
        for g in range(GM_GROUPS):
            cols = slice(g * GM_GROUP_DIM, (g + 1) * GM_GROUP_DIM)
            mixed = jnp.dot(ws_ref[g], vb[rows, cols], preferred_element_type=F32) + bs_ref[:, g:g + 1]
            um_ref[rows, cols] = (u[rows, cols] * mixed).astype(BF16)
    y = jnp.dot(um_ref[...], wout_ref[...], preferred_element_type=F32)
    x1 = x + g1 * _rms(y, ng[1:2])
    x1_ref[...] = x1
    _ffn_prep(x1, ng, sh2, sc2, rw_ref, h_ref, lg_ref)


def _gmlp_mixer(x2d, blk0, n, mod, per_row, tt, tiles_per_batch, ng, win, b_in, ln_g, ln_b, ws, bs, wout,
                rw_t, emit_v):
    out_shape = [jax.ShapeDtypeStruct((n, D), F32), jax.ShapeDtypeStruct((n, DP), jnp.int32),
                 jax.ShapeDtypeStruct((N_EXPERTS, n), F32)]
    out_specs = [pl.BlockSpec((tt, D), lambda i: (i, 0)), pl.BlockSpec((tt, DP), lambda i: (i, 0)),
                 pl.BlockSpec((N_EXPERTS, tt), lambda i: (0, i))]
    if emit_v:
        out_shape.append(jax.ShapeDtypeStruct((n, GM_HALF), F32))
        out_specs.append(pl.BlockSpec((tt, GM_HALF), lambda i: (i, 0)))
    return pl.pallas_call(
        functools.partial(_gmlp_body, n_chunks=tt // GM_CHUNK, emit_v=emit_v),
        grid=(n // tt,),
        in_specs=[pl.BlockSpec((tt, D), lambda i: (i + blk0, 0)),
                  _mod_spec(per_row, tt, tiles_per_batch),
                  _const_spec(ng.shape), _const_spec(win.shape), _const_spec(b_in.shape),
                  _const_spec(ln_g.shape), _const_spec(ln_b.shape), _const_spec(ws.shape),
                  _const_spec(bs.shape), _const_spec(wout.shape), _const_spec(rw_t.shape)],
        out_specs=out_specs,
        out_shape=out_shape,
        scratch_shapes=[pltpu.VMEM((tt, GM_HALF), BF16), pltpu.VMEM((tt, 2 * GM_HALF), F32)],
        compiler_params=_cparams("parallel"),
        name="gmlp_mixer_rows" if per_row else "gmlp_mixer",
    )(x2d, mod, ng, win, b_in, ln_g, ln_b, ws, bs, wout, rw_t)


def _combine_body(x_ref, *refs):
    y_refs = refs[:TOP_K]
    w_ref, h_ref, mod_ref, ng_ref, swg_ref, swu_ref, swd_ref = refs[TOP_K:TOP_K + 7]
    o_ref = refs[-1]
    h_lo, h_hi = _unpack_rows(h_ref[...])
    h_lo = h_lo.astype(BF16)
    h_hi = h_hi.astype(BF16)

    def hdot(w_ref_):
        return (jnp.dot(h_lo, w_ref_[:DP, :], preferred_element_type=F32) +
                jnp.dot(h_hi, w_ref_[DP:, :], preferred_element_type=F32))

    hs = (_silu(hdot(swg_ref)) * hdot(swu_ref)).astype(BF16)
    y = jnp.dot(hs, swd_ref[...], preferred_element_type=F32)
    w = w_ref[...]
    acc_lo = jnp.zeros((x_ref.shape[0], DP), F32)
    acc_hi = jnp.zeros((x_ref.shape[0], DP), F32)
    for k in range(TOP_K):
        lo, hi = _unpack_rows(y_refs[k][...])
        acc_lo += w[:, k:k + 1] * lo
        acc_hi += w[:, k:k + 1] * hi
    y = y + jnp.concatenate([acc_lo, acc_hi], axis=1)
    g2 = mod_ref[:, 5 * D:6 * D]
    o_ref[...] = x_ref[...] + g2 * _rms(y, ng_ref[3:4, :])


def _combine(x2d, y8, wts_t, hp, blk0, mod, per_row, tt, tiles_per_batch, ng, swg, swu, swd,
             out_rows=None, out_blk0=0, out_buf=None):
    n = x2d.shape[0]
    y_specs = [pl.BlockSpec((None, tt, DP), functools.partial(lambda i, k: (k, i + blk0, 0), k=k))
               for k in range(TOP_K)]
    in_specs = ([pl.BlockSpec((tt, D), lambda i: (i, 0))] + y_specs +
                [pl.BlockSpec((tt, TOP_K), lambda i: (i + blk0, 0)),
                 pl.BlockSpec((tt, DP), lambda i: (i + blk0, 0)),
                 _mod_spec(per_row, tt, tiles_per_batch),
                 _const_spec(ng.shape), _const_spec(swg.shape), _const_spec(swu.shape),
                 _const_spec(swd.shape)])
    args = [x2d] + [y8] * TOP_K + [wts_t, hp, mod, ng, swg, swu, swd]
    aliases = {}
    if out_buf is not None:
        in_specs.append(pl.BlockSpec(memory_space=pl.ANY))
        aliases = {len(args): 0}
        args.append(out_buf)
    return pl.pallas_call(
        _combine_body,
        grid=(n // tt,),
        in_specs=in_specs,
        out_specs=pl.BlockSpec((tt, D), lambda i: (i + out_blk0, 0)),
        out_shape=jax.ShapeDtypeStruct((out_rows or n, D), F32),
        input_output_aliases=aliases,
        compiler_params=_cparams("parallel"),
        name="combine_rows" if per_row else "combine",
    )(*args)


def _router_body(lg_ref, bias_ref, tri_ref, eid_ref, rank_ref, wts_ref, cnt_ref, carry_ref, *, n_real):
    step = pl.program_id(0)

    @pl.when(step == 0)
    def _():
        carry_ref[...] = jnp.zeros_like(carry_ref)

    lg = lg_ref[...]
    tn = lg.shape[1]
    real = (step * tn + lax.broadcasted_iota(jnp.int32, (1, tn), 1)) < n_real
    lg = jnp.where(real, lg, 0.0)
    scores = 1.0 / (1.0 + jnp.exp(-lg))
    sel = scores + bias_ref[...]
    neg = -jnp.inf
    sub8 = lax.broadcasted_iota(jnp.int32, (GROUP_SIZE, tn), 0)
    gsub = lax.broadcasted_iota(jnp.int32, (N_EXPERT_GROUPS, tn), 0)
    gs = jnp.zeros((N_EXPERT_GROUPS, tn), F32)
    for g in range(N_EXPERT_GROUPS):
        blk = sel[g * GROUP_SIZE:(g + 1) * GROUP_SIZE, :]
        m1 = jnp.max(blk, axis=0, keepdims=True)
        i1 = jnp.min(jnp.where(blk == m1, sub8, GROUP_SIZE), axis=0, keepdims=True)
        m2 = jnp.max(jnp.where(sub8 == i1, neg, blk), axis=0, keepdims=True)
        gs = jnp.where(gsub == g, m1 + m2, gs)
    gmask = jnp.zeros((N_EXPERT_GROUPS, tn), jnp.bool_)
    for _ in range(TOPK_GROUPS):
        m = jnp.max(gs, axis=0, keepdims=True)
        i = jnp.min(jnp.where(gs == m, gsub, N_EXPERT_GROUPS), axis=0, keepdims=True)
        hit = gsub == i
        gmask = jnp.logical_or(gmask, hit)
        gs = jnp.where(hit, neg, gs)
    gmaskf = gmask.astype(F32)
    blocks = []
    for g in range(N_EXPERT_GROUPS):
        keep = jnp.broadcast_to(gmaskf[g:g + 1, :], (GROUP_SIZE, tn)) > 0.5
        blocks.append(jnp.where(keep, sel[g * GROUP_SIZE:(g + 1) * GROUP_SIZE, :], neg))
    msel = jnp.concatenate(blocks, axis=0)
    esub = lax.broadcasted_iota(jnp.int32, (N_EXPERTS, tn), 0)
    chosen = jnp.zeros((N_EXPERTS, tn), jnp.bool_)
    picks = []
    for _ in range(TOP_K):
        m = jnp.max(msel, axis=0, keepdims=True)
        i = jnp.min(jnp.where(msel == m, esub, N_EXPERTS), axis=0, keepdims=True)
        hit = esub == i
        picks.append(i)
        chosen = jnp.logical_or(chosen, hit)
        msel = jnp.where(hit, neg, msel)
    w = jnp.where(chosen, scores, 0.0)
    w = w / jnp.sum(w, axis=0, keepdims=True) * ROUTED_SCALE
    counted = jnp.where(jnp.logical_and(chosen, real), 1.0, 0.0)
    incl = jnp.dot(counted.astype(BF16), tri_ref[...], preferred_element_type=F32)
    rank_full = carry_ref[:, 0:1] + incl - 1.0
    ksub = lax.broadcasted_iota(jnp.int32, (TOP_K, tn), 0)
    eid = jnp.zeros((TOP_K, tn), jnp.int32)
    rank = jnp.zeros((TOP_K, tn), F32)
    wts = jnp.zeros((TOP_K, tn), F32)
    for k in range(TOP_K):
        hit = esub == picks[k]
        eid = jnp.where(ksub == k, picks[k], eid)
        rank = jnp.where(ksub == k, jnp.sum(jnp.where(hit, rank_full, 0.0), axis=0, keepdims=True), rank)
        wts = jnp.where(ksub == k, jnp.sum(jnp.where(hit, w, 0.0), axis=0, keepdims=True), wts)
    eid_ref[...] = eid
    rank_ref[...] = rank.astype(jnp.int32)
    wts_ref[...] = wts
    carry = carry_ref[...] + incl[:, tn - 1:tn]
    carry_ref[...] = carry
    cnt_ref[...] = carry.astype(jnp.int32)


def _router(lg_t, bias, n_real):
    n = lg_t.shape[1]
    tn = ROUTER_TILE
    idx = jnp.arange(tn)
    tri = (idx[:, None] <= idx[None, :]).astype(BF16)
    kspec = pl.BlockSpec((TOP_K, tn), lambda i: (0, i))
    return pl.pallas_call(
        functools.partial(_router_body, n_real=n_real),
        grid=(n // tn,),
        in_specs=[pl.BlockSpec((N_EXPERTS, tn), lambda i: (0, i)), _const_spec((N_EXPERTS, 1)),
                  _const_spec((tn, tn))],
        out_specs=[kspec, kspec, kspec, _const_spec((N_EXPERTS, LANES))],
        out_shape=[jax.ShapeDtypeStruct((TOP_K, n), jnp.int32), jax.ShapeDtypeStruct((TOP_K, n), jnp.int32),
                   jax.ShapeDtypeStruct((TOP_K, n), F32), jax.ShapeDtypeStruct((N_EXPERTS, LANES), jnp.int32)],
        scratch_shapes=[pltpu.VMEM((N_EXPERTS, LANES), F32)],
        compiler_params=_cparams("arbitrary"),
        name="router",
    )(lg_t, bias.reshape(N_EXPERTS, 1), tri)


def _dest_body(off_ref, eid_ref, rank_ref, dest_ref, *, n_real, last_row):
    eid = eid_ref[...]
    base = jnp.zeros(eid.shape, jnp.int32)
    for e in range(N_EXPERTS):
        base = jnp.where(eid == e, off_ref[e], base)
    tok = lax.broadcasted_iota(jnp.int32, eid.shape, 1)
    slot = lax.broadcasted_iota(jnp.int32, eid.shape, 0)
    unused = last_row - ((tok - n_real) * TOP_K + slot)
    dest_ref[...] = jnp.where(tok < n_real, base + rank_ref[...], unused)


def _dest(off, eid, rank, n_real, last_row):
    spec = pl.BlockSpec(eid.shape, lambda i, off_ref: (0, 0))
    return pl.pallas_call(
        functools.partial(_dest_body, n_real=n_real, last_row=last_row),
        grid_spec=pltpu.PrefetchScalarGridSpec(num_scalar_prefetch=1, grid=(1,), in_specs=[spec, spec],
                                               out_specs=spec),
        out_shape=jax.ShapeDtypeStruct(eid.shape, jnp.int32),
        compiler_params=_cparams("arbitrary"),
        name="dest_rows",
    )(off, eid, rank)


def _sc_mesh():
    return plsc.VectorSubcoreMesh(core_axis_name="core", subcore_axis_name="subcore")


def _sc_dispatch(hp, dest_w, p_alloc):
    n = hp.shape[0]
    w = dest_w.shape[2]

    @functools.partial(pl.kernel, out_type=jax.ShapeDtypeStruct((p_alloc, DP), jnp.int32), mesh=_sc_mesh(),
                       name="sc_dispatch")
    def run(hp_hbm, dest_hbm, xs_hbm):
        def body(x_vmem, i_vmem):
            for k in range(TOP_K):
                pltpu.sync_copy(x_vmem, xs_hbm.at[i_vmem.at[k]])

        pltpu.emit_pipeline(
            body,
            grid=(n // w,),
            in_specs=[pl.BlockSpec((w, DP), lambda i: (i, 0)),
                      pl.BlockSpec((None, TOP_K, w), lambda i: (i, 0, 0))],
            out_specs=[],
            core_axis_name=("core", "subcore"),
            dimension_semantics=(pltpu.PARALLEL,),
        )(hp_hbm, dest_hbm)

    return run(hp, dest_w)


def _sc_gather(ys, dest_g):
    g, _, w = dest_g.shape

    @functools.partial(pl.kernel, out_type=jax.ShapeDtypeStruct((g * w, DP), jnp.int32), mesh=_sc_mesh(),
                       name="sc_gather")
    def run(ys_hbm, dest_hbm, o_hbm):
        def body(i_vmem, o_vmem):
            pltpu.sync_copy(ys_hbm.at[i_vmem.at[0]], o_vmem)

        pltpu.emit_pipeline(
            body,
            grid=(g,),
            in_specs=[pl.BlockSpec((None, 1, w), lambda i: (i, 0, 0))],
            out_specs=[pl.BlockSpec((w, DP), lambda i: (i, 0))],
            core_axis_name=("core", "subcore"),
            dimension_semantics=(pltpu.PARALLEL,),
        )(dest_hbm, o_hbm)

    return run(ys, dest_g)


def _expert_body(first_ref, cnt_ref, nused_ref, xs_hbm, *refs):
    wg_refs = refs[:EXPERT_W_PARTS]
    wu_refs = refs[EXPERT_W_PARTS:2 * EXPERT_W_PARTS]
    wd_refs = refs[2 * EXPERT_W_PARTS:3 * EXPERT_W_PARTS]
    ys_hbm, wg_s, wu_s, wd_s, xbuf, ybuf, xsem, ysem = refs[3 * EXPERT_W_PARTS:]
    _expert_steps(first_ref, cnt_ref, nused_ref, xs_hbm, wg_refs, wu_refs, wd_refs, ys_hbm,
                  wg_s, wu_s, wd_s, xbuf, ybuf, xsem, ysem)


def _expert_steps(first_ref, cnt_ref, nused_ref, xs_hbm, wg_refs, wu_refs, wd_refs, ys_hbm,
                  wg_s, wu_s, wd_s, xbuf, ybuf, xsem, ysem):
    e = pl.program_id(0)
    n_used = nused_ref[0]

    def load(g):
        rows = pl.ds(pl.multiple_of(g * EXPERT_TILE, EXPERT_TILE), EXPERT_TILE)
        slot = g % EXPERT_X_SLOTS
        return pltpu.make_async_copy(xs_hbm.at[rows], xbuf.at[slot], xsem.at[slot])

    def store(g):
        rows = pl.ds(pl.multiple_of(g * EXPERT_TILE, EXPERT_TILE), EXPERT_TILE)
        slot = g % EXPERT_Y_SLOTS
        return pltpu.make_async_copy(ybuf.at[slot], ys_hbm.at[rows], ysem.at[slot])

    @pl.when(e == 0)
    def _():
        for g in range(EXPERT_AHEAD):
            @pl.when(g < n_used)
            def _():
                load(g).start()

    part = D // EXPERT_W_PARTS
    for r in range(EXPERT_W_PARTS):
        wg_s[r * part:(r + 1) * part, :] = wg_refs[r][...].astype(BF16)
        wu_s[r * part:(r + 1) * part, :] = wu_refs[r][...].astype(BF16)
        wd_s[:, r * part:(r + 1) * part] = wd_refs[r][...].astype(BF16)
    first = first_ref[e]
    cnt = cnt_ref[e]

    def acquire(g):
        ahead = g + EXPERT_AHEAD

        @pl.when(ahead < n_used)
        def _():
            load(ahead).start()

        load(g).wait()

        @pl.when(g >= EXPERT_Y_SLOTS)
        def _():
            store(g - EXPERT_Y_SLOTS).wait()

    def compute(g):
        lo, hi = _unpack_rows(xbuf[g % EXPERT_X_SLOTS])
        lo = lo.astype(BF16)
        hi = hi.astype(BF16)

        def xdot(w_s):
            return (jnp.dot(lo, w_s[:DP, :], preferred_element_type=F32) +
                    jnp.dot(hi, w_s[DP:, :], preferred_element_type=F32))

        a = (_silu(xdot(wg_s)) * xdot(wu_s)).astype(BF16)
        ybuf[g % EXPERT_Y_SLOTS] = _pack_rows(jnp.dot(a, wd_s[...], preferred_element_type=F32))

    def pair(j, carry):
        g = first + 2 * j
        acquire(g)
        acquire(g + 1)
        compute(g)
        compute(g + 1)
        store(g).start()
        store(g + 1).start()
        return carry

    lax.fori_loop(0, cnt // 2, pair, 0)

    @pl.when(cnt % 2 == 1)
    def _():
        g = first + cnt - 1
        acquire(g)
        compute(g)
        store(g).start()

    @pl.when(e == N_EXPERTS - 1)
    def _():
        for k in range(EXPERT_Y_SLOTS):
            g = n_used - 1 - k

            @pl.when(g >= 0)
            def _():
                store(g).wait()


def _experts(xs, tile_first, tile_count, n_used, layer, wg, wu, wd):
    part = D // EXPERT_W_PARTS
    in_rows = [pl.BlockSpec((None, None, part, EXPERT_DIM),
                            functools.partial(lambda e, first, cnt, nu, r: (layer, e, r, 0), r=r))
               for r in range(EXPERT_W_PARTS)]
    out_cols = [pl.BlockSpec((None, None, EXPERT_DIM, part),
                             functools.partial(lambda e, first, cnt, nu, c: (layer, e, 0, c), c=c))
                for c in range(EXPERT_W_PARTS)]

    return pl.pallas_call(
        _expert_body,
        grid_spec=pltpu.PrefetchScalarGridSpec(
            num_scalar_prefetch=3, grid=(N_EXPERTS,),
            in_specs=[pl.BlockSpec(memory_space=pl.ANY)] + in_rows + in_rows + out_cols,
            out_specs=pl.BlockSpec(memory_space=pl.ANY),
            scratch_shapes=[pltpu.VMEM((D, EXPERT_DIM), BF16), pltpu.VMEM((D, EXPERT_DIM), BF16),
                            pltpu.VMEM((EXPERT_DIM, D), BF16),
                            pltpu.VMEM((EXPERT_X_SLOTS, EXPERT_TILE, DP), jnp.int32),
                            pltpu.VMEM((EXPERT_Y_SLOTS, EXPERT_TILE, DP), jnp.int32),
                            pltpu.SemaphoreType.DMA((EXPERT_X_SLOTS,)),
                            pltpu.SemaphoreType.DMA((EXPERT_Y_SLOTS,))]),
        out_shape=jax.ShapeDtypeStruct(xs.shape, jnp.int32),
        compiler_params=_cparams("arbitrary"),
        name="experts",
    )(tile_first, tile_count, n_used, xs, *([wg] * EXPERT_W_PARTS), *([wu] * EXPERT_W_PARTS),
      *([wd] * EXPERT_W_PARTS))


def _log_sigmoid(z):
    return jnp.minimum(z, 0.0) - jnp.log(1.0 + jnp.exp(-jnp.abs(z)))


def _gla_gate(hb, wlr_ref, wgk_ref, bgk_ref):
    lr = jnp.dot(hb, wlr_ref[...], preferred_element_type=F32)
    z = _bdot(lr, wgk_ref[...]) + bgk_ref[...]
    return _log_sigmoid(z) * (1.0 / GLA_GATE_NORMALIZER)


def _split3(a):
    hi = a.astype(BF16)
    r1 = a - hi.astype(F32)
    mid = r1.astype(BF16)
    lo = (r1 - mid.astype(F32)).astype(BF16)
    return hi, mid, lo


def _gla_out(o_ref_val, go, gng):
    parts = []
    for hd in range(GLA_HEADS):
        cols = slice(hd * GLA_DV, (hd + 1) * GLA_DV)
        parts.append((_rms(o_ref_val[:, cols], gng) * _silu(go[:, cols])).astype(BF16))
    return jnp.concatenate(parts, axis=1)


def _gla_body(x_ref, mod_ref, ng_ref, wqkvg_ref, wlr_ref, wgk_ref, bgk_ref, tril_ref, gng_ref, wout_ref,
              rw_ref, x1_ref, h_ref, lg_ref, st_ref, st_scr, o_scr, qd_scr, dst_scr, *, tt):
    j = pl.program_id(1)

    @pl.when(j == 0)
    def _():
        st_scr[...] = jnp.zeros_like(st_scr)

    sh1, sc1, g1, sh2, sc2, _ = _mod_slices(mod_ref)
    ng = ng_ref[...]
    x = x_ref[...]
    hb = (_rms(x, ng[0:1]) * (1.0 + sc1) + sh1).astype(BF16)
    proj = jnp.dot(hb, wqkvg_ref[...], preferred_element_type=F32)
    q = proj[:, :GLA_DK_TOT] * (GLA_DK ** -0.5)
    k = proj[:, GLA_DK_TOT:2 * GLA_DK_TOT]
    v = proj[:, 2 * GLA_DK_TOT:2 * GLA_DK_TOT + GLA_DV_TOT].astype(BF16)
    go = proj[:, 2 * GLA_DK_TOT + GLA_DV_TOT:]
    log_a = _gla_gate(hb, wlr_ref, wgk_ref, bgk_ref)
    tril = tril_ref[...]
    b = sum(jnp.dot(tril, part, preferred_element_type=F32) for part in _split3(log_a))
    row = lax.broadcasted_iota(jnp.int32, (GLA_CHUNK, GLA_CHUNK), 0)
    col = lax.broadcasted_iota(jnp.int32, (GLA_CHUNK, GLA_CHUNK), 1)
    causal = row >= col
    n_chunks = tt // GLA_CHUNK
    for c in range(n_chunks):
        rows = slice(c * GLA_CHUNK, (c + 1) * GLA_CHUNK)
        last = (c + 1) * GLA_CHUNK - 1
        for hd in range(GLA_HEADS):
            kc = slice(hd * GLA_DK, (hd + 1) * GLA_DK)
            vc = slice(hd * GLA_DV, (hd + 1) * GLA_DV)
            bb = b[rows, kc]
            b_last = b[last:last + 1, kc]
            q_dec = (q[rows, kc] * jnp.exp(bb)).astype(BF16)
            k_inv = (k[rows, kc] * jnp.exp(-bb)).astype(BF16)
            k_end = (k[rows, kc] * jnp.exp(b_last - bb)).astype(BF16)
            att = jnp.where(causal, _dot_nt(q_dec, k_inv), 0.0).astype(BF16)
            qd_scr[rows, kc] = q_dec
            o_scr[rows, vc] = jnp.dot(att, v[rows, vc], preferred_element_type=F32)
            dst_scr[c * GLA_HEADS + hd] = lax.dot_general(
                v[rows, vc], k_end, (((0,), (0,)), ((), ())), preferred_element_type=F32)
    states = [st_scr[hd] for hd in range(GLA_HEADS)]
    for c in range(n_chunks):
        rows = slice(c * GLA_CHUNK, (c + 1) * GLA_CHUNK)
        last = (c + 1) * GLA_CHUNK - 1
        for hd in range(GLA_HEADS):
            kc = slice(hd * GLA_DK, (hd + 1) * GLA_DK)
            vc = slice(hd * GLA_DV, (hd + 1) * GLA_DV)
            o_scr[rows, vc] += _dot_nt(qd_scr[rows, kc], states[hd].astype(BF16))
            states[hd] = states[hd] * jnp.exp(b[last:last + 1, kc]) + dst_scr[c * GLA_HEADS + hd]
    for hd in range(GLA_HEADS):
        st_scr[hd] = states[hd]

    @pl.when(j == pl.num_programs(1) - 1)
    def _():
        for hd in range(GLA_HEADS):
            st_ref[hd] = st_scr[hd].T

    y = jnp.dot(_gla_out(o_scr[...], go, gng_ref[...]), wout_ref[...], preferred_element_type=F32)
    x1 = x + g1 * _rms(y, ng[1:2])
    x1_ref[...] = x1
    _ffn_prep(x1, ng, sh2, sc2, rw_ref, h_ref, lg_ref)


def _gla_mixer(x2d, mod3, batch, seq, ng, wqkvg, wlr, wgk, bgk, gng, wout, rw_t):
    tt = MIX_TILE
    tpb = seq // tt
    n = x2d.shape[0]
    idx = jnp.arange(tt)
    tril = ((idx[:, None] >= idx[None, :]) &
            (idx[:, None] // GLA_CHUNK == idx[None, :] // GLA_CHUNK)).astype(BF16)
    row_map = lambda b, j: (b * tpb + j, 0)
    consts = (ng, wqkvg, wlr, wgk, bgk, tril, gng, wout, rw_t)
    return pl.pallas_call(
        functools.partial(_gla_body, tt=tt),
        grid=(batch, tpb),
        in_specs=[pl.BlockSpec((tt, D), row_map),
                  pl.BlockSpec((None, 1, 6 * D), lambda b, j: (b, 0, 0))] +
                 [_const_spec(a.shape) for a in consts],
        out_specs=[pl.BlockSpec((tt, D), row_map), pl.BlockSpec((tt, DP), row_map),
                   pl.BlockSpec((N_EXPERTS, tt), lambda b, j: (0, b * tpb + j)),
                   pl.BlockSpec((None, GLA_HEADS, GLA_DK, GLA_DV), lambda b, j: (b, 0, 0, 0))],
        out_shape=[jax.ShapeDtypeStruct((n, D), F32), jax.ShapeDtypeStruct((n, DP), jnp.int32),
                   jax.ShapeDtypeStruct((N_EXPERTS, n), F32),
                   jax.ShapeDtypeStruct((batch, GLA_HEADS, GLA_DK, GLA_DV), F32)],
        scratch_shapes=[pltpu.VMEM((GLA_HEADS, GLA_DV, GLA_DK), F32),
                        pltpu.VMEM((tt, GLA_DV_TOT), F32),
                        pltpu.VMEM((tt, GLA_DK_TOT), BF16),
                        pltpu.VMEM((tt // GLA_CHUNK * GLA_HEADS, GLA_DV, GLA_DK), F32)],
        compiler_params=_cparams("parallel", "arbitrary"),
        name="gla_mixer",
    )(x2d, mod3, *consts)


def _gla1_proj_body(x_ref, mod_ref, ng_ref, wqkvg_ref, wlr_ref, wgk_ref, bgk_ref,
                    q_ref, k_ref, v_ref, go_ref, dec_ref):
    sh1, sc1, _, _, _, _ = _mod_slices(mod_ref)
    ng = ng_ref[...]
    hb = (_rms(x_ref[...], ng[0:1]) * (1.0 + sc1) + sh1).astype(BF16)
    proj = jnp.dot(hb, wqkvg_ref[...], preferred_element_type=F32)
    q_ref[...] = proj[:, :GLA_DK_TOT] * (GLA_DK ** -0.5)
    k_ref[...] = proj[:, GLA_DK_TOT:2 * GLA_DK_TOT]
    v_ref[...] = proj[:, 2 * GLA_DK_TOT:2 * GLA_DK_TOT + GLA_DV_TOT]
    go_ref[...] = proj[:, 2 * GLA_DK_TOT + GLA_DV_TOT:]
    dec_ref[...] = jnp.exp(_gla_gate(hb, wlr_ref, wgk_ref, bgk_ref))


GLA1_TOK = 8


def _gla1_state_body(st_ref, qc_ref, kc_ref, dc_ref, v_ref, nst_ref, o_ref):
    v = v_ref[...]
    for i in range(GLA1_TOK):
        for hd in range(GLA_HEADS):
            vrow = v[i:i + 1, hd * GLA_DV:(hd + 1) * GLA_DV]
            s_new = dc_ref[hd][:, i:i + 1] * st_ref[i, hd] + kc_ref[hd][:, i:i + 1] * vrow
            nst_ref[i, hd] = s_new
            o_ref[i:i + 1, hd * GLA_DV:(hd + 1) * GLA_DV] = jnp.sum(
                qc_ref[hd][:, i:i + 1] * s_new, axis=0, keepdims=True)


def _gla1_out_body(x_ref, o_ref, go_ref, mod_ref, ng_ref, gng_ref, wout_ref, rw_ref, x1_ref, h_ref, lg_ref):
    _, _, g1, sh2, sc2, _ = _mod_slices(mod_ref)
    ng = ng_ref[...]
    y = jnp.dot(_gla_out(o_ref[...], go_ref[...], gng_ref[...]), wout_ref[...], preferred_element_type=F32)
    x1 = x_ref[...] + g1 * _rms(y, ng[1:2])
    x1_ref[...] = x1
    _ffn_prep(x1, ng, sh2, sc2, rw_ref, h_ref, lg_ref)


def _gla_mixer_one(x2d, mod2, state, ng, wqkvg, wlr, wgk, bgk, gng, wout, rw_t):
    n = x2d.shape[0]
    consts = (ng, wqkvg, wlr, wgk, bgk)
    q, k, v, go, dec = pl.pallas_call(
        _gla1_proj_body,
        in_specs=[_const_spec(a.shape) for a in (x2d, mod2) + consts],
        out_specs=[_const_spec((n, GLA_DK_TOT)), _const_spec((n, GLA_DK_TOT)), _const_spec((n, GLA_DV_TOT)),
                   _const_spec((n, GLA_DV_TOT)), _const_spec((n, GLA_DK_TOT))],
        out_shape=[jax.ShapeDtypeStruct((n, GLA_DK_TOT), F32), jax.ShapeDtypeStruct((n, GLA_DK_TOT), F32),
                   jax.ShapeDtypeStruct((n, GLA_DV_TOT), F32), jax.ShapeDtypeStruct((n, GLA_DV_TOT), F32),
                   jax.ShapeDtypeStruct((n, GLA_DK_TOT), F32)],
        grid=(1,),
        compiler_params=_cparams("arbitrary"),
        name="gla1_proj",
    )(x2d, mod2, *consts)

    def cols(a):
        return a.reshape(n // GLA1_TOK, GLA1_TOK, GLA_HEADS, GLA_DK).transpose(0, 2, 3, 1)

    col_spec = pl.BlockSpec((None, GLA_HEADS, GLA_DK, GLA1_TOK), lambda i: (i, 0, 0, 0))
    st_spec = pl.BlockSpec((GLA1_TOK, GLA_HEADS, GLA_DK, GLA_DV), lambda i: (i, 0, 0, 0))
    new_state, o = pl.pallas_call(
        _gla1_state_body,
        grid=(n // GLA1_TOK,),
        in_specs=[st_spec, col_spec, col_spec, col_spec, pl.BlockSpec((GLA1_TOK, GLA_DV_TOT), lambda i: (i, 0))],
        out_specs=[st_spec, pl.BlockSpec((GLA1_TOK, GLA_DV_TOT), lambda i: (i, 0))],
        out_shape=[jax.ShapeDtypeStruct(state.shape, F32), jax.ShapeDtypeStruct((n, GLA_DV_TOT), F32)],
        compiler_params=_cparams("parallel"),
        name="gla1_state",
    )(state, cols(q), cols(k), cols(dec), v)

    consts = (mod2, ng, gng, wout, rw_t)
    x1, h, lg = pl.pallas_call(
        _gla1_out_body,
        grid=(1,),
        in_specs=[_const_spec(a.shape) for a in (x2d, o, go) + consts],
        out_specs=[_const_spec((n, D)), _const_spec((n, DP)), _const_spec((N_EXPERTS, n))],
        out_shape=[jax.ShapeDtypeStruct((n, D), F32), jax.ShapeDtypeStruct((n, DP), jnp.int32),
                   jax.ShapeDtypeStruct((N_EXPERTS, n), F32)],
        compiler_params=_cparams("arbitrary"),
        name="gla1_out",
    )(x2d, o, go, *consts)
    return x1, h, lg, new_state


def _moe_routed(h_p, h_s, lg_p, lg_s, router_bias, layer, wg, wu, wd):
    h, lg = h_p, lg_p
    if h_s is not None:
        h = jnp.concatenate([h_p, h_s], axis=0)
        lg = jnp.concatenate([lg_p, lg_s], axis=1)
    n = h.shape[0]
    n_pad = -(-n // TOKEN_PAD) * TOKEN_PAD
    if n_pad != n:
        h = jnp.pad(h, ((0, n_pad - n), (0, 0)))
        lg = jnp.pad(lg, ((0, 0), (0, n_pad - n)))
    eid, rank, wts, counts = _router(lg, router_bias, n)
    tile_count = ((counts[:, 0] + EXPERT_TILE - 1) // EXPERT_TILE).astype(jnp.int32)
    tile_end = jnp.cumsum(tile_count).astype(jnp.int32)
    tile_first = tile_end - tile_count
    off = tile_first * EXPERT_TILE
    p_alloc = TOP_K * n_pad + N_EXPERTS * EXPERT_TILE
    dest = _dest(off, eid, rank, n, p_alloc - 1)
    dest_w = dest.reshape(TOP_K, n_pad // DISPATCH_W, DISPATCH_W).transpose(1, 0, 2)
    xs = _sc_dispatch(h, dest_w, p_alloc)
    ys = _experts(xs, tile_first, tile_count, tile_end[-1:], layer, wg, wu, wd)
    y8 = _sc_gather(ys, dest.reshape(TOP_K * n_pad // GATHER_W, 1, GATHER_W))
    return y8.reshape(TOP_K, n_pad, DP), wts.T, h


def kernel(x_prompt, x_sample, state_gla, c_prompt, c_sample, norm_g, ada_w, ada_b, gm_w_in, gm_b_in,
           gm_ln_g, gm_ln_b, gm_w_s, gm_b_s, gm_w_out, gla_w_in, gla_w_gk, gla_b_gk, gla_norm_g,
           gla_w_out, router_w, router_bias, exp_w_gate, exp_w_up, exp_w_down, sh_w_gate, sh_w_up,
           sh_w_down):
    batch, seq, _ = x_prompt.shape
    n_s = x_sample.shape[0]
    n_p = batch * seq
    tpb = seq // MIX_TILE
    xp = x_prompt.reshape(n_p, D)
    xs = x_sample.reshape(n_s, D)

    mod = _ada(jnp.concatenate([c_prompt, c_sample], axis=0), ada_w, ada_b)
    mod_p = [mod[i, :batch].reshape(batch, 1, 6 * D) for i in range(2)]
    mod_s = [mod[i, batch:] for i in range(2)]
    rw_t = [jnp.concatenate(_split3(router_w[i].T), axis=0) for i in range(2)]

    ws_causal = jnp.tril(gm_w_s[0]).astype(BF16)
    bs_cols = gm_b_s[0].T
    eye = jnp.eye(GM_CHUNK, dtype=F32)
    ws_first = (gm_w_s[0][:, 0, 0][:, None, None] * eye).astype(BF16)
    bs_first = jnp.broadcast_to(gm_b_s[0][:, 0][None, :], (GM_CHUNK, GM_GROUPS))
    gm_args = (norm_g[0], gm_w_in[0].astype(BF16), gm_b_in[0].reshape(1, -1), gm_ln_g[0].reshape(1, -1),
               gm_ln_b[0].reshape(1, -1))
    wout0 = gm_w_out[0].astype(BF16)
    shared = [(sh_w_gate[i].astype(BF16), sh_w_up[i].astype(BF16), sh_w_down[i].astype(BF16))
              for i in range(2)]
    n_qkvg = 2 * GLA_DK_TOT + 2 * GLA_DV_TOT
    wqkvg = gla_w_in[0][:, :n_qkvg].astype(BF16)
    wlr = jnp.pad(gla_w_in[0][:, n_qkvg:], ((0, 0), (0, LANES - GLA_GATE_RANK))).astype(BF16)
    wgk = jnp.pad(gla_w_gk[0], ((0, LANES - GLA_GATE_RANK), (0, 0))).astype(BF16)
    gla_args = (norm_g[1], wqkvg, wlr, wgk, gla_b_gk[0].reshape(1, -1), gla_norm_g[0].reshape(1, -1),
                gla_w_out[0].astype(BF16), rw_t[1])
    experts = (exp_w_gate, exp_w_up, exp_w_down)

    half = batch // 2
    streams = [(0, half, False), (half, batch - half, True)]
    st = [dict() for _ in streams]

    for s, (b0, nb, with_new) in zip(st, streams):
        s["mod_p"] = [mod_p[i][b0:b0 + nb] for i in range(2)]
        s["n"] = nb * seq
        s["x1p"], s["hp"], s["lgp"] = _gmlp_mixer(xp, b0 * tpb, s["n"], s["mod_p"][0], False, MIX_TILE, tpb,
                                                  *gm_args, ws_causal, bs_cols, wout0, rw_t[0], emit_v=False)
        s["hs"] = s["lgs"] = None
        if with_new:
            s["x1s"], s["hs"], s["lgs"], v_rows = _gmlp_mixer(xs, 0, n_s, mod_s[0], True, n_s, 1, *gm_args,
                                                              ws_first, bs_first, wout0, rw_t[0], emit_v=True)
    for s, (b0, nb, with_new) in zip(st, streams):
        moe = _moe_routed(s["hp"], s["hs"], s["lgp"], s["lgs"], router_bias[0], 0, *experts)
        s["x2p"] = _combine(s["x1p"], *moe, 0, s["mod_p"][0], False, MIX_TILE, tpb, norm_g[0], *shared[0])
        if with_new:
            s["x2s"] = _combine(s["x1s"], *moe, s["n"] // n_s, mod_s[0], True, n_s, 1, norm_g[0], *shared[0])
    for s, (b0, nb, with_new) in zip(st, streams):
        s["x3p"], s["hp"], s["lgp"], s["st_p"] = _gla_mixer(s["x2p"], s["mod_p"][1], nb, seq, *gla_args)
        if with_new:
            s["x3s"], s["hs"], s["lgs"], st_s = _gla_mixer_one(s["x2s"], mod_s[1], state_gla[:, 0], *gla_args)
    y_prompt = None
    for s, (b0, nb, with_new) in zip(st, streams):
        moe = _moe_routed(s["hp"], s["hs"], s["lgp"], s["lgs"], router_bias[1], 1, *experts)
        y_prompt = _combine(s["x3p"], *moe, 0, s["mod_p"][1], False, MIX_TILE, tpb, norm_g[1], *shared[1],
                            out_rows=n_p, out_blk0=b0 * tpb, out_buf=y_prompt)
        if with_new:
            y_new = _combine(s["x3s"], *moe, s["n"] // n_s, mod_s[1], True, n_s, 1, norm_g[1], *shared[1])
    st_p = jnp.concatenate([s["st_p"] for s in st], axis=0)

    return (y_prompt.reshape(batch, seq, D), y_new.reshape(n_s, 1, D), st_p[:, None], st_s[:, None],
            v_rows.reshape(n_s, 1, 1, GM_HALF))
```

```python
import functools
import math

import jax
import jax.numpy as jnp
from jax import lax
from jax.experimental import pallas as pl
from jax.experimental.pallas import tpu as pltpu
from jax.experimental.pallas import tpu_sc as plsc

F32 = jnp.float32
BF16 = jnp.bfloat16

D = 1024
DP = D // 2
GM_CHUNK = 128
GM_HALF = 2 * D
GM_GROUPS = 8
GM_GROUP_DIM = GM_HALF // GM_GROUPS
GLA_HEADS = 4
GLA_DK = 128
GLA_DV = 256
GLA_DK_TOT = GLA_HEADS * GLA_DK
GLA_DV_TOT = GLA_HEADS * GLA_DV
GLA_GATE_RANK = 16
GLA_GATE_NORMALIZER = 16.0
GLA_CHUNK = 64
N_EXPERTS = 64
TOP_K = 8
N_EXPERT_GROUPS = 8
GROUP_SIZE = N_EXPERTS // N_EXPERT_GROUPS
TOPK_GROUPS = 4
EXPERT_DIM = D // 4
ROUTED_SCALE = 2.5
NORM_EPS = 1e-6
LN_EPS = 1e-5

LANES = 128
VMEM_LIMIT = 56 * 1024 * 1024

MIX_TILE = 256
GM_COL_BLOCK = 512
ROUTER_TILE = 512
EXPERT_TILE = 272
EXPERT_X_SLOTS = 6
EXPERT_AHEAD = EXPERT_X_SLOTS - 2
EXPERT_Y_SLOTS = 4
EXPERT_W_PARTS = 4
SC_WORKERS = 32
DISPATCH_W = 32
SC_LANES = 16
SUM_W = 8
TOKEN_PAD = SC_WORKERS * DISPATCH_W


def _cparams(*sem):
    return pltpu.CompilerParams(dimension_semantics=sem, vmem_limit_bytes=VMEM_LIMIT)


def _rms(x, g):
    return x * lax.rsqrt(jnp.mean(x * x, axis=-1, keepdims=True) + NORM_EPS) * g


def _silu(x):
    return x * (1.0 / (1.0 + jnp.exp(-x)))


def _gelu(x):
    return 0.5 * x * (1.0 + lax.erf(x * (1.0 / math.sqrt(2.0))))


def _bdot(a, b):
    return jnp.dot(a.astype(BF16), b.astype(BF16), preferred_element_type=F32)


def _dot_nt(a, b, precision=None):
    return lax.dot_general(a, b, (((1,), (1,)), ((), ())), preferred_element_type=F32,
                           precision=precision)


def _mod_slices(mod_ref):
    return [mod_ref[:, i * D:(i + 1) * D] for i in range(6)]


HI_HALF = -65536


def _pack_rows(x):
    lo = lax.bitcast_convert_type(x[:, :DP].astype(BF16).astype(F32), jnp.int32)
    hi = lax.bitcast_convert_type(x[:, DP:].astype(BF16).astype(F32), jnp.int32)
    return lax.shift_right_logical(lo, 16) | (hi & HI_HALF)


def _unpack_rows(p):
    lo = lax.bitcast_convert_type(lax.shift_left(p, 16), F32)
    hi = lax.bitcast_convert_type(p & HI_HALF, F32)
    return lo, hi


def _ffn_prep(x1, ng, sh2, sc2, rw_ref, h_ref, lg_ref):
    hffn = _rms(x1, ng[2:3]) * (1.0 + sc2) + sh2
    h_ref[...] = _pack_rows(hffn)
    lg3 = _dot_nt(rw_ref[...], hffn.astype(BF16))
    lg_ref[...] = lg3[:N_EXPERTS] + lg3[N_EXPERTS:2 * N_EXPERTS] + lg3[2 * N_EXPERTS:]


def _ada_body(c_ref, w_ref, b_ref, o_ref):
    c = c_ref[...]
    o_ref[...] = _bdot(_silu(c), w_ref[...]) + b_ref[...]


def _ada(c, ada_w, ada_b):
    n = c.shape[0]
    depth = ada_w.shape[0]
    tn = 1536
    return pl.pallas_call(
        _ada_body,
        grid=(depth, 6 * D // tn),
        in_specs=[pl.BlockSpec((n, D), lambda l, j: (0, 0)),
                  pl.BlockSpec((None, D, tn), lambda l, j: (l, 0, j)),
                  pl.BlockSpec((None, 1, tn), lambda l, j: (l, 0, j))],
        out_specs=pl.BlockSpec((None, n, tn), lambda l, j: (l, 0, j)),
        out_shape=jax.ShapeDtypeStruct((depth, n, 6 * D), F32),
        compiler_params=_cparams("parallel", "parallel"),
        name="ada_mod",
    )(c, ada_w, ada_b.reshape(depth, 1, 6 * D))


def _mod_spec(per_row, tt, tiles_per_batch):
    if per_row:
        return pl.BlockSpec((tt, 6 * D), lambda i: (i, 0))
    return pl.BlockSpec((None, 1, 6 * D), lambda i: (i // tiles_per_batch, 0, 0))


def _const_spec(shape):
    zeros = (0,) * len(shape)
    return pl.BlockSpec(shape, lambda *_: zeros)


def _gmlp_body(x_ref, mod_ref, ng_ref, win_ref, bin_ref, lng_ref, lnb_ref, ws_ref, bs_ref, wout_ref,
               rw_ref, x1_ref, h_ref, lg_ref, *rest, n_chunks, emit_v):
    if emit_v:
        v_ref, um_ref, z_ref = rest
    else:
        um_ref, z_ref = rest
    sh1, sc1, g1, sh2, sc2, _ = _mod_slices(mod_ref)
    ng = ng_ref[...]
    x = x_ref[...]
    hb = (_rms(x, ng[0:1]) * (1.0 + sc1) + sh1).astype(BF16)
    for cb in range(2 * GM_HALF // GM_COL_BLOCK):
        cols = slice(cb * GM_COL_BLOCK, (cb + 1) * GM_COL_BLOCK)
        z_ref[:, cols] = _gelu(jnp.dot(hb, win_ref[:, cols], preferred_element_type=F32) + bin_ref[:, cols])
    u = z_ref[:, :GM_HALF]
    v = z_ref[:, GM_HALF:]
    mu = jnp.mean(v, axis=-1, keepdims=True)
    vc = v - mu
    var = jnp.mean(vc * vc, axis=-1, keepdims=True)
    v = vc * lax.rsqrt(var + LN_EPS) * lng_ref[...] + lnb_ref[...]
    if emit_v:
        v_ref[...] = v
    vb = v.astype(BF16)
    for c in range(n_chunks):
        rows = slice(c * GM_CHUNK, (c + 1) * GM_CHUNK)
        for g in range(GM_GROUPS):
            cols = slice(g * GM_GROUP_DIM, (g + 1) * GM_GROUP_DIM)
            mixed = jnp.dot(ws_ref[g], vb[rows, cols], preferred_element_type=F32) + bs_ref[:, g:g + 1]
            um_ref[rows, cols] = (u[rows, cols] * mixed).astype(BF16)
    y = jnp.dot(um_ref[...], wout_ref[...], preferred_element_type=F32)
    x1 = x + g1 * _rms(y, ng[1:2])
    x1_ref[...] = x1
    _ffn_prep(x1, ng, sh2, sc2, rw_ref, h_ref, lg_ref)


def _gmlp_mixer(x2d, blk0, n, mod, per_row, tt, tiles_per_batch, ng, win, b_in, ln_g, ln_b, ws, bs, wout,
                rw_t, emit_v):
    out_shape = [jax.ShapeDtypeStruct((n, D), F32), jax.ShapeDtypeStruct((n, DP), jnp.int32),
                 jax.ShapeDtypeStruct((N_EXPERTS, n), F32)]
    out_specs = [pl.BlockSpec((tt, D), lambda i: (i, 0)), pl.BlockSpec((tt, DP), lambda i: (i, 0)),
                 pl.BlockSpec((N_EXPERTS, tt), lambda i: (0, i))]
    if emit_v:
        out_shape.append(jax.ShapeDtypeStruct((n, GM_HALF), F32))
        out_specs.append(pl.BlockSpec((tt, GM_HALF), lambda i: (i, 0)))
    return pl.pallas_call(
        functools.partial(_gmlp_body, n_chunks=tt // GM_CHUNK, emit_v=emit_v),
        grid=(n // tt,),
        in_specs=[pl.BlockSpec((tt, D), lambda i: (i + blk0, 0)),
                  _mod_spec(per_row, tt, tiles_per_batch),
                  _const_spec(ng.shape), _const_spec(win.shape), _const_spec(b_in.shape),
                  _const_spec(ln_g.shape), _const_spec(ln_b.shape), _const_spec(ws.shape),
                  _const_spec(bs.shape), _const_spec(wout.shape), _const_spec(rw_t.shape)],
        out_specs=out_specs,
        out_shape=out_shape,
        scratch_shapes=[pltpu.VMEM((tt, GM_HALF), BF16), pltpu.VMEM((tt, 2 * GM_HALF), F32)],
        compiler_params=_cparams("parallel"),
        name="gmlp_mixer_rows" if per_row else "gmlp_mixer",
    )(x2d, mod, ng, win, b_in, ln_g, ln_b, ws, bs, wout, rw_t)


def _combine_body(x_ref, y_ref, h_ref, mod_ref, ng_ref, swg_ref, swu_ref, swd_ref, *rest):
    o_ref = rest[-1]
    h_lo, h_hi = _unpack_rows(h_ref[...])
    h_lo = h_lo.astype(BF16)
    h_hi = h_hi.astype(BF16)

    def hdot(w_ref_):
        return (jnp.dot(h_lo, w_ref_[:DP, :], preferred_element_type=F32) +
                jnp.dot(h_hi, w_ref_[DP:, :], preferred_element_type=F32))

    hs = (_silu(hdot(swg_ref)) * hdot(swu_ref)).astype(BF16)
    y = jnp.dot(hs, swd_ref[...], preferred_element_type=F32) + y_ref[...]
    g2 = mod_ref[:, 5 * D:6 * D]
    o_ref[...] = x_ref[...] + g2 * _rms(y, ng_ref[3:4, :])


def _combine(x2d, routed, hp, blk0, mod, per_row, tt, tiles_per_batch, ng, swg, swu, swd,
             out_rows=None, out_blk0=0, out_buf=None):
    n = x2d.shape[0]
    in_specs = [pl.BlockSpec((tt, D), lambda i: (i, 0)),
                pl.BlockSpec((tt, D), lambda i: (i + blk0, 0)),
                pl.BlockSpec((tt, DP), lambda i: (i + blk0, 0)),
                _mod_spec(per_row, tt, tiles_per_batch),
                _const_spec(ng.shape), _const_spec(swg.shape), _const_spec(swu.shape),
                _const_spec(swd.shape)]
    args = [x2d, routed, hp, mod, ng, swg, swu, swd]
    aliases = {}
    if out_buf is not None:
        in_specs.append(pl.BlockSpec(memory_space=pl.ANY))
        aliases = {len(args): 0}
        args.append(out_buf)
    return pl.pallas_call(
        _combine_body,
        grid=(n // tt,),
        in_specs=in_specs,
        out_specs=pl.BlockSpec((tt, D), lambda i: (i + out_blk0, 0)),
        out_shape=jax.ShapeDtypeStruct((out_rows or n, D), F32),
        input_output_aliases=aliases,
        compiler_params=_cparams("parallel"),
        name="combine_rows" if per_row else "combine",
    )(*args)


def _router_body(lg_ref, bias_ref, tri_ref, eid_ref, rank_ref, wts_ref, cnt_ref, carry_ref, *, n_real):
    step = pl.program_id(0)

    @pl.when(step == 0)
    def _():
        carry_ref[...] = jnp.zeros_like(carry_ref)

    lg = lg_ref[...]
    tn = lg.shape[1]
    real = (step * tn + lax.broadcasted_iota(jnp.int32, (1, tn), 1)) < n_real
    lg = jnp.where(real, lg, 0.0)
    scores = 1.0 / (1.0 + jnp.exp(-lg))
    sel = scores + bias_ref[...]
    neg = -jnp.inf
    sub8 = lax.broadcasted_iota(jnp.int32, (GROUP_SIZE, tn), 0)
    gsub = lax.broadcasted_iota(jnp.int32, (N_EXPERT_GROUPS, tn), 0)
    gs = jnp.zeros((N_EXPERT_GROUPS, tn), F32)
    for g in range(N_EXPERT_GROUPS):
        blk = sel[g * GROUP_SIZE:(g + 1) * GROUP_SIZE, :]
        m1 = jnp.max(blk, axis=0, keepdims=True)
        i1 = jnp.min(jnp.where(blk == m1, sub8, GROUP_SIZE), axis=0, keepdims=True)
        m2 = jnp.max(jnp.where(sub8 == i1, neg, blk), axis=0, keepdims=True)
        gs = jnp.where(gsub == g, m1 + m2, gs)
    gmask = jnp.zeros((N_EXPERT_GROUPS, tn), jnp.bool_)
    for _ in range(TOPK_GROUPS):
        m = jnp.max(gs, axis=0, keepdims=True)
        i = jnp.min(jnp.where(gs == m, gsub, N_EXPERT_GROUPS), axis=0, keepdims=True)
        hit = gsub == i
        gmask = jnp.logical_or(gmask, hit)
        gs = jnp.where(hit, neg, gs)
    gmaskf = gmask.astype(F32)
    blocks = []
    for g in range(N_EXPERT_GROUPS):
        keep = jnp.broadcast_to(gmaskf[g:g + 1, :], (GROUP_SIZE, tn)) > 0.5
        blocks.append(jnp.where(keep, sel[g * GROUP_SIZE:(g + 1) * GROUP_SIZE, :], neg))
    msel = jnp.concatenate(blocks, axis=0)
    esub = lax.broadcasted_iota(jnp.int32, (N_EXPERTS, tn), 0)
    chosen = jnp.zeros((N_EXPERTS, tn), jnp.bool_)
    picks = []
    for _ in range(TOP_K):
        m = jnp.max(msel, axis=0, keepdims=True)
        i = jnp.min(jnp.where(msel == m, esub, N_EXPERTS), axis=0, keepdims=True)
        hit = esub == i
        picks.append(i)
        chosen = jnp.logical_or(chosen, hit)
        msel = jnp.where(hit, neg, msel)
    w = jnp.where(chosen, scores, 0.0)
    w = w / jnp.sum(w, axis=0, keepdims=True) * ROUTED_SCALE
    counted = jnp.where(jnp.logical_and(chosen, real), 1.0, 0.0)
    incl = jnp.dot(counted.astype(BF16), tri_ref[...], preferred_element_type=F32)
    rank_full = carry_ref[:, 0:1] + incl - 1.0
    ksub = lax.broadcasted_iota(jnp.int32, (TOP_K, tn), 0)
    eid = jnp.zeros((TOP_K, tn), jnp.int32)
    rank = jnp.zeros((TOP_K, tn), F32)
    wts = jnp.zeros((TOP_K, tn), F32)
    for k in range(TOP_K):
        hit = esub == picks[k]
        eid = jnp.where(ksub == k, picks[k], eid)
        rank = jnp.where(ksub == k, jnp.sum(jnp.where(hit, rank_full, 0.0), axis=0, keepdims=True), rank)
        wts = jnp.where(ksub == k, jnp.sum(jnp.where(hit, w, 0.0), axis=0, keepdims=True), wts)
    eid_ref[...] = eid
    rank_ref[...] = rank.astype(jnp.int32)
    wts_ref[...] = wts
    carry = carry_ref[...] + incl[:, tn - 1:tn]
    carry_ref[...] = carry
    cnt_ref[...] = carry.astype(jnp.int32)


def _router(lg_t, bias, n_real):
    n = lg_t.shape[1]
    tn = ROUTER_TILE
    idx = jnp.arange(tn)
    tri = (idx[:, None] <= idx[None, :]).astype(BF16)
    kspec = pl.BlockSpec((TOP_K, tn), lambda i: (0, i))
    return pl.pallas_call(
        functools.partial(_router_body, n_real=n_real),
        grid=(n // tn,),
        in_specs=[pl.BlockSpec((N_EXPERTS, tn), lambda i: (0, i)), _const_spec((N_EXPERTS, 1)),
                  _const_spec((tn, tn))],
        out_specs=[kspec, kspec, kspec, _const_spec((N_EXPERTS, LANES))],
        out_shape=[jax.ShapeDtypeStruct((TOP_K, n), jnp.int32), jax.ShapeDtypeStruct((TOP_K, n), jnp.int32),
                   jax.ShapeDtypeStruct((TOP_K, n), F32), jax.ShapeDtypeStruct((N_EXPERTS, LANES), jnp.int32)],
        scratch_shapes=[pltpu.VMEM((N_EXPERTS, LANES), F32)],
        compiler_params=_cparams("arbitrary"),
        name="router",
    )(lg_t, bias.reshape(N_EXPERTS, 1), tri)


def _dest_body(off_ref, eid_ref, rank_ref, dest_ref, *, n_real, last_row):
    eid = eid_ref[...]
    base = jnp.zeros(eid.shape, jnp.int32)
    for e in range(N_EXPERTS):
        base = jnp.where(eid == e, off_ref[e], base)
    tok = lax.broadcasted_iota(jnp.int32, eid.shape, 1)
    slot = lax.broadcasted_iota(jnp.int32, eid.shape, 0)
    unused = last_row - ((tok - n_real) * TOP_K + slot)
    dest_ref[...] = jnp.where(tok < n_real, base + rank_ref[...], unused)


def _dest(off, eid, rank, n_real, last_row):
    spec = pl.BlockSpec(eid.shape, lambda i, off_ref: (0, 0))
    return pl.pallas_call(
        functools.partial(_dest_body, n_real=n_real, last_row=last_row),
        grid_spec=pltpu.PrefetchScalarGridSpec(num_scalar_prefetch=1, grid=(1,), in_specs=[spec, spec],
                                               out_specs=spec),
        out_shape=jax.ShapeDtypeStruct(eid.shape, jnp.int32),
        compiler_params=_cparams("arbitrary"),
        name="dest_rows",
    )(off, eid, rank)


def _sc_mesh():
    return plsc.VectorSubcoreMesh(core_axis_name="core", subcore_axis_name="subcore")


def _sc_dispatch(hp, dest_w, p_alloc):
    n = hp.shape[0]
    w = dest_w.shape[2]

    @functools.partial(pl.kernel, out_type=jax.ShapeDtypeStruct((p_alloc, DP), jnp.int32), mesh=_sc_mesh(),
                       name="sc_dispatch")
    def run(hp_hbm, dest_hbm, xs_hbm):
        def body(x_vmem, i_vmem):
            for k in range(TOP_K):
                pltpu.sync_copy(x_vmem, xs_hbm.at[i_vmem.at[k]])

        pltpu.emit_pipeline(
            body,
            grid=(n // w,),
            in_specs=[pl.BlockSpec((w, DP), lambda i: (i, 0)),
                      pl.BlockSpec((None, TOP_K, w), lambda i: (i, 0, 0))],
            out_specs=[],
            core_axis_name=("core", "subcore"),
            dimension_semantics=(pltpu.PARALLEL,),
        )(hp_hbm, dest_hbm)

    return run(hp, dest_w)


def _sc_gather_sum(ys, dest_tm, w_lanes):
    n_win, _, wk = dest_tm.shape
    w = wk // TOP_K
    n_vec = DP // SC_LANES

    @functools.partial(pl.kernel, out_type=jax.ShapeDtypeStruct((n_win * w, D), F32), mesh=_sc_mesh(),
                       scratch_types=[pltpu.VMEM((wk, DP), jnp.int32)],
                       compiler_params=pltpu.CompilerParams(needs_layout_passes=False), name="sc_gather_sum")
    def run(ys_hbm, dest_hbm, w_hbm, o_hbm, rows_v):
        def body(i_vmem, w_vmem, o_vmem):
            pltpu.sync_copy(ys_hbm.at[i_vmem.at[0]], rows_v)

            @pl.loop(0, w)
            def _(t):
                wv = [w_vmem[t * TOP_K + k, :] for k in range(TOP_K)]

                @pl.loop(0, n_vec)
                def _(j):
                    lo = jnp.zeros((SC_LANES,), F32)
                    hi = jnp.zeros((SC_LANES,), F32)
                    for k in range(TOP_K):
                        word = rows_v[t * TOP_K + k, pl.ds(j * SC_LANES, SC_LANES)]
                        lo = lo + wv[k] * plsc.bitcast(lax.shift_left(word, 16), F32)
                        hi = hi + wv[k] * plsc.bitcast(word & HI_HALF, F32)
                    o_vmem[t, pl.ds(j * SC_LANES, SC_LANES)] = lo
                    o_vmem[t, pl.ds(DP + j * SC_LANES, SC_LANES)] = hi

        pltpu.emit_pipeline(
            body,
            grid=(n_win,),
            in_specs=[pl.BlockSpec((None, 1, wk), lambda i: (i, 0, 0)),
                      pl.BlockSpec((wk, SC_LANES), lambda i: (i, 0))],
            out_specs=[pl.BlockSpec((w, D), lambda i: (i, 0))],
            core_axis_name=("core", "subcore"),
            dimension_semantics=(pltpu.PARALLEL,),
        )(dest_hbm, w_hbm, o_hbm)

    return run(ys, dest_tm, w_lanes)


def _expert_body(first_ref, cnt_ref, nused_ref, xs_hbm, *refs):
    wg_refs = refs[:EXPERT_W_PARTS]
    wu_refs = refs[EXPERT_W_PARTS:2 * EXPERT_W_PARTS]
    wd_refs = refs[2 * EXPERT_W_PARTS:3 * EXPERT_W_PARTS]
    ys_hbm, wg_s, wu_s, wd_s, xbuf, ybuf, xsem, ysem = refs[3 * EXPERT_W_PARTS:]
    _expert_steps(first_ref, cnt_ref, nused_ref, xs_hbm, wg_refs, wu_refs, wd_refs, ys_hbm,
                  wg_s, wu_s, wd_s, xbuf, ybuf, xsem, ysem)


def _expert_steps(first_ref, cnt_ref, nused_ref, xs_hbm, wg_refs, wu_refs, wd_refs, ys_hbm,
                  wg_s, wu_s, wd_s, xbuf, ybuf, xsem, ysem):
    e = pl.program_id(0)
    n_used = nused_ref[0]

    def load(g):
        rows = pl.ds(pl.multiple_of(g * EXPERT_TILE, EXPERT_TILE), EXPERT_TILE)
        slot = g % EXPERT_X_SLOTS
        return pltpu.make_async_copy(xs_hbm.at[rows], xbuf.at[slot], xsem.at[slot])

    def store(g):
        rows = pl.ds(pl.multiple_of(g * EXPERT_TILE, EXPERT_TILE), EXPERT_TILE)
        slot = g % EXPERT_Y_SLOTS
        return pltpu.make_async_copy(ybuf.at[slot], ys_hbm.at[rows], ysem.at[slot])

    @pl.when(e == 0)
    def _():
        for g in range(EXPERT_AHEAD):
            @pl.when(g < n_used)
            def _():
                load(g).start()

    part = D // EXPERT_W_PARTS
    for r in range(EXPERT_W_PARTS):
        wg_s[r * part:(r + 1) * part, :] = wg_refs[r][...].astype(BF16)
        wu_s[r * part:(r + 1) * part, :] = wu_refs[r][...].astype(BF16)
        wd_s[:, r * part:(r + 1) * part] = wd_refs[r][...].astype(BF16)
    first = first_ref[e]
    cnt = cnt_ref[e]

    def acquire(g):
        ahead = g + EXPERT_AHEAD

        @pl.when(ahead < n_used)
        def _():
            load(ahead).start()

        load(g).wait()

        @pl.when(g >= EXPERT_Y_SLOTS)
        def _():
            store(g - EXPERT_Y_SLOTS).wait()

    def compute(g):
        lo, hi = _unpack_rows(xbuf[g % EXPERT_X_SLOTS])
        lo = lo.astype(BF16)
        hi = hi.astype(BF16)

        def xdot(w_s):
            return (jnp.dot(lo, w_s[:DP, :], preferred_element_type=F32) +
                    jnp.dot(hi, w_s[DP:, :], preferred_element_type=F32))

        a = (_silu(xdot(wg_s)) * xdot(wu_s)).astype(BF16)
        ybuf[g % EXPERT_Y_SLOTS] = _pack_rows(jnp.dot(a, wd_s[...], preferred_element_type=F32))

    def pair(j, carry):
        g = first + 2 * j
        acquire(g)
        acquire(g + 1)
        compute(g)
        compute(g + 1)
        store(g).start()
        store(g + 1).start()
        return carry

    lax.fori_loop(0, cnt // 2, pair, 0)

    @pl.when(cnt % 2 == 1)
    def _():
        g = first + cnt - 1
        acquire(g)
        compute(g)
        store(g).start()

    @pl.when(e == N_EXPERTS - 1)
    def _():
        for k in range(EXPERT_Y_SLOTS):
            g = n_used - 1 - k

            @pl.when(g >= 0)
            def _():
                store(g).wait()


def _experts(xs, tile_first, tile_count, n_used, layer, wg, wu, wd):
    part = D // EXPERT_W_PARTS
    in_rows = [pl.BlockSpec((None, None, part, EXPERT_DIM),
                            functools.partial(lambda e, first, cnt, nu, r: (layer, e, r, 0), r=r))
               for r in range(EXPERT_W_PARTS)]
    out_cols = [pl.BlockSpec((None, None, EXPERT_DIM, part),
                             functools.partial(lambda e, first, cnt, nu, c: (layer, e, 0, c), c=c))
                for c in range(EXPERT_W_PARTS)]

    return pl.pallas_call(
        _expert_body,
        grid_spec=pltpu.PrefetchScalarGridSpec(
            num_scalar_prefetch=3, grid=(N_EXPERTS,),
            in_specs=[pl.BlockSpec(memory_space=pl.ANY)] + in_rows + in_rows + out_cols,
            out_specs=pl.BlockSpec(memory_space=pl.ANY),
            scratch_shapes=[pltpu.VMEM((D, EXPERT_DIM), BF16), pltpu.VMEM((D, EXPERT_DIM), BF16),
                            pltpu.VMEM((EXPERT_DIM, D), BF16),
                            pltpu.VMEM((EXPERT_X_SLOTS, EXPERT_TILE, DP), jnp.int32),
                            pltpu.VMEM((EXPERT_Y_SLOTS, EXPERT_TILE, DP), jnp.int32),
                            pltpu.SemaphoreType.DMA((EXPERT_X_SLOTS,)),
                            pltpu.SemaphoreType.DMA((EXPERT_Y_SLOTS,))]),
        out_shape=jax.ShapeDtypeStruct(xs.shape, jnp.int32),
        compiler_params=_cparams("arbitrary"),
        name="experts",
    )(tile_first, tile_count, n_used, xs, *([wg] * EXPERT_W_PARTS), *([wu] * EXPERT_W_PARTS),
      *([wd] * EXPERT_W_PARTS))


def _log_sigmoid(z):
    return jnp.minimum(z, 0.0) - jnp.log(1.0 + jnp.exp(-jnp.abs(z)))


def _gla_gate(hb, wlr_ref, wgk_ref, bgk_ref):
    lr = jnp.dot(hb, wlr_ref[...], preferred_element_type=F32)
    z = _bdot(lr, wgk_ref[...]) + bgk_ref[...]
    return _log_sigmoid(z) * (1.0 / GLA_GATE_NORMALIZER)


def _split3(a):
    hi = a.astype(BF16)
    r1 = a - hi.astype(F32)
    mid = r1.astype(BF16)
    lo = (r1 - mid.astype(F32)).astype(BF16)
    return hi, mid, lo


def _gla_out(o_ref_val, go, gng):
    parts = []
    for hd in range(GLA_HEADS):
        cols = slice(hd * GLA_DV, (hd + 1) * GLA_DV)
        parts.append((_rms(o_ref_val[:, cols], gng) * _silu(go[:, cols])).astype(BF16))
    return jnp.concatenate(parts, axis=1)


def _gla_body(x_ref, mod_ref, ng_ref, wqkvg_ref, wlr_ref, wgk_ref, bgk_ref, tril_ref, gng_ref, wout_ref,
              rw_ref, x1_ref, h_ref, lg_ref, st_ref, st_scr, o_scr, qd_scr, dst_scr, *, tt):
    j = pl.program_id(1)

    @pl.when(j == 0)
    def _():
        st_scr[...] = jnp.zeros_like(st_scr)

    sh1, sc1, g1, sh2, sc2, _ = _mod_slices(mod_ref)
    ng = ng_ref[...]
    x = x_ref[...]
    hb = (_rms(x, ng[0:1]) * (1.0 + sc1) + sh1).astype(BF16)
    proj = jnp.dot(hb, wqkvg_ref[...], preferred_element_type=F32)
    q = proj[:, :GLA_DK_TOT] * (GLA_DK ** -0.5)
    k = proj[:, GLA_DK_TOT:2 * GLA_DK_TOT]
    v = proj[:, 2 * GLA_DK_TOT:2 * GLA_DK_TOT + GLA_DV_TOT].astype(BF16)
    go = proj[:, 2 * GLA_DK_TOT + GLA_DV_TOT:]
    log_a = _gla_gate(hb, wlr_ref, wgk_ref, bgk_ref)
    tril = tril_ref[...]
    b = sum(jnp.dot(tril, part, preferred_element_type=F32) for part in _split3(log_a))
    row = lax.broadcasted_iota(jnp.int32, (GLA_CHUNK, GLA_CHUNK), 0)
    col = lax.broadcasted_iota(jnp.int32, (GLA_CHUNK, GLA_CHUNK), 1)
    causal = row >= col
    n_chunks = tt // GLA_CHUNK
    for c in range(n_chunks):
        rows = slice(c * GLA_CHUNK, (c + 1) * GLA_CHUNK)
        last = (c + 1) * GLA_CHUNK - 1
        for hd in range(GLA_HEADS):
            kc = slice(hd * GLA_DK, (hd + 1) * GLA_DK)
            vc = slice(hd * GLA_DV, (hd + 1) * GLA_DV)
            bb = b[rows, kc]
            b_last = b[last:last + 1, kc]
            q_dec = (q[rows, kc] * jnp.exp(bb)).astype(BF16)
            k_inv = (k[rows, kc] * jnp.exp(-bb)).astype(BF16)
            k_end = (k[rows, kc] * jnp.exp(b_last - bb)).astype(BF16)
            att = jnp.where(causal, _dot_nt(q_dec, k_inv), 0.0).astype(BF16)
            qd_scr[rows, kc] = q_dec
            o_scr[rows, vc] = jnp.dot(att, v[rows, vc], preferred_element_type=F32)
            dst_scr[c * GLA_HEADS + hd] = lax.dot_general(
                v[rows, vc], k_end, (((0,), (0,)), ((), ())), preferred_element_type=F32)
    states = [st_scr[hd] for hd in range(GLA_HEADS)]
    for c in range(n_chunks):
        rows = slice(c * GLA_CHUNK, (c + 1) * GLA_CHUNK)
        last = (c + 1) * GLA_CHUNK - 1
        for hd in range(GLA_HEADS):
            kc = slice(hd * GLA_DK, (hd + 1) * GLA_DK)
            vc = slice(hd * GLA_DV, (hd + 1) * GLA_DV)
            o_scr[rows, vc] += _dot_nt(qd_scr[rows, kc], states[hd].astype(BF16))
            states[hd] = states[hd] * jnp.exp(b[last:last + 1, kc]) + dst_scr[c * GLA_HEADS + hd]
    for hd in range(GLA_HEADS):
        st_scr[hd] = states[hd]

    @pl.when(j == pl.num_programs(1) - 1)
    def _():
        for hd in range(GLA_HEADS):
            st_ref[hd] = st_scr[hd].T

    y = jnp.dot(_gla_out(o_scr[...], go, gng_ref[...]), wout_ref[...], preferred_element_type=F32)
    x1 = x + g1 * _rms(y, ng[1:2])
    x1_ref[...] = x1
    _ffn_prep(x1, ng, sh2, sc2, rw_ref, h_ref, lg_ref)


def _gla_mixer(x2d, mod3, batch, seq, ng, wqkvg, wlr, wgk, bgk, gng, wout, rw_t):
    tt = MIX_TILE
    tpb = seq // tt
    n = x2d.shape[0]
    idx = jnp.arange(tt)
    tril = ((idx[:, None] >= idx[None, :]) &
            (idx[:, None] // GLA_CHUNK == idx[None, :] // GLA_CHUNK)).astype(BF16)
    row_map = lambda b, j: (b * tpb + j, 0)
    consts = (ng, wqkvg, wlr, wgk, bgk, tril, gng, wout, rw_t)
    return pl.pallas_call(
        functools.partial(_gla_body, tt=tt),
        grid=(batch, tpb),
        in_specs=[pl.BlockSpec((tt, D), row_map),
                  pl.BlockSpec((None, 1, 6 * D), lambda b, j: (b, 0, 0))] +
                 [_const_spec(a.shape) for a in consts],
        out_specs=[pl.BlockSpec((tt, D), row_map), pl.BlockSpec((tt, DP), row_map),
                   pl.BlockSpec((N_EXPERTS, tt), lambda b, j: (0, b * tpb + j)),
                   pl.BlockSpec((None, GLA_HEADS, GLA_DK, GLA_DV), lambda b, j: (b, 0, 0, 0))],
        out_shape=[jax.ShapeDtypeStruct((n, D), F32), jax.ShapeDtypeStruct((n, DP), jnp.int32),
                   jax.ShapeDtypeStruct((N_EXPERTS, n), F32),
                   jax.ShapeDtypeStruct((batch, GLA_HEADS, GLA_DK, GLA_DV), F32)],
        scratch_shapes=[pltpu.VMEM((GLA_HEADS, GLA_DV, GLA_DK), F32),
                        pltpu.VMEM((tt, GLA_DV_TOT), F32),
                        pltpu.VMEM((tt, GLA_DK_TOT), BF16),
                        pltpu.VMEM((tt // GLA_CHUNK * GLA_HEADS, GLA_DV, GLA_DK), F32)],
        compiler_params=_cparams("parallel", "arbitrary"),
        name="gla_mixer",
    )(x2d, mod3, *consts)


def _gla1_proj_body(x_ref, mod_ref, ng_ref, wqkvg_ref, wlr_ref, wgk_ref, bgk_ref,
                    q_ref, k_ref, v_ref, go_ref, dec_ref):
    sh1, sc1, _, _, _, _ = _mod_slices(mod_ref)
    ng = ng_ref[...]
    hb = (_rms(x_ref[...], ng[0:1]) * (1.0 + sc1) + sh1).astype(BF16)
    proj = jnp.dot(hb, wqkvg_ref[...], preferred_element_type=F32)
    q_ref[...] = proj[:, :GLA_DK_TOT] * (GLA_DK ** -0.5)
    k_ref[...] = proj[:, GLA_DK_TOT:2 * GLA_DK_TOT]
    v_ref[...] = proj[:, 2 * GLA_DK_TOT:2 * GLA_DK_TOT + GLA_DV_TOT]
    go_ref[...] = proj[:, 2 * GLA_DK_TOT + GLA_DV_TOT:]
    dec_ref[...] = jnp.exp(_gla_gate(hb, wlr_ref, wgk_ref, bgk_ref))


GLA1_TOK = 8


def _gla1_state_body(st_ref, qc_ref, kc_ref, dc_ref, v_ref, nst_ref, o_ref):
    v = v_ref[...]
    for i in range(GLA1_TOK):
        for hd in range(GLA_HEADS):
            vrow = v[i:i + 1, hd * GLA_DV:(hd + 1) * GLA_DV]
            s_new = dc_ref[hd][:, i:i + 1] * st_ref[i, hd] + kc_ref[hd][:, i:i + 1] * vrow
            nst_ref[i, hd] = s_new
            o_ref[i:i + 1, hd * GLA_DV:(hd + 1) * GLA_DV] = jnp.sum(
                qc_ref[hd][:, i:i + 1] * s_new, axis=0, keepdims=True)


def _gla1_out_body(x_ref, o_ref, go_ref, mod_ref, ng_ref, gng_ref, wout_ref, rw_ref, x1_ref, h_ref, lg_ref):
    _, _, g1, sh2, sc2, _ = _mod_slices(mod_ref)
    ng = ng_ref[...]
    y = jnp.dot(_gla_out(o_ref[...], go_ref[...], gng_ref[...]), wout_ref[...], preferred_element_type=F32)
    x1 = x_ref[...] + g1 * _rms(y, ng[1:2])
    x1_ref[...] = x1
    _ffn_prep(x1, ng, sh2, sc2, rw_ref, h_ref, lg_ref)


def _gla_mixer_one(x2d, mod2, state, ng, wqkvg, wlr, wgk, bgk, gng, wout, rw_t):
    n = x2d.shape[0]
    consts = (ng, wqkvg, wlr, wgk, bgk)
    q, k, v, go, dec = pl.pallas_call(
        _gla1_proj_body,
        in_specs=[_const_spec(a.shape) for a in (x2d, mod2) + consts],
        out_specs=[_const_spec((n, GLA_DK_TOT)), _const_spec((n, GLA_DK_TOT)), _const_spec((n, GLA_DV_TOT)),
                   _const_spec((n, GLA_DV_TOT)), _const_spec((n, GLA_DK_TOT))],
        out_shape=[jax.ShapeDtypeStruct((n, GLA_DK_TOT), F32), jax.ShapeDtypeStruct((n, GLA_DK_TOT), F32),
                   jax.ShapeDtypeStruct((n, GLA_DV_TOT), F32), jax.ShapeDtypeStruct((n, GLA_DV_TOT), F32),
                   jax.ShapeDtypeStruct((n, GLA_DK_TOT), F32)],
        grid=(1,),
        compiler_params=_cparams("arbitrary"),
        name="gla1_proj",
    )(x2d, mod2, *consts)

    def cols(a):
        return a.reshape(n // GLA1_TOK, GLA1_TOK, GLA_HEADS, GLA_DK).transpose(0, 2, 3, 1)

    col_spec = pl.BlockSpec((None, GLA_HEADS, GLA_DK, GLA1_TOK), lambda i: (i, 0, 0, 0))
    st_spec = pl.BlockSpec((GLA1_TOK, GLA_HEADS, GLA_DK, GLA_DV), lambda i: (i, 0, 0, 0))
    new_state, o = pl.pallas_call(
        _gla1_state_body,
        grid=(n // GLA1_TOK,),
        in_specs=[st_spec, col_spec, col_spec, col_spec, pl.BlockSpec((GLA1_TOK, GLA_DV_TOT), lambda i: (i, 0))],
        out_specs=[st_spec, pl.BlockSpec((GLA1_TOK, GLA_DV_TOT), lambda i: (i, 0))],
        out_shape=[jax.ShapeDtypeStruct(state.shape, F32), jax.ShapeDtypeStruct((n, GLA_DV_TOT), F32)],
        compiler_params=_cparams("parallel"),
        name="gla1_state",
    )(state, cols(q), cols(k), cols(dec), v)

    consts = (mod2, ng, gng, wout, rw_t)
    x1, h, lg = pl.pallas_call(
        _gla1_out_body,
        grid=(1,),
        in_specs=[_const_spec(a.shape) for a in (x2d, o, go) + consts],
        out_specs=[_const_spec((n, D)), _const_spec((n, DP)), _const_spec((N_EXPERTS, n))],
        out_shape=[jax.ShapeDtypeStruct((n, D), F32), jax.ShapeDtypeStruct((n, DP), jnp.int32),
                   jax.ShapeDtypeStruct((N_EXPERTS, n), F32)],
        compiler_params=_cparams("arbitrary"),
        name="gla1_out",
    )(x2d, o, go, *consts)
    return x1, h, lg, new_state


def _moe_routed(h_p, h_s, lg_p, lg_s, router_bias, layer, wg, wu, wd):
    h, lg = h_p, lg_p
    if h_s is not None:
        h = jnp.concatenate([h_p, h_s], axis=0)
        lg = jnp.concatenate([lg_p, lg_s], axis=1)
    n = h.shape[0]
    n_pad = -(-n // TOKEN_PAD) * TOKEN_PAD
    if n_pad != n:
        h = jnp.pad(h, ((0, n_pad - n), (0, 0)))
        lg = jnp.pad(lg, ((0, 0), (0, n_pad - n)))
    eid, rank, wts, counts = _router(lg, router_bias, n)
    tile_count = ((counts[:, 0] + EXPERT_TILE - 1) // EXPERT_TILE).astype(jnp.int32)
    tile_end = jnp.cumsum(tile_count).astype(jnp.int32)
    tile_first = tile_end - tile_count
    off = tile_first * EXPERT_TILE
    p_alloc = TOP_K * n_pad + N_EXPERTS * EXPERT_TILE
    dest = _dest(off, eid, rank, n, p_alloc - 1)
    dest_w = dest.reshape(TOP_K, n_pad // DISPATCH_W, DISPATCH_W).transpose(1, 0, 2)
    xs = _sc_dispatch(h, dest_w, p_alloc)
    ys = _experts(xs, tile_first, tile_count, tile_end[-1:], layer, wg, wu, wd)
    dest_tm = dest.T.reshape(n_pad // SUM_W, 1, SUM_W * TOP_K)
    w_lanes = jnp.broadcast_to(wts.T.reshape(n_pad * TOP_K, 1), (n_pad * TOP_K, SC_LANES))
    return _sc_gather_sum(ys, dest_tm, w_lanes), h


def kernel(x_prompt, x_sample, state_gla, c_prompt, c_sample, norm_g, ada_w, ada_b, gm_w_in, gm_b_in,
           gm_ln_g, gm_ln_b, gm_w_s, gm_b_s, gm_w_out, gla_w_in, gla_w_gk, gla_b_gk, gla_norm_g,
           gla_w_out, router_w, router_bias, exp_w_gate, exp_w_up, exp_w_down, sh_w_gate, sh_w_up,
           sh_w_down):
    batch, seq, _ = x_prompt.shape
    n_s = x_sample.shape[0]
    n_p = batch * seq
    tpb = seq // MIX_TILE
    xp = x_prompt.reshape(n_p, D)
    xs = x_sample.reshape(n_s, D)

    mod = _ada(jnp.concatenate([c_prompt, c_sample], axis=0), ada_w, ada_b)
    mod_p = [mod[i, :batch].reshape(batch, 1, 6 * D) for i in range(2)]
    mod_s = [mod[i, batch:] for i in range(2)]
    rw_t = [jnp.concatenate(_split3(router_w[i].T), axis=0) for i in range(2)]

    ws_causal = jnp.tril(gm_w_s[0]).astype(BF16)
    bs_cols = gm_b_s[0].T
    eye = jnp.eye(GM_CHUNK, dtype=F32)
    ws_first = (gm_w_s[0][:, 0, 0][:, None, None] * eye).astype(BF16)
    bs_first = jnp.broadcast_to(gm_b_s[0][:, 0][None, :], (GM_CHUNK, GM_GROUPS))
    gm_args = (norm_g[0], gm_w_in[0].astype(BF16), gm_b_in[0].reshape(1, -1), gm_ln_g[0].reshape(1, -1),
               gm_ln_b[0].reshape(1, -1))
    wout0 = gm_w_out[0].astype(BF16)
    shared = [(sh_w_gate[i].astype(BF16), sh_w_up[i].astype(BF16), sh_w_down[i].astype(BF16))
              for i in range(2)]
    n_qkvg = 2 * GLA_DK_TOT + 2 * GLA_DV_TOT
    wqkvg = gla_w_in[0][:, :n_qkvg].astype(BF16)
    wlr = jnp.pad(gla_w_in[0][:, n_qkvg:], ((0, 0), (0, LANES - GLA_GATE_RANK))).astype(BF16)
    wgk = jnp.pad(gla_w_gk[0], ((0, LANES - GLA_GATE_RANK), (0, 0))).astype(BF16)
    gla_args = (norm_g[1], wqkvg, wlr, wgk, gla_b_gk[0].reshape(1, -1), gla_norm_g[0].reshape(1, -1),
                gla_w_out[0].astype(BF16), rw_t[1])
    experts = (exp_w_gate, exp_w_up, exp_w_down)

    half = batch // 2
    streams = [(0, half, False), (half, batch - half, True)]
    st = [dict() for _ in streams]

    for s, (b0, nb, with_new) in zip(st, streams):
        s["mod_p"] = [mod_p[i][b0:b0 + nb] for i in range(2)]
        s["n"] = nb * seq
        s["x1p"], s["hp"], s["lgp"] = _gmlp_mixer(xp, b0 * tpb, s["n"], s["mod_p"][0], False, MIX_TILE, tpb,
                                                  *gm_args, ws_causal, bs_cols, wout0, rw_t[0], emit_v=False)
        s["hs"] = s["lgs"] = None
        if with_new:
            s["x1s"], s["hs"], s["lgs"], v_rows = _gmlp_mixer(xs, 0, n_s, mod_s[0], True, n_s, 1, *gm_args,
                                                              ws_first, bs_first, wout0, rw_t[0], emit_v=True)
    for s, (b0, nb, with_new) in zip(st, streams):
        moe = _moe_routed(s["hp"], s["hs"], s["lgp"], s["lgs"], router_bias[0], 0, *experts)
        s["x2p"] = _combine(s["x1p"], *moe, 0, s["mod_p"][0], False, MIX_TILE, tpb, norm_g[0], *shared[0])
        if with_new:
            s["x2s"] = _combine(s["x1s"], *moe, s["n"] // n_s, mod_s[0], True, n_s, 1, norm_g[0], *shared[0])
    for s, (b0, nb, with_new) in zip(st, streams):
        s["x3p"], s["hp"], s["lgp"], s["st_p"] = _gla_mixer(s["x2p"], s["mod_p"][1], nb, seq, *gla_args)
        if with_new:
            s["x3s"], s["hs"], s["lgs"], st_s = _gla_mixer_one(s["x2s"], mod_s[1], state_gla[:, 0], *gla_args)
    y_prompt = None
    for s, (b0, nb, with_new) in zip(st, streams):
        moe = _moe_routed(s["hp"], s["hs"], s["lgp"], s["lgs"], router_bias[1], 1, *experts)
        y_prompt = _combine(s["x3p"], *moe, 0, s["mod_p"][1], False, MIX_TILE, tpb, norm_g[1], *shared[1],
                            out_rows=n_p, out_blk0=b0 * tpb, out_buf=y_prompt)
        if with_new:
            y_new = _combine(s["x3s"], *moe, s["n"] // n_s, mod_s[1], True, n_s, 1, norm_g[1], *shared[1])
    st_p = jnp.concatenate([s["st_p"] for s in st], axis=0)

    return (y_prompt.reshape(batch, seq, D), y_new.reshape(n_s, 1, D), st_p[:, None], st_s[:, None],
            v_rows.reshape(n_s, 1, 1, GM_HALF))
```

```python
import functools
import math

import jax
import jax.numpy as jnp
from jax import lax
from jax.experimental import pallas as pl
from jax.experimental.pallas import tpu as pltpu
from jax.experimental.pallas import tpu_sc as plsc

F32 = jnp.float32
BF16 = jnp.bfloat16

D = 1024
DP = D // 2
GM_CHUNK = 128
GM_HALF = 2 * D
GM_GROUPS = 8
GM_GROUP_DIM = GM_HALF // GM_GROUPS
GLA_HEADS = 4
GLA_DK = 128
GLA_DV = 256
GLA_DK_TOT = GLA_HEADS * GLA_DK
GLA_DV_TOT = GLA_HEADS * GLA_DV
GLA_GATE_RANK = 16
GLA_GATE_NORMALIZER = 16.0
GLA_CHUNK = 64
N_EXPERTS = 64
TOP_K = 8
N_EXPERT_GROUPS = 8
GROUP_SIZE = N_EXPERTS // N_EXPERT_GROUPS
TOPK_GROUPS = 4
EXPERT_DIM = D // 4
ROUTED_SCALE = 2.5
NORM_EPS = 1e-6
LN_EPS = 1e-5

LANES = 128
VMEM_LIMIT = 56 * 1024 * 1024

MIX_TILE = 256
GM_COL_BLOCK = 512
ROUTER_TILE = 512
EXPERT_TILE = 272
EXPERT_X_SLOTS = 6
EXPERT_AHEAD = EXPERT_X_SLOTS - 2
EXPERT_Y_SLOTS = 4
EXPERT_W_PARTS = 4
SC_WORKERS = 32
DISPATCH_W = 32
SC_LANES = 16
SUM_W = 16
SUM_PARTS = 4
SUM_UNROLL = 4
TOKEN_PAD = SC_WORKERS * DISPATCH_W


def _cparams(*sem):
    return pltpu.CompilerParams(dimension_semantics=sem, vmem_limit_bytes=VMEM_LIMIT)


def _rms(x, g):
    return x * lax.rsqrt(jnp.mean(x * x, axis=-1, keepdims=True) + NORM_EPS) * g


def _silu(x):
    return x * (1.0 / (1.0 + jnp.exp(-x)))


def _gelu(x):
    return 0.5 * x * (1.0 + lax.erf(x * (1.0 / math.sqrt(2.0))))


def _bdot(a, b):
    return jnp.dot(a.astype(BF16), b.astype(BF16), preferred_element_type=F32)


def _dot_nt(a, b, precision=None):
    return lax.dot_general(a, b, (((1,), (1,)), ((), ())), preferred_element_type=F32,
                           precision=precision)


def _mod_slices(mod_ref):
    return [mod_ref[:, i * D:(i + 1) * D] for i in range(6)]


HI_HALF = -65536


def _pack_rows(x):
    lo = lax.bitcast_convert_type(x[:, :DP].astype(BF16).astype(F32), jnp.int32)
    hi = lax.bitcast_convert_type(x[:, DP:].astype(BF16).astype(F32), jnp.int32)
    return lax.shift_right_logical(lo, 16) | (hi & HI_HALF)


def _unpack_rows(p):
    lo = lax.bitcast_convert_type(lax.shift_left(p, 16), F32)
    hi = lax.bitcast_convert_type(p & HI_HALF, F32)
    return lo, hi


def _ffn_prep(x1, ng, sh2, sc2, rw_ref, h_ref, lg_ref):
    hffn = _rms(x1, ng[2:3]) * (1.0 + sc2) + sh2
    h_ref[...] = _pack_rows(hffn)
    lg3 = _dot_nt(rw_ref[...], hffn.astype(BF16))
    lg_ref[...] = lg3[:N_EXPERTS] + lg3[N_EXPERTS:2 * N_EXPERTS] + lg3[2 * N_EXPERTS:]


def _ada_body(c_ref, w_ref, b_ref, o_ref):
    c = c_ref[...]
    o_ref[...] = _bdot(_silu(c), w_ref[...]) + b_ref[...]


def _ada(c, ada_w, ada_b):
    n = c.shape[0]
    depth = ada_w.shape[0]
    tn = 1536
    return pl.pallas_call(
        _ada_body,
        grid=(depth, 6 * D // tn),
        in_specs=[pl.BlockSpec((n, D), lambda l, j: (0, 0)),
                  pl.BlockSpec((None, D, tn), lambda l, j: (l, 0, j)),
                  pl.BlockSpec((None, 1, tn), lambda l, j: (l, 0, j))],
        out_specs=pl.BlockSpec((None, n, tn), lambda l, j: (l, 0, j)),
        out_shape=jax.ShapeDtypeStruct((depth, n, 6 * D), F32),
        compiler_params=_cparams("parallel", "parallel"),
        name="ada_mod",
    )(c, ada_w, ada_b.reshape(depth, 1, 6 * D))


def _mod_spec(per_row, tt, tiles_per_batch):
    if per_row:
        return pl.BlockSpec((tt, 6 * D), lambda i: (i, 0))
    return pl.BlockSpec((None, 1, 6 * D), lambda i: (i // tiles_per_batch, 0, 0))


def _const_spec(shape):
    zeros = (0,) * len(shape)
    return pl.BlockSpec(shape, lambda *_: zeros)


def _gmlp_body(x_ref, mod_ref, ng_ref, win_ref, bin_ref, lng_ref, lnb_ref, ws_ref, bs_ref, wout_ref,
               rw_ref, x1_ref, h_ref, lg_ref, *rest, n_chunks, emit_v):
    if emit_v:
        v_ref, um_ref, z_ref = rest
    else:
        um_ref, z_ref = rest
    sh1, sc1, g1, sh2, sc2, _ = _mod_slices(mod_ref)
    ng = ng_ref[...]
    x = x_ref[...]
    hb = (_rms(x, ng[0:1]) * (1.0 + sc1) + sh1).astype(BF16)
    for cb in range(2 * GM_HALF // GM_COL_BLOCK):
        cols = slice(cb * GM_COL_BLOCK, (cb + 1) * GM_COL_BLOCK)
        z_ref[:, cols] = _gelu(jnp.dot(hb, win_ref[:, cols], preferred_element_type=F32) + bin_ref[:, cols])
    u = z_ref[:, :GM_HALF]
    v = z_ref[:, GM_HALF:]
    mu = jnp.mean(v, axis=-1, keepdims=True)
    vc = v - mu
    var = jnp.mean(vc * vc, axis=-1, keepdims=True)
    v = vc * lax.rsqrt(var + LN_EPS) * lng_ref[...] + lnb_ref[...]
    if emit_v:
        v_ref[...] = v
    vb = v.astype(BF16)
    for c in range(n_chunks):
        rows = slice(c * GM_CHUNK, (c + 1) * GM_CHUNK)
        for g in range(GM_GROUPS):
            cols = slice(g * GM_GROUP_DIM, (g + 1) * GM_GROUP_DIM)
            mixed = jnp.dot(ws_ref[g], vb[rows, cols], preferred_element_type=F32) + bs_ref[:, g:g + 1]
            um_ref[rows, cols] = (u[rows, cols] * mixed).astype(BF16)
    y = jnp.dot(um_ref[...], wout_ref[...], preferred_element_type=F32)
    x1 = x + g1 * _rms(y, ng[1:2])
    x1_ref[...] = x1
    _ffn_prep(x1, ng, sh2, sc2, rw_ref, h_ref, lg_ref)


def _gmlp_mixer(x2d, blk0, n, mod, per_row, tt, tiles_per_batch, ng, win, b_in, ln_g, ln_b, ws, bs, wout,
                rw_t, emit_v):
    out_shape = [jax.ShapeDtypeStruct((n, D), F32), jax.ShapeDtypeStruct((n, DP), jnp.int32),
                 jax.ShapeDtypeStruct((N_EXPERTS, n), F32)]
    out_specs = [pl.BlockSpec((tt, D), lambda i: (i, 0)), pl.BlockSpec((tt, DP), lambda i: (i, 0)),
                 pl.BlockSpec((N_EXPERTS, tt), lambda i: (0, i))]
    if emit_v:
        out_shape.append(jax.ShapeDtypeStruct((n, GM_HALF), F32))
        out_specs.append(pl.BlockSpec((tt, GM_HALF), lambda i: (i, 0)))
    return pl.pallas_call(
        functools.partial(_gmlp_body, n_chunks=tt // GM_CHUNK, emit_v=emit_v),
        grid=(n // tt,),
        in_specs=[pl.BlockSpec((tt, D), lambda i: (i + blk0, 0)),
                  _mod_spec(per_row, tt, tiles_per_batch),
                  _const_spec(ng.shape), _const_spec(win.shape), _const_spec(b_in.shape),
                  _const_spec(ln_g.shape), _const_spec(ln_b.shape), _const_spec(ws.shape),
                  _const_spec(bs.shape), _const_spec(wout.shape), _const_spec(rw_t.shape)],
        out_specs=out_specs,
        out_shape=out_shape,
        scratch_shapes=[pltpu.VMEM((tt, GM_HALF), BF16), pltpu.VMEM((tt, 2 * GM_HALF), F32)],
        compiler_params=_cparams("parallel"),
        name="gmlp_mixer_rows" if per_row else "gmlp_mixer",
    )(x2d, mod, ng, win, b_in, ln_g, ln_b, ws, bs, wout, rw_t)


def _combine_body(x_ref, y_ref, h_ref, mod_ref, ng_ref, swg_ref, swu_ref, swd_ref, *rest):
    o_ref = rest[-1]
    h_lo, h_hi = _unpack_rows(h_ref[...])
    h_lo = h_lo.astype(BF16)
    h_hi = h_hi.astype(BF16)

    def hdot(w_ref_):
        return (jnp.dot(h_lo, w_ref_[:DP, :], preferred_element_type=F32) +
                jnp.dot(h_hi, w_ref_[DP:, :], preferred_element_type=F32))

    hs = (_silu(hdot(swg_ref)) * hdot(swu_ref)).astype(BF16)
    y = jnp.dot(hs, swd_ref[...], preferred_element_type=F32) + y_ref[...]
    g2 = mod_ref[:, 5 * D:6 * D]
    o_ref[...] = x_ref[...] + g2 * _rms(y, ng_ref[3:4, :])


def _combine(x2d, routed, hp, blk0, mod, per_row, tt, tiles_per_batch, ng, swg, swu, swd,
             out_rows=None, out_blk0=0, out_buf=None):
    n = x2d.shape[0]
    in_specs = [pl.BlockSpec((tt, D), lambda i: (i, 0)),
                pl.BlockSpec((tt, D), lambda i: (i + blk0, 0)),
                pl.BlockSpec((tt, DP), lambda i: (i + blk0, 0)),
                _mod_spec(per_row, tt, tiles_per_batch),
                _const_spec(ng.shape), _const_spec(swg.shape), _const_spec(swu.shape),
                _const_spec(swd.shape)]
    args = [x2d, routed, hp, mod, ng, swg, swu, swd]
    aliases = {}
    if out_buf is not None:
        in_specs.append(pl.BlockSpec(memory_space=pl.ANY))
        aliases = {len(args): 0}
        args.append(out_buf)
    return pl.pallas_call(
        _combine_body,
        grid=(n // tt,),
        in_specs=in_specs,
        out_specs=pl.BlockSpec((tt, D), lambda i: (i + out_blk0, 0)),
        out_shape=jax.ShapeDtypeStruct((out_rows or n, D), F32),
        input_output_aliases=aliases,
        compiler_params=_cparams("parallel"),
        name="combine_rows" if per_row else "combine",
    )(*args)


def _router_body(lg_ref, bias_ref, tri_ref, eid_ref, rank_ref, wts_ref, cnt_ref, carry_ref, *, n_real):
    step = pl.program_id(0)

    @pl.when(step == 0)
    def _():
        carry_ref[...] = jnp.zeros_like(carry_ref)

    lg = lg_ref[...]
    tn = lg.shape[1]
    real = (step * tn + lax.broadcasted_iota(jnp.int32, (1, tn), 1)) < n_real
    lg = jnp.where(real, lg, 0.0)
    scores = 1.0 / (1.0 + jnp.exp(-lg))
    sel = scores + bias_ref[...]
    neg = -jnp.inf
    sub8 = lax.broadcasted_iota(jnp.int32, (GROUP_SIZE, tn), 0)
    gsub = lax.broadcasted_iota(jnp.int32, (N_EXPERT_GROUPS, tn), 0)
    gs = jnp.zeros((N_EXPERT_GROUPS, tn), F32)
    for g in range(N_EXPERT_GROUPS):
        blk = sel[g * GROUP_SIZE:(g + 1) * GROUP_SIZE, :]
        m1 = jnp.max(blk, axis=0, keepdims=True)
        i1 = jnp.min(jnp.where(blk == m1, sub8, GROUP_SIZE), axis=0, keepdims=True)
        m2 = jnp.max(jnp.where(sub8 == i1, neg, blk), axis=0, keepdims=True)
        gs = jnp.where(gsub == g, m1 + m2, gs)
    gmask = jnp.zeros((N_EXPERT_GROUPS, tn), jnp.bool_)
    for _ in range(TOPK_GROUPS):
        m = jnp.max(gs, axis=0, keepdims=True)
        i = jnp.min(jnp.where(gs == m, gsub, N_EXPERT_GROUPS), axis=0, keepdims=True)
        hit = gsub == i
        gmask = jnp.logical_or(gmask, hit)
        gs = jnp.where(hit, neg, gs)
    gmaskf = gmask.astype(F32)
    blocks = []
    for g in range(N_EXPERT_GROUPS):
        keep = jnp.broadcast_to(gmaskf[g:g + 1, :], (GROUP_SIZE, tn)) > 0.5
        blocks.append(jnp.where(keep, sel[g * GROUP_SIZE:(g + 1) * GROUP_SIZE, :], neg))
    msel = jnp.concatenate(blocks, axis=0)
    esub = lax.broadcasted_iota(jnp.int32, (N_EXPERTS, tn), 0)
    chosen = jnp.zeros((N_EXPERTS, tn), jnp.bool_)
    picks = []
    for _ in range(TOP_K):
        m = jnp.max(msel, axis=0, keepdims=True)
        i = jnp.min(jnp.where(msel == m, esub, N_EXPERTS), axis=0, keepdims=True)
        hit = esub == i
        picks.append(i)
        chosen = jnp.logical_or(chosen, hit)
        msel = jnp.where(hit, neg, msel)
    w = jnp.where(chosen, scores, 0.0)
    w = w / jnp.sum(w, axis=0, keepdims=True) * ROUTED_SCALE
    counted = jnp.where(jnp.logical_and(chosen, real), 1.0, 0.0)
    incl = jnp.dot(counted.astype(BF16), tri_ref[...], preferred_element_type=F32)
    rank_full = carry_ref[:, 0:1] + incl - 1.0
    ksub = lax.broadcasted_iota(jnp.int32, (TOP_K, tn), 0)
    eid = jnp.zeros((TOP_K, tn), jnp.int32)
    rank = jnp.zeros((TOP_K, tn), F32)
    wts = jnp.zeros((TOP_K, tn), F32)
    for k in range(TOP_K):
        hit = esub == picks[k]
        eid = jnp.where(ksub == k, picks[k], eid)
        rank = jnp.where(ksub == k, jnp.sum(jnp.where(hit, rank_full, 0.0), axis=0, keepdims=True), rank)
        wts = jnp.where(ksub == k, jnp.sum(jnp.where(hit, w, 0.0), axis=0, keepdims=True), wts)
    eid_ref[...] = eid
    rank_ref[...] = rank.astype(jnp.int32)
    wts_ref[...] = wts
    carry = carry_ref[...] + incl[:, tn - 1:tn]
    carry_ref[...] = carry
    cnt_ref[...] = carry.astype(jnp.int32)


def _router(lg_t, bias, n_real):
    n = lg_t.shape[1]
    tn = ROUTER_TILE
    idx = jnp.arange(tn)
    tri = (idx[:, None] <= idx[None, :]).astype(BF16)
    kspec = pl.BlockSpec((TOP_K, tn), lambda i: (0, i))
    return pl.pallas_call(
        functools.partial(_router_body, n_real=n_real),
        grid=(n // tn,),
        in_specs=[pl.BlockSpec((N_EXPERTS, tn), lambda i: (0, i)), _const_spec((N_EXPERTS, 1)),
                  _const_spec((tn, tn))],
        out_specs=[kspec, kspec, kspec, _const_spec((N_EXPERTS, LANES))],
        out_shape=[jax.ShapeDtypeStruct((TOP_K, n), jnp.int32), jax.ShapeDtypeStruct((TOP_K, n), jnp.int32),
                   jax.ShapeDtypeStruct((TOP_K, n), F32), jax.ShapeDtypeStruct((N_EXPERTS, LANES), jnp.int32)],
        scratch_shapes=[pltpu.VMEM((N_EXPERTS, LANES), F32)],
        compiler_params=_cparams("arbitrary"),
        name="router",
    )(lg_t, bias.reshape(N_EXPERTS, 1), tri)


def _dest_body(off_ref, eid_ref, rank_ref, dest_ref, *, n_real, last_row):
    eid = eid_ref[...]
    base = jnp.zeros(eid.shape, jnp.int32)
    for e in range(N_EXPERTS):
        base = jnp.where(eid == e, off_ref[e], base)
    tok = lax.broadcasted_iota(jnp.int32, eid.shape, 1)
    slot = lax.broadcasted_iota(jnp.int32, eid.shape, 0)
    unused = last_row - ((tok - n_real) * TOP_K + slot)
    dest_ref[...] = jnp.where(tok < n_real, base + rank_ref[...], unused)


def _dest(off, eid, rank, n_real, last_row):
    spec = pl.BlockSpec(eid.shape, lambda i, off_ref: (0, 0))
    return pl.pallas_call(
        functools.partial(_dest_body, n_real=n_real, last_row=last_row),
        grid_spec=pltpu.PrefetchScalarGridSpec(num_scalar_prefetch=1, grid=(1,), in_specs=[spec, spec],
                                               out_specs=spec),
        out_shape=jax.ShapeDtypeStruct(eid.shape, jnp.int32),
        compiler_params=_cparams("arbitrary"),
        name="dest_rows",
    )(off, eid, rank)


def _sc_mesh():
    return plsc.VectorSubcoreMesh(core_axis_name="core", subcore_axis_name="subcore")


def _sc_dispatch(hp, dest_w, p_alloc):
    n = hp.shape[0]
    w = dest_w.shape[2]

    @functools.partial(pl.kernel, out_type=jax.ShapeDtypeStruct((p_alloc, DP), jnp.int32), mesh=_sc_mesh(),
                       name="sc_dispatch")
    def run(hp_hbm, dest_hbm, xs_hbm):
        def body(x_vmem, i_vmem):
            for k in range(TOP_K):
                pltpu.sync_copy(x_vmem, xs_hbm.at[i_vmem.at[k]])

        pltpu.emit_pipeline(
            body,
            grid=(n // w,),
            in_specs=[pl.BlockSpec((w, DP), lambda i: (i, 0)),
                      pl.BlockSpec((None, TOP_K, w), lambda i: (i, 0, 0))],
            out_specs=[],
            core_axis_name=("core", "subcore"),
            dimension_semantics=(pltpu.PARALLEL,),
        )(hp_hbm, dest_hbm)

    return run(hp, dest_w)


def _sc_gather_sum(ys, dest_tm, w_lanes):
    n_win, parts, pk = dest_tm.shape
    w = parts * pk // TOP_K
    wp = w // parts
    n_vec = DP // SC_LANES

    @functools.partial(pl.kernel, out_type=jax.ShapeDtypeStruct((n_win * w, D), F32), mesh=_sc_mesh(),
                       scratch_types=[pltpu.VMEM((parts, pk, DP), jnp.int32), pltpu.SemaphoreType.DMA((parts,))],
                       compiler_params=pltpu.CompilerParams(needs_layout_passes=False), name="sc_gather_sum")
    def run(ys_hbm, dest_hbm, w_hbm, o_hbm, rows_v, sems):
        def body(i_vmem, w_vmem, o_vmem):
            copies = [pltpu.async_copy(ys_hbm.at[i_vmem.at[p]], rows_v.at[p], sems.at[p]) for p in range(parts)]
            for p in range(parts):
                copies[p].wait()

                @pl.loop(0, wp)
                def _(t):
                    tok = p * wp + t
                    wv = [w_vmem[tok, pl.ds(k * SC_LANES, SC_LANES)] for k in range(TOP_K)]

                    @pl.loop(0, n_vec, step=SUM_UNROLL)
                    def _(j0):
                        for u in range(SUM_UNROLL):
                            col = (j0 + u) * SC_LANES
                            lo = jnp.zeros((SC_LANES,), F32)
                            hi = jnp.zeros((SC_LANES,), F32)
                            for k in range(TOP_K):
                                word = rows_v[p, t * TOP_K + k, pl.ds(col, SC_LANES)]
                                lo = lo + wv[k] * plsc.bitcast(lax.shift_left(word, 16), F32)
                                hi = hi + wv[k] * plsc.bitcast(word & HI_HALF, F32)
                            o_vmem[tok, pl.ds(col, SC_LANES)] = lo
                            o_vmem[tok, pl.ds(DP + col, SC_LANES)] = hi

        pltpu.emit_pipeline(
            body,
            grid=(n_win,),
            in_specs=[pl.BlockSpec((None, parts, pk), lambda i: (i, 0, 0)),
                      pl.BlockSpec((w, TOP_K * SC_LANES), lambda i: (i, 0))],
            out_specs=[pl.BlockSpec((w, D), lambda i: (i, 0))],
            core_axis_name=("core", "subcore"),
            dimension_semantics=(pltpu.PARALLEL,),
        )(dest_hbm, w_hbm, o_hbm)

    return run(ys, dest_tm, w_lanes)


def _expert_body(first_ref, cnt_ref, nused_ref, xs_hbm, *refs):
    wg_refs = refs[:EXPERT_W_PARTS]
    wu_refs = refs[EXPERT_W_PARTS:2 * EXPERT_W_PARTS]
    wd_refs = refs[2 * EXPERT_W_PARTS:3 * EXPERT_W_PARTS]
    ys_hbm, wg_s, wu_s, wd_s, xbuf, ybuf, xsem, ysem = refs[3 * EXPERT_W_PARTS:]
    _expert_steps(first_ref, cnt_ref, nused_ref, xs_hbm, wg_refs, wu_refs, wd_refs, ys_hbm,
                  wg_s, wu_s, wd_s, xbuf, ybuf, xsem, ysem)


def _expert_steps(first_ref, cnt_ref, nused_ref, xs_hbm, wg_refs, wu_refs, wd_refs, ys_hbm,
                  wg_s, wu_s, wd_s, xbuf, ybuf, xsem, ysem):
    e = pl.program_id(0)
    n_used = nused_ref[0]

    def load(g):
        rows = pl.ds(pl.multiple_of(g * EXPERT_TILE, EXPERT_TILE), EXPERT_TILE)
        slot = g % EXPERT_X_SLOTS
        return pltpu.make_async_copy(xs_hbm.at[rows], xbuf.at[slot], xsem.at[slot])

    def store(g):
        rows = pl.ds(pl.multiple_of(g * EXPERT_TILE, EXPERT_TILE), EXPERT_TILE)
        slot = g % EXPERT_Y_SLOTS
        return pltpu.make_async_copy(ybuf.at[slot], ys_hbm.at[rows], ysem.at[slot])

    @pl.when(e == 0)
    def _():
        for g in range(EXPERT_AHEAD):
            @pl.when(g < n_used)
            def _():
                load(g).start()

    part = D // EXPERT_W_PARTS
    for r in range(EXPERT_W_PARTS):
        wg_s[r * part:(r + 1) * part, :] = wg_refs[r][...].astype(BF16)
        wu_s[r * part:(r + 1) * part, :] = wu_refs[r][...].astype(BF16)
        wd_s[:, r * part:(r + 1) * part] = wd_refs[r][...].astype(BF16)
    first = first_ref[e]
    cnt = cnt_ref[e]

    def acquire(g):
        ahead = g + EXPERT_AHEAD

        @pl.when(ahead < n_used)
        def _():
            load(ahead).start()

        load(g).wait()

        @pl.when(g >= EXPERT_Y_SLOTS)
        def _():
            store(g - EXPERT_Y_SLOTS).wait()

    def compute(g):
        lo, hi = _unpack_rows(xbuf[g % EXPERT_X_SLOTS])
        lo = lo.astype(BF16)
        hi = hi.astype(BF16)

        def xdot(w_s):
            return (jnp.dot(lo, w_s[:DP, :], preferred_element_type=F32) +
                    jnp.dot(hi, w_s[DP:, :], preferred_element_type=F32))

        a = (_silu(xdot(wg_s)) * xdot(wu_s)).astype(BF16)
        ybuf[g % EXPERT_Y_SLOTS] = _pack_rows(jnp.dot(a, wd_s[...], preferred_element_type=F32))

    def pair(j, carry):
        g = first + 2 * j
        acquire(g)
        acquire(g + 1)
        compute(g)
        compute(g + 1)
        store(g).start()
        store(g + 1).start()
        return carry

    lax.fori_loop(0, cnt // 2, pair, 0)

    @pl.when(cnt % 2 == 1)
    def _():
        g = first + cnt - 1
        acquire(g)
        compute(g)
        store(g).start()

    @pl.when(e == N_EXPERTS - 1)
    def _():
        for k in range(EXPERT_Y_SLOTS):
            g = n_used - 1 - k

            @pl.when(g >= 0)
            def _():
                store(g).wait()


def _experts(xs, tile_first, tile_count, n_used, layer, wg, wu, wd):
    part = D // EXPERT_W_PARTS
    in_rows = [pl.BlockSpec((None, None, part, EXPERT_DIM),
                            functools.partial(lambda e, first, cnt, nu, r: (layer, e, r, 0), r=r))
               for r in range(EXPERT_W_PARTS)]
    out_cols = [pl.BlockSpec((None, None, EXPERT_DIM, part),
                             functools.partial(lambda e, first, cnt, nu, c: (layer, e, 0, c), c=c))
                for c in range(EXPERT_W_PARTS)]

    return pl.pallas_call(
        _expert_body,
        grid_spec=pltpu.PrefetchScalarGridSpec(
            num_scalar_prefetch=3, grid=(N_EXPERTS,),
            in_specs=[pl.BlockSpec(memory_space=pl.ANY)] + in_rows + in_rows + out_cols,
            out_specs=pl.BlockSpec(memory_space=pl.ANY),
            scratch_shapes=[pltpu.VMEM((D, EXPERT_DIM), BF16), pltpu.VMEM((D, EXPERT_DIM), BF16),
                            pltpu.VMEM((EXPERT_DIM, D), BF16),
                            pltpu.VMEM((EXPERT_X_SLOTS, EXPERT_TILE, DP), jnp.int32),
                            pltpu.VMEM((EXPERT_Y_SLOTS, EXPERT_TILE, DP), jnp.int32),
                            pltpu.SemaphoreType.DMA((EXPERT_X_SLOTS,)),
                            pltpu.SemaphoreType.DMA((EXPERT_Y_SLOTS,))]),
        out_shape=jax.ShapeDtypeStruct(xs.shape, jnp.int32),
        compiler_params=_cparams("arbitrary"),
        name="experts",
    )(tile_first, tile_count, n_used, xs, *([wg] * EXPERT_W_PARTS), *([wu] * EXPERT_W_PARTS),
      *([wd] * EXPERT_W_PARTS))


def _log_sigmoid(z):
    return jnp.minimum(z, 0.0) - jnp.log(1.0 + jnp.exp(-jnp.abs(z)))


def _gla_gate(hb, wlr_ref, wgk_ref, bgk_ref):
    lr = jnp.dot(hb, wlr_ref[...], preferred_element_type=F32)
    z = _bdot(lr, wgk_ref[...]) + bgk_ref[...]
    return _log_sigmoid(z) * (1.0 / GLA_GATE_NORMALIZER)


def _split3(a):
    hi = a.astype(BF16)
    r1 = a - hi.astype(F32)
    mid = r1.astype(BF16)
    lo = (r1 - mid.astype(F32)).astype(BF16)
    return hi, mid, lo


def _gla_out(o_ref_val, go, gng):
    parts = []
    for hd in range(GLA_HEADS):
        cols = slice(hd * GLA_DV, (hd + 1) * GLA_DV)
        parts.append((_rms(o_ref_val[:, cols], gng) * _silu(go[:, cols])).astype(BF16))
    return jnp.concatenate(parts, axis=1)


def _gla_body(x_ref, mod_ref, ng_ref, wqkvg_ref, wlr_ref, wgk_ref, bgk_ref, tril_ref, gng_ref, wout_ref,
              rw_ref, x1_ref, h_ref, lg_ref, st_ref, st_scr, o_scr, qd_scr, dst_scr, *, tt):
    j = pl.program_id(1)

    @pl.when(j == 0)
    def _():
        st_scr[...] = jnp.zeros_like(st_scr)

    sh1, sc1, g1, sh2, sc2, _ = _mod_slices(mod_ref)
    ng = ng_ref[...]
    x = x_ref[...]
    hb = (_rms(x, ng[0:1]) * (1.0 + sc1) + sh1).astype(BF16)
    proj = jnp.dot(hb, wqkvg_ref[...], preferred_element_type=F32)
    q = proj[:, :GLA_DK_TOT] * (GLA_DK ** -0.5)
    k = proj[:, GLA_DK_TOT:2 * GLA_DK_TOT]
    v = proj[:, 2 * GLA_DK_TOT:2 * GLA_DK_TOT + GLA_DV_TOT].astype(BF16)
    go = proj[:, 2 * GLA_DK_TOT + GLA_DV_TOT:]
    log_a = _gla_gate(hb, wlr_ref, wgk_ref, bgk_ref)
    tril = tril_ref[...]
    b = sum(jnp.dot(tril, part, preferred_element_type=F32) for part in _split3(log_a))
    row = lax.broadcasted_iota(jnp.int32, (GLA_CHUNK, GLA_CHUNK), 0)
    col = lax.broadcasted_iota(jnp.int32, (GLA_CHUNK, GLA_CHUNK), 1)
    causal = row >= col
    n_chunks = tt // GLA_CHUNK
    for c in range(n_chunks):
        rows = slice(c * GLA_CHUNK, (c + 1) * GLA_CHUNK)
        last = (c + 1) * GLA_CHUNK - 1
        for hd in range(GLA_HEADS):
            kc = slice(hd * GLA_DK, (hd + 1) * GLA_DK)
            vc = slice(hd * GLA_DV, (hd + 1) * GLA_DV)
            bb = b[rows, kc]
            b_last = b[last:last + 1, kc]
            q_dec = (q[rows, kc] * jnp.exp(bb)).astype(BF16)
            k_inv = (k[rows, kc] * jnp.exp(-bb)).astype(BF16)
            k_end = (k[rows, kc] * jnp.exp(b_last - bb)).astype(BF16)
            att = jnp.where(causal, _dot_nt(q_dec, k_inv), 0.0).astype(BF16)
            qd_scr[rows, kc] = q_dec
            o_scr[rows, vc] = jnp.dot(att, v[rows, vc], preferred_element_type=F32)
            dst_scr[c * GLA_HEADS + hd] = lax.dot_general(
                v[rows, vc], k_end, (((0,), (0,)), ((), ())), preferred_element_type=F32)
    states = [st_scr[hd] for hd in range(GLA_HEADS)]
    for c in range(n_chunks):
        rows = slice(c * GLA_CHUNK, (c + 1) * GLA_CHUNK)
        last = (c + 1) * GLA_CHUNK - 1
        for hd in range(GLA_HEADS):
            kc = slice(hd * GLA_DK, (hd + 1) * GLA_DK)
            vc = slice(hd * GLA_DV, (hd + 1) * GLA_DV)
            o_scr[rows, vc] += _dot_nt(qd_scr[rows, kc], states[hd].astype(BF16))
            states[hd] = states[hd] * jnp.exp(b[last:last + 1, kc]) + dst_scr[c * GLA_HEADS + hd]
    for hd in range(GLA_HEADS):
        st_scr[hd] = states[hd]

    @pl.when(j == pl.num_programs(1) - 1)
    def _():
        for hd in range(GLA_HEADS):
            st_ref[hd] = st_scr[hd].T

    y = jnp.dot(_gla_out(o_scr[...], go, gng_ref[...]), wout_ref[...], preferred_element_type=F32)
    x1 = x + g1 * _rms(y, ng[1:2])
    x1_ref[...] = x1
    _ffn_prep(x1, ng, sh2, sc2, rw_ref, h_ref, lg_ref)


def _gla_mixer(x2d, mod3, batch, seq, ng, wqkvg, wlr, wgk, bgk, gng, wout, rw_t):
    tt = MIX_TILE
    tpb = seq // tt
    n = x2d.shape[0]
    idx = jnp.arange(tt)
    tril = ((idx[:, None] >= idx[None, :]) &
            (idx[:, None] // GLA_CHUNK == idx[None, :] // GLA_CHUNK)).astype(BF16)
    row_map = lambda b, j: (b * tpb + j, 0)
    consts = (ng, wqkvg, wlr, wgk, bgk, tril, gng, wout, rw_t)
    return pl.pallas_call(
        functools.partial(_gla_body, tt=tt),
        grid=(batch, tpb),
        in_specs=[pl.BlockSpec((tt, D), row_map),
                  pl.BlockSpec((None, 1, 6 * D), lambda b, j: (b, 0, 0))] +
                 [_const_spec(a.shape) for a in consts],
        out_specs=[pl.BlockSpec((tt, D), row_map), pl.BlockSpec((tt, DP), row_map),
                   pl.BlockSpec((N_EXPERTS, tt), lambda b, j: (0, b * tpb + j)),
                   pl.BlockSpec((None, GLA_HEADS, GLA_DK, GLA_DV), lambda b, j: (b, 0, 0, 0))],
        out_shape=[jax.ShapeDtypeStruct((n, D), F32), jax.ShapeDtypeStruct((n, DP), jnp.int32),
                   jax.ShapeDtypeStruct((N_EXPERTS, n), F32),
                   jax.ShapeDtypeStruct((batch, GLA_HEADS, GLA_DK, GLA_DV), F32)],
        scratch_shapes=[pltpu.VMEM((GLA_HEADS, GLA_DV, GLA_DK), F32),
                        pltpu.VMEM((tt, GLA_DV_TOT), F32),
                        pltpu.VMEM((tt, GLA_DK_TOT), BF16),
                        pltpu.VMEM((tt // GLA_CHUNK * GLA_HEADS, GLA_DV, GLA_DK), F32)],
        compiler_params=_cparams("parallel", "arbitrary"),
        name="gla_mixer",
    )(x2d, mod3, *consts)


def _gla1_proj_body(x_ref, mod_ref, ng_ref, wqkvg_ref, wlr_ref, wgk_ref, bgk_ref,
                    q_ref, k_ref, v_ref, go_ref, dec_ref):
    sh1, sc1, _, _, _, _ = _mod_slices(mod_ref)
    ng = ng_ref[...]
    hb = (_rms(x_ref[...], ng[0:1]) * (1.0 + sc1) + sh1).astype(BF16)
    proj = jnp.dot(hb, wqkvg_ref[...], preferred_element_type=F32)
    q_ref[...] = proj[:, :GLA_DK_TOT] * (GLA_DK ** -0.5)
    k_ref[...] = proj[:, GLA_DK_TOT:2 * GLA_DK_TOT]
    v_ref[...] = proj[:, 2 * GLA_DK_TOT:2 * GLA_DK_TOT + GLA_DV_TOT]
    go_ref[...] = proj[:, 2 * GLA_DK_TOT + GLA_DV_TOT:]
    dec_ref[...] = jnp.exp(_gla_gate(hb, wlr_ref, wgk_ref, bgk_ref))


GLA1_TOK = 8


def _gla1_state_body(st_ref, qc_ref, kc_ref, dc_ref, v_ref, nst_ref, o_ref):
    v = v_ref[...]
    for i in range(GLA1_TOK):
        for hd in range(GLA_HEADS):
            vrow = v[i:i + 1, hd * GLA_DV:(hd + 1) * GLA_DV]
            s_new = dc_ref[hd][:, i:i + 1] * st_ref[i, hd] + kc_ref[hd][:, i:i + 1] * vrow
            nst_ref[i, hd] = s_new
            o_ref[i:i + 1, hd * GLA_DV:(hd + 1) * GLA_DV] = jnp.sum(
                qc_ref[hd][:, i:i + 1] * s_new, axis=0, keepdims=True)


def _gla1_out_body(x_ref, o_ref, go_ref, mod_ref, ng_ref, gng_ref, wout_ref, rw_ref, x1_ref, h_ref, lg_ref):
    _, _, g1, sh2, sc2, _ = _mod_slices(mod_ref)
    ng = ng_ref[...]
    y = jnp.dot(_gla_out(o_ref[...], go_ref[...], gng_ref[...]), wout_ref[...], preferred_element_type=F32)
    x1 = x_ref[...] + g1 * _rms(y, ng[1:2])
    x1_ref[...] = x1
    _ffn_prep(x1, ng, sh2, sc2, rw_ref, h_ref, lg_ref)


def _gla_mixer_one(x2d, mod2, state, ng, wqkvg, wlr, wgk, bgk, gng, wout, rw_t):
    n = x2d.shape[0]
    consts = (ng, wqkvg, wlr, wgk, bgk)
    q, k, v, go, dec = pl.pallas_call(
        _gla1_proj_body,
        in_specs=[_const_spec(a.shape) for a in (x2d, mod2) + consts],
        out_specs=[_const_spec((n, GLA_DK_TOT)), _const_spec((n, GLA_DK_TOT)), _const_spec((n, GLA_DV_TOT)),
                   _const_spec((n, GLA_DV_TOT)), _const_spec((n, GLA_DK_TOT))],
        out_shape=[jax.ShapeDtypeStruct((n, GLA_DK_TOT), F32), jax.ShapeDtypeStruct((n, GLA_DK_TOT), F32),
                   jax.ShapeDtypeStruct((n, GLA_DV_TOT), F32), jax.ShapeDtypeStruct((n, GLA_DV_TOT), F32),
                   jax.ShapeDtypeStruct((n, GLA_DK_TOT), F32)],
        grid=(1,),
        compiler_params=_cparams("arbitrary"),
        name="gla1_proj",
    )(x2d, mod2, *consts)

    def cols(a):
        return a.reshape(n // GLA1_TOK, GLA1_TOK, GLA_HEADS, GLA_DK).transpose(0, 2, 3, 1)

    col_spec = pl.BlockSpec((None, GLA_HEADS, GLA_DK, GLA1_TOK), lambda i: (i, 0, 0, 0))
    st_spec = pl.BlockSpec((GLA1_TOK, GLA_HEADS, GLA_DK, GLA_DV), lambda i: (i, 0, 0, 0))
    new_state, o = pl.pallas_call(
        _gla1_state_body,
        grid=(n // GLA1_TOK,),
        in_specs=[st_spec, col_spec, col_spec, col_spec, pl.BlockSpec((GLA1_TOK, GLA_DV_TOT), lambda i: (i, 0))],
        out_specs=[st_spec, pl.BlockSpec((GLA1_TOK, GLA_DV_TOT), lambda i: (i, 0))],
        out_shape=[jax.ShapeDtypeStruct(state.shape, F32), jax.ShapeDtypeStruct((n, GLA_DV_TOT), F32)],
        compiler_params=_cparams("parallel"),
        name="gla1_state",
    )(state, cols(q), cols(k), cols(dec), v)

    consts = (mod2, ng, gng, wout, rw_t)
    x1, h, lg = pl.pallas_call(
        _gla1_out_body,
        grid=(1,),
        in_specs=[_const_spec(a.shape) for a in (x2d, o, go) + consts],
        out_specs=[_const_spec((n, D)), _const_spec((n, DP)), _const_spec((N_EXPERTS, n))],
        out_shape=[jax.ShapeDtypeStruct((n, D), F32), jax.ShapeDtypeStruct((n, DP), jnp.int32),
                   jax.ShapeDtypeStruct((N_EXPERTS, n), F32)],
        compiler_params=_cparams("arbitrary"),
        name="gla1_out",
    )(x2d, o, go, *consts)
    return x1, h, lg, new_state


def _moe_routed(h_p, h_s, lg_p, lg_s, router_bias, layer, wg, wu, wd):
    h, lg = h_p, lg_p
    if h_s is not None:
        h = jnp.concatenate([h_p, h_s], axis=0)
        lg = jnp.concatenate([lg_p, lg_s], axis=1)
    n = h.shape[0]
    n_pad = -(-n // TOKEN_PAD) * TOKEN_PAD
    if n_pad != n:
        h = jnp.pad(h, ((0, n_pad - n), (0, 0)))
        lg = jnp.pad(lg, ((0, 0), (0, n_pad - n)))
    eid, rank, wts, counts = _router(lg, router_bias, n)
    tile_count = ((counts[:, 0] + EXPERT_TILE - 1) // EXPERT_TILE).astype(jnp.int32)
    tile_end = jnp.cumsum(tile_count).astype(jnp.int32)
    tile_first = tile_end - tile_count
    off = tile_first * EXPERT_TILE
    p_alloc = TOP_K * n_pad + N_EXPERTS * EXPERT_TILE
    dest = _dest(off, eid, rank, n, p_alloc - 1)
    dest_w = dest.reshape(TOP_K, n_pad // DISPATCH_W, DISPATCH_W).transpose(1, 0, 2)
    xs = _sc_dispatch(h, dest_w, p_alloc)
    ys = _experts(xs, tile_first, tile_count, tile_end[-1:], layer, wg, wu, wd)
    dest_tm = dest.T.reshape(n_pad // SUM_W, SUM_PARTS, SUM_W * TOP_K // SUM_PARTS)
    w_lanes = jnp.repeat(wts.T, SC_LANES, axis=1)
    return _sc_gather_sum(ys, dest_tm, w_lanes), h


def kernel(x_prompt, x_sample, state_gla, c_prompt, c_sample, norm_g, ada_w, ada_b, gm_w_in, gm_b_in,
           gm_ln_g, gm_ln_b, gm_w_s, gm_b_s, gm_w_out, gla_w_in, gla_w_gk, gla_b_gk, gla_norm_g,
           gla_w_out, router_w, router_bias, exp_w_gate, exp_w_up, exp_w_down, sh_w_gate, sh_w_up,
           sh_w_down):
    batch, seq, _ = x_prompt.shape
    n_s = x_sample.shape[0]
    n_p = batch * seq
    tpb = seq // MIX_TILE
    xp = x_prompt.reshape(n_p, D)
    xs = x_sample.reshape(n_s, D)

    mod = _ada(jnp.concatenate([c_prompt, c_sample], axis=0), ada_w, ada_b)
    mod_p = [mod[i, :batch].reshape(batch, 1, 6 * D) for i in range(2)]
    mod_s = [mod[i, batch:] for i in range(2)]
    rw_t = [jnp.concatenate(_split3(router_w[i].T), axis=0) for i in range(2)]

    ws_causal = jnp.tril(gm_w_s[0]).astype(BF16)
    bs_cols = gm_b_s[0].T
    eye = jnp.eye(GM_CHUNK, dtype=F32)
    ws_first = (gm_w_s[0][:, 0, 0][:, None, None] * eye).astype(BF16)
    bs_first = jnp.broadcast_to(gm_b_s[0][:, 0][None, :], (GM_CHUNK, GM_GROUPS))
    gm_args = (norm_g[0], gm_w_in[0].astype(BF16), gm_b_in[0].reshape(1, -1), gm_ln_g[0].reshape(1, -1),
               gm_ln_b[0].reshape(1, -1))
    wout0 = gm_w_out[0].astype(BF16)
    shared = [(sh_w_gate[i].astype(BF16), sh_w_up[i].astype(BF16), sh_w_down[i].astype(BF16))
              for i in range(2)]
    n_qkvg = 2 * GLA_DK_TOT + 2 * GLA_DV_TOT
    wqkvg = gla_w_in[0][:, :n_qkvg].astype(BF16)
    wlr = jnp.pad(gla_w_in[0][:, n_qkvg:], ((0, 0), (0, LANES - GLA_GATE_RANK))).astype(BF16)
    wgk = jnp.pad(gla_w_gk[0], ((0, LANES - GLA_GATE_RANK), (0, 0))).astype(BF16)
    gla_args = (norm_g[1], wqkvg, wlr, wgk, gla_b_gk[0].reshape(1, -1), gla_norm_g[0].reshape(1, -1),
                gla_w_out[0].astype(BF16), rw_t[1])
    experts = (exp_w_gate, exp_w_up, exp_w_down)

    half = batch // 2
    streams = [(0, half, False), (half, batch - half, True)]
    st = [dict() for _ in streams]

    for s, (b0, nb, with_new) in zip(st, streams):
        s["mod_p"] = [mod_p[i][b0:b0 + nb] for i in range(2)]
        s["n"] = nb * seq
        s["x1p"], s["hp"], s["lgp"] = _gmlp_mixer(xp, b0 * tpb, s["n"], s["mod_p"][0], False, MIX_TILE, tpb,
                                                  *gm_args, ws_causal, bs_cols, wout0, rw_t[0], emit_v=False)
        s["hs"] = s["lgs"] = None
        if with_new:
            s["x1s"], s["hs"], s["lgs"], v_rows = _gmlp_mixer(xs, 0, n_s, mod_s[0], True, n_s, 1, *gm_args,
                                                              ws_first, bs_first, wout0, rw_t[0], emit_v=True)
    for s, (b0, nb, with_new) in zip(st, streams):
        moe = _moe_routed(s["hp"], s["hs"], s["lgp"], s["lgs"], router_bias[0], 0, *experts)
        s["x2p"] = _combine(s["x1p"], *moe, 0, s["mod_p"][0], False, MIX_TILE, tpb, norm_g[0], *shared[0])
        if with_new:
            s["x2s"] = _combine(s["x1s"], *moe, s["n"] // n_s, mod_s[0], True, n_s, 1, norm_g[0], *shared[0])
    for s, (b0, nb, with_new) in zip(st, streams):
        s["x3p"], s["hp"], s["lgp"], s["st_p"] = _gla_mixer(s["x2p"], s["mod_p"][1], nb, seq, *gla_args)
        if with_new:
            s["x3s"], s["hs"], s["lgs"], st_s = _gla_mixer_one(s["x2s"], mod_s[1], state_gla[:, 0], *gla_args)
    y_prompt = None
    for s, (b0, nb, with_new) in zip(st, streams):
        moe = _moe_routed(s["hp"], s["hs"], s["lgp"], s["lgs"], router_bias[1], 1, *experts)
        y_prompt = _combine(s["x3p"], *moe, 0, s["mod_p"][1], False, MIX_TILE, tpb, norm_g[1], *shared[1],
                            out_rows=n_p, out_blk0=b0 * tpb, out_buf=y_prompt)
        if with_new:
            y_new = _combine(s["x3s"], *moe, s["n"] // n_s, mod_s[1], True, n_s, 1, norm_g[1], *shared[1])
    st_p = jnp.concatenate([s["st_p"] for s in st], axis=0)

    return (y_prompt.reshape(batch, seq, D), y_new.reshape(n_s, 1, D), st_p[:, None], st_s[:, None],
            v_rows.reshape(n_s, 1, 1, GM_HALF))
```

```python
import functools
import math

import jax
import jax.numpy as jnp
from jax import lax
from jax.experimental import pallas as pl
from jax.experimental.pallas import tpu as pltpu
from jax.experimental.pallas import tpu_sc as plsc

F32 = jnp.float32
BF16 = jnp.bfloat16

D = 1024
DP = D // 2
GM_CHUNK = 128
GM_HALF = 2 * D
GM_GROUPS = 8
GM_GROUP_DIM = GM_HALF // GM_GROUPS
GLA_HEADS = 4
GLA_DK = 128
GLA_DV = 256
GLA_DK_TOT = GLA_HEADS * GLA_DK
GLA_DV_TOT = GLA_HEADS * GLA_DV
GLA_GATE_RANK = 16
GLA_GATE_NORMALIZER = 16.0
GLA_CHUNK = 64
N_EXPERTS = 64
TOP_K = 8
N_EXPERT_GROUPS = 8
GROUP_SIZE = N_EXPERTS // N_EXPERT_GROUPS
TOPK_GROUPS = 4
EXPERT_DIM = D // 4
ROUTED_SCALE = 2.5
NORM_EPS = 1e-6
LN_EPS = 1e-5

LANES = 128
VMEM_LIMIT = 56 * 1024 * 1024

MIX_TILE = 256
GM_COL_BLOCK = 512
ROUTER_TILE = 512
EXPERT_TILE = 272
EXPERT_X_SLOTS = 6
EXPERT_AHEAD = EXPERT_X_SLOTS - 2
EXPERT_Y_SLOTS = 4
EXPERT_W_PARTS = 4
SC_WORKERS = 32
DISPATCH_W = 32
SC_LANES = 16
SUM_W = 16
SUM_PARTS = 4
SUM_UNROLL = 4
TOKEN_PAD = SC_WORKERS * DISPATCH_W


def _cparams(*sem):
    return pltpu.CompilerParams(dimension_semantics=sem, vmem_limit_bytes=VMEM_LIMIT)


def _rms(x, g):
    return x * lax.rsqrt(jnp.mean(x * x, axis=-1, keepdims=True) + NORM_EPS) * g


def _silu(x):
    return x * (1.0 / (1.0 + jnp.exp(-x)))


def _gelu(x):
    return 0.5 * x * (1.0 + lax.erf(x * (1.0 / math.sqrt(2.0))))


def _bdot(a, b):
    return jnp.dot(a.astype(BF16), b.astype(BF16), preferred_element_type=F32)


def _dot_nt(a, b, precision=None):
    return lax.dot_general(a, b, (((1,), (1,)), ((), ())), preferred_element_type=F32,
                           precision=precision)


def _mod_slices(mod_ref):
    return [mod_ref[:, i * D:(i + 1) * D] for i in range(6)]


HI_HALF = -65536


def _pack_rows(x):
    lo = lax.bitcast_convert_type(x[:, :DP].astype(BF16).astype(F32), jnp.int32)
    hi = lax.bitcast_convert_type(x[:, DP:].astype(BF16).astype(F32), jnp.int32)
    return lax.shift_right_logical(lo, 16) | (hi & HI_HALF)


def _unpack_rows(p):
    lo = lax.bitcast_convert_type(lax.shift_left(p, 16), F32)
    hi = lax.bitcast_convert_type(p & HI_HALF, F32)
    return lo, hi


def _ffn_prep(x1, ng, sh2, sc2, rw_ref, h_ref, lg_ref):
    hffn = _rms(x1, ng[2:3]) * (1.0 + sc2) + sh2
    h_ref[...] = _pack_rows(hffn)
    lg3 = _dot_nt(rw_ref[...], hffn.astype(BF16))
    lg_ref[...] = lg3[:N_EXPERTS] + lg3[N_EXPERTS:2 * N_EXPERTS] + lg3[2 * N_EXPERTS:]


def _ada_body(c_ref, w_ref, b_ref, o_ref):
    c = c_ref[...]
    o_ref[...] = _bdot(_silu(c), w_ref[...]) + b_ref[...]


def _ada(c, ada_w, ada_b):
    n = c.shape[0]
    depth = ada_w.shape[0]
    tn = 1536
    return pl.pallas_call(
        _ada_body,
        grid=(depth, 6 * D // tn),
        in_specs=[pl.BlockSpec((n, D), lambda l, j: (0, 0)),
                  pl.BlockSpec((None, D, tn), lambda l, j: (l, 0, j)),
                  pl.BlockSpec((None, 1, tn), lambda l, j: (l, 0, j))],
        out_specs=pl.BlockSpec((None, n, tn), lambda l, j: (l, 0, j)),
        out_shape=jax.ShapeDtypeStruct((depth, n, 6 * D), F32),
        compiler_params=_cparams("parallel", "parallel"),
        name="ada_mod",
    )(c, ada_w, ada_b.reshape(depth, 1, 6 * D))


def _mod_spec(per_row, tt, tiles_per_batch):
    if per_row:
        return pl.BlockSpec((tt, 6 * D), lambda i: (i, 0))
    return pl.BlockSpec((None, 1, 6 * D), lambda i: (i // tiles_per_batch, 0, 0))


def _const_spec(shape):
    zeros = (0,) * len(shape)
    return pl.BlockSpec(shape, lambda *_: zeros)


def _gmlp_body(x_ref, mod_ref, ng_ref, win_ref, bin_ref, lng_ref, lnb_ref, ws_ref, bs_ref, wout_ref,
               rw_ref, x1_ref, h_ref, lg_ref, *rest, n_chunks, emit_v):
    if emit_v:
        v_ref, um_ref, z_ref = rest
    else:
        um_ref, z_ref = rest
    sh1, sc1, g1, sh2, sc2, _ = _mod_slices(mod_ref)
    ng = ng_ref[...]
    x = x_ref[...]
    hb = (_rms(x, ng[0:1]) * (1.0 + sc1) + sh1).astype(BF16)
    for cb in range(2 * GM_HALF // GM_COL_BLOCK):
        cols = slice(cb * GM_COL_BLOCK, (cb + 1) * GM_COL_BLOCK)
        z_ref[:, cols] = _gelu(jnp.dot(hb, win_ref[:, cols], preferred_element_type=F32) + bin_ref[:, cols])
    u = z_ref[:, :GM_HALF]
    v = z_ref[:, GM_HALF:]
    mu = jnp.mean(v, axis=-1, keepdims=True)
    vc = v - mu
    var = jnp.mean(vc * vc, axis=-1, keepdims=True)
    v = vc * lax.rsqrt(var + LN_EPS) * lng_ref[...] + lnb_ref[...]
    if emit_v:
        v_ref[...] = v
    vb = v.astype(BF16)
    for c in range(n_chunks):
        rows = slice(c * GM_CHUNK, (c + 1) * GM_CHUNK)
        for g in range(GM_GROUPS):
            cols = slice(g * GM_GROUP_DIM, (g + 1) * GM_GROUP_DIM)
            mixed = jnp.dot(ws_ref[g], vb[rows, cols], preferred_element_type=F32) + bs_ref[:, g:g + 1]
            um_ref[rows, cols] = (u[rows, cols] * mixed).astype(BF16)
    y = jnp.dot(um_ref[...], wout_ref[...], preferred_element_type=F32)
    x1 = x + g1 * _rms(y, ng[1:2])
    x1_ref[...] = x1
    _ffn_prep(x1, ng, sh2, sc2, rw_ref, h_ref, lg_ref)


def _gmlp_mixer(x2d, blk0, n, mod, per_row, tt, tiles_per_batch, ng, win, b_in, ln_g, ln_b, ws, bs, wout,
                rw_t, emit_v):
    out_shape = [jax.ShapeDtypeStruct((n, D), F32), jax.ShapeDtypeStruct((n, DP), jnp.int32),
                 jax.ShapeDtypeStruct((N_EXPERTS, n), F32)]
    out_specs = [pl.BlockSpec((tt, D), lambda i: (i, 0)), pl.BlockSpec((tt, DP), lambda i: (i, 0)),
                 pl.BlockSpec((N_EXPERTS, tt), lambda i: (0, i))]
    if emit_v:
        out_shape.append(jax.ShapeDtypeStruct((n, GM_HALF), F32))
        out_specs.append(pl.BlockSpec((tt, GM_HALF), lambda i: (i, 0)))
    return pl.pallas_call(
        functools.partial(_gmlp_body, n_chunks=tt // GM_CHUNK, emit_v=emit_v),
        grid=(n // tt,),
        in_specs=[pl.BlockSpec((tt, D), lambda i: (i + blk0, 0)),
                  _mod_spec(per_row, tt, tiles_per_batch),
                  _const_spec(ng.shape), _const_spec(win.shape), _const_spec(b_in.shape),
                  _const_spec(ln_g.shape), _const_spec(ln_b.shape), _const_spec(ws.shape),
                  _const_spec(bs.shape), _const_spec(wout.shape), _const_spec(rw_t.shape)],
        out_specs=out_specs,
        out_shape=out_shape,
        scratch_shapes=[pltpu.VMEM((tt, GM_HALF), BF16), pltpu.VMEM((tt, 2 * GM_HALF), F32)],
        compiler_params=_cparams("parallel"),
        name="gmlp_mixer_rows" if per_row else "gmlp_mixer",
    )(x2d, mod, ng, win, b_in, ln_g, ln_b, ws, bs, wout, rw_t)


def _combine_body(x_ref, y_ref, h_ref, mod_ref, ng_ref, swg_ref, swu_ref, swd_ref, *rest):
    o_ref = rest[-1]
    o_ref[...] = _channel_mix_residual(x_ref[...], y_ref[...], h_ref, mod_ref[:, 5 * D:6 * D],
                                       ng_ref[3:4, :], swg_ref, swu_ref, swd_ref)


def _combine(x2d, routed, hp, blk0, mod, per_row, tt, tiles_per_batch, ng, swg, swu, swd,
             out_rows=None, out_blk0=0, out_buf=None):
    n = x2d.shape[0]
    in_specs = [pl.BlockSpec((tt, D), lambda i: (i, 0)),
                pl.BlockSpec((tt, D), lambda i: (i + blk0, 0)),
                pl.BlockSpec((tt, DP), lambda i: (i + blk0, 0)),
                _mod_spec(per_row, tt, tiles_per_batch),
                _const_spec(ng.shape), _const_spec(swg.shape), _const_spec(swu.shape),
                _const_spec(swd.shape)]
    args = [x2d, routed, hp, mod, ng, swg, swu, swd]
    aliases = {}
    if out_buf is not None:
        in_specs.append(pl.BlockSpec(memory_space=pl.ANY))
        aliases = {len(args): 0}
        args.append(out_buf)
    return pl.pallas_call(
        _combine_body,
        grid=(n // tt,),
        in_specs=in_specs,
        out_specs=pl.BlockSpec((tt, D), lambda i: (i + out_blk0, 0)),
        out_shape=jax.ShapeDtypeStruct((out_rows or n, D), F32),
        input_output_aliases=aliases,
        compiler_params=_cparams("parallel"),
        name="combine_rows" if per_row else "combine",
    )(*args)


def _router_body(lg_ref, bias_ref, tri_ref, eid_ref, rank_ref, wts_ref, cnt_ref, carry_ref, *, n_real):
    step = pl.program_id(0)

    @pl.when(step == 0)
    def _():
        carry_ref[...] = jnp.zeros_like(carry_ref)

    lg = lg_ref[...]
    tn = lg.shape[1]
    real = (step * tn + lax.broadcasted_iota(jnp.int32, (1, tn), 1)) < n_real
    lg = jnp.where(real, lg, 0.0)
    scores = 1.0 / (1.0 + jnp.exp(-lg))
    sel = scores + bias_ref[...]
    neg = -jnp.inf
    sub8 = lax.broadcasted_iota(jnp.int32, (GROUP_SIZE, tn), 0)
    gsub = lax.broadcasted_iota(jnp.int32, (N_EXPERT_GROUPS, tn), 0)
    gs = jnp.zeros((N_EXPERT_GROUPS, tn), F32)
    for g in range(N_EXPERT_GROUPS):
        blk = sel[g * GROUP_SIZE:(g + 1) * GROUP_SIZE, :]
        m1 = jnp.max(blk, axis=0, keepdims=True)
        i1 = jnp.min(jnp.where(blk == m1, sub8, GROUP_SIZE), axis=0, keepdims=True)
        m2 = jnp.max(jnp.where(sub8 == i1, neg, blk), axis=0, keepdims=True)
        gs = jnp.where(gsub == g, m1 + m2, gs)
    gmask = jnp.zeros((N_EXPERT_GROUPS, tn), jnp.bool_)
    for _ in range(TOPK_GROUPS):
        m = jnp.max(gs, axis=0, keepdims=True)
        i = jnp.min(jnp.where(gs == m, gsub, N_EXPERT_GROUPS), axis=0, keepdims=True)
        hit = gsub == i
        gmask = jnp.logical_or(gmask, hit)
        gs = jnp.where(hit, neg, gs)
    gmaskf = gmask.astype(F32)
    blocks = []
    for g in range(N_EXPERT_GROUPS):
        keep = jnp.broadcast_to(gmaskf[g:g + 1, :], (GROUP_SIZE, tn)) > 0.5
        blocks.append(jnp.where(keep, sel[g * GROUP_SIZE:(g + 1) * GROUP_SIZE, :], neg))
    msel = jnp.concatenate(blocks, axis=0)
    esub = lax.broadcasted_iota(jnp.int32, (N_EXPERTS, tn), 0)
    chosen = jnp.zeros((N_EXPERTS, tn), jnp.bool_)
    picks = []
    for _ in range(TOP_K):
        m = jnp.max(msel, axis=0, keepdims=True)
        i = jnp.min(jnp.where(msel == m, esub, N_EXPERTS), axis=0, keepdims=True)
        hit = esub == i
        picks.append(i)
        chosen = jnp.logical_or(chosen, hit)
        msel = jnp.where(hit, neg, msel)
    w = jnp.where(chosen, scores, 0.0)
    w = w / jnp.sum(w, axis=0, keepdims=True) * ROUTED_SCALE
    counted = jnp.where(jnp.logical_and(chosen, real), 1.0, 0.0)
    incl = jnp.dot(counted.astype(BF16), tri_ref[...], preferred_element_type=F32)
    rank_full = carry_ref[:, 0:1] + incl - 1.0
    ksub = lax.broadcasted_iota(jnp.int32, (TOP_K, tn), 0)
    eid = jnp.zeros((TOP_K, tn), jnp.int32)
    rank = jnp.zeros((TOP_K, tn), F32)
    wts = jnp.zeros((TOP_K, tn), F32)
    for k in range(TOP_K):
        hit = esub == picks[k]
        eid = jnp.where(ksub == k, picks[k], eid)
        rank = jnp.where(ksub == k, jnp.sum(jnp.where(hit, rank_full, 0.0), axis=0, keepdims=True), rank)
        wts = jnp.where(ksub == k, jnp.sum(jnp.where(hit, w, 0.0), axis=0, keepdims=True), wts)
    eid_ref[...] = eid
    rank_ref[...] = rank.astype(jnp.int32)
    wts_ref[...] = wts
    carry = carry_ref[...] + incl[:, tn - 1:tn]
    carry_ref[...] = carry
    cnt_ref[...] = carry.astype(jnp.int32)


def _router(lg_t, bias, n_real):
    n = lg_t.shape[1]
    tn = ROUTER_TILE
    idx = jnp.arange(tn)
    tri = (idx[:, None] <= idx[None, :]).astype(BF16)
    kspec = pl.BlockSpec((TOP_K, tn), lambda i: (0, i))
    return pl.pallas_call(
        functools.partial(_router_body, n_real=n_real),
        grid=(n // tn,),
        in_specs=[pl.BlockSpec((N_EXPERTS, tn), lambda i: (0, i)), _const_spec((N_EXPERTS, 1)),
                  _const_spec((tn, tn))],
        out_specs=[kspec, kspec, kspec, _const_spec((N_EXPERTS, LANES))],
        out_shape=[jax.ShapeDtypeStruct((TOP_K, n), jnp.int32), jax.ShapeDtypeStruct((TOP_K, n), jnp.int32),
                   jax.ShapeDtypeStruct((TOP_K, n), F32), jax.ShapeDtypeStruct((N_EXPERTS, LANES), jnp.int32)],
        scratch_shapes=[pltpu.VMEM((N_EXPERTS, LANES), F32)],
        compiler_params=_cparams("arbitrary"),
        name="router",
    )(lg_t, bias.reshape(N_EXPERTS, 1), tri)


def _dest_body(off_ref, eid_ref, rank_ref, dest_ref, *, n_real, last_row):
    eid = eid_ref[...]
    base = jnp.zeros(eid.shape, jnp.int32)
    for e in range(N_EXPERTS):
        base = jnp.where(eid == e, off_ref[e], base)
    tok = lax.broadcasted_iota(jnp.int32, eid.shape, 1)
    slot = lax.broadcasted_iota(jnp.int32, eid.shape, 0)
    unused = last_row - ((tok - n_real) * TOP_K + slot)
    dest_ref[...] = jnp.where(tok < n_real, base + rank_ref[...], unused)


def _dest(off, eid, rank, n_real, last_row):
    spec = pl.BlockSpec(eid.shape, lambda i, off_ref: (0, 0))
    return pl.pallas_call(
        functools.partial(_dest_body, n_real=n_real, last_row=last_row),
        grid_spec=pltpu.PrefetchScalarGridSpec(num_scalar_prefetch=1, grid=(1,), in_specs=[spec, spec],
                                               out_specs=spec),
        out_shape=jax.ShapeDtypeStruct(eid.shape, jnp.int32),
        compiler_params=_cparams("arbitrary"),
        name="dest_rows",
    )(off, eid, rank)


def _sc_mesh():
    return plsc.VectorSubcoreMesh(core_axis_name="core", subcore_axis_name="subcore")


def _sc_dispatch(hp, dest_w, p_alloc):
    n = hp.shape[0]
    w = dest_w.shape[2]

    @functools.partial(pl.kernel, out_type=jax.ShapeDtypeStruct((p_alloc, DP), jnp.int32), mesh=_sc_mesh(),
                       name="sc_dispatch")
    def run(hp_hbm, dest_hbm, xs_hbm):
        def body(x_vmem, i_vmem):
            for k in range(TOP_K):
                pltpu.sync_copy(x_vmem, xs_hbm.at[i_vmem.at[k]])

        pltpu.emit_pipeline(
            body,
            grid=(n // w,),
            in_specs=[pl.BlockSpec((w, DP), lambda i: (i, 0)),
                      pl.BlockSpec((None, TOP_K, w), lambda i: (i, 0, 0))],
            out_specs=[],
            core_axis_name=("core", "subcore"),
            dimension_semantics=(pltpu.PARALLEL,),
        )(hp_hbm, dest_hbm)

    return run(hp, dest_w)


def _sc_gather_sum(ys, dest_tm, w_lanes):
    n_win, parts, pk = dest_tm.shape
    w = parts * pk // TOP_K
    wp = w // parts
    n_vec = DP // SC_LANES

    @functools.partial(pl.kernel, out_type=jax.ShapeDtypeStruct((n_win * w, D), F32), mesh=_sc_mesh(),
                       scratch_types=[pltpu.VMEM((parts, pk, DP), jnp.int32), pltpu.SemaphoreType.DMA((parts,))],
                       compiler_params=pltpu.CompilerParams(needs_layout_passes=False), name="sc_gather_sum")
    def run(ys_hbm, dest_hbm, w_hbm, o_hbm, rows_v, sems):
        def body(i_vmem, w_vmem, o_vmem):
            copies = [pltpu.async_copy(ys_hbm.at[i_vmem.at[p]], rows_v.at[p], sems.at[p]) for p in range(parts)]
            for p in range(parts):
                copies[p].wait()

                @pl.loop(0, wp)
                def _(t):
                    tok = p * wp + t
                    wv = [w_vmem[tok, pl.ds(k * SC_LANES, SC_LANES)] for k in range(TOP_K)]

                    @pl.loop(0, n_vec, step=SUM_UNROLL)
                    def _(j0):
                        for u in range(SUM_UNROLL):
                            col = (j0 + u) * SC_LANES
                            lo = jnp.zeros((SC_LANES,), F32)
                            hi = jnp.zeros((SC_LANES,), F32)
                            for k in range(TOP_K):
                                word = rows_v[p, t * TOP_K + k, pl.ds(col, SC_LANES)]
                                lo = lo + wv[k] * plsc.bitcast(lax.shift_left(word, 16), F32)
                                hi = hi + wv[k] * plsc.bitcast(word & HI_HALF, F32)
                            o_vmem[tok, pl.ds(col, SC_LANES)] = lo
                            o_vmem[tok, pl.ds(DP + col, SC_LANES)] = hi

        pltpu.emit_pipeline(
            body,
            grid=(n_win,),
            in_specs=[pl.BlockSpec((None, parts, pk), lambda i: (i, 0, 0)),
                      pl.BlockSpec((w, TOP_K * SC_LANES), lambda i: (i, 0))],
            out_specs=[pl.BlockSpec((w, D), lambda i: (i, 0))],
            core_axis_name=("core", "subcore"),
            dimension_semantics=(pltpu.PARALLEL,),
        )(dest_hbm, w_hbm, o_hbm)

    return run(ys, dest_tm, w_lanes)


def _expert_body(first_ref, cnt_ref, nused_ref, xs_hbm, *refs):
    wg_refs = refs[:EXPERT_W_PARTS]
    wu_refs = refs[EXPERT_W_PARTS:2 * EXPERT_W_PARTS]
    wd_refs = refs[2 * EXPERT_W_PARTS:3 * EXPERT_W_PARTS]
    ys_hbm, wg_s, wu_s, wd_s, xbuf, ybuf, xsem, ysem = refs[3 * EXPERT_W_PARTS:]
    _expert_steps(first_ref, cnt_ref, nused_ref, xs_hbm, wg_refs, wu_refs, wd_refs, ys_hbm,
                  wg_s, wu_s, wd_s, xbuf, ybuf, xsem, ysem)


def _expert_steps(first_ref, cnt_ref, nused_ref, xs_hbm, wg_refs, wu_refs, wd_refs, ys_hbm,
                  wg_s, wu_s, wd_s, xbuf, ybuf, xsem, ysem):
    e = pl.program_id(0)
    n_used = nused_ref[0]

    def load(g):
        rows = pl.ds(pl.multiple_of(g * EXPERT_TILE, EXPERT_TILE), EXPERT_TILE)
        slot = g % EXPERT_X_SLOTS
        return pltpu.make_async_copy(xs_hbm.at[rows], xbuf.at[slot], xsem.at[slot])

    def store(g):
        rows = pl.ds(pl.multiple_of(g * EXPERT_TILE, EXPERT_TILE), EXPERT_TILE)
        slot = g % EXPERT_Y_SLOTS
        return pltpu.make_async_copy(ybuf.at[slot], ys_hbm.at[rows], ysem.at[slot])

    @pl.when(e == 0)
    def _():
        for g in range(EXPERT_AHEAD):
            @pl.when(g < n_used)
            def _():
                load(g).start()

    part = D // EXPERT_W_PARTS
    for r in range(EXPERT_W_PARTS):
        wg_s[r * part:(r + 1) * part, :] = wg_refs[r][...].astype(BF16)
        wu_s[r * part:(r + 1) * part, :] = wu_refs[r][...].astype(BF16)
        wd_s[:, r * part:(r + 1) * part] = wd_refs[r][...].astype(BF16)
    first = first_ref[e]
    cnt = cnt_ref[e]

    def acquire(g):
        ahead = g + EXPERT_AHEAD

        @pl.when(ahead < n_used)
        def _():
            load(ahead).start()

        load(g).wait()

        @pl.when(g >= EXPERT_Y_SLOTS)
        def _():
            store(g - EXPERT_Y_SLOTS).wait()

    def compute(g):
        lo, hi = _unpack_rows(xbuf[g % EXPERT_X_SLOTS])
        lo = lo.astype(BF16)
        hi = hi.astype(BF16)

        def xdot(w_s):
            return (jnp.dot(lo, w_s[:DP, :], preferred_element_type=F32) +
                    jnp.dot(hi, w_s[DP:, :], preferred_element_type=F32))

        a = (_silu(xdot(wg_s)) * xdot(wu_s)).astype(BF16)
        ybuf[g % EXPERT_Y_SLOTS] = _pack_rows(jnp.dot(a, wd_s[...], preferred_element_type=F32))

    def pair(j, carry):
        g = first + 2 * j
        acquire(g)
        acquire(g + 1)
        compute(g)
        compute(g + 1)
        store(g).start()
        store(g + 1).start()
        return carry

    lax.fori_loop(0, cnt // 2, pair, 0)

    @pl.when(cnt % 2 == 1)
    def _():
        g = first + cnt - 1
        acquire(g)
        compute(g)
        store(g).start()

    @pl.when(e == N_EXPERTS - 1)
    def _():
        for k in range(EXPERT_Y_SLOTS):
            g = n_used - 1 - k

            @pl.when(g >= 0)
            def _():
                store(g).wait()


def _experts(xs, tile_first, tile_count, n_used, layer, wg, wu, wd):
    part = D // EXPERT_W_PARTS
    in_rows = [pl.BlockSpec((None, None, part, EXPERT_DIM),
                            functools.partial(lambda e, first, cnt, nu, r: (layer, e, r, 0), r=r))
               for r in range(EXPERT_W_PARTS)]
    out_cols = [pl.BlockSpec((None, None, EXPERT_DIM, part),
                             functools.partial(lambda e, first, cnt, nu, c: (layer, e, 0, c), c=c))
                for c in range(EXPERT_W_PARTS)]

    return pl.pallas_call(
        _expert_body,
        grid_spec=pltpu.PrefetchScalarGridSpec(
            num_scalar_prefetch=3, grid=(N_EXPERTS,),
            in_specs=[pl.BlockSpec(memory_space=pl.ANY)] + in_rows + in_rows + out_cols,
            out_specs=pl.BlockSpec(memory_space=pl.ANY),
            scratch_shapes=[pltpu.VMEM((D, EXPERT_DIM), BF16), pltpu.VMEM((D, EXPERT_DIM), BF16),
                            pltpu.VMEM((EXPERT_DIM, D), BF16),
                            pltpu.VMEM((EXPERT_X_SLOTS, EXPERT_TILE, DP), jnp.int32),
                            pltpu.VMEM((EXPERT_Y_SLOTS, EXPERT_TILE, DP), jnp.int32),
                            pltpu.SemaphoreType.DMA((EXPERT_X_SLOTS,)),
                            pltpu.SemaphoreType.DMA((EXPERT_Y_SLOTS,))]),
        out_shape=jax.ShapeDtypeStruct(xs.shape, jnp.int32),
        compiler_params=_cparams("arbitrary"),
        name="experts",
    )(tile_first, tile_count, n_used, xs, *([wg] * EXPERT_W_PARTS), *([wu] * EXPERT_W_PARTS),
      *([wd] * EXPERT_W_PARTS))


def _log_sigmoid(z):
    return jnp.minimum(z, 0.0) - jnp.log(1.0 + jnp.exp(-jnp.abs(z)))


def _gla_gate(hb, wlr_ref, wgk_ref, bgk_ref):
    lr = jnp.dot(hb, wlr_ref[...], preferred_element_type=F32)
    z = _bdot(lr, wgk_ref[...]) + bgk_ref[...]
    return _log_sigmoid(z) * (1.0 / GLA_GATE_NORMALIZER)


def _split3(a):
    hi = a.astype(BF16)
    r1 = a - hi.astype(F32)
    mid = r1.astype(BF16)
    lo = (r1 - mid.astype(F32)).astype(BF16)
    return hi, mid, lo


def _gla_out(o_ref_val, go, gng):
    parts = []
    for hd in range(GLA_HEADS):
        cols = slice(hd * GLA_DV, (hd + 1) * GLA_DV)
        parts.append((_rms(o_ref_val[:, cols], gng) * _silu(go[:, cols])).astype(BF16))
    return jnp.concatenate(parts, axis=1)


def _channel_mix_residual(x, routed, h_ref, g2, ng3, swg_ref, swu_ref, swd_ref):
    h_lo, h_hi = _unpack_rows(h_ref[...])
    h_lo = h_lo.astype(BF16)
    h_hi = h_hi.astype(BF16)

    def hdot(w_ref_):
        return (jnp.dot(h_lo, w_ref_[:DP, :], preferred_element_type=F32) +
                jnp.dot(h_hi, w_ref_[DP:, :], preferred_element_type=F32))

    hs = (_silu(hdot(swg_ref)) * hdot(swu_ref)).astype(BF16)
    y = jnp.dot(hs, swd_ref[...], preferred_element_type=F32) + routed
    return x + g2 * _rms(y, ng3)


def _gla_body(x_ref, y_ref, hprev_ref, modprev_ref, ngprev_ref, swg_ref, swu_ref, swd_ref,
              mod_ref, ng_ref, wqkvg_ref, wlr_ref, wgk_ref, bgk_ref, tril_ref, gng_ref, wout_ref,
              rw_ref, x1_ref, h_ref, lg_ref, st_ref, st_scr, o_scr, qd_scr, dst_scr, *, tt):
    j = pl.program_id(1)

    @pl.when(j == 0)
    def _():
        st_scr[...] = jnp.zeros_like(st_scr)

    x = _channel_mix_residual(x_ref[...], y_ref[...], hprev_ref, modprev_ref[:, 5 * D:6 * D],
                              ngprev_ref[3:4, :], swg_ref, swu_ref, swd_ref)
    sh1, sc1, g1, sh2, sc2, _ = _mod_slices(mod_ref)
    ng = ng_ref[...]
    hb = (_rms(x, ng[0:1]) * (1.0 + sc1) + sh1).astype(BF16)
    q = jnp.dot(hb, wqkvg_ref[:, :GLA_DK_TOT], preferred_element_type=F32) * (GLA_DK ** -0.5)
    k = jnp.dot(hb, wqkvg_ref[:, GLA_DK_TOT:2 * GLA_DK_TOT], preferred_element_type=F32)
    v = jnp.dot(hb, wqkvg_ref[:, 2 * GLA_DK_TOT:2 * GLA_DK_TOT + GLA_DV_TOT],
                preferred_element_type=F32).astype(BF16)
    log_a = _gla_gate(hb, wlr_ref, wgk_ref, bgk_ref)
    tril = tril_ref[...]
    b = sum(jnp.dot(tril, part, preferred_element_type=F32) for part in _split3(log_a))
    row = lax.broadcasted_iota(jnp.int32, (GLA_CHUNK, GLA_CHUNK), 0)
    col = lax.broadcasted_iota(jnp.int32, (GLA_CHUNK, GLA_CHUNK), 1)
    causal = row >= col
    n_chunks = tt // GLA_CHUNK
    for c in range(n_chunks):
        rows = slice(c * GLA_CHUNK, (c + 1) * GLA_CHUNK)
        last = (c + 1) * GLA_CHUNK - 1
        for hd in range(GLA_HEADS):
            kc = slice(hd * GLA_DK, (hd + 1) * GLA_DK)
            vc = slice(hd * GLA_DV, (hd + 1) * GLA_DV)
            bb = b[rows, kc]
            b_last = b[last:last + 1, kc]
            q_dec = (q[rows, kc] * jnp.exp(bb)).astype(BF16)
            k_inv = (k[rows, kc] * jnp.exp(-bb)).astype(BF16)
            k_end = (k[rows, kc] * jnp.exp(b_last - bb)).astype(BF16)
            att = jnp.where(causal, _dot_nt(q_dec, k_inv), 0.0).astype(BF16)
            qd_scr[rows, kc] = q_dec
            o_scr[rows, vc] = jnp.dot(att, v[rows, vc], preferred_element_type=F32)
            dst_scr[c * GLA_HEADS + hd] = lax.dot_general(
                v[rows, vc], k_end, (((0,), (0,)), ((), ())), preferred_element_type=F32)
    states = [st_scr[hd] for hd in range(GLA_HEADS)]
    for c in range(n_chunks):
        rows = slice(c * GLA_CHUNK, (c + 1) * GLA_CHUNK)
        last = (c + 1) * GLA_CHUNK - 1
        for hd in range(GLA_HEADS):
            kc = slice(hd * GLA_DK, (hd + 1) * GLA_DK)
            vc = slice(hd * GLA_DV, (hd + 1) * GLA_DV)
            o_scr[rows, vc] += _dot_nt(qd_scr[rows, kc], states[hd].astype(BF16))
            states[hd] = states[hd] * jnp.exp(b[last:last + 1, kc]) + dst_scr[c * GLA_HEADS + hd]
    for hd in range(GLA_HEADS):
        st_scr[hd] = states[hd]

    @pl.when(j == pl.num_programs(1) - 1)
    def _():
        for hd in range(GLA_HEADS):
            st_ref[hd] = st_scr[hd].T

    go = jnp.dot(hb, wqkvg_ref[:, 2 * GLA_DK_TOT + GLA_DV_TOT:], preferred_element_type=F32)
    y = jnp.dot(_gla_out(o_scr[...], go, gng_ref[...]), wout_ref[...], preferred_element_type=F32)
    x1 = x + g1 * _rms(y, ng[1:2])
    x1_ref[...] = x1
    _ffn_prep(x1, ng, sh2, sc2, rw_ref, h_ref, lg_ref)


def _gla_mixer(x2d, routed, hp, mod3_prev, ng_prev, shared_prev, mod3, batch, seq, ng, wqkvg, wlr, wgk, bgk,
               gng, wout, rw_t):
    tt = MIX_TILE
    tpb = seq // tt
    n = x2d.shape[0]
    idx = jnp.arange(tt)
    tril = ((idx[:, None] >= idx[None, :]) &
            (idx[:, None] // GLA_CHUNK == idx[None, :] // GLA_CHUNK)).astype(BF16)
    row_map = lambda b, j: (b * tpb + j, 0)
    mod_map = lambda b, j: (b, 0, 0)
    consts = (ng, wqkvg, wlr, wgk, bgk, tril, gng, wout, rw_t)
    prev_consts = (ng_prev,) + tuple(shared_prev)
    return pl.pallas_call(
        functools.partial(_gla_body, tt=tt),
        grid=(batch, tpb),
        in_specs=[pl.BlockSpec((tt, D), row_map), pl.BlockSpec((tt, D), row_map),
                  pl.BlockSpec((tt, DP), row_map), pl.BlockSpec((None, 1, 6 * D), mod_map)] +
                 [_const_spec(a.shape) for a in prev_consts] +
                 [pl.BlockSpec((None, 1, 6 * D), mod_map)] +
                 [_const_spec(a.shape) for a in consts],
        out_specs=[pl.BlockSpec((tt, D), row_map), pl.BlockSpec((tt, DP), row_map),
                   pl.BlockSpec((N_EXPERTS, tt), lambda b, j: (0, b * tpb + j)),
                   pl.BlockSpec((None, GLA_HEADS, GLA_DK, GLA_DV), lambda b, j: (b, 0, 0, 0))],
        out_shape=[jax.ShapeDtypeStruct((n, D), F32), jax.ShapeDtypeStruct((n, DP), jnp.int32),
                   jax.ShapeDtypeStruct((N_EXPERTS, n), F32),
                   jax.ShapeDtypeStruct((batch, GLA_HEADS, GLA_DK, GLA_DV), F32)],
        scratch_shapes=[pltpu.VMEM((GLA_HEADS, GLA_DV, GLA_DK), F32),
                        pltpu.VMEM((tt, GLA_DV_TOT), F32),
                        pltpu.VMEM((tt, GLA_DK_TOT), BF16),
                        pltpu.VMEM((tt // GLA_CHUNK * GLA_HEADS, GLA_DV, GLA_DK), F32)],
        compiler_params=_cparams("parallel", "arbitrary"),
        name="gla_mixer",
    )(x2d, routed, hp, mod3_prev, *prev_consts, mod3, *consts)


def _gla1_proj_body(x_ref, mod_ref, ng_ref, wqkvg_ref, wlr_ref, wgk_ref, bgk_ref,
                    q_ref, k_ref, v_ref, go_ref, dec_ref):
    sh1, sc1, _, _, _, _ = _mod_slices(mod_ref)
    ng = ng_ref[...]
    hb = (_rms(x_ref[...], ng[0:1]) * (1.0 + sc1) + sh1).astype(BF16)
    proj = jnp.dot(hb, wqkvg_ref[...], preferred_element_type=F32)
    q_ref[...] = proj[:, :GLA_DK_TOT] * (GLA_DK ** -0.5)
    k_ref[...] = proj[:, GLA_DK_TOT:2 * GLA_DK_TOT]
    v_ref[...] = proj[:, 2 * GLA_DK_TOT:2 * GLA_DK_TOT + GLA_DV_TOT]
    go_ref[...] = proj[:, 2 * GLA_DK_TOT + GLA_DV_TOT:]
    dec_ref[...] = jnp.exp(_gla_gate(hb, wlr_ref, wgk_ref, bgk_ref))


GLA1_TOK = 8


def _gla1_state_body(st_ref, qc_ref, kc_ref, dc_ref, v_ref, nst_ref, o_ref):
    v = v_ref[...]
    for i in range(GLA1_TOK):
        for hd in range(GLA_HEADS):
            vrow = v[i:i + 1, hd * GLA_DV:(hd + 1) * GLA_DV]
            s_new = dc_ref[hd][:, i:i + 1] * st_ref[i, hd] + kc_ref[hd][:, i:i + 1] * vrow
            nst_ref[i, hd] = s_new
            o_ref[i:i + 1, hd * GLA_DV:(hd + 1) * GLA_DV] = jnp.sum(
                qc_ref[hd][:, i:i + 1] * s_new, axis=0, keepdims=True)


def _gla1_out_body(x_ref, o_ref, go_ref, mod_ref, ng_ref, gng_ref, wout_ref, rw_ref, x1_ref, h_ref, lg_ref):
    _, _, g1, sh2, sc2, _ = _mod_slices(mod_ref)
    ng = ng_ref[...]
    y = jnp.dot(_gla_out(o_ref[...], go_ref[...], gng_ref[...]), wout_ref[...], preferred_element_type=F32)
    x1 = x_ref[...] + g1 * _rms(y, ng[1:2])
    x1_ref[...] = x1
    _ffn_prep(x1, ng, sh2, sc2, rw_ref, h_ref, lg_ref)


def _gla_mixer_one(x2d, mod2, state, ng, wqkvg, wlr, wgk, bgk, gng, wout, rw_t):
    n = x2d.shape[0]
    consts = (ng, wqkvg, wlr, wgk, bgk)
    q, k, v, go, dec = pl.pallas_call(
        _gla1_proj_body,
        in_specs=[_const_spec(a.shape) for a in (x2d, mod2) + consts],
        out_specs=[_const_spec((n, GLA_DK_TOT)), _const_spec((n, GLA_DK_TOT)), _const_spec((n, GLA_DV_TOT)),
                   _const_spec((n, GLA_DV_TOT)), _const_spec((n, GLA_DK_TOT))],
        out_shape=[jax.ShapeDtypeStruct((n, GLA_DK_TOT), F32), jax.ShapeDtypeStruct((n, GLA_DK_TOT), F32),
                   jax.ShapeDtypeStruct((n, GLA_DV_TOT), F32), jax.ShapeDtypeStruct((n, GLA_DV_TOT), F32),
                   jax.ShapeDtypeStruct((n, GLA_DK_TOT), F32)],
        grid=(1,),
        compiler_params=_cparams("arbitrary"),
        name="gla1_proj",
    )(x2d, mod2, *consts)

    def cols(a):
        return a.reshape(n // GLA1_TOK, GLA1_TOK, GLA_HEADS, GLA_DK).transpose(0, 2, 3, 1)

    col_spec = pl.BlockSpec((None, GLA_HEADS, GLA_DK, GLA1_TOK), lambda i: (i, 0, 0, 0))
    st_spec = pl.BlockSpec((GLA1_TOK, GLA_HEADS, GLA_DK, GLA_DV), lambda i: (i, 0, 0, 0))
    new_state, o = pl.pallas_call(
        _gla1_state_body,
        grid=(n // GLA1_TOK,),
        in_specs=[st_spec, col_spec, col_spec, col_spec, pl.BlockSpec((GLA1_TOK, GLA_DV_TOT), lambda i: (i, 0))],
        out_specs=[st_spec, pl.BlockSpec((GLA1_TOK, GLA_DV_TOT), lambda i: (i, 0))],
        out_shape=[jax.ShapeDtypeStruct(state.shape, F32), jax.ShapeDtypeStruct((n, GLA_DV_TOT), F32)],
        compiler_params=_cparams("parallel"),
        name="gla1_state",
    )(state, cols(q), cols(k), cols(dec), v)

    consts = (mod2, ng, gng, wout, rw_t)
    x1, h, lg = pl.pallas_call(
        _gla1_out_body,
        grid=(1,),
        in_specs=[_const_spec(a.shape) for a in (x2d, o, go) + consts],
        out_specs=[_const_spec((n, D)), _const_spec((n, DP)), _const_spec((N_EXPERTS, n))],
        out_shape=[jax.ShapeDtypeStruct((n, D), F32), jax.ShapeDtypeStruct((n, DP), jnp.int32),
                   jax.ShapeDtypeStruct((N_EXPERTS, n), F32)],
        compiler_params=_cparams("arbitrary"),
        name="gla1_out",
    )(x2d, o, go, *consts)
    return x1, h, lg, new_state


def _moe_routed(h_p, h_s, lg_p, lg_s, router_bias, layer, wg, wu, wd):
    h, lg = h_p, lg_p
    if h_s is not None:
        h = jnp.concatenate([h_p, h_s], axis=0)
        lg = jnp.concatenate([lg_p, lg_s], axis=1)
    n = h.shape[0]
    n_pad = -(-n // TOKEN_PAD) * TOKEN_PAD
    if n_pad != n:
        h = jnp.pad(h, ((0, n_pad - n), (0, 0)))
        lg = jnp.pad(lg, ((0, 0), (0, n_pad - n)))
    eid, rank, wts, counts = _router(lg, router_bias, n)
    tile_count = ((counts[:, 0] + EXPERT_TILE - 1) // EXPERT_TILE).astype(jnp.int32)
    tile_end = jnp.cumsum(tile_count).astype(jnp.int32)
    tile_first = tile_end - tile_count
    off = tile_first * EXPERT_TILE
    p_alloc = TOP_K * n_pad + N_EXPERTS * EXPERT_TILE
    dest = _dest(off, eid, rank, n, p_alloc - 1)
    dest_w = dest.reshape(TOP_K, n_pad // DISPATCH_W, DISPATCH_W).transpose(1, 0, 2)
    xs = _sc_dispatch(h, dest_w, p_alloc)
    ys = _experts(xs, tile_first, tile_count, tile_end[-1:], layer, wg, wu, wd)
    dest_tm = dest.T.reshape(n_pad // SUM_W, SUM_PARTS, SUM_W * TOP_K // SUM_PARTS)
    w_lanes = jnp.repeat(wts.T, SC_LANES, axis=1)
    return _sc_gather_sum(ys, dest_tm, w_lanes), h


def kernel(x_prompt, x_sample, state_gla, c_prompt, c_sample, norm_g, ada_w, ada_b, gm_w_in, gm_b_in,
           gm_ln_g, gm_ln_b, gm_w_s, gm_b_s, gm_w_out, gla_w_in, gla_w_gk, gla_b_gk, gla_norm_g,
           gla_w_out, router_w, router_bias, exp_w_gate, exp_w_up, exp_w_down, sh_w_gate, sh_w_up,
           sh_w_down):
    batch, seq, _ = x_prompt.shape
    n_s = x_sample.shape[0]
    n_p = batch * seq
    tpb = seq // MIX_TILE
    xp = x_prompt.reshape(n_p, D)
    xs = x_sample.reshape(n_s, D)

    mod = _ada(jnp.concatenate([c_prompt, c_sample], axis=0), ada_w, ada_b)
    mod_p = [mod[i, :batch].reshape(batch, 1, 6 * D) for i in range(2)]
    mod_s = [mod[i, batch:] for i in range(2)]
    rw_t = [jnp.concatenate(_split3(router_w[i].T), axis=0) for i in range(2)]

    ws_causal = jnp.tril(gm_w_s[0]).astype(BF16)
    bs_cols = gm_b_s[0].T
    eye = jnp.eye(GM_CHUNK, dtype=F32)
    ws_first = (gm_w_s[0][:, 0, 0][:, None, None] * eye).astype(BF16)
    bs_first = jnp.broadcast_to(gm_b_s[0][:, 0][None, :], (GM_CHUNK, GM_GROUPS))
    gm_args = (norm_g[0], gm_w_in[0].astype(BF16), gm_b_in[0].reshape(1, -1), gm_ln_g[0].reshape(1, -1),
               gm_ln_b[0].reshape(1, -1))
    wout0 = gm_w_out[0].astype(BF16)
    shared = [(sh_w_gate[i].astype(BF16), sh_w_up[i].astype(BF16), sh_w_down[i].astype(BF16))
              for i in range(2)]
    n_qkvg = 2 * GLA_DK_TOT + 2 * GLA_DV_TOT
    wqkvg = gla_w_in[0][:, :n_qkvg].astype(BF16)
    wlr = jnp.pad(gla_w_in[0][:, n_qkvg:], ((0, 0), (0, LANES - GLA_GATE_RANK))).astype(BF16)
    wgk = jnp.pad(gla_w_gk[0], ((0, LANES - GLA_GATE_RANK), (0, 0))).astype(BF16)
    gla_args = (norm_g[1], wqkvg, wlr, wgk, gla_b_gk[0].reshape(1, -1), gla_norm_g[0].reshape(1, -1),
                gla_w_out[0].astype(BF16), rw_t[1])
    experts = (exp_w_gate, exp_w_up, exp_w_down)

    half = batch // 2
    streams = [(0, half, False), (half, batch - half, True)]
    st = [dict() for _ in streams]

    for s, (b0, nb, with_new) in zip(st, streams):
        s["mod_p"] = [mod_p[i][b0:b0 + nb] for i in range(2)]
        s["n"] = nb * seq
        s["x1p"], s["hp"], s["lgp"] = _gmlp_mixer(xp, b0 * tpb, s["n"], s["mod_p"][0], False, MIX_TILE, tpb,
                                                  *gm_args, ws_causal, bs_cols, wout0, rw_t[0], emit_v=False)
        s["hs"] = s["lgs"] = None
        if with_new:
            s["x1s"], s["hs"], s["lgs"], v_rows = _gmlp_mixer(xs, 0, n_s, mod_s[0], True, n_s, 1, *gm_args,
                                                              ws_first, bs_first, wout0, rw_t[0], emit_v=True)
    for s, (b0, nb, with_new) in zip(st, streams):
        moe = s["moe0"] = _moe_routed(s["hp"], s["hs"], s["lgp"], s["lgs"], router_bias[0], 0, *experts)
        if with_new:
            s["x2s"] = _combine(s["x1s"], *moe, s["n"] // n_s, mod_s[0], True, n_s, 1, norm_g[0], *shared[0])
    for s, (b0, nb, with_new) in zip(st, streams):
        s["x3p"], s["hp"], s["lgp"], s["st_p"] = _gla_mixer(s["x1p"], *s["moe0"], s["mod_p"][0], norm_g[0],
                                                            shared[0], s["mod_p"][1], nb, seq, *gla_args)
        if with_new:
            s["x3s"], s["hs"], s["lgs"], st_s = _gla_mixer_one(s["x2s"], mod_s[1], state_gla[:, 0], *gla_args)
    y_prompt = None
    for s, (b0, nb, with_new) in zip(st, streams):
        moe = _moe_routed(s["hp"], s["hs"], s["lgp"], s["lgs"], router_bias[1], 1, *experts)
        y_prompt = _combine(s["x3p"], *moe, 0, s["mod_p"][1], False, MIX_TILE, tpb, norm_g[1], *shared[1],
                            out_rows=n_p, out_blk0=b0 * tpb, out_buf=y_prompt)
        if with_new:
            y_new = _combine(s["x3s"], *moe, s["n"] // n_s, mod_s[1], True, n_s, 1, norm_g[1], *shared[1])
    st_p = jnp.concatenate([s["st_p"] for s in st], axis=0)

    return (y_prompt.reshape(batch, seq, D), y_new.reshape(n_s, 1, D), st_p[:, None], st_s[:, None],
            v_rows.reshape(n_s, 1, 1, GM_HALF))
```

```python
import functools
import math

import jax
import jax.numpy as jnp
from jax import lax
from jax.experimental import pallas as pl
from jax.experimental.pallas import tpu as pltpu
from jax.experimental.pallas import tpu_sc as plsc

F32 = jnp.float32
BF16 = jnp.bfloat16

D = 1024
DP = D // 2
GM_CHUNK = 128
GM_HALF = 2 * D
GM_GROUPS = 8
GM_GROUP_DIM = GM_HALF // GM_GROUPS
GLA_HEADS = 4
GLA_DK = 128
GLA_DV = 256
GLA_DK_TOT = GLA_HEADS * GLA_DK
GLA_DV_TOT = GLA_HEADS * GLA_DV
GLA_GATE_RANK = 16
GLA_GATE_NORMALIZER = 16.0
GLA_CHUNK = 64
N_EXPERTS = 64
TOP_K = 8
N_EXPERT_GROUPS = 8
GROUP_SIZE = N_EXPERTS // N_EXPERT_GROUPS
TOPK_GROUPS = 4
EXPERT_DIM = D // 4
ROUTED_SCALE = 2.5
NORM_EPS = 1e-6
LN_EPS = 1e-5

LANES = 128
VMEM_LIMIT = 56 * 1024 * 1024

MIX_TILE = 256
GLA_TILE = 512
GM_COL_BLOCK = 512
ROUTER_TILE = 1024
EXPERT_TILE = 272
EXPERT_X_SLOTS = 6
EXPERT_AHEAD = EXPERT_X_SLOTS - 2
EXPERT_Y_SLOTS = 4
EXPERT_W_PARTS = 4
SC_WORKERS = 32
DISPATCH_W = 32
SC_LANES = 16
SUM_W = 16
SUM_PARTS = 4
SUM_UNROLL = 4
TOKEN_PAD = SC_WORKERS * DISPATCH_W


def _cparams(*sem):
    return pltpu.CompilerParams(dimension_semantics=sem, vmem_limit_bytes=VMEM_LIMIT)


def _rms(x, g):
    return x * lax.rsqrt(jnp.mean(x * x, axis=-1, keepdims=True) + NORM_EPS) * g


def _silu(x):
    return x * (1.0 / (1.0 + jnp.exp(-x)))


def _gelu(x):
    return 0.5 * x * (1.0 + lax.erf(x * (1.0 / math.sqrt(2.0))))


def _bdot(a, b):
    return jnp.dot(a.astype(BF16), b.astype(BF16), preferred_element_type=F32)


def _dot_nt(a, b, precision=None):
    return lax.dot_general(a, b, (((1,), (1,)), ((), ())), preferred_element_type=F32,
                           precision=precision)


def _mod_slices(mod_ref):
    return [mod_ref[:, i * D:(i + 1) * D] for i in range(6)]


HI_HALF = -65536


def _pack_rows(x):
    lo = lax.bitcast_convert_type(x[:, :DP].astype(BF16).astype(F32), jnp.int32)
    hi = lax.bitcast_convert_type(x[:, DP:].astype(BF16).astype(F32), jnp.int32)
    return lax.shift_right_logical(lo, 16) | (hi & HI_HALF)


def _unpack_rows(p):
    lo = lax.bitcast_convert_type(lax.shift_left(p, 16), F32)
    hi = lax.bitcast_convert_type(p & HI_HALF, F32)
    return lo, hi


def _ffn_prep(x1, ng, sh2, sc2, rw_ref, h_ref, lg_ref):
    hffn = _rms(x1, ng[2:3]) * (1.0 + sc2) + sh2
    h_ref[...] = _pack_rows(hffn)
    lg3 = _dot_nt(rw_ref[...], hffn.astype(BF16))
    lg_ref[...] = lg3[:N_EXPERTS] + lg3[N_EXPERTS:2 * N_EXPERTS] + lg3[2 * N_EXPERTS:]


def _ada_body(c_ref, w_ref, b_ref, o_ref):
    c = c_ref[...]
    o_ref[...] = _bdot(_silu(c), w_ref[...]) + b_ref[...]


def _ada(c, ada_w, ada_b):
    n = c.shape[0]
    depth = ada_w.shape[0]
    tn = 1536
    return pl.pallas_call(
        _ada_body,
        grid=(depth, 6 * D // tn),
        in_specs=[pl.BlockSpec((n, D), lambda l, j: (0, 0)),
                  pl.BlockSpec((None, D, tn), lambda l, j: (l, 0, j)),
                  pl.BlockSpec((None, 1, tn), lambda l, j: (l, 0, j))],
        out_specs=pl.BlockSpec((None, n, tn), lambda l, j: (l, 0, j)),
        out_shape=jax.ShapeDtypeStruct((depth, n, 6 * D), F32),
        compiler_params=_cparams("parallel", "parallel"),
        name="ada_mod",
    )(c, ada_w, ada_b.reshape(depth, 1, 6 * D))


def _mod_spec(per_row, tt, tiles_per_batch):
    if per_row:
        return pl.BlockSpec((tt, 6 * D), lambda i: (i, 0))
    return pl.BlockSpec((None, 1, 6 * D), lambda i: (i // tiles_per_batch, 0, 0))


def _const_spec(shape):
    zeros = (0,) * len(shape)
    return pl.BlockSpec(shape, lambda *_: zeros)


def _gmlp_body(x_ref, mod_ref, ng_ref, win_ref, bin_ref, lng_ref, lnb_ref, ws_ref, bs_ref, wout_ref,
               rw_ref, x1_ref, h_ref, lg_ref, *rest, n_chunks, emit_v):
    if emit_v:
        v_ref, um_ref, z_ref = rest
    else:
        um_ref, z_ref = rest
    sh1, sc1, g1, sh2, sc2, _ = _mod_slices(mod_ref)
    ng = ng_ref[...]
    x = x_ref[...]
    hb = (_rms(x, ng[0:1]) * (1.0 + sc1) + sh1).astype(BF16)
    for cb in range(2 * GM_HALF // GM_COL_BLOCK):
        cols = slice(cb * GM_COL_BLOCK, (cb + 1) * GM_COL_BLOCK)
        z_ref[:, cols] = _gelu(jnp.dot(hb, win_ref[:, cols], preferred_element_type=F32) + bin_ref[:, cols])
    u = z_ref[:, :GM_HALF]
    v = z_ref[:, GM_HALF:]
    mu = jnp.mean(v, axis=-1, keepdims=True)
    vc = v - mu
    var = jnp.mean(vc * vc, axis=-1, keepdims=True)
    v = vc * lax.rsqrt(var + LN_EPS) * lng_ref[...] + lnb_ref[...]
    if emit_v:
        v_ref[...] = v
    vb = v.astype(BF16)
    for c in range(n_chunks):
        rows = slice(c * GM_CHUNK, (c + 1) * GM_CHUNK)
        for g in range(GM_GROUPS):
            cols = slice(g * GM_GROUP_DIM, (g + 1) * GM_GROUP_DIM)
            mixed = jnp.dot(ws_ref[g], vb[rows, cols], preferred_element_type=F32) + bs_ref[:, g:g + 1]
            um_ref[rows, cols] = (u[rows, cols] * mixed).astype(BF16)
    y = jnp.dot(um_ref[...], wout_ref[...], preferred_element_type=F32)
    x1 = x + g1 * _rms(y, ng[1:2])
    x1_ref[...] = x1
    _ffn_prep(x1, ng, sh2, sc2, rw_ref, h_ref, lg_ref)


def _gmlp_mixer(x2d, blk0, n, mod, per_row, tt, tiles_per_batch, ng, win, b_in, ln_g, ln_b, ws, bs, wout,
                rw_t, emit_v):
    out_shape = [jax.ShapeDtypeStruct((n, D), F32), jax.ShapeDtypeStruct((n, DP), jnp.int32),
                 jax.ShapeDtypeStruct((N_EXPERTS, n), F32)]
    out_specs = [pl.BlockSpec((tt, D), lambda i: (i, 0)), pl.BlockSpec((tt, DP), lambda i: (i, 0)),
                 pl.BlockSpec((N_EXPERTS, tt), lambda i: (0, i))]
    if emit_v:
        out_shape.append(jax.ShapeDtypeStruct((n, GM_HALF), F32))
        out_specs.append(pl.BlockSpec((tt, GM_HALF), lambda i: (i, 0)))
    return pl.pallas_call(
        functools.partial(_gmlp_body, n_chunks=tt // GM_CHUNK, emit_v=emit_v),
        grid=(n // tt,),
        in_specs=[pl.BlockSpec((tt, D), lambda i: (i + blk0, 0)),
                  _mod_spec(per_row, tt, tiles_per_batch),
                  _const_spec(ng.shape), _const_spec(win.shape), _const_spec(b_in.shape),
                  _const_spec(ln_g.shape), _const_spec(ln_b.shape), _const_spec(ws.shape),
                  _const_spec(bs.shape), _const_spec(wout.shape), _const_spec(rw_t.shape)],
        out_specs=out_specs,
        out_shape=out_shape,
        scratch_shapes=[pltpu.VMEM((tt, GM_HALF), BF16), pltpu.VMEM((tt, 2 * GM_HALF), F32)],
        compiler_params=_cparams("parallel"),
        name="gmlp_mixer_rows" if per_row else "gmlp_mixer",
    )(x2d, mod, ng, win, b_in, ln_g, ln_b, ws, bs, wout, rw_t)


def _combine_body(x_ref, y_ref, h_ref, mod_ref, ng_ref, swg_ref, swu_ref, swd_ref, *rest):
    o_ref = rest[-1]
    o_ref[...] = _channel_mix_residual(x_ref[...], y_ref[...], h_ref, mod_ref[:, 5 * D:6 * D],
                                       ng_ref[3:4, :], swg_ref, swu_ref, swd_ref)


def _combine(x2d, routed, hp, blk0, mod, per_row, tt, tiles_per_batch, ng, swg, swu, swd,
             out_rows=None, out_blk0=0, out_buf=None):
    n = x2d.shape[0]
    in_specs = [pl.BlockSpec((tt, D), lambda i: (i, 0)),
                pl.BlockSpec((tt, D), lambda i: (i + blk0, 0)),
                pl.BlockSpec((tt, DP), lambda i: (i + blk0, 0)),
                _mod_spec(per_row, tt, tiles_per_batch),
                _const_spec(ng.shape), _const_spec(swg.shape), _const_spec(swu.shape),
                _const_spec(swd.shape)]
    args = [x2d, routed, hp, mod, ng, swg, swu, swd]
    aliases = {}
    if out_buf is not None:
        in_specs.append(pl.BlockSpec(memory_space=pl.ANY))
        aliases = {len(args): 0}
        args.append(out_buf)
    return pl.pallas_call(
        _combine_body,
        grid=(n // tt,),
        in_specs=in_specs,
        out_specs=pl.BlockSpec((tt, D), lambda i: (i + out_blk0, 0)),
        out_shape=jax.ShapeDtypeStruct((out_rows or n, D), F32),
        input_output_aliases=aliases,
        compiler_params=_cparams("parallel"),
        name="combine_rows" if per_row else "combine",
    )(*args)


def _router_body(lg_ref, bias_ref, tri_ref, eid_ref, rank_ref, wts_ref, cnt_ref, carry_ref, *, n_real):
    step = pl.program_id(0)

    @pl.when(step == 0)
    def _():
        carry_ref[...] = jnp.zeros_like(carry_ref)

    lg = lg_ref[...]
    tn = lg.shape[1]
    real = (step * tn + lax.broadcasted_iota(jnp.int32, (1, tn), 1)) < n_real
    lg = jnp.where(real, lg, 0.0)
    scores = 1.0 / (1.0 + jnp.exp(-lg))
    sel = scores + bias_ref[...]
    neg = -jnp.inf
    sub8 = lax.broadcasted_iota(jnp.int32, (GROUP_SIZE, tn), 0)
    gsub = lax.broadcasted_iota(jnp.int32, (N_EXPERT_GROUPS, tn), 0)
    gs = jnp.zeros((N_EXPERT_GROUPS, tn), F32)
    for g in range(N_EXPERT_GROUPS):
        blk = sel[g * GROUP_SIZE:(g + 1) * GROUP_SIZE, :]
        m1 = jnp.max(blk, axis=0, keepdims=True)
        i1 = jnp.min(jnp.where(blk == m1, sub8, GROUP_SIZE), axis=0, keepdims=True)
        m2 = jnp.max(jnp.where(sub8 == i1, neg, blk), axis=0, keepdims=True)
        gs = jnp.where(gsub == g, m1 + m2, gs)
    gmask = jnp.zeros((N_EXPERT_GROUPS, tn), jnp.bool_)
    for _ in range(TOPK_GROUPS):
        m = jnp.max(gs, axis=0, keepdims=True)
        i = jnp.min(jnp.where(gs == m, gsub, N_EXPERT_GROUPS), axis=0, keepdims=True)
        hit = gsub == i
        gmask = jnp.logical_or(gmask, hit)
        gs = jnp.where(hit, neg, gs)
    gmaskf = gmask.astype(F32)
    blocks = []
    for g in range(N_EXPERT_GROUPS):
        keep = jnp.broadcast_to(gmaskf[g:g + 1, :], (GROUP_SIZE, tn)) > 0.5
        blocks.append(jnp.where(keep, sel[g * GROUP_SIZE:(g + 1) * GROUP_SIZE, :], neg))
    msel = jnp.concatenate(blocks, axis=0)
    esub = lax.broadcasted_iota(jnp.int32, (N_EXPERTS, tn), 0)
    chosen = jnp.zeros((N_EXPERTS, tn), jnp.bool_)
    picks = []
    for _ in range(TOP_K):
        m = jnp.max(msel, axis=0, keepdims=True)
        i = jnp.min(jnp.where(msel == m, esub, N_EXPERTS), axis=0, keepdims=True)
        hit = esub == i
        picks.append(i)
        chosen = jnp.logical_or(chosen, hit)
        msel = jnp.where(hit, neg, msel)
    w = jnp.where(chosen, scores, 0.0)
    w = w / jnp.sum(w, axis=0, keepdims=True) * ROUTED_SCALE
    counted = jnp.where(jnp.logical_and(chosen, real), 1.0, 0.0)
    incl = jnp.dot(counted.astype(BF16), tri_ref[...], preferred_element_type=F32)
    rank_full = carry_ref[:, 0:1] + incl - 1.0
    ksub = lax.broadcasted_iota(jnp.int32, (TOP_K, tn), 0)
    eid = jnp.zeros((TOP_K, tn), jnp.int32)
    rank = jnp.zeros((TOP_K, tn), F32)
    wts = jnp.zeros((TOP_K, tn), F32)
    for k in range(TOP_K):
        hit = esub == picks[k]
        eid = jnp.where(ksub == k, picks[k], eid)
        rank = jnp.where(ksub == k, jnp.sum(jnp.where(hit, rank_full, 0.0), axis=0, keepdims=True), rank)
        wts = jnp.where(ksub == k, jnp.sum(jnp.where(hit, w, 0.0), axis=0, keepdims=True), wts)
    eid_ref[...] = eid
    rank_ref[...] = rank.astype(jnp.int32)
    wts_ref[...] = wts
    carry = carry_ref[...] + incl[:, tn - 1:tn]
    carry_ref[...] = carry
    cnt_ref[...] = carry.astype(jnp.int32)


def _router(lg_t, bias, n_real):
    n = lg_t.shape[1]
    tn = ROUTER_TILE
    idx = jnp.arange(tn)
    tri = (idx[:, None] <= idx[None, :]).astype(BF16)
    kspec = pl.BlockSpec((TOP_K, tn), lambda i: (0, i))
    return pl.pallas_call(
        functools.partial(_router_body, n_real=n_real),
        grid=(n // tn,),
        in_specs=[pl.BlockSpec((N_EXPERTS, tn), lambda i: (0, i)), _const_spec((N_EXPERTS, 1)),
                  _const_spec((tn, tn))],
        out_specs=[kspec, kspec, kspec, _const_spec((N_EXPERTS, LANES))],
        out_shape=[jax.ShapeDtypeStruct((TOP_K, n), jnp.int32), jax.ShapeDtypeStruct((TOP_K, n), jnp.int32),
                   jax.ShapeDtypeStruct((TOP_K, n), F32), jax.ShapeDtypeStruct((N_EXPERTS, LANES), jnp.int32)],
        scratch_shapes=[pltpu.VMEM((N_EXPERTS, LANES), F32)],
        compiler_params=_cparams("arbitrary"),
        name="router",
    )(lg_t, bias.reshape(N_EXPERTS, 1), tri)


def _dest_body(off_ref, eid_ref, rank_ref, dest_ref, *, n_real, last_row):
    eid = eid_ref[...]
    base = jnp.zeros(eid.shape, jnp.int32)
    for e in range(N_EXPERTS):
        base = jnp.where(eid == e, off_ref[e], base)
    tok = lax.broadcasted_iota(jnp.int32, eid.shape, 1)
    slot = lax.broadcasted_iota(jnp.int32, eid.shape, 0)
    unused = last_row - ((tok - n_real) * TOP_K + slot)
    dest_ref[...] = jnp.where(tok < n_real, base + rank_ref[...], unused)


def _dest(off, eid, rank, n_real, last_row):
    spec = pl.BlockSpec(eid.shape, lambda i, off_ref: (0, 0))
    return pl.pallas_call(
        functools.partial(_dest_body, n_real=n_real, last_row=last_row),
        grid_spec=pltpu.PrefetchScalarGridSpec(num_scalar_prefetch=1, grid=(1,), in_specs=[spec, spec],
                                               out_specs=spec),
        out_shape=jax.ShapeDtypeStruct(eid.shape, jnp.int32),
        compiler_params=_cparams("arbitrary"),
        name="dest_rows",
    )(off, eid, rank)


def _sc_mesh():
    return plsc.VectorSubcoreMesh(core_axis_name="core", subcore_axis_name="subcore")


def _sc_dispatch(hp, dest_w, p_alloc):
    n = hp.shape[0]
    w = dest_w.shape[2]

    @functools.partial(pl.kernel, out_type=jax.ShapeDtypeStruct((p_alloc, DP), jnp.int32), mesh=_sc_mesh(),
                       name="sc_dispatch")
    def run(hp_hbm, dest_hbm, xs_hbm):
        def body(x_vmem, i_vmem):
            for k in range(TOP_K):
                pltpu.sync_copy(x_vmem, xs_hbm.at[i_vmem.at[k]])

        pltpu.emit_pipeline(
            body,
            grid=(n // w,),
            in_specs=[pl.BlockSpec((w, DP), lambda i: (i, 0)),
                      pl.BlockSpec((None, TOP_K, w), lambda i: (i, 0, 0))],
            out_specs=[],
            core_axis_name=("core", "subcore"),
            dimension_semantics=(pltpu.PARALLEL,),
        )(hp_hbm, dest_hbm)

    return run(hp, dest_w)


def _sc_gather_sum(ys, dest_tm, w_lanes):
    n_win, parts, pk = dest_tm.shape
    w = parts * pk // TOP_K
    wp = w // parts
    n_vec = DP // SC_LANES

    @functools.partial(pl.kernel, out_type=jax.ShapeDtypeStruct((n_win * w, D), F32), mesh=_sc_mesh(),
                       scratch_types=[pltpu.VMEM((parts, pk, DP), jnp.int32), pltpu.SemaphoreType.DMA((parts,))],
                       compiler_params=pltpu.CompilerParams(needs_layout_passes=False), name="sc_gather_sum")
    def run(ys_hbm, dest_hbm, w_hbm, o_hbm, rows_v, sems):
        def body(i_vmem, w_vmem, o_vmem):
            copies = [pltpu.async_copy(ys_hbm.at[i_vmem.at[p]], rows_v.at[p], sems.at[p]) for p in range(parts)]
            for p in range(parts):
                copies[p].wait()

                @pl.loop(0, wp)
                def _(t):
                    tok = p * wp + t
                    wv = [w_vmem[tok, pl.ds(k * SC_LANES, SC_LANES)] for k in range(TOP_K)]

                    @pl.loop(0, n_vec, step=SUM_UNROLL)
                    def _(j0):
                        for u in range(SUM_UNROLL):
                            col = (j0 + u) * SC_LANES
                            lo = jnp.zeros((SC_LANES,), F32)
                            hi = jnp.zeros((SC_LANES,), F32)
                            for k in range(TOP_K):
                                word = rows_v[p, t * TOP_K + k, pl.ds(col, SC_LANES)]
                                lo = lo + wv[k] * plsc.bitcast(lax.shift_left(word, 16), F32)
                                hi = hi + wv[k] * plsc.bitcast(word & HI_HALF, F32)
                            o_vmem[tok, pl.ds(col, SC_LANES)] = lo
                            o_vmem[tok, pl.ds(DP + col, SC_LANES)] = hi

        pltpu.emit_pipeline(
            body,
            grid=(n_win,),
            in_specs=[pl.BlockSpec((None, parts, pk), lambda i: (i, 0, 0)),
                      pl.BlockSpec((w, TOP_K * SC_LANES), lambda i: (i, 0))],
            out_specs=[pl.BlockSpec((w, D), lambda i: (i, 0))],
            core_axis_name=("core", "subcore"),
            dimension_semantics=(pltpu.PARALLEL,),
        )(dest_hbm, w_hbm, o_hbm)

    return run(ys, dest_tm, w_lanes)


def _expert_body(first_ref, cnt_ref, nused_ref, xs_hbm, *refs):
    wg_refs = refs[:EXPERT_W_PARTS]
    wu_refs = refs[EXPERT_W_PARTS:2 * EXPERT_W_PARTS]
    wd_refs = refs[2 * EXPERT_W_PARTS:3 * EXPERT_W_PARTS]
    ys_hbm, wg_s, wu_s, wd_s, xbuf, ybuf, xsem, ysem = refs[3 * EXPERT_W_PARTS:]
    _expert_steps(first_ref, cnt_ref, nused_ref, xs_hbm, wg_refs, wu_refs, wd_refs, ys_hbm,
                  wg_s, wu_s, wd_s, xbuf, ybuf, xsem, ysem)


def _expert_steps(first_ref, cnt_ref, nused_ref, xs_hbm, wg_refs, wu_refs, wd_refs, ys_hbm,
                  wg_s, wu_s, wd_s, xbuf, ybuf, xsem, ysem):
    e = pl.program_id(0)
    n_used = nused_ref[0]

    def load(g):
        rows = pl.ds(pl.multiple_of(g * EXPERT_TILE, EXPERT_TILE), EXPERT_TILE)
        slot = g % EXPERT_X_SLOTS
        return pltpu.make_async_copy(xs_hbm.at[rows], xbuf.at[slot], xsem.at[slot])

    def store(g):
        rows = pl.ds(pl.multiple_of(g * EXPERT_TILE, EXPERT_TILE), EXPERT_TILE)
        slot = g % EXPERT_Y_SLOTS
        return pltpu.make_async_copy(ybuf.at[slot], ys_hbm.at[rows], ysem.at[slot])

    @pl.when(e == 0)
    def _():
        for g in range(EXPERT_AHEAD):
            @pl.when(g < n_used)
            def _():
                load(g).start()

    part = D // EXPERT_W_PARTS
    for r in range(EXPERT_W_PARTS):
        wg_s[r * part:(r + 1) * part, :] = wg_refs[r][...].astype(BF16)
        wu_s[r * part:(r + 1) * part, :] = wu_refs[r][...].astype(BF16)
        wd_s[:, r * part:(r + 1) * part] = wd_refs[r][...].astype(BF16)
    first = first_ref[e]
    cnt = cnt_ref[e]

    def acquire(g):
        ahead = g + EXPERT_AHEAD

        @pl.when(ahead < n_used)
        def _():
            load(ahead).start()

        load(g).wait()

        @pl.when(g >= EXPERT_Y_SLOTS)
        def _():
            store(g - EXPERT_Y_SLOTS).wait()

    def compute(g):
        lo, hi = _unpack_rows(xbuf[g % EXPERT_X_SLOTS])
        lo = lo.astype(BF16)
        hi = hi.astype(BF16)

        def xdot(w_s):
            return (jnp.dot(lo, w_s[:DP, :], preferred_element_type=F32) +
                    jnp.dot(hi, w_s[DP:, :], preferred_element_type=F32))

        a = (_silu(xdot(wg_s)) * xdot(wu_s)).astype(BF16)
        ybuf[g % EXPERT_Y_SLOTS] = _pack_rows(jnp.dot(a, wd_s[...], preferred_element_type=F32))

    def pair(j, carry):
        g = first + 2 * j
        acquire(g)
        acquire(g + 1)
        compute(g)
        compute(g + 1)
        store(g).start()
        store(g + 1).start()
        return carry

    lax.fori_loop(0, cnt // 2, pair, 0)

    @pl.when(cnt % 2 == 1)
    def _():
        g = first + cnt - 1
        acquire(g)
        compute(g)
        store(g).start()

    @pl.when(e == N_EXPERTS - 1)
    def _():
        for k in range(EXPERT_Y_SLOTS):
            g = n_used - 1 - k

            @pl.when(g >= 0)
            def _():
                store(g).wait()


def _experts(xs, tile_first, tile_count, n_used, layer, wg, wu, wd):
    part = D // EXPERT_W_PARTS
    in_rows = [pl.BlockSpec((None, None, part, EXPERT_DIM),
                            functools.partial(lambda e, first, cnt, nu, r: (layer, e, r, 0), r=r))
               for r in range(EXPERT_W_PARTS)]
    out_cols = [pl.BlockSpec((None, None, EXPERT_DIM, part),
                             functools.partial(lambda e, first, cnt, nu, c: (layer, e, 0, c), c=c))
                for c in range(EXPERT_W_PARTS)]

    return pl.pallas_call(
        _expert_body,
        grid_spec=pltpu.PrefetchScalarGridSpec(
            num_scalar_prefetch=3, grid=(N_EXPERTS,),
            in_specs=[pl.BlockSpec(memory_space=pl.ANY)] + in_rows + in_rows + out_cols,
            out_specs=pl.BlockSpec(memory_space=pl.ANY),
            scratch_shapes=[pltpu.VMEM((D, EXPERT_DIM), BF16), pltpu.VMEM((D, EXPERT_DIM), BF16),
                            pltpu.VMEM((EXPERT_DIM, D), BF16),
                            pltpu.VMEM((EXPERT_X_SLOTS, EXPERT_TILE, DP), jnp.int32),
                            pltpu.VMEM((EXPERT_Y_SLOTS, EXPERT_TILE, DP), jnp.int32),
                            pltpu.SemaphoreType.DMA((EXPERT_X_SLOTS,)),
                            pltpu.SemaphoreType.DMA((EXPERT_Y_SLOTS,))]),
        out_shape=jax.ShapeDtypeStruct(xs.shape, jnp.int32),
        compiler_params=_cparams("arbitrary"),
        name="experts",
    )(tile_first, tile_count, n_used, xs, *([wg] * EXPERT_W_PARTS), *([wu] * EXPERT_W_PARTS),
      *([wd] * EXPERT_W_PARTS))


def _log_sigmoid(z):
    return jnp.minimum(z, 0.0) - jnp.log(1.0 + jnp.exp(-jnp.abs(z)))


def _gla_gate(hb, wlr_ref, wgk_ref, bgk_ref):
    lr = jnp.dot(hb, wlr_ref[...], preferred_element_type=F32)
    z = _bdot(lr, wgk_ref[...]) + bgk_ref[...]
    return _log_sigmoid(z) * (1.0 / GLA_GATE_NORMALIZER)


def _split3(a):
    hi = a.astype(BF16)
    r1 = a - hi.astype(F32)
    mid = r1.astype(BF16)
    lo = (r1 - mid.astype(F32)).astype(BF16)
    return hi, mid, lo


def _gla_out(o_ref_val, go, gng):
    parts = []
    for hd in range(GLA_HEADS):
        cols = slice(hd * GLA_DV, (hd + 1) * GLA_DV)
        parts.append((_rms(o_ref_val[:, cols], gng) * _silu(go[:, cols])).astype(BF16))
    return jnp.concatenate(parts, axis=1)


def _channel_mix_residual(x, routed, h_ref, g2, ng3, swg_ref, swu_ref, swd_ref):
    h_lo, h_hi = _unpack_rows(h_ref[...])
    h_lo = h_lo.astype(BF16)
    h_hi = h_hi.astype(BF16)

    def hdot(w_ref_):
        return (jnp.dot(h_lo, w_ref_[:DP, :], preferred_element_type=F32) +
                jnp.dot(h_hi, w_ref_[DP:, :], preferred_element_type=F32))

    hs = (_silu(hdot(swg_ref)) * hdot(swu_ref)).astype(BF16)
    y = jnp.dot(hs, swd_ref[...], preferred_element_type=F32) + routed
    return x + g2 * _rms(y, ng3)


def _gla_body(x_ref, y_ref, hprev_ref, modprev_ref, ngprev_ref, swg_ref, swu_ref, swd_ref,
              mod_ref, ng_ref, wqkvg_ref, wlr_ref, wgk_ref, bgk_ref, tril_ref, gng_ref, wout_ref,
              rw_ref, x1_ref, h_ref, lg_ref, st_ref, st_scr, o_scr, qd_scr, dst_scr, *, tt):
    j = pl.program_id(1)

    @pl.when(j == 0)
    def _():
        st_scr[...] = jnp.zeros_like(st_scr)

    x = _channel_mix_residual(x_ref[...], y_ref[...], hprev_ref, modprev_ref[:, 5 * D:6 * D],
                              ngprev_ref[3:4, :], swg_ref, swu_ref, swd_ref)
    sh1, sc1, g1, sh2, sc2, _ = _mod_slices(mod_ref)
    ng = ng_ref[...]
    hb = (_rms(x, ng[0:1]) * (1.0 + sc1) + sh1).astype(BF16)
    q = jnp.dot(hb, wqkvg_ref[:, :GLA_DK_TOT], preferred_element_type=F32) * (GLA_DK ** -0.5)
    k = jnp.dot(hb, wqkvg_ref[:, GLA_DK_TOT:2 * GLA_DK_TOT], preferred_element_type=F32)
    v = jnp.dot(hb, wqkvg_ref[:, 2 * GLA_DK_TOT:2 * GLA_DK_TOT + GLA_DV_TOT],
                preferred_element_type=F32).astype(BF16)
    log_a = _gla_gate(hb, wlr_ref, wgk_ref, bgk_ref)
    tril = tril_ref[...]
    b = sum(jnp.dot(tril, part, preferred_element_type=F32) for part in _split3(log_a))
    row = lax.broadcasted_iota(jnp.int32, (GLA_CHUNK, GLA_CHUNK), 0)
    col = lax.broadcasted_iota(jnp.int32, (GLA_CHUNK, GLA_CHUNK), 1)
    causal = row >= col
    n_chunks = tt // GLA_CHUNK
    for c in range(n_chunks):
        rows = slice(c * GLA_CHUNK, (c + 1) * GLA_CHUNK)
        last = (c + 1) * GLA_CHUNK - 1
        for hd in range(GLA_HEADS):
            kc = slice(hd * GLA_DK, (hd + 1) * GLA_DK)
            vc = slice(hd * GLA_DV, (hd + 1) * GLA_DV)
            bb = b[rows, kc]
            b_last = b[last:last + 1, kc]
            q_dec = (q[rows, kc] * jnp.exp(bb)).astype(BF16)
            k_inv = (k[rows, kc] * jnp.exp(-bb)).astype(BF16)
            k_end = (k[rows, kc] * jnp.exp(b_last - bb)).astype(BF16)
            att = jnp.where(causal, _dot_nt(q_dec, k_inv), 0.0).astype(BF16)
            qd_scr[rows, kc] = q_dec
            o_scr[rows, vc] = jnp.dot(att, v[rows, vc], preferred_element_type=F32)
            dst_scr[c * GLA_HEADS + hd] = lax.dot_general(
                v[rows, vc], k_end, (((0,), (0,)), ((), ())), preferred_element_type=F32)
    states = [st_scr[hd] for hd in range(GLA_HEADS)]
    for c in range(n_chunks):
        rows = slice(c * GLA_CHUNK, (c + 1) * GLA_CHUNK)
        last = (c + 1) * GLA_CHUNK - 1
        for hd in range(GLA_HEADS):
            kc = slice(hd * GLA_DK, (hd + 1) * GLA_DK)
            vc = slice(hd * GLA_DV, (hd + 1) * GLA_DV)
            o_scr[rows, vc] += _dot_nt(qd_scr[rows, kc], states[hd].astype(BF16))
            states[hd] = states[hd] * jnp.exp(b[last:last + 1, kc]) + dst_scr[c * GLA_HEADS + hd]
    for hd in range(GLA_HEADS):
        st_scr[hd] = states[hd]

    @pl.when(j == pl.num_programs(1) - 1)
    def _():
        for hd in range(GLA_HEADS):
            st_ref[hd] = st_scr[hd].T

    go = jnp.dot(hb, wqkvg_ref[:, 2 * GLA_DK_TOT + GLA_DV_TOT:], preferred_element_type=F32)
    y = jnp.dot(_gla_out(o_scr[...], go, gng_ref[...]), wout_ref[...], preferred_element_type=F32)
    x1 = x + g1 * _rms(y, ng[1:2])
    x1_ref[...] = x1
    _ffn_prep(x1, ng, sh2, sc2, rw_ref, h_ref, lg_ref)


def _gla_mixer(x2d, routed, hp, mod3_prev, ng_prev, shared_prev, mod3, batch, seq, ng, wqkvg, wlr, wgk, bgk,
               gng, wout, rw_t):
    tt = GLA_TILE
    tpb = seq // tt
    n = x2d.shape[0]
    idx = jnp.arange(tt)
    tril = ((idx[:, None] >= idx[None, :]) &
            (idx[:, None] // GLA_CHUNK == idx[None, :] // GLA_CHUNK)).astype(BF16)
    row_map = lambda b, j: (b * tpb + j, 0)
    mod_map = lambda b, j: (b, 0, 0)
    consts = (ng, wqkvg, wlr, wgk, bgk, tril, gng, wout, rw_t)
    prev_consts = (ng_prev,) + tuple(shared_prev)
    return pl.pallas_call(
        functools.partial(_gla_body, tt=tt),
        grid=(batch, tpb),
        in_specs=[pl.BlockSpec((tt, D), row_map), pl.BlockSpec((tt, D), row_map),
                  pl.BlockSpec((tt, DP), row_map), pl.BlockSpec((None, 1, 6 * D), mod_map)] +
                 [_const_spec(a.shape) for a in prev_consts] +
                 [pl.BlockSpec((None, 1, 6 * D), mod_map)] +
                 [_const_spec(a.shape) for a in consts],
        out_specs=[pl.BlockSpec((tt, D), row_map), pl.BlockSpec((tt, DP), row_map),
                   pl.BlockSpec((N_EXPERTS, tt), lambda b, j: (0, b * tpb + j)),
                   pl.BlockSpec((None, GLA_HEADS, GLA_DK, GLA_DV), lambda b, j: (b, 0, 0, 0))],
        out_shape=[jax.ShapeDtypeStruct((n, D), F32), jax.ShapeDtypeStruct((n, DP), jnp.int32),
                   jax.ShapeDtypeStruct((N_EXPERTS, n), F32),
                   jax.ShapeDtypeStruct((batch, GLA_HEADS, GLA_DK, GLA_DV), F32)],
        scratch_shapes=[pltpu.VMEM((GLA_HEADS, GLA_DV, GLA_DK), F32),
                        pltpu.VMEM((tt, GLA_DV_TOT), F32),
                        pltpu.VMEM((tt, GLA_DK_TOT), BF16),
                        pltpu.VMEM((tt // GLA_CHUNK * GLA_HEADS, GLA_DV, GLA_DK), F32)],
        compiler_params=_cparams("parallel", "arbitrary"),
        name="gla_mixer",
    )(x2d, routed, hp, mod3_prev, *prev_consts, mod3, *consts)


def _gla1_proj_body(x_ref, mod_ref, ng_ref, wqkvg_ref, wlr_ref, wgk_ref, bgk_ref,
                    q_ref, k_ref, v_ref, go_ref, dec_ref):
    sh1, sc1, _, _, _, _ = _mod_slices(mod_ref)
    ng = ng_ref[...]
    hb = (_rms(x_ref[...], ng[0:1]) * (1.0 + sc1) + sh1).astype(BF16)
    proj = jnp.dot(hb, wqkvg_ref[...], preferred_element_type=F32)
    q_ref[...] = proj[:, :GLA_DK_TOT] * (GLA_DK ** -0.5)
    k_ref[...] = proj[:, GLA_DK_TOT:2 * GLA_DK_TOT]
    v_ref[...] = proj[:, 2 * GLA_DK_TOT:2 * GLA_DK_TOT + GLA_DV_TOT]
    go_ref[...] = proj[:, 2 * GLA_DK_TOT + GLA_DV_TOT:]
    dec_ref[...] = jnp.exp(_gla_gate(hb, wlr_ref, wgk_ref, bgk_ref))


GLA1_TOK = 8


def _gla1_state_body(st_ref, qc_ref, kc_ref, dc_ref, v_ref, nst_ref, o_ref):
    v = v_ref[...]
    for i in range(GLA1_TOK):
        for hd in range(GLA_HEADS):
            vrow = v[i:i + 1, hd * GLA_DV:(hd + 1) * GLA_DV]
            s_new = dc_ref[hd][:, i:i + 1] * st_ref[i, hd] + kc_ref[hd][:, i:i + 1] * vrow
            nst_ref[i, hd] = s_new
            o_ref[i:i + 1, hd * GLA_DV:(hd + 1) * GLA_DV] = jnp.sum(
                qc_ref[hd][:, i:i + 1] * s_new, axis=0, keepdims=True)


def _gla1_out_body(x_ref, o_ref, go_ref, mod_ref, ng_ref, gng_ref, wout_ref, rw_ref, x1_ref, h_ref, lg_ref):
    _, _, g1, sh2, sc2, _ = _mod_slices(mod_ref)
    ng = ng_ref[...]
    y = jnp.dot(_gla_out(o_ref[...], go_ref[...], gng_ref[...]), wout_ref[...], preferred_element_type=F32)
    x1 = x_ref[...] + g1 * _rms(y, ng[1:2])
    x1_ref[...] = x1
    _ffn_prep(x1, ng, sh2, sc2, rw_ref, h_ref, lg_ref)


def _gla_mixer_one(x2d, mod2, state, ng, wqkvg, wlr, wgk, bgk, gng, wout, rw_t):
    n = x2d.shape[0]
    consts = (ng, wqkvg, wlr, wgk, bgk)
    q, k, v, go, dec = pl.pallas_call(
        _gla1_proj_body,
        in_specs=[_const_spec(a.shape) for a in (x2d, mod2) + consts],
        out_specs=[_const_spec((n, GLA_DK_TOT)), _const_spec((n, GLA_DK_TOT)), _const_spec((n, GLA_DV_TOT)),
                   _const_spec((n, GLA_DV_TOT)), _const_spec((n, GLA_DK_TOT))],
        out_shape=[jax.ShapeDtypeStruct((n, GLA_DK_TOT), F32), jax.ShapeDtypeStruct((n, GLA_DK_TOT), F32),
                   jax.ShapeDtypeStruct((n, GLA_DV_TOT), F32), jax.ShapeDtypeStruct((n, GLA_DV_TOT), F32),
                   jax.ShapeDtypeStruct((n, GLA_DK_TOT), F32)],
        grid=(1,),
        compiler_params=_cparams("arbitrary"),
        name="gla1_proj",
    )(x2d, mod2, *consts)

    def cols(a):
        return a.reshape(n // GLA1_TOK, GLA1_TOK, GLA_HEADS, GLA_DK).transpose(0, 2, 3, 1)

    col_spec = pl.BlockSpec((None, GLA_HEADS, GLA_DK, GLA1_TOK), lambda i: (i, 0, 0, 0))
    st_spec = pl.BlockSpec((GLA1_TOK, GLA_HEADS, GLA_DK, GLA_DV), lambda i: (i, 0, 0, 0))
    new_state, o = pl.pallas_call(
        _gla1_state_body,
        grid=(n // GLA1_TOK,),
        in_specs=[st_spec, col_spec, col_spec, col_spec, pl.BlockSpec((GLA1_TOK, GLA_DV_TOT), lambda i: (i, 0))],
        out_specs=[st_spec, pl.BlockSpec((GLA1_TOK, GLA_DV_TOT), lambda i: (i, 0))],
        out_shape=[jax.ShapeDtypeStruct(state.shape, F32), jax.ShapeDtypeStruct((n, GLA_DV_TOT), F32)],
        compiler_params=_cparams("parallel"),
        name="gla1_state",
    )(state, cols(q), cols(k), cols(dec), v)

    consts = (mod2, ng, gng, wout, rw_t)
    x1, h, lg = pl.pallas_call(
        _gla1_out_body,
        grid=(1,),
        in_specs=[_const_spec(a.shape) for a in (x2d, o, go) + consts],
        out_specs=[_const_spec((n, D)), _const_spec((n, DP)), _const_spec((N_EXPERTS, n))],
        out_shape=[jax.ShapeDtypeStruct((n, D), F32), jax.ShapeDtypeStruct((n, DP), jnp.int32),
                   jax.ShapeDtypeStruct((N_EXPERTS, n), F32)],
        compiler_params=_cparams("arbitrary"),
        name="gla1_out",
    )(x2d, o, go, *consts)
    return x1, h, lg, new_state


def _moe_routed(h_p, h_s, lg_p, lg_s, router_bias, layer, wg, wu, wd):
    h, lg = h_p, lg_p
    if h_s is not None:
        h = jnp.concatenate([h_p, h_s], axis=0)
        lg = jnp.concatenate([lg_p, lg_s], axis=1)
    n = h.shape[0]
    n_pad = -(-n // TOKEN_PAD) * TOKEN_PAD
    if n_pad != n:
        h = jnp.pad(h, ((0, n_pad - n), (0, 0)))
        lg = jnp.pad(lg, ((0, 0), (0, n_pad - n)))
    eid, rank, wts, counts = _router(lg, router_bias, n)
    tile_count = ((counts[:, 0] + EXPERT_TILE - 1) // EXPERT_TILE).astype(jnp.int32)
    tile_end = jnp.cumsum(tile_count).astype(jnp.int32)
    tile_first = tile_end - tile_count
    off = tile_first * EXPERT_TILE
    p_alloc = TOP_K * n_pad + N_EXPERTS * EXPERT_TILE
    dest = _dest(off, eid, rank, n, p_alloc - 1)
    dest_w = dest.reshape(TOP_K, n_pad // DISPATCH_W, DISPATCH_W).transpose(1, 0, 2)
    xs = _sc_dispatch(h, dest_w, p_alloc)
    ys = _experts(xs, tile_first, tile_count, tile_end[-1:], layer, wg, wu, wd)
    dest_tm = dest.T.reshape(n_pad // SUM_W, SUM_PARTS, SUM_W * TOP_K // SUM_PARTS)
    w_lanes = jnp.repeat(wts.T, SC_LANES, axis=1)
    return _sc_gather_sum(ys, dest_tm, w_lanes), h


def kernel(x_prompt, x_sample, state_gla, c_prompt, c_sample, norm_g, ada_w, ada_b, gm_w_in, gm_b_in,
           gm_ln_g, gm_ln_b, gm_w_s, gm_b_s, gm_w_out, gla_w_in, gla_w_gk, gla_b_gk, gla_norm_g,
           gla_w_out, router_w, router_bias, exp_w_gate, exp_w_up, exp_w_down, sh_w_gate, sh_w_up,
           sh_w_down):
    batch, seq, _ = x_prompt.shape
    n_s = x_sample.shape[0]
    n_p = batch * seq
    tpb = seq // MIX_TILE
    xp = x_prompt.reshape(n_p, D)
    xs = x_sample.reshape(n_s, D)

    mod = _ada(jnp.concatenate([c_prompt, c_sample], axis=0), ada_w, ada_b)
    mod_p = [mod[i, :batch].reshape(batch, 1, 6 * D) for i in range(2)]
    mod_s = [mod[i, batch:] for i in range(2)]
    rw_t = [jnp.concatenate(_split3(router_w[i].T), axis=0) for i in range(2)]

    ws_causal = jnp.tril(gm_w_s[0]).astype(BF16)
    bs_cols = gm_b_s[0].T
    eye = jnp.eye(GM_CHUNK, dtype=F32)
    ws_first = (gm_w_s[0][:, 0, 0][:, None, None] * eye).astype(BF16)
    bs_first = jnp.broadcast_to(gm_b_s[0][:, 0][None, :], (GM_CHUNK, GM_GROUPS))
    gm_args = (norm_g[0], gm_w_in[0].astype(BF16), gm_b_in[0].reshape(1, -1), gm_ln_g[0].reshape(1, -1),
               gm_ln_b[0].reshape(1, -1))
    wout0 = gm_w_out[0].astype(BF16)
    shared = [(sh_w_gate[i].astype(BF16), sh_w_up[i].astype(BF16), sh_w_down[i].astype(BF16))
              for i in range(2)]
    n_qkvg = 2 * GLA_DK_TOT + 2 * GLA_DV_TOT
    wqkvg = gla_w_in[0][:, :n_qkvg].astype(BF16)
    wlr = jnp.pad(gla_w_in[0][:, n_qkvg:], ((0, 0), (0, LANES - GLA_GATE_RANK))).astype(BF16)
    wgk = jnp.pad(gla_w_gk[0], ((0, LANES - GLA_GATE_RANK), (0, 0))).astype(BF16)
    gla_args = (norm_g[1], wqkvg, wlr, wgk, gla_b_gk[0].reshape(1, -1), gla_norm_g[0].reshape(1, -1),
                gla_w_out[0].astype(BF16), rw_t[1])
    experts = (exp_w_gate, exp_w_up, exp_w_down)

    half = batch // 2
    streams = [(0, half, False), (half, batch - half, True)]
    st = [dict() for _ in streams]

    for s, (b0, nb, with_new) in zip(st, streams):
        s["mod_p"] = [mod_p[i][b0:b0 + nb] for i in range(2)]
        s["n"] = nb * seq
        s["x1p"], s["hp"], s["lgp"] = _gmlp_mixer(xp, b0 * tpb, s["n"], s["mod_p"][0], False, MIX_TILE, tpb,
                                                  *gm_args, ws_causal, bs_cols, wout0, rw_t[0], emit_v=False)
        s["hs"] = s["lgs"] = None
        if with_new:
            s["x1s"], s["hs"], s["lgs"], v_rows = _gmlp_mixer(xs, 0, n_s, mod_s[0], True, n_s, 1, *gm_args,
                                                              ws_first, bs_first, wout0, rw_t[0], emit_v=True)
    for s, (b0, nb, with_new) in zip(st, streams):
        moe = s["moe0"] = _moe_routed(s["hp"], s["hs"], s["lgp"], s["lgs"], router_bias[0], 0, *experts)
        if with_new:
            s["x2s"] = _combine(s["x1s"], *moe, s["n"] // n_s, mod_s[0], True, n_s, 1, norm_g[0], *shared[0])
    for s, (b0, nb, with_new) in zip(st, streams):
        s["x3p"], s["hp"], s["lgp"], s["st_p"] = _gla_mixer(s["x1p"], *s["moe0"], s["mod_p"][0], norm_g[0],
                                                            shared[0], s["mod_p"][1], nb, seq, *gla_args)
        if with_new:
            s["x3s"], s["hs"], s["lgs"], st_s = _gla_mixer_one(s["x2s"], mod_s[1], state_gla[:, 0], *gla_args)
    y_prompt = None
    for s, (b0, nb, with_new) in zip(st, streams):
        moe = _moe_routed(s["hp"], s["hs"], s["lgp"], s["lgs"], router_bias[1], 1, *experts)
        y_prompt = _combine(s["x3p"], *moe, 0, s["mod_p"][1], False, MIX_TILE, tpb, norm_g[1], *shared[1],
                            out_rows=n_p, out_blk0=b0 * tpb, out_buf=y_prompt)
        if with_new:
            y_new = _combine(s["x3s"], *moe, s["n"] // n_s, mod_s[1], True, n_s, 1, norm_g[1], *shared[1])
    st_p = jnp.concatenate([s["st_p"] for s in st], axis=0)

    return (y_prompt.reshape(batch, seq, D), y_new.reshape(n_s, 1, D), st_p[:, None], st_s[:, None],
            v_rows.reshape(n_s, 1, 1, GM_HALF))
```

```python
import functools
import math

import jax
import jax.numpy as jnp
from jax import lax
from jax.experimental import pallas as pl
from jax.experimental.pallas import tpu as pltpu
from jax.experimental.pallas import tpu_sc as plsc

F32 = jnp.float32
BF16 = jnp.bfloat16

D = 1024
DP = D // 2
GM_CHUNK = 128
GM_HALF = 2 * D
GM_GROUPS = 8
GM_GROUP_DIM = GM_HALF // GM_GROUPS
GLA_HEADS = 4
GLA_DK = 128
GLA_DV = 256
GLA_DK_TOT = GLA_HEADS * GLA_DK
GLA_DV_TOT = GLA_HEADS * GLA_DV
GLA_GATE_RANK = 16
GLA_GATE_NORMALIZER = 16.0
GLA_CHUNK = 64
N_EXPERTS = 64
TOP_K = 8
N_EXPERT_GROUPS = 8
GROUP_SIZE = N_EXPERTS // N_EXPERT_GROUPS
TOPK_GROUPS = 4
EXPERT_DIM = D // 4
ROUTED_SCALE = 2.5
NORM_EPS = 1e-6
LN_EPS = 1e-5

LANES = 128
VMEM_LIMIT = 56 * 1024 * 1024

MIX_TILE = 256
GLA_TILE = 512
GM_COL_BLOCK = 512
ROUTER_TILE = 1024
EXPERT_TILE = 272
EXPERT_X_SLOTS = 6
EXPERT_AHEAD = EXPERT_X_SLOTS - 2
EXPERT_Y_SLOTS = 4
SC_WORKERS = 32
DISPATCH_W = 32
SC_LANES = 16
SUM_W = 16
SUM_PARTS = 4
SUM_UNROLL = 4
TOKEN_PAD = SC_WORKERS * DISPATCH_W


def _cparams(*sem):
    return pltpu.CompilerParams(dimension_semantics=sem, vmem_limit_bytes=VMEM_LIMIT)


def _rms(x, g):
    return x * lax.rsqrt(jnp.mean(x * x, axis=-1, keepdims=True) + NORM_EPS) * g


def _silu(x):
    return x * (1.0 / (1.0 + jnp.exp(-x)))


def _gelu(x):
    return 0.5 * x * (1.0 + lax.erf(x * (1.0 / math.sqrt(2.0))))


def _bdot(a, b):
    return jnp.dot(a.astype(BF16), b.astype(BF16), preferred_element_type=F32)


def _dot_nt(a, b, precision=None):
    return lax.dot_general(a, b, (((1,), (1,)), ((), ())), preferred_element_type=F32,
                           precision=precision)


def _mod_slices(mod_ref):
    return [mod_ref[:, i * D:(i + 1) * D] for i in range(6)]


HI_HALF = -65536


def _pack_rows(x):
    lo = lax.bitcast_convert_type(x[:, :DP].astype(BF16).astype(F32), jnp.int32)
    hi = lax.bitcast_convert_type(x[:, DP:].astype(BF16).astype(F32), jnp.int32)
    return lax.shift_right_logical(lo, 16) | (hi & HI_HALF)


def _unpack_rows(p):
    lo = lax.bitcast_convert_type(lax.shift_left(p, 16), F32)
    hi = lax.bitcast_convert_type(p & HI_HALF, F32)
    return lo, hi


def _ffn_prep(x1, ng, sh2, sc2, rw_ref, h_ref, lg_ref):
    hffn = _rms(x1, ng[2:3]) * (1.0 + sc2) + sh2
    h_ref[...] = _pack_rows(hffn)
    lg3 = _dot_nt(rw_ref[...], hffn.astype(BF16))
    lg_ref[...] = lg3[:N_EXPERTS] + lg3[N_EXPERTS:2 * N_EXPERTS] + lg3[2 * N_EXPERTS:]


def _ada_body(c_ref, w_ref, b_ref, o_ref):
    c = c_ref[...]
    o_ref[...] = _bdot(_silu(c), w_ref[...]) + b_ref[...]


def _ada(c, ada_w, ada_b):
    n = c.shape[0]
    depth = ada_w.shape[0]
    tn = 1536
    return pl.pallas_call(
        _ada_body,
        grid=(depth, 6 * D // tn),
        in_specs=[pl.BlockSpec((n, D), lambda l, j: (0, 0)),
                  pl.BlockSpec((None, D, tn), lambda l, j: (l, 0, j)),
                  pl.BlockSpec((None, 1, tn), lambda l, j: (l, 0, j))],
        out_specs=pl.BlockSpec((None, n, tn), lambda l, j: (l, 0, j)),
        out_shape=jax.ShapeDtypeStruct((depth, n, 6 * D), F32),
        compiler_params=_cparams("parallel", "parallel"),
        name="ada_mod",
    )(c, ada_w, ada_b.reshape(depth, 1, 6 * D))


def _mod_spec(per_row, tt, tiles_per_batch):
    if per_row:
        return pl.BlockSpec((tt, 6 * D), lambda i: (i, 0))
    return pl.BlockSpec((None, 1, 6 * D), lambda i: (i // tiles_per_batch, 0, 0))


def _const_spec(shape):
    zeros = (0,) * len(shape)
    return pl.BlockSpec(shape, lambda *_: zeros)


def _gmlp_body(x_ref, mod_ref, ng_ref, win_ref, bin_ref, lng_ref, lnb_ref, ws_ref, bs_ref, wout_ref,
               rw_ref, *rest, n_chunks, emit_v, cast_w):
    rest = list(rest)
    w32_refs = [rest.pop(0) for _ in range(3)] if cast_w else []
    x1_ref, h_ref, lg_ref = rest[:3]
    rest = rest[3:]
    v_ref = rest.pop(0) if emit_v else None
    w16_refs = [rest.pop(0) for _ in range(3)] if cast_w else []
    um_ref, z_ref = rest
    for src, dst in zip(w32_refs, w16_refs):
        dst[...] = src[...].astype(BF16)
    sh1, sc1, g1, sh2, sc2, _ = _mod_slices(mod_ref)
    ng = ng_ref[...]
    x = x_ref[...]
    hb = (_rms(x, ng[0:1]) * (1.0 + sc1) + sh1).astype(BF16)
    for cb in range(2 * GM_HALF // GM_COL_BLOCK):
        cols = slice(cb * GM_COL_BLOCK, (cb + 1) * GM_COL_BLOCK)
        z_ref[:, cols] = _gelu(jnp.dot(hb, win_ref[:, cols], preferred_element_type=F32) + bin_ref[:, cols])
    u = z_ref[:, :GM_HALF]
    v = z_ref[:, GM_HALF:]
    mu = jnp.mean(v, axis=-1, keepdims=True)
    vc = v - mu
    var = jnp.mean(vc * vc, axis=-1, keepdims=True)
    v = vc * lax.rsqrt(var + LN_EPS) * lng_ref[...] + lnb_ref[...]
    if emit_v:
        v_ref[...] = v
    vb = v.astype(BF16)
    for c in range(n_chunks):
        rows = slice(c * GM_CHUNK, (c + 1) * GM_CHUNK)
        for g in range(GM_GROUPS):
            cols = slice(g * GM_GROUP_DIM, (g + 1) * GM_GROUP_DIM)
            mixed = jnp.dot(ws_ref[g], vb[rows, cols], preferred_element_type=F32) + bs_ref[:, g:g + 1]
            um_ref[rows, cols] = (u[rows, cols] * mixed).astype(BF16)
    y = jnp.dot(um_ref[...], wout_ref[...], preferred_element_type=F32)
    x1 = x + g1 * _rms(y, ng[1:2])
    x1_ref[...] = x1
    _ffn_prep(x1, ng, sh2, sc2, rw_ref, h_ref, lg_ref)


def _gmlp_mixer(x2d, blk0, n, mod, per_row, tt, tiles_per_batch, ng, win, b_in, ln_g, ln_b, ws, bs, wout,
                rw_t, emit_v, cast_w=None):
    steps = n // tt
    out_shape = [jax.ShapeDtypeStruct((n, D), F32), jax.ShapeDtypeStruct((n, DP), jnp.int32),
                 jax.ShapeDtypeStruct((N_EXPERTS, n), F32)]
    out_specs = [pl.BlockSpec((tt, D), lambda i: (i, 0)), pl.BlockSpec((tt, DP), lambda i: (i, 0)),
                 pl.BlockSpec((N_EXPERTS, tt), lambda i: (0, i))]
    if emit_v:
        out_shape.append(jax.ShapeDtypeStruct((n, GM_HALF), F32))
        out_specs.append(pl.BlockSpec((tt, GM_HALF), lambda i: (i, 0)))

    def one_buffer(a):
        zeros = (0,) * a.ndim
        return pl.BlockSpec(a.shape, lambda *_: zeros, pipeline_mode=pl.Buffered(1))

    consts = (ng, win, b_in, ln_g, ln_b, ws, bs, wout, rw_t)
    in_specs = [pl.BlockSpec((tt, D), lambda i: (i + blk0, 0)), _mod_spec(per_row, tt, tiles_per_batch)]
    in_specs += [one_buffer(a) for a in consts]
    args = [x2d, mod, *consts]
    if cast_w is not None:
        layer, *w_all = cast_w
        per_step = N_EXPERTS // steps
        for w in w_all:
            blk = (None, per_step) + w.shape[2:]
            in_specs.append(pl.BlockSpec(blk, lambda i: (layer, i, 0, 0)))
            out_specs.append(pl.BlockSpec(blk[1:], lambda i: (i, 0, 0)))
            out_shape.append(jax.ShapeDtypeStruct(w.shape[1:], BF16))
            args.append(w)
    return pl.pallas_call(
        functools.partial(_gmlp_body, n_chunks=tt // GM_CHUNK, emit_v=emit_v, cast_w=cast_w is not None),
        grid=(steps,),
        in_specs=in_specs,
        out_specs=out_specs,
        out_shape=out_shape,
        scratch_shapes=[pltpu.VMEM((tt, GM_HALF), BF16), pltpu.VMEM((tt, 2 * GM_HALF), F32)],
        compiler_params=_cparams("parallel"),
        name="gmlp_mixer_rows" if per_row else "gmlp_mixer",
    )(*args)


def _combine_body(x_ref, y_ref, h_ref, mod_ref, ng_ref, swg_ref, swu_ref, swd_ref, *rest):
    o_ref = rest[-1]
    o_ref[...] = _channel_mix_residual(x_ref[...], y_ref[...], h_ref, mod_ref[:, 5 * D:6 * D],
                                       ng_ref[3:4, :], swg_ref, swu_ref, swd_ref)


def _combine(x2d, routed, hp, blk0, mod, per_row, tt, tiles_per_batch, ng, swg, swu, swd,
             out_rows=None, out_blk0=0, out_buf=None):
    n = x2d.shape[0]
    in_specs = [pl.BlockSpec((tt, D), lambda i: (i, 0)),
                pl.BlockSpec((tt, D), lambda i: (i + blk0, 0)),
                pl.BlockSpec((tt, DP), lambda i: (i + blk0, 0)),
                _mod_spec(per_row, tt, tiles_per_batch),
                _const_spec(ng.shape), _const_spec(swg.shape), _const_spec(swu.shape),
                _const_spec(swd.shape)]
    args = [x2d, routed, hp, mod, ng, swg, swu, swd]
    aliases = {}
    if out_buf is not None:
        in_specs.append(pl.BlockSpec(memory_space=pl.ANY))
        aliases = {len(args): 0}
        args.append(out_buf)
    return pl.pallas_call(
        _combine_body,
        grid=(n // tt,),
        in_specs=in_specs,
        out_specs=pl.BlockSpec((tt, D), lambda i: (i + out_blk0, 0)),
        out_shape=jax.ShapeDtypeStruct((out_rows or n, D), F32),
        input_output_aliases=aliases,
        compiler_params=_cparams("parallel"),
        name="combine_rows" if per_row else "combine",
    )(*args)


def _router_body(lg_ref, bias_ref, tri_ref, eid_ref, rank_ref, wts_ref, cnt_ref, carry_ref, *, n_real):
    step = pl.program_id(0)

    @pl.when(step == 0)
    def _():
        carry_ref[...] = jnp.zeros_like(carry_ref)

    lg = lg_ref[...]
    tn = lg.shape[1]
    real = (step * tn + lax.broadcasted_iota(jnp.int32, (1, tn), 1)) < n_real
    lg = jnp.where(real, lg, 0.0)
    scores = 1.0 / (1.0 + jnp.exp(-lg))
    sel = scores + bias_ref[...]
    neg = -jnp.inf
    sub8 = lax.broadcasted_iota(jnp.int32, (GROUP_SIZE, tn), 0)
    gsub = lax.broadcasted_iota(jnp.int32, (N_EXPERT_GROUPS, tn), 0)
    gs = jnp.zeros((N_EXPERT_GROUPS, tn), F32)
    for g in range(N_EXPERT_GROUPS):
        blk = sel[g * GROUP_SIZE:(g + 1) * GROUP_SIZE, :]
        m1 = jnp.max(blk, axis=0, keepdims=True)
        i1 = jnp.min(jnp.where(blk == m1, sub8, GROUP_SIZE), axis=0, keepdims=True)
        m2 = jnp.max(jnp.where(sub8 == i1, neg, blk), axis=0, keepdims=True)
        gs = jnp.where(gsub == g, m1 + m2, gs)
    gmask = jnp.zeros((N_EXPERT_GROUPS, tn), jnp.bool_)
    for _ in range(TOPK_GROUPS):
        m = jnp.max(gs, axis=0, keepdims=True)
        i = jnp.min(jnp.where(gs == m, gsub, N_EXPERT_GROUPS), axis=0, keepdims=True)
        hit = gsub == i
        gmask = jnp.logical_or(gmask, hit)
        gs = jnp.where(hit, neg, gs)
    gmaskf = gmask.astype(F32)
    blocks = []
    for g in range(N_EXPERT_GROUPS):
        keep = jnp.broadcast_to(gmaskf[g:g + 1, :], (GROUP_SIZE, tn)) > 0.5
        blocks.append(jnp.where(keep, sel[g * GROUP_SIZE:(g + 1) * GROUP_SIZE, :], neg))
    msel = jnp.concatenate(blocks, axis=0)
    esub = lax.broadcasted_iota(jnp.int32, (N_EXPERTS, tn), 0)
    chosen = jnp.zeros((N_EXPERTS, tn), jnp.bool_)
    picks = []
    for _ in range(TOP_K):
        m = jnp.max(msel, axis=0, keepdims=True)
        i = jnp.min(jnp.where(msel == m, esub, N_EXPERTS), axis=0, keepdims=True)
        hit = esub == i
        picks.append(i)
        chosen = jnp.logical_or(chosen, hit)
        msel = jnp.where(hit, neg, msel)
    w = jnp.where(chosen, scores, 0.0)
    w = w / jnp.sum(w, axis=0, keepdims=True) * ROUTED_SCALE
    counted = jnp.where(jnp.logical_and(chosen, real), 1.0, 0.0)
    incl = jnp.dot(counted.astype(BF16), tri_ref[...], preferred_element_type=F32)
    rank_full = carry_ref[:, 0:1] + incl - 1.0
    ksub = lax.broadcasted_iota(jnp.int32, (TOP_K, tn), 0)
    eid = jnp.zeros((TOP_K, tn), jnp.int32)
    rank = jnp.zeros((TOP_K, tn), F32)
    wts = jnp.zeros((TOP_K, tn), F32)
    for k in range(TOP_K):
        hit = esub == picks[k]
        eid = jnp.where(ksub == k, picks[k], eid)
        rank = jnp.where(ksub == k, jnp.sum(jnp.where(hit, rank_full, 0.0), axis=0, keepdims=True), rank)
        wts = jnp.where(ksub == k, jnp.sum(jnp.where(hit, w, 0.0), axis=0, keepdims=True), wts)
    eid_ref[...] = eid
    rank_ref[...] = rank.astype(jnp.int32)
    wts_ref[...] = wts
    carry = carry_ref[...] + incl[:, tn - 1:tn]
    carry_ref[...] = carry
    cnt_ref[...] = carry.astype(jnp.int32)


def _router(lg_t, bias, n_real):
    n = lg_t.shape[1]
    tn = ROUTER_TILE
    idx = jnp.arange(tn)
    tri = (idx[:, None] <= idx[None, :]).astype(BF16)
    kspec = pl.BlockSpec((TOP_K, tn), lambda i: (0, i))
    return pl.pallas_call(
        functools.partial(_router_body, n_real=n_real),
        grid=(n // tn,),
        in_specs=[pl.BlockSpec((N_EXPERTS, tn), lambda i: (0, i)), _const_spec((N_EXPERTS, 1)),
                  _const_spec((tn, tn))],
        out_specs=[kspec, kspec, kspec, _const_spec((N_EXPERTS, LANES))],
        out_shape=[jax.ShapeDtypeStruct((TOP_K, n), jnp.int32), jax.ShapeDtypeStruct((TOP_K, n), jnp.int32),
                   jax.ShapeDtypeStruct((TOP_K, n), F32), jax.ShapeDtypeStruct((N_EXPERTS, LANES), jnp.int32)],
        scratch_shapes=[pltpu.VMEM((N_EXPERTS, LANES), F32)],
        compiler_params=_cparams("arbitrary"),
        name="router",
    )(lg_t, bias.reshape(N_EXPERTS, 1), tri)


def _dest_body(off_ref, eid_ref, rank_ref, dest_ref, *, n_real, last_row):
    eid = eid_ref[...]
    base = jnp.zeros(eid.shape, jnp.int32)
    for e in range(N_EXPERTS):
        base = jnp.where(eid == e, off_ref[e], base)
    tok = lax.broadcasted_iota(jnp.int32, eid.shape, 1)
    slot = lax.broadcasted_iota(jnp.int32, eid.shape, 0)
    unused = last_row - ((tok - n_real) * TOP_K + slot)
    dest_ref[...] = jnp.where(tok < n_real, base + rank_ref[...], unused)


def _dest(off, eid, rank, n_real, last_row):
    spec = pl.BlockSpec(eid.shape, lambda i, off_ref: (0, 0))
    return pl.pallas_call(
        functools.partial(_dest_body, n_real=n_real, last_row=last_row),
        grid_spec=pltpu.PrefetchScalarGridSpec(num_scalar_prefetch=1, grid=(1,), in_specs=[spec, spec],
                                               out_specs=spec),
        out_shape=jax.ShapeDtypeStruct(eid.shape, jnp.int32),
        compiler_params=_cparams("arbitrary"),
        name="dest_rows",
    )(off, eid, rank)


def _sc_mesh():
    return plsc.VectorSubcoreMesh(core_axis_name="core", subcore_axis_name="subcore")


def _sc_dispatch(hp, dest_w, p_alloc):
    n = hp.shape[0]
    w = dest_w.shape[2]

    @functools.partial(pl.kernel, out_type=jax.ShapeDtypeStruct((p_alloc, DP), jnp.int32), mesh=_sc_mesh(),
                       name="sc_dispatch")
    def run(hp_hbm, dest_hbm, xs_hbm):
        def body(x_vmem, i_vmem):
            for k in range(TOP_K):
                pltpu.sync_copy(x_vmem, xs_hbm.at[i_vmem.at[k]])

        pltpu.emit_pipeline(
            body,
            grid=(n // w,),
            in_specs=[pl.BlockSpec((w, DP), lambda i: (i, 0)),
                      pl.BlockSpec((None, TOP_K, w), lambda i: (i, 0, 0))],
            out_specs=[],
            core_axis_name=("core", "subcore"),
            dimension_semantics=(pltpu.PARALLEL,),
        )(hp_hbm, dest_hbm)

    return run(hp, dest_w)


def _sc_gather_sum(ys, dest_tm, w_lanes):
    n_win, parts, pk = dest_tm.shape
    w = parts * pk // TOP_K
    wp = w // parts
    n_vec = DP // SC_LANES

    @functools.partial(pl.kernel, out_type=jax.ShapeDtypeStruct((n_win * w, D), F32), mesh=_sc_mesh(),
                       scratch_types=[pltpu.VMEM((parts, pk, DP), jnp.int32), pltpu.SemaphoreType.DMA((parts,))],
                       compiler_params=pltpu.CompilerParams(needs_layout_passes=False), name="sc_gather_sum")
    def run(ys_hbm, dest_hbm, w_hbm, o_hbm, rows_v, sems):
        def body(i_vmem, w_vmem, o_vmem):
            copies = [pltpu.async_copy(ys_hbm.at[i_vmem.at[p]], rows_v.at[p], sems.at[p]) for p in range(parts)]
            for p in range(parts):
                copies[p].wait()

                @pl.loop(0, wp)
                def _(t):
                    tok = p * wp + t
                    wv = [w_vmem[tok, pl.ds(k * SC_LANES, SC_LANES)] for k in range(TOP_K)]

                    @pl.loop(0, n_vec, step=SUM_UNROLL)
                    def _(j0):
                        for u in range(SUM_UNROLL):
                            col = (j0 + u) * SC_LANES
                            lo = jnp.zeros((SC_LANES,), F32)
                            hi = jnp.zeros((SC_LANES,), F32)
                            for k in range(TOP_K):
                                word = rows_v[p, t * TOP_K + k, pl.ds(col, SC_LANES)]
                                lo = lo + wv[k] * plsc.bitcast(lax.shift_left(word, 16), F32)
                                hi = hi + wv[k] * plsc.bitcast(word & HI_HALF, F32)
                            o_vmem[tok, pl.ds(col, SC_LANES)] = lo
                            o_vmem[tok, pl.ds(DP + col, SC_LANES)] = hi

        pltpu.emit_pipeline(
            body,
            grid=(n_win,),
            in_specs=[pl.BlockSpec((None, parts, pk), lambda i: (i, 0, 0)),
                      pl.BlockSpec((w, TOP_K * SC_LANES), lambda i: (i, 0))],
            out_specs=[pl.BlockSpec((w, D), lambda i: (i, 0))],
            core_axis_name=("core", "subcore"),
            dimension_semantics=(pltpu.PARALLEL,),
        )(dest_hbm, w_hbm, o_hbm)

    return run(ys, dest_tm, w_lanes)


def _expert_body(first_ref, cnt_ref, nused_ref, xs_hbm, wg_s, wu_s, wd_s, ys_hbm, xbuf, ybuf, xsem, ysem):
    e = pl.program_id(0)
    n_used = nused_ref[0]

    def load(g):
        rows = pl.ds(pl.multiple_of(g * EXPERT_TILE, EXPERT_TILE), EXPERT_TILE)
        slot = g % EXPERT_X_SLOTS
        return pltpu.make_async_copy(xs_hbm.at[rows], xbuf.at[slot], xsem.at[slot])

    def store(g):
        rows = pl.ds(pl.multiple_of(g * EXPERT_TILE, EXPERT_TILE), EXPERT_TILE)
        slot = g % EXPERT_Y_SLOTS
        return pltpu.make_async_copy(ybuf.at[slot], ys_hbm.at[rows], ysem.at[slot])

    @pl.when(e == 0)
    def _():
        for g in range(EXPERT_AHEAD):
            @pl.when(g < n_used)
            def _():
                load(g).start()

    first = first_ref[e]
    cnt = cnt_ref[e]

    def acquire(g):
        ahead = g + EXPERT_AHEAD

        @pl.when(ahead < n_used)
        def _():
            load(ahead).start()

        load(g).wait()

        @pl.when(g >= EXPERT_Y_SLOTS)
        def _():
            store(g - EXPERT_Y_SLOTS).wait()

    def compute(g):
        lo, hi = _unpack_rows(xbuf[g % EXPERT_X_SLOTS])
        lo = lo.astype(BF16)
        hi = hi.astype(BF16)

        def xdot(w_s):
            return (jnp.dot(lo, w_s[:DP, :], preferred_element_type=F32) +
                    jnp.dot(hi, w_s[DP:, :], preferred_element_type=F32))

        a = (_silu(xdot(wg_s)) * xdot(wu_s)).astype(BF16)
        ybuf[g % EXPERT_Y_SLOTS] = _pack_rows(jnp.dot(a, wd_s[...], preferred_element_type=F32))

    def pair(j, carry):
        g = first + 2 * j
        acquire(g)
        acquire(g + 1)
        compute(g)
        compute(g + 1)
        store(g).start()
        store(g + 1).start()
        return carry

    lax.fori_loop(0, cnt // 2, pair, 0)

    @pl.when(cnt % 2 == 1)
    def _():
        g = first + cnt - 1
        acquire(g)
        compute(g)
        store(g).start()

    @pl.when(e == N_EXPERTS - 1)
    def _():
        for k in range(EXPERT_Y_SLOTS):
            g = n_used - 1 - k

            @pl.when(g >= 0)
            def _():
                store(g).wait()


def _experts(xs, tile_first, tile_count, n_used, wg, wu, wd):
    def w_map(e, first, cnt, nu):
        return (e, 0, 0)

    return pl.pallas_call(
        _expert_body,
        grid_spec=pltpu.PrefetchScalarGridSpec(
            num_scalar_prefetch=3, grid=(N_EXPERTS,),
            in_specs=[pl.BlockSpec(memory_space=pl.ANY),
                      pl.BlockSpec((None, D, EXPERT_DIM), w_map),
                      pl.BlockSpec((None, D, EXPERT_DIM), w_map),
                      pl.BlockSpec((None, EXPERT_DIM, D), w_map)],
            out_specs=pl.BlockSpec(memory_space=pl.ANY),
            scratch_shapes=[pltpu.VMEM((EXPERT_X_SLOTS, EXPERT_TILE, DP), jnp.int32),
                            pltpu.VMEM((EXPERT_Y_SLOTS, EXPERT_TILE, DP), jnp.int32),
                            pltpu.SemaphoreType.DMA((EXPERT_X_SLOTS,)),
                            pltpu.SemaphoreType.DMA((EXPERT_Y_SLOTS,))]),
        out_shape=jax.ShapeDtypeStruct(xs.shape, jnp.int32),
        compiler_params=_cparams("arbitrary"),
        name="experts",
    )(tile_first, tile_count, n_used, xs, wg, wu, wd)


def _log_sigmoid(z):
    return jnp.minimum(z, 0.0) - jnp.log(1.0 + jnp.exp(-jnp.abs(z)))


def _gla_gate(hb, wlr_ref, wgk_ref, bgk_ref):
    lr = jnp.dot(hb, wlr_ref[...], preferred_element_type=F32)
    z = _bdot(lr, wgk_ref[...]) + bgk_ref[...]
    return _log_sigmoid(z) * (1.0 / GLA_GATE_NORMALIZER)


def _split3(a):
    hi = a.astype(BF16)
    r1 = a - hi.astype(F32)
    mid = r1.astype(BF16)
    lo = (r1 - mid.astype(F32)).astype(BF16)
    return hi, mid, lo


def _gla_out(o_ref_val, go, gng):
    parts = []
    for hd in range(GLA_HEADS):
        cols = slice(hd * GLA_DV, (hd + 1) * GLA_DV)
        parts.append((_rms(o_ref_val[:, cols], gng) * _silu(go[:, cols])).astype(BF16))
    return jnp.concatenate(parts, axis=1)


def _channel_mix_residual(x, routed, h_ref, g2, ng3, swg_ref, swu_ref, swd_ref):
    h_lo, h_hi = _unpack_rows(h_ref[...])
    h_lo = h_lo.astype(BF16)
    h_hi = h_hi.astype(BF16)

    def hdot(w_ref_):
        return (jnp.dot(h_lo, w_ref_[:DP, :], preferred_element_type=F32) +
                jnp.dot(h_hi, w_ref_[DP:, :], preferred_element_type=F32))

    hs = (_silu(hdot(swg_ref)) * hdot(swu_ref)).astype(BF16)
    y = jnp.dot(hs, swd_ref[...], preferred_element_type=F32) + routed
    return x + g2 * _rms(y, ng3)


def _gla_body(x_ref, y_ref, hprev_ref, modprev_ref, ngprev_ref, swg_ref, swu_ref, swd_ref,
              mod_ref, ng_ref, wqkvg_ref, wlr_ref, wgk_ref, bgk_ref, tril_ref, gng_ref, wout_ref,
              rw_ref, x1_ref, h_ref, lg_ref, st_ref, st_scr, o_scr, qd_scr, dst_scr, *, tt):
    j = pl.program_id(1)

    @pl.when(j == 0)
    def _():
        st_scr[...] = jnp.zeros_like(st_scr)

    x = _channel_mix_residual(x_ref[...], y_ref[...], hprev_ref, modprev_ref[:, 5 * D:6 * D],
                              ngprev_ref[3:4, :], swg_ref, swu_ref, swd_ref)
    sh1, sc1, g1, sh2, sc2, _ = _mod_slices(mod_ref)
    ng = ng_ref[...]
    hb = (_rms(x, ng[0:1]) * (1.0 + sc1) + sh1).astype(BF16)
    q = jnp.dot(hb, wqkvg_ref[:, :GLA_DK_TOT], preferred_element_type=F32) * (GLA_DK ** -0.5)
    k = jnp.dot(hb, wqkvg_ref[:, GLA_DK_TOT:2 * GLA_DK_TOT], preferred_element_type=F32)
    v = jnp.dot(hb, wqkvg_ref[:, 2 * GLA_DK_TOT:2 * GLA_DK_TOT + GLA_DV_TOT],
                preferred_element_type=F32).astype(BF16)
    log_a = _gla_gate(hb, wlr_ref, wgk_ref, bgk_ref)
    tril = tril_ref[...]
    b = sum(jnp.dot(tril, part, preferred_element_type=F32) for part in _split3(log_a))
    row = lax.broadcasted_iota(jnp.int32, (GLA_CHUNK, GLA_CHUNK), 0)
    col = lax.broadcasted_iota(jnp.int32, (GLA_CHUNK, GLA_CHUNK), 1)
    causal = row >= col
    n_chunks = tt // GLA_CHUNK
    for c in range(n_chunks):
        rows = slice(c * GLA_CHUNK, (c + 1) * GLA_CHUNK)
        last = (c + 1) * GLA_CHUNK - 1
        for hd in range(GLA_HEADS):
            kc = slice(hd * GLA_DK, (hd + 1) * GLA_DK)
            vc = slice(hd * GLA_DV, (hd + 1) * GLA_DV)
            bb = b[rows, kc]
            b_last = b[last:last + 1, kc]
            q_dec = (q[rows, kc] * jnp.exp(bb)).astype(BF16)
            k_inv = (k[rows, kc] * jnp.exp(-bb)).astype(BF16)
            k_end = (k[rows, kc] * jnp.exp(b_last - bb)).astype(BF16)
            att = jnp.where(causal, _dot_nt(q_dec, k_inv), 0.0).astype(BF16)
            qd_scr[rows, kc] = q_dec
            o_scr[rows, vc] = jnp.dot(att, v[rows, vc], preferred_element_type=F32)
            dst_scr[c * GLA_HEADS + hd] = lax.dot_general(
                v[rows, vc], k_end, (((0,), (0,)), ((), ())), preferred_element_type=F32)
    states = [st_scr[hd] for hd in range(GLA_HEADS)]
    for c in range(n_chunks):
        rows = slice(c * GLA_CHUNK, (c + 1) * GLA_CHUNK)
        last = (c + 1) * GLA_CHUNK - 1
        for hd in range(GLA_HEADS):
            kc = slice(hd * GLA_DK, (hd + 1) * GLA_DK)
            vc = slice(hd * GLA_DV, (hd + 1) * GLA_DV)
            o_scr[rows, vc] += _dot_nt(qd_scr[rows, kc], states[hd].astype(BF16))
            states[hd] = states[hd] * jnp.exp(b[last:last + 1, kc]) + dst_scr[c * GLA_HEADS + hd]
    for hd in range(GLA_HEADS):
        st_scr[hd] = states[hd]

    @pl.when(j == pl.num_programs(1) - 1)
    def _():
        for hd in range(GLA_HEADS):
            st_ref[hd] = st_scr[hd].T

    go = jnp.dot(hb, wqkvg_ref[:, 2 * GLA_DK_TOT + GLA_DV_TOT:], preferred_element_type=F32)
    y = jnp.dot(_gla_out(o_scr[...], go, gng_ref[...]), wout_ref[...], preferred_element_type=F32)
    x1 = x + g1 * _rms(y, ng[1:2])
    x1_ref[...] = x1
    _ffn_prep(x1, ng, sh2, sc2, rw_ref, h_ref, lg_ref)


def _gla_mixer(x2d, routed, hp, mod3_prev, ng_prev, shared_prev, mod3, batch, seq, ng, wqkvg, wlr, wgk, bgk,
               gng, wout, rw_t):
    tt = GLA_TILE
    tpb = seq // tt
    n = x2d.shape[0]
    idx = jnp.arange(tt)
    tril = ((idx[:, None] >= idx[None, :]) &
            (idx[:, None] // GLA_CHUNK == idx[None, :] // GLA_CHUNK)).astype(BF16)
    row_map = lambda b, j: (b * tpb + j, 0)
    mod_map = lambda b, j: (b, 0, 0)
    consts = (ng, wqkvg, wlr, wgk, bgk, tril, gng, wout, rw_t)
    prev_consts = (ng_prev,) + tuple(shared_prev)
    return pl.pallas_call(
        functools.partial(_gla_body, tt=tt),
        grid=(batch, tpb),
        in_specs=[pl.BlockSpec((tt, D), row_map), pl.BlockSpec((tt, D), row_map),
                  pl.BlockSpec((tt, DP), row_map), pl.BlockSpec((None, 1, 6 * D), mod_map)] +
                 [_const_spec(a.shape) for a in prev_consts] +
                 [pl.BlockSpec((None, 1, 6 * D), mod_map)] +
                 [_const_spec(a.shape) for a in consts],
        out_specs=[pl.BlockSpec((tt, D), row_map), pl.BlockSpec((tt, DP), row_map),
                   pl.BlockSpec((N_EXPERTS, tt), lambda b, j: (0, b * tpb + j)),
                   pl.BlockSpec((None, GLA_HEADS, GLA_DK, GLA_DV), lambda b, j: (b, 0, 0, 0))],
        out_shape=[jax.ShapeDtypeStruct((n, D), F32), jax.ShapeDtypeStruct((n, DP), jnp.int32),
                   jax.ShapeDtypeStruct((N_EXPERTS, n), F32),
                   jax.ShapeDtypeStruct((batch, GLA_HEADS, GLA_DK, GLA_DV), F32)],
        scratch_shapes=[pltpu.VMEM((GLA_HEADS, GLA_DV, GLA_DK), F32),
                        pltpu.VMEM((tt, GLA_DV_TOT), F32),
                        pltpu.VMEM((tt, GLA_DK_TOT), BF16),
                        pltpu.VMEM((tt // GLA_CHUNK * GLA_HEADS, GLA_DV, GLA_DK), F32)],
        compiler_params=_cparams("parallel", "arbitrary"),
        name="gla_mixer",
    )(x2d, routed, hp, mod3_prev, *prev_consts, mod3, *consts)


def _gla1_proj_body(x_ref, mod_ref, ng_ref, wqkvg_ref, wlr_ref, wgk_ref, bgk_ref,
                    q_ref, k_ref, v_ref, go_ref, dec_ref):
    sh1, sc1, _, _, _, _ = _mod_slices(mod_ref)
    ng = ng_ref[...]
    hb = (_rms(x_ref[...], ng[0:1]) * (1.0 + sc1) + sh1).astype(BF16)
    proj = jnp.dot(hb, wqkvg_ref[...], preferred_element_type=F32)
    q_ref[...] = proj[:, :GLA_DK_TOT] * (GLA_DK ** -0.5)
    k_ref[...] = proj[:, GLA_DK_TOT:2 * GLA_DK_TOT]
    v_ref[...] = proj[:, 2 * GLA_DK_TOT:2 * GLA_DK_TOT + GLA_DV_TOT]
    go_ref[...] = proj[:, 2 * GLA_DK_TOT + GLA_DV_TOT:]
    dec_ref[...] = jnp.exp(_gla_gate(hb, wlr_ref, wgk_ref, bgk_ref))


GLA1_TOK = 8


def _gla1_state_body(st_ref, qc_ref, kc_ref, dc_ref, v_ref, nst_ref, o_ref):
    v = v_ref[...]
    for i in range(GLA1_TOK):
        for hd in range(GLA_HEADS):
            vrow = v[i:i + 1, hd * GLA_DV:(hd + 1) * GLA_DV]
            s_new = dc_ref[hd][:, i:i + 1] * st_ref[i, hd] + kc_ref[hd][:, i:i + 1] * vrow
            nst_ref[i, hd] = s_new
            o_ref[i:i + 1, hd * GLA_DV:(hd + 1) * GLA_DV] = jnp.sum(
                qc_ref[hd][:, i:i + 1] * s_new, axis=0, keepdims=True)


def _gla1_out_body(x_ref, o_ref, go_ref, mod_ref, ng_ref, gng_ref, wout_ref, rw_ref, x1_ref, h_ref, lg_ref):
    _, _, g1, sh2, sc2, _ = _mod_slices(mod_ref)
    ng = ng_ref[...]
    y = jnp.dot(_gla_out(o_ref[...], go_ref[...], gng_ref[...]), wout_ref[...], preferred_element_type=F32)
    x1 = x_ref[...] + g1 * _rms(y, ng[1:2])
    x1_ref[...] = x1
    _ffn_prep(x1, ng, sh2, sc2, rw_ref, h_ref, lg_ref)


def _gla_mixer_one(x2d, mod2, state, ng, wqkvg, wlr, wgk, bgk, gng, wout, rw_t):
    n = x2d.shape[0]
    consts = (ng, wqkvg, wlr, wgk, bgk)
    q, k, v, go, dec = pl.pallas_call(
        _gla1_proj_body,
        in_specs=[_const_spec(a.shape) for a in (x2d, mod2) + consts],
        out_specs=[_const_spec((n, GLA_DK_TOT)), _const_spec((n, GLA_DK_TOT)), _const_spec((n, GLA_DV_TOT)),
                   _const_spec((n, GLA_DV_TOT)), _const_spec((n, GLA_DK_TOT))],
        out_shape=[jax.ShapeDtypeStruct((n, GLA_DK_TOT), F32), jax.ShapeDtypeStruct((n, GLA_DK_TOT), F32),
                   jax.ShapeDtypeStruct((n, GLA_DV_TOT), F32), jax.ShapeDtypeStruct((n, GLA_DV_TOT), F32),
                   jax.ShapeDtypeStruct((n, GLA_DK_TOT), F32)],
        grid=(1,),
        compiler_params=_cparams("arbitrary"),
        name="gla1_proj",
    )(x2d, mod2, *consts)

    def cols(a):
        return a.reshape(n // GLA1_TOK, GLA1_TOK, GLA_HEADS, GLA_DK).transpose(0, 2, 3, 1)

    col_spec = pl.BlockSpec((None, GLA_HEADS, GLA_DK, GLA1_TOK), lambda i: (i, 0, 0, 0))
    st_spec = pl.BlockSpec((GLA1_TOK, GLA_HEADS, GLA_DK, GLA_DV), lambda i: (i, 0, 0, 0))
    new_state, o = pl.pallas_call(
        _gla1_state_body,
        grid=(n // GLA1_TOK,),
        in_specs=[st_spec, col_spec, col_spec, col_spec, pl.BlockSpec((GLA1_TOK, GLA_DV_TOT), lambda i: (i, 0))],
        out_specs=[st_spec, pl.BlockSpec((GLA1_TOK, GLA_DV_TOT), lambda i: (i, 0))],
        out_shape=[jax.ShapeDtypeStruct(state.shape, F32), jax.ShapeDtypeStruct((n, GLA_DV_TOT), F32)],
        compiler_params=_cparams("parallel"),
        name="gla1_state",
    )(state, cols(q), cols(k), cols(dec), v)

    consts = (mod2, ng, gng, wout, rw_t)
    x1, h, lg = pl.pallas_call(
        _gla1_out_body,
        grid=(1,),
        in_specs=[_const_spec(a.shape) for a in (x2d, o, go) + consts],
        out_specs=[_const_spec((n, D)), _const_spec((n, DP)), _const_spec((N_EXPERTS, n))],
        out_shape=[jax.ShapeDtypeStruct((n, D), F32), jax.ShapeDtypeStruct((n, DP), jnp.int32),
                   jax.ShapeDtypeStruct((N_EXPERTS, n), F32)],
        compiler_params=_cparams("arbitrary"),
        name="gla1_out",
    )(x2d, o, go, *consts)
    return x1, h, lg, new_state


def _moe_routed(h_p, h_s, lg_p, lg_s, router_bias, wg, wu, wd):
    h, lg = h_p, lg_p
    if h_s is not None:
        h = jnp.concatenate([h_p, h_s], axis=0)
        lg = jnp.concatenate([lg_p, lg_s], axis=1)
    n = h.shape[0]
    n_pad = -(-n // TOKEN_PAD) * TOKEN_PAD
    if n_pad != n:
        h = jnp.pad(h, ((0, n_pad - n), (0, 0)))
        lg = jnp.pad(lg, ((0, 0), (0, n_pad - n)))
    eid, rank, wts, counts = _router(lg, router_bias, n)
    tile_count = ((counts[:, 0] + EXPERT_TILE - 1) // EXPERT_TILE).astype(jnp.int32)
    tile_end = jnp.cumsum(tile_count).astype(jnp.int32)
    tile_first = tile_end - tile_count
    off = tile_first * EXPERT_TILE
    p_alloc = TOP_K * n_pad + N_EXPERTS * EXPERT_TILE
    dest = _dest(off, eid, rank, n, p_alloc - 1)
    dest_w = dest.reshape(TOP_K, n_pad // DISPATCH_W, DISPATCH_W).transpose(1, 0, 2)
    xs = _sc_dispatch(h, dest_w, p_alloc)
    ys = _experts(xs, tile_first, tile_count, tile_end[-1:], wg, wu, wd)
    dest_tm = dest.T.reshape(n_pad // SUM_W, SUM_PARTS, SUM_W * TOP_K // SUM_PARTS)
    w_lanes = jnp.repeat(wts.T, SC_LANES, axis=1)
    return _sc_gather_sum(ys, dest_tm, w_lanes), h


def kernel(x_prompt, x_sample, state_gla, c_prompt, c_sample, norm_g, ada_w, ada_b, gm_w_in, gm_b_in,
           gm_ln_g, gm_ln_b, gm_w_s, gm_b_s, gm_w_out, gla_w_in, gla_w_gk, gla_b_gk, gla_norm_g,
           gla_w_out, router_w, router_bias, exp_w_gate, exp_w_up, exp_w_down, sh_w_gate, sh_w_up,
           sh_w_down):
    batch, seq, _ = x_prompt.shape
    n_s = x_sample.shape[0]
    n_p = batch * seq
    tpb = seq // MIX_TILE
    xp = x_prompt.reshape(n_p, D)
    xs = x_sample.reshape(n_s, D)

    mod = _ada(jnp.concatenate([c_prompt, c_sample], axis=0), ada_w, ada_b)
    mod_p = [mod[i, :batch].reshape(batch, 1, 6 * D) for i in range(2)]
    mod_s = [mod[i, batch:] for i in range(2)]
    rw_t = [jnp.concatenate(_split3(router_w[i].T), axis=0) for i in range(2)]

    ws_causal = jnp.tril(gm_w_s[0]).astype(BF16)
    bs_cols = gm_b_s[0].T
    eye = jnp.eye(GM_CHUNK, dtype=F32)
    ws_first = (gm_w_s[0][:, 0, 0][:, None, None] * eye).astype(BF16)
    bs_first = jnp.broadcast_to(gm_b_s[0][:, 0][None, :], (GM_CHUNK, GM_GROUPS))
    gm_args = (norm_g[0], gm_w_in[0].astype(BF16), gm_b_in[0].reshape(1, -1), gm_ln_g[0].reshape(1, -1),
               gm_ln_b[0].reshape(1, -1))
    wout0 = gm_w_out[0].astype(BF16)
    shared = [(sh_w_gate[i].astype(BF16), sh_w_up[i].astype(BF16), sh_w_down[i].astype(BF16))
              for i in range(2)]
    n_qkvg = 2 * GLA_DK_TOT + 2 * GLA_DV_TOT
    wqkvg = gla_w_in[0][:, :n_qkvg].astype(BF16)
    wlr = jnp.pad(gla_w_in[0][:, n_qkvg:], ((0, 0), (0, LANES - GLA_GATE_RANK))).astype(BF16)
    wgk = jnp.pad(gla_w_gk[0], ((0, LANES - GLA_GATE_RANK), (0, 0))).astype(BF16)
    gla_args = (norm_g[1], wqkvg, wlr, wgk, gla_b_gk[0].reshape(1, -1), gla_norm_g[0].reshape(1, -1),
                gla_w_out[0].astype(BF16), rw_t[1])
    experts_f32 = (exp_w_gate, exp_w_up, exp_w_down)

    half = batch // 2
    streams = [(0, half, False), (half, batch - half, True)]
    st = [dict() for _ in streams]

    experts = []
    for layer, (s, (b0, nb, with_new)) in enumerate(zip(st, streams)):
        s["mod_p"] = [mod_p[i][b0:b0 + nb] for i in range(2)]
        s["n"] = nb * seq
        s["x1p"], s["hp"], s["lgp"], *w16 = _gmlp_mixer(
            xp, b0 * tpb, s["n"], s["mod_p"][0], False, MIX_TILE, tpb, *gm_args, ws_causal, bs_cols, wout0,
            rw_t[0], emit_v=False, cast_w=(layer, *experts_f32))
        experts.append(w16)
        s["hs"] = s["lgs"] = None
        if with_new:
            s["x1s"], s["hs"], s["lgs"], v_rows = _gmlp_mixer(xs, 0, n_s, mod_s[0], True, n_s, 1, *gm_args,
                                                              ws_first, bs_first, wout0, rw_t[0], emit_v=True)
    for s, (b0, nb, with_new) in zip(st, streams):
        moe = s["moe0"] = _moe_routed(s["hp"], s["hs"], s["lgp"], s["lgs"], router_bias[0], *experts[0])
        if with_new:
            s["x2s"] = _combine(s["x1s"], *moe, s["n"] // n_s, mod_s[0], True, n_s, 1, norm_g[0], *shared[0])
    for s, (b0, nb, with_new) in zip(st, streams):
        s["x3p"], s["hp"], s["lgp"], s["st_p"] = _gla_mixer(s["x1p"], *s["moe0"], s["mod_p"][0], norm_g[0],
                                                            shared[0], s["mod_p"][1], nb, seq, *gla_args)
        if with_new:
            s["x3s"], s["hs"], s["lgs"], st_s = _gla_mixer_one(s["x2s"], mod_s[1], state_gla[:, 0], *gla_args)
    y_prompt = None
    for s, (b0, nb, with_new) in zip(st, streams):
        moe = _moe_routed(s["hp"], s["hs"], s["lgp"], s["lgs"], router_bias[1], *experts[1])
        y_prompt = _combine(s["x3p"], *moe, 0, s["mod_p"][1], False, MIX_TILE, tpb, norm_g[1], *shared[1],
                            out_rows=n_p, out_blk0=b0 * tpb, out_buf=y_prompt)
        if with_new:
            y_new = _combine(s["x3s"], *moe, s["n"] // n_s, mod_s[1], True, n_s, 1, norm_g[1], *shared[1])
    st_p = jnp.concatenate([s["st_p"] for s in st], axis=0)

    return (y_prompt.reshape(batch, seq, D), y_new.reshape(n_s, 1, D), st_p[:, None], st_s[:, None],
            v_rows.reshape(n_s, 1, 1, GM_HALF))
```

```python
import functools
import math

import jax
import jax.numpy as jnp
from jax import lax
from jax.experimental import pallas as pl
from jax.experimental.pallas import tpu as pltpu
from jax.experimental.pallas import tpu_sc as plsc

F32 = jnp.float32
BF16 = jnp.bfloat16

D = 1024
DP = D // 2
GM_CHUNK = 128
GM_HALF = 2 * D
GM_GROUPS = 8
GM_GROUP_DIM = GM_HALF // GM_GROUPS
GLA_HEADS = 4
GLA_DK = 128
GLA_DV = 256
GLA_DK_TOT = GLA_HEADS * GLA_DK
GLA_DV_TOT = GLA_HEADS * GLA_DV
GLA_GATE_RANK = 16
GLA_GATE_NORMALIZER = 16.0
GLA_CHUNK = 64
N_EXPERTS = 64
TOP_K = 8
N_EXPERT_GROUPS = 8
GROUP_SIZE = N_EXPERTS // N_EXPERT_GROUPS
TOPK_GROUPS = 4
EXPERT_DIM = D // 4
ROUTED_SCALE = 2.5
NORM_EPS = 1e-6
LN_EPS = 1e-5

LANES = 128
VMEM_LIMIT = 56 * 1024 * 1024

MIX_TILE = 256
GLA_TILE = 512
GLA_CUM_BLOCK = 256
GM_COL_BLOCK = 512
ROUTER_TILE = 1024
EXPERT_TILE = 272
EXPERT_X_SLOTS = 6
EXPERT_AHEAD = EXPERT_X_SLOTS - 2
EXPERT_Y_SLOTS = 4
SC_WORKERS = 32
DISPATCH_W = 32
SC_LANES = 16
SUM_W = 16
SUM_PARTS = 4
SUM_UNROLL = 4
TOKEN_PAD = SC_WORKERS * DISPATCH_W


def _cparams(*sem):
    return pltpu.CompilerParams(dimension_semantics=sem, vmem_limit_bytes=VMEM_LIMIT)


def _rms(x, g):
    return x * lax.rsqrt(jnp.mean(x * x, axis=-1, keepdims=True) + NORM_EPS) * g


def _silu(x):
    return x * (1.0 / (1.0 + jnp.exp(-x)))


def _gelu(x):
    return 0.5 * x * (1.0 + lax.erf(x * (1.0 / math.sqrt(2.0))))


def _bdot(a, b):
    return jnp.dot(a.astype(BF16), b.astype(BF16), preferred_element_type=F32)


def _dot_nt(a, b, precision=None):
    return lax.dot_general(a, b, (((1,), (1,)), ((), ())), preferred_element_type=F32,
                           precision=precision)


def _mod_slices(mod_ref):
    return [mod_ref[:, i * D:(i + 1) * D] for i in range(6)]


HI_HALF = -65536


def _pack_rows(x):
    lo = lax.bitcast_convert_type(x[:, :DP].astype(BF16).astype(F32), jnp.int32)
    hi = lax.bitcast_convert_type(x[:, DP:].astype(BF16).astype(F32), jnp.int32)
    return lax.shift_right_logical(lo, 16) | (hi & HI_HALF)


def _unpack_rows(p):
    lo = lax.bitcast_convert_type(lax.shift_left(p, 16), F32)
    hi = lax.bitcast_convert_type(p & HI_HALF, F32)
    return lo, hi


def _ffn_prep(x1, ng, sh2, sc2, rw_ref, h_ref, lg_ref):
    hffn = _rms(x1, ng[2:3]) * (1.0 + sc2) + sh2
    h_ref[...] = _pack_rows(hffn)
    lg3 = _dot_nt(rw_ref[...], hffn.astype(BF16))
    lg_ref[...] = lg3[:N_EXPERTS] + lg3[N_EXPERTS:2 * N_EXPERTS] + lg3[2 * N_EXPERTS:]


def _ada_body(c_ref, w_ref, b_ref, o_ref):
    c = c_ref[...]
    o_ref[...] = _bdot(_silu(c), w_ref[...]) + b_ref[...]


def _ada(c, ada_w, ada_b):
    n = c.shape[0]
    depth = ada_w.shape[0]
    tn = 1536
    return pl.pallas_call(
        _ada_body,
        grid=(depth, 6 * D // tn),
        in_specs=[pl.BlockSpec((n, D), lambda l, j: (0, 0)),
                  pl.BlockSpec((None, D, tn), lambda l, j: (l, 0, j)),
                  pl.BlockSpec((None, 1, tn), lambda l, j: (l, 0, j))],
        out_specs=pl.BlockSpec((None, n, tn), lambda l, j: (l, 0, j)),
        out_shape=jax.ShapeDtypeStruct((depth, n, 6 * D), F32),
        compiler_params=_cparams("parallel", "parallel"),
        name="ada_mod",
    )(c, ada_w, ada_b.reshape(depth, 1, 6 * D))


def _mod_spec(per_row, tt, tiles_per_batch):
    if per_row:
        return pl.BlockSpec((tt, 6 * D), lambda i: (i, 0))
    return pl.BlockSpec((None, 1, 6 * D), lambda i: (i // tiles_per_batch, 0, 0))


def _const_spec(shape):
    zeros = (0,) * len(shape)
    return pl.BlockSpec(shape, lambda *_: zeros)


def _gmlp_body(x_ref, mod_ref, ng_ref, win_ref, bin_ref, lng_ref, lnb_ref, ws_ref, bs_ref, wout_ref,
               rw_ref, *rest, n_chunks, emit_v, cast_w, n_alias):
    rest = list(rest)
    w32_refs = [rest.pop(0) for _ in range(3)] if cast_w else []
    rest = rest[n_alias:]
    x1_ref, h_ref, lg_ref = rest[:3]
    rest = rest[3:]
    v_ref = rest.pop(0) if emit_v else None
    w16_refs = [rest.pop(0) for _ in range(3)] if cast_w else []
    um_ref, z_ref = rest
    for src, dst in zip(w32_refs, w16_refs):
        dst[...] = src[...].astype(BF16)
    sh1, sc1, g1, sh2, sc2, _ = _mod_slices(mod_ref)
    ng = ng_ref[...]
    x = x_ref[...]
    hb = (_rms(x, ng[0:1]) * (1.0 + sc1) + sh1).astype(BF16)
    for cb in range(2 * GM_HALF // GM_COL_BLOCK):
        cols = slice(cb * GM_COL_BLOCK, (cb + 1) * GM_COL_BLOCK)
        z_ref[:, cols] = _gelu(jnp.dot(hb, win_ref[:, cols], preferred_element_type=F32) + bin_ref[:, cols])
    u = z_ref[:, :GM_HALF]
    v = z_ref[:, GM_HALF:]
    mu = jnp.mean(v, axis=-1, keepdims=True)
    vc = v - mu
    var = jnp.mean(vc * vc, axis=-1, keepdims=True)
    v = vc * lax.rsqrt(var + LN_EPS) * lng_ref[...] + lnb_ref[...]
    if emit_v:
        v_ref[...] = v
    vb = v.astype(BF16)
    for c in range(n_chunks):
        rows = slice(c * GM_CHUNK, (c + 1) * GM_CHUNK)
        for g in range(GM_GROUPS):
            cols = slice(g * GM_GROUP_DIM, (g + 1) * GM_GROUP_DIM)
            mixed = jnp.dot(ws_ref[g], vb[rows, cols], preferred_element_type=F32) + bs_ref[:, g:g + 1]
            um_ref[rows, cols] = (u[rows, cols] * mixed).astype(BF16)
    y = jnp.dot(um_ref[...], wout_ref[...], preferred_element_type=F32)
    x1 = x + g1 * _rms(y, ng[1:2])
    x1_ref[...] = x1
    _ffn_prep(x1, ng, sh2, sc2, rw_ref, h_ref, lg_ref)


def _ffn_out(n, ffn_rows, into):
    rows = ffn_rows or n
    oblk = 0
    bufs = []
    if into is not None:
        *bufs, oblk = into
        rows = bufs[0].shape[0]
    elif rows != n:
        bufs = [jnp.zeros((rows, DP), jnp.int32), jnp.zeros((N_EXPERTS, rows), F32)]
    shapes = [jax.ShapeDtypeStruct((rows, DP), jnp.int32), jax.ShapeDtypeStruct((N_EXPERTS, rows), F32)]
    return shapes, bufs, oblk


def _gmlp_mixer(x2d, blk0, n, mod, per_row, tt, tiles_per_batch, ng, win, b_in, ln_g, ln_b, ws, bs, wout,
                rw_t, emit_v, cast_w=None, ffn_rows=None, into=None):
    steps = n // tt
    ffn_shapes, alias_bufs, oblk = _ffn_out(n, ffn_rows, into)
    out_shape = [jax.ShapeDtypeStruct((n, D), F32)] + ffn_shapes
    out_specs = [pl.BlockSpec((tt, D), lambda i: (i, 0)), pl.BlockSpec((tt, DP), lambda i: (i + oblk, 0)),
                 pl.BlockSpec((N_EXPERTS, tt), lambda i: (0, i + oblk))]
    if emit_v:
        out_shape.append(jax.ShapeDtypeStruct((n, GM_HALF), F32))
        out_specs.append(pl.BlockSpec((tt, GM_HALF), lambda i: (i, 0)))

    def one_buffer(a):
        zeros = (0,) * a.ndim
        return pl.BlockSpec(a.shape, lambda *_: zeros, pipeline_mode=pl.Buffered(1))

    consts = (ng, win, b_in, ln_g, ln_b, ws, bs, wout, rw_t)
    in_specs = [pl.BlockSpec((tt, D), lambda i: (i + blk0, 0)), _mod_spec(per_row, tt, tiles_per_batch)]
    in_specs += [one_buffer(a) for a in consts]
    args = [x2d, mod, *consts]
    if cast_w is not None:
        layer, *w_all = cast_w
        per_step = N_EXPERTS // steps
        for w in w_all:
            blk = (None, per_step) + w.shape[2:]
            in_specs.append(pl.BlockSpec(blk, lambda i: (layer, i, 0, 0)))
            out_specs.append(pl.BlockSpec(blk[1:], lambda i: (i, 0, 0)))
            out_shape.append(jax.ShapeDtypeStruct(w.shape[1:], BF16))
            args.append(w)
    aliases = {len(args) + i: 1 + i for i in range(len(alias_bufs))}
    in_specs += [pl.BlockSpec(memory_space=pl.ANY)] * len(alias_bufs)
    args += alias_bufs
    return pl.pallas_call(
        functools.partial(_gmlp_body, n_chunks=tt // GM_CHUNK, emit_v=emit_v, cast_w=cast_w is not None,
                          n_alias=len(alias_bufs)),
        grid=(steps,),
        in_specs=in_specs,
        out_specs=out_specs,
        out_shape=out_shape,
        input_output_aliases=aliases,
        scratch_shapes=[pltpu.VMEM((tt, GM_HALF), BF16), pltpu.VMEM((tt, 2 * GM_HALF), F32)],
        compiler_params=_cparams("parallel"),
        name="gmlp_mixer_rows" if per_row else "gmlp_mixer",
    )(*args)


def _combine_body(x_ref, y_ref, h_ref, mod_ref, ng_ref, swg_ref, swu_ref, swd_ref, *rest):
    o_ref = rest[-1]
    o_ref[...] = _channel_mix_residual(x_ref[...], y_ref[...], h_ref, mod_ref[:, 5 * D:6 * D],
                                       ng_ref[3:4, :], swg_ref, swu_ref, swd_ref)


def _combine(x2d, routed, hp, blk0, mod, per_row, tt, tiles_per_batch, ng, swg, swu, swd,
             out_rows=None, out_blk0=0, out_buf=None):
    n = x2d.shape[0]
    in_specs = [pl.BlockSpec((tt, D), lambda i: (i, 0)),
                pl.BlockSpec((tt, D), lambda i: (i + blk0, 0)),
                pl.BlockSpec((tt, DP), lambda i: (i + blk0, 0)),
                _mod_spec(per_row, tt, tiles_per_batch),
                _const_spec(ng.shape), _const_spec(swg.shape), _const_spec(swu.shape),
                _const_spec(swd.shape)]
    args = [x2d, routed, hp, mod, ng, swg, swu, swd]
    aliases = {}
    if out_buf is not None:
        in_specs.append(pl.BlockSpec(memory_space=pl.ANY))
        aliases = {len(args): 0}
        args.append(out_buf)
    return pl.pallas_call(
        _combine_body,
        grid=(n // tt,),
        in_specs=in_specs,
        out_specs=pl.BlockSpec((tt, D), lambda i: (i + out_blk0, 0)),
        out_shape=jax.ShapeDtypeStruct((out_rows or n, D), F32),
        input_output_aliases=aliases,
        compiler_params=_cparams("parallel"),
        name="combine_rows" if per_row else "combine",
    )(*args)


def _router_body(lg_ref, bias_ref, tri_ref, eid_ref, rank_ref, wts_ref, cnt_ref, carry_ref, *, n_real):
    step = pl.program_id(0)

    @pl.when(step == 0)
    def _():
        carry_ref[...] = jnp.zeros_like(carry_ref)

    lg = lg_ref[...]
    tn = lg.shape[1]
    real = (step * tn + lax.broadcasted_iota(jnp.int32, (1, tn), 1)) < n_real
    lg = jnp.where(real, lg, 0.0)
    scores = 1.0 / (1.0 + jnp.exp(-lg))
    sel = scores + bias_ref[...]
    neg = -jnp.inf
    sub8 = lax.broadcasted_iota(jnp.int32, (GROUP_SIZE, tn), 0)
    gsub = lax.broadcasted_iota(jnp.int32, (N_EXPERT_GROUPS, tn), 0)
    gs = jnp.zeros((N_EXPERT_GROUPS, tn), F32)
    for g in range(N_EXPERT_GROUPS):
        blk = sel[g * GROUP_SIZE:(g + 1) * GROUP_SIZE, :]
        m1 = jnp.max(blk, axis=0, keepdims=True)
        i1 = jnp.min(jnp.where(blk == m1, sub8, GROUP_SIZE), axis=0, keepdims=True)
        m2 = jnp.max(jnp.where(sub8 == i1, neg, blk), axis=0, keepdims=True)
        gs = jnp.where(gsub == g, m1 + m2, gs)
    gmask = jnp.zeros((N_EXPERT_GROUPS, tn), jnp.bool_)
    for _ in range(TOPK_GROUPS):
        m = jnp.max(gs, axis=0, keepdims=True)
        i = jnp.min(jnp.where(gs == m, gsub, N_EXPERT_GROUPS), axis=0, keepdims=True)
        hit = gsub == i
        gmask = jnp.logical_or(gmask, hit)
        gs = jnp.where(hit, neg, gs)
    gmaskf = gmask.astype(F32)
    blocks = []
    for g in range(N_EXPERT_GROUPS):
        keep = jnp.broadcast_to(gmaskf[g:g + 1, :], (GROUP_SIZE, tn)) > 0.5
        blocks.append(jnp.where(keep, sel[g * GROUP_SIZE:(g + 1) * GROUP_SIZE, :], neg))
    msel = jnp.concatenate(blocks, axis=0)
    esub = lax.broadcasted_iota(jnp.int32, (N_EXPERTS, tn), 0)
    chosen = jnp.zeros((N_EXPERTS, tn), jnp.bool_)
    picks = []
    for _ in range(TOP_K):
        m = jnp.max(msel, axis=0, keepdims=True)
        i = jnp.min(jnp.where(msel == m, esub, N_EXPERTS), axis=0, keepdims=True)
        hit = esub == i
        picks.append(i)
        chosen = jnp.logical_or(chosen, hit)
        msel = jnp.where(hit, neg, msel)
    w = jnp.where(chosen, scores, 0.0)
    w = w / jnp.sum(w, axis=0, keepdims=True) * ROUTED_SCALE
    counted = jnp.where(jnp.logical_and(chosen, real), 1.0, 0.0)
    incl = jnp.dot(counted.astype(BF16), tri_ref[...], preferred_element_type=F32)
    rank_full = carry_ref[:, 0:1] + incl - 1.0
    ksub = lax.broadcasted_iota(jnp.int32, (TOP_K, tn), 0)
    eid = jnp.zeros((TOP_K, tn), jnp.int32)
    rank = jnp.zeros((TOP_K, tn), F32)
    wts = jnp.zeros((TOP_K, tn), F32)
    for k in range(TOP_K):
        hit = esub == picks[k]
        eid = jnp.where(ksub == k, picks[k], eid)
        rank = jnp.where(ksub == k, jnp.sum(jnp.where(hit, rank_full, 0.0), axis=0, keepdims=True), rank)
        wts = jnp.where(ksub == k, jnp.sum(jnp.where(hit, w, 0.0), axis=0, keepdims=True), wts)
    eid_ref[...] = eid
    rank_ref[...] = rank.astype(jnp.int32)
    wts_ref[...] = wts
    carry = carry_ref[...] + incl[:, tn - 1:tn]
    carry_ref[...] = carry
    cnt_ref[...] = carry.astype(jnp.int32)


def _router(lg_t, bias, n_real):
    n = lg_t.shape[1]
    tn = ROUTER_TILE
    idx = jnp.arange(tn)
    tri = (idx[:, None] <= idx[None, :]).astype(BF16)
    kspec = pl.BlockSpec((TOP_K, tn), lambda i: (0, i))
    return pl.pallas_call(
        functools.partial(_router_body, n_real=n_real),
        grid=(n // tn,),
        in_specs=[pl.BlockSpec((N_EXPERTS, tn), lambda i: (0, i)), _const_spec((N_EXPERTS, 1)),
                  _const_spec((tn, tn))],
        out_specs=[kspec, kspec, kspec, _const_spec((N_EXPERTS, LANES))],
        out_shape=[jax.ShapeDtypeStruct((TOP_K, n), jnp.int32), jax.ShapeDtypeStruct((TOP_K, n), jnp.int32),
                   jax.ShapeDtypeStruct((TOP_K, n), F32), jax.ShapeDtypeStruct((N_EXPERTS, LANES), jnp.int32)],
        scratch_shapes=[pltpu.VMEM((N_EXPERTS, LANES), F32)],
        compiler_params=_cparams("arbitrary"),
        name="router",
    )(lg_t, bias.reshape(N_EXPERTS, 1), tri)


def _dest_body(off_ref, eid_ref, rank_ref, dest_ref, *, n_real, last_row):
    eid = eid_ref[...]
    base = jnp.zeros(eid.shape, jnp.int32)
    for e in range(N_EXPERTS):
        base = jnp.where(eid == e, off_ref[e], base)
    tok = lax.broadcasted_iota(jnp.int32, eid.shape, 1)
    slot = lax.broadcasted_iota(jnp.int32, eid.shape, 0)
    unused = last_row - ((tok - n_real) * TOP_K + slot)
    dest_ref[...] = jnp.where(tok < n_real, base + rank_ref[...], unused)


def _dest(off, eid, rank, n_real, last_row):
    spec = pl.BlockSpec(eid.shape, lambda i, off_ref: (0, 0))
    return pl.pallas_call(
        functools.partial(_dest_body, n_real=n_real, last_row=last_row),
        grid_spec=pltpu.PrefetchScalarGridSpec(num_scalar_prefetch=1, grid=(1,), in_specs=[spec, spec],
                                               out_specs=spec),
        out_shape=jax.ShapeDtypeStruct(eid.shape, jnp.int32),
        compiler_params=_cparams("arbitrary"),
        name="dest_rows",
    )(off, eid, rank)


def _sc_mesh():
    return plsc.VectorSubcoreMesh(core_axis_name="core", subcore_axis_name="subcore")


def _sc_dispatch(hp, dest_w, p_alloc):
    n = hp.shape[0]
    w = dest_w.shape[2]

    @functools.partial(pl.kernel, out_type=jax.ShapeDtypeStruct((p_alloc, DP), jnp.int32), mesh=_sc_mesh(),
                       name="sc_dispatch")
    def run(hp_hbm, dest_hbm, xs_hbm):
        def body(x_vmem, i_vmem):
            for k in range(TOP_K):
                pltpu.sync_copy(x_vmem, xs_hbm.at[i_vmem.at[k]])

        pltpu.emit_pipeline(
            body,
            grid=(n // w,),
            in_specs=[pl.BlockSpec((w, DP), lambda i: (i, 0)),
                      pl.BlockSpec((None, TOP_K, w), lambda i: (i, 0, 0))],
            out_specs=[],
            core_axis_name=("core", "subcore"),
            dimension_semantics=(pltpu.PARALLEL,),
        )(hp_hbm, dest_hbm)

    return run(hp, dest_w)


def _sc_gather_sum(ys, dest_tm, w_lanes):
    n_win, parts, pk = dest_tm.shape
    w = parts * pk // TOP_K
    wp = w // parts
    n_vec = DP // SC_LANES

    @functools.partial(pl.kernel, out_type=jax.ShapeDtypeStruct((n_win * w, D), F32), mesh=_sc_mesh(),
                       scratch_types=[pltpu.VMEM((parts, pk, DP), jnp.int32), pltpu.SemaphoreType.DMA((parts,))],
                       compiler_params=pltpu.CompilerParams(needs_layout_passes=False), name="sc_gather_sum")
    def run(ys_hbm, dest_hbm, w_hbm, o_hbm, rows_v, sems):
        def body(i_vmem, w_vmem, o_vmem):
            copies = [pltpu.async_copy(ys_hbm.at[i_vmem.at[p]], rows_v.at[p], sems.at[p]) for p in range(parts)]
            for p in range(parts):
                copies[p].wait()

                @pl.loop(0, wp)
                def _(t):
                    tok = p * wp + t
                    wv = [w_vmem[tok, pl.ds(k * SC_LANES, SC_LANES)] for k in range(TOP_K)]

                    @pl.loop(0, n_vec, step=SUM_UNROLL)
                    def _(j0):
                        for u in range(SUM_UNROLL):
                            col = (j0 + u) * SC_LANES
                            lo = jnp.zeros((SC_LANES,), F32)
                            hi = jnp.zeros((SC_LANES,), F32)
                            for k in range(TOP_K):
                                word = rows_v[p, t * TOP_K + k, pl.ds(col, SC_LANES)]
                                lo = lo + wv[k] * plsc.bitcast(lax.shift_left(word, 16), F32)
                                hi = hi + wv[k] * plsc.bitcast(word & HI_HALF, F32)
                            o_vmem[tok, pl.ds(col, SC_LANES)] = lo
                            o_vmem[tok, pl.ds(DP + col, SC_LANES)] = hi

        pltpu.emit_pipeline(
            body,
            grid=(n_win,),
            in_specs=[pl.BlockSpec((None, parts, pk), lambda i: (i, 0, 0)),
                      pl.BlockSpec((w, TOP_K * SC_LANES), lambda i: (i, 0))],
            out_specs=[pl.BlockSpec((w, D), lambda i: (i, 0))],
            core_axis_name=("core", "subcore"),
            dimension_semantics=(pltpu.PARALLEL,),
        )(dest_hbm, w_hbm, o_hbm)

    return run(ys, dest_tm, w_lanes)


def _expert_body(first_ref, cnt_ref, nused_ref, xs_hbm, wg_s, wu_s, wd_s, ys_hbm, xbuf, ybuf, xsem, ysem):
    e = pl.program_id(0)
    n_used = nused_ref[0]

    def load(g):
        rows = pl.ds(pl.multiple_of(g * EXPERT_TILE, EXPERT_TILE), EXPERT_TILE)
        slot = g % EXPERT_X_SLOTS
        return pltpu.make_async_copy(xs_hbm.at[rows], xbuf.at[slot], xsem.at[slot])

    def store(g):
        rows = pl.ds(pl.multiple_of(g * EXPERT_TILE, EXPERT_TILE), EXPERT_TILE)
        slot = g % EXPERT_Y_SLOTS
        return pltpu.make_async_copy(ybuf.at[slot], ys_hbm.at[rows], ysem.at[slot])

    @pl.when(e == 0)
    def _():
        for g in range(EXPERT_AHEAD):
            @pl.when(g < n_used)
            def _():
                load(g).start()

    first = first_ref[e]
    cnt = cnt_ref[e]

    def acquire(g):
        ahead = g + EXPERT_AHEAD

        @pl.when(ahead < n_used)
        def _():
            load(ahead).start()

        load(g).wait()

        @pl.when(g >= EXPERT_Y_SLOTS)
        def _():
            store(g - EXPERT_Y_SLOTS).wait()

    def compute(g):
        lo, hi = _unpack_rows(xbuf[g % EXPERT_X_SLOTS])
        lo = lo.astype(BF16)
        hi = hi.astype(BF16)

        def xdot(w_s):
            return (jnp.dot(lo, w_s[:DP, :], preferred_element_type=F32) +
                    jnp.dot(hi, w_s[DP:, :], preferred_element_type=F32))

        a = (_silu(xdot(wg_s)) * xdot(wu_s)).astype(BF16)
        ybuf[g % EXPERT_Y_SLOTS] = _pack_rows(jnp.dot(a, wd_s[...], preferred_element_type=F32))

    def pair(j, carry):
        g = first + 2 * j
        acquire(g)
        acquire(g + 1)
        compute(g)
        compute(g + 1)
        store(g).start()
        store(g + 1).start()
        return carry

    lax.fori_loop(0, cnt // 2, pair, 0)

    @pl.when(cnt % 2 == 1)
    def _():
        g = first + cnt - 1
        acquire(g)
        compute(g)
        store(g).start()

    @pl.when(e == N_EXPERTS - 1)
    def _():
        for k in range(EXPERT_Y_SLOTS):
            g = n_used - 1 - k

            @pl.when(g >= 0)
            def _():
                store(g).wait()


def _experts(xs, tile_first, tile_count, n_used, wg, wu, wd):
    def w_map(e, first, cnt, nu):
        return (e, 0, 0)

    return pl.pallas_call(
        _expert_body,
        grid_spec=pltpu.PrefetchScalarGridSpec(
            num_scalar_prefetch=3, grid=(N_EXPERTS,),
            in_specs=[pl.BlockSpec(memory_space=pl.ANY),
                      pl.BlockSpec((None, D, EXPERT_DIM), w_map),
                      pl.BlockSpec((None, D, EXPERT_DIM), w_map),
                      pl.BlockSpec((None, EXPERT_DIM, D), w_map)],
            out_specs=pl.BlockSpec(memory_space=pl.ANY),
            scratch_shapes=[pltpu.VMEM((EXPERT_X_SLOTS, EXPERT_TILE, DP), jnp.int32),
                            pltpu.VMEM((EXPERT_Y_SLOTS, EXPERT_TILE, DP), jnp.int32),
                            pltpu.SemaphoreType.DMA((EXPERT_X_SLOTS,)),
                            pltpu.SemaphoreType.DMA((EXPERT_Y_SLOTS,))]),
        out_shape=jax.ShapeDtypeStruct(xs.shape, jnp.int32),
        compiler_params=_cparams("arbitrary"),
        name="experts",
    )(tile_first, tile_count, n_used, xs, wg, wu, wd)


def _log_sigmoid(z):
    return jnp.minimum(z, 0.0) - jnp.log(1.0 + jnp.exp(-jnp.abs(z)))


def _gla_gate(hb, wlr_ref, wgk_ref, bgk_ref):
    lr = jnp.dot(hb, wlr_ref[...], preferred_element_type=F32)
    z = _bdot(lr, wgk_ref[...]) + bgk_ref[...]
    return _log_sigmoid(z) * (1.0 / GLA_GATE_NORMALIZER)


def _split3(a):
    hi = a.astype(BF16)
    r1 = a - hi.astype(F32)
    mid = r1.astype(BF16)
    lo = (r1 - mid.astype(F32)).astype(BF16)
    return hi, mid, lo


def _gla_out(o_ref_val, go, gng):
    parts = []
    for hd in range(GLA_HEADS):
        cols = slice(hd * GLA_DV, (hd + 1) * GLA_DV)
        parts.append((_rms(o_ref_val[:, cols], gng) * _silu(go[:, cols])).astype(BF16))
    return jnp.concatenate(parts, axis=1)


def _channel_mix_residual(x, routed, h_ref, g2, ng3, swg_ref, swu_ref, swd_ref):
    h_lo, h_hi = _unpack_rows(h_ref[...])
    h_lo = h_lo.astype(BF16)
    h_hi = h_hi.astype(BF16)

    def hdot(w_ref_):
        return (jnp.dot(h_lo, w_ref_[:DP, :], preferred_element_type=F32) +
                jnp.dot(h_hi, w_ref_[DP:, :], preferred_element_type=F32))

    hs = (_silu(hdot(swg_ref)) * hdot(swu_ref)).astype(BF16)
    y = jnp.dot(hs, swd_ref[...], preferred_element_type=F32) + routed
    return x + g2 * _rms(y, ng3)


def _gla_body(x_ref, y_ref, hprev_ref, modprev_ref, ngprev_ref, swg_ref, swu_ref, swd_ref,
              mod_ref, ng_ref, wqkvg_ref, wlr_ref, wgk_ref, bgk_ref, tril_ref, gng_ref, wout_ref,
              rw_ref, *rest, tt, n_alias):
    x1_ref, h_ref, lg_ref, st_ref, st_scr, o_scr, qd_scr, dst_scr = rest[n_alias:]
    j = pl.program_id(1)

    @pl.when(j == 0)
    def _():
        st_scr[...] = jnp.zeros_like(st_scr)

    x = _channel_mix_residual(x_ref[...], y_ref[...], hprev_ref, modprev_ref[:, 5 * D:6 * D],
                              ngprev_ref[3:4, :], swg_ref, swu_ref, swd_ref)
    sh1, sc1, g1, sh2, sc2, _ = _mod_slices(mod_ref)
    ng = ng_ref[...]
    hb = (_rms(x, ng[0:1]) * (1.0 + sc1) + sh1).astype(BF16)
    q = jnp.dot(hb, wqkvg_ref[:, :GLA_DK_TOT], preferred_element_type=F32) * (GLA_DK ** -0.5)
    k = jnp.dot(hb, wqkvg_ref[:, GLA_DK_TOT:2 * GLA_DK_TOT], preferred_element_type=F32)
    v = jnp.dot(hb, wqkvg_ref[:, 2 * GLA_DK_TOT:2 * GLA_DK_TOT + GLA_DV_TOT],
                preferred_element_type=F32).astype(BF16)
    log_a = _gla_gate(hb, wlr_ref, wgk_ref, bgk_ref)
    tril = tril_ref[...]
    parts = _split3(log_a)
    b = jnp.concatenate(
        [sum(jnp.dot(tril, p[r:r + GLA_CUM_BLOCK], preferred_element_type=F32) for p in parts)
         for r in range(0, tt, GLA_CUM_BLOCK)], axis=0)
    row = lax.broadcasted_iota(jnp.int32, (GLA_CHUNK, GLA_CHUNK), 0)
    col = lax.broadcasted_iota(jnp.int32, (GLA_CHUNK, GLA_CHUNK), 1)
    causal = row >= col
    n_chunks = tt // GLA_CHUNK
    for c in range(n_chunks):
        rows = slice(c * GLA_CHUNK, (c + 1) * GLA_CHUNK)
        last = (c + 1) * GLA_CHUNK - 1
        for hd in range(GLA_HEADS):
            kc = slice(hd * GLA_DK, (hd + 1) * GLA_DK)
            vc = slice(hd * GLA_DV, (hd + 1) * GLA_DV)
            bb = b[rows, kc]
            b_last = b[last:last + 1, kc]
            q_dec = (q[rows, kc] * jnp.exp(bb)).astype(BF16)
            k_inv = (k[rows, kc] * jnp.exp(-bb)).astype(BF16)
            k_end = (k[rows, kc] * jnp.exp(b_last - bb)).astype(BF16)
            att = jnp.where(causal, _dot_nt(q_dec, k_inv), 0.0).astype(BF16)
            qd_scr[rows, kc] = q_dec
            o_scr[rows, vc] = jnp.dot(att, v[rows, vc], preferred_element_type=F32)
            dst_scr[c * GLA_HEADS + hd] = lax.dot_general(
                v[rows, vc], k_end, (((0,), (0,)), ((), ())), preferred_element_type=F32)
    states = [st_scr[hd] for hd in range(GLA_HEADS)]
    for c in range(n_chunks):
        rows = slice(c * GLA_CHUNK, (c + 1) * GLA_CHUNK)
        last = (c + 1) * GLA_CHUNK - 1
        for hd in range(GLA_HEADS):
            kc = slice(hd * GLA_DK, (hd + 1) * GLA_DK)
            vc = slice(hd * GLA_DV, (hd + 1) * GLA_DV)
            o_scr[rows, vc] += _dot_nt(qd_scr[rows, kc], states[hd].astype(BF16))
            states[hd] = states[hd] * jnp.exp(b[last:last + 1, kc]) + dst_scr[c * GLA_HEADS + hd]
    for hd in range(GLA_HEADS):
        st_scr[hd] = states[hd]

    @pl.when(j == pl.num_programs(1) - 1)
    def _():
        for hd in range(GLA_HEADS):
            st_ref[hd] = st_scr[hd].T

    go = jnp.dot(hb, wqkvg_ref[:, 2 * GLA_DK_TOT + GLA_DV_TOT:], preferred_element_type=F32)
    y = jnp.dot(_gla_out(o_scr[...], go, gng_ref[...]), wout_ref[...], preferred_element_type=F32)
    x1 = x + g1 * _rms(y, ng[1:2])
    x1_ref[...] = x1
    _ffn_prep(x1, ng, sh2, sc2, rw_ref, h_ref, lg_ref)


def _gla_mixer(x2d, routed, hp, mod3_prev, ng_prev, shared_prev, mod3, batch, seq, ng, wqkvg, wlr, wgk, bgk,
               gng, wout, rw_t, ffn_rows=None):
    tt = GLA_TILE
    tpb = seq // tt
    n = x2d.shape[0]
    idx = jnp.arange(GLA_CUM_BLOCK)
    tril = ((idx[:, None] >= idx[None, :]) &
            (idx[:, None] // GLA_CHUNK == idx[None, :] // GLA_CHUNK)).astype(BF16)
    row_map = lambda b, j: (b * tpb + j, 0)
    mod_map = lambda b, j: (b, 0, 0)
    consts = (ng, wqkvg, wlr, wgk, bgk, tril, gng, wout, rw_t)
    prev_consts = (ng_prev,) + tuple(shared_prev)
    ffn_shapes, alias_bufs, _ = _ffn_out(n, ffn_rows, None)
    args = [x2d, routed, hp, mod3_prev, *prev_consts, mod3, *consts]
    return pl.pallas_call(
        functools.partial(_gla_body, tt=tt, n_alias=len(alias_bufs)),
        grid=(batch, tpb),
        in_specs=[pl.BlockSpec((tt, D), row_map), pl.BlockSpec((tt, D), row_map),
                  pl.BlockSpec((tt, DP), row_map), pl.BlockSpec((None, 1, 6 * D), mod_map)] +
                 [_const_spec(a.shape) for a in prev_consts] +
                 [pl.BlockSpec((None, 1, 6 * D), mod_map)] +
                 [_const_spec(a.shape) for a in consts] +
                 [pl.BlockSpec(memory_space=pl.ANY)] * len(alias_bufs),
        out_specs=[pl.BlockSpec((tt, D), row_map), pl.BlockSpec((tt, DP), row_map),
                   pl.BlockSpec((N_EXPERTS, tt), lambda b, j: (0, b * tpb + j)),
                   pl.BlockSpec((None, GLA_HEADS, GLA_DK, GLA_DV), lambda b, j: (b, 0, 0, 0))],
        out_shape=[jax.ShapeDtypeStruct((n, D), F32)] + ffn_shapes +
                  [jax.ShapeDtypeStruct((batch, GLA_HEADS, GLA_DK, GLA_DV), F32)],
        input_output_aliases={len(args) + i: 1 + i for i in range(len(alias_bufs))},
        scratch_shapes=[pltpu.VMEM((GLA_HEADS, GLA_DV, GLA_DK), F32),
                        pltpu.VMEM((tt, GLA_DV_TOT), F32),
                        pltpu.VMEM((tt, GLA_DK_TOT), BF16),
                        pltpu.VMEM((tt // GLA_CHUNK * GLA_HEADS, GLA_DV, GLA_DK), F32)],
        compiler_params=_cparams("parallel", "arbitrary"),
        name="gla_mixer",
    )(*args, *alias_bufs)


def _gla1_proj_body(x_ref, mod_ref, ng_ref, wqkvg_ref, wlr_ref, wgk_ref, bgk_ref,
                    q_ref, k_ref, v_ref, go_ref, dec_ref):
    sh1, sc1, _, _, _, _ = _mod_slices(mod_ref)
    ng = ng_ref[...]
    hb = (_rms(x_ref[...], ng[0:1]) * (1.0 + sc1) + sh1).astype(BF16)
    proj = jnp.dot(hb, wqkvg_ref[...], preferred_element_type=F32)
    q_ref[...] = proj[:, :GLA_DK_TOT] * (GLA_DK ** -0.5)
    k_ref[...] = proj[:, GLA_DK_TOT:2 * GLA_DK_TOT]
    v_ref[...] = proj[:, 2 * GLA_DK_TOT:2 * GLA_DK_TOT + GLA_DV_TOT]
    go_ref[...] = proj[:, 2 * GLA_DK_TOT + GLA_DV_TOT:]
    dec_ref[...] = jnp.exp(_gla_gate(hb, wlr_ref, wgk_ref, bgk_ref))


GLA1_TOK = 8


def _gla1_state_body(st_ref, qc_ref, kc_ref, dc_ref, v_ref, nst_ref, o_ref):
    v = v_ref[...]
    for i in range(GLA1_TOK):
        for hd in range(GLA_HEADS):
            vrow = v[i:i + 1, hd * GLA_DV:(hd + 1) * GLA_DV]
            s_new = dc_ref[hd][:, i:i + 1] * st_ref[i, hd] + kc_ref[hd][:, i:i + 1] * vrow
            nst_ref[i, hd] = s_new
            o_ref[i:i + 1, hd * GLA_DV:(hd + 1) * GLA_DV] = jnp.sum(
                qc_ref[hd][:, i:i + 1] * s_new, axis=0, keepdims=True)


def _gla1_out_body(x_ref, o_ref, go_ref, mod_ref, ng_ref, gng_ref, wout_ref, rw_ref, *rest):
    x1_ref, h_ref, lg_ref = rest[-3:]
    _, _, g1, sh2, sc2, _ = _mod_slices(mod_ref)
    ng = ng_ref[...]
    y = jnp.dot(_gla_out(o_ref[...], go_ref[...], gng_ref[...]), wout_ref[...], preferred_element_type=F32)
    x1 = x_ref[...] + g1 * _rms(y, ng[1:2])
    x1_ref[...] = x1
    _ffn_prep(x1, ng, sh2, sc2, rw_ref, h_ref, lg_ref)


def _gla_mixer_one(x2d, mod2, state, ng, wqkvg, wlr, wgk, bgk, gng, wout, rw_t, into=None):
    n = x2d.shape[0]
    consts = (ng, wqkvg, wlr, wgk, bgk)
    q, k, v, go, dec = pl.pallas_call(
        _gla1_proj_body,
        in_specs=[_const_spec(a.shape) for a in (x2d, mod2) + consts],
        out_specs=[_const_spec((n, GLA_DK_TOT)), _const_spec((n, GLA_DK_TOT)), _const_spec((n, GLA_DV_TOT)),
                   _const_spec((n, GLA_DV_TOT)), _const_spec((n, GLA_DK_TOT))],
        out_shape=[jax.ShapeDtypeStruct((n, GLA_DK_TOT), F32), jax.ShapeDtypeStruct((n, GLA_DK_TOT), F32),
                   jax.ShapeDtypeStruct((n, GLA_DV_TOT), F32), jax.ShapeDtypeStruct((n, GLA_DV_TOT), F32),
                   jax.ShapeDtypeStruct((n, GLA_DK_TOT), F32)],
        grid=(1,),
        compiler_params=_cparams("arbitrary"),
        name="gla1_proj",
    )(x2d, mod2, *consts)

    def cols(a):
        return a.reshape(n // GLA1_TOK, GLA1_TOK, GLA_HEADS, GLA_DK).transpose(0, 2, 3, 1)

    col_spec = pl.BlockSpec((None, GLA_HEADS, GLA_DK, GLA1_TOK), lambda i: (i, 0, 0, 0))
    st_spec = pl.BlockSpec((GLA1_TOK, GLA_HEADS, GLA_DK, GLA_DV), lambda i: (i, 0, 0, 0))
    new_state, o = pl.pallas_call(
        _gla1_state_body,
        grid=(n // GLA1_TOK,),
        in_specs=[st_spec, col_spec, col_spec, col_spec, pl.BlockSpec((GLA1_TOK, GLA_DV_TOT), lambda i: (i, 0))],
        out_specs=[st_spec, pl.BlockSpec((GLA1_TOK, GLA_DV_TOT), lambda i: (i, 0))],
        out_shape=[jax.ShapeDtypeStruct(state.shape, F32), jax.ShapeDtypeStruct((n, GLA_DV_TOT), F32)],
        compiler_params=_cparams("parallel"),
        name="gla1_state",
    )(state, cols(q), cols(k), cols(dec), v)

    consts = (mod2, ng, gng, wout, rw_t)
    ffn_shapes, alias_bufs, oblk = _ffn_out(n, None, into)
    n_in = 3 + len(consts)
    x1, h, lg = pl.pallas_call(
        _gla1_out_body,
        grid=(1,),
        in_specs=[_const_spec(a.shape) for a in (x2d, o, go) + consts] +
                 [pl.BlockSpec(memory_space=pl.ANY)] * len(alias_bufs),
        out_specs=[_const_spec((n, D)), pl.BlockSpec((n, DP), lambda i: (oblk, 0)),
                   pl.BlockSpec((N_EXPERTS, n), lambda i: (0, oblk))],
        out_shape=[jax.ShapeDtypeStruct((n, D), F32)] + ffn_shapes,
        input_output_aliases={n_in + i: 1 + i for i in range(len(alias_bufs))},
        compiler_params=_cparams("arbitrary"),
        name="gla1_out",
    )(x2d, o, go, *consts, *alias_bufs)
    return x1, h, lg, new_state


def _moe_routed(h, lg, n, router_bias, wg, wu, wd):
    n_pad = h.shape[0]
    eid, rank, wts, counts = _router(lg, router_bias, n)
    tile_count = ((counts[:, 0] + EXPERT_TILE - 1) // EXPERT_TILE).astype(jnp.int32)
    tile_end = jnp.cumsum(tile_count).astype(jnp.int32)
    tile_first = tile_end - tile_count
    off = tile_first * EXPERT_TILE
    p_alloc = TOP_K * n_pad + N_EXPERTS * EXPERT_TILE
    dest = _dest(off, eid, rank, n, p_alloc - 1)
    dest_w = dest.reshape(TOP_K, n_pad // DISPATCH_W, DISPATCH_W).transpose(1, 0, 2)
    xs = _sc_dispatch(h, dest_w, p_alloc)
    ys = _experts(xs, tile_first, tile_count, tile_end[-1:], wg, wu, wd)
    dest_tm = dest.T.reshape(n_pad // SUM_W, SUM_PARTS, SUM_W * TOP_K // SUM_PARTS)
    w_lanes = jnp.repeat(wts.T, SC_LANES, axis=1)
    return _sc_gather_sum(ys, dest_tm, w_lanes)


def kernel(x_prompt, x_sample, state_gla, c_prompt, c_sample, norm_g, ada_w, ada_b, gm_w_in, gm_b_in,
           gm_ln_g, gm_ln_b, gm_w_s, gm_b_s, gm_w_out, gla_w_in, gla_w_gk, gla_b_gk, gla_norm_g,
           gla_w_out, router_w, router_bias, exp_w_gate, exp_w_up, exp_w_down, sh_w_gate, sh_w_up,
           sh_w_down):
    batch, seq, _ = x_prompt.shape
    n_s = x_sample.shape[0]
    n_p = batch * seq
    tpb = seq // MIX_TILE
    xp = x_prompt.reshape(n_p, D)
    xs = x_sample.reshape(n_s, D)

    mod = _ada(jnp.concatenate([c_prompt, c_sample], axis=0), ada_w, ada_b)
    mod_p = [mod[i, :batch].reshape(batch, 1, 6 * D) for i in range(2)]
    mod_s = [mod[i, batch:] for i in range(2)]
    rw_t = [jnp.concatenate(_split3(router_w[i].T), axis=0) for i in range(2)]

    ws_causal = jnp.tril(gm_w_s[0]).astype(BF16)
    bs_cols = gm_b_s[0].T
    eye = jnp.eye(GM_CHUNK, dtype=F32)
    ws_first = (gm_w_s[0][:, 0, 0][:, None, None] * eye).astype(BF16)
    bs_first = jnp.broadcast_to(gm_b_s[0][:, 0][None, :], (GM_CHUNK, GM_GROUPS))
    gm_args = (norm_g[0], gm_w_in[0].astype(BF16), gm_b_in[0].reshape(1, -1), gm_ln_g[0].reshape(1, -1),
               gm_ln_b[0].reshape(1, -1))
    wout0 = gm_w_out[0].astype(BF16)
    shared = [(sh_w_gate[i].astype(BF16), sh_w_up[i].astype(BF16), sh_w_down[i].astype(BF16))
              for i in range(2)]
    n_qkvg = 2 * GLA_DK_TOT + 2 * GLA_DV_TOT
    wqkvg = gla_w_in[0][:, :n_qkvg].astype(BF16)
    wlr = jnp.pad(gla_w_in[0][:, n_qkvg:], ((0, 0), (0, LANES - GLA_GATE_RANK))).astype(BF16)
    wgk = jnp.pad(gla_w_gk[0], ((0, LANES - GLA_GATE_RANK), (0, 0))).astype(BF16)
    gla_args = (norm_g[1], wqkvg, wlr, wgk, gla_b_gk[0].reshape(1, -1), gla_norm_g[0].reshape(1, -1),
                gla_w_out[0].astype(BF16), rw_t[1])
    experts_f32 = (exp_w_gate, exp_w_up, exp_w_down)

    half = batch // 2
    streams = [(0, half, False), (half, batch - half, True)]
    st = [dict() for _ in streams]

    experts = []
    for layer, (s, (b0, nb, with_new)) in enumerate(zip(st, streams)):
        s["mod_p"] = [mod_p[i][b0:b0 + nb] for i in range(2)]
        s["n"] = nb * seq
        s["n_all"] = s["n"] + (n_s if with_new else 0)
        s["n_pad"] = -(-s["n_all"] // TOKEN_PAD) * TOKEN_PAD
        s["x1p"], s["h"], s["lg"], *w16 = _gmlp_mixer(
            xp, b0 * tpb, s["n"], s["mod_p"][0], False, MIX_TILE, tpb, *gm_args, ws_causal, bs_cols, wout0,
            rw_t[0], emit_v=False, cast_w=(layer, *experts_f32), ffn_rows=s["n_pad"])
        experts.append(w16)
        if with_new:
            s["x1s"], s["h"], s["lg"], v_rows = _gmlp_mixer(
                xs, 0, n_s, mod_s[0], True, n_s, 1, *gm_args, ws_first, bs_first, wout0, rw_t[0], emit_v=True,
                into=(s["h"], s["lg"], s["n"] // n_s))
    for s, (b0, nb, with_new) in zip(st, streams):
        s["routed0"] = _moe_routed(s["h"], s["lg"], s["n_all"], router_bias[0], *experts[0])
        if with_new:
            s["x2s"] = _combine(s["x1s"], s["routed0"], s["h"], s["n"] // n_s, mod_s[0], True, n_s, 1,
                                norm_g[0], *shared[0])
    for s, (b0, nb, with_new) in zip(st, streams):
        s["x3p"], h1, lg1, s["st_p"] = _gla_mixer(s["x1p"], s["routed0"], s["h"], s["mod_p"][0], norm_g[0],
                                                  shared[0], s["mod_p"][1], nb, seq, *gla_args,
                                                  ffn_rows=s["n_pad"])
        if with_new:
            s["x3s"], h1, lg1, st_s = _gla_mixer_one(s["x2s"], mod_s[1], state_gla[:, 0], *gla_args,
                                                     into=(h1, lg1, s["n"] // n_s))
        s["h"], s["lg"] = h1, lg1
    y_prompt = None
    for s, (b0, nb, with_new) in zip(st, streams):
        routed = _moe_routed(s["h"], s["lg"], s["n_all"], router_bias[1], *experts[1])
        y_prompt = _combine(s["x3p"], routed, s["h"], 0, s["mod_p"][1], False, MIX_TILE, tpb, norm_g[1],
                            *shared[1], out_rows=n_p, out_blk0=b0 * tpb, out_buf=y_prompt)
        if with_new:
            y_new = _combine(s["x3s"], routed, s["h"], s["n"] // n_s, mod_s[1], True, n_s, 1, norm_g[1],
                             *shared[1])
    st_p = jnp.concatenate([s["st_p"] for s in st], axis=0)

    return (y_prompt.reshape(batch, seq, D), y_new.reshape(n_s, 1, D), st_p[:, None], st_s[:, None],
            v_rows.reshape(n_s, 1, 1, GM_HALF))
```

```python
import functools
import math

import jax
import jax.numpy as jnp
from jax import lax
from jax.experimental import pallas as pl
from jax.experimental.pallas import tpu as pltpu
from jax.experimental.pallas import tpu_sc as plsc

F32 = jnp.float32
BF16 = jnp.bfloat16

D = 1024
DP = D // 2
GM_CHUNK = 128
GM_HALF = 2 * D
GM_GROUPS = 8
GM_GROUP_DIM = GM_HALF // GM_GROUPS
GLA_HEADS = 4
GLA_DK = 128
GLA_DV = 256
GLA_DK_TOT = GLA_HEADS * GLA_DK
GLA_DV_TOT = GLA_HEADS * GLA_DV
GLA_GATE_RANK = 16
GLA_GATE_NORMALIZER = 16.0
GLA_CHUNK = 64
N_EXPERTS = 64
TOP_K = 8
N_EXPERT_GROUPS = 8
GROUP_SIZE = N_EXPERTS // N_EXPERT_GROUPS
TOPK_GROUPS = 4
EXPERT_DIM = D // 4
ROUTED_SCALE = 2.5
NORM_EPS = 1e-6
LN_EPS = 1e-5

LANES = 128
VMEM_LIMIT = 56 * 1024 * 1024

MIX_TILE = 256
GLA_TILE = 512
GLA_CUM_BLOCK = 256
GM_COL_BLOCK = 512
ROUTER_TILE = 1024
EXPERT_TILE = 544
EXPERT_X_SLOTS = 6
EXPERT_AHEAD = EXPERT_X_SLOTS - 2
EXPERT_Y_SLOTS = 4
SC_WORKERS = 32
DISPATCH_W = 32
SC_LANES = 16
SUM_W = 16
SUM_PARTS = 4
SUM_UNROLL = 4
TOKEN_PAD = SC_WORKERS * DISPATCH_W


def _cparams(*sem):
    return pltpu.CompilerParams(dimension_semantics=sem, vmem_limit_bytes=VMEM_LIMIT)


def _rms(x, g):
    return x * lax.rsqrt(jnp.mean(x * x, axis=-1, keepdims=True) + NORM_EPS) * g


def _silu(x):
    return x * (1.0 / (1.0 + jnp.exp(-x)))


def _gelu(x):
    return 0.5 * x * (1.0 + lax.erf(x * (1.0 / math.sqrt(2.0))))


def _bdot(a, b):
    return jnp.dot(a.astype(BF16), b.astype(BF16), preferred_element_type=F32)


def _dot_nt(a, b, precision=None):
    return lax.dot_general(a, b, (((1,), (1,)), ((), ())), preferred_element_type=F32,
                           precision=precision)


def _mod_slices(mod_ref):
    return [mod_ref[:, i * D:(i + 1) * D] for i in range(6)]


HI_HALF = -65536


def _pack_rows(x):
    lo = lax.bitcast_convert_type(x[:, :DP].astype(BF16).astype(F32), jnp.int32)
    hi = lax.bitcast_convert_type(x[:, DP:].astype(BF16).astype(F32), jnp.int32)
    return lax.shift_right_logical(lo, 16) | (hi & HI_HALF)


def _unpack_rows(p):
    lo = lax.bitcast_convert_type(lax.shift_left(p, 16), F32)
    hi = lax.bitcast_convert_type(p & HI_HALF, F32)
    return lo, hi


def _ffn_prep(x1, ng, sh2, sc2, rw_ref, h_ref, lg_ref):
    hffn = _rms(x1, ng[2:3]) * (1.0 + sc2) + sh2
    h_ref[...] = _pack_rows(hffn)
    lg3 = _dot_nt(rw_ref[...], hffn.astype(BF16))
    lg_ref[...] = lg3[:N_EXPERTS] + lg3[N_EXPERTS:2 * N_EXPERTS] + lg3[2 * N_EXPERTS:]


def _ada_body(c_ref, w_ref, b_ref, o_ref):
    c = c_ref[...]
    o_ref[...] = _bdot(_silu(c), w_ref[...]) + b_ref[...]


def _ada(c, ada_w, ada_b):
    n = c.shape[0]
    depth = ada_w.shape[0]
    tn = 1536
    return pl.pallas_call(
        _ada_body,
        grid=(depth, 6 * D // tn),
        in_specs=[pl.BlockSpec((n, D), lambda l, j: (0, 0)),
                  pl.BlockSpec((None, D, tn), lambda l, j: (l, 0, j)),
                  pl.BlockSpec((None, 1, tn), lambda l, j: (l, 0, j))],
        out_specs=pl.BlockSpec((None, n, tn), lambda l, j: (l, 0, j)),
        out_shape=jax.ShapeDtypeStruct((depth, n, 6 * D), F32),
        compiler_params=_cparams("parallel", "parallel"),
        name="ada_mod",
    )(c, ada_w, ada_b.reshape(depth, 1, 6 * D))


def _mod_spec(per_row, tt, tiles_per_batch):
    if per_row:
        return pl.BlockSpec((tt, 6 * D), lambda i: (i, 0))
    return pl.BlockSpec((None, 1, 6 * D), lambda i: (i // tiles_per_batch, 0, 0))


def _const_spec(shape):
    zeros = (0,) * len(shape)
    return pl.BlockSpec(shape, lambda *_: zeros)


def _gmlp_body(x_ref, mod_ref, ng_ref, win_ref, bin_ref, lng_ref, lnb_ref, ws_ref, bs_ref, wout_ref,
               rw_ref, *rest, n_chunks, emit_v, cast_w, n_alias):
    rest = list(rest)
    w32_refs = [rest.pop(0) for _ in range(3)] if cast_w else []
    rest = rest[n_alias:]
    x1_ref, h_ref, lg_ref = rest[:3]
    rest = rest[3:]
    v_ref = rest.pop(0) if emit_v else None
    w16_refs = [rest.pop(0) for _ in range(3)] if cast_w else []
    um_ref, z_ref = rest
    for src, dst in zip(w32_refs, w16_refs):
        dst[...] = src[...].astype(BF16)
    sh1, sc1, g1, sh2, sc2, _ = _mod_slices(mod_ref)
    ng = ng_ref[...]
    x = x_ref[...]
    hb = (_rms(x, ng[0:1]) * (1.0 + sc1) + sh1).astype(BF16)
    for cb in range(2 * GM_HALF // GM_COL_BLOCK):
        cols = slice(cb * GM_COL_BLOCK, (cb + 1) * GM_COL_BLOCK)
        z_ref[:, cols] = _gelu(jnp.dot(hb, win_ref[:, cols], preferred_element_type=F32) + bin_ref[:, cols])
    u = z_ref[:, :GM_HALF]
    v = z_ref[:, GM_HALF:]
    mu = jnp.mean(v, axis=-1, keepdims=True)
    vc = v - mu
    var = jnp.mean(vc * vc, axis=-1, keepdims=True)
    v = vc * lax.rsqrt(var + LN_EPS) * lng_ref[...] + lnb_ref[...]
    if emit_v:
        v_ref[...] = v
    vb = v.astype(BF16)
    for c in range(n_chunks):
        rows = slice(c * GM_CHUNK, (c + 1) * GM_CHUNK)
        for g in range(GM_GROUPS):
            cols = slice(g * GM_GROUP_DIM, (g + 1) * GM_GROUP_DIM)
            mixed = jnp.dot(ws_ref[g], vb[rows, cols], preferred_element_type=F32) + bs_ref[:, g:g + 1]
            um_ref[rows, cols] = (u[rows, cols] * mixed).astype(BF16)
    y = jnp.dot(um_ref[...], wout_ref[...], preferred_element_type=F32)
    x1 = x + g1 * _rms(y, ng[1:2])
    x1_ref[...] = x1
    _ffn_prep(x1, ng, sh2, sc2, rw_ref, h_ref, lg_ref)


def _ffn_out(n, ffn_rows, into):
    rows = ffn_rows or n
    oblk = 0
    bufs = []
    if into is not None:
        *bufs, oblk = into
        rows = bufs[0].shape[0]
    elif rows != n:
        bufs = [jnp.zeros((rows, DP), jnp.int32), jnp.zeros((N_EXPERTS, rows), F32)]
    shapes = [jax.ShapeDtypeStruct((rows, DP), jnp.int32), jax.ShapeDtypeStruct((N_EXPERTS, rows), F32)]
    return shapes, bufs, oblk


def _gmlp_mixer(x2d, blk0, n, mod, per_row, tt, tiles_per_batch, ng, win, b_in, ln_g, ln_b, ws, bs, wout,
                rw_t, emit_v, cast_w=None, ffn_rows=None, into=None):
    steps = n // tt
    ffn_shapes, alias_bufs, oblk = _ffn_out(n, ffn_rows, into)
    out_shape = [jax.ShapeDtypeStruct((n, D), F32)] + ffn_shapes
    out_specs = [pl.BlockSpec((tt, D), lambda i: (i, 0)), pl.BlockSpec((tt, DP), lambda i: (i + oblk, 0)),
                 pl.BlockSpec((N_EXPERTS, tt), lambda i: (0, i + oblk))]
    if emit_v:
        out_shape.append(jax.ShapeDtypeStruct((n, GM_HALF), F32))
        out_specs.append(pl.BlockSpec((tt, GM_HALF), lambda i: (i, 0)))

    def one_buffer(a):
        zeros = (0,) * a.ndim
        return pl.BlockSpec(a.shape, lambda *_: zeros, pipeline_mode=pl.Buffered(1))

    consts = (ng, win, b_in, ln_g, ln_b, ws, bs, wout, rw_t)
    in_specs = [pl.BlockSpec((tt, D), lambda i: (i + blk0, 0)), _mod_spec(per_row, tt, tiles_per_batch)]
    in_specs += [one_buffer(a) for a in consts]
    args = [x2d, mod, *consts]
    if cast_w is not None:
        layer, *w_all = cast_w
        per_step = N_EXPERTS // steps
        for w in w_all:
            blk = (None, per_step) + w.shape[2:]
            in_specs.append(pl.BlockSpec(blk, lambda i: (layer, i, 0, 0)))
            out_specs.append(pl.BlockSpec(blk[1:], lambda i: (i, 0, 0)))
            out_shape.append(jax.ShapeDtypeStruct(w.shape[1:], BF16))
            args.append(w)
    aliases = {len(args) + i: 1 + i for i in range(len(alias_bufs))}
    in_specs += [pl.BlockSpec(memory_space=pl.ANY)] * len(alias_bufs)
    args += alias_bufs
    return pl.pallas_call(
        functools.partial(_gmlp_body, n_chunks=tt // GM_CHUNK, emit_v=emit_v, cast_w=cast_w is not None,
                          n_alias=len(alias_bufs)),
        grid=(steps,),
        in_specs=in_specs,
        out_specs=out_specs,
        out_shape=out_shape,
        input_output_aliases=aliases,
        scratch_shapes=[pltpu.VMEM((tt, GM_HALF), BF16), pltpu.VMEM((tt, 2 * GM_HALF), F32)],
        compiler_params=_cparams("parallel"),
        name="gmlp_mixer_rows" if per_row else "gmlp_mixer",
    )(*args)


def _combine_body(x_ref, y_ref, h_ref, mod_ref, ng_ref, swg_ref, swu_ref, swd_ref, *rest):
    o_ref = rest[-1]
    o_ref[...] = _channel_mix_residual(x_ref[...], y_ref[...], h_ref, mod_ref[:, 5 * D:6 * D],
                                       ng_ref[3:4, :], swg_ref, swu_ref, swd_ref)


def _combine(x2d, routed, hp, blk0, mod, per_row, tt, tiles_per_batch, ng, swg, swu, swd,
             out_rows=None, out_blk0=0, out_buf=None):
    n = x2d.shape[0]
    in_specs = [pl.BlockSpec((tt, D), lambda i: (i, 0)),
                pl.BlockSpec((tt, D), lambda i: (i + blk0, 0)),
                pl.BlockSpec((tt, DP), lambda i: (i + blk0, 0)),
                _mod_spec(per_row, tt, tiles_per_batch),
                _const_spec(ng.shape), _const_spec(swg.shape), _const_spec(swu.shape),
                _const_spec(swd.shape)]
    args = [x2d, routed, hp, mod, ng, swg, swu, swd]
    aliases = {}
    if out_buf is not None:
        in_specs.append(pl.BlockSpec(memory_space=pl.ANY))
        aliases = {len(args): 0}
        args.append(out_buf)
    return pl.pallas_call(
        _combine_body,
        grid=(n // tt,),
        in_specs=in_specs,
        out_specs=pl.BlockSpec((tt, D), lambda i: (i + out_blk0, 0)),
        out_shape=jax.ShapeDtypeStruct((out_rows or n, D), F32),
        input_output_aliases=aliases,
        compiler_params=_cparams("parallel"),
        name="combine_rows" if per_row else "combine",
    )(*args)


def _router_body(lg_ref, bias_ref, tri_ref, eid_ref, rank_ref, wts_ref, cnt_ref, carry_ref, *, n_real):
    step = pl.program_id(0)

    @pl.when(step == 0)
    def _():
        carry_ref[...] = jnp.zeros_like(carry_ref)

    lg = lg_ref[...]
    tn = lg.shape[1]
    real = (step * tn + lax.broadcasted_iota(jnp.int32, (1, tn), 1)) < n_real
    lg = jnp.where(real, lg, 0.0)
    scores = 1.0 / (1.0 + jnp.exp(-lg))
    sel = scores + bias_ref[...]
    neg = -jnp.inf
    sub8 = lax.broadcasted_iota(jnp.int32, (GROUP_SIZE, tn), 0)
    gsub = lax.broadcasted_iota(jnp.int32, (N_EXPERT_GROUPS, tn), 0)
    gs = jnp.zeros((N_EXPERT_GROUPS, tn), F32)
    for g in range(N_EXPERT_GROUPS):
        blk = sel[g * GROUP_SIZE:(g + 1) * GROUP_SIZE, :]
        m1 = jnp.max(blk, axis=0, keepdims=True)
        i1 = jnp.min(jnp.where(blk == m1, sub8, GROUP_SIZE), axis=0, keepdims=True)
        m2 = jnp.max(jnp.where(sub8 == i1, neg, blk), axis=0, keepdims=True)
        gs = jnp.where(gsub == g, m1 + m2, gs)
    gmask = jnp.zeros((N_EXPERT_GROUPS, tn), jnp.bool_)
    for _ in range(TOPK_GROUPS):
        m = jnp.max(gs, axis=0, keepdims=True)
        i = jnp.min(jnp.where(gs == m, gsub, N_EXPERT_GROUPS), axis=0, keepdims=True)
        hit = gsub == i
        gmask = jnp.logical_or(gmask, hit)
        gs = jnp.where(hit, neg, gs)
    gmaskf = gmask.astype(F32)
    blocks = []
    for g in range(N_EXPERT_GROUPS):
        keep = jnp.broadcast_to(gmaskf[g:g + 1, :], (GROUP_SIZE, tn)) > 0.5
        blocks.append(jnp.where(keep, sel[g * GROUP_SIZE:(g + 1) * GROUP_SIZE, :], neg))
    msel = jnp.concatenate(blocks, axis=0)
    esub = lax.broadcasted_iota(jnp.int32, (N_EXPERTS, tn), 0)
    chosen = jnp.zeros((N_EXPERTS, tn), jnp.bool_)
    picks = []
    for _ in range(TOP_K):
        m = jnp.max(msel, axis=0, keepdims=True)
        i = jnp.min(jnp.where(msel == m, esub, N_EXPERTS), axis=0, keepdims=True)
        hit = esub == i
        picks.append(i)
        chosen = jnp.logical_or(chosen, hit)
        msel = jnp.where(hit, neg, msel)
    w = jnp.where(chosen, scores, 0.0)
    w = w / jnp.sum(w, axis=0, keepdims=True) * ROUTED_SCALE
    counted = jnp.where(jnp.logical_and(chosen, real), 1.0, 0.0)
    incl = jnp.dot(counted.astype(BF16), tri_ref[...], preferred_element_type=F32)
    rank_full = carry_ref[:, 0:1] + incl - 1.0
    ksub = lax.broadcasted_iota(jnp.int32, (TOP_K, tn), 0)
    eid = jnp.zeros((TOP_K, tn), jnp.int32)
    rank = jnp.zeros((TOP_K, tn), F32)
    wts = jnp.zeros((TOP_K, tn), F32)
    for k in range(TOP_K):
        hit = esub == picks[k]
        eid = jnp.where(ksub == k, picks[k], eid)
        rank = jnp.where(ksub == k, jnp.sum(jnp.where(hit, rank_full, 0.0), axis=0, keepdims=True), rank)
        wts = jnp.where(ksub == k, jnp.sum(jnp.where(hit, w, 0.0), axis=0, keepdims=True), wts)
    eid_ref[...] = eid
    rank_ref[...] = rank.astype(jnp.int32)
    wts_ref[...] = wts
    carry = carry_ref[...] + incl[:, tn - 1:tn]
    carry_ref[...] = carry
    cnt_ref[...] = carry.astype(jnp.int32)


def _router(lg_t, bias, n_real):
    n = lg_t.shape[1]
    tn = ROUTER_TILE
    idx = jnp.arange(tn)
    tri = (idx[:, None] <= idx[None, :]).astype(BF16)
    kspec = pl.BlockSpec((TOP_K, tn), lambda i: (0, i))
    return pl.pallas_call(
        functools.partial(_router_body, n_real=n_real),
        grid=(n // tn,),
        in_specs=[pl.BlockSpec((N_EXPERTS, tn), lambda i: (0, i)), _const_spec((N_EXPERTS, 1)),
                  _const_spec((tn, tn))],
        out_specs=[kspec, kspec, kspec, _const_spec((N_EXPERTS, LANES))],
        out_shape=[jax.ShapeDtypeStruct((TOP_K, n), jnp.int32), jax.ShapeDtypeStruct((TOP_K, n), jnp.int32),
                   jax.ShapeDtypeStruct((TOP_K, n), F32), jax.ShapeDtypeStruct((N_EXPERTS, LANES), jnp.int32)],
        scratch_shapes=[pltpu.VMEM((N_EXPERTS, LANES), F32)],
        compiler_params=_cparams("arbitrary"),
        name="router",
    )(lg_t, bias.reshape(N_EXPERTS, 1), tri)


def _dest_body(off_ref, eid_ref, rank_ref, dest_ref, *, n_real, last_row):
    eid = eid_ref[...]
    base = jnp.zeros(eid.shape, jnp.int32)
    for e in range(N_EXPERTS):
        base = jnp.where(eid == e, off_ref[e], base)
    tok = lax.broadcasted_iota(jnp.int32, eid.shape, 1)
    slot = lax.broadcasted_iota(jnp.int32, eid.shape, 0)
    unused = last_row - ((tok - n_real) * TOP_K + slot)
    dest_ref[...] = jnp.where(tok < n_real, base + rank_ref[...], unused)


def _dest(off, eid, rank, n_real, last_row):
    spec = pl.BlockSpec(eid.shape, lambda i, off_ref: (0, 0))
    return pl.pallas_call(
        functools.partial(_dest_body, n_real=n_real, last_row=last_row),
        grid_spec=pltpu.PrefetchScalarGridSpec(num_scalar_prefetch=1, grid=(1,), in_specs=[spec, spec],
                                               out_specs=spec),
        out_shape=jax.ShapeDtypeStruct(eid.shape, jnp.int32),
        compiler_params=_cparams("arbitrary"),
        name="dest_rows",
    )(off, eid, rank)


def _sc_mesh():
    return plsc.VectorSubcoreMesh(core_axis_name="core", subcore_axis_name="subcore")


def _sc_dispatch(hp, dest_w, p_alloc):
    n = hp.shape[0]
    w = dest_w.shape[2]

    @functools.partial(pl.kernel, out_type=jax.ShapeDtypeStruct((p_alloc, DP), jnp.int32), mesh=_sc_mesh(),
                       name="sc_dispatch")
    def run(hp_hbm, dest_hbm, xs_hbm):
        def body(x_vmem, i_vmem):
            for k in range(TOP_K):
                pltpu.sync_copy(x_vmem, xs_hbm.at[i_vmem.at[k]])

        pltpu.emit_pipeline(
            body,
            grid=(n // w,),
            in_specs=[pl.BlockSpec((w, DP), lambda i: (i, 0)),
                      pl.BlockSpec((None, TOP_K, w), lambda i: (i, 0, 0))],
            out_specs=[],
            core_axis_name=("core", "subcore"),
            dimension_semantics=(pltpu.PARALLEL,),
        )(hp_hbm, dest_hbm)

    return run(hp, dest_w)


def _sc_gather_sum(ys, dest_tm, w_lanes):
    n_win, parts, pk = dest_tm.shape
    w = parts * pk // TOP_K
    wp = w // parts
    n_vec = DP // SC_LANES

    @functools.partial(pl.kernel, out_type=jax.ShapeDtypeStruct((n_win * w, D), F32), mesh=_sc_mesh(),
                       scratch_types=[pltpu.VMEM((parts, pk, DP), jnp.int32), pltpu.SemaphoreType.DMA((parts,))],
                       compiler_params=pltpu.CompilerParams(needs_layout_passes=False), name="sc_gather_sum")
    def run(ys_hbm, dest_hbm, w_hbm, o_hbm, rows_v, sems):
        def body(i_vmem, w_vmem, o_vmem):
            copies = [pltpu.async_copy(ys_hbm.at[i_vmem.at[p]], rows_v.at[p], sems.at[p]) for p in range(parts)]
            for p in range(parts):
                copies[p].wait()

                @pl.loop(0, wp)
                def _(t):
                    tok = p * wp + t
                    wv = [w_vmem[tok, pl.ds(k * SC_LANES, SC_LANES)] for k in range(TOP_K)]

                    @pl.loop(0, n_vec, step=SUM_UNROLL)
                    def _(j0):
                        for u in range(SUM_UNROLL):
                            col = (j0 + u) * SC_LANES
                            lo = jnp.zeros((SC_LANES,), F32)
                            hi = jnp.zeros((SC_LANES,), F32)
                            for k in range(TOP_K):
                                word = rows_v[p, t * TOP_K + k, pl.ds(col, SC_LANES)]
                                lo = lo + wv[k] * plsc.bitcast(lax.shift_left(word, 16), F32)
                                hi = hi + wv[k] * plsc.bitcast(word & HI_HALF, F32)
                            o_vmem[tok, pl.ds(col, SC_LANES)] = lo
                            o_vmem[tok, pl.ds(DP + col, SC_LANES)] = hi

        pltpu.emit_pipeline(
            body,
            grid=(n_win,),
            in_specs=[pl.BlockSpec((None, parts, pk), lambda i: (i, 0, 0)),
                      pl.BlockSpec((w, TOP_K * SC_LANES), lambda i: (i, 0))],
            out_specs=[pl.BlockSpec((w, D), lambda i: (i, 0))],
            core_axis_name=("core", "subcore"),
            dimension_semantics=(pltpu.PARALLEL,),
        )(dest_hbm, w_hbm, o_hbm)

    return run(ys, dest_tm, w_lanes)


def _expert_body(first_ref, cnt_ref, nused_ref, xs_hbm, wg_s, wu_s, wd_s, ys_hbm, xbuf, ybuf, xsem, ysem):
    e = pl.program_id(0)
    n_used = nused_ref[0]

    def load(g):
        rows = pl.ds(pl.multiple_of(g * EXPERT_TILE, EXPERT_TILE), EXPERT_TILE)
        slot = g % EXPERT_X_SLOTS
        return pltpu.make_async_copy(xs_hbm.at[rows], xbuf.at[slot], xsem.at[slot])

    def store(g):
        rows = pl.ds(pl.multiple_of(g * EXPERT_TILE, EXPERT_TILE), EXPERT_TILE)
        slot = g % EXPERT_Y_SLOTS
        return pltpu.make_async_copy(ybuf.at[slot], ys_hbm.at[rows], ysem.at[slot])

    @pl.when(e == 0)
    def _():
        for g in range(EXPERT_AHEAD):
            @pl.when(g < n_used)
            def _():
                load(g).start()

    first = first_ref[e]
    cnt = cnt_ref[e]

    def acquire(g):
        ahead = g + EXPERT_AHEAD

        @pl.when(ahead < n_used)
        def _():
            load(ahead).start()

        load(g).wait()

        @pl.when(g >= EXPERT_Y_SLOTS)
        def _():
            store(g - EXPERT_Y_SLOTS).wait()

    def compute(g):
        lo, hi = _unpack_rows(xbuf[g % EXPERT_X_SLOTS])
        lo = lo.astype(BF16)
        hi = hi.astype(BF16)

        def xdot(w_s):
            return (jnp.dot(lo, w_s[:DP, :], preferred_element_type=F32) +
                    jnp.dot(hi, w_s[DP:, :], preferred_element_type=F32))

        a = (_silu(xdot(wg_s)) * xdot(wu_s)).astype(BF16)
        ybuf[g % EXPERT_Y_SLOTS] = _pack_rows(jnp.dot(a, wd_s[...], preferred_element_type=F32))

    def pair(j, carry):
        g = first + 2 * j
        acquire(g)
        acquire(g + 1)
        compute(g)
        compute(g + 1)
        store(g).start()
        store(g + 1).start()
        return carry

    lax.fori_loop(0, cnt // 2, pair, 0)

    @pl.when(cnt % 2 == 1)
    def _():
        g = first + cnt - 1
        acquire(g)
        compute(g)
        store(g).start()

    @pl.when(e == N_EXPERTS - 1)
    def _():
        for k in range(EXPERT_Y_SLOTS):
            g = n_used - 1 - k

            @pl.when(g >= 0)
            def _():
                store(g).wait()


def _experts(xs, tile_first, tile_count, n_used, wg, wu, wd):
    def w_map(e, first, cnt, nu):
        return (e, 0, 0)

    return pl.pallas_call(
        _expert_body,
        grid_spec=pltpu.PrefetchScalarGridSpec(
            num_scalar_prefetch=3, grid=(N_EXPERTS,),
            in_specs=[pl.BlockSpec(memory_space=pl.ANY),
                      pl.BlockSpec((None, D, EXPERT_DIM), w_map),
                      pl.BlockSpec((None, D, EXPERT_DIM), w_map),
                      pl.BlockSpec((None, EXPERT_DIM, D), w_map)],
            out_specs=pl.BlockSpec(memory_space=pl.ANY),
            scratch_shapes=[pltpu.VMEM((EXPERT_X_SLOTS, EXPERT_TILE, DP), jnp.int32),
                            pltpu.VMEM((EXPERT_Y_SLOTS, EXPERT_TILE, DP), jnp.int32),
                            pltpu.SemaphoreType.DMA((EXPERT_X_SLOTS,)),
                            pltpu.SemaphoreType.DMA((EXPERT_Y_SLOTS,))]),
        out_shape=jax.ShapeDtypeStruct(xs.shape, jnp.int32),
        compiler_params=_cparams("arbitrary"),
        name="experts",
    )(tile_first, tile_count, n_used, xs, wg, wu, wd)


def _log_sigmoid(z):
    return jnp.minimum(z, 0.0) - jnp.log(1.0 + jnp.exp(-jnp.abs(z)))


def _gla_gate(hb, wlr_ref, wgk_ref, bgk_ref):
    lr = jnp.dot(hb, wlr_ref[...], preferred_element_type=F32)
    z = _bdot(lr, wgk_ref[...]) + bgk_ref[...]
    return _log_sigmoid(z) * (1.0 / GLA_GATE_NORMALIZER)


def _split3(a):
    hi = a.astype(BF16)
    r1 = a - hi.astype(F32)
    mid = r1.astype(BF16)
    lo = (r1 - mid.astype(F32)).astype(BF16)
    return hi, mid, lo


def _gla_out(o_ref_val, go, gng):
    parts = []
    for hd in range(GLA_HEADS):
        cols = slice(hd * GLA_DV, (hd + 1) * GLA_DV)
        parts.append((_rms(o_ref_val[:, cols], gng) * _silu(go[:, cols])).astype(BF16))
    return jnp.concatenate(parts, axis=1)


def _channel_mix_residual(x, routed, h_ref, g2, ng3, swg_ref, swu_ref, swd_ref):
    h_lo, h_hi = _unpack_rows(h_ref[...])
    h_lo = h_lo.astype(BF16)
    h_hi = h_hi.astype(BF16)

    def hdot(w_ref_):
        return (jnp.dot(h_lo, w_ref_[:DP, :], preferred_element_type=F32) +
                jnp.dot(h_hi, w_ref_[DP:, :], preferred_element_type=F32))

    hs = (_silu(hdot(swg_ref)) * hdot(swu_ref)).astype(BF16)
    y = jnp.dot(hs, swd_ref[...], preferred_element_type=F32) + routed
    return x + g2 * _rms(y, ng3)


def _gla_body(x_ref, y_ref, hprev_ref, modprev_ref, ngprev_ref, swg_ref, swu_ref, swd_ref,
              mod_ref, ng_ref, wqkvg_ref, wlr_ref, wgk_ref, bgk_ref, tril_ref, gng_ref, wout_ref,
              rw_ref, *rest, tt, n_alias):
    x1_ref, h_ref, lg_ref, st_ref, st_scr, o_scr, qd_scr, dst_scr = rest[n_alias:]
    j = pl.program_id(1)

    @pl.when(j == 0)
    def _():
        st_scr[...] = jnp.zeros_like(st_scr)

    x = _channel_mix_residual(x_ref[...], y_ref[...], hprev_ref, modprev_ref[:, 5 * D:6 * D],
                              ngprev_ref[3:4, :], swg_ref, swu_ref, swd_ref)
    sh1, sc1, g1, sh2, sc2, _ = _mod_slices(mod_ref)
    ng = ng_ref[...]
    hb = (_rms(x, ng[0:1]) * (1.0 + sc1) + sh1).astype(BF16)
    q = jnp.dot(hb, wqkvg_ref[:, :GLA_DK_TOT], preferred_element_type=F32) * (GLA_DK ** -0.5)
    k = jnp.dot(hb, wqkvg_ref[:, GLA_DK_TOT:2 * GLA_DK_TOT], preferred_element_type=F32)
    v = jnp.dot(hb, wqkvg_ref[:, 2 * GLA_DK_TOT:2 * GLA_DK_TOT + GLA_DV_TOT],
                preferred_element_type=F32).astype(BF16)
    log_a = _gla_gate(hb, wlr_ref, wgk_ref, bgk_ref)
    tril = tril_ref[...]
    parts = _split3(log_a)
    b = jnp.concatenate(
        [sum(jnp.dot(tril, p[r:r + GLA_CUM_BLOCK], preferred_element_type=F32) for p in parts)
         for r in range(0, tt, GLA_CUM_BLOCK)], axis=0)
    row = lax.broadcasted_iota(jnp.int32, (GLA_CHUNK, GLA_CHUNK), 0)
    col = lax.broadcasted_iota(jnp.int32, (GLA_CHUNK, GLA_CHUNK), 1)
    causal = row >= col
    n_chunks = tt // GLA_CHUNK
    for c in range(n_chunks):
        rows = slice(c * GLA_CHUNK, (c + 1) * GLA_CHUNK)
        last = (c + 1) * GLA_CHUNK - 1
        for hd in range(GLA_HEADS):
            kc = slice(hd * GLA_DK, (hd + 1) * GLA_DK)
            vc = slice(hd * GLA_DV, (hd + 1) * GLA_DV)
            bb = b[rows, kc]
            b_last = b[last:last + 1, kc]
            q_dec = (q[rows, kc] * jnp.exp(bb)).astype(BF16)
            k_inv = (k[rows, kc] * jnp.exp(-bb)).astype(BF16)
            k_end = (k[rows, kc] * jnp.exp(b_last - bb)).astype(BF16)
            att = jnp.where(causal, _dot_nt(q_dec, k_inv), 0.0).astype(BF16)
            qd_scr[rows, kc] = q_dec
            o_scr[rows, vc] = jnp.dot(att, v[rows, vc], preferred_element_type=F32)
            dst_scr[c * GLA_HEADS + hd] = lax.dot_general(
                v[rows, vc], k_end, (((0,), (0,)), ((), ())), preferred_element_type=F32)
    states = [st_scr[hd] for hd in range(GLA_HEADS)]
    for c in range(n_chunks):
        rows = slice(c * GLA_CHUNK, (c + 1) * GLA_CHUNK)
        last = (c + 1) * GLA_CHUNK - 1
        for hd in range(GLA_HEADS):
            kc = slice(hd * GLA_DK, (hd + 1) * GLA_DK)
            vc = slice(hd * GLA_DV, (hd + 1) * GLA_DV)
            o_scr[rows, vc] += _dot_nt(qd_scr[rows, kc], states[hd].astype(BF16))
            states[hd] = states[hd] * jnp.exp(b[last:last + 1, kc]) + dst_scr[c * GLA_HEADS + hd]
    for hd in range(GLA_HEADS):
        st_scr[hd] = states[hd]

    @pl.when(j == pl.num_programs(1) - 1)
    def _():
        for hd in range(GLA_HEADS):
            st_ref[hd] = st_scr[hd].T

    go = jnp.dot(hb, wqkvg_ref[:, 2 * GLA_DK_TOT + GLA_DV_TOT:], preferred_element_type=F32)
    y = jnp.dot(_gla_out(o_scr[...], go, gng_ref[...]), wout_ref[...], preferred_element_type=F32)
    x1 = x + g1 * _rms(y, ng[1:2])
    x1_ref[...] = x1
    _ffn_prep(x1, ng, sh2, sc2, rw_ref, h_ref, lg_ref)


def _gla_mixer(x2d, routed, hp, mod3_prev, ng_prev, shared_prev, mod3, batch, seq, ng, wqkvg, wlr, wgk, bgk,
               gng, wout, rw_t, ffn_rows=None):
    tt = GLA_TILE
    tpb = seq // tt
    n = x2d.shape[0]
    idx = jnp.arange(GLA_CUM_BLOCK)
    tril = ((idx[:, None] >= idx[None, :]) &
            (idx[:, None] // GLA_CHUNK == idx[None, :] // GLA_CHUNK)).astype(BF16)
    row_map = lambda b, j: (b * tpb + j, 0)
    mod_map = lambda b, j: (b, 0, 0)
    consts = (ng, wqkvg, wlr, wgk, bgk, tril, gng, wout, rw_t)
    prev_consts = (ng_prev,) + tuple(shared_prev)
    ffn_shapes, alias_bufs, _ = _ffn_out(n, ffn_rows, None)
    args = [x2d, routed, hp, mod3_prev, *prev_consts, mod3, *consts]
    return pl.pallas_call(
        functools.partial(_gla_body, tt=tt, n_alias=len(alias_bufs)),
        grid=(batch, tpb),
        in_specs=[pl.BlockSpec((tt, D), row_map), pl.BlockSpec((tt, D), row_map),
                  pl.BlockSpec((tt, DP), row_map), pl.BlockSpec((None, 1, 6 * D), mod_map)] +
                 [_const_spec(a.shape) for a in prev_consts] +
                 [pl.BlockSpec((None, 1, 6 * D), mod_map)] +
                 [_const_spec(a.shape) for a in consts] +
                 [pl.BlockSpec(memory_space=pl.ANY)] * len(alias_bufs),
        out_specs=[pl.BlockSpec((tt, D), row_map), pl.BlockSpec((tt, DP), row_map),
                   pl.BlockSpec((N_EXPERTS, tt), lambda b, j: (0, b * tpb + j)),
                   pl.BlockSpec((None, GLA_HEADS, GLA_DK, GLA_DV), lambda b, j: (b, 0, 0, 0))],
        out_shape=[jax.ShapeDtypeStruct((n, D), F32)] + ffn_shapes +
                  [jax.ShapeDtypeStruct((batch, GLA_HEADS, GLA_DK, GLA_DV), F32)],
        input_output_aliases={len(args) + i: 1 + i for i in range(len(alias_bufs))},
        scratch_shapes=[pltpu.VMEM((GLA_HEADS, GLA_DV, GLA_DK), F32),
                        pltpu.VMEM((tt, GLA_DV_TOT), F32),
                        pltpu.VMEM((tt, GLA_DK_TOT), BF16),
                        pltpu.VMEM((tt // GLA_CHUNK * GLA_HEADS, GLA_DV, GLA_DK), F32)],
        compiler_params=_cparams("parallel", "arbitrary"),
        name="gla_mixer",
    )(*args, *alias_bufs)


def _gla1_proj_body(x_ref, mod_ref, ng_ref, wqkvg_ref, wlr_ref, wgk_ref, bgk_ref,
                    q_ref, k_ref, v_ref, go_ref, dec_ref):
    sh1, sc1, _, _, _, _ = _mod_slices(mod_ref)
    ng = ng_ref[...]
    hb = (_rms(x_ref[...], ng[0:1]) * (1.0 + sc1) + sh1).astype(BF16)
    proj = jnp.dot(hb, wqkvg_ref[...], preferred_element_type=F32)
    q_ref[...] = proj[:, :GLA_DK_TOT] * (GLA_DK ** -0.5)
    k_ref[...] = proj[:, GLA_DK_TOT:2 * GLA_DK_TOT]
    v_ref[...] = proj[:, 2 * GLA_DK_TOT:2 * GLA_DK_TOT + GLA_DV_TOT]
    go_ref[...] = proj[:, 2 * GLA_DK_TOT + GLA_DV_TOT:]
    dec_ref[...] = jnp.exp(_gla_gate(hb, wlr_ref, wgk_ref, bgk_ref))


GLA1_TOK = 8


def _gla1_state_body(st_ref, qc_ref, kc_ref, dc_ref, v_ref, nst_ref, o_ref):
    v = v_ref[...]
    for i in range(GLA1_TOK):
        for hd in range(GLA_HEADS):
            vrow = v[i:i + 1, hd * GLA_DV:(hd + 1) * GLA_DV]
            s_new = dc_ref[hd][:, i:i + 1] * st_ref[i, hd] + kc_ref[hd][:, i:i + 1] * vrow
            nst_ref[i, hd] = s_new
            o_ref[i:i + 1, hd * GLA_DV:(hd + 1) * GLA_DV] = jnp.sum(
                qc_ref[hd][:, i:i + 1] * s_new, axis=0, keepdims=True)


def _gla1_out_body(x_ref, o_ref, go_ref, mod_ref, ng_ref, gng_ref, wout_ref, rw_ref, *rest):
    x1_ref, h_ref, lg_ref = rest[-3:]
    _, _, g1, sh2, sc2, _ = _mod_slices(mod_ref)
    ng = ng_ref[...]
    y = jnp.dot(_gla_out(o_ref[...], go_ref[...], gng_ref[...]), wout_ref[...], preferred_element_type=F32)
    x1 = x_ref[...] + g1 * _rms(y, ng[1:2])
    x1_ref[...] = x1
    _ffn_prep(x1, ng, sh2, sc2, rw_ref, h_ref, lg_ref)


def _gla_mixer_one(x2d, mod2, state, ng, wqkvg, wlr, wgk, bgk, gng, wout, rw_t, into=None):
    n = x2d.shape[0]
    consts = (ng, wqkvg, wlr, wgk, bgk)
    q, k, v, go, dec = pl.pallas_call(
        _gla1_proj_body,
        in_specs=[_const_spec(a.shape) for a in (x2d, mod2) + consts],
        out_specs=[_const_spec((n, GLA_DK_TOT)), _const_spec((n, GLA_DK_TOT)), _const_spec((n, GLA_DV_TOT)),
                   _const_spec((n, GLA_DV_TOT)), _const_spec((n, GLA_DK_TOT))],
        out_shape=[jax.ShapeDtypeStruct((n, GLA_DK_TOT), F32), jax.ShapeDtypeStruct((n, GLA_DK_TOT), F32),
                   jax.ShapeDtypeStruct((n, GLA_DV_TOT), F32), jax.ShapeDtypeStruct((n, GLA_DV_TOT), F32),
                   jax.ShapeDtypeStruct((n, GLA_DK_TOT), F32)],
        grid=(1,),
        compiler_params=_cparams("arbitrary"),
        name="gla1_proj",
    )(x2d, mod2, *consts)

    def cols(a):
        return a.reshape(n // GLA1_TOK, GLA1_TOK, GLA_HEADS, GLA_DK).transpose(0, 2, 3, 1)

    col_spec = pl.BlockSpec((None, GLA_HEADS, GLA_DK, GLA1_TOK), lambda i: (i, 0, 0, 0))
    st_spec = pl.BlockSpec((GLA1_TOK, GLA_HEADS, GLA_DK, GLA_DV), lambda i: (i, 0, 0, 0))
    new_state, o = pl.pallas_call(
        _gla1_state_body,
        grid=(n // GLA1_TOK,),
        in_specs=[st_spec, col_spec, col_spec, col_spec, pl.BlockSpec((GLA1_TOK, GLA_DV_TOT), lambda i: (i, 0))],
        out_specs=[st_spec, pl.BlockSpec((GLA1_TOK, GLA_DV_TOT), lambda i: (i, 0))],
        out_shape=[jax.ShapeDtypeStruct(state.shape, F32), jax.ShapeDtypeStruct((n, GLA_DV_TOT), F32)],
        compiler_params=_cparams("parallel"),
        name="gla1_state",
    )(state, cols(q), cols(k), cols(dec), v)

    consts = (mod2, ng, gng, wout, rw_t)
    ffn_shapes, alias_bufs, oblk = _ffn_out(n, None, into)
    n_in = 3 + len(consts)
    x1, h, lg = pl.pallas_call(
        _gla1_out_body,
        grid=(1,),
        in_specs=[_const_spec(a.shape) for a in (x2d, o, go) + consts] +
                 [pl.BlockSpec(memory_space=pl.ANY)] * len(alias_bufs),
        out_specs=[_const_spec((n, D)), pl.BlockSpec((n, DP), lambda i: (oblk, 0)),
                   pl.BlockSpec((N_EXPERTS, n), lambda i: (0, oblk))],
        out_shape=[jax.ShapeDtypeStruct((n, D), F32)] + ffn_shapes,
        input_output_aliases={n_in + i: 1 + i for i in range(len(alias_bufs))},
        compiler_params=_cparams("arbitrary"),
        name="gla1_out",
    )(x2d, o, go, *consts, *alias_bufs)
    return x1, h, lg, new_state


def _moe_routed(h, lg, n, router_bias, wg, wu, wd):
    n_pad = h.shape[0]
    eid, rank, wts, counts = _router(lg, router_bias, n)
    tile_count = ((counts[:, 0] + EXPERT_TILE - 1) // EXPERT_TILE).astype(jnp.int32)
    tile_end = jnp.cumsum(tile_count).astype(jnp.int32)
    tile_first = tile_end - tile_count
    off = tile_first * EXPERT_TILE
    p_alloc = TOP_K * n_pad + N_EXPERTS * EXPERT_TILE
    dest = _dest(off, eid, rank, n, p_alloc - 1)
    dest_w = dest.reshape(TOP_K, n_pad // DISPATCH_W, DISPATCH_W).transpose(1, 0, 2)
    xs = _sc_dispatch(h, dest_w, p_alloc)
    ys = _experts(xs, tile_first, tile_count, tile_end[-1:], wg, wu, wd)
    dest_tm = dest.T.reshape(n_pad // SUM_W, SUM_PARTS, SUM_W * TOP_K // SUM_PARTS)
    w_lanes = jnp.repeat(wts.T, SC_LANES, axis=1)
    return _sc_gather_sum(ys, dest_tm, w_lanes)


def kernel(x_prompt, x_sample, state_gla, c_prompt, c_sample, norm_g, ada_w, ada_b, gm_w_in, gm_b_in,
           gm_ln_g, gm_ln_b, gm_w_s, gm_b_s, gm_w_out, gla_w_in, gla_w_gk, gla_b_gk, gla_norm_g,
           gla_w_out, router_w, router_bias, exp_w_gate, exp_w_up, exp_w_down, sh_w_gate, sh_w_up,
           sh_w_down):
    batch, seq, _ = x_prompt.shape
    n_s = x_sample.shape[0]
    n_p = batch * seq
    tpb = seq // MIX_TILE
    xp = x_prompt.reshape(n_p, D)
    xs = x_sample.reshape(n_s, D)

    mod = _ada(jnp.concatenate([c_prompt, c_sample], axis=0), ada_w, ada_b)
    mod_p = [mod[i, :batch].reshape(batch, 1, 6 * D) for i in range(2)]
    mod_s = [mod[i, batch:] for i in range(2)]
    rw_t = [jnp.concatenate(_split3(router_w[i].T), axis=0) for i in range(2)]

    ws_causal = jnp.tril(gm_w_s[0]).astype(BF16)
    bs_cols = gm_b_s[0].T
    eye = jnp.eye(GM_CHUNK, dtype=F32)
    ws_first = (gm_w_s[0][:, 0, 0][:, None, None] * eye).astype(BF16)
    bs_first = jnp.broadcast_to(gm_b_s[0][:, 0][None, :], (GM_CHUNK, GM_GROUPS))
    gm_args = (norm_g[0], gm_w_in[0].astype(BF16), gm_b_in[0].reshape(1, -1), gm_ln_g[0].reshape(1, -1),
               gm_ln_b[0].reshape(1, -1))
    wout0 = gm_w_out[0].astype(BF16)
    shared = [(sh_w_gate[i].astype(BF16), sh_w_up[i].astype(BF16), sh_w_down[i].astype(BF16))
              for i in range(2)]
    n_qkvg = 2 * GLA_DK_TOT + 2 * GLA_DV_TOT
    wqkvg = gla_w_in[0][:, :n_qkvg].astype(BF16)
    wlr = jnp.pad(gla_w_in[0][:, n_qkvg:], ((0, 0), (0, LANES - GLA_GATE_RANK))).astype(BF16)
    wgk = jnp.pad(gla_w_gk[0], ((0, LANES - GLA_GATE_RANK), (0, 0))).astype(BF16)
    gla_args = (norm_g[1], wqkvg, wlr, wgk, gla_b_gk[0].reshape(1, -1), gla_norm_g[0].reshape(1, -1),
                gla_w_out[0].astype(BF16), rw_t[1])
    experts_f32 = (exp_w_gate, exp_w_up, exp_w_down)

    half = batch // 2
    streams = [(0, half, False), (half, batch - half, True)]
    st = [dict() for _ in streams]

    experts = []
    for layer, (s, (b0, nb, with_new)) in enumerate(zip(st, streams)):
        s["mod_p"] = [mod_p[i][b0:b0 + nb] for i in range(2)]
        s["n"] = nb * seq
        s["n_all"] = s["n"] + (n_s if with_new else 0)
        s["n_pad"] = -(-s["n_all"] // TOKEN_PAD) * TOKEN_PAD
        s["x1p"], s["h"], s["lg"], *w16 = _gmlp_mixer(
            xp, b0 * tpb, s["n"], s["mod_p"][0], False, MIX_TILE, tpb, *gm_args, ws_causal, bs_cols, wout0,
            rw_t[0], emit_v=False, cast_w=(layer, *experts_f32), ffn_rows=s["n_pad"])
        experts.append(w16)
        if with_new:
            s["x1s"], s["h"], s["lg"], v_rows = _gmlp_mixer(
                xs, 0, n_s, mod_s[0], True, n_s, 1, *gm_args, ws_first, bs_first, wout0, rw_t[0], emit_v=True,
                into=(s["h"], s["lg"], s["n"] // n_s))
    for s, (b0, nb, with_new) in zip(st, streams):
        s["routed0"] = _moe_routed(s["h"], s["lg"], s["n_all"], router_bias[0], *experts[0])
        if with_new:
            s["x2s"] = _combine(s["x1s"], s["routed0"], s["h"], s["n"] // n_s, mod_s[0], True, n_s, 1,
                                norm_g[0], *shared[0])
    for s, (b0, nb, with_new) in zip(st, streams):
        s["x3p"], h1, lg1, s["st_p"] = _gla_mixer(s["x1p"], s["routed0"], s["h"], s["mod_p"][0], norm_g[0],
                                                  shared[0], s["mod_p"][1], nb, seq, *gla_args,
                                                  ffn_rows=s["n_pad"])
        if with_new:
            s["x3s"], h1, lg1, st_s = _gla_mixer_one(s["x2s"], mod_s[1], state_gla[:, 0], *gla_args,
                                                     into=(h1, lg1, s["n"] // n_s))
        s["h"], s["lg"] = h1, lg1
    y_prompt = None
    for s, (b0, nb, with_new) in zip(st, streams):
        routed = _moe_routed(s["h"], s["lg"], s["n_all"], router_bias[1], *experts[1])
        y_prompt = _combine(s["x3p"], routed, s["h"], 0, s["mod_p"][1], False, MIX_TILE, tpb, norm_g[1],
                            *shared[1], out_rows=n_p, out_blk0=b0 * tpb, out_buf=y_prompt)
        if with_new:
            y_new = _combine(s["x3s"], routed, s["h"], s["n"] // n_s, mod_s[1], True, n_s, 1, norm_g[1],
                             *shared[1])
    st_p = jnp.concatenate([s["st_p"] for s in st], axis=0)

    return (y_prompt.reshape(batch, seq, D), y_new.reshape(n_s, 1, D), st_p[:, None], st_s[:, None],
            v_rows.reshape(n_s, 1, 1, GM_HALF))
```

```python
import functools
import math

import jax
import jax.numpy as jnp
from jax import lax
from jax.experimental import pallas as pl
from jax.experimental.pallas import tpu as pltpu
from jax.experimental.pallas import tpu_sc as plsc

F32 = jnp.float32
BF16 = jnp.bfloat16

D = 1024
DP = D // 2
GM_CHUNK = 128
GM_HALF = 2 * D
GM_GROUPS = 8
GM_GROUP_DIM = GM_HALF // GM_GROUPS
GLA_HEADS = 4
GLA_DK = 128
GLA_DV = 256
GLA_DK_TOT = GLA_HEADS * GLA_DK
GLA_DV_TOT = GLA_HEADS * GLA_DV
GLA_GATE_RANK = 16
GLA_GATE_NORMALIZER = 16.0
GLA_CHUNK = 64
N_EXPERTS = 64
TOP_K = 8
N_EXPERT_GROUPS = 8
GROUP_SIZE = N_EXPERTS // N_EXPERT_GROUPS
TOPK_GROUPS = 4
EXPERT_DIM = D // 4
ROUTED_SCALE = 2.5
NORM_EPS = 1e-6
LN_EPS = 1e-5

LANES = 128
VMEM_LIMIT = 56 * 1024 * 1024

MIX_TILE = 256
GLA_TILE = 512
GLA_SUB = 256
GM_COL_BLOCK = 512
ROUTER_TILE = 1024
EXPERT_TILE = 544
EXPERT_X_SLOTS = 6
EXPERT_AHEAD = EXPERT_X_SLOTS - 2
EXPERT_Y_SLOTS = 4
SC_WORKERS = 32
DISPATCH_W = 32
SC_LANES = 16
SUM_W = 16
SUM_PARTS = 4
SUM_UNROLL = 4
TOKEN_PAD = SC_WORKERS * DISPATCH_W


def _cparams(*sem):
    return pltpu.CompilerParams(dimension_semantics=sem, vmem_limit_bytes=VMEM_LIMIT)


def _rms(x, g):
    return x * lax.rsqrt(jnp.mean(x * x, axis=-1, keepdims=True) + NORM_EPS) * g


def _silu(x):
    return x * (1.0 / (1.0 + jnp.exp(-x)))


def _gelu(x):
    return 0.5 * x * (1.0 + lax.erf(x * (1.0 / math.sqrt(2.0))))


def _bdot(a, b):
    return jnp.dot(a.astype(BF16), b.astype(BF16), preferred_element_type=F32)


def _dot_nt(a, b, precision=None):
    return lax.dot_general(a, b, (((1,), (1,)), ((), ())), preferred_element_type=F32,
                           precision=precision)


def _mod_slices(mod_ref):
    return [mod_ref[:, i * D:(i + 1) * D] for i in range(6)]


HI_HALF = -65536


def _pack_rows(x):
    lo = lax.bitcast_convert_type(x[:, :DP].astype(BF16).astype(F32), jnp.int32)
    hi = lax.bitcast_convert_type(x[:, DP:].astype(BF16).astype(F32), jnp.int32)
    return lax.shift_right_logical(lo, 16) | (hi & HI_HALF)


def _unpack_rows(p):
    lo = lax.bitcast_convert_type(lax.shift_left(p, 16), F32)
    hi = lax.bitcast_convert_type(p & HI_HALF, F32)
    return lo, hi


def _ffn_prep(x1, ng, sh2, sc2, rw_ref, h_ref, lg_ref, rows=slice(None)):
    hffn = _rms(x1, ng[2:3]) * (1.0 + sc2) + sh2
    h_ref[rows, :] = _pack_rows(hffn)
    lg3 = _dot_nt(rw_ref[...], hffn.astype(BF16))
    lg_ref[:, rows] = lg3[:N_EXPERTS] + lg3[N_EXPERTS:2 * N_EXPERTS] + lg3[2 * N_EXPERTS:]


def _ada_body(c_ref, w_ref, b_ref, o_ref):
    c = c_ref[...]
    o_ref[...] = _bdot(_silu(c), w_ref[...]) + b_ref[...]


def _ada(c, ada_w, ada_b):
    n = c.shape[0]
    depth = ada_w.shape[0]
    tn = 1536
    return pl.pallas_call(
        _ada_body,
        grid=(depth, 6 * D // tn),
        in_specs=[pl.BlockSpec((n, D), lambda l, j: (0, 0)),
                  pl.BlockSpec((None, D, tn), lambda l, j: (l, 0, j)),
                  pl.BlockSpec((None, 1, tn), lambda l, j: (l, 0, j))],
        out_specs=pl.BlockSpec((None, n, tn), lambda l, j: (l, 0, j)),
        out_shape=jax.ShapeDtypeStruct((depth, n, 6 * D), F32),
        compiler_params=_cparams("parallel", "parallel"),
        name="ada_mod",
    )(c, ada_w, ada_b.reshape(depth, 1, 6 * D))


def _mod_spec(per_row, tt, tiles_per_batch):
    if per_row:
        return pl.BlockSpec((tt, 6 * D), lambda i: (i, 0))
    return pl.BlockSpec((None, 1, 6 * D), lambda i: (i // tiles_per_batch, 0, 0))


def _const_spec(shape):
    zeros = (0,) * len(shape)
    return pl.BlockSpec(shape, lambda *_: zeros)


def _gmlp_body(x_ref, mod_ref, ng_ref, win_ref, bin_ref, lng_ref, lnb_ref, ws_ref, bs_ref, wout_ref,
               rw_ref, *rest, n_chunks, emit_v, cast_w, n_alias):
    rest = list(rest)
    w32_refs = [rest.pop(0) for _ in range(3)] if cast_w else []
    rest = rest[n_alias:]
    x1_ref, h_ref, lg_ref = rest[:3]
    rest = rest[3:]
    v_ref = rest.pop(0) if emit_v else None
    w16_refs = [rest.pop(0) for _ in range(3)] if cast_w else []
    um_ref, z_ref = rest
    for src, dst in zip(w32_refs, w16_refs):
        dst[...] = src[...].astype(BF16)
    sh1, sc1, g1, sh2, sc2, _ = _mod_slices(mod_ref)
    ng = ng_ref[...]
    x = x_ref[...]
    hb = (_rms(x, ng[0:1]) * (1.0 + sc1) + sh1).astype(BF16)
    for cb in range(2 * GM_HALF // GM_COL_BLOCK):
        cols = slice(cb * GM_COL_BLOCK, (cb + 1) * GM_COL_BLOCK)
        z_ref[:, cols] = _gelu(jnp.dot(hb, win_ref[:, cols], preferred_element_type=F32) + bin_ref[:, cols])
    u = z_ref[:, :GM_HALF]
    v = z_ref[:, GM_HALF:]
    mu = jnp.mean(v, axis=-1, keepdims=True)
    vc = v - mu
    var = jnp.mean(vc * vc, axis=-1, keepdims=True)
    v = vc * lax.rsqrt(var + LN_EPS) * lng_ref[...] + lnb_ref[...]
    if emit_v:
        v_ref[...] = v
    vb = v.astype(BF16)
    for c in range(n_chunks):
        rows = slice(c * GM_CHUNK, (c + 1) * GM_CHUNK)
        for g in range(GM_GROUPS):
            cols = slice(g * GM_GROUP_DIM, (g + 1) * GM_GROUP_DIM)
            mixed = jnp.dot(ws_ref[g], vb[rows, cols], preferred_element_type=F32) + bs_ref[:, g:g + 1]
            um_ref[rows, cols] = (u[rows, cols] * mixed).astype(BF16)
    y = jnp.dot(um_ref[...], wout_ref[...], preferred_element_type=F32)
    x1 = x + g1 * _rms(y, ng[1:2])
    x1_ref[...] = x1
    _ffn_prep(x1, ng, sh2, sc2, rw_ref, h_ref, lg_ref)


def _ffn_out(n, ffn_rows, into):
    rows = ffn_rows or n
    oblk = 0
    bufs = []
    if into is not None:
        *bufs, oblk = into
        rows = bufs[0].shape[0]
    elif rows != n:
        bufs = [jnp.zeros((rows, DP), jnp.int32), jnp.zeros((N_EXPERTS, rows), F32)]
    shapes = [jax.ShapeDtypeStruct((rows, DP), jnp.int32), jax.ShapeDtypeStruct((N_EXPERTS, rows), F32)]
    return shapes, bufs, oblk


def _gmlp_mixer(x2d, blk0, n, mod, per_row, tt, tiles_per_batch, ng, win, b_in, ln_g, ln_b, ws, bs, wout,
                rw_t, emit_v, cast_w=None, ffn_rows=None, into=None):
    steps = n // tt
    ffn_shapes, alias_bufs, oblk = _ffn_out(n, ffn_rows, into)
    out_shape = [jax.ShapeDtypeStruct((n, D), F32)] + ffn_shapes
    out_specs = [pl.BlockSpec((tt, D), lambda i: (i, 0)), pl.BlockSpec((tt, DP), lambda i: (i + oblk, 0)),
                 pl.BlockSpec((N_EXPERTS, tt), lambda i: (0, i + oblk))]
    if emit_v:
        out_shape.append(jax.ShapeDtypeStruct((n, GM_HALF), F32))
        out_specs.append(pl.BlockSpec((tt, GM_HALF), lambda i: (i, 0)))

    def one_buffer(a):
        zeros = (0,) * a.ndim
        return pl.BlockSpec(a.shape, lambda *_: zeros, pipeline_mode=pl.Buffered(1))

    consts = (ng, win, b_in, ln_g, ln_b, ws, bs, wout, rw_t)
    in_specs = [pl.BlockSpec((tt, D), lambda i: (i + blk0, 0)), _mod_spec(per_row, tt, tiles_per_batch)]
    in_specs += [one_buffer(a) for a in consts]
    args = [x2d, mod, *consts]
    if cast_w is not None:
        layer, *w_all = cast_w
        per_step = N_EXPERTS // steps
        for w in w_all:
            blk = (None, per_step) + w.shape[2:]
            in_specs.append(pl.BlockSpec(blk, lambda i: (layer, i, 0, 0)))
            out_specs.append(pl.BlockSpec(blk[1:], lambda i: (i, 0, 0)))
            out_shape.append(jax.ShapeDtypeStruct(w.shape[1:], BF16))
            args.append(w)
    aliases = {len(args) + i: 1 + i for i in range(len(alias_bufs))}
    in_specs += [pl.BlockSpec(memory_space=pl.ANY)] * len(alias_bufs)
    args += alias_bufs
    return pl.pallas_call(
        functools.partial(_gmlp_body, n_chunks=tt // GM_CHUNK, emit_v=emit_v, cast_w=cast_w is not None,
                          n_alias=len(alias_bufs)),
        grid=(steps,),
        in_specs=in_specs,
        out_specs=out_specs,
        out_shape=out_shape,
        input_output_aliases=aliases,
        scratch_shapes=[pltpu.VMEM((tt, GM_HALF), BF16), pltpu.VMEM((tt, 2 * GM_HALF), F32)],
        compiler_params=_cparams("parallel"),
        name="gmlp_mixer_rows" if per_row else "gmlp_mixer",
    )(*args)


def _combine_body(x_ref, y_ref, h_ref, mod_ref, ng_ref, swg_ref, swu_ref, swd_ref, *rest):
    o_ref = rest[-1]
    o_ref[...] = _channel_mix_residual(x_ref[...], y_ref[...], h_ref[...], mod_ref[:, 5 * D:6 * D],
                                       ng_ref[3:4, :], swg_ref, swu_ref, swd_ref)


def _combine(x2d, routed, hp, blk0, mod, per_row, tt, tiles_per_batch, ng, swg, swu, swd,
             out_rows=None, out_blk0=0, out_buf=None):
    n = x2d.shape[0]
    in_specs = [pl.BlockSpec((tt, D), lambda i: (i, 0)),
                pl.BlockSpec((tt, D), lambda i: (i + blk0, 0)),
                pl.BlockSpec((tt, DP), lambda i: (i + blk0, 0)),
                _mod_spec(per_row, tt, tiles_per_batch),
                _const_spec(ng.shape), _const_spec(swg.shape), _const_spec(swu.shape),
                _const_spec(swd.shape)]
    args = [x2d, routed, hp, mod, ng, swg, swu, swd]
    aliases = {}
    if out_buf is not None:
        in_specs.append(pl.BlockSpec(memory_space=pl.ANY))
        aliases = {len(args): 0}
        args.append(out_buf)
    return pl.pallas_call(
        _combine_body,
        grid=(n // tt,),
        in_specs=in_specs,
        out_specs=pl.BlockSpec((tt, D), lambda i: (i + out_blk0, 0)),
        out_shape=jax.ShapeDtypeStruct((out_rows or n, D), F32),
        input_output_aliases=aliases,
        compiler_params=_cparams("parallel"),
        name="combine_rows" if per_row else "combine",
    )(*args)


def _router_body(lg_ref, bias_ref, tri_ref, eid_ref, rank_ref, wts_ref, cnt_ref, carry_ref, *, n_real):
    step = pl.program_id(0)

    @pl.when(step == 0)
    def _():
        carry_ref[...] = jnp.zeros_like(carry_ref)

    lg = lg_ref[...]
    tn = lg.shape[1]
    real = (step * tn + lax.broadcasted_iota(jnp.int32, (1, tn), 1)) < n_real
    lg = jnp.where(real, lg, 0.0)
    scores = 1.0 / (1.0 + jnp.exp(-lg))
    sel = scores + bias_ref[...]
    neg = -jnp.inf
    sub8 = lax.broadcasted_iota(jnp.int32, (GROUP_SIZE, tn), 0)
    gsub = lax.broadcasted_iota(jnp.int32, (N_EXPERT_GROUPS, tn), 0)
    gs = jnp.zeros((N_EXPERT_GROUPS, tn), F32)
    for g in range(N_EXPERT_GROUPS):
        blk = sel[g * GROUP_SIZE:(g + 1) * GROUP_SIZE, :]
        m1 = jnp.max(blk, axis=0, keepdims=True)
        i1 = jnp.min(jnp.where(blk == m1, sub8, GROUP_SIZE), axis=0, keepdims=True)
        m2 = jnp.max(jnp.where(sub8 == i1, neg, blk), axis=0, keepdims=True)
        gs = jnp.where(gsub == g, m1 + m2, gs)
    gmask = jnp.zeros((N_EXPERT_GROUPS, tn), jnp.bool_)
    for _ in range(TOPK_GROUPS):
        m = jnp.max(gs, axis=0, keepdims=True)
        i = jnp.min(jnp.where(gs == m, gsub, N_EXPERT_GROUPS), axis=0, keepdims=True)
        hit = gsub == i
        gmask = jnp.logical_or(gmask, hit)
        gs = jnp.where(hit, neg, gs)
    gmaskf = gmask.astype(F32)
    blocks = []
    for g in range(N_EXPERT_GROUPS):
        keep = jnp.broadcast_to(gmaskf[g:g + 1, :], (GROUP_SIZE, tn)) > 0.5
        blocks.append(jnp.where(keep, sel[g * GROUP_SIZE:(g + 1) * GROUP_SIZE, :], neg))
    msel = jnp.concatenate(blocks, axis=0)
    esub = lax.broadcasted_iota(jnp.int32, (N_EXPERTS, tn), 0)
    chosen = jnp.zeros((N_EXPERTS, tn), jnp.bool_)
    picks = []
    for _ in range(TOP_K):
        m = jnp.max(msel, axis=0, keepdims=True)
        i = jnp.min(jnp.where(msel == m, esub, N_EXPERTS), axis=0, keepdims=True)
        hit = esub == i
        picks.append(i)
        chosen = jnp.logical_or(chosen, hit)
        msel = jnp.where(hit, neg, msel)
    w = jnp.where(chosen, scores, 0.0)
    w = w / jnp.sum(w, axis=0, keepdims=True) * ROUTED_SCALE
    counted = jnp.where(jnp.logical_and(chosen, real), 1.0, 0.0)
    incl = jnp.dot(counted.astype(BF16), tri_ref[...], preferred_element_type=F32)
    rank_full = carry_ref[:, 0:1] + incl - 1.0
    ksub = lax.broadcasted_iota(jnp.int32, (TOP_K, tn), 0)
    eid = jnp.zeros((TOP_K, tn), jnp.int32)
    rank = jnp.zeros((TOP_K, tn), F32)
    wts = jnp.zeros((TOP_K, tn), F32)
    for k in range(TOP_K):
        hit = esub == picks[k]
        eid = jnp.where(ksub == k, picks[k], eid)
        rank = jnp.where(ksub == k, jnp.sum(jnp.where(hit, rank_full, 0.0), axis=0, keepdims=True), rank)
        wts = jnp.where(ksub == k, jnp.sum(jnp.where(hit, w, 0.0), axis=0, keepdims=True), wts)
    eid_ref[...] = eid
    rank_ref[...] = rank.astype(jnp.int32)
    wts_ref[...] = wts
    carry = carry_ref[...] + incl[:, tn - 1:tn]
    carry_ref[...] = carry
    cnt_ref[...] = carry.astype(jnp.int32)


def _router(lg_t, bias, n_real):
    n = lg_t.shape[1]
    tn = ROUTER_TILE
    idx = jnp.arange(tn)
    tri = (idx[:, None] <= idx[None, :]).astype(BF16)
    kspec = pl.BlockSpec((TOP_K, tn), lambda i: (0, i))
    return pl.pallas_call(
        functools.partial(_router_body, n_real=n_real),
        grid=(n // tn,),
        in_specs=[pl.BlockSpec((N_EXPERTS, tn), lambda i: (0, i)), _const_spec((N_EXPERTS, 1)),
                  _const_spec((tn, tn))],
        out_specs=[kspec, kspec, kspec, _const_spec((N_EXPERTS, LANES))],
        out_shape=[jax.ShapeDtypeStruct((TOP_K, n), jnp.int32), jax.ShapeDtypeStruct((TOP_K, n), jnp.int32),
                   jax.ShapeDtypeStruct((TOP_K, n), F32), jax.ShapeDtypeStruct((N_EXPERTS, LANES), jnp.int32)],
        scratch_shapes=[pltpu.VMEM((N_EXPERTS, LANES), F32)],
        compiler_params=_cparams("arbitrary"),
        name="router",
    )(lg_t, bias.reshape(N_EXPERTS, 1), tri)


def _dest_body(off_ref, eid_ref, rank_ref, dest_ref, *, n_real, last_row):
    eid = eid_ref[...]
    base = jnp.zeros(eid.shape, jnp.int32)
    for e in range(N_EXPERTS):
        base = jnp.where(eid == e, off_ref[e], base)
    tok = lax.broadcasted_iota(jnp.int32, eid.shape, 1)
    slot = lax.broadcasted_iota(jnp.int32, eid.shape, 0)
    unused = last_row - ((tok - n_real) * TOP_K + slot)
    dest_ref[...] = jnp.where(tok < n_real, base + rank_ref[...], unused)


def _dest(off, eid, rank, n_real, last_row):
    spec = pl.BlockSpec(eid.shape, lambda i, off_ref: (0, 0))
    return pl.pallas_call(
        functools.partial(_dest_body, n_real=n_real, last_row=last_row),
        grid_spec=pltpu.PrefetchScalarGridSpec(num_scalar_prefetch=1, grid=(1,), in_specs=[spec, spec],
                                               out_specs=spec),
        out_shape=jax.ShapeDtypeStruct(eid.shape, jnp.int32),
        compiler_params=_cparams("arbitrary"),
        name="dest_rows",
    )(off, eid, rank)


def _sc_mesh():
    return plsc.VectorSubcoreMesh(core_axis_name="core", subcore_axis_name="subcore")


def _sc_dispatch(hp, dest_w, p_alloc):
    n = hp.shape[0]
    w = dest_w.shape[2]

    @functools.partial(pl.kernel, out_type=jax.ShapeDtypeStruct((p_alloc, DP), jnp.int32), mesh=_sc_mesh(),
                       name="sc_dispatch")
    def run(hp_hbm, dest_hbm, xs_hbm):
        def body(x_vmem, i_vmem):
            for k in range(TOP_K):
                pltpu.sync_copy(x_vmem, xs_hbm.at[i_vmem.at[k]])

        pltpu.emit_pipeline(
            body,
            grid=(n // w,),
            in_specs=[pl.BlockSpec((w, DP), lambda i: (i, 0)),
                      pl.BlockSpec((None, TOP_K, w), lambda i: (i, 0, 0))],
            out_specs=[],
            core_axis_name=("core", "subcore"),
            dimension_semantics=(pltpu.PARALLEL,),
        )(hp_hbm, dest_hbm)

    return run(hp, dest_w)


def _sc_gather_sum(ys, dest_tm, w_lanes):
    n_win, parts, pk = dest_tm.shape
    w = parts * pk // TOP_K
    wp = w // parts
    n_vec = DP // SC_LANES

    @functools.partial(pl.kernel, out_type=jax.ShapeDtypeStruct((n_win * w, D), F32), mesh=_sc_mesh(),
                       scratch_types=[pltpu.VMEM((parts, pk, DP), jnp.int32), pltpu.SemaphoreType.DMA((parts,))],
                       compiler_params=pltpu.CompilerParams(needs_layout_passes=False), name="sc_gather_sum")
    def run(ys_hbm, dest_hbm, w_hbm, o_hbm, rows_v, sems):
        def body(i_vmem, w_vmem, o_vmem):
            copies = [pltpu.async_copy(ys_hbm.at[i_vmem.at[p]], rows_v.at[p], sems.at[p]) for p in range(parts)]
            for p in range(parts):
                copies[p].wait()

                @pl.loop(0, wp)
                def _(t):
                    tok = p * wp + t
                    wv = [w_vmem[tok, pl.ds(k * SC_LANES, SC_LANES)] for k in range(TOP_K)]

                    @plsc.parallel_loop(0, n_vec, unroll=SUM_UNROLL)
                    def _(j):
                        col = j * SC_LANES
                        lo = jnp.zeros((SC_LANES,), F32)
                        hi = jnp.zeros((SC_LANES,), F32)
                        for k in range(TOP_K):
                            word = rows_v[p, t * TOP_K + k, pl.ds(col, SC_LANES)]
                            lo = lo + wv[k] * plsc.bitcast(lax.shift_left(word, 16), F32)
                            hi = hi + wv[k] * plsc.bitcast(word & HI_HALF, F32)
                        o_vmem[tok, pl.ds(col, SC_LANES)] = lo
                        o_vmem[tok, pl.ds(DP + col, SC_LANES)] = hi

        pltpu.emit_pipeline(
            body,
            grid=(n_win,),
            in_specs=[pl.BlockSpec((None, parts, pk), lambda i: (i, 0, 0)),
                      pl.BlockSpec((w, TOP_K * SC_LANES), lambda i: (i, 0))],
            out_specs=[pl.BlockSpec((w, D), lambda i: (i, 0))],
            core_axis_name=("core", "subcore"),
            dimension_semantics=(pltpu.PARALLEL,),
        )(dest_hbm, w_hbm, o_hbm)

    return run(ys, dest_tm, w_lanes)


def _expert_body(first_ref, cnt_ref, nused_ref, xs_hbm, wg_s, wu_s, wd_s, ys_hbm, xbuf, ybuf, xsem, ysem):
    e = pl.program_id(0)
    n_used = nused_ref[0]

    def load(g):
        rows = pl.ds(pl.multiple_of(g * EXPERT_TILE, EXPERT_TILE), EXPERT_TILE)
        slot = g % EXPERT_X_SLOTS
        return pltpu.make_async_copy(xs_hbm.at[rows], xbuf.at[slot], xsem.at[slot])

    def store(g):
        rows = pl.ds(pl.multiple_of(g * EXPERT_TILE, EXPERT_TILE), EXPERT_TILE)
        slot = g % EXPERT_Y_SLOTS
        return pltpu.make_async_copy(ybuf.at[slot], ys_hbm.at[rows], ysem.at[slot])

    @pl.when(e == 0)
    def _():
        for g in range(EXPERT_AHEAD):
            @pl.when(g < n_used)
            def _():
                load(g).start()

    first = first_ref[e]
    cnt = cnt_ref[e]

    def acquire(g):
        ahead = g + EXPERT_AHEAD

        @pl.when(ahead < n_used)
        def _():
            load(ahead).start()

        load(g).wait()

        @pl.when(g >= EXPERT_Y_SLOTS)
        def _():
            store(g - EXPERT_Y_SLOTS).wait()

    def compute(g):
        lo, hi = _unpack_rows(xbuf[g % EXPERT_X_SLOTS])
        lo = lo.astype(BF16)
        hi = hi.astype(BF16)

        def xdot(w_s):
            return (jnp.dot(lo, w_s[:DP, :], preferred_element_type=F32) +
                    jnp.dot(hi, w_s[DP:, :], preferred_element_type=F32))

        a = (_silu(xdot(wg_s)) * xdot(wu_s)).astype(BF16)
        ybuf[g % EXPERT_Y_SLOTS] = _pack_rows(jnp.dot(a, wd_s[...], preferred_element_type=F32))

    def pair(j, carry):
        g = first + 2 * j
        acquire(g)
        acquire(g + 1)
        compute(g)
        compute(g + 1)
        store(g).start()
        store(g + 1).start()
        return carry

    lax.fori_loop(0, cnt // 2, pair, 0)

    @pl.when(cnt % 2 == 1)
    def _():
        g = first + cnt - 1
        acquire(g)
        compute(g)
        store(g).start()

    @pl.when(e == N_EXPERTS - 1)
    def _():
        for k in range(EXPERT_Y_SLOTS):
            g = n_used - 1 - k

            @pl.when(g >= 0)
            def _():
                store(g).wait()


def _experts(xs, tile_first, tile_count, n_used, wg, wu, wd):
    def w_map(e, first, cnt, nu):
        return (e, 0, 0)

    return pl.pallas_call(
        _expert_body,
        grid_spec=pltpu.PrefetchScalarGridSpec(
            num_scalar_prefetch=3, grid=(N_EXPERTS,),
            in_specs=[pl.BlockSpec(memory_space=pl.ANY),
                      pl.BlockSpec((None, D, EXPERT_DIM), w_map),
                      pl.BlockSpec((None, D, EXPERT_DIM), w_map),
                      pl.BlockSpec((None, EXPERT_DIM, D), w_map)],
            out_specs=pl.BlockSpec(memory_space=pl.ANY),
            scratch_shapes=[pltpu.VMEM((EXPERT_X_SLOTS, EXPERT_TILE, DP), jnp.int32),
                            pltpu.VMEM((EXPERT_Y_SLOTS, EXPERT_TILE, DP), jnp.int32),
                            pltpu.SemaphoreType.DMA((EXPERT_X_SLOTS,)),
                            pltpu.SemaphoreType.DMA((EXPERT_Y_SLOTS,))]),
        out_shape=jax.ShapeDtypeStruct(xs.shape, jnp.int32),
        compiler_params=_cparams("arbitrary"),
        name="experts",
    )(tile_first, tile_count, n_used, xs, wg, wu, wd)


def _log_sigmoid(z):
    return jnp.minimum(z, 0.0) - jnp.log(1.0 + jnp.exp(-jnp.abs(z)))


def _gla_gate(hb, wlr_ref, wgk_ref, bgk_ref):
    lr = jnp.dot(hb, wlr_ref[...], preferred_element_type=F32)
    z = _bdot(lr, wgk_ref[...]) + bgk_ref[...]
    return _log_sigmoid(z) * (1.0 / GLA_GATE_NORMALIZER)


def _split3(a):
    hi = a.astype(BF16)
    r1 = a - hi.astype(F32)
    mid = r1.astype(BF16)
    lo = (r1 - mid.astype(F32)).astype(BF16)
    return hi, mid, lo


def _gla_out(o_ref_val, go, gng):
    parts = []
    for hd in range(GLA_HEADS):
        cols = slice(hd * GLA_DV, (hd + 1) * GLA_DV)
        parts.append((_rms(o_ref_val[:, cols], gng) * _silu(go[:, cols])).astype(BF16))
    return jnp.concatenate(parts, axis=1)


def _channel_mix_residual(x, routed, h_packed, g2, ng3, swg_ref, swu_ref, swd_ref):
    h_lo, h_hi = _unpack_rows(h_packed)
    h_lo = h_lo.astype(BF16)
    h_hi = h_hi.astype(BF16)

    def hdot(w_ref_):
        return (jnp.dot(h_lo, w_ref_[:DP, :], preferred_element_type=F32) +
                jnp.dot(h_hi, w_ref_[DP:, :], preferred_element_type=F32))

    hs = (_silu(hdot(swg_ref)) * hdot(swu_ref)).astype(BF16)
    y = jnp.dot(hs, swd_ref[...], preferred_element_type=F32) + routed
    return x + g2 * _rms(y, ng3)


def _gla_body(x_ref, y_ref, hprev_ref, modprev_ref, ngprev_ref, swg_ref, swu_ref, swd_ref,
              mod_ref, ng_ref, wqkvg_ref, wlr_ref, wgk_ref, bgk_ref, tril_ref, gng_ref, wout_ref,
              rw_ref, *rest, tt, n_alias):
    x1_ref, h_ref, lg_ref, st_ref, st_scr, o_scr, qd_scr, dst_scr, x_scr, hb_scr = rest[n_alias:]
    j = pl.program_id(1)

    @pl.when(j == 0)
    def _():
        st_scr[...] = jnp.zeros_like(st_scr)

    sh1, sc1, g1, sh2, sc2, _ = _mod_slices(mod_ref)
    ng = ng_ref[...]
    tril = tril_ref[...]
    row = lax.broadcasted_iota(jnp.int32, (GLA_CHUNK, GLA_CHUNK), 0)
    col = lax.broadcasted_iota(jnp.int32, (GLA_CHUNK, GLA_CHUNK), 1)
    causal = row >= col
    sub_chunks = GLA_SUB // GLA_CHUNK
    decay = {}
    for sb in range(tt // GLA_SUB):
        blk = slice(sb * GLA_SUB, (sb + 1) * GLA_SUB)
        x = _channel_mix_residual(x_ref[blk, :], y_ref[blk, :], hprev_ref[blk, :], modprev_ref[:, 5 * D:6 * D],
                                  ngprev_ref[3:4, :], swg_ref, swu_ref, swd_ref)
        x_scr[blk, :] = x
        hb = (_rms(x, ng[0:1]) * (1.0 + sc1) + sh1).astype(BF16)
        hb_scr[blk, :] = hb
        q = jnp.dot(hb, wqkvg_ref[:, :GLA_DK_TOT], preferred_element_type=F32) * (GLA_DK ** -0.5)
        k = jnp.dot(hb, wqkvg_ref[:, GLA_DK_TOT:2 * GLA_DK_TOT], preferred_element_type=F32)
        v = jnp.dot(hb, wqkvg_ref[:, 2 * GLA_DK_TOT:2 * GLA_DK_TOT + GLA_DV_TOT],
                    preferred_element_type=F32).astype(BF16)
        log_a = _gla_gate(hb, wlr_ref, wgk_ref, bgk_ref)
        b = sum(jnp.dot(tril, p, preferred_element_type=F32) for p in _split3(log_a))
        for cl in range(sub_chunks):
            c = sb * sub_chunks + cl
            loc = slice(cl * GLA_CHUNK, (cl + 1) * GLA_CHUNK)
            rows = slice(c * GLA_CHUNK, (c + 1) * GLA_CHUNK)
            last = (cl + 1) * GLA_CHUNK - 1
            for hd in range(GLA_HEADS):
                kc = slice(hd * GLA_DK, (hd + 1) * GLA_DK)
                vc = slice(hd * GLA_DV, (hd + 1) * GLA_DV)
                bb = b[loc, kc]
                b_last = b[last:last + 1, kc]
                q_dec = (q[loc, kc] * jnp.exp(bb)).astype(BF16)
                k_inv = (k[loc, kc] * jnp.exp(-bb)).astype(BF16)
                k_end = (k[loc, kc] * jnp.exp(b_last - bb)).astype(BF16)
                att = jnp.where(causal, _dot_nt(q_dec, k_inv), 0.0).astype(BF16)
                qd_scr[rows, kc] = q_dec
                o_scr[rows, vc] = jnp.dot(att, v[loc, vc], preferred_element_type=F32)
                dst_scr[c * GLA_HEADS + hd] = lax.dot_general(
                    v[loc, vc], k_end, (((0,), (0,)), ((), ())), preferred_element_type=F32)
                decay[c, hd] = jnp.exp(b_last)
    states = [st_scr[hd] for hd in range(GLA_HEADS)]
    for sb in range(tt // GLA_SUB):
        blk = slice(sb * GLA_SUB, (sb + 1) * GLA_SUB)
        for c in range(sb * sub_chunks, (sb + 1) * sub_chunks):
            rows = slice(c * GLA_CHUNK, (c + 1) * GLA_CHUNK)
            for hd in range(GLA_HEADS):
                kc = slice(hd * GLA_DK, (hd + 1) * GLA_DK)
                vc = slice(hd * GLA_DV, (hd + 1) * GLA_DV)
                o_scr[rows, vc] += _dot_nt(qd_scr[rows, kc], states[hd].astype(BF16))
                states[hd] = states[hd] * decay[c, hd] + dst_scr[c * GLA_HEADS + hd]
        hb = hb_scr[blk, :]
        go = jnp.dot(hb, wqkvg_ref[:, 2 * GLA_DK_TOT + GLA_DV_TOT:], preferred_element_type=F32)
        y = jnp.dot(_gla_out(o_scr[blk, :], go, gng_ref[...]), wout_ref[...], preferred_element_type=F32)
        x1 = x_scr[blk, :] + g1 * _rms(y, ng[1:2])
        x1_ref[blk, :] = x1
        _ffn_prep(x1, ng, sh2, sc2, rw_ref, h_ref, lg_ref, rows=blk)
    for hd in range(GLA_HEADS):
        st_scr[hd] = states[hd]

    @pl.when(j == pl.num_programs(1) - 1)
    def _():
        for hd in range(GLA_HEADS):
            st_ref[hd] = st_scr[hd].T


def _gla_mixer(x2d, routed, hp, mod3_prev, ng_prev, shared_prev, mod3, batch, seq, ng, wqkvg, wlr, wgk, bgk,
               gng, wout, rw_t, ffn_rows=None):
    tt = GLA_TILE
    tpb = seq // tt
    n = x2d.shape[0]
    idx = jnp.arange(GLA_SUB)
    tril = ((idx[:, None] >= idx[None, :]) &
            (idx[:, None] // GLA_CHUNK == idx[None, :] // GLA_CHUNK)).astype(BF16)
    row_map = lambda b, j: (b * tpb + j, 0)
    mod_map = lambda b, j: (b, 0, 0)
    consts = (ng, wqkvg, wlr, wgk, bgk, tril, gng, wout, rw_t)
    prev_consts = (ng_prev,) + tuple(shared_prev)
    ffn_shapes, alias_bufs, _ = _ffn_out(n, ffn_rows, None)
    args = [x2d, routed, hp, mod3_prev, *prev_consts, mod3, *consts]
    return pl.pallas_call(
        functools.partial(_gla_body, tt=tt, n_alias=len(alias_bufs)),
        grid=(batch, tpb),
        in_specs=[pl.BlockSpec((tt, D), row_map), pl.BlockSpec((tt, D), row_map),
                  pl.BlockSpec((tt, DP), row_map), pl.BlockSpec((None, 1, 6 * D), mod_map)] +
                 [_const_spec(a.shape) for a in prev_consts] +
                 [pl.BlockSpec((None, 1, 6 * D), mod_map)] +
                 [_const_spec(a.shape) for a in consts] +
                 [pl.BlockSpec(memory_space=pl.ANY)] * len(alias_bufs),
        out_specs=[pl.BlockSpec((tt, D), row_map), pl.BlockSpec((tt, DP), row_map),
                   pl.BlockSpec((N_EXPERTS, tt), lambda b, j: (0, b * tpb + j)),
                   pl.BlockSpec((None, GLA_HEADS, GLA_DK, GLA_DV), lambda b, j: (b, 0, 0, 0))],
        out_shape=[jax.ShapeDtypeStruct((n, D), F32)] + ffn_shapes +
                  [jax.ShapeDtypeStruct((batch, GLA_HEADS, GLA_DK, GLA_DV), F32)],
        input_output_aliases={len(args) + i: 1 + i for i in range(len(alias_bufs))},
        scratch_shapes=[pltpu.VMEM((GLA_HEADS, GLA_DV, GLA_DK), F32),
                        pltpu.VMEM((tt, GLA_DV_TOT), F32),
                        pltpu.VMEM((tt, GLA_DK_TOT), BF16),
                        pltpu.VMEM((tt // GLA_CHUNK * GLA_HEADS, GLA_DV, GLA_DK), F32),
                        pltpu.VMEM((tt, D), F32), pltpu.VMEM((tt, D), BF16)],
        compiler_params=_cparams("parallel", "arbitrary"),
        name="gla_mixer",
    )(*args, *alias_bufs)


def _gla1_proj_body(x_ref, mod_ref, ng_ref, wqkvg_ref, wlr_ref, wgk_ref, bgk_ref,
                    q_ref, k_ref, v_ref, go_ref, dec_ref):
    sh1, sc1, _, _, _, _ = _mod_slices(mod_ref)
    ng = ng_ref[...]
    hb = (_rms(x_ref[...], ng[0:1]) * (1.0 + sc1) + sh1).astype(BF16)
    proj = jnp.dot(hb, wqkvg_ref[...], preferred_element_type=F32)
    q_ref[...] = proj[:, :GLA_DK_TOT] * (GLA_DK ** -0.5)
    k_ref[...] = proj[:, GLA_DK_TOT:2 * GLA_DK_TOT]
    v_ref[...] = proj[:, 2 * GLA_DK_TOT:2 * GLA_DK_TOT + GLA_DV_TOT]
    go_ref[...] = proj[:, 2 * GLA_DK_TOT + GLA_DV_TOT:]
    dec_ref[...] = jnp.exp(_gla_gate(hb, wlr_ref, wgk_ref, bgk_ref))


GLA1_TOK = 8


def _gla1_state_body(st_ref, qc_ref, kc_ref, dc_ref, v_ref, nst_ref, o_ref):
    v = v_ref[...]
    for i in range(GLA1_TOK):
        for hd in range(GLA_HEADS):
            vrow = v[i:i + 1, hd * GLA_DV:(hd + 1) * GLA_DV]
            s_new = dc_ref[hd][:, i:i + 1] * st_ref[i, hd] + kc_ref[hd][:, i:i + 1] * vrow
            nst_ref[i, hd] = s_new
            o_ref[i:i + 1, hd * GLA_DV:(hd + 1) * GLA_DV] = jnp.sum(
                qc_ref[hd][:, i:i + 1] * s_new, axis=0, keepdims=True)


def _gla1_out_body(x_ref, o_ref, go_ref, mod_ref, ng_ref, gng_ref, wout_ref, rw_ref, *rest):
    x1_ref, h_ref, lg_ref = rest[-3:]
    _, _, g1, sh2, sc2, _ = _mod_slices(mod_ref)
    ng = ng_ref[...]
    y = jnp.dot(_gla_out(o_ref[...], go_ref[...], gng_ref[...]), wout_ref[...], preferred_element_type=F32)
    x1 = x_ref[...] + g1 * _rms(y, ng[1:2])
    x1_ref[...] = x1
    _ffn_prep(x1, ng, sh2, sc2, rw_ref, h_ref, lg_ref)


def _gla_mixer_one(x2d, mod2, state, ng, wqkvg, wlr, wgk, bgk, gng, wout, rw_t, into=None):
    n = x2d.shape[0]
    consts = (ng, wqkvg, wlr, wgk, bgk)
    q, k, v, go, dec = pl.pallas_call(
        _gla1_proj_body,
        in_specs=[_const_spec(a.shape) for a in (x2d, mod2) + consts],
        out_specs=[_const_spec((n, GLA_DK_TOT)), _const_spec((n, GLA_DK_TOT)), _const_spec((n, GLA_DV_TOT)),
                   _const_spec((n, GLA_DV_TOT)), _const_spec((n, GLA_DK_TOT))],
        out_shape=[jax.ShapeDtypeStruct((n, GLA_DK_TOT), F32), jax.ShapeDtypeStruct((n, GLA_DK_TOT), F32),
                   jax.ShapeDtypeStruct((n, GLA_DV_TOT), F32), jax.ShapeDtypeStruct((n, GLA_DV_TOT), F32),
                   jax.ShapeDtypeStruct((n, GLA_DK_TOT), F32)],
        grid=(1,),
        compiler_params=_cparams("arbitrary"),
        name="gla1_proj",
    )(x2d, mod2, *consts)

    def cols(a):
        return a.reshape(n // GLA1_TOK, GLA1_TOK, GLA_HEADS, GLA_DK).transpose(0, 2, 3, 1)

    col_spec = pl.BlockSpec((None, GLA_HEADS, GLA_DK, GLA1_TOK), lambda i: (i, 0, 0, 0))
    st_spec = pl.BlockSpec((GLA1_TOK, GLA_HEADS, GLA_DK, GLA_DV), lambda i: (i, 0, 0, 0))
    new_state, o = pl.pallas_call(
        _gla1_state_body,
        grid=(n // GLA1_TOK,),
        in_specs=[st_spec, col_spec, col_spec, col_spec, pl.BlockSpec((GLA1_TOK, GLA_DV_TOT), lambda i: (i, 0))],
        out_specs=[st_spec, pl.BlockSpec((GLA1_TOK, GLA_DV_TOT), lambda i: (i, 0))],
        out_shape=[jax.ShapeDtypeStruct(state.shape, F32), jax.ShapeDtypeStruct((n, GLA_DV_TOT), F32)],
        compiler_params=_cparams("parallel"),
        name="gla1_state",
    )(state, cols(q), cols(k), cols(dec), v)

    consts = (mod2, ng, gng, wout, rw_t)
    ffn_shapes, alias_bufs, oblk = _ffn_out(n, None, into)
    n_in = 3 + len(consts)
    x1, h, lg = pl.pallas_call(
        _gla1_out_body,
        grid=(1,),
        in_specs=[_const_spec(a.shape) for a in (x2d, o, go) + consts] +
                 [pl.BlockSpec(memory_space=pl.ANY)] * len(alias_bufs),
        out_specs=[_const_spec((n, D)), pl.BlockSpec((n, DP), lambda i: (oblk, 0)),
                   pl.BlockSpec((N_EXPERTS, n), lambda i: (0, oblk))],
        out_shape=[jax.ShapeDtypeStruct((n, D), F32)] + ffn_shapes,
        input_output_aliases={n_in + i: 1 + i for i in range(len(alias_bufs))},
        compiler_params=_cparams("arbitrary"),
        name="gla1_out",
    )(x2d, o, go, *consts, *alias_bufs)
    return x1, h, lg, new_state


def _moe_routed(h, lg, n, router_bias, wg, wu, wd):
    n_pad = h.shape[0]
    eid, rank, wts, counts = _router(lg, router_bias, n)
    tile_count = ((counts[:, 0] + EXPERT_TILE - 1) // EXPERT_TILE).astype(jnp.int32)
    tile_end = jnp.cumsum(tile_count).astype(jnp.int32)
    tile_first = tile_end - tile_count
    off = tile_first * EXPERT_TILE
    p_alloc = TOP_K * n_pad + N_EXPERTS * EXPERT_TILE
    dest = _dest(off, eid, rank, n, p_alloc - 1)
    dest_w = dest.reshape(TOP_K, n_pad // DISPATCH_W, DISPATCH_W).transpose(1, 0, 2)
    xs = _sc_dispatch(h, dest_w, p_alloc)
    ys = _experts(xs, tile_first, tile_count, tile_end[-1:], wg, wu, wd)
    dest_tm = dest.T.reshape(n_pad // SUM_W, SUM_PARTS, SUM_W * TOP_K // SUM_PARTS)
    w_lanes = jnp.repeat(wts.T, SC_LANES, axis=1)
    return _sc_gather_sum(ys, dest_tm, w_lanes)


def kernel(x_prompt, x_sample, state_gla, c_prompt, c_sample, norm_g, ada_w, ada_b, gm_w_in, gm_b_in,
           gm_ln_g, gm_ln_b, gm_w_s, gm_b_s, gm_w_out, gla_w_in, gla_w_gk, gla_b_gk, gla_norm_g,
           gla_w_out, router_w, router_bias, exp_w_gate, exp_w_up, exp_w_down, sh_w_gate, sh_w_up,
           sh_w_down):
    batch, seq, _ = x_prompt.shape
    n_s = x_sample.shape[0]
    n_p = batch * seq
    tpb = seq // MIX_TILE
    xp = x_prompt.reshape(n_p, D)
    xs = x_sample.reshape(n_s, D)

    mod = _ada(jnp.concatenate([c_prompt, c_sample], axis=0), ada_w, ada_b)
    mod_p = [mod[i, :batch].reshape(batch, 1, 6 * D) for i in range(2)]
    mod_s = [mod[i, batch:] for i in range(2)]
    rw_t = [jnp.concatenate(_split3(router_w[i].T), axis=0) for i in range(2)]

    ws_causal = jnp.tril(gm_w_s[0]).astype(BF16)
    bs_cols = gm_b_s[0].T
    eye = jnp.eye(GM_CHUNK, dtype=F32)
    ws_first = (gm_w_s[0][:, 0, 0][:, None, None] * eye).astype(BF16)
    bs_first = jnp.broadcast_to(gm_b_s[0][:, 0][None, :], (GM_CHUNK, GM_GROUPS))
    gm_args = (norm_g[0], gm_w_in[0].astype(BF16), gm_b_in[0].reshape(1, -1), gm_ln_g[0].reshape(1, -1),
               gm_ln_b[0].reshape(1, -1))
    wout0 = gm_w_out[0].astype(BF16)
    shared = [(sh_w_gate[i].astype(BF16), sh_w_up[i].astype(BF16), sh_w_down[i].astype(BF16))
              for i in range(2)]
    n_qkvg = 2 * GLA_DK_TOT + 2 * GLA_DV_TOT
    wqkvg = gla_w_in[0][:, :n_qkvg].astype(BF16)
    wlr = jnp.pad(gla_w_in[0][:, n_qkvg:], ((0, 0), (0, LANES - GLA_GATE_RANK))).astype(BF16)
    wgk = jnp.pad(gla_w_gk[0], ((0, LANES - GLA_GATE_RANK), (0, 0))).astype(BF16)
    gla_args = (norm_g[1], wqkvg, wlr, wgk, gla_b_gk[0].reshape(1, -1), gla_norm_g[0].reshape(1, -1),
                gla_w_out[0].astype(BF16), rw_t[1])
    experts_f32 = (exp_w_gate, exp_w_up, exp_w_down)

    half = batch // 2
    streams = [(0, half, False), (half, batch - half, True)]
    st = [dict() for _ in streams]

    experts = []
    for layer, (s, (b0, nb, with_new)) in enumerate(zip(st, streams)):
        s["mod_p"] = [mod_p[i][b0:b0 + nb] for i in range(2)]
        s["n"] = nb * seq
        s["n_all"] = s["n"] + (n_s if with_new else 0)
        s["n_pad"] = -(-s["n_all"] // TOKEN_PAD) * TOKEN_PAD
        s["x1p"], s["h"], s["lg"], *w16 = _gmlp_mixer(
            xp, b0 * tpb, s["n"], s["mod_p"][0], False, MIX_TILE, tpb, *gm_args, ws_causal, bs_cols, wout0,
            rw_t[0], emit_v=False, cast_w=(layer, *experts_f32), ffn_rows=s["n_pad"])
        experts.append(w16)
        if with_new:
            s["x1s"], s["h"], s["lg"], v_rows = _gmlp_mixer(
                xs, 0, n_s, mod_s[0], True, n_s, 1, *gm_args, ws_first, bs_first, wout0, rw_t[0], emit_v=True,
                into=(s["h"], s["lg"], s["n"] // n_s))
    for s, (b0, nb, with_new) in zip(st, streams):
        s["routed0"] = _moe_routed(s["h"], s["lg"], s["n_all"], router_bias[0], *experts[0])
        if with_new:
            s["x2s"] = _combine(s["x1s"], s["routed0"], s["h"], s["n"] // n_s, mod_s[0], True, n_s, 1,
                                norm_g[0], *shared[0])
    for s, (b0, nb, with_new) in zip(st, streams):
        s["x3p"], h1, lg1, s["st_p"] = _gla_mixer(s["x1p"], s["routed0"], s["h"], s["mod_p"][0], norm_g[0],
                                                  shared[0], s["mod_p"][1], nb, seq, *gla_args,
                                                  ffn_rows=s["n_pad"])
        if with_new:
            s["x3s"], h1, lg1, st_s = _gla_mixer_one(s["x2s"], mod_s[1], state_gla[:, 0], *gla_args,
                                                     into=(h1, lg1, s["n"] // n_s))
        s["h"], s["lg"] = h1, lg1
    y_prompt = None
    for s, (b0, nb, with_new) in zip(st, streams):
        routed = _moe_routed(s["h"], s["lg"], s["n_all"], router_bias[1], *experts[1])
        y_prompt = _combine(s["x3p"], routed, s["h"], 0, s["mod_p"][1], False, MIX_TILE, tpb, norm_g[1],
                            *shared[1], out_rows=n_p, out_blk0=b0 * tpb, out_buf=y_prompt)
        if with_new:
            y_new = _combine(s["x3s"], routed, s["h"], s["n"] // n_s, mod_s[1], True, n_s, 1, norm_g[1],
                             *shared[1])
    st_p = jnp.concatenate([s["st_p"] for s in st], axis=0)

    return (y_prompt.reshape(batch, seq, D), y_new.reshape(n_s, 1, D), st_p[:, None], st_s[:, None],
            v_rows.reshape(n_s, 1, 1, GM_HALF))
```

```python
import functools
import math

import jax
import jax.numpy as jnp
from jax import lax
from jax.experimental import pallas as pl
from jax.experimental.pallas import tpu as pltpu
from jax.experimental.pallas import tpu_sc as plsc

F32 = jnp.float32
BF16 = jnp.bfloat16

D = 1024
DP = D // 2
GM_CHUNK = 128
GM_HALF = 2 * D
GM_GROUPS = 8
GM_GROUP_DIM = GM_HALF // GM_GROUPS
GLA_HEADS = 4
GLA_DK = 128
GLA_DV = 256
GLA_DK_TOT = GLA_HEADS * GLA_DK
GLA_DV_TOT = GLA_HEADS * GLA_DV
GLA_GATE_RANK = 16
GLA_GATE_NORMALIZER = 16.0
GLA_CHUNK = 64
N_EXPERTS = 64
TOP_K = 8
N_EXPERT_GROUPS = 8
GROUP_SIZE = N_EXPERTS // N_EXPERT_GROUPS
TOPK_GROUPS = 4
EXPERT_DIM = D // 4
ROUTED_SCALE = 2.5
NORM_EPS = 1e-6
LN_EPS = 1e-5

LANES = 128
VMEM_LIMIT = 56 * 1024 * 1024

MIX_TILE = 256
GLA_TILE = 512
GLA_CUM_BLOCK = 256
GM_COL_BLOCK = 512
ROUTER_TILE = 1024
EXPERT_TILE = 544
EXPERT_X_SLOTS = 6
EXPERT_AHEAD = EXPERT_X_SLOTS - 2
EXPERT_Y_SLOTS = 4
SC_WORKERS = 32
DISPATCH_W = 32
SC_LANES = 16
SUM_W = 16
SUM_PARTS = 4
SUM_UNROLL = 8
TOKEN_PAD = SC_WORKERS * DISPATCH_W


def _cparams(*sem):
    return pltpu.CompilerParams(dimension_semantics=sem, vmem_limit_bytes=VMEM_LIMIT)


def _rms(x, g):
    return x * lax.rsqrt(jnp.mean(x * x, axis=-1, keepdims=True) + NORM_EPS) * g


def _silu(x):
    return x * (1.0 / (1.0 + jnp.exp(-x)))


def _gelu(x):
    return 0.5 * x * (1.0 + lax.erf(x * (1.0 / math.sqrt(2.0))))


def _bdot(a, b):
    return jnp.dot(a.astype(BF16), b.astype(BF16), preferred_element_type=F32)


def _dot_nt(a, b, precision=None):
    return lax.dot_general(a, b, (((1,), (1,)), ((), ())), preferred_element_type=F32,
                           precision=precision)


def _mod_slices(mod_ref):
    return [mod_ref[:, i * D:(i + 1) * D] for i in range(6)]


HI_HALF = -65536


def _pack_rows(x):
    lo = lax.bitcast_convert_type(x[:, :DP].astype(BF16).astype(F32), jnp.int32)
    hi = lax.bitcast_convert_type(x[:, DP:].astype(BF16).astype(F32), jnp.int32)
    return lax.shift_right_logical(lo, 16) | (hi & HI_HALF)


def _unpack_rows(p):
    lo = lax.bitcast_convert_type(lax.shift_left(p, 16), F32)
    hi = lax.bitcast_convert_type(p & HI_HALF, F32)
    return lo, hi


def _ffn_prep(x1, ng, sh2, sc2, rw_ref, h_ref, lg_ref, rows=slice(None)):
    hffn = _rms(x1, ng[2:3]) * (1.0 + sc2) + sh2
    h_ref[rows, :] = _pack_rows(hffn)
    lg3 = _dot_nt(rw_ref[...], hffn.astype(BF16))
    lg_ref[:, rows] = lg3[:N_EXPERTS] + lg3[N_EXPERTS:2 * N_EXPERTS] + lg3[2 * N_EXPERTS:]


def _ada_body(c_ref, w_ref, b_ref, o_ref):
    c = c_ref[...]
    o_ref[...] = _bdot(_silu(c), w_ref[...]) + b_ref[...]


def _ada(c, ada_w, ada_b):
    n = c.shape[0]
    depth = ada_w.shape[0]
    tn = 1536
    return pl.pallas_call(
        _ada_body,
        grid=(depth, 6 * D // tn),
        in_specs=[pl.BlockSpec((n, D), lambda l, j: (0, 0)),
                  pl.BlockSpec((None, D, tn), lambda l, j: (l, 0, j)),
                  pl.BlockSpec((None, 1, tn), lambda l, j: (l, 0, j))],
        out_specs=pl.BlockSpec((None, n, tn), lambda l, j: (l, 0, j)),
        out_shape=jax.ShapeDtypeStruct((depth, n, 6 * D), F32),
        compiler_params=_cparams("parallel", "parallel"),
        name="ada_mod",
    )(c, ada_w, ada_b.reshape(depth, 1, 6 * D))


def _mod_spec(per_row, tt, tiles_per_batch):
    if per_row:
        return pl.BlockSpec((tt, 6 * D), lambda i: (i, 0))
    return pl.BlockSpec((None, 1, 6 * D), lambda i: (i // tiles_per_batch, 0, 0))


def _const_spec(shape):
    zeros = (0,) * len(shape)
    return pl.BlockSpec(shape, lambda *_: zeros)


def _gmlp_body(x_ref, mod_ref, ng_ref, win_ref, bin_ref, lng_ref, lnb_ref, ws_ref, bs_ref, wout_ref,
               rw_ref, *rest, n_chunks, emit_v, cast_w, n_alias):
    rest = list(rest)
    w32_refs = [rest.pop(0) for _ in range(3)] if cast_w else []
    rest = rest[n_alias:]
    x1_ref, h_ref, lg_ref = rest[:3]
    rest = rest[3:]
    v_ref = rest.pop(0) if emit_v else None
    w16_refs = [rest.pop(0) for _ in range(3)] if cast_w else []
    um_ref, z_ref = rest
    for src, dst in zip(w32_refs, w16_refs):
        dst[...] = src[...].astype(BF16)
    sh1, sc1, g1, sh2, sc2, _ = _mod_slices(mod_ref)
    ng = ng_ref[...]
    x = x_ref[...]
    hb = (_rms(x, ng[0:1]) * (1.0 + sc1) + sh1).astype(BF16)
    for cb in range(2 * GM_HALF // GM_COL_BLOCK):
        cols = slice(cb * GM_COL_BLOCK, (cb + 1) * GM_COL_BLOCK)
        z_ref[:, cols] = _gelu(jnp.dot(hb, win_ref[:, cols], preferred_element_type=F32) + bin_ref[:, cols])
    u = z_ref[:, :GM_HALF]
    v = z_ref[:, GM_HALF:]
    mu = jnp.mean(v, axis=-1, keepdims=True)
    vc = v - mu
    var = jnp.mean(vc * vc, axis=-1, keepdims=True)
    v = vc * lax.rsqrt(var + LN_EPS) * lng_ref[...] + lnb_ref[...]
    if emit_v:
        v_ref[...] = v
    vb = v.astype(BF16)
    for c in range(n_chunks):
        rows = slice(c * GM_CHUNK, (c + 1) * GM_CHUNK)
        for g in range(GM_GROUPS):
            cols = slice(g * GM_GROUP_DIM, (g + 1) * GM_GROUP_DIM)
            mixed = jnp.dot(ws_ref[g], vb[rows, cols], preferred_element_type=F32) + bs_ref[:, g:g + 1]
            um_ref[rows, cols] = (u[rows, cols] * mixed).astype(BF16)
    y = jnp.dot(um_ref[...], wout_ref[...], preferred_element_type=F32)
    x1 = x + g1 * _rms(y, ng[1:2])
    x1_ref[...] = x1
    _ffn_prep(x1, ng, sh2, sc2, rw_ref, h_ref, lg_ref)


def _ffn_out(n, ffn_rows, into):
    rows = ffn_rows or n
    oblk = 0
    bufs = []
    if into is not None:
        *bufs, oblk = into
        rows = bufs[0].shape[0]
    elif rows != n:
        bufs = [jnp.zeros((rows, DP), jnp.int32), jnp.zeros((N_EXPERTS, rows), F32)]
    shapes = [jax.ShapeDtypeStruct((rows, DP), jnp.int32), jax.ShapeDtypeStruct((N_EXPERTS, rows), F32)]
    return shapes, bufs, oblk


def _gmlp_mixer(x2d, blk0, n, mod, per_row, tt, tiles_per_batch, ng, win, b_in, ln_g, ln_b, ws, bs, wout,
                rw_t, emit_v, cast_w=None, ffn_rows=None, into=None):
    steps = n // tt
    ffn_shapes, alias_bufs, oblk = _ffn_out(n, ffn_rows, into)
    out_shape = [jax.ShapeDtypeStruct((n, D), F32)] + ffn_shapes
    out_specs = [pl.BlockSpec((tt, D), lambda i: (i, 0)), pl.BlockSpec((tt, DP), lambda i: (i + oblk, 0)),
                 pl.BlockSpec((N_EXPERTS, tt), lambda i: (0, i + oblk))]
    if emit_v:
        out_shape.append(jax.ShapeDtypeStruct((n, GM_HALF), F32))
        out_specs.append(pl.BlockSpec((tt, GM_HALF), lambda i: (i, 0)))

    def one_buffer(a):
        zeros = (0,) * a.ndim
        return pl.BlockSpec(a.shape, lambda *_: zeros, pipeline_mode=pl.Buffered(1))

    consts = (ng, win, b_in, ln_g, ln_b, ws, bs, wout, rw_t)
    in_specs = [pl.BlockSpec((tt, D), lambda i: (i + blk0, 0)), _mod_spec(per_row, tt, tiles_per_batch)]
    in_specs += [one_buffer(a) for a in consts]
    args = [x2d, mod, *consts]
    if cast_w is not None:
        layer, *w_all = cast_w
        per_step = N_EXPERTS // steps
        for w in w_all:
            blk = (None, per_step) + w.shape[2:]
            in_specs.append(pl.BlockSpec(blk, lambda i: (layer, i, 0, 0)))
            out_specs.append(pl.BlockSpec(blk[1:], lambda i: (i, 0, 0)))
            out_shape.append(jax.ShapeDtypeStruct(w.shape[1:], BF16))
            args.append(w)
    aliases = {len(args) + i: 1 + i for i in range(len(alias_bufs))}
    in_specs += [pl.BlockSpec(memory_space=pl.ANY)] * len(alias_bufs)
    args += alias_bufs
    return pl.pallas_call(
        functools.partial(_gmlp_body, n_chunks=tt // GM_CHUNK, emit_v=emit_v, cast_w=cast_w is not None,
                          n_alias=len(alias_bufs)),
        grid=(steps,),
        in_specs=in_specs,
        out_specs=out_specs,
        out_shape=out_shape,
        input_output_aliases=aliases,
        scratch_shapes=[pltpu.VMEM((tt, GM_HALF), BF16), pltpu.VMEM((tt, 2 * GM_HALF), F32)],
        compiler_params=_cparams("parallel"),
        name="gmlp_mixer_rows" if per_row else "gmlp_mixer",
    )(*args)


def _combine_body(x_ref, y_ref, h_ref, mod_ref, ng_ref, swg_ref, swu_ref, swd_ref, *rest):
    o_ref = rest[-1]
    o_ref[...] = _channel_mix_residual(x_ref[...], y_ref[...], h_ref[...], mod_ref[:, 5 * D:6 * D],
                                       ng_ref[3:4, :], swg_ref, swu_ref, swd_ref)


def _combine(x2d, routed, hp, blk0, mod, per_row, tt, tiles_per_batch, ng, swg, swu, swd,
             out_rows=None, out_blk0=0, out_buf=None):
    n = x2d.shape[0]
    in_specs = [pl.BlockSpec((tt, D), lambda i: (i, 0)),
                pl.BlockSpec((tt, D), lambda i: (i + blk0, 0)),
                pl.BlockSpec((tt, DP), lambda i: (i + blk0, 0)),
                _mod_spec(per_row, tt, tiles_per_batch),
                _const_spec(ng.shape), _const_spec(swg.shape), _const_spec(swu.shape),
                _const_spec(swd.shape)]
    args = [x2d, routed, hp, mod, ng, swg, swu, swd]
    aliases = {}
    if out_buf is not None:
        in_specs.append(pl.BlockSpec(memory_space=pl.ANY))
        aliases = {len(args): 0}
        args.append(out_buf)
    return pl.pallas_call(
        _combine_body,
        grid=(n // tt,),
        in_specs=in_specs,
        out_specs=pl.BlockSpec((tt, D), lambda i: (i + out_blk0, 0)),
        out_shape=jax.ShapeDtypeStruct((out_rows or n, D), F32),
        input_output_aliases=aliases,
        compiler_params=_cparams("parallel"),
        name="combine_rows" if per_row else "combine",
    )(*args)


def _router_body(lg_ref, bias_ref, tri_ref, eid_ref, rank_ref, wts_ref, cnt_ref, carry_ref, *, n_real):
    step = pl.program_id(0)

    @pl.when(step == 0)
    def _():
        carry_ref[...] = jnp.zeros_like(carry_ref)

    lg = lg_ref[...]
    tn = lg.shape[1]
    real = (step * tn + lax.broadcasted_iota(jnp.int32, (1, tn), 1)) < n_real
    lg = jnp.where(real, lg, 0.0)
    scores = 1.0 / (1.0 + jnp.exp(-lg))
    sel = scores + bias_ref[...]
    neg = -jnp.inf
    sub8 = lax.broadcasted_iota(jnp.int32, (GROUP_SIZE, tn), 0)
    gsub = lax.broadcasted_iota(jnp.int32, (N_EXPERT_GROUPS, tn), 0)
    gs = jnp.zeros((N_EXPERT_GROUPS, tn), F32)
    for g in range(N_EXPERT_GROUPS):
        blk = sel[g * GROUP_SIZE:(g + 1) * GROUP_SIZE, :]
        m1 = jnp.max(blk, axis=0, keepdims=True)
        i1 = jnp.min(jnp.where(blk == m1, sub8, GROUP_SIZE), axis=0, keepdims=True)
        m2 = jnp.max(jnp.where(sub8 == i1, neg, blk), axis=0, keepdims=True)
        gs = jnp.where(gsub == g, m1 + m2, gs)
    gmask = jnp.zeros((N_EXPERT_GROUPS, tn), jnp.bool_)
    for _ in range(TOPK_GROUPS):
        m = jnp.max(gs, axis=0, keepdims=True)
        i = jnp.min(jnp.where(gs == m, gsub, N_EXPERT_GROUPS), axis=0, keepdims=True)
        hit = gsub == i
        gmask = jnp.logical_or(gmask, hit)
        gs = jnp.where(hit, neg, gs)
    gmaskf = gmask.astype(F32)
    blocks = []
    for g in range(N_EXPERT_GROUPS):
        keep = jnp.broadcast_to(gmaskf[g:g + 1, :], (GROUP_SIZE, tn)) > 0.5
        blocks.append(jnp.where(keep, sel[g * GROUP_SIZE:(g + 1) * GROUP_SIZE, :], neg))
    msel = jnp.concatenate(blocks, axis=0)
    esub = lax.broadcasted_iota(jnp.int32, (N_EXPERTS, tn), 0)
    chosen = jnp.zeros((N_EXPERTS, tn), jnp.bool_)
    picks = []
    for _ in range(TOP_K):
        m = jnp.max(msel, axis=0, keepdims=True)
        i = jnp.min(jnp.where(msel == m, esub, N_EXPERTS), axis=0, keepdims=True)
        hit = esub == i
        picks.append(i)
        chosen = jnp.logical_or(chosen, hit)
        msel = jnp.where(hit, neg, msel)
    w = jnp.where(chosen, scores, 0.0)
    w = w / jnp.sum(w, axis=0, keepdims=True) * ROUTED_SCALE
    counted = jnp.where(jnp.logical_and(chosen, real), 1.0, 0.0)
    incl = jnp.dot(counted.astype(BF16), tri_ref[...], preferred_element_type=F32)
    rank_full = carry_ref[:, 0:1] + incl - 1.0
    ksub = lax.broadcasted_iota(jnp.int32, (TOP_K, tn), 0)
    eid = jnp.zeros((TOP_K, tn), jnp.int32)
    rank = jnp.zeros((TOP_K, tn), F32)
    wts = jnp.zeros((TOP_K, tn), F32)
    for k in range(TOP_K):
        hit = esub == picks[k]
        eid = jnp.where(ksub == k, picks[k], eid)
        rank = jnp.where(ksub == k, jnp.sum(jnp.where(hit, rank_full, 0.0), axis=0, keepdims=True), rank)
        wts = jnp.where(ksub == k, jnp.sum(jnp.where(hit, w, 0.0), axis=0, keepdims=True), wts)
    eid_ref[...] = eid
    rank_ref[...] = rank.astype(jnp.int32)
    wts_ref[...] = wts
    carry = carry_ref[...] + incl[:, tn - 1:tn]
    carry_ref[...] = carry
    cnt_ref[...] = carry.astype(jnp.int32)


def _router(lg_t, bias, n_real):
    n = lg_t.shape[1]
    tn = ROUTER_TILE
    idx = jnp.arange(tn)
    tri = (idx[:, None] <= idx[None, :]).astype(BF16)
    kspec = pl.BlockSpec((TOP_K, tn), lambda i: (0, i))
    return pl.pallas_call(
        functools.partial(_router_body, n_real=n_real),
        grid=(n // tn,),
        in_specs=[pl.BlockSpec((N_EXPERTS, tn), lambda i: (0, i)), _const_spec((N_EXPERTS, 1)),
                  _const_spec((tn, tn))],
        out_specs=[kspec, kspec, kspec, _const_spec((N_EXPERTS, LANES))],
        out_shape=[jax.ShapeDtypeStruct((TOP_K, n), jnp.int32), jax.ShapeDtypeStruct((TOP_K, n), jnp.int32),
                   jax.ShapeDtypeStruct((TOP_K, n), F32), jax.ShapeDtypeStruct((N_EXPERTS, LANES), jnp.int32)],
        scratch_shapes=[pltpu.VMEM((N_EXPERTS, LANES), F32)],
        compiler_params=_cparams("arbitrary"),
        name="router",
    )(lg_t, bias.reshape(N_EXPERTS, 1), tri)


def _dest_body(off_ref, eid_ref, rank_ref, dest_ref, *, n_real, last_row):
    eid = eid_ref[...]
    base = jnp.zeros(eid.shape, jnp.int32)
    for e in range(N_EXPERTS):
        base = jnp.where(eid == e, off_ref[e], base)
    tok = lax.broadcasted_iota(jnp.int32, eid.shape, 1)
    slot = lax.broadcasted_iota(jnp.int32, eid.shape, 0)
    unused = last_row - ((tok - n_real) * TOP_K + slot)
    dest_ref[...] = jnp.where(tok < n_real, base + rank_ref[...], unused)


def _dest(off, eid, rank, n_real, last_row):
    spec = pl.BlockSpec(eid.shape, lambda i, off_ref: (0, 0))
    return pl.pallas_call(
        functools.partial(_dest_body, n_real=n_real, last_row=last_row),
        grid_spec=pltpu.PrefetchScalarGridSpec(num_scalar_prefetch=1, grid=(1,), in_specs=[spec, spec],
                                               out_specs=spec),
        out_shape=jax.ShapeDtypeStruct(eid.shape, jnp.int32),
        compiler_params=_cparams("arbitrary"),
        name="dest_rows",
    )(off, eid, rank)


def _sc_mesh():
    return plsc.VectorSubcoreMesh(core_axis_name="core", subcore_axis_name="subcore")


def _sc_dispatch(hp, dest_w, p_alloc):
    n = hp.shape[0]
    w = dest_w.shape[2]

    @functools.partial(pl.kernel, out_type=jax.ShapeDtypeStruct((p_alloc, DP), jnp.int32), mesh=_sc_mesh(),
                       name="sc_dispatch")
    def run(hp_hbm, dest_hbm, xs_hbm):
        def body(x_vmem, i_vmem):
            for k in range(TOP_K):
                pltpu.sync_copy(x_vmem, xs_hbm.at[i_vmem.at[k]])

        pltpu.emit_pipeline(
            body,
            grid=(n // w,),
            in_specs=[pl.BlockSpec((w, DP), lambda i: (i, 0)),
                      pl.BlockSpec((None, TOP_K, w), lambda i: (i, 0, 0))],
            out_specs=[],
            core_axis_name=("core", "subcore"),
            dimension_semantics=(pltpu.PARALLEL,),
        )(hp_hbm, dest_hbm)

    return run(hp, dest_w)


def _sc_gather_sum(ys, dest_tm, w_lanes):
    n_win, parts, pk = dest_tm.shape
    w = parts * pk // TOP_K
    wp = w // parts
    n_vec = DP // SC_LANES

    @functools.partial(pl.kernel, out_type=jax.ShapeDtypeStruct((n_win * w, D), F32), mesh=_sc_mesh(),
                       scratch_types=[pltpu.VMEM((parts, pk, DP), jnp.int32), pltpu.SemaphoreType.DMA((parts,))],
                       compiler_params=pltpu.CompilerParams(needs_layout_passes=False), name="sc_gather_sum")
    def run(ys_hbm, dest_hbm, w_hbm, o_hbm, rows_v, sems):
        def body(i_vmem, w_vmem, o_vmem):
            copies = [pltpu.async_copy(ys_hbm.at[i_vmem.at[p]], rows_v.at[p], sems.at[p]) for p in range(parts)]
            for p in range(parts):
                copies[p].wait()

                @pl.loop(0, wp)
                def _(t):
                    tok = p * wp + t
                    wv = [w_vmem[tok, pl.ds(k * SC_LANES, SC_LANES)] for k in range(TOP_K)]

                    @plsc.parallel_loop(0, n_vec, unroll=SUM_UNROLL)
                    def _(j):
                        col = j * SC_LANES
                        lo = jnp.zeros((SC_LANES,), F32)
                        hi = jnp.zeros((SC_LANES,), F32)
                        for k in range(TOP_K):
                            word = rows_v[p, t * TOP_K + k, pl.ds(col, SC_LANES)]
                            lo = lo + wv[k] * plsc.bitcast(lax.shift_left(word, 16), F32)
                            hi = hi + wv[k] * plsc.bitcast(word & HI_HALF, F32)
                        o_vmem[tok, pl.ds(col, SC_LANES)] = lo
                        o_vmem[tok, pl.ds(DP + col, SC_LANES)] = hi

        pltpu.emit_pipeline(
            body,
            grid=(n_win,),
            in_specs=[pl.BlockSpec((None, parts, pk), lambda i: (i, 0, 0)),
                      pl.BlockSpec((w, TOP_K * SC_LANES), lambda i: (i, 0))],
            out_specs=[pl.BlockSpec((w, D), lambda i: (i, 0))],
            core_axis_name=("core", "subcore"),
            dimension_semantics=(pltpu.PARALLEL,),
        )(dest_hbm, w_hbm, o_hbm)

    return run(ys, dest_tm, w_lanes)


def _expert_body(first_ref, cnt_ref, nused_ref, xs_hbm, wg_s, wu_s, wd_s, ys_hbm, xbuf, ybuf, xsem, ysem):
    e = pl.program_id(0)
    n_used = nused_ref[0]

    def load(g):
        rows = pl.ds(pl.multiple_of(g * EXPERT_TILE, EXPERT_TILE), EXPERT_TILE)
        slot = g % EXPERT_X_SLOTS
        return pltpu.make_async_copy(xs_hbm.at[rows], xbuf.at[slot], xsem.at[slot])

    def store(g):
        rows = pl.ds(pl.multiple_of(g * EXPERT_TILE, EXPERT_TILE), EXPERT_TILE)
        slot = g % EXPERT_Y_SLOTS
        return pltpu.make_async_copy(ybuf.at[slot], ys_hbm.at[rows], ysem.at[slot])

    @pl.when(e == 0)
    def _():
        for g in range(EXPERT_AHEAD):
            @pl.when(g < n_used)
            def _():
                load(g).start()

    first = first_ref[e]
    cnt = cnt_ref[e]

    def acquire(g):
        ahead = g + EXPERT_AHEAD

        @pl.when(ahead < n_used)
        def _():
            load(ahead).start()

        load(g).wait()

        @pl.when(g >= EXPERT_Y_SLOTS)
        def _():
            store(g - EXPERT_Y_SLOTS).wait()

    def compute(g):
        lo, hi = _unpack_rows(xbuf[g % EXPERT_X_SLOTS])
        lo = lo.astype(BF16)
        hi = hi.astype(BF16)

        def xdot(w_s):
            return (jnp.dot(lo, w_s[:DP, :], preferred_element_type=F32) +
                    jnp.dot(hi, w_s[DP:, :], preferred_element_type=F32))

        a = (_silu(xdot(wg_s)) * xdot(wu_s)).astype(BF16)
        ybuf[g % EXPERT_Y_SLOTS] = _pack_rows(jnp.dot(a, wd_s[...], preferred_element_type=F32))

    def pair(j, carry):
        g = first + 2 * j
        acquire(g)
        acquire(g + 1)
        compute(g)
        compute(g + 1)
        store(g).start()
        store(g + 1).start()
        return carry

    lax.fori_loop(0, cnt // 2, pair, 0)

    @pl.when(cnt % 2 == 1)
    def _():
        g = first + cnt - 1
        acquire(g)
        compute(g)
        store(g).start()

    @pl.when(e == N_EXPERTS - 1)
    def _():
        for k in range(EXPERT_Y_SLOTS):
            g = n_used - 1 - k

            @pl.when(g >= 0)
            def _():
                store(g).wait()


def _experts(xs, tile_first, tile_count, n_used, wg, wu, wd):
    def w_map(e, first, cnt, nu):
        return (e, 0, 0)

    return pl.pallas_call(
        _expert_body,
        grid_spec=pltpu.PrefetchScalarGridSpec(
            num_scalar_prefetch=3, grid=(N_EXPERTS,),
            in_specs=[pl.BlockSpec(memory_space=pl.ANY),
                      pl.BlockSpec((None, D, EXPERT_DIM), w_map),
                      pl.BlockSpec((None, D, EXPERT_DIM), w_map),
                      pl.BlockSpec((None, EXPERT_DIM, D), w_map)],
            out_specs=pl.BlockSpec(memory_space=pl.ANY),
            scratch_shapes=[pltpu.VMEM((EXPERT_X_SLOTS, EXPERT_TILE, DP), jnp.int32),
                            pltpu.VMEM((EXPERT_Y_SLOTS, EXPERT_TILE, DP), jnp.int32),
                            pltpu.SemaphoreType.DMA((EXPERT_X_SLOTS,)),
                            pltpu.SemaphoreType.DMA((EXPERT_Y_SLOTS,))]),
        out_shape=jax.ShapeDtypeStruct(xs.shape, jnp.int32),
        compiler_params=_cparams("arbitrary"),
        name="experts",
    )(tile_first, tile_count, n_used, xs, wg, wu, wd)


def _log_sigmoid(z):
    return jnp.minimum(z, 0.0) - jnp.log(1.0 + jnp.exp(-jnp.abs(z)))


def _gla_gate(hb, wlr_ref, wgk_ref, bgk_ref):
    lr = jnp.dot(hb, wlr_ref[...], preferred_element_type=F32)
    z = _bdot(lr, wgk_ref[...]) + bgk_ref[...]
    return _log_sigmoid(z) * (1.0 / GLA_GATE_NORMALIZER)


def _split3(a):
    hi = a.astype(BF16)
    r1 = a - hi.astype(F32)
    mid = r1.astype(BF16)
    lo = (r1 - mid.astype(F32)).astype(BF16)
    return hi, mid, lo


def _gla_out(o_ref_val, go, gng):
    parts = []
    for hd in range(GLA_HEADS):
        cols = slice(hd * GLA_DV, (hd + 1) * GLA_DV)
        parts.append((_rms(o_ref_val[:, cols], gng) * _silu(go[:, cols])).astype(BF16))
    return jnp.concatenate(parts, axis=1)


def _channel_mix_residual(x, routed, h_packed, g2, ng3, swg_ref, swu_ref, swd_ref):
    h_lo, h_hi = _unpack_rows(h_packed)
    h_lo = h_lo.astype(BF16)
    h_hi = h_hi.astype(BF16)

    def hdot(w_ref_):
        return (jnp.dot(h_lo, w_ref_[:DP, :], preferred_element_type=F32) +
                jnp.dot(h_hi, w_ref_[DP:, :], preferred_element_type=F32))

    hs = (_silu(hdot(swg_ref)) * hdot(swu_ref)).astype(BF16)
    y = jnp.dot(hs, swd_ref[...], preferred_element_type=F32) + routed
    return x + g2 * _rms(y, ng3)


def _gla_body(x_ref, y_ref, hprev_ref, modprev_ref, ngprev_ref, swg_ref, swu_ref, swd_ref,
              mod_ref, ng_ref, wqkvg_ref, wlr_ref, wgk_ref, bgk_ref, tril_ref, gng_ref, wout_ref,
              rw_ref, *rest, tt, n_alias):
    x1_ref, h_ref, lg_ref, st_ref, st_scr, o_scr, qd_scr, dst_scr = rest[n_alias:]
    j = pl.program_id(1)

    @pl.when(j == 0)
    def _():
        st_scr[...] = jnp.zeros_like(st_scr)

    x = _channel_mix_residual(x_ref[...], y_ref[...], hprev_ref[...], modprev_ref[:, 5 * D:6 * D],
                              ngprev_ref[3:4, :], swg_ref, swu_ref, swd_ref)
    sh1, sc1, g1, sh2, sc2, _ = _mod_slices(mod_ref)
    ng = ng_ref[...]
    hb = (_rms(x, ng[0:1]) * (1.0 + sc1) + sh1).astype(BF16)
    q = jnp.dot(hb, wqkvg_ref[:, :GLA_DK_TOT], preferred_element_type=F32) * (GLA_DK ** -0.5)
    k = jnp.dot(hb, wqkvg_ref[:, GLA_DK_TOT:2 * GLA_DK_TOT], preferred_element_type=F32)
    v = jnp.dot(hb, wqkvg_ref[:, 2 * GLA_DK_TOT:2 * GLA_DK_TOT + GLA_DV_TOT],
                preferred_element_type=F32).astype(BF16)
    log_a = _gla_gate(hb, wlr_ref, wgk_ref, bgk_ref)
    tril = tril_ref[...]
    parts = _split3(log_a)
    b = jnp.concatenate(
        [sum(jnp.dot(tril, p[r:r + GLA_CUM_BLOCK], preferred_element_type=F32) for p in parts)
         for r in range(0, tt, GLA_CUM_BLOCK)], axis=0)
    row = lax.broadcasted_iota(jnp.int32, (GLA_CHUNK, GLA_CHUNK), 0)
    col = lax.broadcasted_iota(jnp.int32, (GLA_CHUNK, GLA_CHUNK), 1)
    causal = row >= col
    n_chunks = tt // GLA_CHUNK
    for c in range(n_chunks):
        rows = slice(c * GLA_CHUNK, (c + 1) * GLA_CHUNK)
        last = (c + 1) * GLA_CHUNK - 1
        for hd in range(GLA_HEADS):
            kc = slice(hd * GLA_DK, (hd + 1) * GLA_DK)
            vc = slice(hd * GLA_DV, (hd + 1) * GLA_DV)
            bb = b[rows, kc]
            b_last = b[last:last + 1, kc]
            q_dec = (q[rows, kc] * jnp.exp(bb)).astype(BF16)
            k_inv = (k[rows, kc] * jnp.exp(-bb)).astype(BF16)
            k_end = (k[rows, kc] * jnp.exp(b_last - bb)).astype(BF16)
            att = jnp.where(causal, _dot_nt(q_dec, k_inv), 0.0).astype(BF16)
            qd_scr[rows, kc] = q_dec
            o_scr[rows, vc] = jnp.dot(att, v[rows, vc], preferred_element_type=F32)
            dst_scr[c * GLA_HEADS + hd] = lax.dot_general(
                v[rows, vc], k_end, (((0,), (0,)), ((), ())), preferred_element_type=F32)
    states = [st_scr[hd] for hd in range(GLA_HEADS)]
    for c in range(n_chunks):
        rows = slice(c * GLA_CHUNK, (c + 1) * GLA_CHUNK)
        last = (c + 1) * GLA_CHUNK - 1
        for hd in range(GLA_HEADS):
            kc = slice(hd * GLA_DK, (hd + 1) * GLA_DK)
            vc = slice(hd * GLA_DV, (hd + 1) * GLA_DV)
            o_scr[rows, vc] += _dot_nt(qd_scr[rows, kc], states[hd].astype(BF16))
            states[hd] = states[hd] * jnp.exp(b[last:last + 1, kc]) + dst_scr[c * GLA_HEADS + hd]
    for hd in range(GLA_HEADS):
        st_scr[hd] = states[hd]

    @pl.when(j == pl.num_programs(1) - 1)
    def _():
        for hd in range(GLA_HEADS):
            st_ref[hd] = st_scr[hd].T

    go = jnp.dot(hb, wqkvg_ref[:, 2 * GLA_DK_TOT + GLA_DV_TOT:], preferred_element_type=F32)
    y = jnp.dot(_gla_out(o_scr[...], go, gng_ref[...]), wout_ref[...], preferred_element_type=F32)
    x1 = x + g1 * _rms(y, ng[1:2])
    x1_ref[...] = x1
    _ffn_prep(x1, ng, sh2, sc2, rw_ref, h_ref, lg_ref)


def _gla_mixer(x2d, routed, hp, mod3_prev, ng_prev, shared_prev, mod3, batch, seq, ng, wqkvg, wlr, wgk, bgk,
               gng, wout, rw_t, ffn_rows=None):
    tt = GLA_TILE
    tpb = seq // tt
    n = x2d.shape[0]
    idx = jnp.arange(GLA_CUM_BLOCK)
    tril = ((idx[:, None] >= idx[None, :]) &
            (idx[:, None] // GLA_CHUNK == idx[None, :] // GLA_CHUNK)).astype(BF16)
    row_map = lambda b, j: (b * tpb + j, 0)
    mod_map = lambda b, j: (b, 0, 0)
    consts = (ng, wqkvg, wlr, wgk, bgk, tril, gng, wout, rw_t)
    prev_consts = (ng_prev,) + tuple(shared_prev)
    ffn_shapes, alias_bufs, _ = _ffn_out(n, ffn_rows, None)
    args = [x2d, routed, hp, mod3_prev, *prev_consts, mod3, *consts]
    return pl.pallas_call(
        functools.partial(_gla_body, tt=tt, n_alias=len(alias_bufs)),
        grid=(batch, tpb),
        in_specs=[pl.BlockSpec((tt, D), row_map), pl.BlockSpec((tt, D), row_map),
                  pl.BlockSpec((tt, DP), row_map), pl.BlockSpec((None, 1, 6 * D), mod_map)] +
                 [_const_spec(a.shape) for a in prev_consts] +
                 [pl.BlockSpec((None, 1, 6 * D), mod_map)] +
                 [_const_spec(a.shape) for a in consts] +
                 [pl.BlockSpec(memory_space=pl.ANY)] * len(alias_bufs),
        out_specs=[pl.BlockSpec((tt, D), row_map), pl.BlockSpec((tt, DP), row_map),
                   pl.BlockSpec((N_EXPERTS, tt), lambda b, j: (0, b * tpb + j)),
                   pl.BlockSpec((None, GLA_HEADS, GLA_DK, GLA_DV), lambda b, j: (b, 0, 0, 0))],
        out_shape=[jax.ShapeDtypeStruct((n, D), F32)] + ffn_shapes +
                  [jax.ShapeDtypeStruct((batch, GLA_HEADS, GLA_DK, GLA_DV), F32)],
        input_output_aliases={len(args) + i: 1 + i for i in range(len(alias_bufs))},
        scratch_shapes=[pltpu.VMEM((GLA_HEADS, GLA_DV, GLA_DK), F32),
                        pltpu.VMEM((tt, GLA_DV_TOT), F32),
                        pltpu.VMEM((tt, GLA_DK_TOT), BF16),
                        pltpu.VMEM((tt // GLA_CHUNK * GLA_HEADS, GLA_DV, GLA_DK), F32)],
        compiler_params=_cparams("parallel", "arbitrary"),
        name="gla_mixer",
    )(*args, *alias_bufs)


def _gla1_proj_body(x_ref, mod_ref, ng_ref, wqkvg_ref, wlr_ref, wgk_ref, bgk_ref,
                    q_ref, k_ref, v_ref, go_ref, dec_ref):
    sh1, sc1, _, _, _, _ = _mod_slices(mod_ref)
    ng = ng_ref[...]
    hb = (_rms(x_ref[...], ng[0:1]) * (1.0 + sc1) + sh1).astype(BF16)
    proj = jnp.dot(hb, wqkvg_ref[...], preferred_element_type=F32)
    q_ref[...] = proj[:, :GLA_DK_TOT] * (GLA_DK ** -0.5)
    k_ref[...] = proj[:, GLA_DK_TOT:2 * GLA_DK_TOT]
    v_ref[...] = proj[:, 2 * GLA_DK_TOT:2 * GLA_DK_TOT + GLA_DV_TOT]
    go_ref[...] = proj[:, 2 * GLA_DK_TOT + GLA_DV_TOT:]
    dec_ref[...] = jnp.exp(_gla_gate(hb, wlr_ref, wgk_ref, bgk_ref))


GLA1_TOK = 8


def _gla1_state_body(st_ref, qc_ref, kc_ref, dc_ref, v_ref, nst_ref, o_ref):
    v = v_ref[...]
    for i in range(GLA1_TOK):
        for hd in range(GLA_HEADS):
            vrow = v[i:i + 1, hd * GLA_DV:(hd + 1) * GLA_DV]
            s_new = dc_ref[hd][:, i:i + 1] * st_ref[i, hd] + kc_ref[hd][:, i:i + 1] * vrow
            nst_ref[i, hd] = s_new
            o_ref[i:i + 1, hd * GLA_DV:(hd + 1) * GLA_DV] = jnp.sum(
                qc_ref[hd][:, i:i + 1] * s_new, axis=0, keepdims=True)


def _gla1_out_body(x_ref, o_ref, go_ref, mod_ref, ng_ref, gng_ref, wout_ref, rw_ref, *rest):
    x1_ref, h_ref, lg_ref = rest[-3:]
    _, _, g1, sh2, sc2, _ = _mod_slices(mod_ref)
    ng = ng_ref[...]
    y = jnp.dot(_gla_out(o_ref[...], go_ref[...], gng_ref[...]), wout_ref[...], preferred_element_type=F32)
    x1 = x_ref[...] + g1 * _rms(y, ng[1:2])
    x1_ref[...] = x1
    _ffn_prep(x1, ng, sh2, sc2, rw_ref, h_ref, lg_ref)


def _gla_mixer_one(x2d, mod2, state, ng, wqkvg, wlr, wgk, bgk, gng, wout, rw_t, into=None):
    n = x2d.shape[0]
    consts = (ng, wqkvg, wlr, wgk, bgk)
    q, k, v, go, dec = pl.pallas_call(
        _gla1_proj_body,
        in_specs=[_const_spec(a.shape) for a in (x2d, mod2) + consts],
        out_specs=[_const_spec((n, GLA_DK_TOT)), _const_spec((n, GLA_DK_TOT)), _const_spec((n, GLA_DV_TOT)),
                   _const_spec((n, GLA_DV_TOT)), _const_spec((n, GLA_DK_TOT))],
        out_shape=[jax.ShapeDtypeStruct((n, GLA_DK_TOT), F32), jax.ShapeDtypeStruct((n, GLA_DK_TOT), F32),
                   jax.ShapeDtypeStruct((n, GLA_DV_TOT), F32), jax.ShapeDtypeStruct((n, GLA_DV_TOT), F32),
                   jax.ShapeDtypeStruct((n, GLA_DK_TOT), F32)],
        grid=(1,),
        compiler_params=_cparams("arbitrary"),
        name="gla1_proj",
    )(x2d, mod2, *consts)

    def cols(a):
        return a.reshape(n // GLA1_TOK, GLA1_TOK, GLA_HEADS, GLA_DK).transpose(0, 2, 3, 1)

    col_spec = pl.BlockSpec((None, GLA_HEADS, GLA_DK, GLA1_TOK), lambda i: (i, 0, 0, 0))
    st_spec = pl.BlockSpec((GLA1_TOK, GLA_HEADS, GLA_DK, GLA_DV), lambda i: (i, 0, 0, 0))
    new_state, o = pl.pallas_call(
        _gla1_state_body,
        grid=(n // GLA1_TOK,),
        in_specs=[st_spec, col_spec, col_spec, col_spec, pl.BlockSpec((GLA1_TOK, GLA_DV_TOT), lambda i: (i, 0))],
        out_specs=[st_spec, pl.BlockSpec((GLA1_TOK, GLA_DV_TOT), lambda i: (i, 0))],
        out_shape=[jax.ShapeDtypeStruct(state.shape, F32), jax.ShapeDtypeStruct((n, GLA_DV_TOT), F32)],
        compiler_params=_cparams("parallel"),
        name="gla1_state",
    )(state, cols(q), cols(k), cols(dec), v)

    consts = (mod2, ng, gng, wout, rw_t)
    ffn_shapes, alias_bufs, oblk = _ffn_out(n, None, into)
    n_in = 3 + len(consts)
    x1, h, lg = pl.pallas_call(
        _gla1_out_body,
        grid=(1,),
        in_specs=[_const_spec(a.shape) for a in (x2d, o, go) + consts] +
                 [pl.BlockSpec(memory_space=pl.ANY)] * len(alias_bufs),
        out_specs=[_const_spec((n, D)), pl.BlockSpec((n, DP), lambda i: (oblk, 0)),
                   pl.BlockSpec((N_EXPERTS, n), lambda i: (0, oblk))],
        out_shape=[jax.ShapeDtypeStruct((n, D), F32)] + ffn_shapes,
        input_output_aliases={n_in + i: 1 + i for i in range(len(alias_bufs))},
        compiler_params=_cparams("arbitrary"),
        name="gla1_out",
    )(x2d, o, go, *consts, *alias_bufs)
    return x1, h, lg, new_state


def _moe_routed(h, lg, n, router_bias, wg, wu, wd):
    n_pad = h.shape[0]
    eid, rank, wts, counts = _router(lg, router_bias, n)
    tile_count = ((counts[:, 0] + EXPERT_TILE - 1) // EXPERT_TILE).astype(jnp.int32)
    tile_end = jnp.cumsum(tile_count).astype(jnp.int32)
    tile_first = tile_end - tile_count
    off = tile_first * EXPERT_TILE
    p_alloc = TOP_K * n_pad + N_EXPERTS * EXPERT_TILE
    dest = _dest(off, eid, rank, n, p_alloc - 1)
    dest_w = dest.reshape(TOP_K, n_pad // DISPATCH_W, DISPATCH_W).transpose(1, 0, 2)
    xs = _sc_dispatch(h, dest_w, p_alloc)
    ys = _experts(xs, tile_first, tile_count, tile_end[-1:], wg, wu, wd)
    dest_tm = dest.T.reshape(n_pad // SUM_W, SUM_PARTS, SUM_W * TOP_K // SUM_PARTS)
    w_lanes = jnp.repeat(wts.T, SC_LANES, axis=1)
    return _sc_gather_sum(ys, dest_tm, w_lanes)


def kernel(x_prompt, x_sample, state_gla, c_prompt, c_sample, norm_g, ada_w, ada_b, gm_w_in, gm_b_in,
           gm_ln_g, gm_ln_b, gm_w_s, gm_b_s, gm_w_out, gla_w_in, gla_w_gk, gla_b_gk, gla_norm_g,
           gla_w_out, router_w, router_bias, exp_w_gate, exp_w_up, exp_w_down, sh_w_gate, sh_w_up,
           sh_w_down):
    batch, seq, _ = x_prompt.shape
    n_s = x_sample.shape[0]
    n_p = batch * seq
    tpb = seq // MIX_TILE
    xp = x_prompt.reshape(n_p, D)
    xs = x_sample.reshape(n_s, D)

    mod = _ada(jnp.concatenate([c_prompt, c_sample], axis=0), ada_w, ada_b)
    mod_p = [mod[i, :batch].reshape(batch, 1, 6 * D) for i in range(2)]
    mod_s = [mod[i, batch:] for i in range(2)]
    rw_t = [jnp.concatenate(_split3(router_w[i].T), axis=0) for i in range(2)]

    ws_causal = jnp.tril(gm_w_s[0]).astype(BF16)
    bs_cols = gm_b_s[0].T
    eye = jnp.eye(GM_CHUNK, dtype=F32)
    ws_first = (gm_w_s[0][:, 0, 0][:, None, None] * eye).astype(BF16)
    bs_first = jnp.broadcast_to(gm_b_s[0][:, 0][None, :], (GM_CHUNK, GM_GROUPS))
    gm_args = (norm_g[0], gm_w_in[0].astype(BF16), gm_b_in[0].reshape(1, -1), gm_ln_g[0].reshape(1, -1),
               gm_ln_b[0].reshape(1, -1))
    wout0 = gm_w_out[0].astype(BF16)
    shared = [(sh_w_gate[i].astype(BF16), sh_w_up[i].astype(BF16), sh_w_down[i].astype(BF16))
              for i in range(2)]
    n_qkvg = 2 * GLA_DK_TOT + 2 * GLA_DV_TOT
    wqkvg = gla_w_in[0][:, :n_qkvg].astype(BF16)
    wlr = jnp.pad(gla_w_in[0][:, n_qkvg:], ((0, 0), (0, LANES - GLA_GATE_RANK))).astype(BF16)
    wgk = jnp.pad(gla_w_gk[0], ((0, LANES - GLA_GATE_RANK), (0, 0))).astype(BF16)
    gla_args = (norm_g[1], wqkvg, wlr, wgk, gla_b_gk[0].reshape(1, -1), gla_norm_g[0].reshape(1, -1),
                gla_w_out[0].astype(BF16), rw_t[1])
    experts_f32 = (exp_w_gate, exp_w_up, exp_w_down)

    half = batch // 2
    streams = [(0, half, False), (half, batch - half, True)]
    st = [dict() for _ in streams]

    experts = []
    for layer, (s, (b0, nb, with_new)) in enumerate(zip(st, streams)):
        s["mod_p"] = [mod_p[i][b0:b0 + nb] for i in range(2)]
        s["n"] = nb * seq
        s["n_all"] = s["n"] + (n_s if with_new else 0)
        s["n_pad"] = -(-s["n_all"] // TOKEN_PAD) * TOKEN_PAD
        s["x1p"], s["h"], s["lg"], *w16 = _gmlp_mixer(
            xp, b0 * tpb, s["n"], s["mod_p"][0], False, MIX_TILE, tpb, *gm_args, ws_causal, bs_cols, wout0,
            rw_t[0], emit_v=False, cast_w=(layer, *experts_f32), ffn_rows=s["n_pad"])
        experts.append(w16)
        if with_new:
            s["x1s"], s["h"], s["lg"], v_rows = _gmlp_mixer(
                xs, 0, n_s, mod_s[0], True, n_s, 1, *gm_args, ws_first, bs_first, wout0, rw_t[0], emit_v=True,
                into=(s["h"], s["lg"], s["n"] // n_s))
    for s, (b0, nb, with_new) in zip(st, streams):
        s["routed0"] = _moe_routed(s["h"], s["lg"], s["n_all"], router_bias[0], *experts[0])
        if with_new:
            s["x2s"] = _combine(s["x1s"], s["routed0"], s["h"], s["n"] // n_s, mod_s[0], True, n_s, 1,
                                norm_g[0], *shared[0])
    for s, (b0, nb, with_new) in zip(st, streams):
        s["x3p"], h1, lg1, s["st_p"] = _gla_mixer(s["x1p"], s["routed0"], s["h"], s["mod_p"][0], norm_g[0],
                                                  shared[0], s["mod_p"][1], nb, seq, *gla_args,
                                                  ffn_rows=s["n_pad"])
        if with_new:
            s["x3s"], h1, lg1, st_s = _gla_mixer_one(s["x2s"], mod_s[1], state_gla[:, 0], *gla_args,
                                                     into=(h1, lg1, s["n"] // n_s))
        s["h"], s["lg"] = h1, lg1
    y_prompt = None
    for s, (b0, nb, with_new) in zip(st, streams):
        routed = _moe_routed(s["h"], s["lg"], s["n_all"], router_bias[1], *experts[1])
        y_prompt = _combine(s["x3p"], routed, s["h"], 0, s["mod_p"][1], False, MIX_TILE, tpb, norm_g[1],
                            *shared[1], out_rows=n_p, out_blk0=b0 * tpb, out_buf=y_prompt)
        if with_new:
            y_new = _combine(s["x3s"], routed, s["h"], s["n"] // n_s, mod_s[1], True, n_s, 1, norm_g[1],
                             *shared[1])
    st_p = jnp.concatenate([s["st_p"] for s in st], axis=0)

    return (y_prompt.reshape(batch, seq, D), y_new.reshape(n_s, 1, D), st_p[:, None], st_s[:, None],
            v_rows.reshape(n_s, 1, 1, GM_HALF))
```

```python
import functools
import math

import jax
import jax.numpy as jnp
from jax import lax
from jax.experimental import pallas as pl
from jax.experimental.pallas import tpu as pltpu
from jax.experimental.pallas import tpu_sc as plsc

F32 = jnp.float32
BF16 = jnp.bfloat16

D = 1024
DP = D // 2
GM_CHUNK = 128
GM_HALF = 2 * D
GM_GROUPS = 8
GM_GROUP_DIM = GM_HALF // GM_GROUPS
GLA_HEADS = 4
GLA_DK = 128
GLA_DV = 256
GLA_DK_TOT = GLA_HEADS * GLA_DK
GLA_DV_TOT = GLA_HEADS * GLA_DV
GLA_GATE_RANK = 16
GLA_GATE_NORMALIZER = 16.0
GLA_CHUNK = 64
N_EXPERTS = 64
TOP_K = 8
N_EXPERT_GROUPS = 8
GROUP_SIZE = N_EXPERTS // N_EXPERT_GROUPS
TOPK_GROUPS = 4
EXPERT_DIM = D // 4
ROUTED_SCALE = 2.5
NORM_EPS = 1e-6
LN_EPS = 1e-5

LANES = 128
VMEM_LIMIT = 56 * 1024 * 1024

MIX_TILE = 256
GLA_TILE = 512
GLA_CUM_BLOCK = 256
GM_COL_BLOCK = 512
ROUTER_TILE = 1024
EXPERT_TILE = 544
EXPERT_X_SLOTS = 6
EXPERT_AHEAD = EXPERT_X_SLOTS - 2
EXPERT_Y_SLOTS = 4
SC_WORKERS = 32
DISPATCH_W = 32
SC_LANES = 16
SUM_W = 16
SUM_PARTS = 4
SUM_TOKENS = 2
SUM_UNROLL = 4
TOKEN_PAD = SC_WORKERS * DISPATCH_W


def _cparams(*sem):
    return pltpu.CompilerParams(dimension_semantics=sem, vmem_limit_bytes=VMEM_LIMIT)


def _rms(x, g):
    return x * lax.rsqrt(jnp.mean(x * x, axis=-1, keepdims=True) + NORM_EPS) * g


def _silu(x):
    return x * (1.0 / (1.0 + jnp.exp(-x)))


def _gelu(x):
    return 0.5 * x * (1.0 + lax.erf(x * (1.0 / math.sqrt(2.0))))


def _bdot(a, b):
    return jnp.dot(a.astype(BF16), b.astype(BF16), preferred_element_type=F32)


def _dot_nt(a, b, precision=None):
    return lax.dot_general(a, b, (((1,), (1,)), ((), ())), preferred_element_type=F32,
                           precision=precision)


def _mod_slices(mod_ref):
    return [mod_ref[:, i * D:(i + 1) * D] for i in range(6)]


HI_HALF = -65536


def _pack_rows(x):
    lo = lax.bitcast_convert_type(x[:, :DP].astype(BF16).astype(F32), jnp.int32)
    hi = lax.bitcast_convert_type(x[:, DP:].astype(BF16).astype(F32), jnp.int32)
    return lax.shift_right_logical(lo, 16) | (hi & HI_HALF)


def _unpack_rows(p):
    lo = lax.bitcast_convert_type(lax.shift_left(p, 16), F32)
    hi = lax.bitcast_convert_type(p & HI_HALF, F32)
    return lo, hi


def _ffn_prep(x1, ng, sh2, sc2, rw_ref, h_ref, lg_ref, rows=slice(None)):
    hffn = _rms(x1, ng[2:3]) * (1.0 + sc2) + sh2
    h_ref[rows, :] = _pack_rows(hffn)
    lg3 = _dot_nt(rw_ref[...], hffn.astype(BF16))
    lg_ref[:, rows] = lg3[:N_EXPERTS] + lg3[N_EXPERTS:2 * N_EXPERTS] + lg3[2 * N_EXPERTS:]


def _ada_body(c_ref, w_ref, b_ref, o_ref):
    c = c_ref[...]
    o_ref[...] = _bdot(_silu(c), w_ref[...]) + b_ref[...]


def _ada(c, ada_w, ada_b):
    n = c.shape[0]
    depth = ada_w.shape[0]
    tn = 1536
    return pl.pallas_call(
        _ada_body,
        grid=(depth, 6 * D // tn),
        in_specs=[pl.BlockSpec((n, D), lambda l, j: (0, 0)),
                  pl.BlockSpec((None, D, tn), lambda l, j: (l, 0, j)),
                  pl.BlockSpec((None, 1, tn), lambda l, j: (l, 0, j))],
        out_specs=pl.BlockSpec((None, n, tn), lambda l, j: (l, 0, j)),
        out_shape=jax.ShapeDtypeStruct((depth, n, 6 * D), F32),
        compiler_params=_cparams("parallel", "parallel"),
        name="ada_mod",
    )(c, ada_w, ada_b.reshape(depth, 1, 6 * D))


def _mod_spec(per_row, tt, tiles_per_batch):
    if per_row:
        return pl.BlockSpec((tt, 6 * D), lambda i: (i, 0))
    return pl.BlockSpec((None, 1, 6 * D), lambda i: (i // tiles_per_batch, 0, 0))


def _const_spec(shape):
    zeros = (0,) * len(shape)
    return pl.BlockSpec(shape, lambda *_: zeros)


def _gmlp_body(x_ref, mod_ref, ng_ref, win_ref, bin_ref, lng_ref, lnb_ref, ws_ref, bs_ref, wout_ref,
               rw_ref, *rest, n_chunks, emit_v, cast_w, n_alias):
    rest = list(rest)
    w32_refs = [rest.pop(0) for _ in range(3)] if cast_w else []
    rest = rest[n_alias:]
    x1_ref, h_ref, lg_ref = rest[:3]
    rest = rest[3:]
    v_ref = rest.pop(0) if emit_v else None
    w16_refs = [rest.pop(0) for _ in range(3)] if cast_w else []
    um_ref, z_ref = rest
    for src, dst in zip(w32_refs, w16_refs):
        dst[...] = src[...].astype(BF16)
    sh1, sc1, g1, sh2, sc2, _ = _mod_slices(mod_ref)
    ng = ng_ref[...]
    x = x_ref[...]
    hb = (_rms(x, ng[0:1]) * (1.0 + sc1) + sh1).astype(BF16)
    for cb in range(2 * GM_HALF // GM_COL_BLOCK):
        cols = slice(cb * GM_COL_BLOCK, (cb + 1) * GM_COL_BLOCK)
        z_ref[:, cols] = _gelu(jnp.dot(hb, win_ref[:, cols], preferred_element_type=F32) + bin_ref[:, cols])
    u = z_ref[:, :GM_HALF]
    v = z_ref[:, GM_HALF:]
    mu = jnp.mean(v, axis=-1, keepdims=True)
    vc = v - mu
    var = jnp.mean(vc * vc, axis=-1, keepdims=True)
    v = vc * lax.rsqrt(var + LN_EPS) * lng_ref[...] + lnb_ref[...]
    if emit_v:
        v_ref[...] = v
    vb = v.astype(BF16)
    for c in range(n_chunks):
        rows = slice(c * GM_CHUNK, (c + 1) * GM_CHUNK)
        for g in range(GM_GROUPS):
            cols = slice(g * GM_GROUP_DIM, (g + 1) * GM_GROUP_DIM)
            mixed = jnp.dot(ws_ref[g], vb[rows, cols], preferred_element_type=F32) + bs_ref[:, g:g + 1]
            um_ref[rows, cols] = (u[rows, cols] * mixed).astype(BF16)
    y = jnp.dot(um_ref[...], wout_ref[...], preferred_element_type=F32)
    x1 = x + g1 * _rms(y, ng[1:2])
    x1_ref[...] = x1
    _ffn_prep(x1, ng, sh2, sc2, rw_ref, h_ref, lg_ref)


def _ffn_out(n, ffn_rows, into):
    rows = ffn_rows or n
    oblk = 0
    bufs = []
    if into is not None:
        *bufs, oblk = into
        rows = bufs[0].shape[0]
    elif rows != n:
        bufs = [jnp.zeros((rows, DP), jnp.int32), jnp.zeros((N_EXPERTS, rows), F32)]
    shapes = [jax.ShapeDtypeStruct((rows, DP), jnp.int32), jax.ShapeDtypeStruct((N_EXPERTS, rows), F32)]
    return shapes, bufs, oblk


def _gmlp_mixer(x2d, blk0, n, mod, per_row, tt, tiles_per_batch, ng, win, b_in, ln_g, ln_b, ws, bs, wout,
                rw_t, emit_v, cast_w=None, ffn_rows=None, into=None):
    steps = n // tt
    ffn_shapes, alias_bufs, oblk = _ffn_out(n, ffn_rows, into)
    out_shape = [jax.ShapeDtypeStruct((n, D), F32)] + ffn_shapes
    out_specs = [pl.BlockSpec((tt, D), lambda i: (i, 0)), pl.BlockSpec((tt, DP), lambda i: (i + oblk, 0)),
                 pl.BlockSpec((N_EXPERTS, tt), lambda i: (0, i + oblk))]
    if emit_v:
        out_shape.append(jax.ShapeDtypeStruct((n, GM_HALF), F32))
        out_specs.append(pl.BlockSpec((tt, GM_HALF), lambda i: (i, 0)))

    def one_buffer(a):
        zeros = (0,) * a.ndim
        return pl.BlockSpec(a.shape, lambda *_: zeros, pipeline_mode=pl.Buffered(1))

    consts = (ng, win, b_in, ln_g, ln_b, ws, bs, wout, rw_t)
    in_specs = [pl.BlockSpec((tt, D), lambda i: (i + blk0, 0)), _mod_spec(per_row, tt, tiles_per_batch)]
    in_specs += [one_buffer(a) for a in consts]
    args = [x2d, mod, *consts]
    if cast_w is not None:
        layer, *w_all = cast_w
        per_step = N_EXPERTS // steps
        for w in w_all:
            blk = (None, per_step) + w.shape[2:]
            in_specs.append(pl.BlockSpec(blk, lambda i: (layer, i, 0, 0)))
            out_specs.append(pl.BlockSpec(blk[1:], lambda i: (i, 0, 0)))
            out_shape.append(jax.ShapeDtypeStruct(w.shape[1:], BF16))
            args.append(w)
    aliases = {len(args) + i: 1 + i for i in range(len(alias_bufs))}
    in_specs += [pl.BlockSpec(memory_space=pl.ANY)] * len(alias_bufs)
    args += alias_bufs
    return pl.pallas_call(
        functools.partial(_gmlp_body, n_chunks=tt // GM_CHUNK, emit_v=emit_v, cast_w=cast_w is not None,
                          n_alias=len(alias_bufs)),
        grid=(steps,),
        in_specs=in_specs,
        out_specs=out_specs,
        out_shape=out_shape,
        input_output_aliases=aliases,
        scratch_shapes=[pltpu.VMEM((tt, GM_HALF), BF16), pltpu.VMEM((tt, 2 * GM_HALF), F32)],
        compiler_params=_cparams("parallel"),
        name="gmlp_mixer_rows" if per_row else "gmlp_mixer",
    )(*args)


def _combine_body(x_ref, y_ref, h_ref, mod_ref, ng_ref, swg_ref, swu_ref, swd_ref, *rest):
    o_ref = rest[-1]
    o_ref[...] = _channel_mix_residual(x_ref[...], y_ref[...], h_ref[...], mod_ref[:, 5 * D:6 * D],
                                       ng_ref[3:4, :], swg_ref, swu_ref, swd_ref)


def _combine(x2d, routed, hp, blk0, mod, per_row, tt, tiles_per_batch, ng, swg, swu, swd,
             out_rows=None, out_blk0=0, out_buf=None):
    n = x2d.shape[0]
    in_specs = [pl.BlockSpec((tt, D), lambda i: (i, 0)),
                pl.BlockSpec((tt, D), lambda i: (i + blk0, 0)),
                pl.BlockSpec((tt, DP), lambda i: (i + blk0, 0)),
                _mod_spec(per_row, tt, tiles_per_batch),
                _const_spec(ng.shape), _const_spec(swg.shape), _const_spec(swu.shape),
                _const_spec(swd.shape)]
    args = [x2d, routed, hp, mod, ng, swg, swu, swd]
    aliases = {}
    if out_buf is not None:
        in_specs.append(pl.BlockSpec(memory_space=pl.ANY))
        aliases = {len(args): 0}
        args.append(out_buf)
    return pl.pallas_call(
        _combine_body,
        grid=(n // tt,),
        in_specs=in_specs,
        out_specs=pl.BlockSpec((tt, D), lambda i: (i + out_blk0, 0)),
        out_shape=jax.ShapeDtypeStruct((out_rows or n, D), F32),
        input_output_aliases=aliases,
        compiler_params=_cparams("parallel"),
        name="combine_rows" if per_row else "combine",
    )(*args)


def _router_body(lg_ref, bias_ref, tri_ref, eid_ref, rank_ref, wts_ref, cnt_ref, carry_ref, *, n_real):
    step = pl.program_id(0)

    @pl.when(step == 0)
    def _():
        carry_ref[...] = jnp.zeros_like(carry_ref)

    lg = lg_ref[...]
    tn = lg.shape[1]
    real = (step * tn + lax.broadcasted_iota(jnp.int32, (1, tn), 1)) < n_real
    lg = jnp.where(real, lg, 0.0)
    scores = 1.0 / (1.0 + jnp.exp(-lg))
    sel = scores + bias_ref[...]
    neg = -jnp.inf
    sub8 = lax.broadcasted_iota(jnp.int32, (GROUP_SIZE, tn), 0)
    gsub = lax.broadcasted_iota(jnp.int32, (N_EXPERT_GROUPS, tn), 0)
    gs = jnp.zeros((N_EXPERT_GROUPS, tn), F32)
    for g in range(N_EXPERT_GROUPS):
        blk = sel[g * GROUP_SIZE:(g + 1) * GROUP_SIZE, :]
        m1 = jnp.max(blk, axis=0, keepdims=True)
        i1 = jnp.min(jnp.where(blk == m1, sub8, GROUP_SIZE), axis=0, keepdims=True)
        m2 = jnp.max(jnp.where(sub8 == i1, neg, blk), axis=0, keepdims=True)
        gs = jnp.where(gsub == g, m1 + m2, gs)
    gmask = jnp.zeros((N_EXPERT_GROUPS, tn), jnp.bool_)
    for _ in range(TOPK_GROUPS):
        m = jnp.max(gs, axis=0, keepdims=True)
        i = jnp.min(jnp.where(gs == m, gsub, N_EXPERT_GROUPS), axis=0, keepdims=True)
        hit = gsub == i
        gmask = jnp.logical_or(gmask, hit)
        gs = jnp.where(hit, neg, gs)
    gmaskf = gmask.astype(F32)
    blocks = []
    for g in range(N_EXPERT_GROUPS):
        keep = jnp.broadcast_to(gmaskf[g:g + 1, :], (GROUP_SIZE, tn)) > 0.5
        blocks.append(jnp.where(keep, sel[g * GROUP_SIZE:(g + 1) * GROUP_SIZE, :], neg))
    msel = jnp.concatenate(blocks, axis=0)
    esub = lax.broadcasted_iota(jnp.int32, (N_EXPERTS, tn), 0)
    chosen = jnp.zeros((N_EXPERTS, tn), jnp.bool_)
    picks = []
    for _ in range(TOP_K):
        m = jnp.max(msel, axis=0, keepdims=True)
        i = jnp.min(jnp.where(msel == m, esub, N_EXPERTS), axis=0, keepdims=True)
        hit = esub == i
        picks.append(i)
        chosen = jnp.logical_or(chosen, hit)
        msel = jnp.where(hit, neg, msel)
    w = jnp.where(chosen, scores, 0.0)
    w = w / jnp.sum(w, axis=0, keepdims=True) * ROUTED_SCALE
    counted = jnp.where(jnp.logical_and(chosen, real), 1.0, 0.0)
    incl = jnp.dot(counted.astype(BF16), tri_ref[...], preferred_element_type=F32)
    rank_full = carry_ref[:, 0:1] + incl - 1.0
    ksub = lax.broadcasted_iota(jnp.int32, (TOP_K, tn), 0)
    eid = jnp.zeros((TOP_K, tn), jnp.int32)
    rank = jnp.zeros((TOP_K, tn), F32)
    wts = jnp.zeros((TOP_K, tn), F32)
    for k in range(TOP_K):
        hit = esub == picks[k]
        eid = jnp.where(ksub == k, picks[k], eid)
        rank = jnp.where(ksub == k, jnp.sum(jnp.where(hit, rank_full, 0.0), axis=0, keepdims=True), rank)
        wts = jnp.where(ksub == k, jnp.sum(jnp.where(hit, w, 0.0), axis=0, keepdims=True), wts)
    eid_ref[...] = eid
    rank_ref[...] = rank.astype(jnp.int32)
    wts_ref[...] = wts
    carry = carry_ref[...] + incl[:, tn - 1:tn]
    carry_ref[...] = carry
    cnt_ref[...] = carry.astype(jnp.int32)


def _router(lg_t, bias, n_real):
    n = lg_t.shape[1]
    tn = ROUTER_TILE
    idx = jnp.arange(tn)
    tri = (idx[:, None] <= idx[None, :]).astype(BF16)
    kspec = pl.BlockSpec((TOP_K, tn), lambda i: (0, i))
    return pl.pallas_call(
        functools.partial(_router_body, n_real=n_real),
        grid=(n // tn,),
        in_specs=[pl.BlockSpec((N_EXPERTS, tn), lambda i: (0, i)), _const_spec((N_EXPERTS, 1)),
                  _const_spec((tn, tn))],
        out_specs=[kspec, kspec, kspec, _const_spec((N_EXPERTS, LANES))],
        out_shape=[jax.ShapeDtypeStruct((TOP_K, n), jnp.int32), jax.ShapeDtypeStruct((TOP_K, n), jnp.int32),
                   jax.ShapeDtypeStruct((TOP_K, n), F32), jax.ShapeDtypeStruct((N_EXPERTS, LANES), jnp.int32)],
        scratch_shapes=[pltpu.VMEM((N_EXPERTS, LANES), F32)],
        compiler_params=_cparams("arbitrary"),
        name="router",
    )(lg_t, bias.reshape(N_EXPERTS, 1), tri)


def _dest_body(off_ref, eid_ref, rank_ref, dest_ref, *, n_real, last_row):
    eid = eid_ref[...]
    base = jnp.zeros(eid.shape, jnp.int32)
    for e in range(N_EXPERTS):
        base = jnp.where(eid == e, off_ref[e], base)
    tok = lax.broadcasted_iota(jnp.int32, eid.shape, 1)
    slot = lax.broadcasted_iota(jnp.int32, eid.shape, 0)
    unused = last_row - ((tok - n_real) * TOP_K + slot)
    dest_ref[...] = jnp.where(tok < n_real, base + rank_ref[...], unused)


def _dest(off, eid, rank, n_real, last_row):
    spec = pl.BlockSpec(eid.shape, lambda i, off_ref: (0, 0))
    return pl.pallas_call(
        functools.partial(_dest_body, n_real=n_real, last_row=last_row),
        grid_spec=pltpu.PrefetchScalarGridSpec(num_scalar_prefetch=1, grid=(1,), in_specs=[spec, spec],
                                               out_specs=spec),
        out_shape=jax.ShapeDtypeStruct(eid.shape, jnp.int32),
        compiler_params=_cparams("arbitrary"),
        name="dest_rows",
    )(off, eid, rank)


def _sc_mesh():
    return plsc.VectorSubcoreMesh(core_axis_name="core", subcore_axis_name="subcore")


def _sc_dispatch(hp, dest_w, p_alloc):
    n = hp.shape[0]
    w = dest_w.shape[2]

    @functools.partial(pl.kernel, out_type=jax.ShapeDtypeStruct((p_alloc, DP), jnp.int32), mesh=_sc_mesh(),
                       name="sc_dispatch")
    def run(hp_hbm, dest_hbm, xs_hbm):
        def body(x_vmem, i_vmem):
            for k in range(TOP_K):
                pltpu.sync_copy(x_vmem, xs_hbm.at[i_vmem.at[k]])

        pltpu.emit_pipeline(
            body,
            grid=(n // w,),
            in_specs=[pl.BlockSpec((w, DP), lambda i: (i, 0)),
                      pl.BlockSpec((None, TOP_K, w), lambda i: (i, 0, 0))],
            out_specs=[],
            core_axis_name=("core", "subcore"),
            dimension_semantics=(pltpu.PARALLEL,),
        )(hp_hbm, dest_hbm)

    return run(hp, dest_w)


def _sc_gather_sum(ys, dest_tm, w_lanes):
    n_win, parts, pk = dest_tm.shape
    w = parts * pk // TOP_K
    wp = w // parts
    n_vec = DP // SC_LANES

    @functools.partial(pl.kernel, out_type=jax.ShapeDtypeStruct((n_win * w, D), F32), mesh=_sc_mesh(),
                       scratch_types=[pltpu.VMEM((parts, pk, DP), jnp.int32), pltpu.SemaphoreType.DMA((parts,))],
                       compiler_params=pltpu.CompilerParams(needs_layout_passes=False), name="sc_gather_sum")
    def run(ys_hbm, dest_hbm, w_hbm, o_hbm, rows_v, sems):
        def body(i_vmem, w_vmem, o_vmem):
            copies = [pltpu.async_copy(ys_hbm.at[i_vmem.at[p]], rows_v.at[p], sems.at[p]) for p in range(parts)]
            for p in range(parts):
                copies[p].wait()

                for t0 in range(0, wp, SUM_TOKENS):
                    toks = [p * wp + t0 + u for u in range(SUM_TOKENS)]
                    wv = [[w_vmem[tok, pl.ds(k * SC_LANES, SC_LANES)] for k in range(TOP_K)] for tok in toks]

                    @plsc.parallel_loop(0, n_vec, unroll=SUM_UNROLL)
                    def _(j):
                        col = j * SC_LANES
                        for u, tok in enumerate(toks):
                            lo = jnp.zeros((SC_LANES,), F32)
                            hi = jnp.zeros((SC_LANES,), F32)
                            for k in range(TOP_K):
                                word = rows_v[p, (t0 + u) * TOP_K + k, pl.ds(col, SC_LANES)]
                                lo = lo + wv[u][k] * plsc.bitcast(lax.shift_left(word, 16), F32)
                                hi = hi + wv[u][k] * plsc.bitcast(word & HI_HALF, F32)
                            o_vmem[tok, pl.ds(col, SC_LANES)] = lo
                            o_vmem[tok, pl.ds(DP + col, SC_LANES)] = hi

        pltpu.emit_pipeline(
            body,
            grid=(n_win,),
            in_specs=[pl.BlockSpec((None, parts, pk), lambda i: (i, 0, 0)),
                      pl.BlockSpec((w, TOP_K * SC_LANES), lambda i: (i, 0))],
            out_specs=[pl.BlockSpec((w, D), lambda i: (i, 0))],
            core_axis_name=("core", "subcore"),
            dimension_semantics=(pltpu.PARALLEL,),
        )(dest_hbm, w_hbm, o_hbm)

    return run(ys, dest_tm, w_lanes)


def _expert_body(first_ref, cnt_ref, nused_ref, xs_hbm, wg_s, wu_s, wd_s, ys_hbm, xbuf, ybuf, xsem, ysem):
    e = pl.program_id(0)
    n_used = nused_ref[0]

    def load(g):
        rows = pl.ds(pl.multiple_of(g * EXPERT_TILE, EXPERT_TILE), EXPERT_TILE)
        slot = g % EXPERT_X_SLOTS
        return pltpu.make_async_copy(xs_hbm.at[rows], xbuf.at[slot], xsem.at[slot])

    def store(g):
        rows = pl.ds(pl.multiple_of(g * EXPERT_TILE, EXPERT_TILE), EXPERT_TILE)
        slot = g % EXPERT_Y_SLOTS
        return pltpu.make_async_copy(ybuf.at[slot], ys_hbm.at[rows], ysem.at[slot])

    @pl.when(e == 0)
    def _():
        for g in range(EXPERT_AHEAD):
            @pl.when(g < n_used)
            def _():
                load(g).start()

    first = first_ref[e]
    cnt = cnt_ref[e]

    def acquire(g):
        ahead = g + EXPERT_AHEAD

        @pl.when(ahead < n_used)
        def _():
            load(ahead).start()

        load(g).wait()

        @pl.when(g >= EXPERT_Y_SLOTS)
        def _():
            store(g - EXPERT_Y_SLOTS).wait()

    def compute(g):
        lo, hi = _unpack_rows(xbuf[g % EXPERT_X_SLOTS])
        lo = lo.astype(BF16)
        hi = hi.astype(BF16)

        def xdot(w_s):
            return (jnp.dot(lo, w_s[:DP, :], preferred_element_type=F32) +
                    jnp.dot(hi, w_s[DP:, :], preferred_element_type=F32))

        a = (_silu(xdot(wg_s)) * xdot(wu_s)).astype(BF16)
        ybuf[g % EXPERT_Y_SLOTS] = _pack_rows(jnp.dot(a, wd_s[...], preferred_element_type=F32))

    def pair(j, carry):
        g = first + 2 * j
        acquire(g)
        acquire(g + 1)
        compute(g)
        compute(g + 1)
        store(g).start()
        store(g + 1).start()
        return carry

    lax.fori_loop(0, cnt // 2, pair, 0)

    @pl.when(cnt % 2 == 1)
    def _():
        g = first + cnt - 1
        acquire(g)
        compute(g)
        store(g).start()

    @pl.when(e == N_EXPERTS - 1)
    def _():
        for k in range(EXPERT_Y_SLOTS):
            g = n_used - 1 - k

            @pl.when(g >= 0)
            def _():
                store(g).wait()


def _experts(xs, tile_first, tile_count, n_used, wg, wu, wd):
    def w_map(e, first, cnt, nu):
        return (e, 0, 0)

    return pl.pallas_call(
        _expert_body,
        grid_spec=pltpu.PrefetchScalarGridSpec(
            num_scalar_prefetch=3, grid=(N_EXPERTS,),
            in_specs=[pl.BlockSpec(memory_space=pl.ANY),
                      pl.BlockSpec((None, D, EXPERT_DIM), w_map),
                      pl.BlockSpec((None, D, EXPERT_DIM), w_map),
                      pl.BlockSpec((None, EXPERT_DIM, D), w_map)],
            out_specs=pl.BlockSpec(memory_space=pl.ANY),
            scratch_shapes=[pltpu.VMEM((EXPERT_X_SLOTS, EXPERT_TILE, DP), jnp.int32),
                            pltpu.VMEM((EXPERT_Y_SLOTS, EXPERT_TILE, DP), jnp.int32),
                            pltpu.SemaphoreType.DMA((EXPERT_X_SLOTS,)),
                            pltpu.SemaphoreType.DMA((EXPERT_Y_SLOTS,))]),
        out_shape=jax.ShapeDtypeStruct(xs.shape, jnp.int32),
        compiler_params=_cparams("arbitrary"),
        name="experts",
    )(tile_first, tile_count, n_used, xs, wg, wu, wd)


def _log_sigmoid(z):
    return jnp.minimum(z, 0.0) - jnp.log(1.0 + jnp.exp(-jnp.abs(z)))


def _gla_gate(hb, wlr_ref, wgk_ref, bgk_ref):
    lr = jnp.dot(hb, wlr_ref[...], preferred_element_type=F32)
    z = _bdot(lr, wgk_ref[...]) + bgk_ref[...]
    return _log_sigmoid(z) * (1.0 / GLA_GATE_NORMALIZER)


def _split3(a):
    hi = a.astype(BF16)
    r1 = a - hi.astype(F32)
    mid = r1.astype(BF16)
    lo = (r1 - mid.astype(F32)).astype(BF16)
    return hi, mid, lo


def _gla_out(o_ref_val, go, gng):
    parts = []
    for hd in range(GLA_HEADS):
        cols = slice(hd * GLA_DV, (hd + 1) * GLA_DV)
        parts.append((_rms(o_ref_val[:, cols], gng) * _silu(go[:, cols])).astype(BF16))
    return jnp.concatenate(parts, axis=1)


def _channel_mix_residual(x, routed, h_packed, g2, ng3, swg_ref, swu_ref, swd_ref):
    h_lo, h_hi = _unpack_rows(h_packed)
    h_lo = h_lo.astype(BF16)
    h_hi = h_hi.astype(BF16)

    def hdot(w_ref_):
        return (jnp.dot(h_lo, w_ref_[:DP, :], preferred_element_type=F32) +
                jnp.dot(h_hi, w_ref_[DP:, :], preferred_element_type=F32))

    hs = (_silu(hdot(swg_ref)) * hdot(swu_ref)).astype(BF16)
    y = jnp.dot(hs, swd_ref[...], preferred_element_type=F32) + routed
    return x + g2 * _rms(y, ng3)


def _gla_body(x_ref, y_ref, hprev_ref, modprev_ref, ngprev_ref, swg_ref, swu_ref, swd_ref,
              mod_ref, ng_ref, wqkvg_ref, wlr_ref, wgk_ref, bgk_ref, tril_ref, gng_ref, wout_ref,
              rw_ref, *rest, tt, n_alias):
    x1_ref, h_ref, lg_ref, st_ref, st_scr, o_scr, qd_scr, dst_scr = rest[n_alias:]
    j = pl.program_id(1)

    @pl.when(j == 0)
    def _():
        st_scr[...] = jnp.zeros_like(st_scr)

    x = _channel_mix_residual(x_ref[...], y_ref[...], hprev_ref[...], modprev_ref[:, 5 * D:6 * D],
                              ngprev_ref[3:4, :], swg_ref, swu_ref, swd_ref)
    sh1, sc1, g1, sh2, sc2, _ = _mod_slices(mod_ref)
    ng = ng_ref[...]
    hb = (_rms(x, ng[0:1]) * (1.0 + sc1) + sh1).astype(BF16)
    q = jnp.dot(hb, wqkvg_ref[:, :GLA_DK_TOT], preferred_element_type=F32) * (GLA_DK ** -0.5)
    k = jnp.dot(hb, wqkvg_ref[:, GLA_DK_TOT:2 * GLA_DK_TOT], preferred_element_type=F32)
    v = jnp.dot(hb, wqkvg_ref[:, 2 * GLA_DK_TOT:2 * GLA_DK_TOT + GLA_DV_TOT],
                preferred_element_type=F32).astype(BF16)
    log_a = _gla_gate(hb, wlr_ref, wgk_ref, bgk_ref)
    tril = tril_ref[...]
    parts = _split3(log_a)
    b = jnp.concatenate(
        [sum(jnp.dot(tril, p[r:r + GLA_CUM_BLOCK], preferred_element_type=F32) for p in parts)
         for r in range(0, tt, GLA_CUM_BLOCK)], axis=0)
    row = lax.broadcasted_iota(jnp.int32, (GLA_CHUNK, GLA_CHUNK), 0)
    col = lax.broadcasted_iota(jnp.int32, (GLA_CHUNK, GLA_CHUNK), 1)
    causal = row >= col
    n_chunks = tt // GLA_CHUNK
    for c in range(n_chunks):
        rows = slice(c * GLA_CHUNK, (c + 1) * GLA_CHUNK)
        last = (c + 1) * GLA_CHUNK - 1
        for hd in range(GLA_HEADS):
            kc = slice(hd * GLA_DK, (hd + 1) * GLA_DK)
            vc = slice(hd * GLA_DV, (hd + 1) * GLA_DV)
            bb = b[rows, kc]
            b_last = b[last:last + 1, kc]
            q_dec = (q[rows, kc] * jnp.exp(bb)).astype(BF16)
            k_inv = (k[rows, kc] * jnp.exp(-bb)).astype(BF16)
            k_end = (k[rows, kc] * jnp.exp(b_last - bb)).astype(BF16)
            att = jnp.where(causal, _dot_nt(q_dec, k_inv), 0.0).astype(BF16)
            qd_scr[rows, kc] = q_dec
            o_scr[rows, vc] = jnp.dot(att, v[rows, vc], preferred_element_type=F32)
            dst_scr[c * GLA_HEADS + hd] = lax.dot_general(
                v[rows, vc], k_end, (((0,), (0,)), ((), ())), preferred_element_type=F32)
    states = [st_scr[hd] for hd in range(GLA_HEADS)]
    for c in range(n_chunks):
        rows = slice(c * GLA_CHUNK, (c + 1) * GLA_CHUNK)
        last = (c + 1) * GLA_CHUNK - 1
        for hd in range(GLA_HEADS):
            kc = slice(hd * GLA_DK, (hd + 1) * GLA_DK)
            vc = slice(hd * GLA_DV, (hd + 1) * GLA_DV)
            o_scr[rows, vc] += _dot_nt(qd_scr[rows, kc], states[hd].astype(BF16))
            states[hd] = states[hd] * jnp.exp(b[last:last + 1, kc]) + dst_scr[c * GLA_HEADS + hd]
    for hd in range(GLA_HEADS):
        st_scr[hd] = states[hd]

    @pl.when(j == pl.num_programs(1) - 1)
    def _():
        for hd in range(GLA_HEADS):
            st_ref[hd] = st_scr[hd].T

    go = jnp.dot(hb, wqkvg_ref[:, 2 * GLA_DK_TOT + GLA_DV_TOT:], preferred_element_type=F32)
    y = jnp.dot(_gla_out(o_scr[...], go, gng_ref[...]), wout_ref[...], preferred_element_type=F32)
    x1 = x + g1 * _rms(y, ng[1:2])
    x1_ref[...] = x1
    _ffn_prep(x1, ng, sh2, sc2, rw_ref, h_ref, lg_ref)


def _gla_mixer(x2d, routed, hp, mod3_prev, ng_prev, shared_prev, mod3, batch, seq, ng, wqkvg, wlr, wgk, bgk,
               gng, wout, rw_t, ffn_rows=None):
    tt = GLA_TILE
    tpb = seq // tt
    n = x2d.shape[0]
    idx = jnp.arange(GLA_CUM_BLOCK)
    tril = ((idx[:, None] >= idx[None, :]) &
            (idx[:, None] // GLA_CHUNK == idx[None, :] // GLA_CHUNK)).astype(BF16)
    row_map = lambda b, j: (b * tpb + j, 0)
    mod_map = lambda b, j: (b, 0, 0)
    consts = (ng, wqkvg, wlr, wgk, bgk, tril, gng, wout, rw_t)
    prev_consts = (ng_prev,) + tuple(shared_prev)
    ffn_shapes, alias_bufs, _ = _ffn_out(n, ffn_rows, None)
    args = [x2d, routed, hp, mod3_prev, *prev_consts, mod3, *consts]
    return pl.pallas_call(
        functools.partial(_gla_body, tt=tt, n_alias=len(alias_bufs)),
        grid=(batch, tpb),
        in_specs=[pl.BlockSpec((tt, D), row_map), pl.BlockSpec((tt, D), row_map),
                  pl.BlockSpec((tt, DP), row_map), pl.BlockSpec((None, 1, 6 * D), mod_map)] +
                 [_const_spec(a.shape) for a in prev_consts] +
                 [pl.BlockSpec((None, 1, 6 * D), mod_map)] +
                 [_const_spec(a.shape) for a in consts] +
                 [pl.BlockSpec(memory_space=pl.ANY)] * len(alias_bufs),
        out_specs=[pl.BlockSpec((tt, D), row_map), pl.BlockSpec((tt, DP), row_map),
                   pl.BlockSpec((N_EXPERTS, tt), lambda b, j: (0, b * tpb + j)),
                   pl.BlockSpec((None, GLA_HEADS, GLA_DK, GLA_DV), lambda b, j: (b, 0, 0, 0))],
        out_shape=[jax.ShapeDtypeStruct((n, D), F32)] + ffn_shapes +
                  [jax.ShapeDtypeStruct((batch, GLA_HEADS, GLA_DK, GLA_DV), F32)],
        input_output_aliases={len(args) + i: 1 + i for i in range(len(alias_bufs))},
        scratch_shapes=[pltpu.VMEM((GLA_HEADS, GLA_DV, GLA_DK), F32),
                        pltpu.VMEM((tt, GLA_DV_TOT), F32),
                        pltpu.VMEM((tt, GLA_DK_TOT), BF16),
                        pltpu.VMEM((tt // GLA_CHUNK * GLA_HEADS, GLA_DV, GLA_DK), F32)],
        compiler_params=_cparams("parallel", "arbitrary"),
        name="gla_mixer",
    )(*args, *alias_bufs)


def _gla1_proj_body(x_ref, mod_ref, ng_ref, wqkvg_ref, wlr_ref, wgk_ref, bgk_ref,
                    q_ref, k_ref, v_ref, go_ref, dec_ref):
    sh1, sc1, _, _, _, _ = _mod_slices(mod_ref)
    ng = ng_ref[...]
    hb = (_rms(x_ref[...], ng[0:1]) * (1.0 + sc1) + sh1).astype(BF16)
    proj = jnp.dot(hb, wqkvg_ref[...], preferred_element_type=F32)
    q_ref[...] = proj[:, :GLA_DK_TOT] * (GLA_DK ** -0.5)
    k_ref[...] = proj[:, GLA_DK_TOT:2 * GLA_DK_TOT]
    v_ref[...] = proj[:, 2 * GLA_DK_TOT:2 * GLA_DK_TOT + GLA_DV_TOT]
    go_ref[...] = proj[:, 2 * GLA_DK_TOT + GLA_DV_TOT:]
    dec_ref[...] = jnp.exp(_gla_gate(hb, wlr_ref, wgk_ref, bgk_ref))


GLA1_TOK = 16


def _gla1_state_body(st_ref, qc_ref, kc_ref, dc_ref, v_ref, nst_ref, o_ref):
    v = v_ref[...]
    for i in range(GLA1_TOK):
        for hd in range(GLA_HEADS):
            vrow = v[i:i + 1, hd * GLA_DV:(hd + 1) * GLA_DV]
            s_new = dc_ref[hd][:, i:i + 1] * st_ref[i, hd] + kc_ref[hd][:, i:i + 1] * vrow
            nst_ref[i, hd] = s_new
            o_ref[i:i + 1, hd * GLA_DV:(hd + 1) * GLA_DV] = jnp.sum(
                qc_ref[hd][:, i:i + 1] * s_new, axis=0, keepdims=True)


def _gla1_out_body(x_ref, o_ref, go_ref, mod_ref, ng_ref, gng_ref, wout_ref, rw_ref, *rest):
    x1_ref, h_ref, lg_ref = rest[-3:]
    _, _, g1, sh2, sc2, _ = _mod_slices(mod_ref)
    ng = ng_ref[...]
    y = jnp.dot(_gla_out(o_ref[...], go_ref[...], gng_ref[...]), wout_ref[...], preferred_element_type=F32)
    x1 = x_ref[...] + g1 * _rms(y, ng[1:2])
    x1_ref[...] = x1
    _ffn_prep(x1, ng, sh2, sc2, rw_ref, h_ref, lg_ref)


def _gla_mixer_one(x2d, mod2, state, ng, wqkvg, wlr, wgk, bgk, gng, wout, rw_t, into=None):
    n = x2d.shape[0]
    consts = (ng, wqkvg, wlr, wgk, bgk)
    q, k, v, go, dec = pl.pallas_call(
        _gla1_proj_body,
        in_specs=[_const_spec(a.shape) for a in (x2d, mod2) + consts],
        out_specs=[_const_spec((n, GLA_DK_TOT)), _const_spec((n, GLA_DK_TOT)), _const_spec((n, GLA_DV_TOT)),
                   _const_spec((n, GLA_DV_TOT)), _const_spec((n, GLA_DK_TOT))],
        out_shape=[jax.ShapeDtypeStruct((n, GLA_DK_TOT), F32), jax.ShapeDtypeStruct((n, GLA_DK_TOT), F32),
                   jax.ShapeDtypeStruct((n, GLA_DV_TOT), F32), jax.ShapeDtypeStruct((n, GLA_DV_TOT), F32),
                   jax.ShapeDtypeStruct((n, GLA_DK_TOT), F32)],
        grid=(1,),
        compiler_params=_cparams("arbitrary"),
        name="gla1_proj",
    )(x2d, mod2, *consts)

    def cols(a):
        return a.reshape(n // GLA1_TOK, GLA1_TOK, GLA_HEADS, GLA_DK).transpose(0, 2, 3, 1)

    col_spec = pl.BlockSpec((None, GLA_HEADS, GLA_DK, GLA1_TOK), lambda i: (i, 0, 0, 0))
    st_spec = pl.BlockSpec((GLA1_TOK, GLA_HEADS, GLA_DK, GLA_DV), lambda i: (i, 0, 0, 0))
    new_state, o = pl.pallas_call(
        _gla1_state_body,
        grid=(n // GLA1_TOK,),
        in_specs=[st_spec, col_spec, col_spec, col_spec, pl.BlockSpec((GLA1_TOK, GLA_DV_TOT), lambda i: (i, 0))],
        out_specs=[st_spec, pl.BlockSpec((GLA1_TOK, GLA_DV_TOT), lambda i: (i, 0))],
        out_shape=[jax.ShapeDtypeStruct(state.shape, F32), jax.ShapeDtypeStruct((n, GLA_DV_TOT), F32)],
        compiler_params=_cparams("parallel"),
        name="gla1_state",
    )(state, cols(q), cols(k), cols(dec), v)

    consts = (mod2, ng, gng, wout, rw_t)
    ffn_shapes, alias_bufs, oblk = _ffn_out(n, None, into)
    n_in = 3 + len(consts)
    x1, h, lg = pl.pallas_call(
        _gla1_out_body,
        grid=(1,),
        in_specs=[_const_spec(a.shape) for a in (x2d, o, go) + consts] +
                 [pl.BlockSpec(memory_space=pl.ANY)] * len(alias_bufs),
        out_specs=[_const_spec((n, D)), pl.BlockSpec((n, DP), lambda i: (oblk, 0)),
                   pl.BlockSpec((N_EXPERTS, n), lambda i: (0, oblk))],
        out_shape=[jax.ShapeDtypeStruct((n, D), F32)] + ffn_shapes,
        input_output_aliases={n_in + i: 1 + i for i in range(len(alias_bufs))},
        compiler_params=_cparams("arbitrary"),
        name="gla1_out",
    )(x2d, o, go, *consts, *alias_bufs)
    return x1, h, lg, new_state


def _moe_routed(h, lg, n, router_bias, wg, wu, wd):
    n_pad = h.shape[0]
    eid, rank, wts, counts = _router(lg, router_bias, n)
    tile_count = ((counts[:, 0] + EXPERT_TILE - 1) // EXPERT_TILE).astype(jnp.int32)
    tile_end = jnp.cumsum(tile_count).astype(jnp.int32)
    tile_first = tile_end - tile_count
    off = tile_first * EXPERT_TILE
    p_alloc = TOP_K * n_pad + N_EXPERTS * EXPERT_TILE
    dest = _dest(off, eid, rank, n, p_alloc - 1)
    dest_w = dest.reshape(TOP_K, n_pad // DISPATCH_W, DISPATCH_W).transpose(1, 0, 2)
    xs = _sc_dispatch(h, dest_w, p_alloc)
    ys = _experts(xs, tile_first, tile_count, tile_end[-1:], wg, wu, wd)
    dest_tm = dest.T.reshape(n_pad // SUM_W, SUM_PARTS, SUM_W * TOP_K // SUM_PARTS)
    w_lanes = jnp.repeat(wts.T, SC_LANES, axis=1)
    return _sc_gather_sum(ys, dest_tm, w_lanes)


def kernel(x_prompt, x_sample, state_gla, c_prompt, c_sample, norm_g, ada_w, ada_b, gm_w_in, gm_b_in,
           gm_ln_g, gm_ln_b, gm_w_s, gm_b_s, gm_w_out, gla_w_in, gla_w_gk, gla_b_gk, gla_norm_g,
           gla_w_out, router_w, router_bias, exp_w_gate, exp_w_up, exp_w_down, sh_w_gate, sh_w_up,
           sh_w_down):
    batch, seq, _ = x_prompt.shape
    n_s = x_sample.shape[0]
    n_p = batch * seq
    tpb = seq // MIX_TILE
    xp = x_prompt.reshape(n_p, D)
    xs = x_sample.reshape(n_s, D)

    mod = _ada(jnp.concatenate([c_prompt, c_sample], axis=0), ada_w, ada_b)
    mod_p = [mod[i, :batch].reshape(batch, 1, 6 * D) for i in range(2)]
    mod_s = [mod[i, batch:] for i in range(2)]
    rw_t = [jnp.concatenate(_split3(router_w[i].T), axis=0) for i in range(2)]

    ws_causal = jnp.tril(gm_w_s[0]).astype(BF16)
    bs_cols = gm_b_s[0].T
    eye = jnp.eye(GM_CHUNK, dtype=F32)
    ws_first = (gm_w_s[0][:, 0, 0][:, None, None] * eye).astype(BF16)
    bs_first = jnp.broadcast_to(gm_b_s[0][:, 0][None, :], (GM_CHUNK, GM_GROUPS))
    gm_args = (norm_g[0], gm_w_in[0].astype(BF16), gm_b_in[0].reshape(1, -1), gm_ln_g[0].reshape(1, -1),
               gm_ln_b[0].reshape(1, -1))
    wout0 = gm_w_out[0].astype(BF16)
    shared = [(sh_w_gate[i].astype(BF16), sh_w_up[i].astype(BF16), sh_w_down[i].astype(BF16))
              for i in range(2)]
    n_qkvg = 2 * GLA_DK_TOT + 2 * GLA_DV_TOT
    wqkvg = gla_w_in[0][:, :n_qkvg].astype(BF16)
    wlr = jnp.pad(gla_w_in[0][:, n_qkvg:], ((0, 0), (0, LANES - GLA_GATE_RANK))).astype(BF16)
    wgk = jnp.pad(gla_w_gk[0], ((0, LANES - GLA_GATE_RANK), (0, 0))).astype(BF16)
    gla_args = (norm_g[1], wqkvg, wlr, wgk, gla_b_gk[0].reshape(1, -1), gla_norm_g[0].reshape(1, -1),
                gla_w_out[0].astype(BF16), rw_t[1])
    experts_f32 = (exp_w_gate, exp_w_up, exp_w_down)

    half = batch // 2
    streams = [(0, half, False), (half, batch - half, True)]
    st = [dict() for _ in streams]

    experts = []
    for layer, (s, (b0, nb, with_new)) in enumerate(zip(st, streams)):
        s["mod_p"] = [mod_p[i][b0:b0 + nb] for i in range(2)]
        s["n"] = nb * seq
        s["n_all"] = s["n"] + (n_s if with_new else 0)
        s["n_pad"] = -(-s["n_all"] // TOKEN_PAD) * TOKEN_PAD
        s["x1p"], s["h"], s["lg"], *w16 = _gmlp_mixer(
            xp, b0 * tpb, s["n"], s["mod_p"][0], False, MIX_TILE, tpb, *gm_args, ws_causal, bs_cols, wout0,
            rw_t[0], emit_v=False, cast_w=(layer, *experts_f32), ffn_rows=s["n_pad"])
        experts.append(w16)
        if with_new:
            s["x1s"], s["h"], s["lg"], v_rows = _gmlp_mixer(
                xs, 0, n_s, mod_s[0], True, n_s, 1, *gm_args, ws_first, bs_first, wout0, rw_t[0], emit_v=True,
                into=(s["h"], s["lg"], s["n"] // n_s))
    for s, (b0, nb, with_new) in zip(st, streams):
        s["routed0"] = _moe_routed(s["h"], s["lg"], s["n_all"], router_bias[0], *experts[0])
        if with_new:
            s["x2s"] = _combine(s["x1s"], s["routed0"], s["h"], s["n"] // n_s, mod_s[0], True, n_s, 1,
                                norm_g[0], *shared[0])
    for s, (b0, nb, with_new) in zip(st, streams):
        s["x3p"], h1, lg1, s["st_p"] = _gla_mixer(s["x1p"], s["routed0"], s["h"], s["mod_p"][0], norm_g[0],
                                                  shared[0], s["mod_p"][1], nb, seq, *gla_args,
                                                  ffn_rows=s["n_pad"])
        if with_new:
            s["x3s"], h1, lg1, st_s = _gla_mixer_one(s["x2s"], mod_s[1], state_gla[:, 0], *gla_args,
                                                     into=(h1, lg1, s["n"] // n_s))
        s["h"], s["lg"] = h1, lg1
    y_prompt = None
    for s, (b0, nb, with_new) in zip(st, streams):
        routed = _moe_routed(s["h"], s["lg"], s["n_all"], router_bias[1], *experts[1])
        y_prompt = _combine(s["x3p"], routed, s["h"], 0, s["mod_p"][1], False, MIX_TILE, tpb, norm_g[1],
                            *shared[1], out_rows=n_p, out_blk0=b0 * tpb, out_buf=y_prompt)
        if with_new:
            y_new = _combine(s["x3s"], routed, s["h"], s["n"] // n_s, mod_s[1], True, n_s, 1, norm_g[1],
                             *shared[1])
    st_p = jnp.concatenate([s["st_p"] for s in st], axis=0)

    return (y_prompt.reshape(batch, seq, D), y_new.reshape(n_s, 1, D), st_p[:, None], st_s[:, None],
            v_rows.reshape(n_s, 1, 1, GM_HALF))
```

```python
import functools
import math

import jax
import jax.numpy as jnp
from jax import lax
from jax.experimental import pallas as pl
from jax.experimental.pallas import tpu as pltpu
from jax.experimental.pallas import tpu_sc as plsc

F32 = jnp.float32
BF16 = jnp.bfloat16

D = 1024
DP = D // 2
GM_CHUNK = 128
GM_HALF = 2 * D
GM_GROUPS = 8
GM_GROUP_DIM = GM_HALF // GM_GROUPS
GLA_HEADS = 4
GLA_DK = 128
GLA_DV = 256
GLA_DK_TOT = GLA_HEADS * GLA_DK
GLA_DV_TOT = GLA_HEADS * GLA_DV
GLA_GATE_RANK = 16
GLA_GATE_NORMALIZER = 16.0
GLA_CHUNK = 64
N_EXPERTS = 64
TOP_K = 8
N_EXPERT_GROUPS = 8
GROUP_SIZE = N_EXPERTS // N_EXPERT_GROUPS
TOPK_GROUPS = 4
EXPERT_DIM = D // 4
ROUTED_SCALE = 2.5
NORM_EPS = 1e-6
LN_EPS = 1e-5

LANES = 128
VMEM_LIMIT = 56 * 1024 * 1024

MIX_TILE = 256
GLA_TILE = 512
GLA_CUM_BLOCK = 256
GM_COL_BLOCK = 512
ROUTER_TILE = 1024
EXPERT_TILE = 544
EXPERT_X_SLOTS = 6
EXPERT_AHEAD = EXPERT_X_SLOTS - 2
EXPERT_Y_SLOTS = 4
SC_WORKERS = 32
DISPATCH_W = 32
SC_LANES = 16
SUM_W = 16
SUM_PARTS = 4
SUM_TOKENS = 1
SUM_UNROLL = 4
TOKEN_PAD = SC_WORKERS * DISPATCH_W


def _cparams(*sem):
    return pltpu.CompilerParams(dimension_semantics=sem, vmem_limit_bytes=VMEM_LIMIT)


def _rms(x, g):
    return x * lax.rsqrt(jnp.mean(x * x, axis=-1, keepdims=True) + NORM_EPS) * g


def _silu(x):
    return x * (1.0 / (1.0 + jnp.exp(-x)))


def _gelu(x):
    return 0.5 * x * (1.0 + lax.erf(x * (1.0 / math.sqrt(2.0))))


def _bdot(a, b):
    return jnp.dot(a.astype(BF16), b.astype(BF16), preferred_element_type=F32)


def _dot_nt(a, b, precision=None):
    return lax.dot_general(a, b, (((1,), (1,)), ((), ())), preferred_element_type=F32,
                           precision=precision)


def _mod_slices(mod_ref):
    return [mod_ref[:, i * D:(i + 1) * D] for i in range(6)]


HI_HALF = -65536


def _pack_rows(x):
    lo = lax.bitcast_convert_type(x[:, :DP].astype(BF16).astype(F32), jnp.int32)
    hi = lax.bitcast_convert_type(x[:, DP:].astype(BF16).astype(F32), jnp.int32)
    return lax.shift_right_logical(lo, 16) | (hi & HI_HALF)


def _unpack_rows(p):
    lo = lax.bitcast_convert_type(lax.shift_left(p, 16), F32)
    hi = lax.bitcast_convert_type(p & HI_HALF, F32)
    return lo, hi


def _ffn_prep(x1, ng, sh2, sc2, rw_ref, h_ref, lg_ref, rows=slice(None)):
    hffn = _rms(x1, ng[2:3]) * (1.0 + sc2) + sh2
    h_ref[rows, :] = _pack_rows(hffn)
    lg3 = _dot_nt(rw_ref[...], hffn.astype(BF16))
    lg_ref[:, rows] = lg3[:N_EXPERTS] + lg3[N_EXPERTS:2 * N_EXPERTS] + lg3[2 * N_EXPERTS:]


def _ada_body(c_ref, w_ref, b_ref, o_ref):
    c = c_ref[...]
    o_ref[...] = _bdot(_silu(c), w_ref[...]) + b_ref[...]


def _ada(c, ada_w, ada_b):
    n = c.shape[0]
    depth = ada_w.shape[0]
    tn = 1536
    return pl.pallas_call(
        _ada_body,
        grid=(depth, 6 * D // tn),
        in_specs=[pl.BlockSpec((n, D), lambda l, j: (0, 0)),
                  pl.BlockSpec((None, D, tn), lambda l, j: (l, 0, j)),
                  pl.BlockSpec((None, 1, tn), lambda l, j: (l, 0, j))],
        out_specs=pl.BlockSpec((None, n, tn), lambda l, j: (l, 0, j)),
        out_shape=jax.ShapeDtypeStruct((depth, n, 6 * D), F32),
        compiler_params=_cparams("parallel", "parallel"),
        name="ada_mod",
    )(c, ada_w, ada_b.reshape(depth, 1, 6 * D))


def _mod_spec(per_row, tt, tiles_per_batch):
    if per_row:
        return pl.BlockSpec((tt, 6 * D), lambda i: (i, 0))
    return pl.BlockSpec((None, 1, 6 * D), lambda i: (i // tiles_per_batch, 0, 0))


def _const_spec(shape):
    zeros = (0,) * len(shape)
    return pl.BlockSpec(shape, lambda *_: zeros)


def _gmlp_body(x_ref, mod_ref, ng_ref, win_ref, bin_ref, lng_ref, lnb_ref, ws_ref, bs_ref, wout_ref,
               rw_ref, *rest, n_chunks, emit_v, cast_w, n_alias):
    rest = list(rest)
    w32_refs = [rest.pop(0) for _ in range(3)] if cast_w else []
    rest = rest[n_alias:]
    x1_ref, h_ref, lg_ref = rest[:3]
    rest = rest[3:]
    v_ref = rest.pop(0) if emit_v else None
    w16_refs = [rest.pop(0) for _ in range(3)] if cast_w else []
    um_ref, z_ref = rest
    for src, dst in zip(w32_refs, w16_refs):
        dst[...] = src[...].astype(BF16)
    sh1, sc1, g1, sh2, sc2, _ = _mod_slices(mod_ref)
    ng = ng_ref[...]
    x = x_ref[...]
    hb = (_rms(x, ng[0:1]) * (1.0 + sc1) + sh1).astype(BF16)
    for cb in range(2 * GM_HALF // GM_COL_BLOCK):
        cols = slice(cb * GM_COL_BLOCK, (cb + 1) * GM_COL_BLOCK)
        z_ref[:, cols] = _gelu(jnp.dot(hb, win_ref[:, cols], preferred_element_type=F32) + bin_ref[:, cols])
    u = z_ref[:, :GM_HALF]
    v = z_ref[:, GM_HALF:]
    mu = jnp.mean(v, axis=-1, keepdims=True)
    vc = v - mu
    var = jnp.mean(vc * vc, axis=-1, keepdims=True)
    v = vc * lax.rsqrt(var + LN_EPS) * lng_ref[...] + lnb_ref[...]
    if emit_v:
        v_ref[...] = v
    vb = v.astype(BF16)
    for c in range(n_chunks):
        rows = slice(c * GM_CHUNK, (c + 1) * GM_CHUNK)
        for g in range(GM_GROUPS):
            cols = slice(g * GM_GROUP_DIM, (g + 1) * GM_GROUP_DIM)
            mixed = jnp.dot(ws_ref[g], vb[rows, cols], preferred_element_type=F32) + bs_ref[:, g:g + 1]
            um_ref[rows, cols] = (u[rows, cols] * mixed).astype(BF16)
    y = jnp.dot(um_ref[...], wout_ref[...], preferred_element_type=F32)
    x1 = x + g1 * _rms(y, ng[1:2])
    x1_ref[...] = x1
    _ffn_prep(x1, ng, sh2, sc2, rw_ref, h_ref, lg_ref)


def _ffn_out(n, ffn_rows, into):
    rows = ffn_rows or n
    oblk = 0
    bufs = []
    if into is not None:
        *bufs, oblk = into
        rows = bufs[0].shape[0]
    elif rows != n:
        bufs = [jnp.zeros((rows, DP), jnp.int32), jnp.zeros((N_EXPERTS, rows), F32)]
    shapes = [jax.ShapeDtypeStruct((rows, DP), jnp.int32), jax.ShapeDtypeStruct((N_EXPERTS, rows), F32)]
    return shapes, bufs, oblk


def _gmlp_mixer(x2d, blk0, n, mod, per_row, tt, tiles_per_batch, ng, win, b_in, ln_g, ln_b, ws, bs, wout,
                rw_t, emit_v, cast_w=None, ffn_rows=None, into=None):
    steps = n // tt
    ffn_shapes, alias_bufs, oblk = _ffn_out(n, ffn_rows, into)
    out_shape = [jax.ShapeDtypeStruct((n, D), F32)] + ffn_shapes
    out_specs = [pl.BlockSpec((tt, D), lambda i: (i, 0)), pl.BlockSpec((tt, DP), lambda i: (i + oblk, 0)),
                 pl.BlockSpec((N_EXPERTS, tt), lambda i: (0, i + oblk))]
    if emit_v:
        out_shape.append(jax.ShapeDtypeStruct((n, GM_HALF), F32))
        out_specs.append(pl.BlockSpec((tt, GM_HALF), lambda i: (i, 0)))

    def one_buffer(a):
        zeros = (0,) * a.ndim
        return pl.BlockSpec(a.shape, lambda *_: zeros, pipeline_mode=pl.Buffered(1))

    consts = (ng, win, b_in, ln_g, ln_b, ws, bs, wout, rw_t)
    in_specs = [pl.BlockSpec((tt, D), lambda i: (i + blk0, 0)), _mod_spec(per_row, tt, tiles_per_batch)]
    in_specs += [one_buffer(a) for a in consts]
    args = [x2d, mod, *consts]
    if cast_w is not None:
        layer, *w_all = cast_w
        per_step = N_EXPERTS // steps
        for w in w_all:
            blk = (None, per_step) + w.shape[2:]
            in_specs.append(pl.BlockSpec(blk, lambda i: (layer, i, 0, 0)))
            out_specs.append(pl.BlockSpec(blk[1:], lambda i: (i, 0, 0)))
            out_shape.append(jax.ShapeDtypeStruct(w.shape[1:], BF16))
            args.append(w)
    aliases = {len(args) + i: 1 + i for i in range(len(alias_bufs))}
    in_specs += [pl.BlockSpec(memory_space=pl.ANY)] * len(alias_bufs)
    args += alias_bufs
    return pl.pallas_call(
        functools.partial(_gmlp_body, n_chunks=tt // GM_CHUNK, emit_v=emit_v, cast_w=cast_w is not None,
                          n_alias=len(alias_bufs)),
        grid=(steps,),
        in_specs=in_specs,
        out_specs=out_specs,
        out_shape=out_shape,
        input_output_aliases=aliases,
        scratch_shapes=[pltpu.VMEM((tt, GM_HALF), BF16), pltpu.VMEM((tt, 2 * GM_HALF), F32)],
        compiler_params=_cparams("parallel"),
        name="gmlp_mixer_rows" if per_row else "gmlp_mixer",
    )(*args)


def _combine_body(x_ref, y_ref, h_ref, mod_ref, ng_ref, swg_ref, swu_ref, swd_ref, *rest):
    o_ref = rest[-1]
    o_ref[...] = _channel_mix_residual(x_ref[...], y_ref[...], h_ref[...], mod_ref[:, 5 * D:6 * D],
                                       ng_ref[3:4, :], swg_ref, swu_ref, swd_ref)


def _combine(x2d, routed, hp, blk0, mod, per_row, tt, tiles_per_batch, ng, swg, swu, swd,
             out_rows=None, out_blk0=0, out_buf=None):
    n = x2d.shape[0]
    in_specs = [pl.BlockSpec((tt, D), lambda i: (i, 0)),
                pl.BlockSpec((tt, D), lambda i: (i + blk0, 0)),
                pl.BlockSpec((tt, DP), lambda i: (i + blk0, 0)),
                _mod_spec(per_row, tt, tiles_per_batch),
                _const_spec(ng.shape), _const_spec(swg.shape), _const_spec(swu.shape),
                _const_spec(swd.shape)]
    args = [x2d, routed, hp, mod, ng, swg, swu, swd]
    aliases = {}
    if out_buf is not None:
        in_specs.append(pl.BlockSpec(memory_space=pl.ANY))
        aliases = {len(args): 0}
        args.append(out_buf)
    return pl.pallas_call(
        _combine_body,
        grid=(n // tt,),
        in_specs=in_specs,
        out_specs=pl.BlockSpec((tt, D), lambda i: (i + out_blk0, 0)),
        out_shape=jax.ShapeDtypeStruct((out_rows or n, D), F32),
        input_output_aliases=aliases,
        compiler_params=_cparams("parallel"),
        name="combine_rows" if per_row else "combine",
    )(*args)


def _router_body(lg_ref, bias_ref, tri_ref, eid_ref, rank_ref, wts_ref, cnt_ref, carry_ref, *, n_real):
    step = pl.program_id(0)

    @pl.when(step == 0)
    def _():
        carry_ref[...] = jnp.zeros_like(carry_ref)

    lg = lg_ref[...]
    tn = lg.shape[1]
    real = (step * tn + lax.broadcasted_iota(jnp.int32, (1, tn), 1)) < n_real
    lg = jnp.where(real, lg, 0.0)
    scores = 1.0 / (1.0 + jnp.exp(-lg))
    sel = scores + bias_ref[...]
    neg = -jnp.inf
    sub8 = lax.broadcasted_iota(jnp.int32, (GROUP_SIZE, tn), 0)
    gsub = lax.broadcasted_iota(jnp.int32, (N_EXPERT_GROUPS, tn), 0)
    gs = jnp.zeros((N_EXPERT_GROUPS, tn), F32)
    for g in range(N_EXPERT_GROUPS):
        blk = sel[g * GROUP_SIZE:(g + 1) * GROUP_SIZE, :]
        m1 = jnp.max(blk, axis=0, keepdims=True)
        i1 = jnp.min(jnp.where(blk == m1, sub8, GROUP_SIZE), axis=0, keepdims=True)
        m2 = jnp.max(jnp.where(sub8 == i1, neg, blk), axis=0, keepdims=True)
        gs = jnp.where(gsub == g, m1 + m2, gs)
    gmask = jnp.zeros((N_EXPERT_GROUPS, tn), jnp.bool_)
    for _ in range(TOPK_GROUPS):
        m = jnp.max(gs, axis=0, keepdims=True)
        i = jnp.min(jnp.where(gs == m, gsub, N_EXPERT_GROUPS), axis=0, keepdims=True)
        hit = gsub == i
        gmask = jnp.logical_or(gmask, hit)
        gs = jnp.where(hit, neg, gs)
    gmaskf = gmask.astype(F32)
    blocks = []
    for g in range(N_EXPERT_GROUPS):
        keep = jnp.broadcast_to(gmaskf[g:g + 1, :], (GROUP_SIZE, tn)) > 0.5
        blocks.append(jnp.where(keep, sel[g * GROUP_SIZE:(g + 1) * GROUP_SIZE, :], neg))
    msel = jnp.concatenate(blocks, axis=0)
    esub = lax.broadcasted_iota(jnp.int32, (N_EXPERTS, tn), 0)
    chosen = jnp.zeros((N_EXPERTS, tn), jnp.bool_)
    picks = []
    for _ in range(TOP_K):
        m = jnp.max(msel, axis=0, keepdims=True)
        i = jnp.min(jnp.where(msel == m, esub, N_EXPERTS), axis=0, keepdims=True)
        hit = esub == i
        picks.append(i)
        chosen = jnp.logical_or(chosen, hit)
        msel = jnp.where(hit, neg, msel)
    w = jnp.where(chosen, scores, 0.0)
    w = w / jnp.sum(w, axis=0, keepdims=True) * ROUTED_SCALE
    counted = jnp.where(jnp.logical_and(chosen, real), 1.0, 0.0)
    incl = jnp.dot(counted.astype(BF16), tri_ref[...], preferred_element_type=F32)
    rank_full = carry_ref[:, 0:1] + incl - 1.0
    ksub = lax.broadcasted_iota(jnp.int32, (TOP_K, tn), 0)
    eid = jnp.zeros((TOP_K, tn), jnp.int32)
    rank = jnp.zeros((TOP_K, tn), F32)
    wts = jnp.zeros((TOP_K, tn), F32)
    for k in range(TOP_K):
        hit = esub == picks[k]
        eid = jnp.where(ksub == k, picks[k], eid)
        rank = jnp.where(ksub == k, jnp.sum(jnp.where(hit, rank_full, 0.0), axis=0, keepdims=True), rank)
        wts = jnp.where(ksub == k, jnp.sum(jnp.where(hit, w, 0.0), axis=0, keepdims=True), wts)
    eid_ref[...] = eid
    rank_ref[...] = rank.astype(jnp.int32)
    wts_ref[...] = wts
    carry = carry_ref[...] + incl[:, tn - 1:tn]
    carry_ref[...] = carry
    cnt_ref[...] = carry.astype(jnp.int32)


def _router(lg_t, bias, n_real):
    n = lg_t.shape[1]
    tn = ROUTER_TILE
    idx = jnp.arange(tn)
    tri = (idx[:, None] <= idx[None, :]).astype(BF16)
    kspec = pl.BlockSpec((TOP_K, tn), lambda i: (0, i))
    return pl.pallas_call(
        functools.partial(_router_body, n_real=n_real),
        grid=(n // tn,),
        in_specs=[pl.BlockSpec((N_EXPERTS, tn), lambda i: (0, i)), _const_spec((N_EXPERTS, 1)),
                  _const_spec((tn, tn))],
        out_specs=[kspec, kspec, kspec, _const_spec((N_EXPERTS, LANES))],
        out_shape=[jax.ShapeDtypeStruct((TOP_K, n), jnp.int32), jax.ShapeDtypeStruct((TOP_K, n), jnp.int32),
                   jax.ShapeDtypeStruct((TOP_K, n), F32), jax.ShapeDtypeStruct((N_EXPERTS, LANES), jnp.int32)],
        scratch_shapes=[pltpu.VMEM((N_EXPERTS, LANES), F32)],
        compiler_params=_cparams("arbitrary"),
        name="router",
    )(lg_t, bias.reshape(N_EXPERTS, 1), tri)


def _dest_body(off_ref, eid_ref, rank_ref, dest_ref, *, n_real, last_row):
    eid = eid_ref[...]
    base = jnp.zeros(eid.shape, jnp.int32)
    for e in range(N_EXPERTS):
        base = jnp.where(eid == e, off_ref[e], base)
    tok = lax.broadcasted_iota(jnp.int32, eid.shape, 1)
    slot = lax.broadcasted_iota(jnp.int32, eid.shape, 0)
    unused = last_row - ((tok - n_real) * TOP_K + slot)
    dest_ref[...] = jnp.where(tok < n_real, base + rank_ref[...], unused)


def _dest(off, eid, rank, n_real, last_row):
    spec = pl.BlockSpec(eid.shape, lambda i, off_ref: (0, 0))
    return pl.pallas_call(
        functools.partial(_dest_body, n_real=n_real, last_row=last_row),
        grid_spec=pltpu.PrefetchScalarGridSpec(num_scalar_prefetch=1, grid=(1,), in_specs=[spec, spec],
                                               out_specs=spec),
        out_shape=jax.ShapeDtypeStruct(eid.shape, jnp.int32),
        compiler_params=_cparams("arbitrary"),
        name="dest_rows",
    )(off, eid, rank)


def _sc_mesh():
    return plsc.VectorSubcoreMesh(core_axis_name="core", subcore_axis_name="subcore")


def _sc_dispatch(hp, dest_w, p_alloc):
    n = hp.shape[0]
    w = dest_w.shape[2]

    @functools.partial(pl.kernel, out_type=jax.ShapeDtypeStruct((p_alloc, DP), jnp.int32), mesh=_sc_mesh(),
                       name="sc_dispatch")
    def run(hp_hbm, dest_hbm, xs_hbm):
        def body(x_vmem, i_vmem):
            for k in range(TOP_K):
                pltpu.sync_copy(x_vmem, xs_hbm.at[i_vmem.at[k]])

        pltpu.emit_pipeline(
            body,
            grid=(n // w,),
            in_specs=[pl.BlockSpec((w, DP), lambda i: (i, 0)),
                      pl.BlockSpec((None, TOP_K, w), lambda i: (i, 0, 0))],
            out_specs=[],
            core_axis_name=("core", "subcore"),
            dimension_semantics=(pltpu.PARALLEL,),
        )(hp_hbm, dest_hbm)

    return run(hp, dest_w)


def _sc_gather_sum(ys, dest_tm, w_lanes):
    n_win, parts, pk = dest_tm.shape
    w = parts * pk // TOP_K
    wp = w // parts
    n_vec = DP // SC_LANES

    @functools.partial(pl.kernel, out_type=jax.ShapeDtypeStruct((n_win * w, D), F32), mesh=_sc_mesh(),
                       scratch_types=[pltpu.VMEM((parts, pk, DP), jnp.int32), pltpu.SemaphoreType.DMA((parts,))],
                       compiler_params=pltpu.CompilerParams(needs_layout_passes=False), name="sc_gather_sum")
    def run(ys_hbm, dest_hbm, w_hbm, o_hbm, rows_v, sems):
        def body(i_vmem, w_vmem, o_vmem):
            copies = [pltpu.async_copy(ys_hbm.at[i_vmem.at[p]], rows_v.at[p], sems.at[p]) for p in range(parts)]
            for p in range(parts):
                copies[p].wait()

                for t0 in range(0, wp, SUM_TOKENS):
                    toks = [p * wp + t0 + u for u in range(SUM_TOKENS)]
                    wv = [[w_vmem[tok, pl.ds(k * SC_LANES, SC_LANES)] for k in range(TOP_K)] for tok in toks]

                    @plsc.parallel_loop(0, n_vec, unroll=SUM_UNROLL)
                    def _(j):
                        col = j * SC_LANES
                        for u, tok in enumerate(toks):
                            lo = jnp.zeros((SC_LANES,), F32)
                            hi = jnp.zeros((SC_LANES,), F32)
                            for k in range(TOP_K):
                                word = rows_v[p, (t0 + u) * TOP_K + k, pl.ds(col, SC_LANES)]
                                lo = lo + wv[u][k] * plsc.bitcast(lax.shift_left(word, 16), F32)
                                hi = hi + wv[u][k] * plsc.bitcast(word & HI_HALF, F32)
                            o_vmem[tok, pl.ds(col, SC_LANES)] = lo
                            o_vmem[tok, pl.ds(DP + col, SC_LANES)] = hi

        pltpu.emit_pipeline(
            body,
            grid=(n_win,),
            in_specs=[pl.BlockSpec((None, parts, pk), lambda i: (i, 0, 0)),
                      pl.BlockSpec((w, TOP_K * SC_LANES), lambda i: (i, 0))],
            out_specs=[pl.BlockSpec((w, D), lambda i: (i, 0))],
            core_axis_name=("core", "subcore"),
            dimension_semantics=(pltpu.PARALLEL,),
        )(dest_hbm, w_hbm, o_hbm)

    return run(ys, dest_tm, w_lanes)


def _expert_body(first_ref, cnt_ref, nused_ref, xs_hbm, wg_s, wu_s, wd_s, ys_hbm, xbuf, ybuf, xsem, ysem):
    e = pl.program_id(0)
    n_used = nused_ref[0]

    def load(g):
        rows = pl.ds(pl.multiple_of(g * EXPERT_TILE, EXPERT_TILE), EXPERT_TILE)
        slot = g % EXPERT_X_SLOTS
        return pltpu.make_async_copy(xs_hbm.at[rows], xbuf.at[slot], xsem.at[slot])

    def store(g):
        rows = pl.ds(pl.multiple_of(g * EXPERT_TILE, EXPERT_TILE), EXPERT_TILE)
        slot = g % EXPERT_Y_SLOTS
        return pltpu.make_async_copy(ybuf.at[slot], ys_hbm.at[rows], ysem.at[slot])

    @pl.when(e == 0)
    def _():
        for g in range(EXPERT_AHEAD):
            @pl.when(g < n_used)
            def _():
                load(g).start()

    first = first_ref[e]
    cnt = cnt_ref[e]

    def acquire(g):
        ahead = g + EXPERT_AHEAD

        @pl.when(ahead < n_used)
        def _():
            load(ahead).start()

        load(g).wait()

        @pl.when(g >= EXPERT_Y_SLOTS)
        def _():
            store(g - EXPERT_Y_SLOTS).wait()

    def compute(g):
        lo, hi = _unpack_rows(xbuf[g % EXPERT_X_SLOTS])
        lo = lo.astype(BF16)
        hi = hi.astype(BF16)

        def xdot(w_s):
            return (jnp.dot(lo, w_s[:DP, :], preferred_element_type=F32) +
                    jnp.dot(hi, w_s[DP:, :], preferred_element_type=F32))

        a = (_silu(xdot(wg_s)) * xdot(wu_s)).astype(BF16)
        ybuf[g % EXPERT_Y_SLOTS] = _pack_rows(jnp.dot(a, wd_s[...], preferred_element_type=F32))

    def pair(j, carry):
        g = first + 2 * j
        acquire(g)
        acquire(g + 1)
        compute(g)
        compute(g + 1)
        store(g).start()
        store(g + 1).start()
        return carry

    lax.fori_loop(0, cnt // 2, pair, 0)

    @pl.when(cnt % 2 == 1)
    def _():
        g = first + cnt - 1
        acquire(g)
        compute(g)
        store(g).start()

    @pl.when(e == N_EXPERTS - 1)
    def _():
        for k in range(EXPERT_Y_SLOTS):
            g = n_used - 1 - k

            @pl.when(g >= 0)
            def _():
                store(g).wait()


def _experts(xs, tile_first, tile_count, n_used, wg, wu, wd):
    def w_map(e, first, cnt, nu):
        return (e, 0, 0)

    return pl.pallas_call(
        _expert_body,
        grid_spec=pltpu.PrefetchScalarGridSpec(
            num_scalar_prefetch=3, grid=(N_EXPERTS,),
            in_specs=[pl.BlockSpec(memory_space=pl.ANY),
                      pl.BlockSpec((None, D, EXPERT_DIM), w_map),
                      pl.BlockSpec((None, D, EXPERT_DIM), w_map),
                      pl.BlockSpec((None, EXPERT_DIM, D), w_map)],
            out_specs=pl.BlockSpec(memory_space=pl.ANY),
            scratch_shapes=[pltpu.VMEM((EXPERT_X_SLOTS, EXPERT_TILE, DP), jnp.int32),
                            pltpu.VMEM((EXPERT_Y_SLOTS, EXPERT_TILE, DP), jnp.int32),
                            pltpu.SemaphoreType.DMA((EXPERT_X_SLOTS,)),
                            pltpu.SemaphoreType.DMA((EXPERT_Y_SLOTS,))]),
        out_shape=jax.ShapeDtypeStruct(xs.shape, jnp.int32),
        compiler_params=_cparams("arbitrary"),
        name="experts",
    )(tile_first, tile_count, n_used, xs, wg, wu, wd)


def _log_sigmoid(z):
    return jnp.minimum(z, 0.0) - jnp.log(1.0 + jnp.exp(-jnp.abs(z)))


def _gla_gate(hb, wlr_ref, wgk_ref, bgk_ref):
    lr = jnp.dot(hb, wlr_ref[...], preferred_element_type=F32)
    z = _bdot(lr, wgk_ref[...]) + bgk_ref[...]
    return _log_sigmoid(z) * (1.0 / GLA_GATE_NORMALIZER)


def _split3(a):
    hi = a.astype(BF16)
    r1 = a - hi.astype(F32)
    mid = r1.astype(BF16)
    lo = (r1 - mid.astype(F32)).astype(BF16)
    return hi, mid, lo


def _gla_out(o_ref_val, go, gng):
    parts = []
    for hd in range(GLA_HEADS):
        cols = slice(hd * GLA_DV, (hd + 1) * GLA_DV)
        parts.append((_rms(o_ref_val[:, cols], gng) * _silu(go[:, cols])).astype(BF16))
    return jnp.concatenate(parts, axis=1)


def _channel_mix_residual(x, routed, h_packed, g2, ng3, swg_ref, swu_ref, swd_ref):
    h_lo, h_hi = _unpack_rows(h_packed)
    h_lo = h_lo.astype(BF16)
    h_hi = h_hi.astype(BF16)

    def hdot(w_ref_):
        return (jnp.dot(h_lo, w_ref_[:DP, :], preferred_element_type=F32) +
                jnp.dot(h_hi, w_ref_[DP:, :], preferred_element_type=F32))

    hs = (_silu(hdot(swg_ref)) * hdot(swu_ref)).astype(BF16)
    y = jnp.dot(hs, swd_ref[...], preferred_element_type=F32) + routed
    return x + g2 * _rms(y, ng3)


def _gla_body(x_ref, y_ref, hprev_ref, modprev_ref, ngprev_ref, swg_ref, swu_ref, swd_ref,
              mod_ref, ng_ref, wqkvg_ref, wlr_ref, wgk_ref, bgk_ref, tril_ref, gng_ref, wout_ref,
              rw_ref, *rest, tt, n_alias):
    x1_ref, h_ref, lg_ref, st_ref, st_scr, o_scr, qd_scr, dst_scr = rest[n_alias:]
    j = pl.program_id(1)

    @pl.when(j == 0)
    def _():
        st_scr[...] = jnp.zeros_like(st_scr)

    x = _channel_mix_residual(x_ref[...], y_ref[...], hprev_ref[...], modprev_ref[:, 5 * D:6 * D],
                              ngprev_ref[3:4, :], swg_ref, swu_ref, swd_ref)
    sh1, sc1, g1, sh2, sc2, _ = _mod_slices(mod_ref)
    ng = ng_ref[...]
    hb = (_rms(x, ng[0:1]) * (1.0 + sc1) + sh1).astype(BF16)
    q = jnp.dot(hb, wqkvg_ref[:, :GLA_DK_TOT], preferred_element_type=F32) * (GLA_DK ** -0.5)
    k = jnp.dot(hb, wqkvg_ref[:, GLA_DK_TOT:2 * GLA_DK_TOT], preferred_element_type=F32)
    v = jnp.dot(hb, wqkvg_ref[:, 2 * GLA_DK_TOT:2 * GLA_DK_TOT + GLA_DV_TOT],
                preferred_element_type=F32).astype(BF16)
    log_a = _gla_gate(hb, wlr_ref, wgk_ref, bgk_ref)
    tril = tril_ref[...]
    parts = _split3(log_a)
    b = jnp.concatenate(
        [sum(jnp.dot(tril, p[r:r + GLA_CUM_BLOCK], preferred_element_type=F32) for p in parts)
         for r in range(0, tt, GLA_CUM_BLOCK)], axis=0)
    row = lax.broadcasted_iota(jnp.int32, (GLA_CHUNK, GLA_CHUNK), 0)
    col = lax.broadcasted_iota(jnp.int32, (GLA_CHUNK, GLA_CHUNK), 1)
    causal = row >= col
    n_chunks = tt // GLA_CHUNK
    for c in range(n_chunks):
        rows = slice(c * GLA_CHUNK, (c + 1) * GLA_CHUNK)
        last = (c + 1) * GLA_CHUNK - 1
        for hd in range(GLA_HEADS):
            kc = slice(hd * GLA_DK, (hd + 1) * GLA_DK)
            vc = slice(hd * GLA_DV, (hd + 1) * GLA_DV)
            bb = b[rows, kc]
            b_last = b[last:last + 1, kc]
            q_dec = (q[rows, kc] * jnp.exp(bb)).astype(BF16)
            k_inv = (k[rows, kc] * jnp.exp(-bb)).astype(BF16)
            k_end = (k[rows, kc] * jnp.exp(b_last - bb)).astype(BF16)
            att = jnp.where(causal, _dot_nt(q_dec, k_inv), 0.0).astype(BF16)
            qd_scr[rows, kc] = q_dec
            o_scr[rows, vc] = jnp.dot(att, v[rows, vc], preferred_element_type=F32)
            dst_scr[c * GLA_HEADS + hd] = lax.dot_general(
                v[rows, vc], k_end, (((0,), (0,)), ((), ())), preferred_element_type=F32)
    states = [st_scr[hd] for hd in range(GLA_HEADS)]
    for c in range(n_chunks):
        rows = slice(c * GLA_CHUNK, (c + 1) * GLA_CHUNK)
        last = (c + 1) * GLA_CHUNK - 1
        for hd in range(GLA_HEADS):
            kc = slice(hd * GLA_DK, (hd + 1) * GLA_DK)
            vc = slice(hd * GLA_DV, (hd + 1) * GLA_DV)
            o_scr[rows, vc] += _dot_nt(qd_scr[rows, kc], states[hd].astype(BF16))
            states[hd] = states[hd] * jnp.exp(b[last:last + 1, kc]) + dst_scr[c * GLA_HEADS + hd]
    for hd in range(GLA_HEADS):
        st_scr[hd] = states[hd]

    @pl.when(j == pl.num_programs(1) - 1)
    def _():
        for hd in range(GLA_HEADS):
            st_ref[hd] = st_scr[hd].T

    go = jnp.dot(hb, wqkvg_ref[:, 2 * GLA_DK_TOT + GLA_DV_TOT:], preferred_element_type=F32)
    y = jnp.dot(_gla_out(o_scr[...], go, gng_ref[...]), wout_ref[...], preferred_element_type=F32)
    x1 = x + g1 * _rms(y, ng[1:2])
    x1_ref[...] = x1
    _ffn_prep(x1, ng, sh2, sc2, rw_ref, h_ref, lg_ref)


def _gla_mixer(x2d, routed, hp, mod3_prev, ng_prev, shared_prev, mod3, batch, seq, ng, wqkvg, wlr, wgk, bgk,
               gng, wout, rw_t, ffn_rows=None):
    tt = GLA_TILE
    tpb = seq // tt
    n = x2d.shape[0]
    idx = jnp.arange(GLA_CUM_BLOCK)
    tril = ((idx[:, None] >= idx[None, :]) &
            (idx[:, None] // GLA_CHUNK == idx[None, :] // GLA_CHUNK)).astype(BF16)
    row_map = lambda b, j: (b * tpb + j, 0)
    mod_map = lambda b, j: (b, 0, 0)
    consts = (ng, wqkvg, wlr, wgk, bgk, tril, gng, wout, rw_t)
    prev_consts = (ng_prev,) + tuple(shared_prev)
    ffn_shapes, alias_bufs, _ = _ffn_out(n, ffn_rows, None)
    args = [x2d, routed, hp, mod3_prev, *prev_consts, mod3, *consts]
    return pl.pallas_call(
        functools.partial(_gla_body, tt=tt, n_alias=len(alias_bufs)),
        grid=(batch, tpb),
        in_specs=[pl.BlockSpec((tt, D), row_map), pl.BlockSpec((tt, D), row_map),
                  pl.BlockSpec((tt, DP), row_map), pl.BlockSpec((None, 1, 6 * D), mod_map)] +
                 [_const_spec(a.shape) for a in prev_consts] +
                 [pl.BlockSpec((None, 1, 6 * D), mod_map)] +
                 [_const_spec(a.shape) for a in consts] +
                 [pl.BlockSpec(memory_space=pl.ANY)] * len(alias_bufs),
        out_specs=[pl.BlockSpec((tt, D), row_map), pl.BlockSpec((tt, DP), row_map),
                   pl.BlockSpec((N_EXPERTS, tt), lambda b, j: (0, b * tpb + j)),
                   pl.BlockSpec((None, GLA_HEADS, GLA_DK, GLA_DV), lambda b, j: (b, 0, 0, 0))],
        out_shape=[jax.ShapeDtypeStruct((n, D), F32)] + ffn_shapes +
                  [jax.ShapeDtypeStruct((batch, GLA_HEADS, GLA_DK, GLA_DV), F32)],
        input_output_aliases={len(args) + i: 1 + i for i in range(len(alias_bufs))},
        scratch_shapes=[pltpu.VMEM((GLA_HEADS, GLA_DV, GLA_DK), F32),
                        pltpu.VMEM((tt, GLA_DV_TOT), F32),
                        pltpu.VMEM((tt, GLA_DK_TOT), BF16),
                        pltpu.VMEM((tt // GLA_CHUNK * GLA_HEADS, GLA_DV, GLA_DK), F32)],
        compiler_params=_cparams("parallel", "arbitrary"),
        name="gla_mixer",
    )(*args, *alias_bufs)


def _gla1_proj_body(x_ref, mod_ref, ng_ref, wqkvg_ref, wlr_ref, wgk_ref, bgk_ref,
                    q_ref, k_ref, v_ref, go_ref, dec_ref):
    sh1, sc1, _, _, _, _ = _mod_slices(mod_ref)
    ng = ng_ref[...]
    hb = (_rms(x_ref[...], ng[0:1]) * (1.0 + sc1) + sh1).astype(BF16)
    proj = jnp.dot(hb, wqkvg_ref[...], preferred_element_type=F32)
    q_ref[...] = proj[:, :GLA_DK_TOT] * (GLA_DK ** -0.5)
    k_ref[...] = proj[:, GLA_DK_TOT:2 * GLA_DK_TOT]
    v_ref[...] = proj[:, 2 * GLA_DK_TOT:2 * GLA_DK_TOT + GLA_DV_TOT]
    go_ref[...] = proj[:, 2 * GLA_DK_TOT + GLA_DV_TOT:]
    dec_ref[...] = jnp.exp(_gla_gate(hb, wlr_ref, wgk_ref, bgk_ref))


GLA1_TOK = 16


def _gla1_state_body(st_ref, qc_ref, kc_ref, dc_ref, v_ref, nst_ref, o_ref):
    v = v_ref[...]
    for i in range(GLA1_TOK):
        for hd in range(GLA_HEADS):
            vrow = v[i:i + 1, hd * GLA_DV:(hd + 1) * GLA_DV]
            s_new = dc_ref[hd][:, i:i + 1] * st_ref[i, hd] + kc_ref[hd][:, i:i + 1] * vrow
            nst_ref[i, hd] = s_new
            o_ref[i:i + 1, hd * GLA_DV:(hd + 1) * GLA_DV] = jnp.sum(
                qc_ref[hd][:, i:i + 1] * s_new, axis=0, keepdims=True)


def _gla1_out_body(x_ref, o_ref, go_ref, mod_ref, ng_ref, gng_ref, wout_ref, rw_ref, *rest):
    x1_ref, h_ref, lg_ref = rest[-3:]
    _, _, g1, sh2, sc2, _ = _mod_slices(mod_ref)
    ng = ng_ref[...]
    y = jnp.dot(_gla_out(o_ref[...], go_ref[...], gng_ref[...]), wout_ref[...], preferred_element_type=F32)
    x1 = x_ref[...] + g1 * _rms(y, ng[1:2])
    x1_ref[...] = x1
    _ffn_prep(x1, ng, sh2, sc2, rw_ref, h_ref, lg_ref)


def _gla_mixer_one(x2d, mod2, state, ng, wqkvg, wlr, wgk, bgk, gng, wout, rw_t, into=None):
    n = x2d.shape[0]
    consts = (ng, wqkvg, wlr, wgk, bgk)
    q, k, v, go, dec = pl.pallas_call(
        _gla1_proj_body,
        in_specs=[_const_spec(a.shape) for a in (x2d, mod2) + consts],
        out_specs=[_const_spec((n, GLA_DK_TOT)), _const_spec((n, GLA_DK_TOT)), _const_spec((n, GLA_DV_TOT)),
                   _const_spec((n, GLA_DV_TOT)), _const_spec((n, GLA_DK_TOT))],
        out_shape=[jax.ShapeDtypeStruct((n, GLA_DK_TOT), F32), jax.ShapeDtypeStruct((n, GLA_DK_TOT), F32),
                   jax.ShapeDtypeStruct((n, GLA_DV_TOT), F32), jax.ShapeDtypeStruct((n, GLA_DV_TOT), F32),
                   jax.ShapeDtypeStruct((n, GLA_DK_TOT), F32)],
        grid=(1,),
        compiler_params=_cparams("arbitrary"),
        name="gla1_proj",
    )(x2d, mod2, *consts)

    def cols(a):
        return a.reshape(n // GLA1_TOK, GLA1_TOK, GLA_HEADS, GLA_DK).transpose(0, 2, 3, 1)

    col_spec = pl.BlockSpec((None, GLA_HEADS, GLA_DK, GLA1_TOK), lambda i: (i, 0, 0, 0))
    st_spec = pl.BlockSpec((GLA1_TOK, GLA_HEADS, GLA_DK, GLA_DV), lambda i: (i, 0, 0, 0))
    new_state, o = pl.pallas_call(
        _gla1_state_body,
        grid=(n // GLA1_TOK,),
        in_specs=[st_spec, col_spec, col_spec, col_spec, pl.BlockSpec((GLA1_TOK, GLA_DV_TOT), lambda i: (i, 0))],
        out_specs=[st_spec, pl.BlockSpec((GLA1_TOK, GLA_DV_TOT), lambda i: (i, 0))],
        out_shape=[jax.ShapeDtypeStruct(state.shape, F32), jax.ShapeDtypeStruct((n, GLA_DV_TOT), F32)],
        compiler_params=_cparams("parallel"),
        name="gla1_state",
    )(state, cols(q), cols(k), cols(dec), v)

    consts = (mod2, ng, gng, wout, rw_t)
    ffn_shapes, alias_bufs, oblk = _ffn_out(n, None, into)
    n_in = 3 + len(consts)
    x1, h, lg = pl.pallas_call(
        _gla1_out_body,
        grid=(1,),
        in_specs=[_const_spec(a.shape) for a in (x2d, o, go) + consts] +
                 [pl.BlockSpec(memory_space=pl.ANY)] * len(alias_bufs),
        out_specs=[_const_spec((n, D)), pl.BlockSpec((n, DP), lambda i: (oblk, 0)),
                   pl.BlockSpec((N_EXPERTS, n), lambda i: (0, oblk))],
        out_shape=[jax.ShapeDtypeStruct((n, D), F32)] + ffn_shapes,
        input_output_aliases={n_in + i: 1 + i for i in range(len(alias_bufs))},
        compiler_params=_cparams("arbitrary"),
        name="gla1_out",
    )(x2d, o, go, *consts, *alias_bufs)
    return x1, h, lg, new_state


def _moe_routed(h, lg, n, router_bias, wg, wu, wd):
    n_pad = h.shape[0]
    eid, rank, wts, counts = _router(lg, router_bias, n)
    tile_count = ((counts[:, 0] + EXPERT_TILE - 1) // EXPERT_TILE).astype(jnp.int32)
    tile_end = jnp.cumsum(tile_count).astype(jnp.int32)
    tile_first = tile_end - tile_count
    off = tile_first * EXPERT_TILE
    p_alloc = TOP_K * n_pad + N_EXPERTS * EXPERT_TILE
    dest = _dest(off, eid, rank, n, p_alloc - 1)
    dest_w = dest.reshape(TOP_K, n_pad // DISPATCH_W, DISPATCH_W).transpose(1, 0, 2)
    xs = _sc_dispatch(h, dest_w, p_alloc)
    ys = _experts(xs, tile_first, tile_count, tile_end[-1:], wg, wu, wd)
    dest_tm = dest.T.reshape(n_pad // SUM_W, SUM_PARTS, SUM_W * TOP_K // SUM_PARTS)
    w_lanes = jnp.repeat(wts.T, SC_LANES, axis=1)
    return _sc_gather_sum(ys, dest_tm, w_lanes)


def kernel(x_prompt, x_sample, state_gla, c_prompt, c_sample, norm_g, ada_w, ada_b, gm_w_in, gm_b_in,
           gm_ln_g, gm_ln_b, gm_w_s, gm_b_s, gm_w_out, gla_w_in, gla_w_gk, gla_b_gk, gla_norm_g,
           gla_w_out, router_w, router_bias, exp_w_gate, exp_w_up, exp_w_down, sh_w_gate, sh_w_up,
           sh_w_down):
    batch, seq, _ = x_prompt.shape
    n_s = x_sample.shape[0]
    n_p = batch * seq
    tpb = seq // MIX_TILE
    xp = x_prompt.reshape(n_p, D)
    xs = x_sample.reshape(n_s, D)

    mod = _ada(jnp.concatenate([c_prompt, c_sample], axis=0), ada_w, ada_b)
    mod_p = [mod[i, :batch].reshape(batch, 1, 6 * D) for i in range(2)]
    mod_s = [mod[i, batch:] for i in range(2)]
    rw_t = [jnp.concatenate(_split3(router_w[i].T), axis=0) for i in range(2)]

    ws_causal = jnp.tril(gm_w_s[0]).astype(BF16)
    bs_cols = gm_b_s[0].T
    eye = jnp.eye(GM_CHUNK, dtype=F32)
    ws_first = (gm_w_s[0][:, 0, 0][:, None, None] * eye).astype(BF16)
    bs_first = jnp.broadcast_to(gm_b_s[0][:, 0][None, :], (GM_CHUNK, GM_GROUPS))
    gm_args = (norm_g[0], gm_w_in[0].astype(BF16), gm_b_in[0].reshape(1, -1), gm_ln_g[0].reshape(1, -1),
               gm_ln_b[0].reshape(1, -1))
    wout0 = gm_w_out[0].astype(BF16)
    shared = [(sh_w_gate[i].astype(BF16), sh_w_up[i].astype(BF16), sh_w_down[i].astype(BF16))
              for i in range(2)]
    n_qkvg = 2 * GLA_DK_TOT + 2 * GLA_DV_TOT
    wqkvg = gla_w_in[0][:, :n_qkvg].astype(BF16)
    wlr = jnp.pad(gla_w_in[0][:, n_qkvg:], ((0, 0), (0, LANES - GLA_GATE_RANK))).astype(BF16)
    wgk = jnp.pad(gla_w_gk[0], ((0, LANES - GLA_GATE_RANK), (0, 0))).astype(BF16)
    gla_args = (norm_g[1], wqkvg, wlr, wgk, gla_b_gk[0].reshape(1, -1), gla_norm_g[0].reshape(1, -1),
                gla_w_out[0].astype(BF16), rw_t[1])
    experts_f32 = (exp_w_gate, exp_w_up, exp_w_down)

    half = batch // 2
    streams = [(0, half, False), (half, batch - half, True)]
    st = [dict() for _ in streams]

    experts = []
    for layer, (s, (b0, nb, with_new)) in enumerate(zip(st, streams)):
        s["mod_p"] = [mod_p[i][b0:b0 + nb] for i in range(2)]
        s["n"] = nb * seq
        s["n_all"] = s["n"] + (n_s if with_new else 0)
        s["n_pad"] = -(-s["n_all"] // TOKEN_PAD) * TOKEN_PAD
        s["x1p"], s["h"], s["lg"], *w16 = _gmlp_mixer(
            xp, b0 * tpb, s["n"], s["mod_p"][0], False, MIX_TILE, tpb, *gm_args, ws_causal, bs_cols, wout0,
            rw_t[0], emit_v=False, cast_w=(layer, *experts_f32), ffn_rows=s["n_pad"])
        experts.append(w16)
        if with_new:
            s["x1s"], s["h"], s["lg"], v_rows = _gmlp_mixer(
                xs, 0, n_s, mod_s[0], True, n_s, 1, *gm_args, ws_first, bs_first, wout0, rw_t[0], emit_v=True,
                into=(s["h"], s["lg"], s["n"] // n_s))
    for s, (b0, nb, with_new) in zip(st, streams):
        s["routed0"] = _moe_routed(s["h"], s["lg"], s["n_all"], router_bias[0], *experts[0])
        if with_new:
            s["x2s"] = _combine(s["x1s"], s["routed0"], s["h"], s["n"] // n_s, mod_s[0], True, n_s, 1,
                                norm_g[0], *shared[0])
    for s, (b0, nb, with_new) in zip(st, streams):
        s["x3p"], h1, lg1, s["st_p"] = _gla_mixer(s["x1p"], s["routed0"], s["h"], s["mod_p"][0], norm_g[0],
                                                  shared[0], s["mod_p"][1], nb, seq, *gla_args,
                                                  ffn_rows=s["n_pad"])
        if with_new:
            s["x3s"], h1, lg1, st_s = _gla_mixer_one(s["x2s"], mod_s[1], state_gla[:, 0], *gla_args,
                                                     into=(h1, lg1, s["n"] // n_s))
        s["h"], s["lg"] = h1, lg1
    y_prompt = None
    for s, (b0, nb, with_new) in zip(st, streams):
        routed = _moe_routed(s["h"], s["lg"], s["n_all"], router_bias[1], *experts[1])
        y_prompt = _combine(s["x3p"], routed, s["h"], 0, s["mod_p"][1], False, MIX_TILE, tpb, norm_g[1],
                            *shared[1], out_rows=n_p, out_blk0=b0 * tpb, out_buf=y_prompt)
        if with_new:
            y_new = _combine(s["x3s"], routed, s["h"], s["n"] // n_s, mod_s[1], True, n_s, 1, norm_g[1],
                             *shared[1])
    st_p = jnp.concatenate([s["st_p"] for s in st], axis=0)

    return (y_prompt.reshape(batch, seq, D), y_new.reshape(n_s, 1, D), st_p[:, None], st_s[:, None],
            v_rows.reshape(n_s, 1, 1, GM_HALF))
```

```python
import functools
import math

import jax
import jax.numpy as jnp
from jax import lax
from jax.experimental import pallas as pl
from jax.experimental.pallas import tpu as pltpu
from jax.experimental.pallas import tpu_sc as plsc

F32 = jnp.float32
BF16 = jnp.bfloat16

D = 1024
DP = D // 2
GM_CHUNK = 128
GM_HALF = 2 * D
GM_GROUPS = 8
GM_GROUP_DIM = GM_HALF // GM_GROUPS
GLA_HEADS = 4
GLA_DK = 128
GLA_DV = 256
GLA_DK_TOT = GLA_HEADS * GLA_DK
GLA_DV_TOT = GLA_HEADS * GLA_DV
GLA_GATE_RANK = 16
GLA_GATE_NORMALIZER = 16.0
GLA_CHUNK = 64
N_EXPERTS = 64
TOP_K = 8
N_EXPERT_GROUPS = 8
GROUP_SIZE = N_EXPERTS // N_EXPERT_GROUPS
TOPK_GROUPS = 4
EXPERT_DIM = D // 4
ROUTED_SCALE = 2.5
NORM_EPS = 1e-6
LN_EPS = 1e-5

LANES = 128
VMEM_LIMIT = 56 * 1024 * 1024

MIX_TILE = 256
GLA_TILE = 512
GLA_CUM_BLOCK = 256
GM_COL_BLOCK = 512
ROUTER_TILE = 1024
EXPERT_TILE = 544
EXPERT_X_SLOTS = 6
EXPERT_AHEAD = EXPERT_X_SLOTS - 2
EXPERT_Y_SLOTS = 4
SC_WORKERS = 32
DISPATCH_W = 32
SC_LANES = 16
SUM_W = 16
SUM_PARTS = 4
SUM_UNROLL = 4
TOKEN_PAD = SC_WORKERS * DISPATCH_W


def _cparams(*sem):
    return pltpu.CompilerParams(dimension_semantics=sem, vmem_limit_bytes=VMEM_LIMIT)


def _rms(x, g):
    return x * lax.rsqrt(jnp.mean(x * x, axis=-1, keepdims=True) + NORM_EPS) * g


def _silu(x):
    return x * (1.0 / (1.0 + jnp.exp(-x)))


def _gelu(x):
    return 0.5 * x * (1.0 + lax.erf(x * (1.0 / math.sqrt(2.0))))


def _bdot(a, b):
    return jnp.dot(a.astype(BF16), b.astype(BF16), preferred_element_type=F32)


def _dot_nt(a, b, precision=None):
    return lax.dot_general(a, b, (((1,), (1,)), ((), ())), preferred_element_type=F32,
                           precision=precision)


def _mod_slices(mod_ref):
    return [mod_ref[:, i * D:(i + 1) * D] for i in range(6)]


HI_HALF = -65536


def _pack_rows(x):
    lo = lax.bitcast_convert_type(x[:, :DP].astype(BF16).astype(F32), jnp.int32)
    hi = lax.bitcast_convert_type(x[:, DP:].astype(BF16).astype(F32), jnp.int32)
    return lax.shift_right_logical(lo, 16) | (hi & HI_HALF)


def _unpack_rows(p):
    lo = lax.bitcast_convert_type(lax.shift_left(p, 16), F32)
    hi = lax.bitcast_convert_type(p & HI_HALF, F32)
    return lo, hi


def _ffn_prep(x1, ng, sh2, sc2, rw_ref, h_ref, lg_ref, rows=slice(None)):
    hffn = _rms(x1, ng[2:3]) * (1.0 + sc2) + sh2
    h_ref[rows, :] = _pack_rows(hffn)
    lg3 = _dot_nt(rw_ref[...], hffn.astype(BF16))
    lg_ref[:, rows] = lg3[:N_EXPERTS] + lg3[N_EXPERTS:2 * N_EXPERTS] + lg3[2 * N_EXPERTS:]


def _ada_body(c_ref, w_ref, b_ref, o_ref):
    c = c_ref[...]
    o_ref[...] = _bdot(_silu(c), w_ref[...]) + b_ref[...]


def _ada(c, ada_w, ada_b):
    n = c.shape[0]
    depth = ada_w.shape[0]
    tn = 1536
    return pl.pallas_call(
        _ada_body,
        grid=(depth, 6 * D // tn),
        in_specs=[pl.BlockSpec((n, D), lambda l, j: (0, 0)),
                  pl.BlockSpec((None, D, tn), lambda l, j: (l, 0, j)),
                  pl.BlockSpec((None, 1, tn), lambda l, j: (l, 0, j))],
        out_specs=pl.BlockSpec((None, n, tn), lambda l, j: (l, 0, j)),
        out_shape=jax.ShapeDtypeStruct((depth, n, 6 * D), F32),
        compiler_params=_cparams("parallel", "parallel"),
        name="ada_mod",
    )(c, ada_w, ada_b.reshape(depth, 1, 6 * D))


def _mod_spec(per_row, tt, tiles_per_batch):
    if per_row:
        return pl.BlockSpec((tt, 6 * D), lambda i: (i, 0))
    return pl.BlockSpec((None, 1, 6 * D), lambda i: (i // tiles_per_batch, 0, 0))


def _const_spec(shape):
    zeros = (0,) * len(shape)
    return pl.BlockSpec(shape, lambda *_: zeros)


def _gmlp_body(x_ref, mod_ref, ng_ref, win_ref, bin_ref, lng_ref, lnb_ref, ws_ref, bs_ref, wout_ref,
               rw_ref, *rest, n_chunks, emit_v, cast_w, n_alias):
    rest = list(rest)
    w32_refs = [rest.pop(0) for _ in range(3)] if cast_w else []
    rest = rest[n_alias:]
    x1_ref, h_ref, lg_ref = rest[:3]
    rest = rest[3:]
    v_ref = rest.pop(0) if emit_v else None
    w16_refs = [rest.pop(0) for _ in range(3)] if cast_w else []
    um_ref, z_ref = rest
    for src, dst in zip(w32_refs, w16_refs):
        dst[...] = src[...].astype(BF16)
    sh1, sc1, g1, sh2, sc2, _ = _mod_slices(mod_ref)
    ng = ng_ref[...]
    x = x_ref[...]
    hb = (_rms(x, ng[0:1]) * (1.0 + sc1) + sh1).astype(BF16)
    for cb in range(2 * GM_HALF // GM_COL_BLOCK):
        cols = slice(cb * GM_COL_BLOCK, (cb + 1) * GM_COL_BLOCK)
        z_ref[:, cols] = _gelu(jnp.dot(hb, win_ref[:, cols], preferred_element_type=F32) + bin_ref[:, cols])
    u = z_ref[:, :GM_HALF]
    v = z_ref[:, GM_HALF:]
    mu = jnp.mean(v, axis=-1, keepdims=True)
    vc = v - mu
    var = jnp.mean(vc * vc, axis=-1, keepdims=True)
    v = vc * lax.rsqrt(var + LN_EPS) * lng_ref[...] + lnb_ref[...]
    if emit_v:
        v_ref[...] = v
    vb = v.astype(BF16)
    for c in range(n_chunks):
        rows = slice(c * GM_CHUNK, (c + 1) * GM_CHUNK)
        for g in range(GM_GROUPS):
            cols = slice(g * GM_GROUP_DIM, (g + 1) * GM_GROUP_DIM)
            mixed = jnp.dot(ws_ref[g], vb[rows, cols], preferred_element_type=F32) + bs_ref[:, g:g + 1]
            um_ref[rows, cols] = (u[rows, cols] * mixed).astype(BF16)
    y = jnp.dot(um_ref[...], wout_ref[...], preferred_element_type=F32)
    x1 = x + g1 * _rms(y, ng[1:2])
    x1_ref[...] = x1
    _ffn_prep(x1, ng, sh2, sc2, rw_ref, h_ref, lg_ref)


def _ffn_out(n, ffn_rows, into):
    rows = ffn_rows or n
    oblk = 0
    bufs = []
    if into is not None:
        *bufs, oblk = into
        rows = bufs[0].shape[0]
    elif rows != n:
        bufs = [jnp.zeros((rows, DP), jnp.int32), jnp.zeros((N_EXPERTS, rows), F32)]
    shapes = [jax.ShapeDtypeStruct((rows, DP), jnp.int32), jax.ShapeDtypeStruct((N_EXPERTS, rows), F32)]
    return shapes, bufs, oblk


def _gmlp_mixer(x2d, blk0, n, mod, per_row, tt, tiles_per_batch, ng, win, b_in, ln_g, ln_b, ws, bs, wout,
                rw_t, emit_v, cast_w=None, ffn_rows=None, into=None):
    steps = n // tt
    ffn_shapes, alias_bufs, oblk = _ffn_out(n, ffn_rows, into)
    out_shape = [jax.ShapeDtypeStruct((n, D), F32)] + ffn_shapes
    out_specs = [pl.BlockSpec((tt, D), lambda i: (i, 0)), pl.BlockSpec((tt, DP), lambda i: (i + oblk, 0)),
                 pl.BlockSpec((N_EXPERTS, tt), lambda i: (0, i + oblk))]
    if emit_v:
        out_shape.append(jax.ShapeDtypeStruct((n, GM_HALF), F32))
        out_specs.append(pl.BlockSpec((tt, GM_HALF), lambda i: (i, 0)))

    def one_buffer(a):
        zeros = (0,) * a.ndim
        return pl.BlockSpec(a.shape, lambda *_: zeros, pipeline_mode=pl.Buffered(1))

    consts = (ng, win, b_in, ln_g, ln_b, ws, bs, wout, rw_t)
    in_specs = [pl.BlockSpec((tt, D), lambda i: (i + blk0, 0)), _mod_spec(per_row, tt, tiles_per_batch)]
    in_specs += [one_buffer(a) for a in consts]
    args = [x2d, mod, *consts]
    if cast_w is not None:
        layer, *w_all = cast_w
        per_step = N_EXPERTS // steps
        for w in w_all:
            blk = (None, per_step) + w.shape[2:]
            in_specs.append(pl.BlockSpec(blk, lambda i: (layer, i, 0, 0)))
            out_specs.append(pl.BlockSpec(blk[1:], lambda i: (i, 0, 0)))
            out_shape.append(jax.ShapeDtypeStruct(w.shape[1:], BF16))
            args.append(w)
    aliases = {len(args) + i: 1 + i for i in range(len(alias_bufs))}
    in_specs += [pl.BlockSpec(memory_space=pl.ANY)] * len(alias_bufs)
    args += alias_bufs
    return pl.pallas_call(
        functools.partial(_gmlp_body, n_chunks=tt // GM_CHUNK, emit_v=emit_v, cast_w=cast_w is not None,
                          n_alias=len(alias_bufs)),
        grid=(steps,),
        in_specs=in_specs,
        out_specs=out_specs,
        out_shape=out_shape,
        input_output_aliases=aliases,
        scratch_shapes=[pltpu.VMEM((tt, GM_HALF), BF16), pltpu.VMEM((tt, 2 * GM_HALF), F32)],
        compiler_params=_cparams("parallel"),
        name="gmlp_mixer_rows" if per_row else "gmlp_mixer",
    )(*args)


def _combine_body(x_ref, y_ref, h_ref, mod_ref, ng_ref, swg_ref, swu_ref, swd_ref, *rest):
    o_ref = rest[-1]
    o_ref[...] = _channel_mix_residual(x_ref[...], y_ref[...], h_ref[...], mod_ref[:, 5 * D:6 * D],
                                       ng_ref[3:4, :], swg_ref, swu_ref, swd_ref)


def _combine(x2d, routed, hp, blk0, mod, per_row, tt, tiles_per_batch, ng, swg, swu, swd,
             out_rows=None, out_blk0=0, out_buf=None):
    n = x2d.shape[0]
    in_specs = [pl.BlockSpec((tt, D), lambda i: (i, 0)),
                pl.BlockSpec((tt, D), lambda i: (i + blk0, 0)),
                pl.BlockSpec((tt, DP), lambda i: (i + blk0, 0)),
                _mod_spec(per_row, tt, tiles_per_batch),
                _const_spec(ng.shape), _const_spec(swg.shape), _const_spec(swu.shape),
                _const_spec(swd.shape)]
    args = [x2d, routed, hp, mod, ng, swg, swu, swd]
    aliases = {}
    if out_buf is not None:
        in_specs.append(pl.BlockSpec(memory_space=pl.ANY))
        aliases = {len(args): 0}
        args.append(out_buf)
    return pl.pallas_call(
        _combine_body,
        grid=(n // tt,),
        in_specs=in_specs,
        out_specs=pl.BlockSpec((tt, D), lambda i: (i + out_blk0, 0)),
        out_shape=jax.ShapeDtypeStruct((out_rows or n, D), F32),
        input_output_aliases=aliases,
        compiler_params=_cparams("parallel"),
        name="combine_rows" if per_row else "combine",
    )(*args)


def _router_body(lg_ref, bias_ref, tri_ref, eid_ref, rank_ref, wts_ref, cnt_ref, carry_ref, *, n_real):
    step = pl.program_id(0)

    @pl.when(step == 0)
    def _():
        carry_ref[...] = jnp.zeros_like(carry_ref)

    lg = lg_ref[...]
    tn = lg.shape[1]
    real = (step * tn + lax.broadcasted_iota(jnp.int32, (1, tn), 1)) < n_real
    lg = jnp.where(real, lg, 0.0)
    scores = 1.0 / (1.0 + jnp.exp(-lg))
    sel = scores + bias_ref[...]
    neg = -jnp.inf
    sub8 = lax.broadcasted_iota(jnp.int32, (GROUP_SIZE, tn), 0)
    gsub = lax.broadcasted_iota(jnp.int32, (N_EXPERT_GROUPS, tn), 0)
    gs = jnp.zeros((N_EXPERT_GROUPS, tn), F32)
    for g in range(N_EXPERT_GROUPS):
        blk = sel[g * GROUP_SIZE:(g + 1) * GROUP_SIZE, :]
        m1 = jnp.max(blk, axis=0, keepdims=True)
        i1 = jnp.min(jnp.where(blk == m1, sub8, GROUP_SIZE), axis=0, keepdims=True)
        m2 = jnp.max(jnp.where(sub8 == i1, neg, blk), axis=0, keepdims=True)
        gs = jnp.where(gsub == g, m1 + m2, gs)
    gmask = jnp.zeros((N_EXPERT_GROUPS, tn), jnp.bool_)
    for _ in range(TOPK_GROUPS):
        m = jnp.max(gs, axis=0, keepdims=True)
        i = jnp.min(jnp.where(gs == m, gsub, N_EXPERT_GROUPS), axis=0, keepdims=True)
        hit = gsub == i
        gmask = jnp.logical_or(gmask, hit)
        gs = jnp.where(hit, neg, gs)
    gmaskf = gmask.astype(F32)
    blocks = []
    for g in range(N_EXPERT_GROUPS):
        keep = jnp.broadcast_to(gmaskf[g:g + 1, :], (GROUP_SIZE, tn)) > 0.5
        blocks.append(jnp.where(keep, sel[g * GROUP_SIZE:(g + 1) * GROUP_SIZE, :], neg))
    msel = jnp.concatenate(blocks, axis=0)
    esub = lax.broadcasted_iota(jnp.int32, (N_EXPERTS, tn), 0)
    chosen = jnp.zeros((N_EXPERTS, tn), jnp.bool_)
    picks = []
    for _ in range(TOP_K):
        m = jnp.max(msel, axis=0, keepdims=True)
        i = jnp.min(jnp.where(msel == m, esub, N_EXPERTS), axis=0, keepdims=True)
        hit = esub == i
        picks.append(i)
        chosen = jnp.logical_or(chosen, hit)
        msel = jnp.where(hit, neg, msel)
    w = jnp.where(chosen, scores, 0.0)
    w = w / jnp.sum(w, axis=0, keepdims=True) * ROUTED_SCALE
    counted = jnp.where(jnp.logical_and(chosen, real), 1.0, 0.0)
    incl = jnp.dot(counted.astype(BF16), tri_ref[...], preferred_element_type=F32)
    rank_full = carry_ref[:, 0:1] + incl - 1.0
    ksub = lax.broadcasted_iota(jnp.int32, (TOP_K, tn), 0)
    eid = jnp.zeros((TOP_K, tn), jnp.int32)
    rank = jnp.zeros((TOP_K, tn), F32)
    wts = jnp.zeros((TOP_K, tn), F32)
    for k in range(TOP_K):
        hit = esub == picks[k]
        eid = jnp.where(ksub == k, picks[k], eid)
        rank = jnp.where(ksub == k, jnp.sum(jnp.where(hit, rank_full, 0.0), axis=0, keepdims=True), rank)
        wts = jnp.where(ksub == k, jnp.sum(jnp.where(hit, w, 0.0), axis=0, keepdims=True), wts)
    eid_ref[...] = eid
    rank_ref[...] = rank.astype(jnp.int32)
    wts_ref[...] = wts
    carry = carry_ref[...] + incl[:, tn - 1:tn]
    carry_ref[...] = carry
    cnt_ref[...] = carry.astype(jnp.int32)


def _router(lg_t, bias, n_real):
    n = lg_t.shape[1]
    tn = ROUTER_TILE
    idx = jnp.arange(tn)
    tri = (idx[:, None] <= idx[None, :]).astype(BF16)
    kspec = pl.BlockSpec((TOP_K, tn), lambda i: (0, i))
    return pl.pallas_call(
        functools.partial(_router_body, n_real=n_real),
        grid=(n // tn,),
        in_specs=[pl.BlockSpec((N_EXPERTS, tn), lambda i: (0, i)), _const_spec((N_EXPERTS, 1)),
                  _const_spec((tn, tn))],
        out_specs=[kspec, kspec, kspec, _const_spec((N_EXPERTS, LANES))],
        out_shape=[jax.ShapeDtypeStruct((TOP_K, n), jnp.int32), jax.ShapeDtypeStruct((TOP_K, n), jnp.int32),
                   jax.ShapeDtypeStruct((TOP_K, n), F32), jax.ShapeDtypeStruct((N_EXPERTS, LANES), jnp.int32)],
        scratch_shapes=[pltpu.VMEM((N_EXPERTS, LANES), F32)],
        compiler_params=_cparams("arbitrary"),
        name="router",
    )(lg_t, bias.reshape(N_EXPERTS, 1), tri)


def _dest_body(off_ref, eid_ref, rank_ref, dest_ref, *, n_real, last_row):
    eid = eid_ref[...]
    base = jnp.zeros(eid.shape, jnp.int32)
    for e in range(N_EXPERTS):
        base = jnp.where(eid == e, off_ref[e], base)
    tok = lax.broadcasted_iota(jnp.int32, eid.shape, 1)
    slot = lax.broadcasted_iota(jnp.int32, eid.shape, 0)
    unused = last_row - ((tok - n_real) * TOP_K + slot)
    dest_ref[...] = jnp.where(tok < n_real, base + rank_ref[...], unused)


def _dest(off, eid, rank, n_real, last_row):
    spec = pl.BlockSpec(eid.shape, lambda i, off_ref: (0, 0))
    return pl.pallas_call(
        functools.partial(_dest_body, n_real=n_real, last_row=last_row),
        grid_spec=pltpu.PrefetchScalarGridSpec(num_scalar_prefetch=1, grid=(1,), in_specs=[spec, spec],
                                               out_specs=spec),
        out_shape=jax.ShapeDtypeStruct(eid.shape, jnp.int32),
        compiler_params=_cparams("arbitrary"),
        name="dest_rows",
    )(off, eid, rank)


def _sc_mesh():
    return plsc.VectorSubcoreMesh(core_axis_name="core", subcore_axis_name="subcore")


def _sc_dispatch(hp, dest_w, p_alloc):
    n = hp.shape[0]
    w = dest_w.shape[2]

    @functools.partial(pl.kernel, out_type=jax.ShapeDtypeStruct((p_alloc, DP), jnp.int32), mesh=_sc_mesh(),
                       name="sc_dispatch")
    def run(hp_hbm, dest_hbm, xs_hbm):
        def body(x_vmem, i_vmem):
            for k in range(TOP_K):
                pltpu.sync_copy(x_vmem, xs_hbm.at[i_vmem.at[k]])

        pltpu.emit_pipeline(
            body,
            grid=(n // w,),
            in_specs=[pl.BlockSpec((w, DP), lambda i: (i, 0)),
                      pl.BlockSpec((None, TOP_K, w), lambda i: (i, 0, 0))],
            out_specs=[],
            core_axis_name=("core", "subcore"),
            dimension_semantics=(pltpu.PARALLEL,),
        )(hp_hbm, dest_hbm)

    return run(hp, dest_w)


def _sc_gather_sum(ys, dest_tm, w_lanes):
    n_win, parts, pk = dest_tm.shape
    w = parts * pk // TOP_K
    wp = w // parts
    n_vec = DP // SC_LANES

    @functools.partial(pl.kernel, out_type=jax.ShapeDtypeStruct((n_win * w, D), F32), mesh=_sc_mesh(),
                       scratch_types=[pltpu.VMEM((parts, pk, DP), jnp.int32), pltpu.SemaphoreType.DMA((parts,))],
                       compiler_params=pltpu.CompilerParams(needs_layout_passes=False), name="sc_gather_sum")
    def run(ys_hbm, dest_hbm, w_hbm, o_hbm, rows_v, sems):
        def body(i_vmem, w_vmem, o_vmem):
            copies = [pltpu.async_copy(ys_hbm.at[i_vmem.at[p]], rows_v.at[p], sems.at[p]) for p in range(parts)]
            for p in range(parts):
                copies[p].wait()

                @pl.loop(0, wp)
                def _(t):
                    tok = p * wp + t
                    wv = [w_vmem[tok, pl.ds(k * SC_LANES, SC_LANES)] for k in range(TOP_K)]

                    @plsc.parallel_loop(0, n_vec, unroll=SUM_UNROLL)
                    def _(j):
                        col = j * SC_LANES
                        lo = jnp.zeros((SC_LANES,), F32)
                        hi = jnp.zeros((SC_LANES,), F32)
                        for k in range(TOP_K):
                            word = rows_v[p, t * TOP_K + k, pl.ds(col, SC_LANES)]
                            lo = lo + wv[k] * plsc.bitcast(lax.shift_left(word, 16), F32)
                            hi = hi + wv[k] * plsc.bitcast(word & HI_HALF, F32)
                        o_vmem[tok, pl.ds(col, SC_LANES)] = lo
                        o_vmem[tok, pl.ds(DP + col, SC_LANES)] = hi

        pltpu.emit_pipeline(
            body,
            grid=(n_win,),
            in_specs=[pl.BlockSpec((None, parts, pk), lambda i: (i, 0, 0)),
                      pl.BlockSpec((w, TOP_K * SC_LANES), lambda i: (i, 0))],
            out_specs=[pl.BlockSpec((w, D), lambda i: (i, 0))],
            core_axis_name=("core", "subcore"),
            dimension_semantics=(pltpu.PARALLEL,),
        )(dest_hbm, w_hbm, o_hbm)

    return run(ys, dest_tm, w_lanes)


def _expert_body(first_ref, cnt_ref, nused_ref, xs_hbm, wg_s, wu_s, wd_s, ys_hbm, xbuf, ybuf, xsem, ysem):
    e = pl.program_id(0)
    n_used = nused_ref[0]

    def load(g):
        rows = pl.ds(pl.multiple_of(g * EXPERT_TILE, EXPERT_TILE), EXPERT_TILE)
        slot = g % EXPERT_X_SLOTS
        return pltpu.make_async_copy(xs_hbm.at[rows], xbuf.at[slot], xsem.at[slot])

    def store(g):
        rows = pl.ds(pl.multiple_of(g * EXPERT_TILE, EXPERT_TILE), EXPERT_TILE)
        slot = g % EXPERT_Y_SLOTS
        return pltpu.make_async_copy(ybuf.at[slot], ys_hbm.at[rows], ysem.at[slot])

    @pl.when(e == 0)
    def _():
        for g in range(EXPERT_AHEAD):
            @pl.when(g < n_used)
            def _():
                load(g).start()

    first = first_ref[e]
    cnt = cnt_ref[e]

    def acquire(g):
        ahead = g + EXPERT_AHEAD

        @pl.when(ahead < n_used)
        def _():
            load(ahead).start()

        load(g).wait()

        @pl.when(g >= EXPERT_Y_SLOTS)
        def _():
            store(g - EXPERT_Y_SLOTS).wait()

    def compute(g):
        lo, hi = _unpack_rows(xbuf[g % EXPERT_X_SLOTS])
        lo = lo.astype(BF16)
        hi = hi.astype(BF16)

        def xdot(w_s):
            return (jnp.dot(lo, w_s[:DP, :], preferred_element_type=F32) +
                    jnp.dot(hi, w_s[DP:, :], preferred_element_type=F32))

        a = (_silu(xdot(wg_s)) * xdot(wu_s)).astype(BF16)
        ybuf[g % EXPERT_Y_SLOTS] = _pack_rows(jnp.dot(a, wd_s[...], preferred_element_type=F32))

    def pair(j, carry):
        g = first + 2 * j
        acquire(g)
        acquire(g + 1)
        compute(g)
        compute(g + 1)
        store(g).start()
        store(g + 1).start()
        return carry

    lax.fori_loop(0, cnt // 2, pair, 0)

    @pl.when(cnt % 2 == 1)
    def _():
        g = first + cnt - 1
        acquire(g)
        compute(g)
        store(g).start()

    @pl.when(e == N_EXPERTS - 1)
    def _():
        for k in range(EXPERT_Y_SLOTS):
            g = n_used - 1 - k

            @pl.when(g >= 0)
            def _():
                store(g).wait()


def _experts(xs, tile_first, tile_count, n_used, wg, wu, wd):
    def w_map(e, first, cnt, nu):
        return (e, 0, 0)

    return pl.pallas_call(
        _expert_body,
        grid_spec=pltpu.PrefetchScalarGridSpec(
            num_scalar_prefetch=3, grid=(N_EXPERTS,),
            in_specs=[pl.BlockSpec(memory_space=pl.ANY),
                      pl.BlockSpec((None, D, EXPERT_DIM), w_map),
                      pl.BlockSpec((None, D, EXPERT_DIM), w_map),
                      pl.BlockSpec((None, EXPERT_DIM, D), w_map)],
            out_specs=pl.BlockSpec(memory_space=pl.ANY),
            scratch_shapes=[pltpu.VMEM((EXPERT_X_SLOTS, EXPERT_TILE, DP), jnp.int32),
                            pltpu.VMEM((EXPERT_Y_SLOTS, EXPERT_TILE, DP), jnp.int32),
                            pltpu.SemaphoreType.DMA((EXPERT_X_SLOTS,)),
                            pltpu.SemaphoreType.DMA((EXPERT_Y_SLOTS,))]),
        out_shape=jax.ShapeDtypeStruct(xs.shape, jnp.int32),
        compiler_params=_cparams("arbitrary"),
        name="experts",
    )(tile_first, tile_count, n_used, xs, wg, wu, wd)


def _log_sigmoid(z):
    return jnp.minimum(z, 0.0) - jnp.log(1.0 + jnp.exp(-jnp.abs(z)))


def _gla_gate(hb, wlr_ref, wgk_ref, bgk_ref):
    lr = jnp.dot(hb, wlr_ref[...], preferred_element_type=F32)
    z = _bdot(lr, wgk_ref[...]) + bgk_ref[...]
    return _log_sigmoid(z) * (1.0 / GLA_GATE_NORMALIZER)


def _split3(a):
    hi = a.astype(BF16)
    r1 = a - hi.astype(F32)
    mid = r1.astype(BF16)
    lo = (r1 - mid.astype(F32)).astype(BF16)
    return hi, mid, lo


def _gla_out(o_ref_val, go, gng):
    parts = []
    for hd in range(GLA_HEADS):
        cols = slice(hd * GLA_DV, (hd + 1) * GLA_DV)
        parts.append((_rms(o_ref_val[:, cols], gng) * _silu(go[:, cols])).astype(BF16))
    return jnp.concatenate(parts, axis=1)


def _channel_mix_residual(x, routed, h_packed, g2, ng3, swg_ref, swu_ref, swd_ref):
    h_lo, h_hi = _unpack_rows(h_packed)
    h_lo = h_lo.astype(BF16)
    h_hi = h_hi.astype(BF16)

    def hdot(w_ref_):
        return (jnp.dot(h_lo, w_ref_[:DP, :], preferred_element_type=F32) +
                jnp.dot(h_hi, w_ref_[DP:, :], preferred_element_type=F32))

    hs = (_silu(hdot(swg_ref)) * hdot(swu_ref)).astype(BF16)
    y = jnp.dot(hs, swd_ref[...], preferred_element_type=F32) + routed
    return x + g2 * _rms(y, ng3)


def _gla_body(x_ref, y_ref, hprev_ref, modprev_ref, ngprev_ref, swg_ref, swu_ref, swd_ref,
              mod_ref, ng_ref, wqkvg_ref, wlr_ref, wgk_ref, bgk_ref, tril_ref, gng_ref, wout_ref,
              rw_ref, *rest, tt, n_alias):
    x1_ref, h_ref, lg_ref, st_ref, st_scr, o_scr, qd_scr, dst_scr = rest[n_alias:]
    j = pl.program_id(1)

    @pl.when(j == 0)
    def _():
        st_scr[...] = jnp.zeros_like(st_scr)

    x = _channel_mix_residual(x_ref[...], y_ref[...], hprev_ref[...], modprev_ref[:, 5 * D:6 * D],
                              ngprev_ref[3:4, :], swg_ref, swu_ref, swd_ref)
    sh1, sc1, g1, sh2, sc2, _ = _mod_slices(mod_ref)
    ng = ng_ref[...]
    hb = (_rms(x, ng[0:1]) * (1.0 + sc1) + sh1).astype(BF16)
    q = jnp.dot(hb, wqkvg_ref[:, :GLA_DK_TOT], preferred_element_type=F32) * (GLA_DK ** -0.5)
    k = jnp.dot(hb, wqkvg_ref[:, GLA_DK_TOT:2 * GLA_DK_TOT], preferred_element_type=F32)
    v = jnp.dot(hb, wqkvg_ref[:, 2 * GLA_DK_TOT:2 * GLA_DK_TOT + GLA_DV_TOT],
                preferred_element_type=F32).astype(BF16)
    log_a = _gla_gate(hb, wlr_ref, wgk_ref, bgk_ref)
    tril = tril_ref[...]
    parts = _split3(log_a)
    b = jnp.concatenate(
        [sum(jnp.dot(tril, p[r:r + GLA_CUM_BLOCK], preferred_element_type=F32) for p in parts)
         for r in range(0, tt, GLA_CUM_BLOCK)], axis=0)
    row = lax.broadcasted_iota(jnp.int32, (GLA_CHUNK, GLA_CHUNK), 0)
    col = lax.broadcasted_iota(jnp.int32, (GLA_CHUNK, GLA_CHUNK), 1)
    causal = row >= col
    n_chunks = tt // GLA_CHUNK
    for c in range(n_chunks):
        rows = slice(c * GLA_CHUNK, (c + 1) * GLA_CHUNK)
        last = (c + 1) * GLA_CHUNK - 1
        for hd in range(GLA_HEADS):
            kc = slice(hd * GLA_DK, (hd + 1) * GLA_DK)
            vc = slice(hd * GLA_DV, (hd + 1) * GLA_DV)
            bb = b[rows, kc]
            b_last = b[last:last + 1, kc]
            q_dec = (q[rows, kc] * jnp.exp(bb)).astype(BF16)
            k_inv = (k[rows, kc] * jnp.exp(-bb)).astype(BF16)
            k_end = (k[rows, kc] * jnp.exp(b_last - bb)).astype(BF16)
            att = jnp.where(causal, _dot_nt(q_dec, k_inv), 0.0).astype(BF16)
            qd_scr[rows, kc] = q_dec
            o_scr[rows, vc] = jnp.dot(att, v[rows, vc], preferred_element_type=F32)
            dst_scr[c * GLA_HEADS + hd] = lax.dot_general(
                v[rows, vc], k_end, (((0,), (0,)), ((), ())), preferred_element_type=F32)
    states = [st_scr[hd] for hd in range(GLA_HEADS)]
    for c in range(n_chunks):
        rows = slice(c * GLA_CHUNK, (c + 1) * GLA_CHUNK)
        last = (c + 1) * GLA_CHUNK - 1
        for hd in range(GLA_HEADS):
            kc = slice(hd * GLA_DK, (hd + 1) * GLA_DK)
            vc = slice(hd * GLA_DV, (hd + 1) * GLA_DV)
            o_scr[rows, vc] += _dot_nt(qd_scr[rows, kc], states[hd].astype(BF16))
            states[hd] = states[hd] * jnp.exp(b[last:last + 1, kc]) + dst_scr[c * GLA_HEADS + hd]
    for hd in range(GLA_HEADS):
        st_scr[hd] = states[hd]

    @pl.when(j == pl.num_programs(1) - 1)
    def _():
        for hd in range(GLA_HEADS):
            st_ref[hd] = st_scr[hd].T

    go = jnp.dot(hb, wqkvg_ref[:, 2 * GLA_DK_TOT + GLA_DV_TOT:], preferred_element_type=F32)
    y = jnp.dot(_gla_out(o_scr[...], go, gng_ref[...]), wout_ref[...], preferred_element_type=F32)
    x1 = x + g1 * _rms(y, ng[1:2])
    x1_ref[...] = x1
    _ffn_prep(x1, ng, sh2, sc2, rw_ref, h_ref, lg_ref)


def _gla_mixer(x2d, routed, hp, mod3_prev, ng_prev, shared_prev, mod3, batch, seq, ng, wqkvg, wlr, wgk, bgk,
               gng, wout, rw_t, ffn_rows=None):
    tt = GLA_TILE
    tpb = seq // tt
    n = x2d.shape[0]
    idx = jnp.arange(GLA_CUM_BLOCK)
    tril = ((idx[:, None] >= idx[None, :]) &
            (idx[:, None] // GLA_CHUNK == idx[None, :] // GLA_CHUNK)).astype(BF16)
    row_map = lambda b, j: (b * tpb + j, 0)
    mod_map = lambda b, j: (b, 0, 0)
    consts = (ng, wqkvg, wlr, wgk, bgk, tril, gng, wout, rw_t)
    prev_consts = (ng_prev,) + tuple(shared_prev)
    ffn_shapes, alias_bufs, _ = _ffn_out(n, ffn_rows, None)
    args = [x2d, routed, hp, mod3_prev, *prev_consts, mod3, *consts]
    return pl.pallas_call(
        functools.partial(_gla_body, tt=tt, n_alias=len(alias_bufs)),
        grid=(batch, tpb),
        in_specs=[pl.BlockSpec((tt, D), row_map), pl.BlockSpec((tt, D), row_map),
                  pl.BlockSpec((tt, DP), row_map), pl.BlockSpec((None, 1, 6 * D), mod_map)] +
                 [_const_spec(a.shape) for a in prev_consts] +
                 [pl.BlockSpec((None, 1, 6 * D), mod_map)] +
                 [_const_spec(a.shape) for a in consts] +
                 [pl.BlockSpec(memory_space=pl.ANY)] * len(alias_bufs),
        out_specs=[pl.BlockSpec((tt, D), row_map), pl.BlockSpec((tt, DP), row_map),
                   pl.BlockSpec((N_EXPERTS, tt), lambda b, j: (0, b * tpb + j)),
                   pl.BlockSpec((None, GLA_HEADS, GLA_DK, GLA_DV), lambda b, j: (b, 0, 0, 0))],
        out_shape=[jax.ShapeDtypeStruct((n, D), F32)] + ffn_shapes +
                  [jax.ShapeDtypeStruct((batch, GLA_HEADS, GLA_DK, GLA_DV), F32)],
        input_output_aliases={len(args) + i: 1 + i for i in range(len(alias_bufs))},
        scratch_shapes=[pltpu.VMEM((GLA_HEADS, GLA_DV, GLA_DK), F32),
                        pltpu.VMEM((tt, GLA_DV_TOT), F32),
                        pltpu.VMEM((tt, GLA_DK_TOT), BF16),
                        pltpu.VMEM((tt // GLA_CHUNK * GLA_HEADS, GLA_DV, GLA_DK), F32)],
        compiler_params=_cparams("parallel", "arbitrary"),
        name="gla_mixer",
    )(*args, *alias_bufs)


def _gla1_proj_body(x_ref, mod_ref, ng_ref, wqkvg_ref, wlr_ref, wgk_ref, bgk_ref,
                    q_ref, k_ref, v_ref, go_ref, dec_ref):
    sh1, sc1, _, _, _, _ = _mod_slices(mod_ref)
    ng = ng_ref[...]
    hb = (_rms(x_ref[...], ng[0:1]) * (1.0 + sc1) + sh1).astype(BF16)
    proj = jnp.dot(hb, wqkvg_ref[...], preferred_element_type=F32)
    q_ref[...] = proj[:, :GLA_DK_TOT] * (GLA_DK ** -0.5)
    k_ref[...] = proj[:, GLA_DK_TOT:2 * GLA_DK_TOT]
    v_ref[...] = proj[:, 2 * GLA_DK_TOT:2 * GLA_DK_TOT + GLA_DV_TOT]
    go_ref[...] = proj[:, 2 * GLA_DK_TOT + GLA_DV_TOT:]
    dec_ref[...] = jnp.exp(_gla_gate(hb, wlr_ref, wgk_ref, bgk_ref))


GLA1_TOK = 16


def _gla1_state_body(st_ref, qc_ref, kc_ref, dc_ref, v_ref, nst_ref, o_ref):
    v = v_ref[...]
    for i in range(GLA1_TOK):
        for hd in range(GLA_HEADS):
            vrow = v[i:i + 1, hd * GLA_DV:(hd + 1) * GLA_DV]
            s_new = dc_ref[hd][:, i:i + 1] * st_ref[i, hd] + kc_ref[hd][:, i:i + 1] * vrow
            nst_ref[i, hd] = s_new
            o_ref[i:i + 1, hd * GLA_DV:(hd + 1) * GLA_DV] = jnp.sum(
                qc_ref[hd][:, i:i + 1] * s_new, axis=0, keepdims=True)


def _gla1_out_body(x_ref, o_ref, go_ref, mod_ref, ng_ref, gng_ref, wout_ref, rw_ref, *rest):
    x1_ref, h_ref, lg_ref = rest[-3:]
    _, _, g1, sh2, sc2, _ = _mod_slices(mod_ref)
    ng = ng_ref[...]
    y = jnp.dot(_gla_out(o_ref[...], go_ref[...], gng_ref[...]), wout_ref[...], preferred_element_type=F32)
    x1 = x_ref[...] + g1 * _rms(y, ng[1:2])
    x1_ref[...] = x1
    _ffn_prep(x1, ng, sh2, sc2, rw_ref, h_ref, lg_ref)


def _gla_mixer_one(x2d, mod2, state, ng, wqkvg, wlr, wgk, bgk, gng, wout, rw_t, into=None):
    n = x2d.shape[0]
    consts = (ng, wqkvg, wlr, wgk, bgk)
    q, k, v, go, dec = pl.pallas_call(
        _gla1_proj_body,
        in_specs=[_const_spec(a.shape) for a in (x2d, mod2) + consts],
        out_specs=[_const_spec((n, GLA_DK_TOT)), _const_spec((n, GLA_DK_TOT)), _const_spec((n, GLA_DV_TOT)),
                   _const_spec((n, GLA_DV_TOT)), _const_spec((n, GLA_DK_TOT))],
        out_shape=[jax.ShapeDtypeStruct((n, GLA_DK_TOT), F32), jax.ShapeDtypeStruct((n, GLA_DK_TOT), F32),
                   jax.ShapeDtypeStruct((n, GLA_DV_TOT), F32), jax.ShapeDtypeStruct((n, GLA_DV_TOT), F32),
                   jax.ShapeDtypeStruct((n, GLA_DK_TOT), F32)],
        grid=(1,),
        compiler_params=_cparams("arbitrary"),
        name="gla1_proj",
    )(x2d, mod2, *consts)

    def cols(a):
        return a.reshape(n // GLA1_TOK, GLA1_TOK, GLA_HEADS, GLA_DK).transpose(0, 2, 3, 1)

    col_spec = pl.BlockSpec((None, GLA_HEADS, GLA_DK, GLA1_TOK), lambda i: (i, 0, 0, 0))
    st_spec = pl.BlockSpec((GLA1_TOK, GLA_HEADS, GLA_DK, GLA_DV), lambda i: (i, 0, 0, 0))
    new_state, o = pl.pallas_call(
        _gla1_state_body,
        grid=(n // GLA1_TOK,),
        in_specs=[st_spec, col_spec, col_spec, col_spec, pl.BlockSpec((GLA1_TOK, GLA_DV_TOT), lambda i: (i, 0))],
        out_specs=[st_spec, pl.BlockSpec((GLA1_TOK, GLA_DV_TOT), lambda i: (i, 0))],
        out_shape=[jax.ShapeDtypeStruct(state.shape, F32), jax.ShapeDtypeStruct((n, GLA_DV_TOT), F32)],
        compiler_params=_cparams("parallel"),
        name="gla1_state",
    )(state, cols(q), cols(k), cols(dec), v)

    consts = (mod2, ng, gng, wout, rw_t)
    ffn_shapes, alias_bufs, oblk = _ffn_out(n, None, into)
    n_in = 3 + len(consts)
    x1, h, lg = pl.pallas_call(
        _gla1_out_body,
        grid=(1,),
        in_specs=[_const_spec(a.shape) for a in (x2d, o, go) + consts] +
                 [pl.BlockSpec(memory_space=pl.ANY)] * len(alias_bufs),
        out_specs=[_const_spec((n, D)), pl.BlockSpec((n, DP), lambda i: (oblk, 0)),
                   pl.BlockSpec((N_EXPERTS, n), lambda i: (0, oblk))],
        out_shape=[jax.ShapeDtypeStruct((n, D), F32)] + ffn_shapes,
        input_output_aliases={n_in + i: 1 + i for i in range(len(alias_bufs))},
        compiler_params=_cparams("arbitrary"),
        name="gla1_out",
    )(x2d, o, go, *consts, *alias_bufs)
    return x1, h, lg, new_state


def _moe_routed(h, lg, n, router_bias, wg, wu, wd):
    n_pad = h.shape[0]
    eid, rank, wts, counts = _router(lg, router_bias, n)
    tile_count = ((counts[:, 0] + EXPERT_TILE - 1) // EXPERT_TILE).astype(jnp.int32)
    tile_end = jnp.cumsum(tile_count).astype(jnp.int32)
    tile_first = tile_end - tile_count
    off = tile_first * EXPERT_TILE
    p_alloc = TOP_K * n_pad + N_EXPERTS * EXPERT_TILE
    dest = _dest(off, eid, rank, n, p_alloc - 1)
    dest_w = dest.reshape(TOP_K, n_pad // DISPATCH_W, DISPATCH_W).transpose(1, 0, 2)
    xs = _sc_dispatch(h, dest_w, p_alloc)
    ys = _experts(xs, tile_first, tile_count, tile_end[-1:], wg, wu, wd)
    dest_tm = dest.T.reshape(n_pad // SUM_W, SUM_PARTS, SUM_W * TOP_K // SUM_PARTS)
    w_lanes = jnp.repeat(wts.T, SC_LANES, axis=1)
    return _sc_gather_sum(ys, dest_tm, w_lanes)


def kernel(x_prompt, x_sample, state_gla, c_prompt, c_sample, norm_g, ada_w, ada_b, gm_w_in, gm_b_in,
           gm_ln_g, gm_ln_b, gm_w_s, gm_b_s, gm_w_out, gla_w_in, gla_w_gk, gla_b_gk, gla_norm_g,
           gla_w_out, router_w, router_bias, exp_w_gate, exp_w_up, exp_w_down, sh_w_gate, sh_w_up,
           sh_w_down):
    batch, seq, _ = x_prompt.shape
    n_s = x_sample.shape[0]
    n_p = batch * seq
    tpb = seq // MIX_TILE
    xp = x_prompt.reshape(n_p, D)
    xs = x_sample.reshape(n_s, D)

    mod = _ada(jnp.concatenate([c_prompt, c_sample], axis=0), ada_w, ada_b)
    mod_p = [mod[i, :batch].reshape(batch, 1, 6 * D) for i in range(2)]
    mod_s = [mod[i, batch:] for i in range(2)]
    rw_t = [jnp.concatenate(_split3(router_w[i].T), axis=0) for i in range(2)]

    ws_causal = jnp.tril(gm_w_s[0]).astype(BF16)
    bs_cols = gm_b_s[0].T
    eye = jnp.eye(GM_CHUNK, dtype=F32)
    ws_first = (gm_w_s[0][:, 0, 0][:, None, None] * eye).astype(BF16)
    bs_first = jnp.broadcast_to(gm_b_s[0][:, 0][None, :], (GM_CHUNK, GM_GROUPS))
    gm_args = (norm_g[0], gm_w_in[0].astype(BF16), gm_b_in[0].reshape(1, -1), gm_ln_g[0].reshape(1, -1),
               gm_ln_b[0].reshape(1, -1))
    wout0 = gm_w_out[0].astype(BF16)
    shared = [(sh_w_gate[i].astype(BF16), sh_w_up[i].astype(BF16), sh_w_down[i].astype(BF16))
              for i in range(2)]
    n_qkvg = 2 * GLA_DK_TOT + 2 * GLA_DV_TOT
    wqkvg = gla_w_in[0][:, :n_qkvg].astype(BF16)
    wlr = jnp.pad(gla_w_in[0][:, n_qkvg:], ((0, 0), (0, LANES - GLA_GATE_RANK))).astype(BF16)
    wgk = jnp.pad(gla_w_gk[0], ((0, LANES - GLA_GATE_RANK), (0, 0))).astype(BF16)
    gla_args = (norm_g[1], wqkvg, wlr, wgk, gla_b_gk[0].reshape(1, -1), gla_norm_g[0].reshape(1, -1),
                gla_w_out[0].astype(BF16), rw_t[1])
    experts_f32 = (exp_w_gate, exp_w_up, exp_w_down)

    half = batch // 2
    streams = [(0, half, False), (half, batch - half, True)]
    st = [dict() for _ in streams]

    experts = []
    for layer, (s, (b0, nb, with_new)) in enumerate(zip(st, streams)):
        s["mod_p"] = [mod_p[i][b0:b0 + nb] for i in range(2)]
        s["n"] = nb * seq
        s["n_all"] = s["n"] + (n_s if with_new else 0)
        s["n_pad"] = -(-s["n_all"] // TOKEN_PAD) * TOKEN_PAD
        s["x1p"], s["h"], s["lg"], *w16 = _gmlp_mixer(
            xp, b0 * tpb, s["n"], s["mod_p"][0], False, MIX_TILE, tpb, *gm_args, ws_causal, bs_cols, wout0,
            rw_t[0], emit_v=False, cast_w=(layer, *experts_f32), ffn_rows=s["n_pad"])
        experts.append(w16)
        if with_new:
            s["x1s"], s["h"], s["lg"], v_rows = _gmlp_mixer(
                xs, 0, n_s, mod_s[0], True, n_s, 1, *gm_args, ws_first, bs_first, wout0, rw_t[0], emit_v=True,
                into=(s["h"], s["lg"], s["n"] // n_s))
    for s, (b0, nb, with_new) in zip(st, streams):
        s["routed0"] = _moe_routed(s["h"], s["lg"], s["n_all"], router_bias[0], *experts[0])
        if with_new:
            s["x2s"] = _combine(s["x1s"], s["routed0"], s["h"], s["n"] // n_s, mod_s[0], True, n_s, 1,
                                norm_g[0], *shared[0])
    for s, (b0, nb, with_new) in zip(st, streams):
        s["x3p"], h1, lg1, s["st_p"] = _gla_mixer(s["x1p"], s["routed0"], s["h"], s["mod_p"][0], norm_g[0],
                                                  shared[0], s["mod_p"][1], nb, seq, *gla_args,
                                                  ffn_rows=s["n_pad"])
        if with_new:
            s["x3s"], h1, lg1, st_s = _gla_mixer_one(s["x2s"], mod_s[1], state_gla[:, 0], *gla_args,
                                                     into=(h1, lg1, s["n"] // n_s))
        s["h"], s["lg"] = h1, lg1
    y_prompt = None
    for s, (b0, nb, with_new) in zip(st, streams):
        routed = _moe_routed(s["h"], s["lg"], s["n_all"], router_bias[1], *experts[1])
        y_prompt = _combine(s["x3p"], routed, s["h"], 0, s["mod_p"][1], False, MIX_TILE, tpb, norm_g[1],
                            *shared[1], out_rows=n_p, out_blk0=b0 * tpb, out_buf=y_prompt)
        if with_new:
            y_new = _combine(s["x3s"], routed, s["h"], s["n"] // n_s, mod_s[1], True, n_s, 1, norm_g[1],
                             *shared[1])
    st_p = jnp.concatenate([s["st_p"] for s in st], axis=0)

    return (y_prompt.reshape(batch, seq, D), y_new.reshape(n_s, 1, D), st_p[:, None], st_s[:, None],
            v_rows.reshape(n_s, 1, 1, GM_HALF))
```

```python
import functools
import math

import jax
import jax.numpy as jnp
from jax import lax
from jax.experimental import pallas as pl
from jax.experimental.pallas import tpu as pltpu
from jax.experimental.pallas import tpu_sc as plsc

F32 = jnp.float32
BF16 = jnp.bfloat16

D = 1024
DP = D // 2
GM_CHUNK = 128
GM_HALF = 2 * D
GM_GROUPS = 8
GM_GROUP_DIM = GM_HALF // GM_GROUPS
GLA_HEADS = 4
GLA_DK = 128
GLA_DV = 256
GLA_DK_TOT = GLA_HEADS * GLA_DK
GLA_DV_TOT = GLA_HEADS * GLA_DV
GLA_GATE_RANK = 16
GLA_GATE_NORMALIZER = 16.0
GLA_CHUNK = 64
N_EXPERTS = 64
TOP_K = 8
N_EXPERT_GROUPS = 8
GROUP_SIZE = N_EXPERTS // N_EXPERT_GROUPS
TOPK_GROUPS = 4
EXPERT_DIM = D // 4
ROUTED_SCALE = 2.5
NORM_EPS = 1e-6
LN_EPS = 1e-5

LANES = 128
VMEM_LIMIT = 56 * 1024 * 1024

MIX_TILE = 256
GLA_TILE = 512
GLA_CUM_BLOCK = 256
GM_COL_BLOCK = 512
ROUTER_TILE = 1024
EXPERT_TILE = 544
EXPERT_X_SLOTS = 6
EXPERT_AHEAD = EXPERT_X_SLOTS - 2
EXPERT_Y_SLOTS = 4
SC_WORKERS = 32
DISPATCH_W = 32
SC_LANES = 16
SUM_W = 16
SUM_PARTS = 4
SUM_UNROLL = 4
TOKEN_PAD = SC_WORKERS * DISPATCH_W


def _cparams(*sem):
    return pltpu.CompilerParams(dimension_semantics=sem, vmem_limit_bytes=VMEM_LIMIT)


def _rms(x, g):
    return x * lax.rsqrt(jnp.mean(x * x, axis=-1, keepdims=True) + NORM_EPS) * g


def _silu(x):
    return x * (1.0 / (1.0 + jnp.exp(-x)))


def _gelu(x):
    return 0.5 * x * (1.0 + lax.erf(x * (1.0 / math.sqrt(2.0))))


def _bdot(a, b):
    return jnp.dot(a.astype(BF16), b.astype(BF16), preferred_element_type=F32)


def _dot_nt(a, b, precision=None):
    return lax.dot_general(a, b, (((1,), (1,)), ((), ())), preferred_element_type=F32,
                           precision=precision)


def _mod_slices(mod_ref):
    return [mod_ref[:, i * D:(i + 1) * D] for i in range(6)]


HI_HALF = -65536


def _pack_rows(x):
    lo = lax.bitcast_convert_type(x[:, :DP].astype(BF16).astype(F32), jnp.int32)
    hi = lax.bitcast_convert_type(x[:, DP:].astype(BF16).astype(F32), jnp.int32)
    return lax.shift_right_logical(lo, 16) | (hi & HI_HALF)


def _unpack_rows(p):
    lo = lax.bitcast_convert_type(lax.shift_left(p, 16), F32)
    hi = lax.bitcast_convert_type(p & HI_HALF, F32)
    return lo, hi


def _ffn_prep(x1, ng, sh2, sc2, rw_ref, h_ref, lg_ref, rows=slice(None)):
    hffn = _rms(x1, ng[2:3]) * (1.0 + sc2) + sh2
    h_ref[rows, :] = _pack_rows(hffn)
    lg3 = _dot_nt(rw_ref[...], hffn.astype(BF16))
    lg_ref[:, rows] = lg3[:N_EXPERTS] + lg3[N_EXPERTS:2 * N_EXPERTS] + lg3[2 * N_EXPERTS:]


def _ada_body(c_ref, w_ref, b_ref, o_ref):
    c = c_ref[...]
    o_ref[...] = _bdot(_silu(c), w_ref[...]) + b_ref[...]


def _ada(c, ada_w, ada_b):
    n = c.shape[0]
    depth = ada_w.shape[0]
    tn = 1536
    return pl.pallas_call(
        _ada_body,
        grid=(depth, 6 * D // tn),
        in_specs=[pl.BlockSpec((n, D), lambda l, j: (0, 0)),
                  pl.BlockSpec((None, D, tn), lambda l, j: (l, 0, j)),
                  pl.BlockSpec((None, 1, tn), lambda l, j: (l, 0, j))],
        out_specs=pl.BlockSpec((None, n, tn), lambda l, j: (l, 0, j)),
        out_shape=jax.ShapeDtypeStruct((depth, n, 6 * D), F32),
        compiler_params=_cparams("parallel", "parallel"),
        name="ada_mod",
    )(c, ada_w, ada_b.reshape(depth, 1, 6 * D))


def _mod_spec(per_row, tt, tiles_per_batch):
    if per_row:
        return pl.BlockSpec((tt, 6 * D), lambda i: (i, 0))
    return pl.BlockSpec((None, 1, 6 * D), lambda i: (i // tiles_per_batch, 0, 0))


def _const_spec(shape):
    zeros = (0,) * len(shape)
    return pl.BlockSpec(shape, lambda *_: zeros)


def _gmlp_body(x_ref, mod_ref, ng_ref, win_ref, bin_ref, lng_ref, lnb_ref, ws_ref, bs_ref, wout_ref,
               rw_ref, *rest, n_chunks, emit_v, cast_w, n_alias):
    rest = list(rest)
    w32_refs = [rest.pop(0) for _ in range(3)] if cast_w else []
    rest = rest[n_alias:]
    x1_ref, h_ref, lg_ref = rest[:3]
    rest = rest[3:]
    v_ref = rest.pop(0) if emit_v else None
    w16_refs = [rest.pop(0) for _ in range(3)] if cast_w else []
    um_ref, z_ref = rest
    for src, dst in zip(w32_refs, w16_refs):
        dst[...] = src[...].astype(BF16)
    sh1, sc1, g1, sh2, sc2, _ = _mod_slices(mod_ref)
    ng = ng_ref[...]
    x = x_ref[...]
    hb = (_rms(x, ng[0:1]) * (1.0 + sc1) + sh1).astype(BF16)
    for cb in range(2 * GM_HALF // GM_COL_BLOCK):
        cols = slice(cb * GM_COL_BLOCK, (cb + 1) * GM_COL_BLOCK)
        z_ref[:, cols] = _gelu(jnp.dot(hb, win_ref[:, cols], preferred_element_type=F32) + bin_ref[:, cols])
    u = z_ref[:, :GM_HALF]
    v = z_ref[:, GM_HALF:]
    mu = jnp.mean(v, axis=-1, keepdims=True)
    vc = v - mu
    var = jnp.mean(vc * vc, axis=-1, keepdims=True)
    v = vc * lax.rsqrt(var + LN_EPS) * lng_ref[...] + lnb_ref[...]
    if emit_v:
        v_ref[...] = v
    vb = v.astype(BF16)
    for c in range(n_chunks):
        rows = slice(c * GM_CHUNK, (c + 1) * GM_CHUNK)
        for g in range(GM_GROUPS):
            cols = slice(g * GM_GROUP_DIM, (g + 1) * GM_GROUP_DIM)
            mixed = jnp.dot(ws_ref[g], vb[rows, cols], preferred_element_type=F32) + bs_ref[:, g:g + 1]
            um_ref[rows, cols] = (u[rows, cols] * mixed).astype(BF16)
    y = jnp.dot(um_ref[...], wout_ref[...], preferred_element_type=F32)
    x1 = x + g1 * _rms(y, ng[1:2])
    x1_ref[...] = x1
    _ffn_prep(x1, ng, sh2, sc2, rw_ref, h_ref, lg_ref)


def _ffn_out(n, ffn_rows, into):
    rows = ffn_rows or n
    oblk = 0
    bufs = []
    if into is not None:
        *bufs, oblk = into
        rows = bufs[0].shape[0]
    elif rows != n:
        bufs = [jnp.zeros((rows, DP), jnp.int32), jnp.zeros((N_EXPERTS, rows), F32)]
    shapes = [jax.ShapeDtypeStruct((rows, DP), jnp.int32), jax.ShapeDtypeStruct((N_EXPERTS, rows), F32)]
    return shapes, bufs, oblk


def _gmlp_mixer(x2d, blk0, n, mod, per_row, tt, tiles_per_batch, ng, win, b_in, ln_g, ln_b, ws, bs, wout,
                rw_t, emit_v, cast_w=None, ffn_rows=None, into=None):
    steps = n // tt
    ffn_shapes, alias_bufs, oblk = _ffn_out(n, ffn_rows, into)
    out_shape = [jax.ShapeDtypeStruct((n, D), F32)] + ffn_shapes
    out_specs = [pl.BlockSpec((tt, D), lambda i: (i, 0)), pl.BlockSpec((tt, DP), lambda i: (i + oblk, 0)),
                 pl.BlockSpec((N_EXPERTS, tt), lambda i: (0, i + oblk))]
    if emit_v:
        out_shape.append(jax.ShapeDtypeStruct((n, GM_HALF), F32))
        out_specs.append(pl.BlockSpec((tt, GM_HALF), lambda i: (i, 0)))

    def one_buffer(a):
        zeros = (0,) * a.ndim
        return pl.BlockSpec(a.shape, lambda *_: zeros, pipeline_mode=pl.Buffered(1))

    consts = (ng, win, b_in, ln_g, ln_b, ws, bs, wout, rw_t)
    in_specs = [pl.BlockSpec((tt, D), lambda i: (i + blk0, 0)), _mod_spec(per_row, tt, tiles_per_batch)]
    in_specs += [one_buffer(a) for a in consts]
    args = [x2d, mod, *consts]
    if cast_w is not None:
        layer, *w_all = cast_w
        per_step = N_EXPERTS // steps
        for w in w_all:
            blk = (None, per_step) + w.shape[2:]
            in_specs.append(pl.BlockSpec(blk, lambda i: (layer, i, 0, 0)))
            out_specs.append(pl.BlockSpec(blk[1:], lambda i: (i, 0, 0)))
            out_shape.append(jax.ShapeDtypeStruct(w.shape[1:], BF16))
            args.append(w)
    aliases = {len(args) + i: 1 + i for i in range(len(alias_bufs))}
    in_specs += [pl.BlockSpec(memory_space=pl.ANY)] * len(alias_bufs)
    args += alias_bufs
    return pl.pallas_call(
        functools.partial(_gmlp_body, n_chunks=tt // GM_CHUNK, emit_v=emit_v, cast_w=cast_w is not None,
                          n_alias=len(alias_bufs)),
        grid=(steps,),
        in_specs=in_specs,
        out_specs=out_specs,
        out_shape=out_shape,
        input_output_aliases=aliases,
        scratch_shapes=[pltpu.VMEM((tt, GM_HALF), BF16), pltpu.VMEM((tt, 2 * GM_HALF), F32)],
        compiler_params=_cparams("parallel"),
        name="gmlp_mixer_rows" if per_row else "gmlp_mixer",
    )(*args)


def _combine_body(x_ref, y_ref, h_ref, mod_ref, ng_ref, swg_ref, swu_ref, swd_ref, *rest):
    o_ref = rest[-1]
    o_ref[...] = _channel_mix_residual(x_ref[...], y_ref[...], h_ref[...], mod_ref[:, 5 * D:6 * D],
                                       ng_ref[3:4, :], swg_ref, swu_ref, swd_ref)


def _combine(x2d, routed, hp, blk0, mod, per_row, tt, tiles_per_batch, ng, swg, swu, swd,
             out_rows=None, out_blk0=0, out_buf=None):
    n = x2d.shape[0]
    in_specs = [pl.BlockSpec((tt, D), lambda i: (i, 0)),
                pl.BlockSpec((tt, D), lambda i: (i + blk0, 0)),
                pl.BlockSpec((tt, DP), lambda i: (i + blk0, 0)),
                _mod_spec(per_row, tt, tiles_per_batch),
                _const_spec(ng.shape), _const_spec(swg.shape), _const_spec(swu.shape),
                _const_spec(swd.shape)]
    args = [x2d, routed, hp, mod, ng, swg, swu, swd]
    aliases = {}
    if out_buf is not None:
        in_specs.append(pl.BlockSpec(memory_space=pl.ANY))
        aliases = {len(args): 0}
        args.append(out_buf)
    return pl.pallas_call(
        _combine_body,
        grid=(n // tt,),
        in_specs=in_specs,
        out_specs=pl.BlockSpec((tt, D), lambda i: (i + out_blk0, 0)),
        out_shape=jax.ShapeDtypeStruct((out_rows or n, D), F32),
        input_output_aliases=aliases,
        compiler_params=_cparams("parallel"),
        name="combine_rows" if per_row else "combine",
    )(*args)


def _router_body(lg_ref, bias_ref, tri_ref, eid_ref, rank_ref, wts_ref, cnt_ref, carry_ref, *, n_real):
    step = pl.program_id(0)

    @pl.when(step == 0)
    def _():
        carry_ref[...] = jnp.zeros_like(carry_ref)

    lg = lg_ref[...]
    tn = lg.shape[1]
    real = (step * tn + lax.broadcasted_iota(jnp.int32, (1, tn), 1)) < n_real
    lg = jnp.where(real, lg, 0.0)
    scores = 1.0 / (1.0 + jnp.exp(-lg))
    sel = scores + bias_ref[...]
    neg = -jnp.inf
    sub8 = lax.broadcasted_iota(jnp.int32, (GROUP_SIZE, tn), 0)
    gsub = lax.broadcasted_iota(jnp.int32, (N_EXPERT_GROUPS, tn), 0)
    gs = jnp.zeros((N_EXPERT_GROUPS, tn), F32)
    for g in range(N_EXPERT_GROUPS):
        blk = sel[g * GROUP_SIZE:(g + 1) * GROUP_SIZE, :]
        m1 = jnp.max(blk, axis=0, keepdims=True)
        i1 = jnp.min(jnp.where(blk == m1, sub8, GROUP_SIZE), axis=0, keepdims=True)
        m2 = jnp.max(jnp.where(sub8 == i1, neg, blk), axis=0, keepdims=True)
        gs = jnp.where(gsub == g, m1 + m2, gs)
    gmask = jnp.zeros((N_EXPERT_GROUPS, tn), jnp.bool_)
    for _ in range(TOPK_GROUPS):
        m = jnp.max(gs, axis=0, keepdims=True)
        i = jnp.min(jnp.where(gs == m, gsub, N_EXPERT_GROUPS), axis=0, keepdims=True)
        hit = gsub == i
        gmask = jnp.logical_or(gmask, hit)
        gs = jnp.where(hit, neg, gs)
    gmaskf = gmask.astype(F32)
    blocks = []
    for g in range(N_EXPERT_GROUPS):
        keep = jnp.broadcast_to(gmaskf[g:g + 1, :], (GROUP_SIZE, tn)) > 0.5
        blocks.append(jnp.where(keep, sel[g * GROUP_SIZE:(g + 1) * GROUP_SIZE, :], neg))
    msel = jnp.concatenate(blocks, axis=0)
    esub = lax.broadcasted_iota(jnp.int32, (N_EXPERTS, tn), 0)
    chosen = jnp.zeros((N_EXPERTS, tn), jnp.bool_)
    picks = []
    for _ in range(TOP_K):
        m = jnp.max(msel, axis=0, keepdims=True)
        i = jnp.min(jnp.where(msel == m, esub, N_EXPERTS), axis=0, keepdims=True)
        hit = esub == i
        picks.append(i)
        chosen = jnp.logical_or(chosen, hit)
        msel = jnp.where(hit, neg, msel)
    w = jnp.where(chosen, scores, 0.0)
    w = w / jnp.sum(w, axis=0, keepdims=True) * ROUTED_SCALE
    counted = jnp.where(jnp.logical_and(chosen, real), 1.0, 0.0)
    incl = jnp.dot(counted.astype(BF16), tri_ref[...], preferred_element_type=F32)
    rank_full = carry_ref[:, 0:1] + incl - 1.0
    ksub = lax.broadcasted_iota(jnp.int32, (TOP_K, tn), 0)
    eid = jnp.zeros((TOP_K, tn), jnp.int32)
    rank = jnp.zeros((TOP_K, tn), F32)
    wts = jnp.zeros((TOP_K, tn), F32)
    for k in range(TOP_K):
        hit = esub == picks[k]
        eid = jnp.where(ksub == k, picks[k], eid)
        rank = jnp.where(ksub == k, jnp.sum(jnp.where(hit, rank_full, 0.0), axis=0, keepdims=True), rank)
        wts = jnp.where(ksub == k, jnp.sum(jnp.where(hit, w, 0.0), axis=0, keepdims=True), wts)
    eid_ref[...] = eid
    rank_ref[...] = rank.astype(jnp.int32)
    wts_ref[...] = wts
    carry = carry_ref[...] + incl[:, tn - 1:tn]
    carry_ref[...] = carry
    cnt_ref[...] = carry.astype(jnp.int32)


def _router(lg_t, bias, n_real):
    n = lg_t.shape[1]
    tn = ROUTER_TILE
    idx = jnp.arange(tn)
    tri = (idx[:, None] <= idx[None, :]).astype(BF16)
    kspec = pl.BlockSpec((TOP_K, tn), lambda i: (0, i))
    return pl.pallas_call(
        functools.partial(_router_body, n_real=n_real),
        grid=(n // tn,),
        in_specs=[pl.BlockSpec((N_EXPERTS, tn), lambda i: (0, i)), _const_spec((N_EXPERTS, 1)),
                  _const_spec((tn, tn))],
        out_specs=[kspec, kspec, kspec, _const_spec((N_EXPERTS, LANES))],
        out_shape=[jax.ShapeDtypeStruct((TOP_K, n), jnp.int32), jax.ShapeDtypeStruct((TOP_K, n), jnp.int32),
                   jax.ShapeDtypeStruct((TOP_K, n), F32), jax.ShapeDtypeStruct((N_EXPERTS, LANES), jnp.int32)],
        scratch_shapes=[pltpu.VMEM((N_EXPERTS, LANES), F32)],
        compiler_params=_cparams("arbitrary"),
        name="router",
    )(lg_t, bias.reshape(N_EXPERTS, 1), tri)


def _dest_body(off_ref, eid_ref, rank_ref, dest_ref, *, n_real, last_row):
    eid = eid_ref[...]
    base = jnp.zeros(eid.shape, jnp.int32)
    for e in range(N_EXPERTS):
        base = jnp.where(eid == e, off_ref[e], base)
    tok = lax.broadcasted_iota(jnp.int32, eid.shape, 1)
    slot = lax.broadcasted_iota(jnp.int32, eid.shape, 0)
    unused = last_row - ((tok - n_real) * TOP_K + slot)
    dest_ref[...] = jnp.where(tok < n_real, base + rank_ref[...], unused)


def _dest(off, eid, rank, n_real, last_row):
    spec = pl.BlockSpec(eid.shape, lambda i, off_ref: (0, 0))
    return pl.pallas_call(
        functools.partial(_dest_body, n_real=n_real, last_row=last_row),
        grid_spec=pltpu.PrefetchScalarGridSpec(num_scalar_prefetch=1, grid=(1,), in_specs=[spec, spec],
                                               out_specs=spec),
        out_shape=jax.ShapeDtypeStruct(eid.shape, jnp.int32),
        compiler_params=_cparams("arbitrary"),
        name="dest_rows",
    )(off, eid, rank)


def _sc_mesh():
    return plsc.VectorSubcoreMesh(core_axis_name="core", subcore_axis_name="subcore")


def _sc_dispatch(hp, dest_w, p_alloc):
    n = hp.shape[0]
    w = dest_w.shape[2]

    @functools.partial(pl.kernel, out_type=jax.ShapeDtypeStruct((p_alloc, DP), jnp.int32), mesh=_sc_mesh(),
                       name="sc_dispatch")
    def run(hp_hbm, dest_hbm, xs_hbm):
        def body(x_vmem, i_vmem):
            for k in range(TOP_K):
                pltpu.sync_copy(x_vmem, xs_hbm.at[i_vmem.at[k]])

        pltpu.emit_pipeline(
            body,
            grid=(n // w,),
            in_specs=[pl.BlockSpec((w, DP), lambda i: (i, 0)),
                      pl.BlockSpec((None, TOP_K, w), lambda i: (i, 0, 0))],
            out_specs=[],
            core_axis_name=("core", "subcore"),
            dimension_semantics=(pltpu.PARALLEL,),
        )(hp_hbm, dest_hbm)

    return run(hp, dest_w)


def _sc_gather_sum(ys, dest_tm, w_lanes):
    n_win, parts, pk = dest_tm.shape
    w = parts * pk // TOP_K
    wp = w // parts
    n_vec = DP // SC_LANES

    @functools.partial(pl.kernel, out_type=jax.ShapeDtypeStruct((n_win * w, D), F32), mesh=_sc_mesh(),
                       scratch_types=[pltpu.VMEM((parts, pk, DP), jnp.int32), pltpu.SemaphoreType.DMA((parts,))],
                       compiler_params=pltpu.CompilerParams(needs_layout_passes=False), name="sc_gather_sum")
    def run(ys_hbm, dest_hbm, w_hbm, o_hbm, rows_v, sems):
        def body(i_vmem, w_vmem, o_vmem):
            copies = [pltpu.async_copy(ys_hbm.at[i_vmem.at[p]], rows_v.at[p], sems.at[p]) for p in range(parts)]
            for p in range(parts):
                copies[p].wait()

                @pl.loop(0, wp)
                def _(t):
                    tok = p * wp + t
                    wv = [w_vmem[tok, pl.ds(k * SC_LANES, SC_LANES)] for k in range(TOP_K)]

                    @plsc.parallel_loop(0, n_vec, unroll=SUM_UNROLL)
                    def _(j):
                        col = j * SC_LANES
                        lo = jnp.zeros((SC_LANES,), F32)
                        hi = jnp.zeros((SC_LANES,), F32)
                        for k in range(TOP_K):
                            word = rows_v[p, t * TOP_K + k, pl.ds(col, SC_LANES)]
                            lo = lo + wv[k] * plsc.bitcast(lax.shift_left(word, 16), F32)
                            hi = hi + wv[k] * plsc.bitcast(word & HI_HALF, F32)
                        o_vmem[tok, pl.ds(col, SC_LANES)] = lo
                        o_vmem[tok, pl.ds(DP + col, SC_LANES)] = hi

        pltpu.emit_pipeline(
            body,
            grid=(n_win,),
            in_specs=[pl.BlockSpec((None, parts, pk), lambda i: (i, 0, 0)),
                      pl.BlockSpec((w, TOP_K * SC_LANES), lambda i: (i, 0))],
            out_specs=[pl.BlockSpec((w, D), lambda i: (i, 0))],
            core_axis_name=("core", "subcore"),
            dimension_semantics=(pltpu.PARALLEL,),
        )(dest_hbm, w_hbm, o_hbm)

    return run(ys, dest_tm, w_lanes)


def _expert_body(first_ref, cnt_ref, nused_ref, xs_hbm, wg_s, wu_s, wd_s, ys_hbm, xbuf, ybuf, xsem, ysem):
    e = pl.program_id(0)
    n_used = nused_ref[0]

    def load(g):
        rows = pl.ds(pl.multiple_of(g * EXPERT_TILE, EXPERT_TILE), EXPERT_TILE)
        slot = g % EXPERT_X_SLOTS
        return pltpu.make_async_copy(xs_hbm.at[rows], xbuf.at[slot], xsem.at[slot])

    def store(g):
        rows = pl.ds(pl.multiple_of(g * EXPERT_TILE, EXPERT_TILE), EXPERT_TILE)
        slot = g % EXPERT_Y_SLOTS
        return pltpu.make_async_copy(ybuf.at[slot], ys_hbm.at[rows], ysem.at[slot])

    @pl.when(e == 0)
    def _():
        for g in range(EXPERT_AHEAD):
            @pl.when(g < n_used)
            def _():
                load(g).start()

    first = first_ref[e]
    cnt = cnt_ref[e]

    def acquire(g):
        ahead = g + EXPERT_AHEAD

        @pl.when(ahead < n_used)
        def _():
            load(ahead).start()

        load(g).wait()

        @pl.when(g >= EXPERT_Y_SLOTS)
        def _():
            store(g - EXPERT_Y_SLOTS).wait()

    def compute(g):
        lo, hi = _unpack_rows(xbuf[g % EXPERT_X_SLOTS])
        lo = lo.astype(BF16)
        hi = hi.astype(BF16)

        def xdot(w_s):
            return (jnp.dot(lo, w_s[:DP, :], preferred_element_type=F32) +
                    jnp.dot(hi, w_s[DP:, :], preferred_element_type=F32))

        a = (_silu(xdot(wg_s)) * xdot(wu_s)).astype(BF16)
        ybuf[g % EXPERT_Y_SLOTS] = _pack_rows(jnp.dot(a, wd_s[...], preferred_element_type=F32))

    def pair(j, carry):
        g = first + 2 * j
        acquire(g)
        acquire(g + 1)
        compute(g)
        compute(g + 1)
        store(g).start()
        store(g + 1).start()
        return carry

    lax.fori_loop(0, cnt // 2, pair, 0)

    @pl.when(cnt % 2 == 1)
    def _():
        g = first + cnt - 1
        acquire(g)
        compute(g)
        store(g).start()

    @pl.when(e == N_EXPERTS - 1)
    def _():
        for k in range(EXPERT_Y_SLOTS):
            g = n_used - 1 - k

            @pl.when(g >= 0)
            def _():
                store(g).wait()


def _experts(xs, tile_first, tile_count, n_used, wg, wu, wd):
    def w_map(e, first, cnt, nu):
        return (e, 0, 0)

    return pl.pallas_call(
        _expert_body,
        grid_spec=pltpu.PrefetchScalarGridSpec(
            num_scalar_prefetch=3, grid=(N_EXPERTS,),
            in_specs=[pl.BlockSpec(memory_space=pl.ANY),
                      pl.BlockSpec((None, D, EXPERT_DIM), w_map),
                      pl.BlockSpec((None, D, EXPERT_DIM), w_map),
                      pl.BlockSpec((None, EXPERT_DIM, D), w_map)],
            out_specs=pl.BlockSpec(memory_space=pl.ANY),
            scratch_shapes=[pltpu.VMEM((EXPERT_X_SLOTS, EXPERT_TILE, DP), jnp.int32),
                            pltpu.VMEM((EXPERT_Y_SLOTS, EXPERT_TILE, DP), jnp.int32),
                            pltpu.SemaphoreType.DMA((EXPERT_X_SLOTS,)),
                            pltpu.SemaphoreType.DMA((EXPERT_Y_SLOTS,))]),
        out_shape=jax.ShapeDtypeStruct(xs.shape, jnp.int32),
        compiler_params=_cparams("arbitrary"),
        name="experts",
    )(tile_first, tile_count, n_used, xs, wg, wu, wd)


def _log_sigmoid(z):
    return jnp.minimum(z, 0.0) - jnp.log(1.0 + jnp.exp(-jnp.abs(z)))


def _gla_gate(hb, wlr_ref, wgk_ref, bgk_ref):
    lr = jnp.dot(hb, wlr_ref[...], preferred_element_type=F32)
    z = _bdot(lr, wgk_ref[...]) + bgk_ref[...]
    return _log_sigmoid(z) * (1.0 / GLA_GATE_NORMALIZER)


def _split3(a):
    hi = a.astype(BF16)
    r1 = a - hi.astype(F32)
    mid = r1.astype(BF16)
    lo = (r1 - mid.astype(F32)).astype(BF16)
    return hi, mid, lo


def _gla_out(o_ref_val, go, gng):
    parts = []
    for hd in range(GLA_HEADS):
        cols = slice(hd * GLA_DV, (hd + 1) * GLA_DV)
        parts.append((_rms(o_ref_val[:, cols], gng) * _silu(go[:, cols])).astype(BF16))
    return jnp.concatenate(parts, axis=1)


def _channel_mix_residual(x, routed, h_packed, g2, ng3, swg_ref, swu_ref, swd_ref):
    h_lo, h_hi = _unpack_rows(h_packed)
    h_lo = h_lo.astype(BF16)
    h_hi = h_hi.astype(BF16)

    def hdot(w_ref_):
        return (jnp.dot(h_lo, w_ref_[:DP, :], preferred_element_type=F32) +
                jnp.dot(h_hi, w_ref_[DP:, :], preferred_element_type=F32))

    hs = (_silu(hdot(swg_ref)) * hdot(swu_ref)).astype(BF16)
    y = jnp.dot(hs, swd_ref[...], preferred_element_type=F32) + routed
    return x + g2 * _rms(y, ng3)


def _gla_body(x_ref, y_ref, hprev_ref, modprev_ref, ngprev_ref, swg_ref, swu_ref, swd_ref,
              mod_ref, ng_ref, wqkvg_ref, wlr_ref, wgk_ref, bgk_ref, tril_ref, gng_ref, wout_ref,
              rw_ref, *rest, tt, n_alias):
    x1_ref, h_ref, lg_ref, st_ref, st_scr, o_scr, qd_scr, dst_scr = rest[n_alias:]
    j = pl.program_id(1)

    @pl.when(j == 0)
    def _():
        st_scr[...] = jnp.zeros_like(st_scr)

    x = _channel_mix_residual(x_ref[...], y_ref[...], hprev_ref[...], modprev_ref[:, 5 * D:6 * D],
                              ngprev_ref[3:4, :], swg_ref, swu_ref, swd_ref)
    sh1, sc1, g1, sh2, sc2, _ = _mod_slices(mod_ref)
    ng = ng_ref[...]
    hb = (_rms(x, ng[0:1]) * (1.0 + sc1) + sh1).astype(BF16)
    q = jnp.dot(hb, wqkvg_ref[:, :GLA_DK_TOT], preferred_element_type=F32) * (GLA_DK ** -0.5)
    k = jnp.dot(hb, wqkvg_ref[:, GLA_DK_TOT:2 * GLA_DK_TOT], preferred_element_type=F32)
    v = jnp.dot(hb, wqkvg_ref[:, 2 * GLA_DK_TOT:2 * GLA_DK_TOT + GLA_DV_TOT],
                preferred_element_type=F32).astype(BF16)
    log_a = _gla_gate(hb, wlr_ref, wgk_ref, bgk_ref)
    tril = tril_ref[...]
    parts = _split3(log_a)
    b = jnp.concatenate(
        [sum(jnp.dot(tril, p[r:r + GLA_CUM_BLOCK], preferred_element_type=F32) for p in parts)
         for r in range(0, tt, GLA_CUM_BLOCK)], axis=0)
    n_chunks = tt // GLA_CHUNK
    in_chunk_causal = tril > 0
    for r in range(0, tt, GLA_CUM_BLOCK):
        blk = slice(r, r + GLA_CUM_BLOCK)
        for hd in range(GLA_HEADS):
            kc = slice(hd * GLA_DK, (hd + 1) * GLA_DK)
            vc = slice(hd * GLA_DV, (hd + 1) * GLA_DV)
            bb = b[blk, kc]
            q_dec = (q[blk, kc] * jnp.exp(bb)).astype(BF16)
            k_inv = (k[blk, kc] * jnp.exp(-bb)).astype(BF16)
            att = jnp.where(in_chunk_causal, _dot_nt(q_dec, k_inv), 0.0).astype(BF16)
            qd_scr[blk, kc] = q_dec
            o_scr[blk, vc] = jnp.dot(att, v[blk, vc], preferred_element_type=F32)
    for c in range(n_chunks):
        rows = slice(c * GLA_CHUNK, (c + 1) * GLA_CHUNK)
        last = (c + 1) * GLA_CHUNK - 1
        for hd in range(GLA_HEADS):
            kc = slice(hd * GLA_DK, (hd + 1) * GLA_DK)
            vc = slice(hd * GLA_DV, (hd + 1) * GLA_DV)
            k_end = (k[rows, kc] * jnp.exp(b[last:last + 1, kc] - b[rows, kc])).astype(BF16)
            dst_scr[c * GLA_HEADS + hd] = lax.dot_general(
                v[rows, vc], k_end, (((0,), (0,)), ((), ())), preferred_element_type=F32)
    states = [st_scr[hd] for hd in range(GLA_HEADS)]
    for c in range(n_chunks):
        rows = slice(c * GLA_CHUNK, (c + 1) * GLA_CHUNK)
        last = (c + 1) * GLA_CHUNK - 1
        for hd in range(GLA_HEADS):
            kc = slice(hd * GLA_DK, (hd + 1) * GLA_DK)
            vc = slice(hd * GLA_DV, (hd + 1) * GLA_DV)
            o_scr[rows, vc] += _dot_nt(qd_scr[rows, kc], states[hd].astype(BF16))
            states[hd] = states[hd] * jnp.exp(b[last:last + 1, kc]) + dst_scr[c * GLA_HEADS + hd]
    for hd in range(GLA_HEADS):
        st_scr[hd] = states[hd]

    @pl.when(j == pl.num_programs(1) - 1)
    def _():
        for hd in range(GLA_HEADS):
            st_ref[hd] = st_scr[hd].T

    go = jnp.dot(hb, wqkvg_ref[:, 2 * GLA_DK_TOT + GLA_DV_TOT:], preferred_element_type=F32)
    y = jnp.dot(_gla_out(o_scr[...], go, gng_ref[...]), wout_ref[...], preferred_element_type=F32)
    x1 = x + g1 * _rms(y, ng[1:2])
    x1_ref[...] = x1
    _ffn_prep(x1, ng, sh2, sc2, rw_ref, h_ref, lg_ref)


def _gla_mixer(x2d, routed, hp, mod3_prev, ng_prev, shared_prev, mod3, batch, seq, ng, wqkvg, wlr, wgk, bgk,
               gng, wout, rw_t, ffn_rows=None):
    tt = GLA_TILE
    tpb = seq // tt
    n = x2d.shape[0]
    idx = jnp.arange(GLA_CUM_BLOCK)
    tril = ((idx[:, None] >= idx[None, :]) &
            (idx[:, None] // GLA_CHUNK == idx[None, :] // GLA_CHUNK)).astype(BF16)
    row_map = lambda b, j: (b * tpb + j, 0)
    mod_map = lambda b, j: (b, 0, 0)
    consts = (ng, wqkvg, wlr, wgk, bgk, tril, gng, wout, rw_t)
    prev_consts = (ng_prev,) + tuple(shared_prev)
    ffn_shapes, alias_bufs, _ = _ffn_out(n, ffn_rows, None)
    args = [x2d, routed, hp, mod3_prev, *prev_consts, mod3, *consts]
    return pl.pallas_call(
        functools.partial(_gla_body, tt=tt, n_alias=len(alias_bufs)),
        grid=(batch, tpb),
        in_specs=[pl.BlockSpec((tt, D), row_map), pl.BlockSpec((tt, D), row_map),
                  pl.BlockSpec((tt, DP), row_map), pl.BlockSpec((None, 1, 6 * D), mod_map)] +
                 [_const_spec(a.shape) for a in prev_consts] +
                 [pl.BlockSpec((None, 1, 6 * D), mod_map)] +
                 [_const_spec(a.shape) for a in consts] +
                 [pl.BlockSpec(memory_space=pl.ANY)] * len(alias_bufs),
        out_specs=[pl.BlockSpec((tt, D), row_map), pl.BlockSpec((tt, DP), row_map),
                   pl.BlockSpec((N_EXPERTS, tt), lambda b, j: (0, b * tpb + j)),
                   pl.BlockSpec((None, GLA_HEADS, GLA_DK, GLA_DV), lambda b, j: (b, 0, 0, 0))],
        out_shape=[jax.ShapeDtypeStruct((n, D), F32)] + ffn_shapes +
                  [jax.ShapeDtypeStruct((batch, GLA_HEADS, GLA_DK, GLA_DV), F32)],
        input_output_aliases={len(args) + i: 1 + i for i in range(len(alias_bufs))},
        scratch_shapes=[pltpu.VMEM((GLA_HEADS, GLA_DV, GLA_DK), F32),
                        pltpu.VMEM((tt, GLA_DV_TOT), F32),
                        pltpu.VMEM((tt, GLA_DK_TOT), BF16),
                        pltpu.VMEM((tt // GLA_CHUNK * GLA_HEADS, GLA_DV, GLA_DK), F32)],
        compiler_params=_cparams("parallel", "arbitrary"),
        name="gla_mixer",
    )(*args, *alias_bufs)


def _gla1_proj_body(x_ref, mod_ref, ng_ref, wqkvg_ref, wlr_ref, wgk_ref, bgk_ref,
                    q_ref, k_ref, v_ref, go_ref, dec_ref):
    sh1, sc1, _, _, _, _ = _mod_slices(mod_ref)
    ng = ng_ref[...]
    hb = (_rms(x_ref[...], ng[0:1]) * (1.0 + sc1) + sh1).astype(BF16)
    proj = jnp.dot(hb, wqkvg_ref[...], preferred_element_type=F32)
    q_ref[...] = proj[:, :GLA_DK_TOT] * (GLA_DK ** -0.5)
    k_ref[...] = proj[:, GLA_DK_TOT:2 * GLA_DK_TOT]
    v_ref[...] = proj[:, 2 * GLA_DK_TOT:2 * GLA_DK_TOT + GLA_DV_TOT]
    go_ref[...] = proj[:, 2 * GLA_DK_TOT + GLA_DV_TOT:]
    dec_ref[...] = jnp.exp(_gla_gate(hb, wlr_ref, wgk_ref, bgk_ref))


GLA1_TOK = 16


def _gla1_state_body(st_ref, qc_ref, kc_ref, dc_ref, v_ref, nst_ref, o_ref):
    v = v_ref[...]
    for i in range(GLA1_TOK):
        for hd in range(GLA_HEADS):
            vrow = v[i:i + 1, hd * GLA_DV:(hd + 1) * GLA_DV]
            s_new = dc_ref[hd][:, i:i + 1] * st_ref[i, hd] + kc_ref[hd][:, i:i + 1] * vrow
            nst_ref[i, hd] = s_new
            o_ref[i:i + 1, hd * GLA_DV:(hd + 1) * GLA_DV] = jnp.sum(
                qc_ref[hd][:, i:i + 1] * s_new, axis=0, keepdims=True)


def _gla1_out_body(x_ref, o_ref, go_ref, mod_ref, ng_ref, gng_ref, wout_ref, rw_ref, *rest):
    x1_ref, h_ref, lg_ref = rest[-3:]
    _, _, g1, sh2, sc2, _ = _mod_slices(mod_ref)
    ng = ng_ref[...]
    y = jnp.dot(_gla_out(o_ref[...], go_ref[...], gng_ref[...]), wout_ref[...], preferred_element_type=F32)
    x1 = x_ref[...] + g1 * _rms(y, ng[1:2])
    x1_ref[...] = x1
    _ffn_prep(x1, ng, sh2, sc2, rw_ref, h_ref, lg_ref)


def _gla_mixer_one(x2d, mod2, state, ng, wqkvg, wlr, wgk, bgk, gng, wout, rw_t, into=None):
    n = x2d.shape[0]
    consts = (ng, wqkvg, wlr, wgk, bgk)
    q, k, v, go, dec = pl.pallas_call(
        _gla1_proj_body,
        in_specs=[_const_spec(a.shape) for a in (x2d, mod2) + consts],
        out_specs=[_const_spec((n, GLA_DK_TOT)), _const_spec((n, GLA_DK_TOT)), _const_spec((n, GLA_DV_TOT)),
                   _const_spec((n, GLA_DV_TOT)), _const_spec((n, GLA_DK_TOT))],
        out_shape=[jax.ShapeDtypeStruct((n, GLA_DK_TOT), F32), jax.ShapeDtypeStruct((n, GLA_DK_TOT), F32),
                   jax.ShapeDtypeStruct((n, GLA_DV_TOT), F32), jax.ShapeDtypeStruct((n, GLA_DV_TOT), F32),
                   jax.ShapeDtypeStruct((n, GLA_DK_TOT), F32)],
        grid=(1,),
        compiler_params=_cparams("arbitrary"),
        name="gla1_proj",
    )(x2d, mod2, *consts)

    def cols(a):
        return a.reshape(n // GLA1_TOK, GLA1_TOK, GLA_HEADS, GLA_DK).transpose(0, 2, 3, 1)

    col_spec = pl.BlockSpec((None, GLA_HEADS, GLA_DK, GLA1_TOK), lambda i: (i, 0, 0, 0))
    st_spec = pl.BlockSpec((GLA1_TOK, GLA_HEADS, GLA_DK, GLA_DV), lambda i: (i, 0, 0, 0))
    new_state, o = pl.pallas_call(
        _gla1_state_body,
        grid=(n // GLA1_TOK,),
        in_specs=[st_spec, col_spec, col_spec, col_spec, pl.BlockSpec((GLA1_TOK, GLA_DV_TOT), lambda i: (i, 0))],
        out_specs=[st_spec, pl.BlockSpec((GLA1_TOK, GLA_DV_TOT), lambda i: (i, 0))],
        out_shape=[jax.ShapeDtypeStruct(state.shape, F32), jax.ShapeDtypeStruct((n, GLA_DV_TOT), F32)],
        compiler_params=_cparams("parallel"),
        name="gla1_state",
    )(state, cols(q), cols(k), cols(dec), v)

    consts = (mod2, ng, gng, wout, rw_t)
    ffn_shapes, alias_bufs, oblk = _ffn_out(n, None, into)
    n_in = 3 + len(consts)
    x1, h, lg = pl.pallas_call(
        _gla1_out_body,
        grid=(1,),
        in_specs=[_const_spec(a.shape) for a in (x2d, o, go) + consts] +
                 [pl.BlockSpec(memory_space=pl.ANY)] * len(alias_bufs),
        out_specs=[_const_spec((n, D)), pl.BlockSpec((n, DP), lambda i: (oblk, 0)),
                   pl.BlockSpec((N_EXPERTS, n), lambda i: (0, oblk))],
        out_shape=[jax.ShapeDtypeStruct((n, D), F32)] + ffn_shapes,
        input_output_aliases={n_in + i: 1 + i for i in range(len(alias_bufs))},
        compiler_params=_cparams("arbitrary"),
        name="gla1_out",
    )(x2d, o, go, *consts, *alias_bufs)
    return x1, h, lg, new_state


def _moe_routed(h, lg, n, router_bias, wg, wu, wd):
    n_pad = h.shape[0]
    eid, rank, wts, counts = _router(lg, router_bias, n)
    tile_count = ((counts[:, 0] + EXPERT_TILE - 1) // EXPERT_TILE).astype(jnp.int32)
    tile_end = jnp.cumsum(tile_count).astype(jnp.int32)
    tile_first = tile_end - tile_count
    off = tile_first * EXPERT_TILE
    p_alloc = TOP_K * n_pad + N_EXPERTS * EXPERT_TILE
    dest = _dest(off, eid, rank, n, p_alloc - 1)
    dest_w = dest.reshape(TOP_K, n_pad // DISPATCH_W, DISPATCH_W).transpose(1, 0, 2)
    xs = _sc_dispatch(h, dest_w, p_alloc)
    ys = _experts(xs, tile_first, tile_count, tile_end[-1:], wg, wu, wd)
    dest_tm = dest.T.reshape(n_pad // SUM_W, SUM_PARTS, SUM_W * TOP_K // SUM_PARTS)
    w_lanes = jnp.repeat(wts.T, SC_LANES, axis=1)
    return _sc_gather_sum(ys, dest_tm, w_lanes)


def kernel(x_prompt, x_sample, state_gla, c_prompt, c_sample, norm_g, ada_w, ada_b, gm_w_in, gm_b_in,
           gm_ln_g, gm_ln_b, gm_w_s, gm_b_s, gm_w_out, gla_w_in, gla_w_gk, gla_b_gk, gla_norm_g,
           gla_w_out, router_w, router_bias, exp_w_gate, exp_w_up, exp_w_down, sh_w_gate, sh_w_up,
           sh_w_down):
    batch, seq, _ = x_prompt.shape
    n_s = x_sample.shape[0]
    n_p = batch * seq
    tpb = seq // MIX_TILE
    xp = x_prompt.reshape(n_p, D)
    xs = x_sample.reshape(n_s, D)

    mod = _ada(jnp.concatenate([c_prompt, c_sample], axis=0), ada_w, ada_b)
    mod_p = [mod[i, :batch].reshape(batch, 1, 6 * D) for i in range(2)]
    mod_s = [mod[i, batch:] for i in range(2)]
    rw_t = [jnp.concatenate(_split3(router_w[i].T), axis=0) for i in range(2)]

    ws_causal = jnp.tril(gm_w_s[0]).astype(BF16)
    bs_cols = gm_b_s[0].T
    eye = jnp.eye(GM_CHUNK, dtype=F32)
    ws_first = (gm_w_s[0][:, 0, 0][:, None, None] * eye).astype(BF16)
    bs_first = jnp.broadcast_to(gm_b_s[0][:, 0][None, :], (GM_CHUNK, GM_GROUPS))
    gm_args = (norm_g[0], gm_w_in[0].astype(BF16), gm_b_in[0].reshape(1, -1), gm_ln_g[0].reshape(1, -1),
               gm_ln_b[0].reshape(1, -1))
    wout0 = gm_w_out[0].astype(BF16)
    shared = [(sh_w_gate[i].astype(BF16), sh_w_up[i].astype(BF16), sh_w_down[i].astype(BF16))
              for i in range(2)]
    n_qkvg = 2 * GLA_DK_TOT + 2 * GLA_DV_TOT
    wqkvg = gla_w_in[0][:, :n_qkvg].astype(BF16)
    wlr = jnp.pad(gla_w_in[0][:, n_qkvg:], ((0, 0), (0, LANES - GLA_GATE_RANK))).astype(BF16)
    wgk = jnp.pad(gla_w_gk[0], ((0, LANES - GLA_GATE_RANK), (0, 0))).astype(BF16)
    gla_args = (norm_g[1], wqkvg, wlr, wgk, gla_b_gk[0].reshape(1, -1), gla_norm_g[0].reshape(1, -1),
                gla_w_out[0].astype(BF16), rw_t[1])
    experts_f32 = (exp_w_gate, exp_w_up, exp_w_down)

    half = batch // 2
    streams = [(0, half, False), (half, batch - half, True)]
    st = [dict() for _ in streams]

    experts = []
    for layer, (s, (b0, nb, with_new)) in enumerate(zip(st, streams)):
        s["mod_p"] = [mod_p[i][b0:b0 + nb] for i in range(2)]
        s["n"] = nb * seq
        s["n_all"] = s["n"] + (n_s if with_new else 0)
        s["n_pad"] = -(-s["n_all"] // TOKEN_PAD) * TOKEN_PAD
        s["x1p"], s["h"], s["lg"], *w16 = _gmlp_mixer(
            xp, b0 * tpb, s["n"], s["mod_p"][0], False, MIX_TILE, tpb, *gm_args, ws_causal, bs_cols, wout0,
            rw_t[0], emit_v=False, cast_w=(layer, *experts_f32), ffn_rows=s["n_pad"])
        experts.append(w16)
        if with_new:
            s["x1s"], s["h"], s["lg"], v_rows = _gmlp_mixer(
                xs, 0, n_s, mod_s[0], True, n_s, 1, *gm_args, ws_first, bs_first, wout0, rw_t[0], emit_v=True,
                into=(s["h"], s["lg"], s["n"] // n_s))
    for s, (b0, nb, with_new) in zip(st, streams):
        s["routed0"] = _moe_routed(s["h"], s["lg"], s["n_all"], router_bias[0], *experts[0])
        if with_new:
            s["x2s"] = _combine(s["x1s"], s["routed0"], s["h"], s["n"] // n_s, mod_s[0], True, n_s, 1,
                                norm_g[0], *shared[0])
    for s, (b0, nb, with_new) in zip(st, streams):
        s["x3p"], h1, lg1, s["st_p"] = _gla_mixer(s["x1p"], s["routed0"], s["h"], s["mod_p"][0], norm_g[0],
                                                  shared[0], s["mod_p"][1], nb, seq, *gla_args,
                                                  ffn_rows=s["n_pad"])
        if with_new:
            s["x3s"], h1, lg1, st_s = _gla_mixer_one(s["x2s"], mod_s[1], state_gla[:, 0], *gla_args,
                                                     into=(h1, lg1, s["n"] // n_s))
        s["h"], s["lg"] = h1, lg1
    y_prompt = None
    for s, (b0, nb, with_new) in zip(st, streams):
        routed = _moe_routed(s["h"], s["lg"], s["n_all"], router_bias[1], *experts[1])
        y_prompt = _combine(s["x3p"], routed, s["h"], 0, s["mod_p"][1], False, MIX_TILE, tpb, norm_g[1],
                            *shared[1], out_rows=n_p, out_blk0=b0 * tpb, out_buf=y_prompt)
        if with_new:
            y_new = _combine(s["x3s"], routed, s["h"], s["n"] // n_s, mod_s[1], True, n_s, 1, norm_g[1],
                             *shared[1])
    st_p = jnp.concatenate([s["st_p"] for s in st], axis=0)

    return (y_prompt.reshape(batch, seq, D), y_new.reshape(n_s, 1, D), st_p[:, None], st_s[:, None],
            v_rows.reshape(n_s, 1, 1, GM_HALF))
```

```python
import functools
import math

import jax
import jax.numpy as jnp
from jax import lax
from jax.experimental import pallas as pl
from jax.experimental.pallas import tpu as pltpu
from jax.experimental.pallas import tpu_sc as plsc

F32 = jnp.float32
BF16 = jnp.bfloat16

D = 1024
DP = D // 2
GM_CHUNK = 128
GM_HALF = 2 * D
GM_GROUPS = 8
GM_GROUP_DIM = GM_HALF // GM_GROUPS
GLA_HEADS = 4
GLA_DK = 128
GLA_DV = 256
GLA_DK_TOT = GLA_HEADS * GLA_DK
GLA_DV_TOT = GLA_HEADS * GLA_DV
GLA_GATE_RANK = 16
GLA_GATE_NORMALIZER = 16.0
GLA_CHUNK = 64
N_EXPERTS = 64
TOP_K = 8
N_EXPERT_GROUPS = 8
GROUP_SIZE = N_EXPERTS // N_EXPERT_GROUPS
TOPK_GROUPS = 4
EXPERT_DIM = D // 4
ROUTED_SCALE = 2.5
NORM_EPS = 1e-6
LN_EPS = 1e-5

LANES = 128
VMEM_LIMIT = 56 * 1024 * 1024

MIX_TILE = 256
COMBINE_TILE = 512
GLA_TILE = 512
GLA_CUM_BLOCK = 128
GM_COL_BLOCK = 512
ROUTER_TILE = 1024
EXPERT_TILE = 544
EXPERT_X_SLOTS = 6
EXPERT_AHEAD = EXPERT_X_SLOTS - 2
EXPERT_Y_SLOTS = 4
SC_WORKERS = 32
DISPATCH_W = 32
SC_LANES = 16
SUM_W = 16
SUM_PARTS = 4
SUM_UNROLL = 4
TOKEN_PAD = SC_WORKERS * DISPATCH_W


def _cparams(*sem):
    return pltpu.CompilerParams(dimension_semantics=sem, vmem_limit_bytes=VMEM_LIMIT)


def _rms(x, g):
    return x * lax.rsqrt(jnp.mean(x * x, axis=-1, keepdims=True) + NORM_EPS) * g


def _silu(x):
    return x * (1.0 / (1.0 + jnp.exp(-x)))


def _gelu(x):
    return 0.5 * x * (1.0 + lax.erf(x * (1.0 / math.sqrt(2.0))))


def _bdot(a, b):
    return jnp.dot(a.astype(BF16), b.astype(BF16), preferred_element_type=F32)


def _dot_nt(a, b, precision=None):
    return lax.dot_general(a, b, (((1,), (1,)), ((), ())), preferred_element_type=F32,
                           precision=precision)


def _mod_slices(mod_ref):
    return [mod_ref[:, i * D:(i + 1) * D] for i in range(6)]


HI_HALF = -65536


def _pack_rows(x):
    lo = lax.bitcast_convert_type(x[:, :DP].astype(BF16).astype(F32), jnp.int32)
    hi = lax.bitcast_convert_type(x[:, DP:].astype(BF16).astype(F32), jnp.int32)
    return lax.shift_right_logical(lo, 16) | (hi & HI_HALF)


def _unpack_rows(p):
    lo = lax.bitcast_convert_type(lax.shift_left(p, 16), F32)
    hi = lax.bitcast_convert_type(p & HI_HALF, F32)
    return lo, hi


def _ffn_prep(x1, ng, sh2, sc2, rw_ref, h_ref, lg_ref, rows=slice(None)):
    hffn = _rms(x1, ng[2:3]) * (1.0 + sc2) + sh2
    h_ref[rows, :] = _pack_rows(hffn)
    lg3 = _dot_nt(rw_ref[...], hffn.astype(BF16))
    lg_ref[:, rows] = lg3[:N_EXPERTS] + lg3[N_EXPERTS:2 * N_EXPERTS] + lg3[2 * N_EXPERTS:]


def _ada_body(c_ref, w_ref, b_ref, o_ref):
    c = c_ref[...]
    o_ref[...] = _bdot(_silu(c), w_ref[...]) + b_ref[...]


def _ada(c, ada_w, ada_b):
    n = c.shape[0]
    depth = ada_w.shape[0]
    tn = 1536
    return pl.pallas_call(
        _ada_body,
        grid=(depth, 6 * D // tn),
        in_specs=[pl.BlockSpec((n, D), lambda l, j: (0, 0)),
                  pl.BlockSpec((None, D, tn), lambda l, j: (l, 0, j)),
                  pl.BlockSpec((None, 1, tn), lambda l, j: (l, 0, j))],
        out_specs=pl.BlockSpec((None, n, tn), lambda l, j: (l, 0, j)),
        out_shape=jax.ShapeDtypeStruct((depth, n, 6 * D), F32),
        compiler_params=_cparams("parallel", "parallel"),
        name="ada_mod",
    )(c, ada_w, ada_b.reshape(depth, 1, 6 * D))


def _mod_spec(per_row, tt, tiles_per_batch):
    if per_row:
        return pl.BlockSpec((tt, 6 * D), lambda i: (i, 0))
    return pl.BlockSpec((None, 1, 6 * D), lambda i: (i // tiles_per_batch, 0, 0))


def _const_spec(shape):
    zeros = (0,) * len(shape)
    return pl.BlockSpec(shape, lambda *_: zeros)


def _gmlp_body(x_ref, mod_ref, ng_ref, win_ref, bin_ref, lng_ref, lnb_ref, ws_ref, bs_ref, wout_ref,
               rw_ref, *rest, n_chunks, emit_v, cast_w, n_alias):
    rest = list(rest)
    w32_refs = [rest.pop(0) for _ in range(3)] if cast_w else []
    rest = rest[n_alias:]
    x1_ref, h_ref, lg_ref = rest[:3]
    rest = rest[3:]
    v_ref = rest.pop(0) if emit_v else None
    w16_refs = [rest.pop(0) for _ in range(3)] if cast_w else []
    um_ref, z_ref = rest
    for src, dst in zip(w32_refs, w16_refs):
        dst[...] = src[...].astype(BF16)
    sh1, sc1, g1, sh2, sc2, _ = _mod_slices(mod_ref)
    ng = ng_ref[...]
    x = x_ref[...]
    hb = (_rms(x, ng[0:1]) * (1.0 + sc1) + sh1).astype(BF16)
    for cb in range(2 * GM_HALF // GM_COL_BLOCK):
        cols = slice(cb * GM_COL_BLOCK, (cb + 1) * GM_COL_BLOCK)
        z_ref[:, cols] = _gelu(jnp.dot(hb, win_ref[:, cols], preferred_element_type=F32) + bin_ref[:, cols])
    u = z_ref[:, :GM_HALF]
    v = z_ref[:, GM_HALF:]
    mu = jnp.mean(v, axis=-1, keepdims=True)
    vc = v - mu
    var = jnp.mean(vc * vc, axis=-1, keepdims=True)
    v = vc * lax.rsqrt(var + LN_EPS) * lng_ref[...] + lnb_ref[...]
    if emit_v:
        v_ref[...] = v
    vb = v.astype(BF16)
    for c in range(n_chunks):
        rows = slice(c * GM_CHUNK, (c + 1) * GM_CHUNK)
        for g in range(GM_GROUPS):
            cols = slice(g * GM_GROUP_DIM, (g + 1) * GM_GROUP_DIM)
            mixed = jnp.dot(ws_ref[g], vb[rows, cols], preferred_element_type=F32) + bs_ref[:, g:g + 1]
            um_ref[rows, cols] = (u[rows, cols] * mixed).astype(BF16)
    y = jnp.dot(um_ref[...], wout_ref[...], preferred_element_type=F32)
    x1 = x + g1 * _rms(y, ng[1:2])
    x1_ref[...] = x1
    _ffn_prep(x1, ng, sh2, sc2, rw_ref, h_ref, lg_ref)


def _ffn_out(n, ffn_rows, into):
    rows = ffn_rows or n
    oblk = 0
    bufs = []
    if into is not None:
        *bufs, oblk = into
        rows = bufs[0].shape[0]
    elif rows != n:
        bufs = [jnp.zeros((rows, DP), jnp.int32), jnp.zeros((N_EXPERTS, rows), F32)]
    shapes = [jax.ShapeDtypeStruct((rows, DP), jnp.int32), jax.ShapeDtypeStruct((N_EXPERTS, rows), F32)]
    return shapes, bufs, oblk


def _gmlp_mixer(x2d, blk0, n, mod, per_row, tt, tiles_per_batch, ng, win, b_in, ln_g, ln_b, ws, bs, wout,
                rw_t, emit_v, cast_w=None, ffn_rows=None, into=None):
    steps = n // tt
    ffn_shapes, alias_bufs, oblk = _ffn_out(n, ffn_rows, into)
    out_shape = [jax.ShapeDtypeStruct((n, D), F32)] + ffn_shapes
    out_specs = [pl.BlockSpec((tt, D), lambda i: (i, 0)), pl.BlockSpec((tt, DP), lambda i: (i + oblk, 0)),
                 pl.BlockSpec((N_EXPERTS, tt), lambda i: (0, i + oblk))]
    if emit_v:
        out_shape.append(jax.ShapeDtypeStruct((n, GM_HALF), F32))
        out_specs.append(pl.BlockSpec((tt, GM_HALF), lambda i: (i, 0)))

    def one_buffer(a):
        zeros = (0,) * a.ndim
        return pl.BlockSpec(a.shape, lambda *_: zeros, pipeline_mode=pl.Buffered(1))

    consts = (ng, win, b_in, ln_g, ln_b, ws, bs, wout, rw_t)
    in_specs = [pl.BlockSpec((tt, D), lambda i: (i + blk0, 0)), _mod_spec(per_row, tt, tiles_per_batch)]
    in_specs += [one_buffer(a) for a in consts]
    args = [x2d, mod, *consts]
    if cast_w is not None:
        layer, *w_all = cast_w
        per_step = N_EXPERTS // steps
        for w in w_all:
            blk = (None, per_step) + w.shape[2:]
            in_specs.append(pl.BlockSpec(blk, lambda i: (layer, i, 0, 0)))
            out_specs.append(pl.BlockSpec(blk[1:], lambda i: (i, 0, 0)))
            out_shape.append(jax.ShapeDtypeStruct(w.shape[1:], BF16))
            args.append(w)
    aliases = {len(args) + i: 1 + i for i in range(len(alias_bufs))}
    in_specs += [pl.BlockSpec(memory_space=pl.ANY)] * len(alias_bufs)
    args += alias_bufs
    return pl.pallas_call(
        functools.partial(_gmlp_body, n_chunks=tt // GM_CHUNK, emit_v=emit_v, cast_w=cast_w is not None,
                          n_alias=len(alias_bufs)),
        grid=(steps,),
        in_specs=in_specs,
        out_specs=out_specs,
        out_shape=out_shape,
        input_output_aliases=aliases,
        scratch_shapes=[pltpu.VMEM((tt, GM_HALF), BF16), pltpu.VMEM((tt, 2 * GM_HALF), F32)],
        compiler_params=_cparams("parallel"),
        name="gmlp_mixer_rows" if per_row else "gmlp_mixer",
    )(*args)


def _combine_body(x_ref, y_ref, h_ref, mod_ref, ng_ref, swg_ref, swu_ref, swd_ref, *rest):
    o_ref = rest[-1]
    o_ref[...] = _channel_mix_residual(x_ref[...], y_ref[...], h_ref[...], mod_ref[:, 5 * D:6 * D],
                                       ng_ref[3:4, :], swg_ref, swu_ref, swd_ref)


def _combine(x2d, routed, hp, blk0, mod, per_row, tt, tiles_per_batch, ng, swg, swu, swd,
             out_rows=None, out_blk0=0, out_buf=None):
    n = x2d.shape[0]
    in_specs = [pl.BlockSpec((tt, D), lambda i: (i, 0)),
                pl.BlockSpec((tt, D), lambda i: (i + blk0, 0)),
                pl.BlockSpec((tt, DP), lambda i: (i + blk0, 0)),
                _mod_spec(per_row, tt, tiles_per_batch),
                _const_spec(ng.shape), _const_spec(swg.shape), _const_spec(swu.shape),
                _const_spec(swd.shape)]
    args = [x2d, routed, hp, mod, ng, swg, swu, swd]
    aliases = {}
    if out_buf is not None:
        in_specs.append(pl.BlockSpec(memory_space=pl.ANY))
        aliases = {len(args): 0}
        args.append(out_buf)
    return pl.pallas_call(
        _combine_body,
        grid=(n // tt,),
        in_specs=in_specs,
        out_specs=pl.BlockSpec((tt, D), lambda i: (i + out_blk0, 0)),
        out_shape=jax.ShapeDtypeStruct((out_rows or n, D), F32),
        input_output_aliases=aliases,
        compiler_params=_cparams("parallel"),
        name="combine_rows" if per_row else "combine",
    )(*args)


def _router_body(lg_ref, bias_ref, tri_ref, eid_ref, rank_ref, wts_ref, cnt_ref, carry_ref, *, n_real):
    step = pl.program_id(0)

    @pl.when(step == 0)
    def _():
        carry_ref[...] = jnp.zeros_like(carry_ref)

    lg = lg_ref[...]
    tn = lg.shape[1]
    real = (step * tn + lax.broadcasted_iota(jnp.int32, (1, tn), 1)) < n_real
    lg = jnp.where(real, lg, 0.0)
    scores = 1.0 / (1.0 + jnp.exp(-lg))
    sel = scores + bias_ref[...]
    neg = -jnp.inf
    sub8 = lax.broadcasted_iota(jnp.int32, (GROUP_SIZE, tn), 0)
    gsub = lax.broadcasted_iota(jnp.int32, (N_EXPERT_GROUPS, tn), 0)
    gs = jnp.zeros((N_EXPERT_GROUPS, tn), F32)
    for g in range(N_EXPERT_GROUPS):
        blk = sel[g * GROUP_SIZE:(g + 1) * GROUP_SIZE, :]
        m1 = jnp.max(blk, axis=0, keepdims=True)
        i1 = jnp.min(jnp.where(blk == m1, sub8, GROUP_SIZE), axis=0, keepdims=True)
        m2 = jnp.max(jnp.where(sub8 == i1, neg, blk), axis=0, keepdims=True)
        gs = jnp.where(gsub == g, m1 + m2, gs)
    gmask = jnp.zeros((N_EXPERT_GROUPS, tn), jnp.bool_)
    for _ in range(TOPK_GROUPS):
        m = jnp.max(gs, axis=0, keepdims=True)
        i = jnp.min(jnp.where(gs == m, gsub, N_EXPERT_GROUPS), axis=0, keepdims=True)
        hit = gsub == i
        gmask = jnp.logical_or(gmask, hit)
        gs = jnp.where(hit, neg, gs)
    gmaskf = gmask.astype(F32)
    blocks = []
    for g in range(N_EXPERT_GROUPS):
        keep = jnp.broadcast_to(gmaskf[g:g + 1, :], (GROUP_SIZE, tn)) > 0.5
        blocks.append(jnp.where(keep, sel[g * GROUP_SIZE:(g + 1) * GROUP_SIZE, :], neg))
    msel = jnp.concatenate(blocks, axis=0)
    esub = lax.broadcasted_iota(jnp.int32, (N_EXPERTS, tn), 0)
    chosen = jnp.zeros((N_EXPERTS, tn), jnp.bool_)
    picks = []
    for _ in range(TOP_K):
        m = jnp.max(msel, axis=0, keepdims=True)
        i = jnp.min(jnp.where(msel == m, esub, N_EXPERTS), axis=0, keepdims=True)
        hit = esub == i
        picks.append(i)
        chosen = jnp.logical_or(chosen, hit)
        msel = jnp.where(hit, neg, msel)
    w = jnp.where(chosen, scores, 0.0)
    w = w / jnp.sum(w, axis=0, keepdims=True) * ROUTED_SCALE
    counted = jnp.where(jnp.logical_and(chosen, real), 1.0, 0.0)
    incl = jnp.dot(counted.astype(BF16), tri_ref[...], preferred_element_type=F32)
    rank_full = carry_ref[:, 0:1] + incl - 1.0
    ksub = lax.broadcasted_iota(jnp.int32, (TOP_K, tn), 0)
    eid = jnp.zeros((TOP_K, tn), jnp.int32)
    rank = jnp.zeros((TOP_K, tn), F32)
    wts = jnp.zeros((TOP_K, tn), F32)
    for k in range(TOP_K):
        hit = esub == picks[k]
        eid = jnp.where(ksub == k, picks[k], eid)
        rank = jnp.where(ksub == k, jnp.sum(jnp.where(hit, rank_full, 0.0), axis=0, keepdims=True), rank)
        wts = jnp.where(ksub == k, jnp.sum(jnp.where(hit, w, 0.0), axis=0, keepdims=True), wts)
    eid_ref[...] = eid
    rank_ref[...] = rank.astype(jnp.int32)
    wts_ref[...] = wts
    carry = carry_ref[...] + incl[:, tn - 1:tn]
    carry_ref[...] = carry
    cnt_ref[...] = carry.astype(jnp.int32)


def _router(lg_t, bias, n_real):
    n = lg_t.shape[1]
    tn = ROUTER_TILE
    idx = jnp.arange(tn)
    tri = (idx[:, None] <= idx[None, :]).astype(BF16)
    kspec = pl.BlockSpec((TOP_K, tn), lambda i: (0, i))
    return pl.pallas_call(
        functools.partial(_router_body, n_real=n_real),
        grid=(n // tn,),
        in_specs=[pl.BlockSpec((N_EXPERTS, tn), lambda i: (0, i)), _const_spec((N_EXPERTS, 1)),
                  _const_spec((tn, tn))],
        out_specs=[kspec, kspec, kspec, _const_spec((N_EXPERTS, LANES))],
        out_shape=[jax.ShapeDtypeStruct((TOP_K, n), jnp.int32), jax.ShapeDtypeStruct((TOP_K, n), jnp.int32),
                   jax.ShapeDtypeStruct((TOP_K, n), F32), jax.ShapeDtypeStruct((N_EXPERTS, LANES), jnp.int32)],
        scratch_shapes=[pltpu.VMEM((N_EXPERTS, LANES), F32)],
        compiler_params=_cparams("arbitrary"),
        name="router",
    )(lg_t, bias.reshape(N_EXPERTS, 1), tri)


def _dest_body(off_ref, eid_ref, rank_ref, dest_ref, *, n_real, last_row):
    eid = eid_ref[...]
    base = jnp.zeros(eid.shape, jnp.int32)
    for e in range(N_EXPERTS):
        base = jnp.where(eid == e, off_ref[e], base)
    tok = lax.broadcasted_iota(jnp.int32, eid.shape, 1)
    slot = lax.broadcasted_iota(jnp.int32, eid.shape, 0)
    unused = last_row - ((tok - n_real) * TOP_K + slot)
    dest_ref[...] = jnp.where(tok < n_real, base + rank_ref[...], unused)


def _dest(off, eid, rank, n_real, last_row):
    spec = pl.BlockSpec(eid.shape, lambda i, off_ref: (0, 0))
    return pl.pallas_call(
        functools.partial(_dest_body, n_real=n_real, last_row=last_row),
        grid_spec=pltpu.PrefetchScalarGridSpec(num_scalar_prefetch=1, grid=(1,), in_specs=[spec, spec],
                                               out_specs=spec),
        out_shape=jax.ShapeDtypeStruct(eid.shape, jnp.int32),
        compiler_params=_cparams("arbitrary"),
        name="dest_rows",
    )(off, eid, rank)


def _sc_mesh():
    return plsc.VectorSubcoreMesh(core_axis_name="core", subcore_axis_name="subcore")


def _sc_dispatch(hp, dest_w, p_alloc):
    n = hp.shape[0]
    w = dest_w.shape[2]

    @functools.partial(pl.kernel, out_type=jax.ShapeDtypeStruct((p_alloc, DP), jnp.int32), mesh=_sc_mesh(),
                       name="sc_dispatch")
    def run(hp_hbm, dest_hbm, xs_hbm):
        def body(x_vmem, i_vmem):
            for k in range(TOP_K):
                pltpu.sync_copy(x_vmem, xs_hbm.at[i_vmem.at[k]])

        pltpu.emit_pipeline(
            body,
            grid=(n // w,),
            in_specs=[pl.BlockSpec((w, DP), lambda i: (i, 0)),
                      pl.BlockSpec((None, TOP_K, w), lambda i: (i, 0, 0))],
            out_specs=[],
            core_axis_name=("core", "subcore"),
            dimension_semantics=(pltpu.PARALLEL,),
        )(hp_hbm, dest_hbm)

    return run(hp, dest_w)


def _sc_gather_sum(ys, dest_tm, w_lanes):
    n_win, parts, pk = dest_tm.shape
    w = parts * pk // TOP_K
    wp = w // parts
    n_vec = DP // SC_LANES

    @functools.partial(pl.kernel, out_type=jax.ShapeDtypeStruct((n_win * w, D), F32), mesh=_sc_mesh(),
                       scratch_types=[pltpu.VMEM((parts, pk, DP), jnp.int32), pltpu.SemaphoreType.DMA((parts,))],
                       compiler_params=pltpu.CompilerParams(needs_layout_passes=False), name="sc_gather_sum")
    def run(ys_hbm, dest_hbm, w_hbm, o_hbm, rows_v, sems):
        def body(i_vmem, w_vmem, o_vmem):
            copies = [pltpu.async_copy(ys_hbm.at[i_vmem.at[p]], rows_v.at[p], sems.at[p]) for p in range(parts)]
            for p in range(parts):
                copies[p].wait()

                @pl.loop(0, wp)
                def _(t):
                    tok = p * wp + t
                    wv = [w_vmem[tok, pl.ds(k * SC_LANES, SC_LANES)] for k in range(TOP_K)]

                    @plsc.parallel_loop(0, n_vec, unroll=SUM_UNROLL)
                    def _(j):
                        col = j * SC_LANES
                        lo = jnp.zeros((SC_LANES,), F32)
                        hi = jnp.zeros((SC_LANES,), F32)
                        for k in range(TOP_K):
                            word = rows_v[p, t * TOP_K + k, pl.ds(col, SC_LANES)]
                            lo = lo + wv[k] * plsc.bitcast(lax.shift_left(word, 16), F32)
                            hi = hi + wv[k] * plsc.bitcast(word & HI_HALF, F32)
                        o_vmem[tok, pl.ds(col, SC_LANES)] = lo
                        o_vmem[tok, pl.ds(DP + col, SC_LANES)] = hi

        pltpu.emit_pipeline(
            body,
            grid=(n_win,),
            in_specs=[pl.BlockSpec((None, parts, pk), lambda i: (i, 0, 0)),
                      pl.BlockSpec((w, TOP_K * SC_LANES), lambda i: (i, 0))],
            out_specs=[pl.BlockSpec((w, D), lambda i: (i, 0))],
            core_axis_name=("core", "subcore"),
            dimension_semantics=(pltpu.PARALLEL,),
        )(dest_hbm, w_hbm, o_hbm)

    return run(ys, dest_tm, w_lanes)


def _expert_body(first_ref, cnt_ref, nused_ref, xs_hbm, wg_s, wu_s, wd_s, ys_hbm, xbuf, ybuf, xsem, ysem):
    e = pl.program_id(0)
    n_used = nused_ref[0]

    def load(g):
        rows = pl.ds(pl.multiple_of(g * EXPERT_TILE, EXPERT_TILE), EXPERT_TILE)
        slot = g % EXPERT_X_SLOTS
        return pltpu.make_async_copy(xs_hbm.at[rows], xbuf.at[slot], xsem.at[slot])

    def store(g):
        rows = pl.ds(pl.multiple_of(g * EXPERT_TILE, EXPERT_TILE), EXPERT_TILE)
        slot = g % EXPERT_Y_SLOTS
        return pltpu.make_async_copy(ybuf.at[slot], ys_hbm.at[rows], ysem.at[slot])

    @pl.when(e == 0)
    def _():
        for g in range(EXPERT_AHEAD):
            @pl.when(g < n_used)
            def _():
                load(g).start()

    first = first_ref[e]
    cnt = cnt_ref[e]

    def acquire(g):
        ahead = g + EXPERT_AHEAD

        @pl.when(ahead < n_used)
        def _():
            load(ahead).start()

        load(g).wait()

        @pl.when(g >= EXPERT_Y_SLOTS)
        def _():
            store(g - EXPERT_Y_SLOTS).wait()

    def compute(g):
        lo, hi = _unpack_rows(xbuf[g % EXPERT_X_SLOTS])
        lo = lo.astype(BF16)
        hi = hi.astype(BF16)

        def xdot(w_s):
            return (jnp.dot(lo, w_s[:DP, :], preferred_element_type=F32) +
                    jnp.dot(hi, w_s[DP:, :], preferred_element_type=F32))

        a = (_silu(xdot(wg_s)) * xdot(wu_s)).astype(BF16)
        ybuf[g % EXPERT_Y_SLOTS] = _pack_rows(jnp.dot(a, wd_s[...], preferred_element_type=F32))

    def pair(j, carry):
        g = first + 2 * j
        acquire(g)
        acquire(g + 1)
        compute(g)
        compute(g + 1)
        store(g).start()
        store(g + 1).start()
        return carry

    lax.fori_loop(0, cnt // 2, pair, 0)

    @pl.when(cnt % 2 == 1)
    def _():
        g = first + cnt - 1
        acquire(g)
        compute(g)
        store(g).start()

    @pl.when(e == N_EXPERTS - 1)
    def _():
        for k in range(EXPERT_Y_SLOTS):
            g = n_used - 1 - k

            @pl.when(g >= 0)
            def _():
                store(g).wait()


def _experts(xs, tile_first, tile_count, n_used, wg, wu, wd):
    def w_map(e, first, cnt, nu):
        return (e, 0, 0)

    return pl.pallas_call(
        _expert_body,
        grid_spec=pltpu.PrefetchScalarGridSpec(
            num_scalar_prefetch=3, grid=(N_EXPERTS,),
            in_specs=[pl.BlockSpec(memory_space=pl.ANY),
                      pl.BlockSpec((None, D, EXPERT_DIM), w_map),
                      pl.BlockSpec((None, D, EXPERT_DIM), w_map),
                      pl.BlockSpec((None, EXPERT_DIM, D), w_map)],
            out_specs=pl.BlockSpec(memory_space=pl.ANY),
            scratch_shapes=[pltpu.VMEM((EXPERT_X_SLOTS, EXPERT_TILE, DP), jnp.int32),
                            pltpu.VMEM((EXPERT_Y_SLOTS, EXPERT_TILE, DP), jnp.int32),
                            pltpu.SemaphoreType.DMA((EXPERT_X_SLOTS,)),
                            pltpu.SemaphoreType.DMA((EXPERT_Y_SLOTS,))]),
        out_shape=jax.ShapeDtypeStruct(xs.shape, jnp.int32),
        compiler_params=_cparams("arbitrary"),
        name="experts",
    )(tile_first, tile_count, n_used, xs, wg, wu, wd)


def _log_sigmoid(z):
    return jnp.minimum(z, 0.0) - jnp.log(1.0 + jnp.exp(-jnp.abs(z)))


def _gla_gate(hb, wlr_ref, wgk_ref, bgk_ref):
    lr = jnp.dot(hb, wlr_ref[...], preferred_element_type=F32)
    z = _bdot(lr, wgk_ref[...]) + bgk_ref[...]
    return _log_sigmoid(z) * (1.0 / GLA_GATE_NORMALIZER)


def _split3(a):
    hi = a.astype(BF16)
    r1 = a - hi.astype(F32)
    mid = r1.astype(BF16)
    lo = (r1 - mid.astype(F32)).astype(BF16)
    return hi, mid, lo


def _gla_out(o_ref_val, go, gng):
    parts = []
    for hd in range(GLA_HEADS):
        cols = slice(hd * GLA_DV, (hd + 1) * GLA_DV)
        parts.append((_rms(o_ref_val[:, cols], gng) * _silu(go[:, cols])).astype(BF16))
    return jnp.concatenate(parts, axis=1)


def _channel_mix_residual(x, routed, h_packed, g2, ng3, swg_ref, swu_ref, swd_ref):
    h_lo, h_hi = _unpack_rows(h_packed)
    h_lo = h_lo.astype(BF16)
    h_hi = h_hi.astype(BF16)

    def hdot(w_ref_):
        return (jnp.dot(h_lo, w_ref_[:DP, :], preferred_element_type=F32) +
                jnp.dot(h_hi, w_ref_[DP:, :], preferred_element_type=F32))

    hs = (_silu(hdot(swg_ref)) * hdot(swu_ref)).astype(BF16)
    y = jnp.dot(hs, swd_ref[...], preferred_element_type=F32) + routed
    return x + g2 * _rms(y, ng3)


def _gla_body(x_ref, y_ref, hprev_ref, modprev_ref, ngprev_ref, swg_ref, swu_ref, swd_ref,
              mod_ref, ng_ref, wqkvg_ref, wlr_ref, wgk_ref, bgk_ref, tril_ref, gng_ref, wout_ref,
              rw_ref, *rest, tt, n_alias):
    x1_ref, h_ref, lg_ref, st_ref, st_scr, o_scr, qd_scr, dst_scr = rest[n_alias:]
    j = pl.program_id(1)

    @pl.when(j == 0)
    def _():
        st_scr[...] = jnp.zeros_like(st_scr)

    x = _channel_mix_residual(x_ref[...], y_ref[...], hprev_ref[...], modprev_ref[:, 5 * D:6 * D],
                              ngprev_ref[3:4, :], swg_ref, swu_ref, swd_ref)
    sh1, sc1, g1, sh2, sc2, _ = _mod_slices(mod_ref)
    ng = ng_ref[...]
    hb = (_rms(x, ng[0:1]) * (1.0 + sc1) + sh1).astype(BF16)
    q = jnp.dot(hb, wqkvg_ref[:, :GLA_DK_TOT], preferred_element_type=F32) * (GLA_DK ** -0.5)
    k = jnp.dot(hb, wqkvg_ref[:, GLA_DK_TOT:2 * GLA_DK_TOT], preferred_element_type=F32)
    v = jnp.dot(hb, wqkvg_ref[:, 2 * GLA_DK_TOT:2 * GLA_DK_TOT + GLA_DV_TOT],
                preferred_element_type=F32).astype(BF16)
    log_a = _gla_gate(hb, wlr_ref, wgk_ref, bgk_ref)
    tril = tril_ref[...]
    parts = _split3(log_a)
    b = jnp.concatenate(
        [sum(jnp.dot(tril, p[r:r + GLA_CUM_BLOCK], preferred_element_type=F32) for p in parts)
         for r in range(0, tt, GLA_CUM_BLOCK)], axis=0)
    n_chunks = tt // GLA_CHUNK
    in_chunk_causal = tril > 0
    for r in range(0, tt, GLA_CUM_BLOCK):
        blk = slice(r, r + GLA_CUM_BLOCK)
        for hd in range(GLA_HEADS):
            kc = slice(hd * GLA_DK, (hd + 1) * GLA_DK)
            vc = slice(hd * GLA_DV, (hd + 1) * GLA_DV)
            bb = b[blk, kc]
            q_dec = (q[blk, kc] * jnp.exp(bb)).astype(BF16)
            k_inv = (k[blk, kc] * jnp.exp(-bb)).astype(BF16)
            att = jnp.where(in_chunk_causal, _dot_nt(q_dec, k_inv), 0.0).astype(BF16)
            qd_scr[blk, kc] = q_dec
            o_scr[blk, vc] = jnp.dot(att, v[blk, vc], preferred_element_type=F32)
    for c in range(n_chunks):
        rows = slice(c * GLA_CHUNK, (c + 1) * GLA_CHUNK)
        last = (c + 1) * GLA_CHUNK - 1
        for hd in range(GLA_HEADS):
            kc = slice(hd * GLA_DK, (hd + 1) * GLA_DK)
            vc = slice(hd * GLA_DV, (hd + 1) * GLA_DV)
            k_end = (k[rows, kc] * jnp.exp(b[last:last + 1, kc] - b[rows, kc])).astype(BF16)
            dst_scr[c * GLA_HEADS + hd] = lax.dot_general(
                v[rows, vc], k_end, (((0,), (0,)), ((), ())), preferred_element_type=F32)
    states = [st_scr[hd] for hd in range(GLA_HEADS)]
    for c in range(n_chunks):
        rows = slice(c * GLA_CHUNK, (c + 1) * GLA_CHUNK)
        last = (c + 1) * GLA_CHUNK - 1
        for hd in range(GLA_HEADS):
            kc = slice(hd * GLA_DK, (hd + 1) * GLA_DK)
            vc = slice(hd * GLA_DV, (hd + 1) * GLA_DV)
            o_scr[rows, vc] += _dot_nt(qd_scr[rows, kc], states[hd].astype(BF16))
            states[hd] = states[hd] * jnp.exp(b[last:last + 1, kc]) + dst_scr[c * GLA_HEADS + hd]
    for hd in range(GLA_HEADS):
        st_scr[hd] = states[hd]

    @pl.when(j == pl.num_programs(1) - 1)
    def _():
        for hd in range(GLA_HEADS):
            st_ref[hd] = st_scr[hd].T

    go = jnp.dot(hb, wqkvg_ref[:, 2 * GLA_DK_TOT + GLA_DV_TOT:], preferred_element_type=F32)
    y = jnp.dot(_gla_out(o_scr[...], go, gng_ref[...]), wout_ref[...], preferred_element_type=F32)
    x1 = x + g1 * _rms(y, ng[1:2])
    x1_ref[...] = x1
    _ffn_prep(x1, ng, sh2, sc2, rw_ref, h_ref, lg_ref)


def _gla_mixer(x2d, routed, hp, mod3_prev, ng_prev, shared_prev, mod3, batch, seq, ng, wqkvg, wlr, wgk, bgk,
               gng, wout, rw_t, ffn_rows=None):
    tt = GLA_TILE
    tpb = seq // tt
    n = x2d.shape[0]
    idx = jnp.arange(GLA_CUM_BLOCK)
    tril = ((idx[:, None] >= idx[None, :]) &
            (idx[:, None] // GLA_CHUNK == idx[None, :] // GLA_CHUNK)).astype(BF16)
    row_map = lambda b, j: (b * tpb + j, 0)
    mod_map = lambda b, j: (b, 0, 0)
    consts = (ng, wqkvg, wlr, wgk, bgk, tril, gng, wout, rw_t)
    prev_consts = (ng_prev,) + tuple(shared_prev)
    ffn_shapes, alias_bufs, _ = _ffn_out(n, ffn_rows, None)
    args = [x2d, routed, hp, mod3_prev, *prev_consts, mod3, *consts]
    return pl.pallas_call(
        functools.partial(_gla_body, tt=tt, n_alias=len(alias_bufs)),
        grid=(batch, tpb),
        in_specs=[pl.BlockSpec((tt, D), row_map), pl.BlockSpec((tt, D), row_map),
                  pl.BlockSpec((tt, DP), row_map), pl.BlockSpec((None, 1, 6 * D), mod_map)] +
                 [_const_spec(a.shape) for a in prev_consts] +
                 [pl.BlockSpec((None, 1, 6 * D), mod_map)] +
                 [_const_spec(a.shape) for a in consts] +
                 [pl.BlockSpec(memory_space=pl.ANY)] * len(alias_bufs),
        out_specs=[pl.BlockSpec((tt, D), row_map), pl.BlockSpec((tt, DP), row_map),
                   pl.BlockSpec((N_EXPERTS, tt), lambda b, j: (0, b * tpb + j)),
                   pl.BlockSpec((None, GLA_HEADS, GLA_DK, GLA_DV), lambda b, j: (b, 0, 0, 0))],
        out_shape=[jax.ShapeDtypeStruct((n, D), F32)] + ffn_shapes +
                  [jax.ShapeDtypeStruct((batch, GLA_HEADS, GLA_DK, GLA_DV), F32)],
        input_output_aliases={len(args) + i: 1 + i for i in range(len(alias_bufs))},
        scratch_shapes=[pltpu.VMEM((GLA_HEADS, GLA_DV, GLA_DK), F32),
                        pltpu.VMEM((tt, GLA_DV_TOT), F32),
                        pltpu.VMEM((tt, GLA_DK_TOT), BF16),
                        pltpu.VMEM((tt // GLA_CHUNK * GLA_HEADS, GLA_DV, GLA_DK), F32)],
        compiler_params=_cparams("parallel", "arbitrary"),
        name="gla_mixer",
    )(*args, *alias_bufs)


def _gla1_proj_body(x_ref, mod_ref, ng_ref, wqkvg_ref, wlr_ref, wgk_ref, bgk_ref,
                    q_ref, k_ref, v_ref, go_ref, dec_ref):
    sh1, sc1, _, _, _, _ = _mod_slices(mod_ref)
    ng = ng_ref[...]
    hb = (_rms(x_ref[...], ng[0:1]) * (1.0 + sc1) + sh1).astype(BF16)
    proj = jnp.dot(hb, wqkvg_ref[...], preferred_element_type=F32)
    q_ref[...] = proj[:, :GLA_DK_TOT] * (GLA_DK ** -0.5)
    k_ref[...] = proj[:, GLA_DK_TOT:2 * GLA_DK_TOT]
    v_ref[...] = proj[:, 2 * GLA_DK_TOT:2 * GLA_DK_TOT + GLA_DV_TOT]
    go_ref[...] = proj[:, 2 * GLA_DK_TOT + GLA_DV_TOT:]
    dec_ref[...] = jnp.exp(_gla_gate(hb, wlr_ref, wgk_ref, bgk_ref))


GLA1_TOK = 16


def _gla1_state_body(st_ref, qc_ref, kc_ref, dc_ref, v_ref, nst_ref, o_ref):
    v = v_ref[...]
    for i in range(GLA1_TOK):
        for hd in range(GLA_HEADS):
            vrow = v[i:i + 1, hd * GLA_DV:(hd + 1) * GLA_DV]
            s_new = dc_ref[hd][:, i:i + 1] * st_ref[i, hd] + kc_ref[hd][:, i:i + 1] * vrow
            nst_ref[i, hd] = s_new
            o_ref[i:i + 1, hd * GLA_DV:(hd + 1) * GLA_DV] = jnp.sum(
                qc_ref[hd][:, i:i + 1] * s_new, axis=0, keepdims=True)


def _gla1_out_body(x_ref, o_ref, go_ref, mod_ref, ng_ref, gng_ref, wout_ref, rw_ref, *rest):
    x1_ref, h_ref, lg_ref = rest[-3:]
    _, _, g1, sh2, sc2, _ = _mod_slices(mod_ref)
    ng = ng_ref[...]
    y = jnp.dot(_gla_out(o_ref[...], go_ref[...], gng_ref[...]), wout_ref[...], preferred_element_type=F32)
    x1 = x_ref[...] + g1 * _rms(y, ng[1:2])
    x1_ref[...] = x1
    _ffn_prep(x1, ng, sh2, sc2, rw_ref, h_ref, lg_ref)


def _gla_mixer_one(x2d, mod2, state, ng, wqkvg, wlr, wgk, bgk, gng, wout, rw_t, into=None):
    n = x2d.shape[0]
    consts = (ng, wqkvg, wlr, wgk, bgk)
    q, k, v, go, dec = pl.pallas_call(
        _gla1_proj_body,
        in_specs=[_const_spec(a.shape) for a in (x2d, mod2) + consts],
        out_specs=[_const_spec((n, GLA_DK_TOT)), _const_spec((n, GLA_DK_TOT)), _const_spec((n, GLA_DV_TOT)),
                   _const_spec((n, GLA_DV_TOT)), _const_spec((n, GLA_DK_TOT))],
        out_shape=[jax.ShapeDtypeStruct((n, GLA_DK_TOT), F32), jax.ShapeDtypeStruct((n, GLA_DK_TOT), F32),
                   jax.ShapeDtypeStruct((n, GLA_DV_TOT), F32), jax.ShapeDtypeStruct((n, GLA_DV_TOT), F32),
                   jax.ShapeDtypeStruct((n, GLA_DK_TOT), F32)],
        grid=(1,),
        compiler_params=_cparams("arbitrary"),
        name="gla1_proj",
    )(x2d, mod2, *consts)

    def cols(a):
        return a.reshape(n // GLA1_TOK, GLA1_TOK, GLA_HEADS, GLA_DK).transpose(0, 2, 3, 1)

    col_spec = pl.BlockSpec((None, GLA_HEADS, GLA_DK, GLA1_TOK), lambda i: (i, 0, 0, 0))
    st_spec = pl.BlockSpec((GLA1_TOK, GLA_HEADS, GLA_DK, GLA_DV), lambda i: (i, 0, 0, 0))
    new_state, o = pl.pallas_call(
        _gla1_state_body,
        grid=(n // GLA1_TOK,),
        in_specs=[st_spec, col_spec, col_spec, col_spec, pl.BlockSpec((GLA1_TOK, GLA_DV_TOT), lambda i: (i, 0))],
        out_specs=[st_spec, pl.BlockSpec((GLA1_TOK, GLA_DV_TOT), lambda i: (i, 0))],
        out_shape=[jax.ShapeDtypeStruct(state.shape, F32), jax.ShapeDtypeStruct((n, GLA_DV_TOT), F32)],
        compiler_params=_cparams("parallel"),
        name="gla1_state",
    )(state, cols(q), cols(k), cols(dec), v)

    consts = (mod2, ng, gng, wout, rw_t)
    ffn_shapes, alias_bufs, oblk = _ffn_out(n, None, into)
    n_in = 3 + len(consts)
    x1, h, lg = pl.pallas_call(
        _gla1_out_body,
        grid=(1,),
        in_specs=[_const_spec(a.shape) for a in (x2d, o, go) + consts] +
                 [pl.BlockSpec(memory_space=pl.ANY)] * len(alias_bufs),
        out_specs=[_const_spec((n, D)), pl.BlockSpec((n, DP), lambda i: (oblk, 0)),
                   pl.BlockSpec((N_EXPERTS, n), lambda i: (0, oblk))],
        out_shape=[jax.ShapeDtypeStruct((n, D), F32)] + ffn_shapes,
        input_output_aliases={n_in + i: 1 + i for i in range(len(alias_bufs))},
        compiler_params=_cparams("arbitrary"),
        name="gla1_out",
    )(x2d, o, go, *consts, *alias_bufs)
    return x1, h, lg, new_state


def _moe_routed(h, lg, n, router_bias, wg, wu, wd):
    n_pad = h.shape[0]
    eid, rank, wts, counts = _router(lg, router_bias, n)
    tile_count = ((counts[:, 0] + EXPERT_TILE - 1) // EXPERT_TILE).astype(jnp.int32)
    tile_end = jnp.cumsum(tile_count).astype(jnp.int32)
    tile_first = tile_end - tile_count
    off = tile_first * EXPERT_TILE
    p_alloc = TOP_K * n_pad + N_EXPERTS * EXPERT_TILE
    dest = _dest(off, eid, rank, n, p_alloc - 1)
    dest_w = dest.reshape(TOP_K, n_pad // DISPATCH_W, DISPATCH_W).transpose(1, 0, 2)
    xs = _sc_dispatch(h, dest_w, p_alloc)
    ys = _experts(xs, tile_first, tile_count, tile_end[-1:], wg, wu, wd)
    dest_tm = dest.T.reshape(n_pad // SUM_W, SUM_PARTS, SUM_W * TOP_K // SUM_PARTS)
    w_lanes = jnp.repeat(wts.T, SC_LANES, axis=1)
    return _sc_gather_sum(ys, dest_tm, w_lanes)


def kernel(x_prompt, x_sample, state_gla, c_prompt, c_sample, norm_g, ada_w, ada_b, gm_w_in, gm_b_in,
           gm_ln_g, gm_ln_b, gm_w_s, gm_b_s, gm_w_out, gla_w_in, gla_w_gk, gla_b_gk, gla_norm_g,
           gla_w_out, router_w, router_bias, exp_w_gate, exp_w_up, exp_w_down, sh_w_gate, sh_w_up,
           sh_w_down):
    batch, seq, _ = x_prompt.shape
    n_s = x_sample.shape[0]
    n_p = batch * seq
    tpb = seq // MIX_TILE
    xp = x_prompt.reshape(n_p, D)
    xs = x_sample.reshape(n_s, D)

    mod = _ada(jnp.concatenate([c_prompt, c_sample], axis=0), ada_w, ada_b)
    mod_p = [mod[i, :batch].reshape(batch, 1, 6 * D) for i in range(2)]
    mod_s = [mod[i, batch:] for i in range(2)]
    rw_t = [jnp.concatenate(_split3(router_w[i].T), axis=0) for i in range(2)]

    ws_causal = jnp.tril(gm_w_s[0]).astype(BF16)
    bs_cols = gm_b_s[0].T
    eye = jnp.eye(GM_CHUNK, dtype=F32)
    ws_first = (gm_w_s[0][:, 0, 0][:, None, None] * eye).astype(BF16)
    bs_first = jnp.broadcast_to(gm_b_s[0][:, 0][None, :], (GM_CHUNK, GM_GROUPS))
    gm_args = (norm_g[0], gm_w_in[0].astype(BF16), gm_b_in[0].reshape(1, -1), gm_ln_g[0].reshape(1, -1),
               gm_ln_b[0].reshape(1, -1))
    wout0 = gm_w_out[0].astype(BF16)
    shared = [(sh_w_gate[i].astype(BF16), sh_w_up[i].astype(BF16), sh_w_down[i].astype(BF16))
              for i in range(2)]
    n_qkvg = 2 * GLA_DK_TOT + 2 * GLA_DV_TOT
    wqkvg = gla_w_in[0][:, :n_qkvg].astype(BF16)
    wlr = jnp.pad(gla_w_in[0][:, n_qkvg:], ((0, 0), (0, LANES - GLA_GATE_RANK))).astype(BF16)
    wgk = jnp.pad(gla_w_gk[0], ((0, LANES - GLA_GATE_RANK), (0, 0))).astype(BF16)
    gla_args = (norm_g[1], wqkvg, wlr, wgk, gla_b_gk[0].reshape(1, -1), gla_norm_g[0].reshape(1, -1),
                gla_w_out[0].astype(BF16), rw_t[1])
    experts_f32 = (exp_w_gate, exp_w_up, exp_w_down)

    half = batch // 2
    streams = [(0, half, False), (half, batch - half, True)]
    st = [dict() for _ in streams]

    experts = []
    for layer, (s, (b0, nb, with_new)) in enumerate(zip(st, streams)):
        s["mod_p"] = [mod_p[i][b0:b0 + nb] for i in range(2)]
        s["n"] = nb * seq
        s["n_all"] = s["n"] + (n_s if with_new else 0)
        s["n_pad"] = -(-s["n_all"] // TOKEN_PAD) * TOKEN_PAD
        s["x1p"], s["h"], s["lg"], *w16 = _gmlp_mixer(
            xp, b0 * tpb, s["n"], s["mod_p"][0], False, MIX_TILE, tpb, *gm_args, ws_causal, bs_cols, wout0,
            rw_t[0], emit_v=False, cast_w=(layer, *experts_f32), ffn_rows=s["n_pad"])
        experts.append(w16)
        if with_new:
            s["x1s"], s["h"], s["lg"], v_rows = _gmlp_mixer(
                xs, 0, n_s, mod_s[0], True, n_s, 1, *gm_args, ws_first, bs_first, wout0, rw_t[0], emit_v=True,
                into=(s["h"], s["lg"], s["n"] // n_s))
    for s, (b0, nb, with_new) in zip(st, streams):
        s["routed0"] = _moe_routed(s["h"], s["lg"], s["n_all"], router_bias[0], *experts[0])
        if with_new:
            s["x2s"] = _combine(s["x1s"], s["routed0"], s["h"], s["n"] // n_s, mod_s[0], True, n_s, 1,
                                norm_g[0], *shared[0])
    for s, (b0, nb, with_new) in zip(st, streams):
        s["x3p"], h1, lg1, s["st_p"] = _gla_mixer(s["x1p"], s["routed0"], s["h"], s["mod_p"][0], norm_g[0],
                                                  shared[0], s["mod_p"][1], nb, seq, *gla_args,
                                                  ffn_rows=s["n_pad"])
        if with_new:
            s["x3s"], h1, lg1, st_s = _gla_mixer_one(s["x2s"], mod_s[1], state_gla[:, 0], *gla_args,
                                                     into=(h1, lg1, s["n"] // n_s))
        s["h"], s["lg"] = h1, lg1
    y_prompt = None
    for s, (b0, nb, with_new) in zip(st, streams):
        routed = _moe_routed(s["h"], s["lg"], s["n_all"], router_bias[1], *experts[1])
        ctpb = seq // COMBINE_TILE
        y_prompt = _combine(s["x3p"], routed, s["h"], 0, s["mod_p"][1], False, COMBINE_TILE, ctpb, norm_g[1],
                            *shared[1], out_rows=n_p, out_blk0=b0 * ctpb, out_buf=y_prompt)
        if with_new:
            y_new = _combine(s["x3s"], routed, s["h"], s["n"] // n_s, mod_s[1], True, n_s, 1, norm_g[1],
                             *shared[1])
    st_p = jnp.concatenate([s["st_p"] for s in st], axis=0)

    return (y_prompt.reshape(batch, seq, D), y_new.reshape(n_s, 1, D), st_p[:, None], st_s[:, None],
            v_rows.reshape(n_s, 1, 1, GM_HALF))
```

```python
import functools
import math

import jax
import jax.numpy as jnp
from jax import lax
from jax.experimental import pallas as pl
from jax.experimental.pallas import tpu as pltpu
from jax.experimental.pallas import tpu_sc as plsc

F32 = jnp.float32
BF16 = jnp.bfloat16

D = 1024
DP = D // 2
GM_CHUNK = 128
GM_HALF = 2 * D
GM_GROUPS = 8
GM_GROUP_DIM = GM_HALF // GM_GROUPS
GLA_HEADS = 4
GLA_DK = 128
GLA_DV = 256
GLA_DK_TOT = GLA_HEADS * GLA_DK
GLA_DV_TOT = GLA_HEADS * GLA_DV
GLA_GATE_RANK = 16
GLA_GATE_NORMALIZER = 16.0
GLA_CHUNK = 64
N_EXPERTS = 64
TOP_K = 8
N_EXPERT_GROUPS = 8
GROUP_SIZE = N_EXPERTS // N_EXPERT_GROUPS
TOPK_GROUPS = 4
EXPERT_DIM = D // 4
ROUTED_SCALE = 2.5
NORM_EPS = 1e-6
LN_EPS = 1e-5

LANES = 128
VMEM_LIMIT = 56 * 1024 * 1024

MIX_TILE = 256
COMBINE_TILE = 1024
GLA_TILE = 512
GLA_CUM_BLOCK = 128
GM_COL_BLOCK = 512
ROUTER_TILE = 1024
EXPERT_TILE = 544
EXPERT_X_SLOTS = 6
EXPERT_AHEAD = EXPERT_X_SLOTS - 2
EXPERT_Y_SLOTS = 4
SC_WORKERS = 32
DISPATCH_W = 32
SC_LANES = 16
SUM_W = 16
SUM_PARTS = 4
SUM_UNROLL = 4
TOKEN_PAD = SC_WORKERS * DISPATCH_W


def _cparams(*sem):
    return pltpu.CompilerParams(dimension_semantics=sem, vmem_limit_bytes=VMEM_LIMIT)


def _rms(x, g):
    return x * lax.rsqrt(jnp.mean(x * x, axis=-1, keepdims=True) + NORM_EPS) * g


def _silu(x):
    return x * (1.0 / (1.0 + jnp.exp(-x)))


def _gelu(x):
    return 0.5 * x * (1.0 + lax.erf(x * (1.0 / math.sqrt(2.0))))


def _bdot(a, b):
    return jnp.dot(a.astype(BF16), b.astype(BF16), preferred_element_type=F32)


def _dot_nt(a, b, precision=None):
    return lax.dot_general(a, b, (((1,), (1,)), ((), ())), preferred_element_type=F32,
                           precision=precision)


def _mod_slices(mod_ref):
    return [mod_ref[:, i * D:(i + 1) * D] for i in range(6)]


HI_HALF = -65536


def _pack_rows(x):
    lo = lax.bitcast_convert_type(x[:, :DP].astype(BF16).astype(F32), jnp.int32)
    hi = lax.bitcast_convert_type(x[:, DP:].astype(BF16).astype(F32), jnp.int32)
    return lax.shift_right_logical(lo, 16) | (hi & HI_HALF)


def _unpack_rows(p):
    lo = lax.bitcast_convert_type(lax.shift_left(p, 16), F32)
    hi = lax.bitcast_convert_type(p & HI_HALF, F32)
    return lo, hi


def _ffn_prep(x1, ng, sh2, sc2, rw_ref, h_ref, lg_ref, rows=slice(None)):
    hffn = _rms(x1, ng[2:3]) * (1.0 + sc2) + sh2
    h_ref[rows, :] = _pack_rows(hffn)
    lg3 = _dot_nt(rw_ref[...], hffn.astype(BF16))
    lg_ref[:, rows] = lg3[:N_EXPERTS] + lg3[N_EXPERTS:2 * N_EXPERTS] + lg3[2 * N_EXPERTS:]


def _ada_body(c_ref, w_ref, b_ref, o_ref):
    c = c_ref[...]
    o_ref[...] = _bdot(_silu(c), w_ref[...]) + b_ref[...]


def _ada(c, ada_w, ada_b):
    n = c.shape[0]
    depth = ada_w.shape[0]
    tn = 1536
    return pl.pallas_call(
        _ada_body,
        grid=(depth, 6 * D // tn),
        in_specs=[pl.BlockSpec((n, D), lambda l, j: (0, 0)),
                  pl.BlockSpec((None, D, tn), lambda l, j: (l, 0, j)),
                  pl.BlockSpec((None, 1, tn), lambda l, j: (l, 0, j))],
        out_specs=pl.BlockSpec((None, n, tn), lambda l, j: (l, 0, j)),
        out_shape=jax.ShapeDtypeStruct((depth, n, 6 * D), F32),
        compiler_params=_cparams("parallel", "parallel"),
        name="ada_mod",
    )(c, ada_w, ada_b.reshape(depth, 1, 6 * D))


def _mod_spec(per_row, tt, tiles_per_batch):
    if per_row:
        return pl.BlockSpec((tt, 6 * D), lambda i: (i, 0))
    return pl.BlockSpec((None, 1, 6 * D), lambda i: (i // tiles_per_batch, 0, 0))


def _const_spec(shape):
    zeros = (0,) * len(shape)
    return pl.BlockSpec(shape, lambda *_: zeros)


def _gmlp_body(x_ref, mod_ref, ng_ref, win_ref, bin_ref, lng_ref, lnb_ref, ws_ref, bs_ref, wout_ref,
               rw_ref, *rest, n_chunks, emit_v, cast_w, n_alias):
    rest = list(rest)
    w32_refs = [rest.pop(0) for _ in range(3)] if cast_w else []
    rest = rest[n_alias:]
    x1_ref, h_ref, lg_ref = rest[:3]
    rest = rest[3:]
    v_ref = rest.pop(0) if emit_v else None
    w16_refs = [rest.pop(0) for _ in range(3)] if cast_w else []
    um_ref, z_ref = rest
    for src, dst in zip(w32_refs, w16_refs):
        dst[...] = src[...].astype(BF16)
    sh1, sc1, g1, sh2, sc2, _ = _mod_slices(mod_ref)
    ng = ng_ref[...]
    x = x_ref[...]
    hb = (_rms(x, ng[0:1]) * (1.0 + sc1) + sh1).astype(BF16)
    for cb in range(2 * GM_HALF // GM_COL_BLOCK):
        cols = slice(cb * GM_COL_BLOCK, (cb + 1) * GM_COL_BLOCK)
        z_ref[:, cols] = _gelu(jnp.dot(hb, win_ref[:, cols], preferred_element_type=F32) + bin_ref[:, cols])
    u = z_ref[:, :GM_HALF]
    v = z_ref[:, GM_HALF:]
    mu = jnp.mean(v, axis=-1, keepdims=True)
    vc = v - mu
    var = jnp.mean(vc * vc, axis=-1, keepdims=True)
    v = vc * lax.rsqrt(var + LN_EPS) * lng_ref[...] + lnb_ref[...]
    if emit_v:
        v_ref[...] = v
    vb = v.astype(BF16)
    for c in range(n_chunks):
        rows = slice(c * GM_CHUNK, (c + 1) * GM_CHUNK)
        for g in range(GM_GROUPS):
            cols = slice(g * GM_GROUP_DIM, (g + 1) * GM_GROUP_DIM)
            mixed = jnp.dot(ws_ref[g], vb[rows, cols], preferred_element_type=F32) + bs_ref[:, g:g + 1]
            um_ref[rows, cols] = (u[rows, cols] * mixed).astype(BF16)
    y = jnp.dot(um_ref[...], wout_ref[...], preferred_element_type=F32)
    x1 = x + g1 * _rms(y, ng[1:2])
    x1_ref[...] = x1
    _ffn_prep(x1, ng, sh2, sc2, rw_ref, h_ref, lg_ref)


def _ffn_out(n, ffn_rows, into):
    rows = ffn_rows or n
    oblk = 0
    bufs = []
    if into is not None:
        *bufs, oblk = into
        rows = bufs[0].shape[0]
    elif rows != n:
        bufs = [jnp.zeros((rows, DP), jnp.int32), jnp.zeros((N_EXPERTS, rows), F32)]
    shapes = [jax.ShapeDtypeStruct((rows, DP), jnp.int32), jax.ShapeDtypeStruct((N_EXPERTS, rows), F32)]
    return shapes, bufs, oblk


def _gmlp_mixer(x2d, blk0, n, mod, per_row, tt, tiles_per_batch, ng, win, b_in, ln_g, ln_b, ws, bs, wout,
                rw_t, emit_v, cast_w=None, ffn_rows=None, into=None):
    steps = n // tt
    ffn_shapes, alias_bufs, oblk = _ffn_out(n, ffn_rows, into)
    out_shape = [jax.ShapeDtypeStruct((n, D), F32)] + ffn_shapes
    out_specs = [pl.BlockSpec((tt, D), lambda i: (i, 0)), pl.BlockSpec((tt, DP), lambda i: (i + oblk, 0)),
                 pl.BlockSpec((N_EXPERTS, tt), lambda i: (0, i + oblk))]
    if emit_v:
        out_shape.append(jax.ShapeDtypeStruct((n, GM_HALF), F32))
        out_specs.append(pl.BlockSpec((tt, GM_HALF), lambda i: (i, 0)))

    def one_buffer(a):
        zeros = (0,) * a.ndim
        return pl.BlockSpec(a.shape, lambda *_: zeros, pipeline_mode=pl.Buffered(1))

    consts = (ng, win, b_in, ln_g, ln_b, ws, bs, wout, rw_t)
    in_specs = [pl.BlockSpec((tt, D), lambda i: (i + blk0, 0)), _mod_spec(per_row, tt, tiles_per_batch)]
    in_specs += [one_buffer(a) for a in consts]
    args = [x2d, mod, *consts]
    if cast_w is not None:
        layer, *w_all = cast_w
        per_step = N_EXPERTS // steps
        for w in w_all:
            blk = (None, per_step) + w.shape[2:]
            in_specs.append(pl.BlockSpec(blk, lambda i: (layer, i, 0, 0)))
            out_specs.append(pl.BlockSpec(blk[1:], lambda i: (i, 0, 0)))
            out_shape.append(jax.ShapeDtypeStruct(w.shape[1:], BF16))
            args.append(w)
    aliases = {len(args) + i: 1 + i for i in range(len(alias_bufs))}
    in_specs += [pl.BlockSpec(memory_space=pl.ANY)] * len(alias_bufs)
    args += alias_bufs
    return pl.pallas_call(
        functools.partial(_gmlp_body, n_chunks=tt // GM_CHUNK, emit_v=emit_v, cast_w=cast_w is not None,
                          n_alias=len(alias_bufs)),
        grid=(steps,),
        in_specs=in_specs,
        out_specs=out_specs,
        out_shape=out_shape,
        input_output_aliases=aliases,
        scratch_shapes=[pltpu.VMEM((tt, GM_HALF), BF16), pltpu.VMEM((tt, 2 * GM_HALF), F32)],
        compiler_params=_cparams("parallel"),
        name="gmlp_mixer_rows" if per_row else "gmlp_mixer",
    )(*args)


def _combine_body(x_ref, y_ref, h_ref, mod_ref, ng_ref, swg_ref, swu_ref, swd_ref, *rest):
    o_ref = rest[-1]
    o_ref[...] = _channel_mix_residual(x_ref[...], y_ref[...], h_ref[...], mod_ref[:, 5 * D:6 * D],
                                       ng_ref[3:4, :], swg_ref, swu_ref, swd_ref)


def _combine(x2d, routed, hp, blk0, mod, per_row, tt, tiles_per_batch, ng, swg, swu, swd,
             out_rows=None, out_blk0=0, out_buf=None):
    n = x2d.shape[0]
    in_specs = [pl.BlockSpec((tt, D), lambda i: (i, 0)),
                pl.BlockSpec((tt, D), lambda i: (i + blk0, 0)),
                pl.BlockSpec((tt, DP), lambda i: (i + blk0, 0)),
                _mod_spec(per_row, tt, tiles_per_batch),
                _const_spec(ng.shape), _const_spec(swg.shape), _const_spec(swu.shape),
                _const_spec(swd.shape)]
    args = [x2d, routed, hp, mod, ng, swg, swu, swd]
    aliases = {}
    if out_buf is not None:
        in_specs.append(pl.BlockSpec(memory_space=pl.ANY))
        aliases = {len(args): 0}
        args.append(out_buf)
    return pl.pallas_call(
        _combine_body,
        grid=(n // tt,),
        in_specs=in_specs,
        out_specs=pl.BlockSpec((tt, D), lambda i: (i + out_blk0, 0)),
        out_shape=jax.ShapeDtypeStruct((out_rows or n, D), F32),
        input_output_aliases=aliases,
        compiler_params=_cparams("parallel"),
        name="combine_rows" if per_row else "combine",
    )(*args)


def _router_body(lg_ref, bias_ref, tri_ref, eid_ref, rank_ref, wts_ref, cnt_ref, carry_ref, *, n_real):
    step = pl.program_id(0)

    @pl.when(step == 0)
    def _():
        carry_ref[...] = jnp.zeros_like(carry_ref)

    lg = lg_ref[...]
    tn = lg.shape[1]
    real = (step * tn + lax.broadcasted_iota(jnp.int32, (1, tn), 1)) < n_real
    lg = jnp.where(real, lg, 0.0)
    scores = 1.0 / (1.0 + jnp.exp(-lg))
    sel = scores + bias_ref[...]
    neg = -jnp.inf
    sub8 = lax.broadcasted_iota(jnp.int32, (GROUP_SIZE, tn), 0)
    gsub = lax.broadcasted_iota(jnp.int32, (N_EXPERT_GROUPS, tn), 0)
    gs = jnp.zeros((N_EXPERT_GROUPS, tn), F32)
    for g in range(N_EXPERT_GROUPS):
        blk = sel[g * GROUP_SIZE:(g + 1) * GROUP_SIZE, :]
        m1 = jnp.max(blk, axis=0, keepdims=True)
        i1 = jnp.min(jnp.where(blk == m1, sub8, GROUP_SIZE), axis=0, keepdims=True)
        m2 = jnp.max(jnp.where(sub8 == i1, neg, blk), axis=0, keepdims=True)
        gs = jnp.where(gsub == g, m1 + m2, gs)
    gmask = jnp.zeros((N_EXPERT_GROUPS, tn), jnp.bool_)
    for _ in range(TOPK_GROUPS):
        m = jnp.max(gs, axis=0, keepdims=True)
        i = jnp.min(jnp.where(gs == m, gsub, N_EXPERT_GROUPS), axis=0, keepdims=True)
        hit = gsub == i
        gmask = jnp.logical_or(gmask, hit)
        gs = jnp.where(hit, neg, gs)
    gmaskf = gmask.astype(F32)
    blocks = []
    for g in range(N_EXPERT_GROUPS):
        keep = jnp.broadcast_to(gmaskf[g:g + 1, :], (GROUP_SIZE, tn)) > 0.5
        blocks.append(jnp.where(keep, sel[g * GROUP_SIZE:(g + 1) * GROUP_SIZE, :], neg))
    msel = jnp.concatenate(blocks, axis=0)
    esub = lax.broadcasted_iota(jnp.int32, (N_EXPERTS, tn), 0)
    chosen = jnp.zeros((N_EXPERTS, tn), jnp.bool_)
    picks = []
    for _ in range(TOP_K):
        m = jnp.max(msel, axis=0, keepdims=True)
        i = jnp.min(jnp.where(msel == m, esub, N_EXPERTS), axis=0, keepdims=True)
        hit = esub == i
        picks.append(i)
        chosen = jnp.logical_or(chosen, hit)
        msel = jnp.where(hit, neg, msel)
    w = jnp.where(chosen, scores, 0.0)
    w = w / jnp.sum(w, axis=0, keepdims=True) * ROUTED_SCALE
    counted = jnp.where(jnp.logical_and(chosen, real), 1.0, 0.0)
    incl = jnp.dot(counted.astype(BF16), tri_ref[...], preferred_element_type=F32)
    rank_full = carry_ref[:, 0:1] + incl - 1.0
    ksub = lax.broadcasted_iota(jnp.int32, (TOP_K, tn), 0)
    eid = jnp.zeros((TOP_K, tn), jnp.int32)
    rank = jnp.zeros((TOP_K, tn), F32)
    wts = jnp.zeros((TOP_K, tn), F32)
    for k in range(TOP_K):
        hit = esub == picks[k]
        eid = jnp.where(ksub == k, picks[k], eid)
        rank = jnp.where(ksub == k, jnp.sum(jnp.where(hit, rank_full, 0.0), axis=0, keepdims=True), rank)
        wts = jnp.where(ksub == k, jnp.sum(jnp.where(hit, w, 0.0), axis=0, keepdims=True), wts)
    eid_ref[...] = eid
    rank_ref[...] = rank.astype(jnp.int32)
    wts_ref[...] = wts
    carry = carry_ref[...] + incl[:, tn - 1:tn]
    carry_ref[...] = carry
    cnt_ref[...] = carry.astype(jnp.int32)


def _router(lg_t, bias, n_real):
    n = lg_t.shape[1]
    tn = ROUTER_TILE
    idx = jnp.arange(tn)
    tri = (idx[:, None] <= idx[None, :]).astype(BF16)
    kspec = pl.BlockSpec((TOP_K, tn), lambda i: (0, i))
    return pl.pallas_call(
        functools.partial(_router_body, n_real=n_real),
        grid=(n // tn,),
        in_specs=[pl.BlockSpec((N_EXPERTS, tn), lambda i: (0, i)), _const_spec((N_EXPERTS, 1)),
                  _const_spec((tn, tn))],
        out_specs=[kspec, kspec, kspec, _const_spec((N_EXPERTS, LANES))],
        out_shape=[jax.ShapeDtypeStruct((TOP_K, n), jnp.int32), jax.ShapeDtypeStruct((TOP_K, n), jnp.int32),
                   jax.ShapeDtypeStruct((TOP_K, n), F32), jax.ShapeDtypeStruct((N_EXPERTS, LANES), jnp.int32)],
        scratch_shapes=[pltpu.VMEM((N_EXPERTS, LANES), F32)],
        compiler_params=_cparams("arbitrary"),
        name="router",
    )(lg_t, bias.reshape(N_EXPERTS, 1), tri)


def _dest_body(off_ref, eid_ref, rank_ref, dest_ref, *, n_real, last_row):
    eid = eid_ref[...]
    base = jnp.zeros(eid.shape, jnp.int32)
    for e in range(N_EXPERTS):
        base = jnp.where(eid == e, off_ref[e], base)
    tok = lax.broadcasted_iota(jnp.int32, eid.shape, 1)
    slot = lax.broadcasted_iota(jnp.int32, eid.shape, 0)
    unused = last_row - ((tok - n_real) * TOP_K + slot)
    dest_ref[...] = jnp.where(tok < n_real, base + rank_ref[...], unused)


def _dest(off, eid, rank, n_real, last_row):
    spec = pl.BlockSpec(eid.shape, lambda i, off_ref: (0, 0))
    return pl.pallas_call(
        functools.partial(_dest_body, n_real=n_real, last_row=last_row),
        grid_spec=pltpu.PrefetchScalarGridSpec(num_scalar_prefetch=1, grid=(1,), in_specs=[spec, spec],
                                               out_specs=spec),
        out_shape=jax.ShapeDtypeStruct(eid.shape, jnp.int32),
        compiler_params=_cparams("arbitrary"),
        name="dest_rows",
    )(off, eid, rank)


def _sc_mesh():
    return plsc.VectorSubcoreMesh(core_axis_name="core", subcore_axis_name="subcore")


def _sc_dispatch(hp, dest_w, p_alloc):
    n = hp.shape[0]
    w = dest_w.shape[2]

    @functools.partial(pl.kernel, out_type=jax.ShapeDtypeStruct((p_alloc, DP), jnp.int32), mesh=_sc_mesh(),
                       name="sc_dispatch")
    def run(hp_hbm, dest_hbm, xs_hbm):
        def body(x_vmem, i_vmem):
            for k in range(TOP_K):
                pltpu.sync_copy(x_vmem, xs_hbm.at[i_vmem.at[k]])

        pltpu.emit_pipeline(
            body,
            grid=(n // w,),
            in_specs=[pl.BlockSpec((w, DP), lambda i: (i, 0)),
                      pl.BlockSpec((None, TOP_K, w), lambda i: (i, 0, 0))],
            out_specs=[],
            core_axis_name=("core", "subcore"),
            dimension_semantics=(pltpu.PARALLEL,),
        )(hp_hbm, dest_hbm)

    return run(hp, dest_w)


def _sc_gather_sum(ys, dest_tm, w_lanes):
    n_win, parts, pk = dest_tm.shape
    w = parts * pk // TOP_K
    wp = w // parts
    n_vec = DP // SC_LANES

    @functools.partial(pl.kernel, out_type=jax.ShapeDtypeStruct((n_win * w, D), F32), mesh=_sc_mesh(),
                       scratch_types=[pltpu.VMEM((parts, pk, DP), jnp.int32), pltpu.SemaphoreType.DMA((parts,))],
                       compiler_params=pltpu.CompilerParams(needs_layout_passes=False), name="sc_gather_sum")
    def run(ys_hbm, dest_hbm, w_hbm, o_hbm, rows_v, sems):
        def body(i_vmem, w_vmem, o_vmem):
            copies = [pltpu.async_copy(ys_hbm.at[i_vmem.at[p]], rows_v.at[p], sems.at[p]) for p in range(parts)]
            for p in range(parts):
                copies[p].wait()

                @pl.loop(0, wp)
                def _(t):
                    tok = p * wp + t
                    wv = [w_vmem[tok, pl.ds(k * SC_LANES, SC_LANES)] for k in range(TOP_K)]

                    @plsc.parallel_loop(0, n_vec, unroll=SUM_UNROLL)
                    def _(j):
                        col = j * SC_LANES
                        lo = jnp.zeros((SC_LANES,), F32)
                        hi = jnp.zeros((SC_LANES,), F32)
                        for k in range(TOP_K):
                            word = rows_v[p, t * TOP_K + k, pl.ds(col, SC_LANES)]
                            lo = lo + wv[k] * plsc.bitcast(lax.shift_left(word, 16), F32)
                            hi = hi + wv[k] * plsc.bitcast(word & HI_HALF, F32)
                        o_vmem[tok, pl.ds(col, SC_LANES)] = lo
                        o_vmem[tok, pl.ds(DP + col, SC_LANES)] = hi

        pltpu.emit_pipeline(
            body,
            grid=(n_win,),
            in_specs=[pl.BlockSpec((None, parts, pk), lambda i: (i, 0, 0)),
                      pl.BlockSpec((w, TOP_K * SC_LANES), lambda i: (i, 0))],
            out_specs=[pl.BlockSpec((w, D), lambda i: (i, 0))],
            core_axis_name=("core", "subcore"),
            dimension_semantics=(pltpu.PARALLEL,),
        )(dest_hbm, w_hbm, o_hbm)

    return run(ys, dest_tm, w_lanes)


def _expert_body(first_ref, cnt_ref, nused_ref, xs_hbm, wg_s, wu_s, wd_s, ys_hbm, xbuf, ybuf, xsem, ysem):
    e = pl.program_id(0)
    n_used = nused_ref[0]

    def load(g):
        rows = pl.ds(pl.multiple_of(g * EXPERT_TILE, EXPERT_TILE), EXPERT_TILE)
        slot = g % EXPERT_X_SLOTS
        return pltpu.make_async_copy(xs_hbm.at[rows], xbuf.at[slot], xsem.at[slot])

    def store(g):
        rows = pl.ds(pl.multiple_of(g * EXPERT_TILE, EXPERT_TILE), EXPERT_TILE)
        slot = g % EXPERT_Y_SLOTS
        return pltpu.make_async_copy(ybuf.at[slot], ys_hbm.at[rows], ysem.at[slot])

    @pl.when(e == 0)
    def _():
        for g in range(EXPERT_AHEAD):
            @pl.when(g < n_used)
            def _():
                load(g).start()

    first = first_ref[e]
    cnt = cnt_ref[e]

    def acquire(g):
        ahead = g + EXPERT_AHEAD

        @pl.when(ahead < n_used)
        def _():
            load(ahead).start()

        load(g).wait()

        @pl.when(g >= EXPERT_Y_SLOTS)
        def _():
            store(g - EXPERT_Y_SLOTS).wait()

    def compute(g):
        lo, hi = _unpack_rows(xbuf[g % EXPERT_X_SLOTS])
        lo = lo.astype(BF16)
        hi = hi.astype(BF16)

        def xdot(w_s):
            return (jnp.dot(lo, w_s[:DP, :], preferred_element_type=F32) +
                    jnp.dot(hi, w_s[DP:, :], preferred_element_type=F32))

        a = (_silu(xdot(wg_s)) * xdot(wu_s)).astype(BF16)
        ybuf[g % EXPERT_Y_SLOTS] = _pack_rows(jnp.dot(a, wd_s[...], preferred_element_type=F32))

    def pair(j, carry):
        g = first + 2 * j
        acquire(g)
        acquire(g + 1)
        compute(g)
        compute(g + 1)
        store(g).start()
        store(g + 1).start()
        return carry

    lax.fori_loop(0, cnt // 2, pair, 0)

    @pl.when(cnt % 2 == 1)
    def _():
        g = first + cnt - 1
        acquire(g)
        compute(g)
        store(g).start()

    @pl.when(e == N_EXPERTS - 1)
    def _():
        for k in range(EXPERT_Y_SLOTS):
            g = n_used - 1 - k

            @pl.when(g >= 0)
            def _():
                store(g).wait()


def _experts(xs, tile_first, tile_count, n_used, wg, wu, wd):
    def w_map(e, first, cnt, nu):
        return (e, 0, 0)

    return pl.pallas_call(
        _expert_body,
        grid_spec=pltpu.PrefetchScalarGridSpec(
            num_scalar_prefetch=3, grid=(N_EXPERTS,),
            in_specs=[pl.BlockSpec(memory_space=pl.ANY),
                      pl.BlockSpec((None, D, EXPERT_DIM), w_map),
                      pl.BlockSpec((None, D, EXPERT_DIM), w_map),
                      pl.BlockSpec((None, EXPERT_DIM, D), w_map)],
            out_specs=pl.BlockSpec(memory_space=pl.ANY),
            scratch_shapes=[pltpu.VMEM((EXPERT_X_SLOTS, EXPERT_TILE, DP), jnp.int32),
                            pltpu.VMEM((EXPERT_Y_SLOTS, EXPERT_TILE, DP), jnp.int32),
                            pltpu.SemaphoreType.DMA((EXPERT_X_SLOTS,)),
                            pltpu.SemaphoreType.DMA((EXPERT_Y_SLOTS,))]),
        out_shape=jax.ShapeDtypeStruct(xs.shape, jnp.int32),
        compiler_params=_cparams("arbitrary"),
        name="experts",
    )(tile_first, tile_count, n_used, xs, wg, wu, wd)


def _log_sigmoid(z):
    return jnp.minimum(z, 0.0) - jnp.log(1.0 + jnp.exp(-jnp.abs(z)))


def _gla_gate(hb, wlr_ref, wgk_ref, bgk_ref):
    lr = jnp.dot(hb, wlr_ref[...], preferred_element_type=F32)
    z = _bdot(lr, wgk_ref[...]) + bgk_ref[...]
    return _log_sigmoid(z) * (1.0 / GLA_GATE_NORMALIZER)


def _split3(a):
    hi = a.astype(BF16)
    r1 = a - hi.astype(F32)
    mid = r1.astype(BF16)
    lo = (r1 - mid.astype(F32)).astype(BF16)
    return hi, mid, lo


def _gla_out(o_ref_val, go, gng):
    parts = []
    for hd in range(GLA_HEADS):
        cols = slice(hd * GLA_DV, (hd + 1) * GLA_DV)
        parts.append((_rms(o_ref_val[:, cols], gng) * _silu(go[:, cols])).astype(BF16))
    return jnp.concatenate(parts, axis=1)


def _channel_mix_residual(x, routed, h_packed, g2, ng3, swg_ref, swu_ref, swd_ref):
    h_lo, h_hi = _unpack_rows(h_packed)
    h_lo = h_lo.astype(BF16)
    h_hi = h_hi.astype(BF16)

    def hdot(w_ref_):
        return (jnp.dot(h_lo, w_ref_[:DP, :], preferred_element_type=F32) +
                jnp.dot(h_hi, w_ref_[DP:, :], preferred_element_type=F32))

    hs = (_silu(hdot(swg_ref)) * hdot(swu_ref)).astype(BF16)
    y = jnp.dot(hs, swd_ref[...], preferred_element_type=F32) + routed
    return x + g2 * _rms(y, ng3)


def _gla_body(x_ref, y_ref, hprev_ref, modprev_ref, ngprev_ref, swg_ref, swu_ref, swd_ref,
              mod_ref, ng_ref, wqkvg_ref, wlr_ref, wgk_ref, bgk_ref, tril_ref, gng_ref, wout_ref,
              rw_ref, *rest, tt, n_alias):
    x1_ref, h_ref, lg_ref, st_ref, st_scr, o_scr, qd_scr, dst_scr = rest[n_alias:]
    j = pl.program_id(1)

    @pl.when(j == 0)
    def _():
        st_scr[...] = jnp.zeros_like(st_scr)

    x = _channel_mix_residual(x_ref[...], y_ref[...], hprev_ref[...], modprev_ref[:, 5 * D:6 * D],
                              ngprev_ref[3:4, :], swg_ref, swu_ref, swd_ref)
    sh1, sc1, g1, sh2, sc2, _ = _mod_slices(mod_ref)
    ng = ng_ref[...]
    hb = (_rms(x, ng[0:1]) * (1.0 + sc1) + sh1).astype(BF16)
    q = jnp.dot(hb, wqkvg_ref[:, :GLA_DK_TOT], preferred_element_type=F32) * (GLA_DK ** -0.5)
    k = jnp.dot(hb, wqkvg_ref[:, GLA_DK_TOT:2 * GLA_DK_TOT], preferred_element_type=F32)
    v = jnp.dot(hb, wqkvg_ref[:, 2 * GLA_DK_TOT:2 * GLA_DK_TOT + GLA_DV_TOT],
                preferred_element_type=F32).astype(BF16)
    log_a = _gla_gate(hb, wlr_ref, wgk_ref, bgk_ref)
    tril = tril_ref[...]
    parts = _split3(log_a)
    b = jnp.concatenate(
        [sum(jnp.dot(tril, p[r:r + GLA_CUM_BLOCK], preferred_element_type=F32) for p in parts)
         for r in range(0, tt, GLA_CUM_BLOCK)], axis=0)
    n_chunks = tt // GLA_CHUNK
    in_chunk_causal = tril > 0
    for r in range(0, tt, GLA_CUM_BLOCK):
        blk = slice(r, r + GLA_CUM_BLOCK)
        for hd in range(GLA_HEADS):
            kc = slice(hd * GLA_DK, (hd + 1) * GLA_DK)
            vc = slice(hd * GLA_DV, (hd + 1) * GLA_DV)
            bb = b[blk, kc]
            q_dec = (q[blk, kc] * jnp.exp(bb)).astype(BF16)
            k_inv = (k[blk, kc] * jnp.exp(-bb)).astype(BF16)
            att = jnp.where(in_chunk_causal, _dot_nt(q_dec, k_inv), 0.0).astype(BF16)
            qd_scr[blk, kc] = q_dec
            o_scr[blk, vc] = jnp.dot(att, v[blk, vc], preferred_element_type=F32)
    for c in range(n_chunks):
        rows = slice(c * GLA_CHUNK, (c + 1) * GLA_CHUNK)
        last = (c + 1) * GLA_CHUNK - 1
        for hd in range(GLA_HEADS):
            kc = slice(hd * GLA_DK, (hd + 1) * GLA_DK)
            vc = slice(hd * GLA_DV, (hd + 1) * GLA_DV)
            k_end = (k[rows, kc] * jnp.exp(b[last:last + 1, kc] - b[rows, kc])).astype(BF16)
            dst_scr[c * GLA_HEADS + hd] = lax.dot_general(
                v[rows, vc], k_end, (((0,), (0,)), ((), ())), preferred_element_type=F32)
    states = [st_scr[hd] for hd in range(GLA_HEADS)]
    for c in range(n_chunks):
        rows = slice(c * GLA_CHUNK, (c + 1) * GLA_CHUNK)
        last = (c + 1) * GLA_CHUNK - 1
        for hd in range(GLA_HEADS):
            kc = slice(hd * GLA_DK, (hd + 1) * GLA_DK)
            vc = slice(hd * GLA_DV, (hd + 1) * GLA_DV)
            o_scr[rows, vc] += _dot_nt(qd_scr[rows, kc], states[hd].astype(BF16))
            states[hd] = states[hd] * jnp.exp(b[last:last + 1, kc]) + dst_scr[c * GLA_HEADS + hd]
    for hd in range(GLA_HEADS):
        st_scr[hd] = states[hd]

    @pl.when(j == pl.num_programs(1) - 1)
    def _():
        for hd in range(GLA_HEADS):
            st_ref[hd] = st_scr[hd].T

    go = jnp.dot(hb, wqkvg_ref[:, 2 * GLA_DK_TOT + GLA_DV_TOT:], preferred_element_type=F32)
    y = jnp.dot(_gla_out(o_scr[...], go, gng_ref[...]), wout_ref[...], preferred_element_type=F32)
    x1 = x + g1 * _rms(y, ng[1:2])
    x1_ref[...] = x1
    _ffn_prep(x1, ng, sh2, sc2, rw_ref, h_ref, lg_ref)


def _gla_mixer(x2d, routed, hp, mod3_prev, ng_prev, shared_prev, mod3, batch, seq, ng, wqkvg, wlr, wgk, bgk,
               gng, wout, rw_t, ffn_rows=None):
    tt = GLA_TILE
    tpb = seq // tt
    n = x2d.shape[0]
    idx = jnp.arange(GLA_CUM_BLOCK)
    tril = ((idx[:, None] >= idx[None, :]) &
            (idx[:, None] // GLA_CHUNK == idx[None, :] // GLA_CHUNK)).astype(BF16)
    row_map = lambda b, j: (b * tpb + j, 0)
    mod_map = lambda b, j: (b, 0, 0)
    consts = (ng, wqkvg, wlr, wgk, bgk, tril, gng, wout, rw_t)
    prev_consts = (ng_prev,) + tuple(shared_prev)
    ffn_shapes, alias_bufs, _ = _ffn_out(n, ffn_rows, None)
    args = [x2d, routed, hp, mod3_prev, *prev_consts, mod3, *consts]
    return pl.pallas_call(
        functools.partial(_gla_body, tt=tt, n_alias=len(alias_bufs)),
        grid=(batch, tpb),
        in_specs=[pl.BlockSpec((tt, D), row_map), pl.BlockSpec((tt, D), row_map),
                  pl.BlockSpec((tt, DP), row_map), pl.BlockSpec((None, 1, 6 * D), mod_map)] +
                 [_const_spec(a.shape) for a in prev_consts] +
                 [pl.BlockSpec((None, 1, 6 * D), mod_map)] +
                 [_const_spec(a.shape) for a in consts] +
                 [pl.BlockSpec(memory_space=pl.ANY)] * len(alias_bufs),
        out_specs=[pl.BlockSpec((tt, D), row_map), pl.BlockSpec((tt, DP), row_map),
                   pl.BlockSpec((N_EXPERTS, tt), lambda b, j: (0, b * tpb + j)),
                   pl.BlockSpec((None, GLA_HEADS, GLA_DK, GLA_DV), lambda b, j: (b, 0, 0, 0))],
        out_shape=[jax.ShapeDtypeStruct((n, D), F32)] + ffn_shapes +
                  [jax.ShapeDtypeStruct((batch, GLA_HEADS, GLA_DK, GLA_DV), F32)],
        input_output_aliases={len(args) + i: 1 + i for i in range(len(alias_bufs))},
        scratch_shapes=[pltpu.VMEM((GLA_HEADS, GLA_DV, GLA_DK), F32),
                        pltpu.VMEM((tt, GLA_DV_TOT), F32),
                        pltpu.VMEM((tt, GLA_DK_TOT), BF16),
                        pltpu.VMEM((tt // GLA_CHUNK * GLA_HEADS, GLA_DV, GLA_DK), F32)],
        compiler_params=_cparams("parallel", "arbitrary"),
        name="gla_mixer",
    )(*args, *alias_bufs)


def _gla1_proj_body(x_ref, mod_ref, ng_ref, wqkvg_ref, wlr_ref, wgk_ref, bgk_ref,
                    q_ref, k_ref, v_ref, go_ref, dec_ref):
    sh1, sc1, _, _, _, _ = _mod_slices(mod_ref)
    ng = ng_ref[...]
    hb = (_rms(x_ref[...], ng[0:1]) * (1.0 + sc1) + sh1).astype(BF16)
    proj = jnp.dot(hb, wqkvg_ref[...], preferred_element_type=F32)
    q_ref[...] = proj[:, :GLA_DK_TOT] * (GLA_DK ** -0.5)
    k_ref[...] = proj[:, GLA_DK_TOT:2 * GLA_DK_TOT]
    v_ref[...] = proj[:, 2 * GLA_DK_TOT:2 * GLA_DK_TOT + GLA_DV_TOT]
    go_ref[...] = proj[:, 2 * GLA_DK_TOT + GLA_DV_TOT:]
    dec_ref[...] = jnp.exp(_gla_gate(hb, wlr_ref, wgk_ref, bgk_ref))


GLA1_TOK = 16


def _gla1_state_body(st_ref, qc_ref, kc_ref, dc_ref, v_ref, nst_ref, o_ref):
    v = v_ref[...]
    for i in range(GLA1_TOK):
        for hd in range(GLA_HEADS):
            vrow = v[i:i + 1, hd * GLA_DV:(hd + 1) * GLA_DV]
            s_new = dc_ref[hd][:, i:i + 1] * st_ref[i, hd] + kc_ref[hd][:, i:i + 1] * vrow
            nst_ref[i, hd] = s_new
            o_ref[i:i + 1, hd * GLA_DV:(hd + 1) * GLA_DV] = jnp.sum(
                qc_ref[hd][:, i:i + 1] * s_new, axis=0, keepdims=True)


def _gla1_out_body(x_ref, o_ref, go_ref, mod_ref, ng_ref, gng_ref, wout_ref, rw_ref, *rest):
    x1_ref, h_ref, lg_ref = rest[-3:]
    _, _, g1, sh2, sc2, _ = _mod_slices(mod_ref)
    ng = ng_ref[...]
    y = jnp.dot(_gla_out(o_ref[...], go_ref[...], gng_ref[...]), wout_ref[...], preferred_element_type=F32)
    x1 = x_ref[...] + g1 * _rms(y, ng[1:2])
    x1_ref[...] = x1
    _ffn_prep(x1, ng, sh2, sc2, rw_ref, h_ref, lg_ref)


def _gla_mixer_one(x2d, mod2, state, ng, wqkvg, wlr, wgk, bgk, gng, wout, rw_t, into=None):
    n = x2d.shape[0]
    consts = (ng, wqkvg, wlr, wgk, bgk)
    q, k, v, go, dec = pl.pallas_call(
        _gla1_proj_body,
        in_specs=[_const_spec(a.shape) for a in (x2d, mod2) + consts],
        out_specs=[_const_spec((n, GLA_DK_TOT)), _const_spec((n, GLA_DK_TOT)), _const_spec((n, GLA_DV_TOT)),
                   _const_spec((n, GLA_DV_TOT)), _const_spec((n, GLA_DK_TOT))],
        out_shape=[jax.ShapeDtypeStruct((n, GLA_DK_TOT), F32), jax.ShapeDtypeStruct((n, GLA_DK_TOT), F32),
                   jax.ShapeDtypeStruct((n, GLA_DV_TOT), F32), jax.ShapeDtypeStruct((n, GLA_DV_TOT), F32),
                   jax.ShapeDtypeStruct((n, GLA_DK_TOT), F32)],
        grid=(1,),
        compiler_params=_cparams("arbitrary"),
        name="gla1_proj",
    )(x2d, mod2, *consts)

    def cols(a):
        return a.reshape(n // GLA1_TOK, GLA1_TOK, GLA_HEADS, GLA_DK).transpose(0, 2, 3, 1)

    col_spec = pl.BlockSpec((None, GLA_HEADS, GLA_DK, GLA1_TOK), lambda i: (i, 0, 0, 0))
    st_spec = pl.BlockSpec((GLA1_TOK, GLA_HEADS, GLA_DK, GLA_DV), lambda i: (i, 0, 0, 0))
    new_state, o = pl.pallas_call(
        _gla1_state_body,
        grid=(n // GLA1_TOK,),
        in_specs=[st_spec, col_spec, col_spec, col_spec, pl.BlockSpec((GLA1_TOK, GLA_DV_TOT), lambda i: (i, 0))],
        out_specs=[st_spec, pl.BlockSpec((GLA1_TOK, GLA_DV_TOT), lambda i: (i, 0))],
        out_shape=[jax.ShapeDtypeStruct(state.shape, F32), jax.ShapeDtypeStruct((n, GLA_DV_TOT), F32)],
        compiler_params=_cparams("parallel"),
        name="gla1_state",
    )(state, cols(q), cols(k), cols(dec), v)

    consts = (mod2, ng, gng, wout, rw_t)
    ffn_shapes, alias_bufs, oblk = _ffn_out(n, None, into)
    n_in = 3 + len(consts)
    x1, h, lg = pl.pallas_call(
        _gla1_out_body,
        grid=(1,),
        in_specs=[_const_spec(a.shape) for a in (x2d, o, go) + consts] +
                 [pl.BlockSpec(memory_space=pl.ANY)] * len(alias_bufs),
        out_specs=[_const_spec((n, D)), pl.BlockSpec((n, DP), lambda i: (oblk, 0)),
                   pl.BlockSpec((N_EXPERTS, n), lambda i: (0, oblk))],
        out_shape=[jax.ShapeDtypeStruct((n, D), F32)] + ffn_shapes,
        input_output_aliases={n_in + i: 1 + i for i in range(len(alias_bufs))},
        compiler_params=_cparams("arbitrary"),
        name="gla1_out",
    )(x2d, o, go, *consts, *alias_bufs)
    return x1, h, lg, new_state


def _moe_routed(h, lg, n, router_bias, wg, wu, wd):
    n_pad = h.shape[0]
    eid, rank, wts, counts = _router(lg, router_bias, n)
    tile_count = ((counts[:, 0] + EXPERT_TILE - 1) // EXPERT_TILE).astype(jnp.int32)
    tile_end = jnp.cumsum(tile_count).astype(jnp.int32)
    tile_first = tile_end - tile_count
    off = tile_first * EXPERT_TILE
    p_alloc = TOP_K * n_pad + N_EXPERTS * EXPERT_TILE
    dest = _dest(off, eid, rank, n, p_alloc - 1)
    dest_w = dest.reshape(TOP_K, n_pad // DISPATCH_W, DISPATCH_W).transpose(1, 0, 2)
    xs = _sc_dispatch(h, dest_w, p_alloc)
    ys = _experts(xs, tile_first, tile_count, tile_end[-1:], wg, wu, wd)
    dest_tm = dest.T.reshape(n_pad // SUM_W, SUM_PARTS, SUM_W * TOP_K // SUM_PARTS)
    w_lanes = jnp.repeat(wts.T, SC_LANES, axis=1)
    return _sc_gather_sum(ys, dest_tm, w_lanes)


def kernel(x_prompt, x_sample, state_gla, c_prompt, c_sample, norm_g, ada_w, ada_b, gm_w_in, gm_b_in,
           gm_ln_g, gm_ln_b, gm_w_s, gm_b_s, gm_w_out, gla_w_in, gla_w_gk, gla_b_gk, gla_norm_g,
           gla_w_out, router_w, router_bias, exp_w_gate, exp_w_up, exp_w_down, sh_w_gate, sh_w_up,
           sh_w_down):
    batch, seq, _ = x_prompt.shape
    n_s = x_sample.shape[0]
    n_p = batch * seq
    tpb = seq // MIX_TILE
    xp = x_prompt.reshape(n_p, D)
    xs = x_sample.reshape(n_s, D)

    mod = _ada(jnp.concatenate([c_prompt, c_sample], axis=0), ada_w, ada_b)
    mod_p = [mod[i, :batch].reshape(batch, 1, 6 * D) for i in range(2)]
    mod_s = [mod[i, batch:] for i in range(2)]
    rw_t = [jnp.concatenate(_split3(router_w[i].T), axis=0) for i in range(2)]

    ws_causal = jnp.tril(gm_w_s[0]).astype(BF16)
    bs_cols = gm_b_s[0].T
    eye = jnp.eye(GM_CHUNK, dtype=F32)
    ws_first = (gm_w_s[0][:, 0, 0][:, None, None] * eye).astype(BF16)
    bs_first = jnp.broadcast_to(gm_b_s[0][:, 0][None, :], (GM_CHUNK, GM_GROUPS))
    gm_args = (norm_g[0], gm_w_in[0].astype(BF16), gm_b_in[0].reshape(1, -1), gm_ln_g[0].reshape(1, -1),
               gm_ln_b[0].reshape(1, -1))
    wout0 = gm_w_out[0].astype(BF16)
    shared = [(sh_w_gate[i].astype(BF16), sh_w_up[i].astype(BF16), sh_w_down[i].astype(BF16))
              for i in range(2)]
    n_qkvg = 2 * GLA_DK_TOT + 2 * GLA_DV_TOT
    wqkvg = gla_w_in[0][:, :n_qkvg].astype(BF16)
    wlr = jnp.pad(gla_w_in[0][:, n_qkvg:], ((0, 0), (0, LANES - GLA_GATE_RANK))).astype(BF16)
    wgk = jnp.pad(gla_w_gk[0], ((0, LANES - GLA_GATE_RANK), (0, 0))).astype(BF16)
    gla_args = (norm_g[1], wqkvg, wlr, wgk, gla_b_gk[0].reshape(1, -1), gla_norm_g[0].reshape(1, -1),
                gla_w_out[0].astype(BF16), rw_t[1])
    experts_f32 = (exp_w_gate, exp_w_up, exp_w_down)

    half = batch // 2
    streams = [(0, half, False), (half, batch - half, True)]
    st = [dict() for _ in streams]

    experts = []
    for layer, (s, (b0, nb, with_new)) in enumerate(zip(st, streams)):
        s["mod_p"] = [mod_p[i][b0:b0 + nb] for i in range(2)]
        s["n"] = nb * seq
        s["n_all"] = s["n"] + (n_s if with_new else 0)
        s["n_pad"] = -(-s["n_all"] // TOKEN_PAD) * TOKEN_PAD
        s["x1p"], s["h"], s["lg"], *w16 = _gmlp_mixer(
            xp, b0 * tpb, s["n"], s["mod_p"][0], False, MIX_TILE, tpb, *gm_args, ws_causal, bs_cols, wout0,
            rw_t[0], emit_v=False, cast_w=(layer, *experts_f32), ffn_rows=s["n_pad"])
        experts.append(w16)
        if with_new:
            s["x1s"], s["h"], s["lg"], v_rows = _gmlp_mixer(
                xs, 0, n_s, mod_s[0], True, n_s, 1, *gm_args, ws_first, bs_first, wout0, rw_t[0], emit_v=True,
                into=(s["h"], s["lg"], s["n"] // n_s))
    for s, (b0, nb, with_new) in zip(st, streams):
        s["routed0"] = _moe_routed(s["h"], s["lg"], s["n_all"], router_bias[0], *experts[0])
        if with_new:
            s["x2s"] = _combine(s["x1s"], s["routed0"], s["h"], s["n"] // n_s, mod_s[0], True, n_s, 1,
                                norm_g[0], *shared[0])
    for s, (b0, nb, with_new) in zip(st, streams):
        s["x3p"], h1, lg1, s["st_p"] = _gla_mixer(s["x1p"], s["routed0"], s["h"], s["mod_p"][0], norm_g[0],
                                                  shared[0], s["mod_p"][1], nb, seq, *gla_args,
                                                  ffn_rows=s["n_pad"])
        if with_new:
            s["x3s"], h1, lg1, st_s = _gla_mixer_one(s["x2s"], mod_s[1], state_gla[:, 0], *gla_args,
                                                     into=(h1, lg1, s["n"] // n_s))
        s["h"], s["lg"] = h1, lg1
    y_prompt = None
    for s, (b0, nb, with_new) in zip(st, streams):
        routed = _moe_routed(s["h"], s["lg"], s["n_all"], router_bias[1], *experts[1])
        ctpb = seq // COMBINE_TILE
        y_prompt = _combine(s["x3p"], routed, s["h"], 0, s["mod_p"][1], False, COMBINE_TILE, ctpb, norm_g[1],
                            *shared[1], out_rows=n_p, out_blk0=b0 * ctpb, out_buf=y_prompt)
        if with_new:
            y_new = _combine(s["x3s"], routed, s["h"], s["n"] // n_s, mod_s[1], True, n_s, 1, norm_g[1],
                             *shared[1])
    st_p = jnp.concatenate([s["st_p"] for s in st], axis=0)

    return (y_prompt.reshape(batch, seq, D), y_new.reshape(n_s, 1, D), st_p[:, None], st_s[:, None],
            v_rows.reshape(n_s, 1, 1, GM_HALF))
```

```python
import functools
import math

import jax
import jax.numpy as jnp
from jax import lax
from jax.experimental import pallas as pl
from jax.experimental.pallas import tpu as pltpu
from jax.experimental.pallas import tpu_sc as plsc

F32 = jnp.float32
BF16 = jnp.bfloat16

D = 1024
DP = D // 2
GM_CHUNK = 128
GM_HALF = 2 * D
GM_GROUPS = 8
GM_GROUP_DIM = GM_HALF // GM_GROUPS
GLA_HEADS = 4
GLA_DK = 128
GLA_DV = 256
GLA_DK_TOT = GLA_HEADS * GLA_DK
GLA_DV_TOT = GLA_HEADS * GLA_DV
GLA_GATE_RANK = 16
GLA_GATE_NORMALIZER = 16.0
GLA_CHUNK = 64
N_EXPERTS = 64
TOP_K = 8
N_EXPERT_GROUPS = 8
GROUP_SIZE = N_EXPERTS // N_EXPERT_GROUPS
TOPK_GROUPS = 4
EXPERT_DIM = D // 4
ROUTED_SCALE = 2.5
NORM_EPS = 1e-6
LN_EPS = 1e-5

LANES = 128
VMEM_LIMIT = 56 * 1024 * 1024

MIX_TILE = 256
COMBINE_TILE = 1024
GLA_TILE = 512
GLA_CUM_BLOCK = 128
GM_COL_BLOCK = 512
ROUTER_TILE = 1024
EXPERT_TILE = 544
EXPERT_X_SLOTS = 6
EXPERT_AHEAD = EXPERT_X_SLOTS - 2
EXPERT_Y_SLOTS = 4
SC_WORKERS = 32
DISPATCH_W = 32
SC_LANES = 16
SUM_W = 16
SUM_PARTS = 8
SUM_UNROLL = 4
TOKEN_PAD = SC_WORKERS * DISPATCH_W


def _cparams(*sem):
    return pltpu.CompilerParams(dimension_semantics=sem, vmem_limit_bytes=VMEM_LIMIT)


def _rms(x, g):
    return x * lax.rsqrt(jnp.mean(x * x, axis=-1, keepdims=True) + NORM_EPS) * g


def _silu(x):
    return x * (1.0 / (1.0 + jnp.exp(-x)))


def _gelu(x):
    return 0.5 * x * (1.0 + lax.erf(x * (1.0 / math.sqrt(2.0))))


def _bdot(a, b):
    return jnp.dot(a.astype(BF16), b.astype(BF16), preferred_element_type=F32)


def _dot_nt(a, b, precision=None):
    return lax.dot_general(a, b, (((1,), (1,)), ((), ())), preferred_element_type=F32,
                           precision=precision)


def _mod_slices(mod_ref):
    return [mod_ref[:, i * D:(i + 1) * D] for i in range(6)]


HI_HALF = -65536


def _pack_rows(x):
    lo = lax.bitcast_convert_type(x[:, :DP].astype(BF16).astype(F32), jnp.int32)
    hi = lax.bitcast_convert_type(x[:, DP:].astype(BF16).astype(F32), jnp.int32)
    return lax.shift_right_logical(lo, 16) | (hi & HI_HALF)


def _unpack_rows(p):
    lo = lax.bitcast_convert_type(lax.shift_left(p, 16), F32)
    hi = lax.bitcast_convert_type(p & HI_HALF, F32)
    return lo, hi


def _ffn_prep(x1, ng, sh2, sc2, rw_ref, h_ref, lg_ref, rows=slice(None)):
    hffn = _rms(x1, ng[2:3]) * (1.0 + sc2) + sh2
    h_ref[rows, :] = _pack_rows(hffn)
    lg3 = _dot_nt(rw_ref[...], hffn.astype(BF16))
    lg_ref[:, rows] = lg3[:N_EXPERTS] + lg3[N_EXPERTS:2 * N_EXPERTS] + lg3[2 * N_EXPERTS:]


def _ada_body(c_ref, w_ref, b_ref, o_ref):
    c = c_ref[...]
    o_ref[...] = _bdot(_silu(c), w_ref[...]) + b_ref[...]


def _ada(c, ada_w, ada_b):
    n = c.shape[0]
    depth = ada_w.shape[0]
    tn = 1536
    return pl.pallas_call(
        _ada_body,
        grid=(depth, 6 * D // tn),
        in_specs=[pl.BlockSpec((n, D), lambda l, j: (0, 0)),
                  pl.BlockSpec((None, D, tn), lambda l, j: (l, 0, j)),
                  pl.BlockSpec((None, 1, tn), lambda l, j: (l, 0, j))],
        out_specs=pl.BlockSpec((None, n, tn), lambda l, j: (l, 0, j)),
        out_shape=jax.ShapeDtypeStruct((depth, n, 6 * D), F32),
        compiler_params=_cparams("parallel", "parallel"),
        name="ada_mod",
    )(c, ada_w, ada_b.reshape(depth, 1, 6 * D))


def _mod_spec(per_row, tt, tiles_per_batch):
    if per_row:
        return pl.BlockSpec((tt, 6 * D), lambda i: (i, 0))
    return pl.BlockSpec((None, 1, 6 * D), lambda i: (i // tiles_per_batch, 0, 0))


def _const_spec(shape):
    zeros = (0,) * len(shape)
    return pl.BlockSpec(shape, lambda *_: zeros)


def _gmlp_body(x_ref, mod_ref, ng_ref, win_ref, bin_ref, lng_ref, lnb_ref, ws_ref, bs_ref, wout_ref,
               rw_ref, *rest, n_chunks, emit_v, cast_w, n_alias):
    rest = list(rest)
    w32_refs = [rest.pop(0) for _ in range(3)] if cast_w else []
    rest = rest[n_alias:]
    x1_ref, h_ref, lg_ref = rest[:3]
    rest = rest[3:]
    v_ref = rest.pop(0) if emit_v else None
    w16_refs = [rest.pop(0) for _ in range(3)] if cast_w else []
    um_ref, z_ref = rest
    for src, dst in zip(w32_refs, w16_refs):
        dst[...] = src[...].astype(BF16)
    sh1, sc1, g1, sh2, sc2, _ = _mod_slices(mod_ref)
    ng = ng_ref[...]
    x = x_ref[...]
    hb = (_rms(x, ng[0:1]) * (1.0 + sc1) + sh1).astype(BF16)
    for cb in range(2 * GM_HALF // GM_COL_BLOCK):
        cols = slice(cb * GM_COL_BLOCK, (cb + 1) * GM_COL_BLOCK)
        z_ref[:, cols] = _gelu(jnp.dot(hb, win_ref[:, cols], preferred_element_type=F32) + bin_ref[:, cols])
    u = z_ref[:, :GM_HALF]
    v = z_ref[:, GM_HALF:]
    mu = jnp.mean(v, axis=-1, keepdims=True)
    vc = v - mu
    var = jnp.mean(vc * vc, axis=-1, keepdims=True)
    v = vc * lax.rsqrt(var + LN_EPS) * lng_ref[...] + lnb_ref[...]
    if emit_v:
        v_ref[...] = v
    vb = v.astype(BF16)
    for c in range(n_chunks):
        rows = slice(c * GM_CHUNK, (c + 1) * GM_CHUNK)
        for g in range(GM_GROUPS):
            cols = slice(g * GM_GROUP_DIM, (g + 1) * GM_GROUP_DIM)
            mixed = jnp.dot(ws_ref[g], vb[rows, cols], preferred_element_type=F32) + bs_ref[:, g:g + 1]
            um_ref[rows, cols] = (u[rows, cols] * mixed).astype(BF16)
    y = jnp.dot(um_ref[...], wout_ref[...], preferred_element_type=F32)
    x1 = x + g1 * _rms(y, ng[1:2])
    x1_ref[...] = x1
    _ffn_prep(x1, ng, sh2, sc2, rw_ref, h_ref, lg_ref)


def _ffn_out(n, ffn_rows, into):
    rows = ffn_rows or n
    oblk = 0
    bufs = []
    if into is not None:
        *bufs, oblk = into
        rows = bufs[0].shape[0]
    elif rows != n:
        bufs = [jnp.zeros((rows, DP), jnp.int32), jnp.zeros((N_EXPERTS, rows), F32)]
    shapes = [jax.ShapeDtypeStruct((rows, DP), jnp.int32), jax.ShapeDtypeStruct((N_EXPERTS, rows), F32)]
    return shapes, bufs, oblk


def _gmlp_mixer(x2d, blk0, n, mod, per_row, tt, tiles_per_batch, ng, win, b_in, ln_g, ln_b, ws, bs, wout,
                rw_t, emit_v, cast_w=None, ffn_rows=None, into=None):
    steps = n // tt
    ffn_shapes, alias_bufs, oblk = _ffn_out(n, ffn_rows, into)
    out_shape = [jax.ShapeDtypeStruct((n, D), F32)] + ffn_shapes
    out_specs = [pl.BlockSpec((tt, D), lambda i: (i, 0)), pl.BlockSpec((tt, DP), lambda i: (i + oblk, 0)),
                 pl.BlockSpec((N_EXPERTS, tt), lambda i: (0, i + oblk))]
    if emit_v:
        out_shape.append(jax.ShapeDtypeStruct((n, GM_HALF), F32))
        out_specs.append(pl.BlockSpec((tt, GM_HALF), lambda i: (i, 0)))

    def one_buffer(a):
        zeros = (0,) * a.ndim
        return pl.BlockSpec(a.shape, lambda *_: zeros, pipeline_mode=pl.Buffered(1))

    consts = (ng, win, b_in, ln_g, ln_b, ws, bs, wout, rw_t)
    in_specs = [pl.BlockSpec((tt, D), lambda i: (i + blk0, 0)), _mod_spec(per_row, tt, tiles_per_batch)]
    in_specs += [one_buffer(a) for a in consts]
    args = [x2d, mod, *consts]
    if cast_w is not None:
        layer, *w_all = cast_w
        per_step = N_EXPERTS // steps
        for w in w_all:
            blk = (None, per_step) + w.shape[2:]
            in_specs.append(pl.BlockSpec(blk, lambda i: (layer, i, 0, 0)))
            out_specs.append(pl.BlockSpec(blk[1:], lambda i: (i, 0, 0)))
            out_shape.append(jax.ShapeDtypeStruct(w.shape[1:], BF16))
            args.append(w)
    aliases = {len(args) + i: 1 + i for i in range(len(alias_bufs))}
    in_specs += [pl.BlockSpec(memory_space=pl.ANY)] * len(alias_bufs)
    args += alias_bufs
    return pl.pallas_call(
        functools.partial(_gmlp_body, n_chunks=tt // GM_CHUNK, emit_v=emit_v, cast_w=cast_w is not None,
                          n_alias=len(alias_bufs)),
        grid=(steps,),
        in_specs=in_specs,
        out_specs=out_specs,
        out_shape=out_shape,
        input_output_aliases=aliases,
        scratch_shapes=[pltpu.VMEM((tt, GM_HALF), BF16), pltpu.VMEM((tt, 2 * GM_HALF), F32)],
        compiler_params=_cparams("parallel"),
        name="gmlp_mixer_rows" if per_row else "gmlp_mixer",
    )(*args)


def _combine_body(x_ref, y_ref, h_ref, mod_ref, ng_ref, swg_ref, swu_ref, swd_ref, *rest):
    o_ref = rest[-1]
    o_ref[...] = _channel_mix_residual(x_ref[...], y_ref[...], h_ref[...], mod_ref[:, 5 * D:6 * D],
                                       ng_ref[3:4, :], swg_ref, swu_ref, swd_ref)


def _combine(x2d, routed, hp, blk0, mod, per_row, tt, tiles_per_batch, ng, swg, swu, swd,
             out_rows=None, out_blk0=0, out_buf=None):
    n = x2d.shape[0]
    in_specs = [pl.BlockSpec((tt, D), lambda i: (i, 0)),
                pl.BlockSpec((tt, D), lambda i: (i + blk0, 0)),
                pl.BlockSpec((tt, DP), lambda i: (i + blk0, 0)),
                _mod_spec(per_row, tt, tiles_per_batch),
                _const_spec(ng.shape), _const_spec(swg.shape), _const_spec(swu.shape),
                _const_spec(swd.shape)]
    args = [x2d, routed, hp, mod, ng, swg, swu, swd]
    aliases = {}
    if out_buf is not None:
        in_specs.append(pl.BlockSpec(memory_space=pl.ANY))
        aliases = {len(args): 0}
        args.append(out_buf)
    return pl.pallas_call(
        _combine_body,
        grid=(n // tt,),
        in_specs=in_specs,
        out_specs=pl.BlockSpec((tt, D), lambda i: (i + out_blk0, 0)),
        out_shape=jax.ShapeDtypeStruct((out_rows or n, D), F32),
        input_output_aliases=aliases,
        compiler_params=_cparams("parallel"),
        name="combine_rows" if per_row else "combine",
    )(*args)


def _router_body(lg_ref, bias_ref, tri_ref, eid_ref, rank_ref, wts_ref, cnt_ref, carry_ref, *, n_real):
    step = pl.program_id(0)

    @pl.when(step == 0)
    def _():
        carry_ref[...] = jnp.zeros_like(carry_ref)

    lg = lg_ref[...]
    tn = lg.shape[1]
    real = (step * tn + lax.broadcasted_iota(jnp.int32, (1, tn), 1)) < n_real
    lg = jnp.where(real, lg, 0.0)
    scores = 1.0 / (1.0 + jnp.exp(-lg))
    sel = scores + bias_ref[...]
    neg = -jnp.inf
    sub8 = lax.broadcasted_iota(jnp.int32, (GROUP_SIZE, tn), 0)
    gsub = lax.broadcasted_iota(jnp.int32, (N_EXPERT_GROUPS, tn), 0)
    gs = jnp.zeros((N_EXPERT_GROUPS, tn), F32)
    for g in range(N_EXPERT_GROUPS):
        blk = sel[g * GROUP_SIZE:(g + 1) * GROUP_SIZE, :]
        m1 = jnp.max(blk, axis=0, keepdims=True)
        i1 = jnp.min(jnp.where(blk == m1, sub8, GROUP_SIZE), axis=0, keepdims=True)
        m2 = jnp.max(jnp.where(sub8 == i1, neg, blk), axis=0, keepdims=True)
        gs = jnp.where(gsub == g, m1 + m2, gs)
    gmask = jnp.zeros((N_EXPERT_GROUPS, tn), jnp.bool_)
    for _ in range(TOPK_GROUPS):
        m = jnp.max(gs, axis=0, keepdims=True)
        i = jnp.min(jnp.where(gs == m, gsub, N_EXPERT_GROUPS), axis=0, keepdims=True)
        hit = gsub == i
        gmask = jnp.logical_or(gmask, hit)
        gs = jnp.where(hit, neg, gs)
    gmaskf = gmask.astype(F32)
    blocks = []
    for g in range(N_EXPERT_GROUPS):
        keep = jnp.broadcast_to(gmaskf[g:g + 1, :], (GROUP_SIZE, tn)) > 0.5
        blocks.append(jnp.where(keep, sel[g * GROUP_SIZE:(g + 1) * GROUP_SIZE, :], neg))
    msel = jnp.concatenate(blocks, axis=0)
    esub = lax.broadcasted_iota(jnp.int32, (N_EXPERTS, tn), 0)
    chosen = jnp.zeros((N_EXPERTS, tn), jnp.bool_)
    picks = []
    for _ in range(TOP_K):
        m = jnp.max(msel, axis=0, keepdims=True)
        i = jnp.min(jnp.where(msel == m, esub, N_EXPERTS), axis=0, keepdims=True)
        hit = esub == i
        picks.append(i)
        chosen = jnp.logical_or(chosen, hit)
        msel = jnp.where(hit, neg, msel)
    w = jnp.where(chosen, scores, 0.0)
    w = w / jnp.sum(w, axis=0, keepdims=True) * ROUTED_SCALE
    counted = jnp.where(jnp.logical_and(chosen, real), 1.0, 0.0)
    incl = jnp.dot(counted.astype(BF16), tri_ref[...], preferred_element_type=F32)
    rank_full = carry_ref[:, 0:1] + incl - 1.0
    ksub = lax.broadcasted_iota(jnp.int32, (TOP_K, tn), 0)
    eid = jnp.zeros((TOP_K, tn), jnp.int32)
    rank = jnp.zeros((TOP_K, tn), F32)
    wts = jnp.zeros((TOP_K, tn), F32)
    for k in range(TOP_K):
        hit = esub == picks[k]
        eid = jnp.where(ksub == k, picks[k], eid)
        rank = jnp.where(ksub == k, jnp.sum(jnp.where(hit, rank_full, 0.0), axis=0, keepdims=True), rank)
        wts = jnp.where(ksub == k, jnp.sum(jnp.where(hit, w, 0.0), axis=0, keepdims=True), wts)
    eid_ref[...] = eid
    rank_ref[...] = rank.astype(jnp.int32)
    wts_ref[...] = wts
    carry = carry_ref[...] + incl[:, tn - 1:tn]
    carry_ref[...] = carry
    cnt_ref[...] = carry.astype(jnp.int32)


def _router(lg_t, bias, n_real):
    n = lg_t.shape[1]
    tn = ROUTER_TILE
    idx = jnp.arange(tn)
    tri = (idx[:, None] <= idx[None, :]).astype(BF16)
    kspec = pl.BlockSpec((TOP_K, tn), lambda i: (0, i))
    return pl.pallas_call(
        functools.partial(_router_body, n_real=n_real),
        grid=(n // tn,),
        in_specs=[pl.BlockSpec((N_EXPERTS, tn), lambda i: (0, i)), _const_spec((N_EXPERTS, 1)),
                  _const_spec((tn, tn))],
        out_specs=[kspec, kspec, kspec, _const_spec((N_EXPERTS, LANES))],
        out_shape=[jax.ShapeDtypeStruct((TOP_K, n), jnp.int32), jax.ShapeDtypeStruct((TOP_K, n), jnp.int32),
                   jax.ShapeDtypeStruct((TOP_K, n), F32), jax.ShapeDtypeStruct((N_EXPERTS, LANES), jnp.int32)],
        scratch_shapes=[pltpu.VMEM((N_EXPERTS, LANES), F32)],
        compiler_params=_cparams("arbitrary"),
        name="router",
    )(lg_t, bias.reshape(N_EXPERTS, 1), tri)


def _dest_body(off_ref, eid_ref, rank_ref, dest_ref, *, n_real, last_row):
    eid = eid_ref[...]
    base = jnp.zeros(eid.shape, jnp.int32)
    for e in range(N_EXPERTS):
        base = jnp.where(eid == e, off_ref[e], base)
    tok = lax.broadcasted_iota(jnp.int32, eid.shape, 1)
    slot = lax.broadcasted_iota(jnp.int32, eid.shape, 0)
    unused = last_row - ((tok - n_real) * TOP_K + slot)
    dest_ref[...] = jnp.where(tok < n_real, base + rank_ref[...], unused)


def _dest(off, eid, rank, n_real, last_row):
    spec = pl.BlockSpec(eid.shape, lambda i, off_ref: (0, 0))
    return pl.pallas_call(
        functools.partial(_dest_body, n_real=n_real, last_row=last_row),
        grid_spec=pltpu.PrefetchScalarGridSpec(num_scalar_prefetch=1, grid=(1,), in_specs=[spec, spec],
                                               out_specs=spec),
        out_shape=jax.ShapeDtypeStruct(eid.shape, jnp.int32),
        compiler_params=_cparams("arbitrary"),
        name="dest_rows",
    )(off, eid, rank)


def _sc_mesh():
    return plsc.VectorSubcoreMesh(core_axis_name="core", subcore_axis_name="subcore")


def _sc_dispatch(hp, dest_w, p_alloc):
    n = hp.shape[0]
    w = dest_w.shape[2]

    @functools.partial(pl.kernel, out_type=jax.ShapeDtypeStruct((p_alloc, DP), jnp.int32), mesh=_sc_mesh(),
                       name="sc_dispatch")
    def run(hp_hbm, dest_hbm, xs_hbm):
        def body(x_vmem, i_vmem):
            for k in range(TOP_K):
                pltpu.sync_copy(x_vmem, xs_hbm.at[i_vmem.at[k]])

        pltpu.emit_pipeline(
            body,
            grid=(n // w,),
            in_specs=[pl.BlockSpec((w, DP), lambda i: (i, 0)),
                      pl.BlockSpec((None, TOP_K, w), lambda i: (i, 0, 0))],
            out_specs=[],
            core_axis_name=("core", "subcore"),
            dimension_semantics=(pltpu.PARALLEL,),
        )(hp_hbm, dest_hbm)

    return run(hp, dest_w)


def _sc_gather_sum(ys, dest_tm, w_lanes):
    n_win, parts, pk = dest_tm.shape
    w = parts * pk // TOP_K
    wp = w // parts
    n_vec = DP // SC_LANES

    @functools.partial(pl.kernel, out_type=jax.ShapeDtypeStruct((n_win * w, D), F32), mesh=_sc_mesh(),
                       scratch_types=[pltpu.VMEM((parts, pk, DP), jnp.int32), pltpu.SemaphoreType.DMA((parts,))],
                       compiler_params=pltpu.CompilerParams(needs_layout_passes=False), name="sc_gather_sum")
    def run(ys_hbm, dest_hbm, w_hbm, o_hbm, rows_v, sems):
        def body(i_vmem, w_vmem, o_vmem):
            copies = [pltpu.async_copy(ys_hbm.at[i_vmem.at[p]], rows_v.at[p], sems.at[p]) for p in range(parts)]
            for p in range(parts):
                copies[p].wait()

                @pl.loop(0, wp)
                def _(t):
                    tok = p * wp + t
                    wv = [w_vmem[tok, pl.ds(k * SC_LANES, SC_LANES)] for k in range(TOP_K)]

                    @plsc.parallel_loop(0, n_vec, unroll=SUM_UNROLL)
                    def _(j):
                        col = j * SC_LANES
                        lo = jnp.zeros((SC_LANES,), F32)
                        hi = jnp.zeros((SC_LANES,), F32)
                        for k in range(TOP_K):
                            word = rows_v[p, t * TOP_K + k, pl.ds(col, SC_LANES)]
                            lo = lo + wv[k] * plsc.bitcast(lax.shift_left(word, 16), F32)
                            hi = hi + wv[k] * plsc.bitcast(word & HI_HALF, F32)
                        o_vmem[tok, pl.ds(col, SC_LANES)] = lo
                        o_vmem[tok, pl.ds(DP + col, SC_LANES)] = hi

        pltpu.emit_pipeline(
            body,
            grid=(n_win,),
            in_specs=[pl.BlockSpec((None, parts, pk), lambda i: (i, 0, 0)),
                      pl.BlockSpec((w, TOP_K * SC_LANES), lambda i: (i, 0))],
            out_specs=[pl.BlockSpec((w, D), lambda i: (i, 0))],
            core_axis_name=("core", "subcore"),
            dimension_semantics=(pltpu.PARALLEL,),
        )(dest_hbm, w_hbm, o_hbm)

    return run(ys, dest_tm, w_lanes)


def _expert_body(first_ref, cnt_ref, nused_ref, xs_hbm, wg_s, wu_s, wd_s, ys_hbm, xbuf, ybuf, xsem, ysem):
    e = pl.program_id(0)
    n_used = nused_ref[0]

    def load(g):
        rows = pl.ds(pl.multiple_of(g * EXPERT_TILE, EXPERT_TILE), EXPERT_TILE)
        slot = g % EXPERT_X_SLOTS
        return pltpu.make_async_copy(xs_hbm.at[rows], xbuf.at[slot], xsem.at[slot])

    def store(g):
        rows = pl.ds(pl.multiple_of(g * EXPERT_TILE, EXPERT_TILE), EXPERT_TILE)
        slot = g % EXPERT_Y_SLOTS
        return pltpu.make_async_copy(ybuf.at[slot], ys_hbm.at[rows], ysem.at[slot])

    @pl.when(e == 0)
    def _():
        for g in range(EXPERT_AHEAD):
            @pl.when(g < n_used)
            def _():
                load(g).start()

    first = first_ref[e]
    cnt = cnt_ref[e]

    def acquire(g):
        ahead = g + EXPERT_AHEAD

        @pl.when(ahead < n_used)
        def _():
            load(ahead).start()

        load(g).wait()

        @pl.when(g >= EXPERT_Y_SLOTS)
        def _():
            store(g - EXPERT_Y_SLOTS).wait()

    def compute(g):
        lo, hi = _unpack_rows(xbuf[g % EXPERT_X_SLOTS])
        lo = lo.astype(BF16)
        hi = hi.astype(BF16)

        def xdot(w_s):
            return (jnp.dot(lo, w_s[:DP, :], preferred_element_type=F32) +
                    jnp.dot(hi, w_s[DP:, :], preferred_element_type=F32))

        a = (_silu(xdot(wg_s)) * xdot(wu_s)).astype(BF16)
        ybuf[g % EXPERT_Y_SLOTS] = _pack_rows(jnp.dot(a, wd_s[...], preferred_element_type=F32))

    def pair(j, carry):
        g = first + 2 * j
        acquire(g)
        acquire(g + 1)
        compute(g)
        compute(g + 1)
        store(g).start()
        store(g + 1).start()
        return carry

    lax.fori_loop(0, cnt // 2, pair, 0)

    @pl.when(cnt % 2 == 1)
    def _():
        g = first + cnt - 1
        acquire(g)
        compute(g)
        store(g).start()

    @pl.when(e == N_EXPERTS - 1)
    def _():
        for k in range(EXPERT_Y_SLOTS):
            g = n_used - 1 - k

            @pl.when(g >= 0)
            def _():
                store(g).wait()


def _experts(xs, tile_first, tile_count, n_used, wg, wu, wd):
    def w_map(e, first, cnt, nu):
        return (e, 0, 0)

    return pl.pallas_call(
        _expert_body,
        grid_spec=pltpu.PrefetchScalarGridSpec(
            num_scalar_prefetch=3, grid=(N_EXPERTS,),
            in_specs=[pl.BlockSpec(memory_space=pl.ANY),
                      pl.BlockSpec((None, D, EXPERT_DIM), w_map),
                      pl.BlockSpec((None, D, EXPERT_DIM), w_map),
                      pl.BlockSpec((None, EXPERT_DIM, D), w_map)],
            out_specs=pl.BlockSpec(memory_space=pl.ANY),
            scratch_shapes=[pltpu.VMEM((EXPERT_X_SLOTS, EXPERT_TILE, DP), jnp.int32),
                            pltpu.VMEM((EXPERT_Y_SLOTS, EXPERT_TILE, DP), jnp.int32),
                            pltpu.SemaphoreType.DMA((EXPERT_X_SLOTS,)),
                            pltpu.SemaphoreType.DMA((EXPERT_Y_SLOTS,))]),
        out_shape=jax.ShapeDtypeStruct(xs.shape, jnp.int32),
        compiler_params=_cparams("arbitrary"),
        name="experts",
    )(tile_first, tile_count, n_used, xs, wg, wu, wd)


def _log_sigmoid(z):
    return jnp.minimum(z, 0.0) - jnp.log(1.0 + jnp.exp(-jnp.abs(z)))


def _gla_gate(hb, wlr_ref, wgk_ref, bgk_ref):
    lr = jnp.dot(hb, wlr_ref[...], preferred_element_type=F32)
    z = _bdot(lr, wgk_ref[...]) + bgk_ref[...]
    return _log_sigmoid(z) * (1.0 / GLA_GATE_NORMALIZER)


def _split3(a):
    hi = a.astype(BF16)
    r1 = a - hi.astype(F32)
    mid = r1.astype(BF16)
    lo = (r1 - mid.astype(F32)).astype(BF16)
    return hi, mid, lo


def _gla_out(o_ref_val, go, gng):
    parts = []
    for hd in range(GLA_HEADS):
        cols = slice(hd * GLA_DV, (hd + 1) * GLA_DV)
        parts.append((_rms(o_ref_val[:, cols], gng) * _silu(go[:, cols])).astype(BF16))
    return jnp.concatenate(parts, axis=1)


def _channel_mix_residual(x, routed, h_packed, g2, ng3, swg_ref, swu_ref, swd_ref):
    h_lo, h_hi = _unpack_rows(h_packed)
    h_lo = h_lo.astype(BF16)
    h_hi = h_hi.astype(BF16)

    def hdot(w_ref_):
        return (jnp.dot(h_lo, w_ref_[:DP, :], preferred_element_type=F32) +
                jnp.dot(h_hi, w_ref_[DP:, :], preferred_element_type=F32))

    hs = (_silu(hdot(swg_ref)) * hdot(swu_ref)).astype(BF16)
    y = jnp.dot(hs, swd_ref[...], preferred_element_type=F32) + routed
    return x + g2 * _rms(y, ng3)


def _gla_body(x_ref, y_ref, hprev_ref, modprev_ref, ngprev_ref, swg_ref, swu_ref, swd_ref,
              mod_ref, ng_ref, wqkvg_ref, wlr_ref, wgk_ref, bgk_ref, tril_ref, gng_ref, wout_ref,
              rw_ref, *rest, tt, n_alias):
    x1_ref, h_ref, lg_ref, st_ref, st_scr, o_scr, qd_scr, dst_scr = rest[n_alias:]
    j = pl.program_id(1)

    @pl.when(j == 0)
    def _():
        st_scr[...] = jnp.zeros_like(st_scr)

    x = _channel_mix_residual(x_ref[...], y_ref[...], hprev_ref[...], modprev_ref[:, 5 * D:6 * D],
                              ngprev_ref[3:4, :], swg_ref, swu_ref, swd_ref)
    sh1, sc1, g1, sh2, sc2, _ = _mod_slices(mod_ref)
    ng = ng_ref[...]
    hb = (_rms(x, ng[0:1]) * (1.0 + sc1) + sh1).astype(BF16)
    q = jnp.dot(hb, wqkvg_ref[:, :GLA_DK_TOT], preferred_element_type=F32) * (GLA_DK ** -0.5)
    k = jnp.dot(hb, wqkvg_ref[:, GLA_DK_TOT:2 * GLA_DK_TOT], preferred_element_type=F32)
    v = jnp.dot(hb, wqkvg_ref[:, 2 * GLA_DK_TOT:2 * GLA_DK_TOT + GLA_DV_TOT],
                preferred_element_type=F32).astype(BF16)
    log_a = _gla_gate(hb, wlr_ref, wgk_ref, bgk_ref)
    tril = tril_ref[...]
    parts = _split3(log_a)
    b = jnp.concatenate(
        [sum(jnp.dot(tril, p[r:r + GLA_CUM_BLOCK], preferred_element_type=F32) for p in parts)
         for r in range(0, tt, GLA_CUM_BLOCK)], axis=0)
    n_chunks = tt // GLA_CHUNK
    in_chunk_causal = tril > 0
    for r in range(0, tt, GLA_CUM_BLOCK):
        blk = slice(r, r + GLA_CUM_BLOCK)
        for hd in range(GLA_HEADS):
            kc = slice(hd * GLA_DK, (hd + 1) * GLA_DK)
            vc = slice(hd * GLA_DV, (hd + 1) * GLA_DV)
            bb = b[blk, kc]
            q_dec = (q[blk, kc] * jnp.exp(bb)).astype(BF16)
            k_inv = (k[blk, kc] * jnp.exp(-bb)).astype(BF16)
            att = jnp.where(in_chunk_causal, _dot_nt(q_dec, k_inv), 0.0).astype(BF16)
            qd_scr[blk, kc] = q_dec
            o_scr[blk, vc] = jnp.dot(att, v[blk, vc], preferred_element_type=F32)
    for c in range(n_chunks):
        rows = slice(c * GLA_CHUNK, (c + 1) * GLA_CHUNK)
        last = (c + 1) * GLA_CHUNK - 1
        for hd in range(GLA_HEADS):
            kc = slice(hd * GLA_DK, (hd + 1) * GLA_DK)
            vc = slice(hd * GLA_DV, (hd + 1) * GLA_DV)
            k_end = (k[rows, kc] * jnp.exp(b[last:last + 1, kc] - b[rows, kc])).astype(BF16)
            dst_scr[c * GLA_HEADS + hd] = lax.dot_general(
                v[rows, vc], k_end, (((0,), (0,)), ((), ())), preferred_element_type=F32)
    states = [st_scr[hd] for hd in range(GLA_HEADS)]
    for c in range(n_chunks):
        rows = slice(c * GLA_CHUNK, (c + 1) * GLA_CHUNK)
        last = (c + 1) * GLA_CHUNK - 1
        for hd in range(GLA_HEADS):
            kc = slice(hd * GLA_DK, (hd + 1) * GLA_DK)
            vc = slice(hd * GLA_DV, (hd + 1) * GLA_DV)
            o_scr[rows, vc] += _dot_nt(qd_scr[rows, kc], states[hd].astype(BF16))
            states[hd] = states[hd] * jnp.exp(b[last:last + 1, kc]) + dst_scr[c * GLA_HEADS + hd]
    for hd in range(GLA_HEADS):
        st_scr[hd] = states[hd]

    @pl.when(j == pl.num_programs(1) - 1)
    def _():
        for hd in range(GLA_HEADS):
            st_ref[hd] = st_scr[hd].T

    go = jnp.dot(hb, wqkvg_ref[:, 2 * GLA_DK_TOT + GLA_DV_TOT:], preferred_element_type=F32)
    y = jnp.dot(_gla_out(o_scr[...], go, gng_ref[...]), wout_ref[...], preferred_element_type=F32)
    x1 = x + g1 * _rms(y, ng[1:2])
    x1_ref[...] = x1
    _ffn_prep(x1, ng, sh2, sc2, rw_ref, h_ref, lg_ref)


def _gla_mixer(x2d, routed, hp, mod3_prev, ng_prev, shared_prev, mod3, batch, seq, ng, wqkvg, wlr, wgk, bgk,
               gng, wout, rw_t, ffn_rows=None):
    tt = GLA_TILE
    tpb = seq // tt
    n = x2d.shape[0]
    idx = jnp.arange(GLA_CUM_BLOCK)
    tril = ((idx[:, None] >= idx[None, :]) &
            (idx[:, None] // GLA_CHUNK == idx[None, :] // GLA_CHUNK)).astype(BF16)
    row_map = lambda b, j: (b * tpb + j, 0)
    mod_map = lambda b, j: (b, 0, 0)
    consts = (ng, wqkvg, wlr, wgk, bgk, tril, gng, wout, rw_t)
    prev_consts = (ng_prev,) + tuple(shared_prev)
    ffn_shapes, alias_bufs, _ = _ffn_out(n, ffn_rows, None)
    args = [x2d, routed, hp, mod3_prev, *prev_consts, mod3, *consts]
    return pl.pallas_call(
        functools.partial(_gla_body, tt=tt, n_alias=len(alias_bufs)),
        grid=(batch, tpb),
        in_specs=[pl.BlockSpec((tt, D), row_map), pl.BlockSpec((tt, D), row_map),
                  pl.BlockSpec((tt, DP), row_map), pl.BlockSpec((None, 1, 6 * D), mod_map)] +
                 [_const_spec(a.shape) for a in prev_consts] +
                 [pl.BlockSpec((None, 1, 6 * D), mod_map)] +
                 [_const_spec(a.shape) for a in consts] +
                 [pl.BlockSpec(memory_space=pl.ANY)] * len(alias_bufs),
        out_specs=[pl.BlockSpec((tt, D), row_map), pl.BlockSpec((tt, DP), row_map),
                   pl.BlockSpec((N_EXPERTS, tt), lambda b, j: (0, b * tpb + j)),
                   pl.BlockSpec((None, GLA_HEADS, GLA_DK, GLA_DV), lambda b, j: (b, 0, 0, 0))],
        out_shape=[jax.ShapeDtypeStruct((n, D), F32)] + ffn_shapes +
                  [jax.ShapeDtypeStruct((batch, GLA_HEADS, GLA_DK, GLA_DV), F32)],
        input_output_aliases={len(args) + i: 1 + i for i in range(len(alias_bufs))},
        scratch_shapes=[pltpu.VMEM((GLA_HEADS, GLA_DV, GLA_DK), F32),
                        pltpu.VMEM((tt, GLA_DV_TOT), F32),
                        pltpu.VMEM((tt, GLA_DK_TOT), BF16),
                        pltpu.VMEM((tt // GLA_CHUNK * GLA_HEADS, GLA_DV, GLA_DK), F32)],
        compiler_params=_cparams("parallel", "arbitrary"),
        name="gla_mixer",
    )(*args, *alias_bufs)


def _gla1_proj_body(x_ref, mod_ref, ng_ref, wqkvg_ref, wlr_ref, wgk_ref, bgk_ref,
                    q_ref, k_ref, v_ref, go_ref, dec_ref):
    sh1, sc1, _, _, _, _ = _mod_slices(mod_ref)
    ng = ng_ref[...]
    hb = (_rms(x_ref[...], ng[0:1]) * (1.0 + sc1) + sh1).astype(BF16)
    proj = jnp.dot(hb, wqkvg_ref[...], preferred_element_type=F32)
    q_ref[...] = proj[:, :GLA_DK_TOT] * (GLA_DK ** -0.5)
    k_ref[...] = proj[:, GLA_DK_TOT:2 * GLA_DK_TOT]
    v_ref[...] = proj[:, 2 * GLA_DK_TOT:2 * GLA_DK_TOT + GLA_DV_TOT]
    go_ref[...] = proj[:, 2 * GLA_DK_TOT + GLA_DV_TOT:]
    dec_ref[...] = jnp.exp(_gla_gate(hb, wlr_ref, wgk_ref, bgk_ref))


GLA1_TOK = 16


def _gla1_state_body(st_ref, qc_ref, kc_ref, dc_ref, v_ref, nst_ref, o_ref):
    v = v_ref[...]
    for i in range(GLA1_TOK):
        for hd in range(GLA_HEADS):
            vrow = v[i:i + 1, hd * GLA_DV:(hd + 1) * GLA_DV]
            s_new = dc_ref[hd][:, i:i + 1] * st_ref[i, hd] + kc_ref[hd][:, i:i + 1] * vrow
            nst_ref[i, hd] = s_new
            o_ref[i:i + 1, hd * GLA_DV:(hd + 1) * GLA_DV] = jnp.sum(
                qc_ref[hd][:, i:i + 1] * s_new, axis=0, keepdims=True)


def _gla1_out_body(x_ref, o_ref, go_ref, mod_ref, ng_ref, gng_ref, wout_ref, rw_ref, *rest):
    x1_ref, h_ref, lg_ref = rest[-3:]
    _, _, g1, sh2, sc2, _ = _mod_slices(mod_ref)
    ng = ng_ref[...]
    y = jnp.dot(_gla_out(o_ref[...], go_ref[...], gng_ref[...]), wout_ref[...], preferred_element_type=F32)
    x1 = x_ref[...] + g1 * _rms(y, ng[1:2])
    x1_ref[...] = x1
    _ffn_prep(x1, ng, sh2, sc2, rw_ref, h_ref, lg_ref)


def _gla_mixer_one(x2d, mod2, state, ng, wqkvg, wlr, wgk, bgk, gng, wout, rw_t, into=None):
    n = x2d.shape[0]
    consts = (ng, wqkvg, wlr, wgk, bgk)
    q, k, v, go, dec = pl.pallas_call(
        _gla1_proj_body,
        in_specs=[_const_spec(a.shape) for a in (x2d, mod2) + consts],
        out_specs=[_const_spec((n, GLA_DK_TOT)), _const_spec((n, GLA_DK_TOT)), _const_spec((n, GLA_DV_TOT)),
                   _const_spec((n, GLA_DV_TOT)), _const_spec((n, GLA_DK_TOT))],
        out_shape=[jax.ShapeDtypeStruct((n, GLA_DK_TOT), F32), jax.ShapeDtypeStruct((n, GLA_DK_TOT), F32),
                   jax.ShapeDtypeStruct((n, GLA_DV_TOT), F32), jax.ShapeDtypeStruct((n, GLA_DV_TOT), F32),
                   jax.ShapeDtypeStruct((n, GLA_DK_TOT), F32)],
        grid=(1,),
        compiler_params=_cparams("arbitrary"),
        name="gla1_proj",
    )(x2d, mod2, *consts)

    def cols(a):
        return a.reshape(n // GLA1_TOK, GLA1_TOK, GLA_HEADS, GLA_DK).transpose(0, 2, 3, 1)

    col_spec = pl.BlockSpec((None, GLA_HEADS, GLA_DK, GLA1_TOK), lambda i: (i, 0, 0, 0))
    st_spec = pl.BlockSpec((GLA1_TOK, GLA_HEADS, GLA_DK, GLA_DV), lambda i: (i, 0, 0, 0))
    new_state, o = pl.pallas_call(
        _gla1_state_body,
        grid=(n // GLA1_TOK,),
        in_specs=[st_spec, col_spec, col_spec, col_spec, pl.BlockSpec((GLA1_TOK, GLA_DV_TOT), lambda i: (i, 0))],
        out_specs=[st_spec, pl.BlockSpec((GLA1_TOK, GLA_DV_TOT), lambda i: (i, 0))],
        out_shape=[jax.ShapeDtypeStruct(state.shape, F32), jax.ShapeDtypeStruct((n, GLA_DV_TOT), F32)],
        compiler_params=_cparams("parallel"),
        name="gla1_state",
    )(state, cols(q), cols(k), cols(dec), v)

    consts = (mod2, ng, gng, wout, rw_t)
    ffn_shapes, alias_bufs, oblk = _ffn_out(n, None, into)
    n_in = 3 + len(consts)
    x1, h, lg = pl.pallas_call(
        _gla1_out_body,
        grid=(1,),
        in_specs=[_const_spec(a.shape) for a in (x2d, o, go) + consts] +
                 [pl.BlockSpec(memory_space=pl.ANY)] * len(alias_bufs),
        out_specs=[_const_spec((n, D)), pl.BlockSpec((n, DP), lambda i: (oblk, 0)),
                   pl.BlockSpec((N_EXPERTS, n), lambda i: (0, oblk))],
        out_shape=[jax.ShapeDtypeStruct((n, D), F32)] + ffn_shapes,
        input_output_aliases={n_in + i: 1 + i for i in range(len(alias_bufs))},
        compiler_params=_cparams("arbitrary"),
        name="gla1_out",
    )(x2d, o, go, *consts, *alias_bufs)
    return x1, h, lg, new_state


def _moe_routed(h, lg, n, router_bias, wg, wu, wd):
    n_pad = h.shape[0]
    eid, rank, wts, counts = _router(lg, router_bias, n)
    tile_count = ((counts[:, 0] + EXPERT_TILE - 1) // EXPERT_TILE).astype(jnp.int32)
    tile_end = jnp.cumsum(tile_count).astype(jnp.int32)
    tile_first = tile_end - tile_count
    off = tile_first * EXPERT_TILE
    p_alloc = TOP_K * n_pad + N_EXPERTS * EXPERT_TILE
    dest = _dest(off, eid, rank, n, p_alloc - 1)
    dest_w = dest.reshape(TOP_K, n_pad // DISPATCH_W, DISPATCH_W).transpose(1, 0, 2)
    xs = _sc_dispatch(h, dest_w, p_alloc)
    ys = _experts(xs, tile_first, tile_count, tile_end[-1:], wg, wu, wd)
    dest_tm = dest.T.reshape(n_pad // SUM_W, SUM_PARTS, SUM_W * TOP_K // SUM_PARTS)
    w_lanes = jnp.repeat(wts.T, SC_LANES, axis=1)
    return _sc_gather_sum(ys, dest_tm, w_lanes)


def kernel(x_prompt, x_sample, state_gla, c_prompt, c_sample, norm_g, ada_w, ada_b, gm_w_in, gm_b_in,
           gm_ln_g, gm_ln_b, gm_w_s, gm_b_s, gm_w_out, gla_w_in, gla_w_gk, gla_b_gk, gla_norm_g,
           gla_w_out, router_w, router_bias, exp_w_gate, exp_w_up, exp_w_down, sh_w_gate, sh_w_up,
           sh_w_down):
    batch, seq, _ = x_prompt.shape
    n_s = x_sample.shape[0]
    n_p = batch * seq
    tpb = seq // MIX_TILE
    xp = x_prompt.reshape(n_p, D)
    xs = x_sample.reshape(n_s, D)

    mod = _ada(jnp.concatenate([c_prompt, c_sample], axis=0), ada_w, ada_b)
    mod_p = [mod[i, :batch].reshape(batch, 1, 6 * D) for i in range(2)]
    mod_s = [mod[i, batch:] for i in range(2)]
    rw_t = [jnp.concatenate(_split3(router_w[i].T), axis=0) for i in range(2)]

    ws_causal = jnp.tril(gm_w_s[0]).astype(BF16)
    bs_cols = gm_b_s[0].T
    eye = jnp.eye(GM_CHUNK, dtype=F32)
    ws_first = (gm_w_s[0][:, 0, 0][:, None, None] * eye).astype(BF16)
    bs_first = jnp.broadcast_to(gm_b_s[0][:, 0][None, :], (GM_CHUNK, GM_GROUPS))
    gm_args = (norm_g[0], gm_w_in[0].astype(BF16), gm_b_in[0].reshape(1, -1), gm_ln_g[0].reshape(1, -1),
               gm_ln_b[0].reshape(1, -1))
    wout0 = gm_w_out[0].astype(BF16)
    shared = [(sh_w_gate[i].astype(BF16), sh_w_up[i].astype(BF16), sh_w_down[i].astype(BF16))
              for i in range(2)]
    n_qkvg = 2 * GLA_DK_TOT + 2 * GLA_DV_TOT
    wqkvg = gla_w_in[0][:, :n_qkvg].astype(BF16)
    wlr = jnp.pad(gla_w_in[0][:, n_qkvg:], ((0, 0), (0, LANES - GLA_GATE_RANK))).astype(BF16)
    wgk = jnp.pad(gla_w_gk[0], ((0, LANES - GLA_GATE_RANK), (0, 0))).astype(BF16)
    gla_args = (norm_g[1], wqkvg, wlr, wgk, gla_b_gk[0].reshape(1, -1), gla_norm_g[0].reshape(1, -1),
                gla_w_out[0].astype(BF16), rw_t[1])
    experts_f32 = (exp_w_gate, exp_w_up, exp_w_down)

    half = batch // 2
    streams = [(0, half, False), (half, batch - half, True)]
    st = [dict() for _ in streams]

    experts = []
    for layer, (s, (b0, nb, with_new)) in enumerate(zip(st, streams)):
        s["mod_p"] = [mod_p[i][b0:b0 + nb] for i in range(2)]
        s["n"] = nb * seq
        s["n_all"] = s["n"] + (n_s if with_new else 0)
        s["n_pad"] = -(-s["n_all"] // TOKEN_PAD) * TOKEN_PAD
        s["x1p"], s["h"], s["lg"], *w16 = _gmlp_mixer(
            xp, b0 * tpb, s["n"], s["mod_p"][0], False, MIX_TILE, tpb, *gm_args, ws_causal, bs_cols, wout0,
            rw_t[0], emit_v=False, cast_w=(layer, *experts_f32), ffn_rows=s["n_pad"])
        experts.append(w16)
        if with_new:
            s["x1s"], s["h"], s["lg"], v_rows = _gmlp_mixer(
                xs, 0, n_s, mod_s[0], True, n_s, 1, *gm_args, ws_first, bs_first, wout0, rw_t[0], emit_v=True,
                into=(s["h"], s["lg"], s["n"] // n_s))
    for s, (b0, nb, with_new) in zip(st, streams):
        s["routed0"] = _moe_routed(s["h"], s["lg"], s["n_all"], router_bias[0], *experts[0])
        if with_new:
            s["x2s"] = _combine(s["x1s"], s["routed0"], s["h"], s["n"] // n_s, mod_s[0], True, n_s, 1,
                                norm_g[0], *shared[0])
    for s, (b0, nb, with_new) in zip(st, streams):
        s["x3p"], h1, lg1, s["st_p"] = _gla_mixer(s["x1p"], s["routed0"], s["h"], s["mod_p"][0], norm_g[0],
                                                  shared[0], s["mod_p"][1], nb, seq, *gla_args,
                                                  ffn_rows=s["n_pad"])
        if with_new:
            s["x3s"], h1, lg1, st_s = _gla_mixer_one(s["x2s"], mod_s[1], state_gla[:, 0], *gla_args,
                                                     into=(h1, lg1, s["n"] // n_s))
        s["h"], s["lg"] = h1, lg1
    y_prompt = None
    for s, (b0, nb, with_new) in zip(st, streams):
        routed = _moe_routed(s["h"], s["lg"], s["n_all"], router_bias[1], *experts[1])
        ctpb = seq // COMBINE_TILE
        y_prompt = _combine(s["x3p"], routed, s["h"], 0, s["mod_p"][1], False, COMBINE_TILE, ctpb, norm_g[1],
                            *shared[1], out_rows=n_p, out_blk0=b0 * ctpb, out_buf=y_prompt)
        if with_new:
            y_new = _combine(s["x3s"], routed, s["h"], s["n"] // n_s, mod_s[1], True, n_s, 1, norm_g[1],
                             *shared[1])
    st_p = jnp.concatenate([s["st_p"] for s in st], axis=0)

    return (y_prompt.reshape(batch, seq, D), y_new.reshape(n_s, 1, D), st_p[:, None], st_s[:, None],
            v_rows.reshape(n_s, 1, 1, GM_HALF))
```
